```python
import math
import jax, jax.numpy as jnp
from jax import lax
import numpy as np


D_MODEL = 1024
BATCH = 16
SEQ = 4096
DEPTH = 1

MIX_WIDTH = D_MODEL
SSD_WIDTH = MIX_WIDTH // 2
SSD_HEAD_DIM = 64
SSD_HEADS = SSD_WIDTH // SSD_HEAD_DIM
SSD_GROUPS = 2
SSD_STATE = 128
SSD_CONV = 4
SSD_CHUNK = 128
SSD_BC = SSD_GROUPS * SSD_STATE
CONV_CH = SSD_WIDTH + 2 * SSD_BC
S5_WIDTH = MIX_WIDTH - SSD_WIDTH
S5_GROUP_CH = 16
S5_GROUPS = S5_WIDTH // S5_GROUP_CH
S5_STATE = 64
IN_COLS = SSD_WIDTH + CONV_CH + SSD_HEADS + S5_WIDTH
D_FF = ((8 * D_MODEL // 3 + 255) // 256) * 256
N_MOD = 9
ALPHA = (2 * DEPTH) ** 0.25
BETA = (8 * DEPTH) ** -0.25
LN_EPS = 1e-5

kernel_name = 'hymba_ssd_s5_macaron_deepnorm'


def layer_norm(x, g, b):
    xf = x.astype(jnp.float32)
    mu = jnp.mean(xf, axis=-1, keepdims=True)
    var = jnp.mean(jnp.square(xf - mu), axis=-1, keepdims=True)
    return ((xf - mu) * lax.rsqrt(var + LN_EPS) * g + b).astype(x.dtype)


def modulate(x, shift, scale):
    return x * (1 + scale[:, None, :]) + shift[:, None, :]


def swiglu(u, w1, w3, w2):
    return (jax.nn.silu(u @ w1) * (u @ w3)) @ w2


def causal_dwconv(x, w, b):
    k = w.shape[0]
    y = lax.conv_general_dilated(x, w[:, None, :], window_strides=(1,), padding=[(k - 1, 0)],
                                 dimension_numbers=('NWC', 'WIO', 'NWC'),
                                 feature_group_count=x.shape[-1])
    return y + b


def ssd_chunked(xs, dt, a, bm, cm):
    bsz, s_len, n_h, p = xs.shape
    n_g, n_s = bm.shape[-2:]
    n_z = n_h // n_g
    l = SSD_CHUNK
    nc = s_len // l
    x = (xs * dt[..., None]).reshape(bsz, nc, l, n_g, n_z, p)
    a_dt = (dt * a).reshape(bsz, nc, l, n_g, n_z).transpose(0, 3, 4, 1, 2)
    bm = bm.reshape(bsz, nc, l, n_g, n_s)
    cm = cm.reshape(bsz, nc, l, n_g, n_s)
    a_cs = jnp.cumsum(a_dt, axis=-1)
    causal = jnp.tril(jnp.ones((l, l), dtype=bool))
    seg = a_cs[..., :, None] - a_cs[..., None, :]
    lmat = jnp.exp(jnp.where(causal, seg, -jnp.inf))
    cb = jnp.einsum('bclgn,bcsgn->bcgls', cm, bm)
    y_diag = jnp.einsum('bcgls,bgzcls,bcsgzp->bclgzp', cb, lmat, x)
    decay = jnp.exp(a_cs[..., -1:] - a_cs)
    states = jnp.einsum('bclgn,bgzcl,bclgzp->bcgzpn', bm, decay, x)
    chunk_decay = jnp.exp(a_cs[..., -1])

    def step(h, inp):
        s_c, d_c = inp
        return d_c[..., None, None] * h + s_c, h

    h0 = jnp.zeros((bsz, n_g, n_z, p, n_s), dtype=states.dtype)
    _, prev = lax.scan(step, h0, (jnp.moveaxis(states, 1, 0), jnp.moveaxis(chunk_decay, 3, 0)))
    prev = jnp.moveaxis(prev, 0, 1)
    y_off = jnp.einsum('bclgn,bcgzpn,bgzcl->bclgzp', cm, prev, jnp.exp(a_cs))
    return (y_diag + y_off).reshape(bsz, s_len, n_h, p)


def s5_mixer(u, a_re, a_im, log_dt, b_re, b_im, c_re, c_im, d, w_glu, b_glu):
    f32 = jnp.float32
    bsz, s_len, _ = u.shape
    uf = u.astype(f32).reshape(bsz, s_len, S5_GROUPS, S5_GROUP_CH)
    ar, ai = a_re.astype(f32), a_im.astype(f32)
    dt = jnp.exp(log_dt.astype(f32))[:, None]
    mag = jnp.exp(dt * ar)
    ab_re, ab_im = mag * jnp.cos(dt * ai), mag * jnp.sin(dt * ai)
    den = ar * ar + ai * ai
    nr, ni = ab_re - 1.0, ab_im
    f_re, f_im = (nr * ar + ni * ai) / den, (ni * ar - nr * ai) / den
    br, bi = b_re.astype(f32), b_im.astype(f32)
    bb_re = f_re[..., None] * br - f_im[..., None] * bi
    bb_im = f_re[..., None] * bi + f_im[..., None] * br
    bu_re = jnp.einsum('bsgh,gph->bsgp', uf, bb_re)
    bu_im = jnp.einsum('bsgh,gph->bsgp', uf, bb_im)
    a_seq_re = jnp.broadcast_to(ab_re, (1, s_len, S5_GROUPS, S5_STATE))
    a_seq_im = jnp.broadcast_to(ab_im, (1, s_len, S5_GROUPS, S5_STATE))

    def combine(e1, e2):
        a1r, a1i, b1r, b1i = e1
        a2r, a2i, b2r, b2i = e2
        return (a2r * a1r - a2i * a1i, a2r * a1i + a2i * a1r,
                a2r * b1r - a2i * b1i + b2r, a2r * b1i + a2i * b1r + b2i)

    _, _, xr, xi = lax.associative_scan(combine, (a_seq_re, a_seq_im, bu_re, bu_im), axis=1)
    y = (jnp.einsum('bsgp,ghp->bsgh', xr, c_re.astype(f32))
         - jnp.einsum('bsgp,ghp->bsgh', xi, c_im.astype(f32))
         + uf * d.astype(f32).reshape(S5_GROUPS, S5_GROUP_CH))
    y = y.reshape(bsz, s_len, S5_WIDTH)
    g = jax.nn.gelu(y)
    out = g * jax.nn.sigmoid(g @ w_glu.astype(f32) + b_glu.astype(f32))
    return out.astype(u.dtype)


def hybrid_mixer(h, w_in, conv_w, conv_b, dt_bias, a_log, d_ssd, ssd_norm_w,
                 s5_a_re, s5_a_im, s5_log_dt, s5_b_re, s5_b_im, s5_c_re, s5_c_im, s5_d,
                 w_glu, b_glu, w_out):
    f32 = jnp.float32
    bsz, s_len, _ = h.shape
    proj = h @ w_in
    z, xbc, dt_raw, u = jnp.split(proj, [SSD_WIDTH, SSD_WIDTH + CONV_CH,
                                         SSD_WIDTH + CONV_CH + SSD_HEADS], axis=-1)
    xbc = jax.nn.silu(causal_dwconv(xbc, conv_w, conv_b))
    xs, bm, cm = jnp.split(xbc.astype(f32), [SSD_WIDTH, SSD_WIDTH + SSD_BC], axis=-1)
    dt = jax.nn.softplus(dt_raw.astype(f32) + dt_bias.astype(f32))
    a = -jnp.exp(a_log.astype(f32))
    xs = xs.reshape(bsz, s_len, SSD_HEADS, SSD_HEAD_DIM)
    y = ssd_chunked(xs, dt, a,
                    bm.reshape(bsz, s_len, SSD_GROUPS, SSD_STATE),
                    cm.reshape(bsz, s_len, SSD_GROUPS, SSD_STATE))
    y = y + d_ssd.astype(f32)[:, None] * xs
    y = y.reshape(bsz, s_len, SSD_WIDTH) * jax.nn.silu(z.astype(f32))
    yg = y.reshape(bsz, s_len, SSD_GROUPS, SSD_WIDTH // SSD_GROUPS)
    yg = yg * lax.rsqrt(jnp.mean(jnp.square(yg), axis=-1, keepdims=True) + LN_EPS)
    y_ssd = (yg.reshape(bsz, s_len, SSD_WIDTH) * ssd_norm_w.astype(f32)).astype(h.dtype)
    y_s5 = s5_mixer(u, s5_a_re, s5_a_im, s5_log_dt, s5_b_re, s5_b_im, s5_c_re, s5_c_im,
                    s5_d, w_glu, b_glu)
    return jnp.concatenate([y_ssd, y_s5], axis=-1) @ w_out


def _fwd_setup_inputs(seed: int = 0) -> dict:
    key = jax.random.key(seed)
    ks = iter(jax.random.split(key, 48))
    f32 = jnp.float32
    nl = DEPTH

    def nrm(shape, std):
        return std * jax.random.normal(next(ks), shape, f32)

    def unif(shape, lo, hi):
        return jax.random.uniform(next(ks), shape, f32, minval=lo, maxval=hi)

    x = nrm((BATCH, SEQ, D_MODEL), 1.0)
    c = nrm((BATCH, D_MODEL), 1.0)
    w_ada = nrm((nl, D_MODEL, N_MOD * D_MODEL), 0.5 * D_MODEL ** -0.5)
    b_ada = nrm((nl, N_MOD * D_MODEL), 0.02)
    ffn1_w1 = nrm((nl, D_MODEL, D_FF), D_MODEL ** -0.5)
    ffn1_w3 = nrm((nl, D_MODEL, D_FF), D_MODEL ** -0.5)
    ffn1_w2 = nrm((nl, D_FF, D_MODEL), BETA * D_FF ** -0.5)
    ln1_g = 1.0 + nrm((nl, D_MODEL), 0.02)
    ln1_b = nrm((nl, D_MODEL), 0.02)
    w_in = nrm((nl, D_MODEL, IN_COLS), D_MODEL ** -0.5)
    conv_w = nrm((nl, SSD_CONV, CONV_CH), SSD_CONV ** -0.5)
    conv_b = nrm((nl, CONV_CH), 0.01)
    dt0 = jnp.exp(unif((nl, SSD_HEADS), math.log(1e-3), math.log(1e-1)))
    dt_bias = dt0 + jnp.log(-jnp.expm1(-dt0))
    a_log = jnp.log(unif((nl, SSD_HEADS), 1.0, 16.0))
    d_ssd = 1.0 + nrm((nl, SSD_HEADS), 0.1)
    ssd_norm_w = 1.0 + nrm((nl, SSD_WIDTH), 0.02)
    s5_a_re = -0.5 + nrm((nl, S5_GROUPS, S5_STATE), 0.01)
    s5_a_im = math.pi * jnp.arange(S5_STATE, dtype=f32)[None, None, :] + nrm((nl, S5_GROUPS, S5_STATE), 0.01)
    s5_log_dt = unif((nl, S5_GROUPS), math.log(1e-3), math.log(1e-1))
    s5_b_re = nrm((nl, S5_GROUPS, S5_STATE, S5_GROUP_CH), (2 * S5_GROUP_CH) ** -0.5)
    s5_b_im = nrm((nl, S5_GROUPS, S5_STATE, S5_GROUP_CH), (2 * S5_GROUP_CH) ** -0.5)
    s5_c_re = nrm((nl, S5_GROUPS, S5_GROUP_CH, S5_STATE), S5_STATE ** -0.5)
    s5_c_im = nrm((nl, S5_GROUPS, S5_GROUP_CH, S5_STATE), S5_STATE ** -0.5)
    s5_d = nrm((nl, S5_WIDTH), 1.0)
    w_glu = nrm((nl, S5_WIDTH, S5_WIDTH), S5_WIDTH ** -0.5)
    b_glu = nrm((nl, S5_WIDTH), 0.01)
    w_out = nrm((nl, MIX_WIDTH, D_MODEL), BETA * MIX_WIDTH ** -0.5)
    ln2_g = 1.0 + nrm((nl, D_MODEL), 0.02)
    ln2_b = nrm((nl, D_MODEL), 0.02)
    ffn2_w1 = nrm((nl, D_MODEL, D_FF), D_MODEL ** -0.5)
    ffn2_w3 = nrm((nl, D_MODEL, D_FF), D_MODEL ** -0.5)
    ffn2_w2 = nrm((nl, D_FF, D_MODEL), BETA * D_FF ** -0.5)
    ln3_g = 1.0 + nrm((nl, D_MODEL), 0.02)
    ln3_b = nrm((nl, D_MODEL), 0.02)
    return {'x': x, 'c': c, 'w_ada': w_ada, 'b_ada': b_ada,
            'ffn1_w1': ffn1_w1, 'ffn1_w3': ffn1_w3, 'ffn1_w2': ffn1_w2, 'ln1_g': ln1_g, 'ln1_b': ln1_b,
            'w_in': w_in, 'conv_w': conv_w, 'conv_b': conv_b, 'dt_bias': dt_bias, 'a_log': a_log,
            'd_ssd': d_ssd, 'ssd_norm_w': ssd_norm_w, 's5_a_re': s5_a_re, 's5_a_im': s5_a_im,
            's5_log_dt': s5_log_dt, 's5_b_re': s5_b_re, 's5_b_im': s5_b_im, 's5_c_re': s5_c_re,
            's5_c_im': s5_c_im, 's5_d': s5_d, 'w_glu': w_glu, 'b_glu': b_glu, 'w_out': w_out,
            'ln2_g': ln2_g, 'ln2_b': ln2_b, 'ffn2_w1': ffn2_w1, 'ffn2_w3': ffn2_w3, 'ffn2_w2': ffn2_w2,
            'ln3_g': ln3_g, 'ln3_b': ln3_b}


def _fwd_reference(x, c, w_ada, b_ada, ffn1_w1, ffn1_w3, ffn1_w2, ln1_g, ln1_b,
              w_in, conv_w, conv_b, dt_bias, a_log, d_ssd, ssd_norm_w, s5_a_re, s5_a_im,
              s5_log_dt, s5_b_re, s5_b_im, s5_c_re, s5_c_im, s5_d, w_glu, b_glu, w_out,
              ln2_g, ln2_b, ffn2_w1, ffn2_w3, ffn2_w2, ln3_g, ln3_b):
    bsz = x.shape[0]
    cs = jax.nn.silu(c)
    for l in range(DEPTH):
        mod = (cs @ w_ada[l] + b_ada[l]).reshape(bsz, N_MOD, D_MODEL)
        sh1, sc1, g1 = mod[:, 0], mod[:, 1], mod[:, 2]
        sh2, sc2, g2 = mod[:, 3], mod[:, 4], mod[:, 5]
        sh3, sc3, g3 = mod[:, 6], mod[:, 7], mod[:, 8]
        h = modulate(x, sh1, sc1)
        x = layer_norm(ALPHA * x + 0.5 * g1[:, None, :] * swiglu(h, ffn1_w1[l], ffn1_w3[l], ffn1_w2[l]),
                       ln1_g[l], ln1_b[l])
        h = modulate(x, sh2, sc2)
        m = hybrid_mixer(h, w_in[l], conv_w[l], conv_b[l], dt_bias[l], a_log[l], d_ssd[l],
                         ssd_norm_w[l], s5_a_re[l], s5_a_im[l], s5_log_dt[l], s5_b_re[l],
                         s5_b_im[l], s5_c_re[l], s5_c_im[l], s5_d[l], w_glu[l], b_glu[l], w_out[l])
        x = layer_norm(ALPHA * x + g2[:, None, :] * m, ln2_g[l], ln2_b[l])
        h = modulate(x, sh3, sc3)
        x = layer_norm(ALPHA * x + 0.5 * g3[:, None, :] * swiglu(h, ffn2_w1[l], ffn2_w3[l], ffn2_w2[l]),
                       ln3_g[l], ln3_b[l])
    return x


import jax as _jax
import jax.numpy as _jnp

TWIN_FORMAT = 'train_step'
FWD_PARAMS = ['x', 'c', 'w_ada', 'b_ada', 'ffn1_w1', 'ffn1_w3', 'ffn1_w2', 'ln1_g', 'ln1_b', 'w_in', 'conv_w', 'conv_b', 'dt_bias', 'a_log', 'd_ssd', 'ssd_norm_w', 's5_a_re', 's5_a_im', 's5_log_dt', 's5_b_re', 's5_b_im', 's5_c_re', 's5_c_im', 's5_d', 'w_glu', 'b_glu', 'w_out', 'ln2_g', 'ln2_b', 'ffn2_w1', 'ffn2_w3', 'ffn2_w2', 'ln3_g', 'ln3_b']
TWIN_WEIGHTS = ['w_ada', 'b_ada', 'ffn1_w1', 'ffn1_w3', 'ffn1_w2', 'ln1_g', 'ln1_b', 'w_in', 'conv_w', 'conv_b', 'dt_bias', 'a_log', 'd_ssd', 'ssd_norm_w', 's5_a_re', 's5_a_im', 's5_log_dt', 's5_b_re', 's5_b_im', 's5_c_re', 's5_c_im', 's5_d', 'w_glu', 'b_glu', 'w_out', 'ln2_g', 'ln2_b', 'ffn2_w1', 'ffn2_w3', 'ffn2_w2', 'ln3_g', 'ln3_b']
TWIN_DIFF_INPUT = 'x'
TWIN_INPUTS = ['x', 'c', 'w_ada', 'b_ada', 'ffn1_w1', 'ffn1_w3', 'ffn1_w2', 'ln1_g', 'ln1_b', 'w_in', 'conv_w', 'conv_b', 'dt_bias', 'a_log', 'd_ssd', 'ssd_norm_w', 's5_a_re', 's5_a_im', 's5_log_dt', 's5_b_re', 's5_b_im', 's5_c_re', 's5_c_im', 's5_d', 'w_glu', 'b_glu', 'w_out', 'ln2_g', 'ln2_b', 'ffn2_w1', 'ffn2_w3', 'ffn2_w2', 'ln3_g', 'ln3_b', 'loss_target', 'm_w_ada', 'm_b_ada', 'm_ffn1_w1', 'm_ffn1_w3', 'm_ffn1_w2', 'm_ln1_g', 'm_ln1_b', 'm_w_in', 'm_conv_w', 'm_conv_b', 'm_dt_bias', 'm_a_log', 'm_d_ssd', 'm_ssd_norm_w', 'm_s5_a_re', 'm_s5_a_im', 'm_s5_log_dt', 'm_s5_b_re', 'm_s5_b_im', 'm_s5_c_re', 'm_s5_c_im', 'm_s5_d', 'm_w_glu', 'm_b_glu', 'm_w_out', 'm_ln2_g', 'm_ln2_b', 'm_ffn2_w1', 'm_ffn2_w3', 'm_ffn2_w2', 'm_ln3_g', 'm_ln3_b', 'v_w_ada', 'v_b_ada', 'v_ffn1_w1', 'v_ffn1_w3', 'v_ffn1_w2', 'v_ln1_g', 'v_ln1_b', 'v_w_in', 'v_conv_w', 'v_conv_b', 'v_dt_bias', 'v_a_log', 'v_d_ssd', 'v_ssd_norm_w', 'v_s5_a_re', 'v_s5_a_im', 'v_s5_log_dt', 'v_s5_b_re', 'v_s5_b_im', 'v_s5_c_re', 'v_s5_c_im', 'v_s5_d', 'v_w_glu', 'v_b_glu', 'v_w_out', 'v_ln2_g', 'v_ln2_b', 'v_ffn2_w1', 'v_ffn2_w3', 'v_ffn2_w2', 'v_ln3_g', 'v_ln3_b']
TWIN_OUTPUTS = ['loss', 'grad_x', 'grad_w_ada', 'grad_b_ada', 'grad_ffn1_w1', 'grad_ffn1_w3', 'grad_ffn1_w2', 'grad_ln1_g', 'grad_ln1_b', 'grad_w_in', 'grad_conv_w', 'grad_conv_b', 'grad_dt_bias', 'grad_a_log', 'grad_d_ssd', 'grad_ssd_norm_w', 'grad_s5_a_re', 'grad_s5_a_im', 'grad_s5_log_dt', 'grad_s5_b_re', 'grad_s5_b_im', 'grad_s5_c_re', 'grad_s5_c_im', 'grad_s5_d', 'grad_w_glu', 'grad_b_glu', 'grad_w_out', 'grad_ln2_g', 'grad_ln2_b', 'grad_ffn2_w1', 'grad_ffn2_w3', 'grad_ffn2_w2', 'grad_ln3_g', 'grad_ln3_b', 'delta_w_ada', 'delta_b_ada', 'delta_ffn1_w1', 'delta_ffn1_w3', 'delta_ffn1_w2', 'delta_ln1_g', 'delta_ln1_b', 'delta_w_in', 'delta_conv_w', 'delta_conv_b', 'delta_dt_bias', 'delta_a_log', 'delta_d_ssd', 'delta_ssd_norm_w', 'delta_s5_a_re', 'delta_s5_a_im', 'delta_s5_log_dt', 'delta_s5_b_re', 'delta_s5_b_im', 'delta_s5_c_re', 'delta_s5_c_im', 'delta_s5_d', 'delta_w_glu', 'delta_b_glu', 'delta_w_out', 'delta_ln2_g', 'delta_ln2_b', 'delta_ffn2_w1', 'delta_ffn2_w3', 'delta_ffn2_w2', 'delta_ln3_g', 'delta_ln3_b', 'new_m_w_ada', 'new_m_b_ada', 'new_m_ffn1_w1', 'new_m_ffn1_w3', 'new_m_ffn1_w2', 'new_m_ln1_g', 'new_m_ln1_b', 'new_m_w_in', 'new_m_conv_w', 'new_m_conv_b', 'new_m_dt_bias', 'new_m_a_log', 'new_m_d_ssd', 'new_m_ssd_norm_w', 'new_m_s5_a_re', 'new_m_s5_a_im', 'new_m_s5_log_dt', 'new_m_s5_b_re', 'new_m_s5_b_im', 'new_m_s5_c_re', 'new_m_s5_c_im', 'new_m_s5_d', 'new_m_w_glu', 'new_m_b_glu', 'new_m_w_out', 'new_m_ln2_g', 'new_m_ln2_b', 'new_m_ffn2_w1', 'new_m_ffn2_w3', 'new_m_ffn2_w2', 'new_m_ln3_g', 'new_m_ln3_b', 'new_v_w_ada', 'new_v_b_ada', 'new_v_ffn1_w1', 'new_v_ffn1_w3', 'new_v_ffn1_w2', 'new_v_ln1_g', 'new_v_ln1_b', 'new_v_w_in', 'new_v_conv_w', 'new_v_conv_b', 'new_v_dt_bias', 'new_v_a_log', 'new_v_d_ssd', 'new_v_ssd_norm_w', 'new_v_s5_a_re', 'new_v_s5_a_im', 'new_v_s5_log_dt', 'new_v_s5_b_re', 'new_v_s5_b_im', 'new_v_s5_c_re', 'new_v_s5_c_im', 'new_v_s5_d', 'new_v_w_glu', 'new_v_b_glu', 'new_v_w_out', 'new_v_ln2_g', 'new_v_ln2_b', 'new_v_ffn2_w1', 'new_v_ffn2_w3', 'new_v_ffn2_w2', 'new_v_ln3_g', 'new_v_ln3_b']
TWIN_LEAF_KINDS = {'loss': 'loss', 'grad_x': 'grad_x', 'grad_w_ada': 'grad_w', 'grad_b_ada': 'grad_w', 'grad_ffn1_w1': 'grad_w', 'grad_ffn1_w3': 'grad_w', 'grad_ffn1_w2': 'grad_w', 'grad_ln1_g': 'grad_w', 'grad_ln1_b': 'grad_w', 'grad_w_in': 'grad_w', 'grad_conv_w': 'grad_w', 'grad_conv_b': 'grad_w', 'grad_dt_bias': 'grad_w', 'grad_a_log': 'grad_w', 'grad_d_ssd': 'grad_w', 'grad_ssd_norm_w': 'grad_w', 'grad_s5_a_re': 'grad_w', 'grad_s5_a_im': 'grad_w', 'grad_s5_log_dt': 'grad_w', 'grad_s5_b_re': 'grad_w', 'grad_s5_b_im': 'grad_w', 'grad_s5_c_re': 'grad_w', 'grad_s5_c_im': 'grad_w', 'grad_s5_d': 'grad_w', 'grad_w_glu': 'grad_w', 'grad_b_glu': 'grad_w', 'grad_w_out': 'grad_w', 'grad_ln2_g': 'grad_w', 'grad_ln2_b': 'grad_w', 'grad_ffn2_w1': 'grad_w', 'grad_ffn2_w3': 'grad_w', 'grad_ffn2_w2': 'grad_w', 'grad_ln3_g': 'grad_w', 'grad_ln3_b': 'grad_w', 'delta_w_ada': 'delta_w', 'delta_b_ada': 'delta_w', 'delta_ffn1_w1': 'delta_w', 'delta_ffn1_w3': 'delta_w', 'delta_ffn1_w2': 'delta_w', 'delta_ln1_g': 'delta_w', 'delta_ln1_b': 'delta_w', 'delta_w_in': 'delta_w', 'delta_conv_w': 'delta_w', 'delta_conv_b': 'delta_w', 'delta_dt_bias': 'delta_w', 'delta_a_log': 'delta_w', 'delta_d_ssd': 'delta_w', 'delta_ssd_norm_w': 'delta_w', 'delta_s5_a_re': 'delta_w', 'delta_s5_a_im': 'delta_w', 'delta_s5_log_dt': 'delta_w', 'delta_s5_b_re': 'delta_w', 'delta_s5_b_im': 'delta_w', 'delta_s5_c_re': 'delta_w', 'delta_s5_c_im': 'delta_w', 'delta_s5_d': 'delta_w', 'delta_w_glu': 'delta_w', 'delta_b_glu': 'delta_w', 'delta_w_out': 'delta_w', 'delta_ln2_g': 'delta_w', 'delta_ln2_b': 'delta_w', 'delta_ffn2_w1': 'delta_w', 'delta_ffn2_w3': 'delta_w', 'delta_ffn2_w2': 'delta_w', 'delta_ln3_g': 'delta_w', 'delta_ln3_b': 'delta_w', 'new_m_w_ada': 'new_m', 'new_m_b_ada': 'new_m', 'new_m_ffn1_w1': 'new_m', 'new_m_ffn1_w3': 'new_m', 'new_m_ffn1_w2': 'new_m', 'new_m_ln1_g': 'new_m', 'new_m_ln1_b': 'new_m', 'new_m_w_in': 'new_m', 'new_m_conv_w': 'new_m', 'new_m_conv_b': 'new_m', 'new_m_dt_bias': 'new_m', 'new_m_a_log': 'new_m', 'new_m_d_ssd': 'new_m', 'new_m_ssd_norm_w': 'new_m', 'new_m_s5_a_re': 'new_m', 'new_m_s5_a_im': 'new_m', 'new_m_s5_log_dt': 'new_m', 'new_m_s5_b_re': 'new_m', 'new_m_s5_b_im': 'new_m', 'new_m_s5_c_re': 'new_m', 'new_m_s5_c_im': 'new_m', 'new_m_s5_d': 'new_m', 'new_m_w_glu': 'new_m', 'new_m_b_glu': 'new_m', 'new_m_w_out': 'new_m', 'new_m_ln2_g': 'new_m', 'new_m_ln2_b': 'new_m', 'new_m_ffn2_w1': 'new_m', 'new_m_ffn2_w3': 'new_m', 'new_m_ffn2_w2': 'new_m', 'new_m_ln3_g': 'new_m', 'new_m_ln3_b': 'new_m', 'new_v_w_ada': 'new_v', 'new_v_b_ada': 'new_v', 'new_v_ffn1_w1': 'new_v', 'new_v_ffn1_w3': 'new_v', 'new_v_ffn1_w2': 'new_v', 'new_v_ln1_g': 'new_v', 'new_v_ln1_b': 'new_v', 'new_v_w_in': 'new_v', 'new_v_conv_w': 'new_v', 'new_v_conv_b': 'new_v', 'new_v_dt_bias': 'new_v', 'new_v_a_log': 'new_v', 'new_v_d_ssd': 'new_v', 'new_v_ssd_norm_w': 'new_v', 'new_v_s5_a_re': 'new_v', 'new_v_s5_a_im': 'new_v', 'new_v_s5_log_dt': 'new_v', 'new_v_s5_b_re': 'new_v', 'new_v_s5_b_im': 'new_v', 'new_v_s5_c_re': 'new_v', 'new_v_s5_c_im': 'new_v', 'new_v_s5_d': 'new_v', 'new_v_w_glu': 'new_v', 'new_v_b_glu': 'new_v', 'new_v_w_out': 'new_v', 'new_v_ln2_g': 'new_v', 'new_v_ln2_b': 'new_v', 'new_v_ffn2_w1': 'new_v', 'new_v_ffn2_w3': 'new_v', 'new_v_ffn2_w2': 'new_v', 'new_v_ln3_g': 'new_v', 'new_v_ln3_b': 'new_v'}


def _forward(args):
    return _fwd_reference(*[args[k] for k in FWD_PARAMS])


def _output_shape():
    out = _jax.eval_shape(lambda: _forward(_fwd_setup_inputs(0)))
    return out.shape, out.dtype

N_MICROBATCH = 1
ADAM_LR = 0.001
ADAM_B1 = 0.9
ADAM_B2 = 0.999
ADAM_EPS = 1e-08
ADAM_WD = 0.01
ADAM_STEP = 10
PER_EXAMPLE_BATCH_AXIS = {'x': 0, 'c': 0, 'loss_target': 0}
SHARED_INPUTS = []
_WEIGHT_DTYPES = {'w_ada': _jnp.float32, 'b_ada': _jnp.float32, 'ffn1_w1': _jnp.float32, 'ffn1_w3': _jnp.float32, 'ffn1_w2': _jnp.float32, 'ln1_g': _jnp.float32, 'ln1_b': _jnp.float32, 'w_in': _jnp.float32, 'conv_w': _jnp.float32, 'conv_b': _jnp.float32, 'dt_bias': _jnp.float32, 'a_log': _jnp.float32, 'd_ssd': _jnp.float32, 'ssd_norm_w': _jnp.float32, 's5_a_re': _jnp.float32, 's5_a_im': _jnp.float32, 's5_log_dt': _jnp.float32, 's5_b_re': _jnp.float32, 's5_b_im': _jnp.float32, 's5_c_re': _jnp.float32, 's5_c_im': _jnp.float32, 's5_d': _jnp.float32, 'w_glu': _jnp.float32, 'b_glu': _jnp.float32, 'w_out': _jnp.float32, 'ln2_g': _jnp.float32, 'ln2_b': _jnp.float32, 'ffn2_w1': _jnp.float32, 'ffn2_w3': _jnp.float32, 'ffn2_w2': _jnp.float32, 'ln3_g': _jnp.float32, 'ln3_b': _jnp.float32}
MOMENT_SCALE = {'w_ada': 2.904595e-02, 'b_ada': 4.840176e-02, 'ffn1_w1': 8.622200e-03, 'ffn1_w3': 8.363124e-03, 'ffn1_w2': 2.331802e-02, 'ln1_g': 1.994451e+00, 'ln1_b': 6.809682e-01, 'w_in': 3.006413e-02, 'conv_w': 2.999926e-02, 'conv_b': 4.254740e-02, 'dt_bias': 5.498699e-02, 'a_log': 4.156427e-01, 'd_ssd': 1.838553e-01, 'ssd_norm_w': 4.331624e-02, 's5_a_re': 1.531883e-03, 's5_a_im': 2.265346e-03, 's5_log_dt': 5.627277e-01, 's5_b_re': 1.150672e-03, 's5_b_im': 1.089316e-03, 's5_c_re': 1.683215e-03, 's5_c_im': 1.587836e-03, 's5_d': 1.907546e-02, 'w_glu': 5.039796e-03, 'b_glu': 7.109532e-03, 'w_out': 5.256794e-02, 'ln2_g': 1.986862e+00, 'ln2_b': 6.865391e-01, 'ffn2_w1': 8.653817e-03, 'ffn2_w3': 8.401182e-03, 'ffn2_w2': 2.350638e-02, 'ln3_g': 6.395302e+01, 'ln3_b': 1.574099e+00}


def _to_microbatches(a, axis):
    t = _jnp.moveaxis(a, axis, 0)
    t = t.reshape((N_MICROBATCH, t.shape[0] // N_MICROBATCH) + t.shape[1:])
    return _jnp.moveaxis(t, 1, axis + 1)


def setup_inputs(seed: int = 0) -> dict:
    inp = _fwd_setup_inputs(seed)
    key = _jax.random.fold_in(_jax.random.key(seed), 7919)
    shape, _ = _output_shape()
    out = dict(inp)
    out["loss_target"] = _jax.random.normal(_jax.random.fold_in(key, 0), shape, _jnp.float32)
    for i, name in enumerate(TWIN_WEIGHTS):
        w = inp[name].astype(_jnp.float32)
        if MOMENT_SCALE is None:
            s = _jnp.sqrt(_jnp.mean(_jnp.square(w)) + 1e-30)
        else:
            s = MOMENT_SCALE[name]
        km, kv = _jax.random.split(_jax.random.fold_in(key, i + 1))
        out[name] = w
        out["m_" + name] = s * _jax.random.normal(km, w.shape, _jnp.float32)
        out["v_" + name] = (s * s) * _jax.random.uniform(kv, w.shape, _jnp.float32, 0.5, 1.5)
    if N_MICROBATCH > 1:
        for name, axis in PER_EXAMPLE_BATCH_AXIS.items():
            out[name] = _to_microbatches(out[name], axis)
    return {'x': out['x'], 'c': out['c'], 'w_ada': out['w_ada'], 'b_ada': out['b_ada'], 'ffn1_w1': out['ffn1_w1'], 'ffn1_w3': out['ffn1_w3'], 'ffn1_w2': out['ffn1_w2'], 'ln1_g': out['ln1_g'], 'ln1_b': out['ln1_b'], 'w_in': out['w_in'], 'conv_w': out['conv_w'], 'conv_b': out['conv_b'], 'dt_bias': out['dt_bias'], 'a_log': out['a_log'], 'd_ssd': out['d_ssd'], 'ssd_norm_w': out['ssd_norm_w'], 's5_a_re': out['s5_a_re'], 's5_a_im': out['s5_a_im'], 's5_log_dt': out['s5_log_dt'], 's5_b_re': out['s5_b_re'], 's5_b_im': out['s5_b_im'], 's5_c_re': out['s5_c_re'], 's5_c_im': out['s5_c_im'], 's5_d': out['s5_d'], 'w_glu': out['w_glu'], 'b_glu': out['b_glu'], 'w_out': out['w_out'], 'ln2_g': out['ln2_g'], 'ln2_b': out['ln2_b'], 'ffn2_w1': out['ffn2_w1'], 'ffn2_w3': out['ffn2_w3'], 'ffn2_w2': out['ffn2_w2'], 'ln3_g': out['ln3_g'], 'ln3_b': out['ln3_b'], 'loss_target': out['loss_target'], 'm_w_ada': out['m_w_ada'], 'm_b_ada': out['m_b_ada'], 'm_ffn1_w1': out['m_ffn1_w1'], 'm_ffn1_w3': out['m_ffn1_w3'], 'm_ffn1_w2': out['m_ffn1_w2'], 'm_ln1_g': out['m_ln1_g'], 'm_ln1_b': out['m_ln1_b'], 'm_w_in': out['m_w_in'], 'm_conv_w': out['m_conv_w'], 'm_conv_b': out['m_conv_b'], 'm_dt_bias': out['m_dt_bias'], 'm_a_log': out['m_a_log'], 'm_d_ssd': out['m_d_ssd'], 'm_ssd_norm_w': out['m_ssd_norm_w'], 'm_s5_a_re': out['m_s5_a_re'], 'm_s5_a_im': out['m_s5_a_im'], 'm_s5_log_dt': out['m_s5_log_dt'], 'm_s5_b_re': out['m_s5_b_re'], 'm_s5_b_im': out['m_s5_b_im'], 'm_s5_c_re': out['m_s5_c_re'], 'm_s5_c_im': out['m_s5_c_im'], 'm_s5_d': out['m_s5_d'], 'm_w_glu': out['m_w_glu'], 'm_b_glu': out['m_b_glu'], 'm_w_out': out['m_w_out'], 'm_ln2_g': out['m_ln2_g'], 'm_ln2_b': out['m_ln2_b'], 'm_ffn2_w1': out['m_ffn2_w1'], 'm_ffn2_w3': out['m_ffn2_w3'], 'm_ffn2_w2': out['m_ffn2_w2'], 'm_ln3_g': out['m_ln3_g'], 'm_ln3_b': out['m_ln3_b'], 'v_w_ada': out['v_w_ada'], 'v_b_ada': out['v_b_ada'], 'v_ffn1_w1': out['v_ffn1_w1'], 'v_ffn1_w3': out['v_ffn1_w3'], 'v_ffn1_w2': out['v_ffn1_w2'], 'v_ln1_g': out['v_ln1_g'], 'v_ln1_b': out['v_ln1_b'], 'v_w_in': out['v_w_in'], 'v_conv_w': out['v_conv_w'], 'v_conv_b': out['v_conv_b'], 'v_dt_bias': out['v_dt_bias'], 'v_a_log': out['v_a_log'], 'v_d_ssd': out['v_d_ssd'], 'v_ssd_norm_w': out['v_ssd_norm_w'], 'v_s5_a_re': out['v_s5_a_re'], 'v_s5_a_im': out['v_s5_a_im'], 'v_s5_log_dt': out['v_s5_log_dt'], 'v_s5_b_re': out['v_s5_b_re'], 'v_s5_b_im': out['v_s5_b_im'], 'v_s5_c_re': out['v_s5_c_re'], 'v_s5_c_im': out['v_s5_c_im'], 'v_s5_d': out['v_s5_d'], 'v_w_glu': out['v_w_glu'], 'v_b_glu': out['v_b_glu'], 'v_w_out': out['v_w_out'], 'v_ln2_g': out['v_ln2_g'], 'v_ln2_b': out['v_ln2_b'], 'v_ffn2_w1': out['v_ffn2_w1'], 'v_ffn2_w3': out['v_ffn2_w3'], 'v_ffn2_w2': out['v_ffn2_w2'], 'v_ln3_g': out['v_ln3_g'], 'v_ln3_b': out['v_ln3_b']}


def _loss(weights, diff, rest, loss_target):
    with _jax.named_scope("forward"):
        args = {**rest, TWIN_DIFF_INPUT: diff, **{k: w.astype(_WEIGHT_DTYPES[k]) for k, w in weights.items()}}
        y = _forward(args)
    with _jax.named_scope("loss_head"):
        err = _jnp.square(y.astype(_jnp.float32) - loss_target)
        return 0.5 * _jnp.sum(_jnp.mean(err, axis=-1)) if err.ndim else 0.5 * err


def _adamw(w, g, m, v):
    m = ADAM_B1 * m + (1.0 - ADAM_B1) * g
    v = ADAM_B2 * v + (1.0 - ADAM_B2) * _jnp.square(g)
    m_hat = m / (1.0 - ADAM_B1 ** ADAM_STEP)
    v_hat = v / (1.0 - ADAM_B2 ** ADAM_STEP)
    delta = -ADAM_LR * (m_hat / (_jnp.sqrt(v_hat) + ADAM_EPS) + ADAM_WD * w)
    return delta, m, v


def reference(x, c, w_ada, b_ada, ffn1_w1, ffn1_w3, ffn1_w2, ln1_g, ln1_b, w_in, conv_w, conv_b, dt_bias, a_log, d_ssd, ssd_norm_w, s5_a_re, s5_a_im, s5_log_dt, s5_b_re, s5_b_im, s5_c_re, s5_c_im, s5_d, w_glu, b_glu, w_out, ln2_g, ln2_b, ffn2_w1, ffn2_w3, ffn2_w2, ln3_g, ln3_b, loss_target, m_w_ada, m_b_ada, m_ffn1_w1, m_ffn1_w3, m_ffn1_w2, m_ln1_g, m_ln1_b, m_w_in, m_conv_w, m_conv_b, m_dt_bias, m_a_log, m_d_ssd, m_ssd_norm_w, m_s5_a_re, m_s5_a_im, m_s5_log_dt, m_s5_b_re, m_s5_b_im, m_s5_c_re, m_s5_c_im, m_s5_d, m_w_glu, m_b_glu, m_w_out, m_ln2_g, m_ln2_b, m_ffn2_w1, m_ffn2_w3, m_ffn2_w2, m_ln3_g, m_ln3_b, v_w_ada, v_b_ada, v_ffn1_w1, v_ffn1_w3, v_ffn1_w2, v_ln1_g, v_ln1_b, v_w_in, v_conv_w, v_conv_b, v_dt_bias, v_a_log, v_d_ssd, v_ssd_norm_w, v_s5_a_re, v_s5_a_im, v_s5_log_dt, v_s5_b_re, v_s5_b_im, v_s5_c_re, v_s5_c_im, v_s5_d, v_w_glu, v_b_glu, v_w_out, v_ln2_g, v_ln2_b, v_ffn2_w1, v_ffn2_w3, v_ffn2_w2, v_ln3_g, v_ln3_b):
    given = dict(x=x, c=c, w_ada=w_ada, b_ada=b_ada, ffn1_w1=ffn1_w1, ffn1_w3=ffn1_w3, ffn1_w2=ffn1_w2, ln1_g=ln1_g, ln1_b=ln1_b, w_in=w_in, conv_w=conv_w, conv_b=conv_b, dt_bias=dt_bias, a_log=a_log, d_ssd=d_ssd, ssd_norm_w=ssd_norm_w, s5_a_re=s5_a_re, s5_a_im=s5_a_im, s5_log_dt=s5_log_dt, s5_b_re=s5_b_re, s5_b_im=s5_b_im, s5_c_re=s5_c_re, s5_c_im=s5_c_im, s5_d=s5_d, w_glu=w_glu, b_glu=b_glu, w_out=w_out, ln2_g=ln2_g, ln2_b=ln2_b, ffn2_w1=ffn2_w1, ffn2_w3=ffn2_w3, ffn2_w2=ffn2_w2, ln3_g=ln3_g, ln3_b=ln3_b, loss_target=loss_target, m_w_ada=m_w_ada, m_b_ada=m_b_ada, m_ffn1_w1=m_ffn1_w1, m_ffn1_w3=m_ffn1_w3, m_ffn1_w2=m_ffn1_w2, m_ln1_g=m_ln1_g, m_ln1_b=m_ln1_b, m_w_in=m_w_in, m_conv_w=m_conv_w, m_conv_b=m_conv_b, m_dt_bias=m_dt_bias, m_a_log=m_a_log, m_d_ssd=m_d_ssd, m_ssd_norm_w=m_ssd_norm_w, m_s5_a_re=m_s5_a_re, m_s5_a_im=m_s5_a_im, m_s5_log_dt=m_s5_log_dt, m_s5_b_re=m_s5_b_re, m_s5_b_im=m_s5_b_im, m_s5_c_re=m_s5_c_re, m_s5_c_im=m_s5_c_im, m_s5_d=m_s5_d, m_w_glu=m_w_glu, m_b_glu=m_b_glu, m_w_out=m_w_out, m_ln2_g=m_ln2_g, m_ln2_b=m_ln2_b, m_ffn2_w1=m_ffn2_w1, m_ffn2_w3=m_ffn2_w3, m_ffn2_w2=m_ffn2_w2, m_ln3_g=m_ln3_g, m_ln3_b=m_ln3_b, v_w_ada=v_w_ada, v_b_ada=v_b_ada, v_ffn1_w1=v_ffn1_w1, v_ffn1_w3=v_ffn1_w3, v_ffn1_w2=v_ffn1_w2, v_ln1_g=v_ln1_g, v_ln1_b=v_ln1_b, v_w_in=v_w_in, v_conv_w=v_conv_w, v_conv_b=v_conv_b, v_dt_bias=v_dt_bias, v_a_log=v_a_log, v_d_ssd=v_d_ssd, v_ssd_norm_w=v_ssd_norm_w, v_s5_a_re=v_s5_a_re, v_s5_a_im=v_s5_a_im, v_s5_log_dt=v_s5_log_dt, v_s5_b_re=v_s5_b_re, v_s5_b_im=v_s5_b_im, v_s5_c_re=v_s5_c_re, v_s5_c_im=v_s5_c_im, v_s5_d=v_s5_d, v_w_glu=v_w_glu, v_b_glu=v_b_glu, v_w_out=v_w_out, v_ln2_g=v_ln2_g, v_ln2_b=v_ln2_b, v_ffn2_w1=v_ffn2_w1, v_ffn2_w3=v_ffn2_w3, v_ffn2_w2=v_ffn2_w2, v_ln3_g=v_ln3_g, v_ln3_b=v_ln3_b)
    weights = {n: given[n] for n in TWIN_WEIGHTS}
    shared = {n: given[n] for n in SHARED_INPUTS}
    per_example = {n: given[n] for n in ['x', 'c']}
    grad_fn = _jax.value_and_grad(_loss, argnums=(0, 1))

    def one_microbatch(ex, loss_target):
        ex = dict(ex)
        diff = ex.pop(TWIN_DIFF_INPUT)
        return grad_fn(weights, diff, {**shared, **ex}, loss_target)

    if N_MICROBATCH == 1:
        loss, (grad_w, grad_x) = one_microbatch(per_example, given["loss_target"])
    else:
        def body(carry, xs):
            loss_sum, grad_sum = carry
            l_k, (gw_k, gx_k) = one_microbatch(xs[0], xs[1])
            with _jax.named_scope("update"):
                return (loss_sum + l_k, _jax.tree.map(_jnp.add, grad_sum, gw_k)), gx_k

        init = (_jnp.zeros((), _jnp.float32), _jax.tree.map(_jnp.zeros_like, weights))
        (loss, grad_w), grad_x = _jax.lax.scan(body, init, (per_example, given["loss_target"]))
    with _jax.named_scope("update"):
        delta_w, new_m, new_v = {}, {}, {}
        for n in TWIN_WEIGHTS:
            delta_w[n], new_m[n], new_v[n] = _adamw(weights[n], grad_w[n], given["m_" + n], given["v_" + n])
    return (loss, grad_x, *[grad_w[n] for n in TWIN_WEIGHTS], *[delta_w[n] for n in TWIN_WEIGHTS],
            *[new_m[n] for n in TWIN_WEIGHTS], *[new_v[n] for n in TWIN_WEIGHTS])
```

```python
import math

import jax
import jax.numpy as jnp
from jax import lax
from jax.experimental import pallas as pl
from jax.experimental.pallas import tpu as pltpu

F32 = jnp.float32
BF16 = jnp.bfloat16
HI = lax.Precision.HIGHEST
MESH = pl.DeviceIdType.MESH

N_DEV = 8
D_MODEL = 1024
D_FF = 2816
N_MOD = 9
SSD_WIDTH = 512
SSD_HEADS = 8
SSD_HEAD_DIM = 64
SSD_GROUPS = 2
SSD_STATE = 128
SSD_CHUNK = 128
GROUP_COLS = SSD_WIDTH // SSD_GROUPS
HEADS_PER_GROUP = SSD_HEADS // SSD_GROUPS
CONV_K = 4
CONV_CH = 1024
S5_WIDTH = 512
S5_GROUPS = 32
S5_GROUP_CH = 16
S5_STATE = 64
S5_COLS = S5_GROUPS * S5_STATE
S5_Q = 4
S5_CHUNK = 256
ALPHA = 2.0 ** 0.25
LN_EPS = 1e-5
LANE = 128
HALO = 8

P_XBC, P_Z, P_U, P_DT = 0, 1024, 1536, 2048
P_COLS = 2048 + SSD_GROUPS * LANE

ADAM_LR, ADAM_B1, ADAM_B2, ADAM_EPS, ADAM_WD, ADAM_STEP = 0.001, 0.9, 0.999, 1e-08, 0.01, 10

VMEM_LIMIT = 56 * 1024 * 1024


def _cp(*sem):
    return pltpu.CompilerParams(dimension_semantics=sem if sem else None, vmem_limit_bytes=VMEM_LIMIT)


def _dg(a, b, ca, cb):
    return lax.dot_general(a.astype(BF16), b.astype(BF16), (((ca,), (cb,)), ((), ())), preferred_element_type=F32)


@jax.custom_vjp
def bdot_nn(a, b):
    return _dg(a, b, 1, 0)


bdot_nn.defvjp(lambda a, b: (_dg(a, b, 1, 0), (a, b)),
               lambda r, g: (_dg(g, r[1], 1, 1), _dg(r[0], g, 0, 0)))


@jax.custom_vjp
def bdot_nt(a, b):
    return _dg(a, b, 1, 1)


bdot_nt.defvjp(lambda a, b: (_dg(a, b, 1, 1), (a, b)),
               lambda r, g: (_dg(g, r[1], 1, 0), _dg(g, r[0], 0, 0)))


@jax.custom_vjp
def bdot_tn(a, b):
    return _dg(a, b, 0, 0)


bdot_tn.defvjp(lambda a, b: (_dg(a, b, 0, 0), (a, b)),
               lambda r, g: (_dg(r[1], g, 1, 1), _dg(r[0], g, 1, 0)))


def _take_col(z):
    @jax.custom_vjp
    def take(x):
        return x[:, z:z + 1]

    def bwd(shape, g):
        hot = (lax.broadcasted_iota(jnp.int32, (1, shape[1]), 1) == z).astype(F32)
        return (g * hot,)

    take.defvjp(lambda x: (x[:, z:z + 1], x.shape), bwd)
    return take


def _take_row(z):
    @jax.custom_vjp
    def take(x):
        return x[z:z + 1, :]

    def bwd(shape, g):
        hot = (lax.broadcasted_iota(jnp.int32, (shape[0], 1), 0) == z).astype(F32)
        return (hot * g,)

    take.defvjp(lambda x: (x[z:z + 1, :], x.shape), bwd)
    return take


def _view(a):
    return a if isinstance(a, tuple) else (a, 0, a.shape[1])


def _col_spec(view, rows, width, index):
    _, off, _ = view
    assert off % width == 0
    return pl.BlockSpec((rows, width), lambda *g: (index(*g)[0], off // width + index(*g)[1]))


def _rw_in_specs(rows, bps, gps, tm, tps):
    specs = [_col_spec(_view(r), tm, _view(r)[2], lambda i: (i, 0)) for r in rows]
    specs += [pl.BlockSpec((1, 1, b.shape[2]), lambda i: (i // tps, 0, 0)) for b in bps]
    specs += [pl.BlockSpec(g.shape, lambda i, nd=g.ndim: (0,) * nd) for g in gps]
    return specs


def _rw_vals(refs, nr, nb, ng):
    vals = [r[...] for r in refs[:nr]]
    vals += [b[0] for b in refs[nr:nr + nb]]
    vals += [g[...] for g in refs[nr + nb:nr + nb + ng]]
    return vals


def rowwise_fwd(name, f, rows, bps, gps, outs, seq, tm):
    t = _view(rows[0])[0].shape[0]
    tps = seq // tm
    nr, nb, ng = len(rows), len(bps), len(gps)

    def body(*refs):
        res = f(*_rw_vals(refs, nr, nb, ng))
        for o, v in zip(refs[nr + nb + ng:], res):
            o[...] = v.astype(o.dtype)

    return pl.pallas_call(
        body, name=name, grid=(t // tm,),
        in_specs=_rw_in_specs(rows, bps, gps, tm, tps),
        out_specs=[pl.BlockSpec((tm, c), lambda i: (i, 0)) for c, _ in outs],
        out_shape=[jax.ShapeDtypeStruct((t, c), d) for c, d in outs],
        compiler_params=_cp("arbitrary"),
    )(*[_view(r)[0] for r in rows], *bps, *gps)


def rowwise_bwd(name, f, rows, bps, gps, douts, seq, tm, row_grads, add_rows=None):
    add_rows = add_rows or {}
    t = _view(rows[0])[0].shape[0]
    tps = seq // tm
    nr, nb, ng, nd = len(rows), len(bps), len(gps), len(douts)
    want = [k for k in range(nr) if row_grads[k] is not None]
    adds = sorted(add_rows)
    n_in = nr + nb + ng + nd + len(adds)

    def body(*refs):
        vals = _rw_vals(refs, nr, nb, ng)
        dvals = tuple(r[...] for r in refs[nr + nb + ng:nr + nb + ng + nd])
        add_refs = dict(zip(adds, refs[nr + nb + ng + nd:n_in]))
        out_refs = refs[n_in:]
        _, pull = jax.vjp(f, *vals)
        grads = pull(dvals)
        i = pl.program_id(0)
        for o, k in zip(out_refs, want):
            g = grads[k]
            if k in add_refs:
                g = g + add_refs[k][...]
            o[...] = g.astype(o.dtype)
        for j in range(nb):
            o = out_refs[len(want) + j]

            @pl.when(i % tps == 0)
            def _(o=o):
                o[...] = jnp.zeros_like(o)

            o[0] = o[0] + grads[nr + j]
        for j in range(ng):
            o = out_refs[len(want) + nb + j]

            @pl.when(i == 0)
            def _(o=o):
                o[...] = jnp.zeros_like(o)

            o[...] = o[...] + grads[nr + nb + j]

    in_specs = _rw_in_specs(rows, bps, gps, tm, tps)
    in_specs += [pl.BlockSpec((tm, d.shape[1]), lambda i: (i, 0)) for d in douts]
    in_specs += [pl.BlockSpec((tm, add_rows[k].shape[1]), lambda i: (i, 0)) for k in adds]
    out_specs = [pl.BlockSpec((tm, _view(rows[k])[2]), lambda i: (i, 0)) for k in want]
    out_shape = [jax.ShapeDtypeStruct((t, _view(rows[k])[2]), row_grads[k]) for k in want]
    out_specs += [pl.BlockSpec((1, 1, b.shape[2]), lambda i: (i // tps, 0, 0)) for b in bps]
    out_shape += [jax.ShapeDtypeStruct(b.shape, F32) for b in bps]
    out_specs += [pl.BlockSpec(g.shape, lambda i, n=g.ndim: (0,) * n) for g in gps]
    out_shape += [jax.ShapeDtypeStruct(g.shape, F32) for g in gps]
    res = pl.pallas_call(
        body, name=name, grid=(t // tm,), in_specs=in_specs, out_specs=out_specs, out_shape=out_shape,
        compiler_params=_cp("arbitrary"),
    )(*[_view(r)[0] for r in rows], *bps, *gps, *douts, *[add_rows[k] for k in adds])
    nw = len(want)
    return res[:nw], res[nw:nw + nb], res[nw + nb:]


def mm_nn(name, xs, ws, tm, tn, out_dtype=F32):
    views = [_view(x) for x in xs]
    t, n, k = views[0][0].shape[0], ws[0].shape[1], len(xs)

    def body(*refs):
        acc = _dg(refs[0][...], refs[k][...], 1, 0)
        for i in range(1, k):
            acc = acc + _dg(refs[i][...], refs[k + i][...], 1, 0)
        refs[2 * k][...] = acc.astype(out_dtype)

    in_specs = [_col_spec(v, tm, v[2], lambda i, j: (i, 0)) for v in views]
    in_specs += [pl.BlockSpec((w.shape[0], tn), lambda i, j: (0, j)) for w in ws]
    return pl.pallas_call(
        body, name=name, grid=(t // tm, n // tn), in_specs=in_specs,
        out_specs=pl.BlockSpec((tm, tn), lambda i, j: (i, j)),
        out_shape=jax.ShapeDtypeStruct((t, n), out_dtype),
        compiler_params=_cp("parallel", "parallel"),
    )(*[v[0] for v in views], *ws)


def mm_nt(name, dys, ws, tm, tk, out_dtype=F32):
    views = [_view(d) for d in dys]
    t, kk, k = views[0][0].shape[0], ws[0].shape[0], len(dys)

    def body(*refs):
        acc = _dg(refs[0][...], refs[k][...], 1, 1)
        for i in range(1, k):
            acc = acc + _dg(refs[i][...], refs[k + i][...], 1, 1)
        refs[2 * k][...] = acc.astype(out_dtype)

    in_specs = [_col_spec(v, tm, v[2], lambda i, j: (i, 0)) for v in views]
    in_specs += [pl.BlockSpec((tk, w.shape[1]), lambda i, j: (j, 0)) for w in ws]
    return pl.pallas_call(
        body, name=name, grid=(t // tm, kk // tk), in_specs=in_specs,
        out_specs=pl.BlockSpec((tm, tk), lambda i, j: (i, j)),
        out_shape=jax.ShapeDtypeStruct((t, kk), out_dtype),
        compiler_params=_cp("parallel", "parallel"),
    )(*[v[0] for v in views], *ws)


def mm_tn(name, x, dy, tk, tn, tt):
    xv, dv = _view(x), _view(dy)
    t, kk, n = xv[0].shape[0], xv[2], dv[2]

    def body(x_ref, d_ref, o_ref):
        @pl.when(pl.program_id(2) == 0)
        def _():
            o_ref[...] = jnp.zeros_like(o_ref)

        o_ref[...] += _dg(x_ref[...], d_ref[...], 0, 0)

    return pl.pallas_call(
        body, name=name, grid=(kk // tk, n // tn, t // tt),
        in_specs=[_col_spec(xv, tt, tk, lambda a, b, c: (c, a)), _col_spec(dv, tt, tn, lambda a, b, c: (c, b))],
        out_specs=pl.BlockSpec((tk, tn), lambda a, b, c: (a, b)),
        out_shape=jax.ShapeDtypeStruct((kk, n), F32),
        compiler_params=_cp("parallel", "parallel", "arbitrary"),
    )(xv[0], dv[0])


def _silu(x):
    return x * jax.nn.sigmoid(x)


def f_modulate(x, sc, sh):
    return (x * (1.0 + sc) + sh,)


def f_gate(a, b):
    return (_silu(a) * b,)


def _res_ln(coef):
    def f(x, y, g, lg, lb):
        r = ALPHA * x + (coef * g) * y
        mu = jnp.mean(r, axis=-1, keepdims=True)
        d = r - mu
        var = jnp.mean(d * d, axis=-1, keepdims=True)
        return (d * lax.rsqrt(var + LN_EPS) * lg + lb,)
    return f


def f_glu(y, w, b):
    g = jax.nn.gelu(y)
    return (g * jax.nn.sigmoid(bdot_nn(g, w) + b),)


def loss_head(y, target, tm):
    t, d = y.shape

    def body(y_ref, t_ref, dy_ref, l_ref):
        @pl.when(pl.program_id(0) == 0)
        def _():
            l_ref[...] = jnp.zeros_like(l_ref)

        e = y_ref[...] - t_ref[...]
        dy_ref[...] = e * (1.0 / d)
        l_ref[...] += 0.5 * jnp.sum(jnp.mean(e * e, axis=-1, keepdims=True), axis=0, keepdims=True)

    dy, l = pl.pallas_call(
        body, name="loss_head", grid=(t // tm,),
        in_specs=[pl.BlockSpec((tm, d), lambda i: (i, 0))] * 2,
        out_specs=[pl.BlockSpec((tm, d), lambda i: (i, 0)), pl.BlockSpec((1, 1), lambda i: (0, 0))],
        out_shape=[jax.ShapeDtypeStruct((t, d), F32), jax.ShapeDtypeStruct((1, 1), F32)],
        compiler_params=_cp("arbitrary"),
    )(y, target)
    return dy, l[0, 0]


def _shift_down(x, halo, k):
    if k == 0:
        return x
    r = pltpu.roll(x, k, 0)
    hr = pltpu.roll(halo, k, 0)
    row = lax.broadcasted_iota(jnp.int32, (HALO, 1), 0)
    top = jnp.where(row < k, hr, r[:HALO])
    return jnp.concatenate([top, r[HALO:]], axis=0)


def _shift_up(x, halo, k):
    if k == 0:
        return x
    n = x.shape[0]
    r = pltpu.roll(x, n - k, 0)
    hr = pltpu.roll(halo, HALO - k, 0)
    row = lax.broadcasted_iota(jnp.int32, (HALO, 1), 0)
    bot = jnp.where(row >= HALO - k, hr, r[n - HALO:])
    return jnp.concatenate([r[:n - HALO], bot], axis=0)


def _conv_pre(x, halo, w, b):
    acc = x * w[CONV_K - 1:CONV_K, :] + b
    for k in range(1, CONV_K):
        acc = acc + _shift_down(x, halo, k) * w[CONV_K - 1 - k:CONV_K - k, :]
    return acc


def _rows_before(width, tm):
    return pl.BlockSpec((HALO, width), lambda i: (jnp.maximum(i * (tm // HALO) - 1, 0), 0))


def conv_fwd(proj, w, b, seq, tm):
    t = proj.shape[0]
    tps = seq // tm

    def body(x_ref, h_ref, w_ref, b_ref, o_ref):
        first = (pl.program_id(0) % tps == 0)
        halo = jnp.where(first, 0.0, h_ref[...])
        o_ref[...] = _silu(_conv_pre(x_ref[...], halo, w_ref[...], b_ref[...]))

    return pl.pallas_call(
        body, name="conv_fwd", grid=(t // tm,),
        in_specs=[pl.BlockSpec((tm, CONV_CH), lambda i: (i, 0)), _rows_before(CONV_CH, tm),
                  pl.BlockSpec((CONV_K, CONV_CH), lambda i: (0, 0)), pl.BlockSpec((1, CONV_CH), lambda i: (0, 0))],
        out_specs=pl.BlockSpec((tm, CONV_CH), lambda i: (i, 0)),
        out_shape=jax.ShapeDtypeStruct((t, CONV_CH), F32),
        compiler_params=_cp("arbitrary"),
    )(proj, proj, w, b)


def conv_bwd_pre(proj, w, b, dxs, dbm, dcm, seq, tm):
    t = proj.shape[0]
    tps = seq // tm

    def body(x_ref, h_ref, w_ref, b_ref, d1, d2, d3, dp_ref, dw_ref, db_ref):
        i = pl.program_id(0)
        halo = jnp.where(i % tps == 0, 0.0, h_ref[...])
        x = x_ref[...]
        pre = _conv_pre(x, halo, w_ref[...], b_ref[...])
        sg = jax.nn.sigmoid(pre)
        dout = jnp.concatenate([d1[...], d2[...], d3[...]], axis=1)
        dp = dout * (sg * (1.0 + pre * (1.0 - sg)))
        dp_ref[...] = dp

        @pl.when(i == 0)
        def _():
            dw_ref[...] = jnp.zeros_like(dw_ref)
            db_ref[...] = jnp.zeros_like(db_ref)

        db_ref[...] += jnp.sum(dp, axis=0, keepdims=True)
        for k in range(CONV_K):
            j = CONV_K - 1 - k
            dw_ref[j:j + 1, :] += jnp.sum(dp * _shift_down(x, halo, k), axis=0, keepdims=True)

    return pl.pallas_call(
        body, name="conv_bwd_pre", grid=(t // tm,),
        in_specs=[pl.BlockSpec((tm, CONV_CH), lambda i: (i, 0)), _rows_before(CONV_CH, tm),
                  pl.BlockSpec((CONV_K, CONV_CH), lambda i: (0, 0)), pl.BlockSpec((1, CONV_CH), lambda i: (0, 0)),
                  pl.BlockSpec((tm, 512), lambda i: (i, 0)), pl.BlockSpec((tm, 256), lambda i: (i, 0)),
                  pl.BlockSpec((tm, 256), lambda i: (i, 0))],
        out_specs=[pl.BlockSpec((tm, CONV_CH), lambda i: (i, 0)), pl.BlockSpec((CONV_K, CONV_CH), lambda i: (0, 0)),
                   pl.BlockSpec((1, CONV_CH), lambda i: (0, 0))],
        out_shape=[jax.ShapeDtypeStruct((t, CONV_CH), F32), jax.ShapeDtypeStruct((CONV_K, CONV_CH), F32),
                   jax.ShapeDtypeStruct((1, CONV_CH), F32)],
        compiler_params=_cp("arbitrary"),
    )(proj, proj, w, b, dxs, dbm, dcm)


def conv_bwd_x(dpre, w, seq, tm):
    t = dpre.shape[0]
    tps = seq // tm
    blocks = tm // HALO
    last = t // HALO - 1

    def body(d_ref, h_ref, w_ref, o_ref):
        halo = jnp.where(pl.program_id(0) % tps == tps - 1, 0.0, h_ref[...])
        d = d_ref[...]
        w = w_ref[...]
        acc = d * w[CONV_K - 1:CONV_K, :]
        for k in range(1, CONV_K):
            acc = acc + _shift_up(d, halo, k) * w[CONV_K - 1 - k:CONV_K - k, :]
        o_ref[...] = acc

    return pl.pallas_call(
        body, name="conv_bwd_x", grid=(t // tm,),
        in_specs=[pl.BlockSpec((tm, CONV_CH), lambda i: (i, 0)),
                  pl.BlockSpec((HALO, CONV_CH), lambda i: (jnp.minimum((i + 1) * blocks, last), 0)),
                  pl.BlockSpec((CONV_K, CONV_CH), lambda i: (0, 0))],
        out_specs=pl.BlockSpec((tm, CONV_CH), lambda i: (i, 0)),
        out_shape=jax.ShapeDtypeStruct((t, CONV_CH), F32),
        compiler_params=_cp("arbitrary"),
    )(dpre, dpre, w)


def _softplus(x):
    return jnp.maximum(x, 0.0) + jnp.log1p(jnp.exp(-jnp.abs(x)))


def _ssd_chunk(xs, bg, cg, dtr, zz, hp, dtb, alog, dcol, nw):
    l = xs.shape[0]
    row = lax.broadcasted_iota(jnp.int32, (l, l), 0)
    col = lax.broadcasted_iota(jnp.int32, (l, l), 1)
    causal = row >= col
    tril = causal.astype(F32)
    expand = (lax.broadcasted_iota(jnp.int32, (LANE, GROUP_COLS), 1) // SSD_HEAD_DIM
              == lax.broadcasted_iota(jnp.int32, (LANE, GROUP_COLS), 0)).astype(F32)
    head_of_col = lax.broadcasted_iota(jnp.int32, (1, GROUP_COLS), 1) // SSD_HEAD_DIM
    last_row = (lax.broadcasted_iota(jnp.int32, (l, 1), 0) == l - 1).astype(F32)

    dtc = _softplus(dtr + dtb)
    a_c = dtc * (-jnp.exp(alog))
    acs_c = jnp.dot(tril, a_c, precision=HI, preferred_element_type=F32)
    dt_e = jnp.dot(dtc, expand, precision=HI, preferred_element_type=F32)
    acs_e = jnp.dot(acs_c, expand, precision=HI, preferred_element_type=F32)
    alast_e = jnp.sum(acs_e * last_row, axis=0, keepdims=True)
    x = xs * dt_e
    states = bdot_tn(bg, x * jnp.exp(alast_e - acs_e))
    h_next = jnp.exp(alast_e) * hp + states
    d_e = jnp.sum(dcol * expand, axis=0, keepdims=True)
    y = bdot_nn(cg, hp) * jnp.exp(acs_e) + d_e * xs
    cb = bdot_nt(cg, bg)
    acs_t = acs_c.T
    for z in range(HEADS_PER_GROUP):
        seg = _take_col(z)(acs_c) - _take_row(z)(acs_t)
        lmat = jnp.exp(jnp.where(causal, seg, -1e30))
        y = y + bdot_nn(cb * lmat, x * (head_of_col == z).astype(F32))
    yz = y * _silu(zz)
    ms = jnp.mean(yz * yz, axis=-1, keepdims=True)
    return yz * lax.rsqrt(ms + LN_EPS) * nw, h_next


def _ssd_in_specs(nc, rev):
    def tok(g, b, c):
        return b * nc + (nc - 1 - c if rev else c)

    return [
        pl.BlockSpec((SSD_CHUNK, GROUP_COLS), lambda g, b, c: (tok(g, b, c), g)),
        pl.BlockSpec((SSD_CHUNK, SSD_STATE), lambda g, b, c: (tok(g, b, c), 4 + g)),
        pl.BlockSpec((SSD_CHUNK, SSD_STATE), lambda g, b, c: (tok(g, b, c), 6 + g)),
        pl.BlockSpec((SSD_CHUNK, LANE), lambda g, b, c: (tok(g, b, c), P_DT // LANE + g)),
        pl.BlockSpec((SSD_CHUNK, GROUP_COLS), lambda g, b, c: (tok(g, b, c), P_Z // GROUP_COLS + g)),
        pl.BlockSpec((1, 1, LANE), lambda g, b, c: (g, 0, 0)),
        pl.BlockSpec((1, 1, LANE), lambda g, b, c: (g, 0, 0)),
        pl.BlockSpec((1, LANE, 1), lambda g, b, c: (g, 0, 0)),
        pl.BlockSpec((1, 1, GROUP_COLS), lambda g, b, c: (g, 0, 0)),
    ], tok


def ssd_fwd(xc, proj, dtb, alog, dcol, nw, bsz, seq):
    t = xc.shape[0]
    nc = seq // SSD_CHUNK
    in_specs, tok = _ssd_in_specs(nc, False)

    def body(xs, bg, cg, dtr, zz, dtb_r, alog_r, dcol_r, nw_r, y_ref, hp_ref, h_scr):
        @pl.when(pl.program_id(2) == 0)
        def _():
            h_scr[...] = jnp.zeros_like(h_scr)

        hp = h_scr[...]
        hp_ref[0, 0, 0] = hp
        y, hn = _ssd_chunk(xs[...], bg[...], cg[...], dtr[...], zz[...], hp, dtb_r[0], alog_r[0], dcol_r[0], nw_r[0])
        y_ref[...] = y
        h_scr[...] = hn

    return pl.pallas_call(
        body, name="ssd_fwd", grid=(SSD_GROUPS, bsz, nc), in_specs=in_specs,
        out_specs=[pl.BlockSpec((SSD_CHUNK, GROUP_COLS), lambda g, b, c: (tok(g, b, c), g)),
                   pl.BlockSpec((1, 1, 1, SSD_STATE, GROUP_COLS), lambda g, b, c: (g, b, c, 0, 0))],
        out_shape=[jax.ShapeDtypeStruct((t, SSD_WIDTH), F32),
                   jax.ShapeDtypeStruct((SSD_GROUPS, bsz, nc, SSD_STATE, GROUP_COLS), F32)],
        scratch_shapes=[pltpu.VMEM((SSD_STATE, GROUP_COLS), F32)],
        compiler_params=_cp("arbitrary", "arbitrary", "arbitrary"),
    )(xc, xc, xc, proj, proj, dtb, alog, dcol, nw)


def ssd_bwd(xc, proj, dtb, alog, dcol, nw, hprev, dy, bsz, seq):
    t = xc.shape[0]
    nc = seq // SSD_CHUNK
    in_specs, tok = _ssd_in_specs(nc, True)
    in_specs += [pl.BlockSpec((1, 1, 1, SSD_STATE, GROUP_COLS), lambda g, b, c: (g, b, nc - 1 - c, 0, 0)),
                 pl.BlockSpec((SSD_CHUNK, GROUP_COLS), lambda g, b, c: (tok(g, b, c), g))]

    def body(xs, bg, cg, dtr, zz, dtb_r, alog_r, dcol_r, nw_r, hp_ref, dy_ref,
             dxs, dbg, dcg, ddt, dzz, ddtb, dalog, ddcol, dnw, dh_scr):
        b, c = pl.program_id(1), pl.program_id(2)

        @pl.when(c == 0)
        def _():
            dh_scr[...] = jnp.zeros_like(dh_scr)

        @pl.when((b == 0) & (c == 0))
        def _():
            for r in (ddtb, dalog, ddcol, dnw):
                r[...] = jnp.zeros_like(r)

        _, pull = jax.vjp(_ssd_chunk, xs[...], bg[...], cg[...], dtr[...], zz[...], hp_ref[0, 0, 0],
                          dtb_r[0], alog_r[0], dcol_r[0], nw_r[0])
        g = pull((dy_ref[...], dh_scr[...]))
        dxs[...], dbg[...], dcg[...], ddt[...], dzz[...] = g[0], g[1], g[2], g[3], g[4]
        dh_scr[...] = g[5]
        ddtb[0] += g[6]
        dalog[0] += g[7]
        ddcol[0] += g[8]
        dnw[0] += g[9]

    def tile(w):
        return pl.BlockSpec((SSD_CHUNK, w), lambda g, b, c: (tok(g, b, c), g))

    return pl.pallas_call(
        body, name="ssd_bwd", grid=(SSD_GROUPS, bsz, nc), in_specs=in_specs,
        out_specs=[tile(GROUP_COLS), tile(SSD_STATE), tile(SSD_STATE), tile(LANE), tile(GROUP_COLS),
                   pl.BlockSpec((1, 1, LANE), lambda g, b, c: (g, 0, 0)),
                   pl.BlockSpec((1, 1, LANE), lambda g, b, c: (g, 0, 0)),
                   pl.BlockSpec((1, LANE, 1), lambda g, b, c: (g, 0, 0)),
                   pl.BlockSpec((1, 1, GROUP_COLS), lambda g, b, c: (g, 0, 0))],
        out_shape=[jax.ShapeDtypeStruct((t, SSD_WIDTH), F32), jax.ShapeDtypeStruct((t, 2 * SSD_STATE), F32),
                   jax.ShapeDtypeStruct((t, 2 * SSD_STATE), F32), jax.ShapeDtypeStruct((t, 2 * LANE), F32),
                   jax.ShapeDtypeStruct((t, SSD_WIDTH), F32),
                   jax.ShapeDtypeStruct((SSD_GROUPS, 1, LANE), F32), jax.ShapeDtypeStruct((SSD_GROUPS, 1, LANE), F32),
                   jax.ShapeDtypeStruct((SSD_GROUPS, LANE, 1), F32),
                   jax.ShapeDtypeStruct((SSD_GROUPS, 1, GROUP_COLS), F32)],
        scratch_shapes=[pltpu.VMEM((SSD_STATE, GROUP_COLS), F32)],
        compiler_params=_cp("arbitrary", "arbitrary", "arbitrary"),
    )(xc, xc, xc, proj, proj, dtb, alog, dcol, nw, hprev, dy)


def _disc_a(a_re, a_im, log_dt):
    dt = jnp.exp(log_dt)
    mag = jnp.exp(dt * a_re)
    ab_re, ab_im = mag * jnp.cos(dt * a_im), mag * jnp.sin(dt * a_im)
    den = a_re * a_re + a_im * a_im
    nr, ni = ab_re - 1.0, ab_im
    f_re, f_im = (nr * a_re + ni * a_im) / den, (ni * a_re - nr * a_im) / den
    return ab_re, ab_im, f_re, f_im


def _disc_b(f_re, f_im, b_re, b_im):
    return f_re * b_re - f_im * b_im, f_re * b_im + f_im * b_re


def _whole(f, name, args, outs):
    def body(*refs):
        res = f(*[r[...] for r in refs[:len(args)]])
        for o, v in zip(refs[len(args):], res):
            o[...] = v

    return pl.pallas_call(body, name=name, out_shape=[jax.ShapeDtypeStruct(s, F32) for s in outs])(*args)


def _whole_vjp(f, name, args, cts):
    def body(*refs):
        vals = [r[...] for r in refs[:len(args)]]
        _, pull = jax.vjp(f, *vals)
        res = pull(tuple(r[...] for r in refs[len(args):len(args) + len(cts)]))
        for o, v in zip(refs[len(args) + len(cts):], res):
            o[...] = v

    return pl.pallas_call(body, name=name, out_shape=[jax.ShapeDtypeStruct(a.shape, F32) for a in args])(*args, *cts)


def s5_tables(lam_re, lam_im, length):
    assert int(math.log2(length)) <= HALO

    def body(lr_ref, li_ref, pw_re, pw_im, up_re, up_im, dn_re, dn_im):
        lr, li = lr_ref[...], li_ref[...]

        def power(k):
            m = jnp.exp(k * lr)
            return m * jnp.cos(k * li), m * jnp.sin(k * li)

        srow = lax.broadcasted_iota(jnp.int32, (HALO, 1), 0)
        pw_re[...], pw_im[...] = power(jnp.left_shift(1, srow).astype(F32))
        trow = lax.broadcasted_iota(jnp.int32, (length, 1), 0)
        up_re[...], up_im[...] = power((trow + 1).astype(F32))
        dn_re[...], dn_im[...] = power((length - trow).astype(F32))

    shp = [jax.ShapeDtypeStruct((HALO, S5_COLS), F32)] * 2 + [jax.ShapeDtypeStruct((length, S5_COLS), F32)] * 4
    return pl.pallas_call(body, name="s5_tables", out_shape=shp)(lam_re, lam_im)


def _s5_specs(n5, rev):
    def tok(q, b, c):
        return b * n5 + (n5 - 1 - c if rev else c)

    qcols = S5_COLS // S5_Q
    specs = [
        pl.BlockSpec((S5_CHUNK, LANE), lambda q, b, c: (tok(q, b, c), P_U // LANE + q)),
        pl.BlockSpec((1, LANE, qcols), lambda q, b, c: (q, 0, 0)),
        pl.BlockSpec((1, LANE, qcols), lambda q, b, c: (q, 0, 0)),
        pl.BlockSpec((1, qcols, LANE), lambda q, b, c: (q, 0, 0)),
        pl.BlockSpec((1, qcols, LANE), lambda q, b, c: (q, 0, 0)),
        pl.BlockSpec((HALO, qcols), lambda q, b, c: (0, q)),
        pl.BlockSpec((HALO, qcols), lambda q, b, c: (0, q)),
        pl.BlockSpec((S5_CHUNK, qcols), lambda q, b, c: (0, q)),
        pl.BlockSpec((S5_CHUNK, qcols), lambda q, b, c: (0, q)),
        pl.BlockSpec((1, 1, LANE), lambda q, b, c: (q, 0, 0)),
    ]
    return specs, tok, qcols


def s5_fwd(proj, wb_re, wb_im, wc_re, wc_im, pw_re, pw_im, up_re, up_im, dvec, bsz, seq):
    t = proj.shape[0]
    n5 = seq // S5_CHUNK
    nsteps = int(math.log2(S5_CHUNK))
    in_specs, tok, qcols = _s5_specs(n5, False)

    def body(u_ref, wbr, wbi, wcr, wci, pwr, pwi, upr, upi, d_ref, y_ref, xr_ref, xi_ref, cr_scr, ci_scr):
        @pl.when(pl.program_id(2) == 0)
        def _():
            cr_scr[...] = jnp.zeros_like(cr_scr)
            ci_scr[...] = jnp.zeros_like(ci_scr)

        u = u_ref[...]
        xr, xi = _dg(u, wbr[0], 1, 0), _dg(u, wbi[0], 1, 0)
        row = lax.broadcasted_iota(jnp.int32, (S5_CHUNK, 1), 0)
        for s in range(nsteps):
            k = 1 << s
            ar, ai = pwr[s:s + 1, :], pwi[s:s + 1, :]
            sr = jnp.where(row >= k, pltpu.roll(xr, k, 0), 0.0)
            si = jnp.where(row >= k, pltpu.roll(xi, k, 0), 0.0)
            xr, xi = xr + ar * sr - ai * si, xi + ar * si + ai * sr
        cr, ci = cr_scr[...], ci_scr[...]
        pr, pi = upr[...], upi[...]
        xr, xi = xr + pr * cr - pi * ci, xi + pr * ci + pi * cr
        xr_ref[...], xi_ref[...] = xr, xi
        cr_scr[...], ci_scr[...] = xr[S5_CHUNK - 1:, :], xi[S5_CHUNK - 1:, :]
        y_ref[...] = _dg(xr, wcr[0], 1, 0) - _dg(xi, wci[0], 1, 0) + u * d_ref[0]

    def tile(w):
        return pl.BlockSpec((S5_CHUNK, w), lambda q, b, c: (tok(q, b, c), q))

    return pl.pallas_call(
        body, name="s5_fwd", grid=(S5_Q, bsz, n5), in_specs=in_specs,
        out_specs=[tile(LANE), tile(qcols), tile(qcols)],
        out_shape=[jax.ShapeDtypeStruct((t, S5_WIDTH), F32), jax.ShapeDtypeStruct((t, S5_COLS), F32),
                   jax.ShapeDtypeStruct((t, S5_COLS), F32)],
        scratch_shapes=[pltpu.VMEM((1, qcols), F32)] * 2,
        compiler_params=_cp("arbitrary", "arbitrary", "arbitrary"),
    )(proj, wb_re, wb_im, wc_re, wc_im, pw_re, pw_im, up_re, up_im, dvec)


def s5_bwd(proj, wb_re, wb_im, wc_re, wc_im, pw_re, pw_im, dn_re, dn_im, dvec, xr_all, xi_all, dy, bsz, seq):
    t = proj.shape[0]
    n5 = seq // S5_CHUNK
    nsteps = int(math.log2(S5_CHUNK))
    in_specs, tok, qcols = _s5_specs(n5, True)
    blocks = S5_CHUNK // HALO

    def prev_rows(q, b, c):
        return (jnp.maximum(tok(q, b, c) * blocks - 1, 0), q)

    in_specs += [pl.BlockSpec((S5_CHUNK, qcols), lambda q, b, c: (tok(q, b, c), q)),
                 pl.BlockSpec((S5_CHUNK, qcols), lambda q, b, c: (tok(q, b, c), q)),
                 pl.BlockSpec((HALO, qcols), prev_rows), pl.BlockSpec((HALO, qcols), prev_rows),
                 pl.BlockSpec((S5_CHUNK, LANE), lambda q, b, c: (tok(q, b, c), q))]

    def body(u_ref, wbr, wbi, wcr, wci, pwr, pwi, dnr, dni, d_ref, xr_ref, xi_ref, pr_ref, pi_ref, dy_ref,
             du_ref, dwbr, dwbi, dwcr, dwci, dar, dai, dd_ref, gr_scr, gi_scr):
        b, c = pl.program_id(1), pl.program_id(2)

        @pl.when(c == 0)
        def _():
            gr_scr[...] = jnp.zeros_like(gr_scr)
            gi_scr[...] = jnp.zeros_like(gi_scr)

        @pl.when((b == 0) & (c == 0))
        def _():
            for r in (dwbr, dwbi, dwcr, dwci, dar, dai, dd_ref):
                r[...] = jnp.zeros_like(r)

        u, dy_v = u_ref[...], dy_ref[...]
        gr, gi = _dg(dy_v, wcr[0], 1, 1), -_dg(dy_v, wci[0], 1, 1)
        row = lax.broadcasted_iota(jnp.int32, (S5_CHUNK, 1), 0)
        for s in range(nsteps):
            k = 1 << s
            ar, ai = pwr[s:s + 1, :], pwi[s:s + 1, :]
            sr = jnp.where(row < S5_CHUNK - k, pltpu.roll(gr, S5_CHUNK - k, 0), 0.0)
            si = jnp.where(row < S5_CHUNK - k, pltpu.roll(gi, S5_CHUNK - k, 0), 0.0)
            gr, gi = gr + ar * sr + ai * si, gi + ar * si - ai * sr
        cr, ci = gr_scr[...], gi_scr[...]
        pr, pi = dnr[...], dni[...]
        gr, gi = gr + pr * cr + pi * ci, gi + pr * ci - pi * cr
        gr_scr[...], gi_scr[...] = gr[:1, :], gi[:1, :]

        xr, xi = xr_ref[...], xi_ref[...]
        is_first = (c == n5 - 1)
        hr = jnp.where(is_first, 0.0, pr_ref[...][HALO - 1:, :])
        hi = jnp.where(is_first, 0.0, pi_ref[...][HALO - 1:, :])
        xpr = jnp.where(row >= 1, pltpu.roll(xr, 1, 0), hr)
        xpi = jnp.where(row >= 1, pltpu.roll(xi, 1, 0), hi)
        dar[0] += jnp.sum(xpr * gr + xpi * gi, axis=0, keepdims=True)
        dai[0] += jnp.sum(xpr * gi - xpi * gr, axis=0, keepdims=True)
        du_ref[...] = _dg(gr, wbr[0], 1, 1) + _dg(gi, wbi[0], 1, 1) + dy_v * d_ref[0]
        dwbr[0] += _dg(u, gr, 0, 0)
        dwbi[0] += _dg(u, gi, 0, 0)
        dwcr[0] += _dg(xr, dy_v, 0, 0)
        dwci[0] -= _dg(xi, dy_v, 0, 0)
        dd_ref[0] += jnp.sum(dy_v * u, axis=0, keepdims=True)

    def acc(shape):
        return pl.BlockSpec((1,) + shape, lambda q, b, c: (q, 0, 0))

    return pl.pallas_call(
        body, name="s5_bwd", grid=(S5_Q, bsz, n5), in_specs=in_specs,
        out_specs=[pl.BlockSpec((S5_CHUNK, LANE), lambda q, b, c: (tok(q, b, c), q)),
                   acc((LANE, qcols)), acc((LANE, qcols)), acc((qcols, LANE)), acc((qcols, LANE)),
                   acc((1, qcols)), acc((1, qcols)), acc((1, LANE))],
        out_shape=[jax.ShapeDtypeStruct((t, S5_WIDTH), F32),
                   jax.ShapeDtypeStruct((S5_Q, LANE, qcols), F32), jax.ShapeDtypeStruct((S5_Q, LANE, qcols), F32),
                   jax.ShapeDtypeStruct((S5_Q, qcols, LANE), F32), jax.ShapeDtypeStruct((S5_Q, qcols, LANE), F32),
                   jax.ShapeDtypeStruct((S5_Q, 1, qcols), F32), jax.ShapeDtypeStruct((S5_Q, 1, qcols), F32),
                   jax.ShapeDtypeStruct((S5_Q, 1, LANE), F32)],
        scratch_shapes=[pltpu.VMEM((1, qcols), F32)] * 2,
        compiler_params=_cp("arbitrary", "arbitrary", "arbitrary"),
    )(proj, wb_re, wb_im, wc_re, wc_im, pw_re, pw_im, dn_re, dn_im, dvec, xr_all, xi_all, xr_all, xi_all, dy)


def _blockdiag_b(bb):
    b4 = bb.reshape(S5_Q, 8, S5_STATE, S5_GROUP_CH)
    eye = jnp.eye(8, dtype=bb.dtype)
    w = jnp.einsum("qgph,gk->qghkp", b4, eye)
    return w.reshape(S5_Q, LANE, S5_COLS // S5_Q)


def _unblock_b(dw):
    d = dw.reshape(S5_Q, 8, S5_GROUP_CH, 8, S5_STATE)
    d = jnp.einsum("qghgp->qgph", d)
    return d.reshape(S5_COLS, S5_GROUP_CH)


def _blockdiag_c(cc):
    c4 = cc.reshape(S5_Q, 8, S5_GROUP_CH, S5_STATE)
    eye = jnp.eye(8, dtype=cc.dtype)
    w = jnp.einsum("qghp,gk->qgpkh", c4, eye)
    return w.reshape(S5_Q, S5_COLS // S5_Q, LANE)


def _unblock_c(dw):
    d = dw.reshape(S5_Q, 8, S5_STATE, 8, S5_GROUP_CH)
    d = jnp.einsum("qgpgh->qghp", d)
    return d.reshape(S5_GROUPS, S5_GROUP_CH, S5_STATE)


def ada_fwd(c_all, w_loc, b_loc):
    def body(c_ref, w_ref, b_ref, o_ref):
        o_ref[...] = _dg(_silu(c_ref[...]), w_ref[...], 1, 0) + b_ref[...]

    return pl.pallas_call(body, name="ada_fwd",
                          out_shape=jax.ShapeDtypeStruct((c_all.shape[0], w_loc.shape[1]), F32),
                          compiler_params=_cp())(c_all, w_loc, b_loc)


def ada_bwd(c_all, dmod_all, dmod_cols):
    def body(c_ref, da_ref, dc_ref, gb_ref, gw_ref):
        gb_ref[...] = jnp.sum(da_ref[...], axis=0, keepdims=True)
        gw_ref[...] = _dg(_silu(c_ref[...]), dc_ref[...], 0, 0)

    return pl.pallas_call(body, name="ada_bwd",
                          out_shape=[jax.ShapeDtypeStruct((1, dmod_all.shape[1]), F32),
                                     jax.ShapeDtypeStruct((c_all.shape[1], dmod_cols.shape[1]), F32)],
                          compiler_params=_cp())(c_all, dmod_all, dmod_cols)


_FLIPS = [(0, 0, 1), (1, 0, 0), (0, 1, 0), (1, 1, 0), (1, 0, 1), (0, 1, 1), (1, 1, 1)]


def exchange(name, arrs, gather):
    n = len(arrs)

    def body(*refs):
        srcs, outs = refs[:n], refs[n:2 * n]
        send_sems, recv_sems, loc_sems = refs[2 * n:]
        x, y, c = lax.axis_index("x"), lax.axis_index("y"), lax.axis_index("c")
        me = 4 * x + 2 * y + c
        peers = []
        for fx, fy, fc in _FLIPS:
            px, py, pc = (1 - x if fx else x), (1 - y if fy else y), (1 - c if fc else c)
            peers.append(((px, py, pc), 4 * px + 2 * py + pc))

        def copy(k, j, slot_src, slot_dst):
            src = srcs[k] if gather[k] else srcs[k].at[slot_src]
            return pltpu.make_async_remote_copy(src_ref=src, dst_ref=outs[k].at[slot_dst],
                                                send_sem=send_sems.at[k, j], recv_sem=recv_sems.at[k, j],
                                                device_id=peers[j][0], device_id_type=MESH)

        sends, locs = [], []
        for k in range(n):
            for j in range(N_DEV - 1):
                cp = copy(k, j, peers[j][1], me)
                cp.start()
                sends.append(cp)
            own = srcs[k] if gather[k] else srcs[k].at[me]
            lc = pltpu.make_async_copy(own, outs[k].at[me], loc_sems.at[k])
            lc.start()
            locs.append(lc)
        for k in range(n):
            for j in range(N_DEV - 1):
                copy(k, j, me, peers[j][1]).wait_recv()
        for cp in sends:
            cp.wait_send()
        for lc in locs:
            lc.wait()

    any_spec = pl.BlockSpec(memory_space=pl.ANY)
    shapes = [jax.ShapeDtypeStruct(((N_DEV,) + a.shape) if g else a.shape, a.dtype) for a, g in zip(arrs, gather)]
    return pl.pallas_call(
        body, name=name, in_specs=[any_spec] * n, out_specs=[any_spec] * n, out_shape=shapes,
        scratch_shapes=[pltpu.SemaphoreType.DMA((n, N_DEV - 1)), pltpu.SemaphoreType.DMA((n, N_DEV - 1)),
                        pltpu.SemaphoreType.DMA((n,))],
        compiler_params=pltpu.CompilerParams(has_side_effects=True),
    )(*arrs)


def adamw(name, g, w, m, v, tr):
    slots = g.ndim == 3
    r, c = w.shape
    c1, c2 = 1.0 - ADAM_B1 ** ADAM_STEP, 1.0 - ADAM_B2 ** ADAM_STEP

    def body(g_ref, w_ref, m_ref, v_ref, go, do, mo, vo):
        if slots:
            gg = g_ref[0].astype(F32)
            for j in range(1, N_DEV):
                gg = gg + g_ref[j].astype(F32)
        else:
            gg = g_ref[...]
        mn = ADAM_B1 * m_ref[...] + (1.0 - ADAM_B1) * gg
        vn = ADAM_B2 * v_ref[...] + (1.0 - ADAM_B2) * (gg * gg)
        go[...], mo[...], vo[...] = gg, mn, vn
        do[...] = -ADAM_LR * ((mn / c1) / (jnp.sqrt(vn / c2) + ADAM_EPS) + ADAM_WD * w_ref[...])

    blk = pl.BlockSpec((tr, c), lambda i: (i, 0))
    gspec = pl.BlockSpec((N_DEV, tr, c), lambda i: (0, i, 0)) if slots else blk
    return pl.pallas_call(
        body, name=name, grid=(r // tr,), in_specs=[gspec, blk, blk, blk], out_specs=[blk] * 4,
        out_shape=[jax.ShapeDtypeStruct((r, c), F32)] * 4, compiler_params=_cp("parallel"),
    )(g, w, m, v)


def _pack(arrs, rows):
    flat = jnp.concatenate([a.reshape(-1).astype(F32) for a in arrs])
    return jnp.pad(flat, (0, rows * LANE - flat.shape[0])).reshape(rows, LANE)


def _unpack(buf, shapes):
    flat, out, off = buf.reshape(-1), [], 0
    for s in shapes:
        n = math.prod(s)
        out.append(flat[off:off + n].reshape(s))
        off += n
    return out


def _cols_to_full(g):
    return jnp.transpose(g, (1, 0, 2)).reshape(g.shape[1], N_DEV * g.shape[2])


def _full_to_cols(w):
    r, c = w.shape
    return jnp.transpose(w.reshape(r, N_DEV, c // N_DEV), (1, 0, 2))


def _ffn_fwd(tag, x, sc, sh, g, w1, w3, w2, lg, lb, seq, tm):
    h, = rowwise_fwd(tag + "_mod", f_modulate, [x], [sc, sh], [], [(D_MODEL, BF16)], seq, tm)
    a = mm_nn(tag + "_a", [h], [w1], tm, D_FF // 2)
    b = mm_nn(tag + "_b", [h], [w3], tm, D_FF // 2)
    s, = rowwise_fwd(tag + "_gate", f_gate, [a, b], [], [], [(D_FF, BF16)], seq, tm // 2)
    f = mm_nn(tag + "_f", [s], [w2], tm, D_MODEL)
    y, = rowwise_fwd(tag + "_ln", _res_ln(0.5), [x, f], [g], [lg, lb], [(D_MODEL, F32)], seq, tm)
    return y, (h, a, b, s, f)


def _ffn_bwd(tag, dy, x, sc, sh, g, w1, w3, w2, lg, lb, res, seq, tm):
    h, a, b, s, f = res
    (dx_a, df), (dg,), (dlg, dlb) = rowwise_bwd(tag + "_ln_b", _res_ln(0.5), [x, f], [g], [lg, lb], [dy], seq, tm,
                                                [F32, BF16])
    ds = mm_nt(tag + "_ds", [df], [w2], tm, D_FF // 2)
    dw2 = mm_tn(tag + "_dw2", s, df, D_FF // 2, D_MODEL, tm)
    (da, db), _, _ = rowwise_bwd(tag + "_gate_b", f_gate, [a, b], [], [], [ds], seq, tm // 2, [BF16, BF16])
    dh = mm_nt(tag + "_dh", [da, db], [w1, w3], tm // 2, D_MODEL)
    dw1 = mm_tn(tag + "_dw1", h, da, D_MODEL, D_FF // 2, tm)
    dw3 = mm_tn(tag + "_dw3", h, db, D_MODEL, D_FF // 2, tm)
    (dx,), (dsc, dsh), _ = rowwise_bwd(tag + "_mod_b", f_modulate, [x], [sc, sh], [], [dh], seq, tm, [F32],
                                       add_rows={0: dx_a})
    return dx, (dsh, dsc, dg), (dw1, dw3, dw2, dlg, dlb)


def kernel(x, c, w_ada, b_ada, ffn1_w1, ffn1_w3, ffn1_w2, ln1_g, ln1_b, w_in, conv_w, conv_b, dt_bias, a_log, d_ssd, ssd_norm_w, s5_a_re, s5_a_im, s5_log_dt, s5_b_re, s5_b_im, s5_c_re, s5_c_im, s5_d, w_glu, b_glu, w_out, ln2_g, ln2_b, ffn2_w1, ffn2_w3, ffn2_w2, ln3_g, ln3_b, loss_target, m_w_ada, m_b_ada, m_ffn1_w1, m_ffn1_w3, m_ffn1_w2, m_ln1_g, m_ln1_b, m_w_in, m_conv_w, m_conv_b, m_dt_bias, m_a_log, m_d_ssd, m_ssd_norm_w, m_s5_a_re, m_s5_a_im, m_s5_log_dt, m_s5_b_re, m_s5_b_im, m_s5_c_re, m_s5_c_im, m_s5_d, m_w_glu, m_b_glu, m_w_out, m_ln2_g, m_ln2_b, m_ffn2_w1, m_ffn2_w3, m_ffn2_w2, m_ln3_g, m_ln3_b, v_w_ada, v_b_ada, v_ffn1_w1, v_ffn1_w3, v_ffn1_w2, v_ln1_g, v_ln1_b, v_w_in, v_conv_w, v_conv_b, v_dt_bias, v_a_log, v_d_ssd, v_ssd_norm_w, v_s5_a_re, v_s5_a_im, v_s5_log_dt, v_s5_b_re, v_s5_b_im, v_s5_c_re, v_s5_c_im, v_s5_d, v_w_glu, v_b_glu, v_w_out, v_ln2_g, v_ln2_b, v_ffn2_w1, v_ffn2_w3, v_ffn2_w2, v_ln3_g, v_ln3_b):
    given = dict(locals())
    bsz, seq, _ = x.shape
    t = bsz * seq
    tm = min(512, seq)
    me = 4 * lax.axis_index("x") + 2 * lax.axis_index("y") + lax.axis_index("c")
    x0 = x.reshape(t, D_MODEL)
    target = loss_target.reshape(t, D_MODEL)

    col_stack = jnp.stack([ffn1_w1[0], ffn1_w3[0], ffn2_w1[0], ffn2_w3[0]]).astype(BF16)
    row_stack = jnp.stack([ffn1_w2[0], ffn2_w2[0]]).astype(BF16)
    g_col, g_row, g_win, g_glu, g_out, g_conv, g_c = exchange(
        "gather_weights",
        [col_stack, row_stack, w_in[0].astype(BF16), w_glu[0].astype(BF16), w_out[0].astype(BF16), conv_w[0], c],
        [True] * 7)
    f1w1, f1w3, f2w1, f2w3 = [_cols_to_full(g_col[:, k]) for k in range(4)]
    f1w2, f2w2 = [g_row[:, k].reshape(D_FF, D_MODEL) for k in range(2)]
    win = _cols_to_full(g_win)
    wglu = g_glu.reshape(S5_WIDTH, S5_WIDTH).astype(F32)
    wout = g_out.reshape(D_MODEL, D_MODEL)
    wo_ssd, wo_s5 = wout[:SSD_WIDTH], wout[SSD_WIDTH:]
    convw = jnp.transpose(g_conv, (1, 0, 2)).reshape(CONV_K, CONV_CH)
    c_all = g_c.reshape(N_DEV * bsz, D_MODEL)
    w_z, w_xbc = win[:, :SSD_WIDTH], win[:, SSD_WIDTH:SSD_WIDTH + CONV_CH]
    w_dt = win[:, SSD_WIDTH + CONV_CH:SSD_WIDTH + CONV_CH + SSD_HEADS]
    w_u = win[:, SSD_WIDTH + CONV_CH + SSD_HEADS:]
    dt_pad = [jnp.pad(w_dt[:, HEADS_PER_GROUP * g:HEADS_PER_GROUP * (g + 1)], ((0, 0), (0, LANE - HEADS_PER_GROUP)))
              for g in range(SSD_GROUPS)]
    w_dtp = jnp.concatenate(dt_pad, axis=1)
    w_proj = jnp.concatenate([w_xbc, w_z, w_u, w_dtp], axis=1)

    n_loc = w_ada.shape[2]
    b_loc = lax.dynamic_slice(b_ada, (0, me * n_loc), (1, n_loc))
    mod_cols = ada_fwd(c_all, w_ada[0], b_loc)
    g_mod, = exchange("gather_mod", [mod_cols], [True])
    mine = lax.dynamic_slice(g_mod, (0, me * bsz, 0), (N_DEV, bsz, n_loc))
    mod = jnp.transpose(mine, (1, 0, 2)).reshape(bsz, N_MOD, 1, D_MODEL)
    sh1, sc1, g1, sh2, sc2, g2, sh3, sc3, g3 = [mod[:, k] for k in range(N_MOD)]

    x1, res1 = _ffn_fwd("ffn1", x0, sc1, sh1, g1, f1w1, f1w3, f1w2, ln1_g, ln1_b, seq, tm)

    h2, = rowwise_fwd("mix_mod", f_modulate, [x1], [sc2, sh2], [], [(D_MODEL, BF16)], seq, tm)
    proj = mm_nn("mix_proj", [h2], [w_proj], tm, P_COLS // 2)
    xc = conv_fwd(proj, convw, conv_b, seq, tm)
    dtb = jnp.pad(dt_bias.reshape(SSD_GROUPS, 1, HEADS_PER_GROUP), ((0, 0), (0, 0), (0, LANE - HEADS_PER_GROUP)))
    alog = jnp.pad(a_log.reshape(SSD_GROUPS, 1, HEADS_PER_GROUP), ((0, 0), (0, 0), (0, LANE - HEADS_PER_GROUP)))
    dcol = jnp.pad(d_ssd.reshape(SSD_GROUPS, HEADS_PER_GROUP, 1), ((0, 0), (0, LANE - HEADS_PER_GROUP), (0, 0)))
    nw = ssd_norm_w.reshape(SSD_GROUPS, 1, GROUP_COLS)
    y_ssd, hprev = ssd_fwd(xc, proj, dtb, alog, dcol, nw, bsz, seq)

    a_re2, a_im2, ldt2 = s5_a_re[0], s5_a_im[0], s5_log_dt.reshape(S5_GROUPS, 1)
    ab_re, ab_im, f_re, f_im = _whole(_disc_a, "s5_disc_a", [a_re2, a_im2, ldt2], [(S5_GROUPS, S5_STATE)] * 4)
    b_re2, b_im2 = s5_b_re.reshape(S5_COLS, S5_GROUP_CH), s5_b_im.reshape(S5_COLS, S5_GROUP_CH)
    fr_col, fi_col = f_re.reshape(S5_COLS, 1), f_im.reshape(S5_COLS, 1)
    bb_re, bb_im = _whole(_disc_b, "s5_disc_b", [fr_col, fi_col, b_re2, b_im2], [(S5_COLS, S5_GROUP_CH)] * 2)
    wb_re, wb_im = _blockdiag_b(bb_re).astype(BF16), _blockdiag_b(bb_im).astype(BF16)
    wc_re, wc_im = _blockdiag_c(s5_c_re[0]).astype(BF16), _blockdiag_c(s5_c_im[0]).astype(BF16)
    dt5 = jnp.exp(ldt2)
    lam_re, lam_im = (dt5 * a_re2).reshape(1, S5_COLS), (dt5 * a_im2).reshape(1, S5_COLS)
    pw_re, pw_im, up_re, up_im, dn_re, dn_im = s5_tables(lam_re, lam_im, S5_CHUNK)
    d5 = s5_d.reshape(S5_Q, 1, LANE)
    y5, xr_all, xi_all = s5_fwd(proj, wb_re, wb_im, wc_re, wc_im, pw_re, pw_im, up_re, up_im, d5, bsz, seq)
    o5, = rowwise_fwd("s5_glu", f_glu, [y5], [], [wglu, b_glu], [(S5_WIDTH, F32)], seq, tm)

    mix = mm_nn("mix_out", [y_ssd, o5], [wo_ssd, wo_s5], tm, D_MODEL)
    x2, = rowwise_fwd("mix_ln", _res_ln(1.0), [x1, mix], [g2], [ln2_g, ln2_b], [(D_MODEL, F32)], seq, tm)

    x3, res3 = _ffn_fwd("ffn2", x2, sc3, sh3, g3, f2w1, f2w3, f2w2, ln3_g, ln3_b, seq, tm)
    dy, loss_loc = loss_head(x3, target, tm)

    dx2, dmod3, (d_f2w1, d_f2w3, d_f2w2, d_ln3g, d_ln3b) = _ffn_bwd(
        "ffn2", dy, x2, sc3, sh3, g3, f2w1, f2w3, f2w2, ln3_g, ln3_b, res3, seq, tm)

    (dx1_a, dmix), (dg2,), (d_ln2g, d_ln2b) = rowwise_bwd(
        "mix_ln_b", _res_ln(1.0), [x1, mix], [g2], [ln2_g, ln2_b], [dx2], seq, tm, [F32, BF16])
    d_wo = jnp.concatenate([mm_tn("mix_dwo_ssd", y_ssd, dmix, SSD_WIDTH, D_MODEL, tm),
                            mm_tn("mix_dwo_s5", o5, dmix, S5_WIDTH, D_MODEL, tm)], axis=0)
    dy_ssd = mm_nt("mix_dy_ssd", [dmix], [wo_ssd], tm, SSD_WIDTH)
    do5 = mm_nt("mix_do5", [dmix], [wo_s5], tm, S5_WIDTH)

    (dy5,), _, (d_wglu, d_bglu) = rowwise_bwd("s5_glu_b", f_glu, [y5], [], [wglu, b_glu], [do5], seq, tm, [F32])
    du, dwbr, dwbi, dwcr, dwci, dab_re, dab_im, dd5 = s5_bwd(
        proj, wb_re, wb_im, wc_re, wc_im, pw_re, pw_im, dn_re, dn_im, d5, xr_all, xi_all, dy5, bsz, seq)
    dbb_re, dbb_im = _unblock_b(dwbr), _unblock_b(dwbi)
    dfr_col, dfi_col, d_b_re, d_b_im = _whole_vjp(_disc_b, "s5_disc_b_b", [fr_col, fi_col, b_re2, b_im2],
                                                  [dbb_re, dbb_im])
    d_a_re, d_a_im, d_ldt = _whole_vjp(
        _disc_a, "s5_disc_a_b", [a_re2, a_im2, ldt2],
        [dab_re.reshape(S5_GROUPS, S5_STATE), dab_im.reshape(S5_GROUPS, S5_STATE),
         dfr_col.reshape(S5_GROUPS, S5_STATE), dfi_col.reshape(S5_GROUPS, S5_STATE)])
    d_c_re, d_c_im = _unblock_c(dwcr), _unblock_c(dwci)

    dxs, dbm, dcm, ddt, dz, ddtb, dalog, ddcol, dnw = ssd_bwd(xc, proj, dtb, alog, dcol, nw, hprev, dy_ssd, bsz, seq)
    dpre, d_convw, d_convb = conv_bwd_pre(proj, convw, conv_b, dxs, dbm, dcm, seq, tm)
    dxbc = conv_bwd_x(dpre, convw, seq, tm)

    dh2 = mm_nt("mix_dh", [dxbc, dz, du, ddt], [w_xbc, w_z, w_u, w_dtp], tm, D_MODEL)
    dw_xbc = mm_tn("mix_dw_xbc", h2, dxbc, D_MODEL, CONV_CH, tm)
    dw_z = mm_tn("mix_dw_z", h2, dz, D_MODEL, SSD_WIDTH, tm)
    dw_u = mm_tn("mix_dw_u", h2, du, D_MODEL, S5_WIDTH, tm)
    dw_dt = mm_tn("mix_dw_dt", h2, ddt, D_MODEL, 2 * LANE, tm)
    dw_dt8 = jnp.concatenate([dw_dt[:, LANE * g:LANE * g + HEADS_PER_GROUP] for g in range(SSD_GROUPS)], axis=1)
    d_win = jnp.concatenate([dw_z, dw_xbc, dw_dt8, dw_u], axis=1)
    (dx1,), (dsc2, dsh2), _ = rowwise_bwd("mix_mod_b", f_modulate, [x1], [sc2, sh2], [], [dh2], seq, tm, [F32],
                                          add_rows={0: dx1_a})

    dx0, dmod1, (d_f1w1, d_f1w3, d_f1w2, d_ln1g, d_ln1b) = _ffn_bwd(
        "ffn1", dx1, x0, sc1, sh1, g1, f1w1, f1w3, f1w2, ln1_g, ln1_b, res1, seq, tm)

    dmod = jnp.concatenate(list(dmod1) + [dsh2, dsc2, dg2] + list(dmod3), axis=1).reshape(bsz, N_MOD * D_MODEL)
    small = {
        "ln1_g": d_ln1g, "ln1_b": d_ln1b, "conv_w": d_convw, "conv_b": d_convb,
        "dt_bias": ddtb[:, 0, :HEADS_PER_GROUP].reshape(1, SSD_HEADS),
        "a_log": dalog[:, 0, :HEADS_PER_GROUP].reshape(1, SSD_HEADS),
        "d_ssd": ddcol[:, :HEADS_PER_GROUP, 0].reshape(1, SSD_HEADS),
        "ssd_norm_w": dnw.reshape(1, SSD_WIDTH),
        "s5_a_re": d_a_re[None], "s5_a_im": d_a_im[None], "s5_log_dt": d_ldt.reshape(1, S5_GROUPS),
        "s5_b_re": d_b_re.reshape(s5_b_re.shape), "s5_b_im": d_b_im.reshape(s5_b_im.shape),
        "s5_c_re": d_c_re[None], "s5_c_im": d_c_im[None], "s5_d": dd5.reshape(1, S5_WIDTH),
        "b_glu": d_bglu, "ln2_g": d_ln2g, "ln2_b": d_ln2b, "ln3_g": d_ln3g, "ln3_b": d_ln3b,
    }
    names = list(small)
    n_small = sum(math.prod(small[k].shape) for k in names)
    rows = -(-n_small // (8 * LANE)) * 8
    packed = _pack([small[k] for k in names], rows)
    col_grads = jnp.stack([_full_to_cols(d) for d in (d_f1w1, d_f1w3, d_f2w1, d_f2w3)], axis=1).astype(BF16)
    row_grads = jnp.stack([d.reshape(N_DEV, D_FF // N_DEV, D_MODEL) for d in (d_f1w2, d_f2w2)], axis=1).astype(BF16)
    s_col, s_row, s_win, s_glu, s_out, s_small, s_dmod = exchange(
        "sum_grads",
        [col_grads, row_grads, _full_to_cols(d_win).astype(BF16),
         d_wglu.reshape(N_DEV, S5_WIDTH // N_DEV, S5_WIDTH).astype(BF16),
         d_wo.reshape(N_DEV, D_MODEL // N_DEV, D_MODEL).astype(BF16), packed, dmod],
        [False] * 5 + [True] * 2)

    out = {"loss": lax.psum(loss_loc, ("x", "y", "c")), "grad_x": dx0.reshape(x.shape)}

    def put(name, res, shape):
        for key, val in zip(("grad_", "delta_", "new_m_", "new_v_"), res):
            out[key + name] = val.reshape(shape)

    for k, name in enumerate(("ffn1_w1", "ffn1_w3", "ffn2_w1", "ffn2_w3")):
        w = given[name]
        put(name, adamw("adam_" + name, s_col[:, k], w[0], given["m_" + name][0], given["v_" + name][0], 256), w.shape)
    for k, name in enumerate(("ffn1_w2", "ffn2_w2")):
        w = given[name]
        put(name, adamw("adam_" + name, s_row[:, k], w[0], given["m_" + name][0], given["v_" + name][0], 176), w.shape)
    put("w_in", adamw("adam_w_in", s_win, w_in[0], m_w_in[0], v_w_in[0], 256), w_in.shape)
    put("w_glu", adamw("adam_w_glu", s_glu, w_glu[0], m_w_glu[0], v_w_glu[0], 64), w_glu.shape)
    put("w_out", adamw("adam_w_out", s_out, w_out[0], m_w_out[0], v_w_out[0], 128), w_out.shape)

    dmod_all = s_dmod.reshape(N_DEV * bsz, N_MOD * D_MODEL)
    g_bada, g_wada = ada_bwd(c_all, dmod_all, lax.dynamic_slice(dmod_all, (0, me * n_loc), (N_DEV * bsz, n_loc)))
    put("w_ada", adamw("adam_w_ada", g_wada, w_ada[0], m_w_ada[0], v_w_ada[0], 256), w_ada.shape)
    put("b_ada", adamw("adam_b_ada", g_bada, b_ada, m_b_ada, v_b_ada, 1), b_ada.shape)

    zero_cw = jnp.zeros((CONV_K, CONV_CH), F32)
    pw, pm, pv = [_pack([zero_cw if k == "conv_w" else given[pre + k] for k in names], rows) for pre in ("", "m_", "v_")]
    res_small = adamw("adam_small", s_small, pw, pm, pv, rows)
    parts = [_unpack(r, [small[k].shape for k in names]) for r in res_small]
    for i, k in enumerate(names):
        if k != "conv_w":
            put(k, [p[i] for p in parts], given[k].shape)
    g_cw = lax.dynamic_slice(parts[0][names.index("conv_w")], (0, me * LANE), (CONV_K, LANE))
    put("conv_w", adamw("adam_conv_w", g_cw, conv_w[0], m_conv_w[0], v_conv_w[0], CONV_K), conv_w.shape)

    order = ["w_ada", "b_ada", "ffn1_w1", "ffn1_w3", "ffn1_w2", "ln1_g", "ln1_b", "w_in", "conv_w", "conv_b", "dt_bias",
             "a_log", "d_ssd", "ssd_norm_w", "s5_a_re", "s5_a_im", "s5_log_dt", "s5_b_re", "s5_b_im", "s5_c_re",
             "s5_c_im", "s5_d", "w_glu", "b_glu", "w_out", "ln2_g", "ln2_b", "ffn2_w1", "ffn2_w3", "ffn2_w2", "ln3_g",
             "ln3_b"]
    return (out["loss"], out["grad_x"], *[out[p + n] for p in ("grad_", "delta_", "new_m_", "new_v_") for n in order])
```

```python
import functools
import math

import jax
import jax.numpy as jnp
from jax import lax
from jax.experimental import pallas as pl
from jax.experimental.pallas import tpu as pltpu

F32 = jnp.float32
BF16 = jnp.bfloat16
HI = lax.Precision.HIGHEST
MESH = pl.DeviceIdType.MESH

N_DEV = 8
D_MODEL = 1024
D_FF = 2816
N_MOD = 9
SSD_WIDTH = 512
SSD_HEADS = 8
SSD_HEAD_DIM = 64
SSD_GROUPS = 2
SSD_STATE = 128
SSD_CHUNK = 128
GROUP_COLS = SSD_WIDTH // SSD_GROUPS
HEADS_PER_GROUP = SSD_HEADS // SSD_GROUPS
CONV_K = 4
CONV_CH = 1024
S5_WIDTH = 512
S5_GROUPS = 32
S5_GROUP_CH = 16
S5_STATE = 64
S5_COLS = S5_GROUPS * S5_STATE
S5_Q = 4
S5_CHUNK = 256
ALPHA = 2.0 ** 0.25
LN_EPS = 1e-5
LANE = 128
HALO = 8

P_XBC, P_Z, P_U, P_DT = 0, 1024, 1536, 2048
P_COLS = 2048 + SSD_GROUPS * LANE

ADAM_LR, ADAM_B1, ADAM_B2, ADAM_EPS, ADAM_WD, ADAM_STEP = 0.001, 0.9, 0.999, 1e-08, 0.01, 10

VMEM_LIMIT = 56 * 1024 * 1024


def _cp(*sem):
    return pltpu.CompilerParams(dimension_semantics=sem if sem else None, vmem_limit_bytes=VMEM_LIMIT)


def _dg(a, b, ca, cb):
    return lax.dot_general(a.astype(BF16), b.astype(BF16), (((ca,), (cb,)), ((), ())), preferred_element_type=F32)


@jax.custom_vjp
def bdot_nn(a, b):
    return _dg(a, b, 1, 0)


bdot_nn.defvjp(lambda a, b: (_dg(a, b, 1, 0), (a, b)),
               lambda r, g: (_dg(g, r[1], 1, 1), _dg(r[0], g, 0, 0)))


@jax.custom_vjp
def bdot_nt(a, b):
    return _dg(a, b, 1, 1)


bdot_nt.defvjp(lambda a, b: (_dg(a, b, 1, 1), (a, b)),
               lambda r, g: (_dg(g, r[1], 1, 0), _dg(g, r[0], 0, 0)))


@jax.custom_vjp
def bdot_tn(a, b):
    return _dg(a, b, 0, 0)


bdot_tn.defvjp(lambda a, b: (_dg(a, b, 0, 0), (a, b)),
               lambda r, g: (_dg(r[1], g, 1, 1), _dg(r[0], g, 1, 0)))


def _take_col(z):
    @jax.custom_vjp
    def take(x):
        return x[:, z:z + 1]

    def bwd(shape, g):
        hot = (lax.broadcasted_iota(jnp.int32, (1, shape[1]), 1) == z).astype(F32)
        return (g * hot,)

    take.defvjp(lambda x: (x[:, z:z + 1], x.shape), bwd)
    return take


def _take_row(z):
    @jax.custom_vjp
    def take(x):
        return x[z:z + 1, :]

    def bwd(shape, g):
        hot = (lax.broadcasted_iota(jnp.int32, (shape[0], 1), 0) == z).astype(F32)
        return (hot * g,)

    take.defvjp(lambda x: (x[z:z + 1, :], x.shape), bwd)
    return take


def _view(a):
    return a if isinstance(a, tuple) else (a, 0, a.shape[1])


def _col_spec(view, rows, width, index):
    _, off, _ = view
    assert off % width == 0
    return pl.BlockSpec((rows, width), lambda *g: (index(*g)[0], off // width + index(*g)[1]))


def _rw_in_specs(rows, bps, gps, tm, tps):
    specs = [_col_spec(_view(r), tm, _view(r)[2], lambda i: (i, 0)) for r in rows]
    specs += [pl.BlockSpec((1, 1, b.shape[2]), lambda i: (i // tps, 0, 0)) for b in bps]
    specs += [pl.BlockSpec(g.shape, lambda i, nd=g.ndim: (0,) * nd) for g in gps]
    return specs


def _rw_vals(refs, nr, nb, ng):
    vals = [r[...] for r in refs[:nr]]
    vals += [b[0] for b in refs[nr:nr + nb]]
    vals += [g[...] for g in refs[nr + nb:nr + nb + ng]]
    return vals


def rowwise_fwd(name, f, rows, bps, gps, outs, seq, tm):
    t = _view(rows[0])[0].shape[0]
    tps = seq // tm
    nr, nb, ng = len(rows), len(bps), len(gps)

    def body(*refs):
        res = f(*_rw_vals(refs, nr, nb, ng))
        for o, v in zip(refs[nr + nb + ng:], res):
            o[...] = v.astype(o.dtype)

    return pl.pallas_call(
        body, name=name, grid=(t // tm,),
        in_specs=_rw_in_specs(rows, bps, gps, tm, tps),
        out_specs=[pl.BlockSpec((tm, c), lambda i: (i, 0)) for c, _ in outs],
        out_shape=[jax.ShapeDtypeStruct((t, c), d) for c, d in outs],
        compiler_params=_cp("arbitrary"),
    )(*[_view(r)[0] for r in rows], *bps, *gps)


def rowwise_bwd(name, f, rows, bps, gps, douts, seq, tm, row_grads, add_rows=None):
    add_rows = add_rows or {}
    t = _view(rows[0])[0].shape[0]
    tps = seq // tm
    nr, nb, ng, nd = len(rows), len(bps), len(gps), len(douts)
    want = [k for k in range(nr) if row_grads[k] is not None]
    adds = sorted(add_rows)
    n_in = nr + nb + ng + nd + len(adds)

    def body(*refs):
        vals = _rw_vals(refs, nr, nb, ng)
        dvals = tuple(r[...] for r in refs[nr + nb + ng:nr + nb + ng + nd])
        add_refs = dict(zip(adds, refs[nr + nb + ng + nd:n_in]))
        out_refs = refs[n_in:]
        _, pull = jax.vjp(f, *vals)
        grads = pull(dvals)
        i = pl.program_id(0)
        for o, k in zip(out_refs, want):
            g = grads[k]
            if k in add_refs:
                g = g + add_refs[k][...]
            o[...] = g.astype(o.dtype)
        for j in range(nb):
            o = out_refs[len(want) + j]

            @pl.when(i % tps == 0)
            def _(o=o):
                o[...] = jnp.zeros_like(o)

            o[0] = o[0] + grads[nr + j]
        for j in range(ng):
            o = out_refs[len(want) + nb + j]

            @pl.when(i == 0)
            def _(o=o):
                o[...] = jnp.zeros_like(o)

            o[...] = o[...] + grads[nr + nb + j]

    in_specs = _rw_in_specs(rows, bps, gps, tm, tps)
    in_specs += [pl.BlockSpec((tm, d.shape[1]), lambda i: (i, 0)) for d in douts]
    in_specs += [pl.BlockSpec((tm, add_rows[k].shape[1]), lambda i: (i, 0)) for k in adds]
    out_specs = [pl.BlockSpec((tm, _view(rows[k])[2]), lambda i: (i, 0)) for k in want]
    out_shape = [jax.ShapeDtypeStruct((t, _view(rows[k])[2]), row_grads[k]) for k in want]
    out_specs += [pl.BlockSpec((1, 1, b.shape[2]), lambda i: (i // tps, 0, 0)) for b in bps]
    out_shape += [jax.ShapeDtypeStruct(b.shape, F32) for b in bps]
    out_specs += [pl.BlockSpec(g.shape, lambda i, n=g.ndim: (0,) * n) for g in gps]
    out_shape += [jax.ShapeDtypeStruct(g.shape, F32) for g in gps]
    res = pl.pallas_call(
        body, name=name, grid=(t // tm,), in_specs=in_specs, out_specs=out_specs, out_shape=out_shape,
        compiler_params=_cp("arbitrary"),
    )(*[_view(r)[0] for r in rows], *bps, *gps, *douts, *[add_rows[k] for k in adds])
    nw = len(want)
    return res[:nw], res[nw:nw + nb], res[nw + nb:]


def mm_nn(name, xs, ws, tm, tn, out_dtype=F32):
    views = [_view(x) for x in xs]
    t, n, k = views[0][0].shape[0], ws[0].shape[1], len(xs)

    def body(*refs):
        acc = _dg(refs[0][...], refs[k][...], 1, 0)
        for i in range(1, k):
            acc = acc + _dg(refs[i][...], refs[k + i][...], 1, 0)
        refs[2 * k][...] = acc.astype(out_dtype)

    in_specs = [_col_spec(v, tm, v[2], lambda i, j: (i, 0)) for v in views]
    in_specs += [pl.BlockSpec((w.shape[0], tn), lambda i, j: (0, j)) for w in ws]
    return pl.pallas_call(
        body, name=name, grid=(t // tm, n // tn), in_specs=in_specs,
        out_specs=pl.BlockSpec((tm, tn), lambda i, j: (i, j)),
        out_shape=jax.ShapeDtypeStruct((t, n), out_dtype),
        compiler_params=_cp("parallel", "parallel"),
    )(*[v[0] for v in views], *ws)


def mm_nt(name, dys, ws, tm, tk, out_dtype=F32):
    views = [_view(d) for d in dys]
    t, kk, k = views[0][0].shape[0], ws[0].shape[0], len(dys)

    def body(*refs):
        acc = _dg(refs[0][...], refs[k][...], 1, 1)
        for i in range(1, k):
            acc = acc + _dg(refs[i][...], refs[k + i][...], 1, 1)
        refs[2 * k][...] = acc.astype(out_dtype)

    in_specs = [_col_spec(v, tm, v[2], lambda i, j: (i, 0)) for v in views]
    in_specs += [pl.BlockSpec((tk, w.shape[1]), lambda i, j: (j, 0)) for w in ws]
    return pl.pallas_call(
        body, name=name, grid=(t // tm, kk // tk), in_specs=in_specs,
        out_specs=pl.BlockSpec((tm, tk), lambda i, j: (i, j)),
        out_shape=jax.ShapeDtypeStruct((t, kk), out_dtype),
        compiler_params=_cp("parallel", "parallel"),
    )(*[v[0] for v in views], *ws)


def mm_tn(name, x, dy, tk, tn, tt, out_dtype=F32):
    xv, dv = _view(x), _view(dy)
    t, kk, n = xv[0].shape[0], xv[2], dv[2]
    steps = t // tt

    def body(x_ref, d_ref, o_ref, acc_ref):
        @pl.when(pl.program_id(2) == 0)
        def _():
            acc_ref[...] = jnp.zeros_like(acc_ref)

        acc_ref[...] += _dg(x_ref[...], d_ref[...], 0, 0)

        @pl.when(pl.program_id(2) == steps - 1)
        def _():
            o_ref[...] = acc_ref[...].astype(out_dtype)

    return pl.pallas_call(
        body, name=name, grid=(kk // tk, n // tn, steps),
        in_specs=[_col_spec(xv, tt, tk, lambda a, b, c: (c, a)), _col_spec(dv, tt, tn, lambda a, b, c: (c, b))],
        out_specs=pl.BlockSpec((tk, tn), lambda a, b, c: (a, b)),
        out_shape=jax.ShapeDtypeStruct((kk, n), out_dtype),
        scratch_shapes=[pltpu.VMEM((tk, tn), F32)],
        compiler_params=_cp("parallel", "parallel", "arbitrary"),
    )(xv[0], dv[0])


def _silu(x):
    return x * jax.nn.sigmoid(x)


def f_modulate(x, sc, sh):
    return (x * (1.0 + sc) + sh,)


def f_gate(a, b):
    return (_silu(a) * b,)


def _res_ln(coef):
    def f(x, y, g, lg, lb):
        r = ALPHA * x + (coef * g) * y
        mu = jnp.mean(r, axis=-1, keepdims=True)
        d = r - mu
        var = jnp.mean(d * d, axis=-1, keepdims=True)
        return (d * lax.rsqrt(var + LN_EPS) * lg + lb,)
    return f


def f_glu(y, w, b):
    g = jax.nn.gelu(y)
    return (g * jax.nn.sigmoid(bdot_nn(g, w) + b),)


def loss_head(y, target, tm):
    t, d = y.shape

    def body(y_ref, t_ref, dy_ref, l_ref):
        @pl.when(pl.program_id(0) == 0)
        def _():
            l_ref[...] = jnp.zeros_like(l_ref)

        e = y_ref[...] - t_ref[...]
        dy_ref[...] = e * (1.0 / d)
        l_ref[...] += 0.5 * jnp.sum(jnp.mean(e * e, axis=-1, keepdims=True), axis=0, keepdims=True)

    dy, l = pl.pallas_call(
        body, name="loss_head", grid=(t // tm,),
        in_specs=[pl.BlockSpec((tm, d), lambda i: (i, 0))] * 2,
        out_specs=[pl.BlockSpec((tm, d), lambda i: (i, 0)), pl.BlockSpec((1, 1), lambda i: (0, 0))],
        out_shape=[jax.ShapeDtypeStruct((t, d), F32), jax.ShapeDtypeStruct((1, 1), F32)],
        compiler_params=_cp("arbitrary"),
    )(y, target)
    return dy, l[0, 0]


def _shift_down(x, halo, k):
    if k == 0:
        return x
    r = pltpu.roll(x, k, 0)
    hr = pltpu.roll(halo, k, 0)
    row = lax.broadcasted_iota(jnp.int32, (HALO, 1), 0)
    top = jnp.where(row < k, hr, r[:HALO])
    return jnp.concatenate([top, r[HALO:]], axis=0)


def _shift_up(x, halo, k):
    if k == 0:
        return x
    n = x.shape[0]
    r = pltpu.roll(x, n - k, 0)
    hr = pltpu.roll(halo, HALO - k, 0)
    row = lax.broadcasted_iota(jnp.int32, (HALO, 1), 0)
    bot = jnp.where(row >= HALO - k, hr, r[n - HALO:])
    return jnp.concatenate([r[:n - HALO], bot], axis=0)


def _conv_pre(x, halo, w, b):
    acc = x * w[CONV_K - 1:CONV_K, :] + b
    for k in range(1, CONV_K):
        acc = acc + _shift_down(x, halo, k) * w[CONV_K - 1 - k:CONV_K - k, :]
    return acc


def _rows_before(width, tm):
    return pl.BlockSpec((HALO, width), lambda i: (jnp.maximum(i * (tm // HALO) - 1, 0), 0))


def conv_fwd(proj, w, b, seq, tm):
    t = proj.shape[0]
    tps = seq // tm

    def body(x_ref, h_ref, w_ref, b_ref, o_ref):
        first = (pl.program_id(0) % tps == 0)
        halo = jnp.where(first, 0.0, h_ref[...])
        o_ref[...] = _silu(_conv_pre(x_ref[...], halo, w_ref[...], b_ref[...]))

    return pl.pallas_call(
        body, name="conv_fwd", grid=(t // tm,),
        in_specs=[pl.BlockSpec((tm, CONV_CH), lambda i: (i, 0)), _rows_before(CONV_CH, tm),
                  pl.BlockSpec((CONV_K, CONV_CH), lambda i: (0, 0)), pl.BlockSpec((1, CONV_CH), lambda i: (0, 0))],
        out_specs=pl.BlockSpec((tm, CONV_CH), lambda i: (i, 0)),
        out_shape=jax.ShapeDtypeStruct((t, CONV_CH), F32),
        compiler_params=_cp("arbitrary"),
    )(proj, proj, w, b)


def conv_bwd_pre(proj, w, b, dxs, dbm, dcm, seq, tm):
    t = proj.shape[0]
    tps = seq // tm

    def body(x_ref, h_ref, w_ref, b_ref, d1, d2, d3, dp_ref, dw_ref, db_ref):
        i = pl.program_id(0)
        halo = jnp.where(i % tps == 0, 0.0, h_ref[...])
        x = x_ref[...]
        pre = _conv_pre(x, halo, w_ref[...], b_ref[...])
        sg = jax.nn.sigmoid(pre)
        dout = jnp.concatenate([d1[...], d2[...], d3[...]], axis=1)
        dp = dout * (sg * (1.0 + pre * (1.0 - sg)))
        dp_ref[...] = dp

        @pl.when(i == 0)
        def _():
            dw_ref[...] = jnp.zeros_like(dw_ref)
            db_ref[...] = jnp.zeros_like(db_ref)

        db_ref[...] += jnp.sum(dp, axis=0, keepdims=True)
        for k in range(CONV_K):
            j = CONV_K - 1 - k
            dw_ref[j:j + 1, :] += jnp.sum(dp * _shift_down(x, halo, k), axis=0, keepdims=True)

    return pl.pallas_call(
        body, name="conv_bwd_pre", grid=(t // tm,),
        in_specs=[pl.BlockSpec((tm, CONV_CH), lambda i: (i, 0)), _rows_before(CONV_CH, tm),
                  pl.BlockSpec((CONV_K, CONV_CH), lambda i: (0, 0)), pl.BlockSpec((1, CONV_CH), lambda i: (0, 0)),
                  pl.BlockSpec((tm, 512), lambda i: (i, 0)), pl.BlockSpec((tm, 256), lambda i: (i, 0)),
                  pl.BlockSpec((tm, 256), lambda i: (i, 0))],
        out_specs=[pl.BlockSpec((tm, CONV_CH), lambda i: (i, 0)), pl.BlockSpec((CONV_K, CONV_CH), lambda i: (0, 0)),
                   pl.BlockSpec((1, CONV_CH), lambda i: (0, 0))],
        out_shape=[jax.ShapeDtypeStruct((t, CONV_CH), F32), jax.ShapeDtypeStruct((CONV_K, CONV_CH), F32),
                   jax.ShapeDtypeStruct((1, CONV_CH), F32)],
        compiler_params=_cp("arbitrary"),
    )(proj, proj, w, b, dxs, dbm, dcm)


def conv_bwd_x(dpre, w, seq, tm):
    t = dpre.shape[0]
    tps = seq // tm
    blocks = tm // HALO
    last = t // HALO - 1

    def body(d_ref, h_ref, w_ref, o_ref):
        halo = jnp.where(pl.program_id(0) % tps == tps - 1, 0.0, h_ref[...])
        d = d_ref[...]
        w = w_ref[...]
        acc = d * w[CONV_K - 1:CONV_K, :]
        for k in range(1, CONV_K):
            acc = acc + _shift_up(d, halo, k) * w[CONV_K - 1 - k:CONV_K - k, :]
        o_ref[...] = acc

    return pl.pallas_call(
        body, name="conv_bwd_x", grid=(t // tm,),
        in_specs=[pl.BlockSpec((tm, CONV_CH), lambda i: (i, 0)),
                  pl.BlockSpec((HALO, CONV_CH), lambda i: (jnp.minimum((i + 1) * blocks, last), 0)),
                  pl.BlockSpec((CONV_K, CONV_CH), lambda i: (0, 0))],
        out_specs=pl.BlockSpec((tm, CONV_CH), lambda i: (i, 0)),
        out_shape=jax.ShapeDtypeStruct((t, CONV_CH), F32),
        compiler_params=_cp("arbitrary"),
    )(dpre, dpre, w)


def _softplus(x):
    return jnp.maximum(x, 0.0) + jnp.log1p(jnp.exp(-jnp.abs(x)))


def _ssd_chunk(xs, bg, cg, dtr, zz, hp, dtb, alog, dcol, nw):
    l = xs.shape[0]
    row = lax.broadcasted_iota(jnp.int32, (l, l), 0)
    col = lax.broadcasted_iota(jnp.int32, (l, l), 1)
    causal = row >= col
    tril = causal.astype(F32)
    expand = (lax.broadcasted_iota(jnp.int32, (LANE, GROUP_COLS), 1) // SSD_HEAD_DIM
              == lax.broadcasted_iota(jnp.int32, (LANE, GROUP_COLS), 0)).astype(F32)
    head_of_col = lax.broadcasted_iota(jnp.int32, (1, GROUP_COLS), 1) // SSD_HEAD_DIM
    last_row = (lax.broadcasted_iota(jnp.int32, (l, 1), 0) == l - 1).astype(F32)

    dtc = _softplus(dtr + dtb)
    a_c = dtc * (-jnp.exp(alog))
    acs_c = jnp.dot(tril, a_c, precision=HI, preferred_element_type=F32)
    dt_e = jnp.dot(dtc, expand, precision=HI, preferred_element_type=F32)
    acs_e = jnp.dot(acs_c, expand, precision=HI, preferred_element_type=F32)
    alast_e = jnp.sum(acs_e * last_row, axis=0, keepdims=True)
    x = xs * dt_e
    states = bdot_tn(bg, x * jnp.exp(alast_e - acs_e))
    h_next = jnp.exp(alast_e) * hp + states
    d_e = jnp.sum(dcol * expand, axis=0, keepdims=True)
    y = bdot_nn(cg, hp) * jnp.exp(acs_e) + d_e * xs
    cb = bdot_nt(cg, bg)
    acs_t = acs_c.T
    for z in range(HEADS_PER_GROUP):
        seg = _take_col(z)(acs_c) - _take_row(z)(acs_t)
        lmat = jnp.exp(jnp.where(causal, seg, -1e30))
        y = y + bdot_nn(cb * lmat, x * (head_of_col == z).astype(F32))
    yz = y * _silu(zz)
    ms = jnp.mean(yz * yz, axis=-1, keepdims=True)
    return yz * lax.rsqrt(ms + LN_EPS) * nw, h_next


def _ssd_in_specs(nc, rev):
    def tok(g, b, c):
        return b * nc + (nc - 1 - c if rev else c)

    return [
        pl.BlockSpec((SSD_CHUNK, GROUP_COLS), lambda g, b, c: (tok(g, b, c), g)),
        pl.BlockSpec((SSD_CHUNK, SSD_STATE), lambda g, b, c: (tok(g, b, c), 4 + g)),
        pl.BlockSpec((SSD_CHUNK, SSD_STATE), lambda g, b, c: (tok(g, b, c), 6 + g)),
        pl.BlockSpec((SSD_CHUNK, LANE), lambda g, b, c: (tok(g, b, c), P_DT // LANE + g)),
        pl.BlockSpec((SSD_CHUNK, GROUP_COLS), lambda g, b, c: (tok(g, b, c), P_Z // GROUP_COLS + g)),
        pl.BlockSpec((1, 1, LANE), lambda g, b, c: (g, 0, 0)),
        pl.BlockSpec((1, 1, LANE), lambda g, b, c: (g, 0, 0)),
        pl.BlockSpec((1, LANE, 1), lambda g, b, c: (g, 0, 0)),
        pl.BlockSpec((1, 1, GROUP_COLS), lambda g, b, c: (g, 0, 0)),
    ], tok


def ssd_fwd(xc, proj, dtb, alog, dcol, nw, bsz, seq):
    t = xc.shape[0]
    nc = seq // SSD_CHUNK
    in_specs, tok = _ssd_in_specs(nc, False)

    def body(xs, bg, cg, dtr, zz, dtb_r, alog_r, dcol_r, nw_r, y_ref, hp_ref, h_scr):
        @pl.when(pl.program_id(2) == 0)
        def _():
            h_scr[...] = jnp.zeros_like(h_scr)

        hp = h_scr[...]
        hp_ref[0, 0, 0] = hp
        y, hn = _ssd_chunk(xs[...], bg[...], cg[...], dtr[...], zz[...], hp, dtb_r[0], alog_r[0], dcol_r[0], nw_r[0])
        y_ref[...] = y
        h_scr[...] = hn

    return pl.pallas_call(
        body, name="ssd_fwd", grid=(SSD_GROUPS, bsz, nc), in_specs=in_specs,
        out_specs=[pl.BlockSpec((SSD_CHUNK, GROUP_COLS), lambda g, b, c: (tok(g, b, c), g)),
                   pl.BlockSpec((1, 1, 1, SSD_STATE, GROUP_COLS), lambda g, b, c: (g, b, c, 0, 0))],
        out_shape=[jax.ShapeDtypeStruct((t, SSD_WIDTH), F32),
                   jax.ShapeDtypeStruct((SSD_GROUPS, bsz, nc, SSD_STATE, GROUP_COLS), F32)],
        scratch_shapes=[pltpu.VMEM((SSD_STATE, GROUP_COLS), F32)],
        compiler_params=_cp("arbitrary", "arbitrary", "arbitrary"),
    )(xc, xc, xc, proj, proj, dtb, alog, dcol, nw)


def ssd_bwd(xc, proj, dtb, alog, dcol, nw, hprev, dy, bsz, seq):
    t = xc.shape[0]
    nc = seq // SSD_CHUNK
    in_specs, tok = _ssd_in_specs(nc, True)
    in_specs += [pl.BlockSpec((1, 1, 1, SSD_STATE, GROUP_COLS), lambda g, b, c: (g, b, nc - 1 - c, 0, 0)),
                 pl.BlockSpec((SSD_CHUNK, GROUP_COLS), lambda g, b, c: (tok(g, b, c), g))]

    def body(xs, bg, cg, dtr, zz, dtb_r, alog_r, dcol_r, nw_r, hp_ref, dy_ref,
             dxs, dbg, dcg, ddt, dzz, ddtb, dalog, ddcol, dnw, dh_scr):
        b, c = pl.program_id(1), pl.program_id(2)

        @pl.when(c == 0)
        def _():
            dh_scr[...] = jnp.zeros_like(dh_scr)

        @pl.when((b == 0) & (c == 0))
        def _():
            for r in (ddtb, dalog, ddcol, dnw):
                r[...] = jnp.zeros_like(r)

        _, pull = jax.vjp(_ssd_chunk, xs[...], bg[...], cg[...], dtr[...], zz[...], hp_ref[0, 0, 0],
                          dtb_r[0], alog_r[0], dcol_r[0], nw_r[0])
        g = pull((dy_ref[...], dh_scr[...]))
        dxs[...], dbg[...], dcg[...], ddt[...], dzz[...] = g[0], g[1], g[2], g[3], g[4]
        dh_scr[...] = g[5]
        ddtb[0] += g[6]
        dalog[0] += g[7]
        ddcol[0] += g[8]
        dnw[0] += g[9]

    def tile(w):
        return pl.BlockSpec((SSD_CHUNK, w), lambda g, b, c: (tok(g, b, c), g))

    return pl.pallas_call(
        body, name="ssd_bwd", grid=(SSD_GROUPS, bsz, nc), in_specs=in_specs,
        out_specs=[tile(GROUP_COLS), tile(SSD_STATE), tile(SSD_STATE), tile(LANE), tile(GROUP_COLS),
                   pl.BlockSpec((1, 1, LANE), lambda g, b, c: (g, 0, 0)),
                   pl.BlockSpec((1, 1, LANE), lambda g, b, c: (g, 0, 0)),
                   pl.BlockSpec((1, LANE, 1), lambda g, b, c: (g, 0, 0)),
                   pl.BlockSpec((1, 1, GROUP_COLS), lambda g, b, c: (g, 0, 0))],
        out_shape=[jax.ShapeDtypeStruct((t, SSD_WIDTH), F32), jax.ShapeDtypeStruct((t, 2 * SSD_STATE), F32),
                   jax.ShapeDtypeStruct((t, 2 * SSD_STATE), F32), jax.ShapeDtypeStruct((t, 2 * LANE), F32),
                   jax.ShapeDtypeStruct((t, SSD_WIDTH), F32),
                   jax.ShapeDtypeStruct((SSD_GROUPS, 1, LANE), F32), jax.ShapeDtypeStruct((SSD_GROUPS, 1, LANE), F32),
                   jax.ShapeDtypeStruct((SSD_GROUPS, LANE, 1), F32),
                   jax.ShapeDtypeStruct((SSD_GROUPS, 1, GROUP_COLS), F32)],
        scratch_shapes=[pltpu.VMEM((SSD_STATE, GROUP_COLS), F32)],
        compiler_params=_cp("arbitrary", "arbitrary", "arbitrary"),
    )(xc, xc, xc, proj, proj, dtb, alog, dcol, nw, hprev, dy)


def _disc_a(a_re, a_im, log_dt):
    dt = jnp.exp(log_dt)
    mag = jnp.exp(dt * a_re)
    ab_re, ab_im = mag * jnp.cos(dt * a_im), mag * jnp.sin(dt * a_im)
    den = a_re * a_re + a_im * a_im
    nr, ni = ab_re - 1.0, ab_im
    f_re, f_im = (nr * a_re + ni * a_im) / den, (ni * a_re - nr * a_im) / den
    return ab_re, ab_im, f_re, f_im


def _disc_b(f_re, f_im, b_re, b_im):
    return f_re * b_re - f_im * b_im, f_re * b_im + f_im * b_re


def _whole(f, name, args, outs):
    def body(*refs):
        res = f(*[r[...] for r in refs[:len(args)]])
        for o, v in zip(refs[len(args):], res):
            o[...] = v

    return pl.pallas_call(body, name=name, out_shape=[jax.ShapeDtypeStruct(s, F32) for s in outs])(*args)


def _whole_vjp(f, name, args, cts):
    def body(*refs):
        vals = [r[...] for r in refs[:len(args)]]
        _, pull = jax.vjp(f, *vals)
        res = pull(tuple(r[...] for r in refs[len(args):len(args) + len(cts)]))
        for o, v in zip(refs[len(args) + len(cts):], res):
            o[...] = v

    return pl.pallas_call(body, name=name, out_shape=[jax.ShapeDtypeStruct(a.shape, F32) for a in args])(*args, *cts)


def s5_tables(lam_re, lam_im, length):
    assert int(math.log2(length)) <= HALO

    def body(lr_ref, li_ref, pw_re, pw_im, up_re, up_im, dn_re, dn_im):
        lr, li = lr_ref[...], li_ref[...]

        def power(k):
            m = jnp.exp(k * lr)
            return m * jnp.cos(k * li), m * jnp.sin(k * li)

        srow = lax.broadcasted_iota(jnp.int32, (HALO, 1), 0)
        pw_re[...], pw_im[...] = power(jnp.left_shift(1, srow).astype(F32))
        trow = lax.broadcasted_iota(jnp.int32, (length, 1), 0)
        up_re[...], up_im[...] = power((trow + 1).astype(F32))
        dn_re[...], dn_im[...] = power((length - trow).astype(F32))

    shp = [jax.ShapeDtypeStruct((HALO, S5_COLS), F32)] * 2 + [jax.ShapeDtypeStruct((length, S5_COLS), F32)] * 4
    return pl.pallas_call(body, name="s5_tables", out_shape=shp)(lam_re, lam_im)


def _s5_specs(n5, rev):
    def tok(q, b, c):
        return b * n5 + (n5 - 1 - c if rev else c)

    qcols = S5_COLS // S5_Q
    specs = [
        pl.BlockSpec((S5_CHUNK, LANE), lambda q, b, c: (tok(q, b, c), P_U // LANE + q)),
        pl.BlockSpec((1, LANE, qcols), lambda q, b, c: (q, 0, 0)),
        pl.BlockSpec((1, LANE, qcols), lambda q, b, c: (q, 0, 0)),
        pl.BlockSpec((1, qcols, LANE), lambda q, b, c: (q, 0, 0)),
        pl.BlockSpec((1, qcols, LANE), lambda q, b, c: (q, 0, 0)),
        pl.BlockSpec((HALO, qcols), lambda q, b, c: (0, q)),
        pl.BlockSpec((HALO, qcols), lambda q, b, c: (0, q)),
        pl.BlockSpec((S5_CHUNK, qcols), lambda q, b, c: (0, q)),
        pl.BlockSpec((S5_CHUNK, qcols), lambda q, b, c: (0, q)),
        pl.BlockSpec((1, 1, LANE), lambda q, b, c: (q, 0, 0)),
    ]
    return specs, tok, qcols


def s5_fwd(proj, wb_re, wb_im, wc_re, wc_im, pw_re, pw_im, up_re, up_im, dvec, bsz, seq, ride=None):
    t = proj.shape[0]
    n5 = seq // S5_CHUNK
    nsteps = int(math.log2(S5_CHUNK))
    in_specs, tok, qcols = _s5_specs(n5, False)

    def body(u_ref, wbr, wbi, wcr, wci, pwr, pwi, upr, upi, d_ref, y_ref, xr_ref, xi_ref, cr_scr, ci_scr):
        @pl.when(pl.program_id(2) == 0)
        def _():
            cr_scr[...] = jnp.zeros_like(cr_scr)
            ci_scr[...] = jnp.zeros_like(ci_scr)

        u = u_ref[...]
        xr, xi = _dg(u, wbr[0], 1, 0), _dg(u, wbi[0], 1, 0)
        row = lax.broadcasted_iota(jnp.int32, (S5_CHUNK, 1), 0)
        for s in range(nsteps):
            k = 1 << s
            ar, ai = pwr[s:s + 1, :], pwi[s:s + 1, :]
            sr = jnp.where(row >= k, pltpu.roll(xr, k, 0), 0.0)
            si = jnp.where(row >= k, pltpu.roll(xi, k, 0), 0.0)
            xr, xi = xr + ar * sr - ai * si, xi + ar * si + ai * sr
        cr, ci = cr_scr[...], ci_scr[...]
        pr, pi = upr[...], upi[...]
        xr, xi = xr + pr * cr - pi * ci, xi + pr * ci + pi * cr
        xr_ref[...], xi_ref[...] = xr, xi
        cr_scr[...], ci_scr[...] = xr[S5_CHUNK - 1:, :], xi[S5_CHUNK - 1:, :]
        y_ref[...] = _dg(xr, wcr[0], 1, 0) - _dg(xi, wci[0], 1, 0) + u * d_ref[0]

    def tile(w):
        return pl.BlockSpec((S5_CHUNK, w), lambda q, b, c: (tok(q, b, c), q))

    return hosted_call(
        body, name="s5_fwd", grid=(S5_Q, bsz, n5), in_specs=in_specs,
        out_specs=[tile(LANE), tile(qcols), tile(qcols)],
        out_shape=[jax.ShapeDtypeStruct((t, S5_WIDTH), F32), jax.ShapeDtypeStruct((t, S5_COLS), F32),
                   jax.ShapeDtypeStruct((t, S5_COLS), F32)],
        scratch=[pltpu.VMEM((1, qcols), F32)] * 2,
        args=(proj, wb_re, wb_im, wc_re, wc_im, pw_re, pw_im, up_re, up_im, dvec), ride=ride)


def s5_bwd(proj, wb_re, wb_im, wc_re, wc_im, pw_re, pw_im, dn_re, dn_im, dvec, xr_all, xi_all, dy, bsz, seq,
           ride=None):
    t = proj.shape[0]
    n5 = seq // S5_CHUNK
    nsteps = int(math.log2(S5_CHUNK))
    in_specs, tok, qcols = _s5_specs(n5, True)
    blocks = S5_CHUNK // HALO

    def prev_rows(q, b, c):
        return (jnp.maximum(tok(q, b, c) * blocks - 1, 0), q)

    in_specs += [pl.BlockSpec((S5_CHUNK, qcols), lambda q, b, c: (tok(q, b, c), q)),
                 pl.BlockSpec((S5_CHUNK, qcols), lambda q, b, c: (tok(q, b, c), q)),
                 pl.BlockSpec((HALO, qcols), prev_rows), pl.BlockSpec((HALO, qcols), prev_rows),
                 pl.BlockSpec((S5_CHUNK, LANE), lambda q, b, c: (tok(q, b, c), q))]

    def body(u_ref, wbr, wbi, wcr, wci, pwr, pwi, dnr, dni, d_ref, xr_ref, xi_ref, pr_ref, pi_ref, dy_ref,
             du_ref, dwbr, dwbi, dwcr, dwci, dar, dai, dd_ref, gr_scr, gi_scr):
        b, c = pl.program_id(1), pl.program_id(2)

        @pl.when(c == 0)
        def _():
            gr_scr[...] = jnp.zeros_like(gr_scr)
            gi_scr[...] = jnp.zeros_like(gi_scr)

        @pl.when((b == 0) & (c == 0))
        def _():
            for r in (dwbr, dwbi, dwcr, dwci, dar, dai, dd_ref):
                r[...] = jnp.zeros_like(r)

        u, dy_v = u_ref[...], dy_ref[...]
        gr, gi = _dg(dy_v, wcr[0], 1, 1), -_dg(dy_v, wci[0], 1, 1)
        row = lax.broadcasted_iota(jnp.int32, (S5_CHUNK, 1), 0)
        for s in range(nsteps):
            k = 1 << s
            ar, ai = pwr[s:s + 1, :], pwi[s:s + 1, :]
            sr = jnp.where(row < S5_CHUNK - k, pltpu.roll(gr, S5_CHUNK - k, 0), 0.0)
            si = jnp.where(row < S5_CHUNK - k, pltpu.roll(gi, S5_CHUNK - k, 0), 0.0)
            gr, gi = gr + ar * sr + ai * si, gi + ar * si - ai * sr
        cr, ci = gr_scr[...], gi_scr[...]
        pr, pi = dnr[...], dni[...]
        gr, gi = gr + pr * cr + pi * ci, gi + pr * ci - pi * cr
        gr_scr[...], gi_scr[...] = gr[:1, :], gi[:1, :]

        xr, xi = xr_ref[...], xi_ref[...]
        is_first = (c == n5 - 1)
        hr = jnp.where(is_first, 0.0, pr_ref[...][HALO - 1:, :])
        hi = jnp.where(is_first, 0.0, pi_ref[...][HALO - 1:, :])
        xpr = jnp.where(row >= 1, pltpu.roll(xr, 1, 0), hr)
        xpi = jnp.where(row >= 1, pltpu.roll(xi, 1, 0), hi)
        dar[0] += jnp.sum(xpr * gr + xpi * gi, axis=0, keepdims=True)
        dai[0] += jnp.sum(xpr * gi - xpi * gr, axis=0, keepdims=True)
        du_ref[...] = _dg(gr, wbr[0], 1, 1) + _dg(gi, wbi[0], 1, 1) + dy_v * d_ref[0]
        dwbr[0] += _dg(u, gr, 0, 0)
        dwbi[0] += _dg(u, gi, 0, 0)
        dwcr[0] += _dg(xr, dy_v, 0, 0)
        dwci[0] -= _dg(xi, dy_v, 0, 0)
        dd_ref[0] += jnp.sum(dy_v * u, axis=0, keepdims=True)

    def acc(shape):
        return pl.BlockSpec((1,) + shape, lambda q, b, c: (q, 0, 0))

    return hosted_call(
        body, name="s5_bwd", grid=(S5_Q, bsz, n5), in_specs=in_specs,
        out_specs=[pl.BlockSpec((S5_CHUNK, LANE), lambda q, b, c: (tok(q, b, c), q)),
                   acc((LANE, qcols)), acc((LANE, qcols)), acc((qcols, LANE)), acc((qcols, LANE)),
                   acc((1, qcols)), acc((1, qcols)), acc((1, LANE))],
        out_shape=[jax.ShapeDtypeStruct((t, S5_WIDTH), F32),
                   jax.ShapeDtypeStruct((S5_Q, LANE, qcols), F32), jax.ShapeDtypeStruct((S5_Q, LANE, qcols), F32),
                   jax.ShapeDtypeStruct((S5_Q, qcols, LANE), F32), jax.ShapeDtypeStruct((S5_Q, qcols, LANE), F32),
                   jax.ShapeDtypeStruct((S5_Q, 1, qcols), F32), jax.ShapeDtypeStruct((S5_Q, 1, qcols), F32),
                   jax.ShapeDtypeStruct((S5_Q, 1, LANE), F32)],
        scratch=[pltpu.VMEM((1, qcols), F32)] * 2,
        args=(proj, wb_re, wb_im, wc_re, wc_im, pw_re, pw_im, dn_re, dn_im, dvec, xr_all, xi_all, xr_all, xi_all, dy),
        ride=ride)


def _blockdiag_b(bb):
    b4 = bb.reshape(S5_Q, 8, S5_STATE, S5_GROUP_CH)
    eye = jnp.eye(8, dtype=bb.dtype)
    w = jnp.einsum("qgph,gk->qghkp", b4, eye)
    return w.reshape(S5_Q, LANE, S5_COLS // S5_Q)


def _unblock_b(dw):
    d = dw.reshape(S5_Q, 8, S5_GROUP_CH, 8, S5_STATE)
    d = jnp.einsum("qghgp->qgph", d)
    return d.reshape(S5_COLS, S5_GROUP_CH)


def _blockdiag_c(cc):
    c4 = cc.reshape(S5_Q, 8, S5_GROUP_CH, S5_STATE)
    eye = jnp.eye(8, dtype=cc.dtype)
    w = jnp.einsum("qghp,gk->qgpkh", c4, eye)
    return w.reshape(S5_Q, S5_COLS // S5_Q, LANE)


def _unblock_c(dw):
    d = dw.reshape(S5_Q, 8, S5_STATE, 8, S5_GROUP_CH)
    d = jnp.einsum("qgpgh->qghp", d)
    return d.reshape(S5_GROUPS, S5_GROUP_CH, S5_STATE)


def ada_fwd(c_all, w_loc, b_loc):
    def body(c_ref, w_ref, b_ref, o_ref):
        o_ref[...] = _dg(_silu(c_ref[...]), w_ref[...], 1, 0) + b_ref[...]

    return pl.pallas_call(body, name="ada_fwd",
                          out_shape=jax.ShapeDtypeStruct((c_all.shape[0], w_loc.shape[1]), F32),
                          compiler_params=_cp())(c_all, w_loc, b_loc)


def ada_bwd(c_all, dmod_all, dmod_cols):
    def body(c_ref, da_ref, dc_ref, gb_ref, gw_ref):
        gb_ref[...] = jnp.sum(da_ref[...], axis=0, keepdims=True)
        gw_ref[...] = _dg(_silu(c_ref[...]), dc_ref[...], 0, 0)

    return pl.pallas_call(body, name="ada_bwd",
                          out_shape=[jax.ShapeDtypeStruct((1, dmod_all.shape[1]), F32),
                                     jax.ShapeDtypeStruct((c_all.shape[1], dmod_cols.shape[1]), F32)],
                          compiler_params=_cp())(c_all, dmod_all, dmod_cols)


_FLIPS = [(0, 0, 1), (1, 0, 0), (0, 1, 0), (1, 1, 0), (1, 0, 1), (0, 1, 1), (1, 1, 1)]


def _exchange_ops(srcs, outs, sems, gather):
    n = len(srcs)
    send_sems, recv_sems, loc_sems = sems
    x, y, c = lax.axis_index("x"), lax.axis_index("y"), lax.axis_index("c")
    me = 4 * x + 2 * y + c
    peers = []
    for fx, fy, fc in _FLIPS:
        px, py, pc = (1 - x if fx else x), (1 - y if fy else y), (1 - c if fc else c)
        peers.append(((px, py, pc), 4 * px + 2 * py + pc))

    def copy(k, j, slot_src, slot_dst):
        src = srcs[k] if gather[k] else srcs[k].at[slot_src]
        return pltpu.make_async_remote_copy(src_ref=src, dst_ref=outs[k].at[slot_dst],
                                            send_sem=send_sems.at[k, j], recv_sem=recv_sems.at[k, j],
                                            device_id=peers[j][0], device_id_type=MESH)

    def local(k):
        own = srcs[k] if gather[k] else srcs[k].at[me]
        return pltpu.make_async_copy(own, outs[k].at[me], loc_sems.at[k])

    def start():
        for k in range(n):
            for j in range(N_DEV - 1):
                copy(k, j, peers[j][1], me).start()
            local(k).start()

    def wait():
        for k in range(n):
            for j in range(N_DEV - 1):
                copy(k, j, me, peers[j][1]).wait_recv()
        for k in range(n):
            for j in range(N_DEV - 1):
                copy(k, j, peers[j][1], me).wait_send()
            local(k).wait()

    return start, wait


def _exchange_parts(arrs, gather):
    n = len(arrs)
    any_spec = pl.BlockSpec(memory_space=pl.ANY)
    shapes = [jax.ShapeDtypeStruct(((N_DEV,) + a.shape) if g else a.shape, a.dtype) for a, g in zip(arrs, gather)]
    sems = [pltpu.SemaphoreType.DMA((n, N_DEV - 1)), pltpu.SemaphoreType.DMA((n, N_DEV - 1)),
            pltpu.SemaphoreType.DMA((n,))]
    return [any_spec] * n, shapes, sems


def exchange(name, arrs, gather):
    n = len(arrs)
    specs, shapes, sems = _exchange_parts(arrs, gather)

    def body(*refs):
        start, wait = _exchange_ops(refs[:n], refs[n:2 * n], refs[2 * n:], gather)
        start()
        wait()

    return pl.pallas_call(
        body, name=name, in_specs=specs, out_specs=specs, out_shape=shapes, scratch_shapes=sems,
        compiler_params=pltpu.CompilerParams(has_side_effects=True),
    )(*arrs)


def hosted_call(body, *, name, grid, in_specs, out_specs, out_shape, args, scratch=(), ride=None):
    sem = ("arbitrary",) * len(grid)
    if ride is None:
        res = pl.pallas_call(body, name=name, grid=grid, in_specs=in_specs, out_specs=out_specs, out_shape=out_shape,
                             scratch_shapes=list(scratch), compiler_params=_cp(*sem))(*args)
        return list(res), []
    arrs, gather = ride
    n, n_in, n_out, n_scr = len(arrs), len(in_specs), len(out_specs), len(scratch)
    specs, shapes, sems = _exchange_parts(arrs, gather)

    def both(*refs):
        ins, srcs = refs[:n_in], refs[n_in:n_in + n]
        outs, landed = refs[n_in + n:n_in + n + n_out], refs[n_in + n + n_out:n_in + 2 * n + n_out]
        scr, ex_sems = refs[n_in + 2 * n + n_out:n_in + 2 * n + n_out + n_scr], refs[n_in + 2 * n + n_out + n_scr:]
        start, wait = _exchange_ops(srcs, landed, ex_sems, gather)
        first = functools.reduce(lambda a, b: a & b, [pl.program_id(d) == 0 for d in range(len(grid))])
        last = functools.reduce(lambda a, b: a & b, [pl.program_id(d) == grid[d] - 1 for d in range(len(grid))])
        pl.when(first)(start)
        body(*ins, *outs, *scr)
        pl.when(last)(wait)

    res = pl.pallas_call(
        both, name=name, grid=grid, in_specs=list(in_specs) + specs, out_specs=list(out_specs) + specs,
        out_shape=list(out_shape) + shapes, scratch_shapes=list(scratch) + sems, compiler_params=_cp(*sem),
    )(*args, *arrs)
    return list(res[:n_out]), list(res[n_out:])


def adamw(name, g, w, m, v, tr, sel=None):
    slots = g.ndim >= 3
    r, c = w.shape
    c1, c2 = 1.0 - ADAM_B1 ** ADAM_STEP, 1.0 - ADAM_B2 ** ADAM_STEP

    def body(g_ref, w_ref, m_ref, v_ref, go, do, mo, vo):
        if slots:
            gg = g_ref[0].astype(F32)
            for j in range(1, N_DEV):
                gg = gg + g_ref[j].astype(F32)
        else:
            gg = g_ref[...]
        mn = ADAM_B1 * m_ref[...] + (1.0 - ADAM_B1) * gg
        vn = ADAM_B2 * v_ref[...] + (1.0 - ADAM_B2) * (gg * gg)
        go[...], mo[...], vo[...] = gg, mn, vn
        do[...] = -ADAM_LR * ((mn / c1) / (jnp.sqrt(vn / c2) + ADAM_EPS) + ADAM_WD * w_ref[...])

    blk = pl.BlockSpec((tr, c), lambda i: (i, 0))
    if g.ndim == 4:
        gspec = pl.BlockSpec((N_DEV, None, tr, c), lambda i: (0, sel, i, 0))
    else:
        gspec = pl.BlockSpec((N_DEV, tr, c), lambda i: (0, i, 0)) if slots else blk
    return pl.pallas_call(
        body, name=name, grid=(r // tr,), in_specs=[gspec, blk, blk, blk], out_specs=[blk] * 4,
        out_shape=[jax.ShapeDtypeStruct((r, c), F32)] * 4, compiler_params=_cp("parallel"),
    )(g, w, m, v)


def _pack(arrs, rows):
    flat = jnp.concatenate([a.reshape(-1).astype(F32) for a in arrs])
    return jnp.pad(flat, (0, rows * LANE - flat.shape[0])).reshape(rows, LANE)


def _unpack(buf, shapes):
    flat, out, off = buf.reshape(-1), [], 0
    for s in shapes:
        n = math.prod(s)
        out.append(flat[off:off + n].reshape(s))
        off += n
    return out


def _cols_to_full(g):
    return jnp.transpose(g, (1, 0, 2)).reshape(g.shape[1], N_DEV * g.shape[2])


def _full_to_cols(w):
    r, c = w.shape
    return jnp.transpose(w.reshape(r, N_DEV, c // N_DEV), (1, 0, 2))


FF_CHUNK = D_FF // 2
FFN_TM = 256


def _resident(shape):
    return pl.BlockSpec(shape, lambda i: (0,) * len(shape), pipeline_mode=pl.Buffered(1))


def _ffn_fwd(tag, x, sc, sh, g, w1, w3, w2, lg, lb, seq, tm, ride=None):
    t = x.shape[0]
    tm = min(FFN_TM, tm)
    tps = seq // tm
    ln = _res_ln(0.5)

    def body(x_ref, sc_ref, sh_ref, g_ref, lg_ref, lb_ref, w1_ref, w3_ref, w2_ref, y_ref, h_ref, a_ref, b_ref, f_ref):
        xv = x_ref[...]
        h = (xv * (1.0 + sc_ref[0]) + sh_ref[0]).astype(BF16)
        h_ref[...] = h
        acc = jnp.zeros((tm, D_MODEL), F32)
        for j in range(D_FF // FF_CHUNK):
            sl = slice(j * FF_CHUNK, (j + 1) * FF_CHUNK)
            a = _dg(h, w1_ref[:, sl], 1, 0)
            b = _dg(h, w3_ref[:, sl], 1, 0)
            a_ref[:, sl] = a
            b_ref[:, sl] = b
            acc = acc + _dg(_silu(a) * b, w2_ref[sl, :], 1, 0)
        f_ref[...] = acc
        y_ref[...] = ln(xv, acc, g_ref[0], lg_ref[...], lb_ref[...])[0]

    row = lambda c: pl.BlockSpec((tm, c), lambda i: (i, 0))
    per_seq = pl.BlockSpec((1, 1, D_MODEL), lambda i: (i // tps, 0, 0))
    vec = pl.BlockSpec((1, D_MODEL), lambda i: (0, 0))
    (y, h, a, b, f), landed = hosted_call(
        body, name=tag + "_fwd", grid=(t // tm,),
        in_specs=[row(D_MODEL), per_seq, per_seq, per_seq, vec, vec,
                  _resident((D_MODEL, D_FF)), _resident((D_MODEL, D_FF)), _resident((D_FF, D_MODEL))],
        out_specs=[row(D_MODEL), row(D_MODEL), row(D_FF), row(D_FF), row(D_MODEL)],
        out_shape=[jax.ShapeDtypeStruct((t, D_MODEL), F32), jax.ShapeDtypeStruct((t, D_MODEL), BF16),
                   jax.ShapeDtypeStruct((t, D_FF), F32), jax.ShapeDtypeStruct((t, D_FF), F32),
                   jax.ShapeDtypeStruct((t, D_MODEL), F32)],
        args=(x, sc, sh, g, lg, lb, w1, w3, w2), ride=ride)
    return y, (h, a, b, f), landed


def _ffn_bwd(tag, dy, x, sc, sh, g, w1, w3, w2, lg, lb, res, seq, tm, ride=None):
    h, a, b, f = res
    t = x.shape[0]
    tmk = min(FFN_TM, tm)
    tps = seq // tmk
    ln = _res_ln(0.5)

    def body(dy_ref, x_ref, f_ref, a_ref, b_ref, sc_ref, sh_ref, g_ref, lg_ref, lb_ref, w1_ref, w3_ref, w2_ref,
             dx_ref, da_ref, db_ref, s_ref, df_ref, dsc_ref, dsh_ref, dg_ref, dlg_ref, dlb_ref):
        i = pl.program_id(0)

        @pl.when(i % tps == 0)
        def _():
            for r in (dsc_ref, dsh_ref, dg_ref):
                r[...] = jnp.zeros_like(r)

        @pl.when(i == 0)
        def _():
            dlg_ref[...] = jnp.zeros_like(dlg_ref)
            dlb_ref[...] = jnp.zeros_like(dlb_ref)

        xv = x_ref[...]
        _, pull = jax.vjp(ln, xv, f_ref[...], g_ref[0], lg_ref[...], lb_ref[...])
        dx_res, df, dg, dlg, dlb = pull((dy_ref[...],))
        dfb = df.astype(BF16)
        df_ref[...] = dfb
        dh = jnp.zeros((tmk, D_MODEL), F32)
        for j in range(D_FF // FF_CHUNK):
            sl = slice(j * FF_CHUNK, (j + 1) * FF_CHUNK)
            ds = _dg(dfb, w2_ref[sl, :], 1, 1)
            av, bv = a_ref[:, sl], b_ref[:, sl]
            sg = jax.nn.sigmoid(av)
            si = av * sg
            s_ref[:, sl] = (si * bv).astype(BF16)
            da = (ds * bv * (sg * (1.0 + av * (1.0 - sg)))).astype(BF16)
            db = (ds * si).astype(BF16)
            da_ref[:, sl] = da
            db_ref[:, sl] = db
            dh = dh + _dg(da, w1_ref[:, sl], 1, 1) + _dg(db, w3_ref[:, sl], 1, 1)
        dx_ref[...] = dx_res + dh * (1.0 + sc_ref[0])
        dsc_ref[0] += jnp.sum(dh * xv, axis=0, keepdims=True)
        dsh_ref[0] += jnp.sum(dh, axis=0, keepdims=True)
        dg_ref[0] += dg
        dlg_ref[...] += dlg
        dlb_ref[...] += dlb

    row = lambda c: pl.BlockSpec((tmk, c), lambda i: (i, 0))
    per_seq = pl.BlockSpec((1, 1, D_MODEL), lambda i: (i // tps, 0, 0))
    vec = pl.BlockSpec((1, D_MODEL), lambda i: (0, 0))
    seq_shape = jax.ShapeDtypeStruct(sc.shape, F32)
    vec_shape = jax.ShapeDtypeStruct((1, D_MODEL), F32)
    (dx, da, db, s, df, dsc, dsh, dg, dlg, dlb), landed = hosted_call(
        body, name=tag + "_bwd", grid=(t // tmk,),
        in_specs=[row(D_MODEL), row(D_MODEL), row(D_MODEL), row(D_FF), row(D_FF), per_seq, per_seq, per_seq, vec, vec,
                  _resident((D_MODEL, D_FF)), _resident((D_MODEL, D_FF)), _resident((D_FF, D_MODEL))],
        out_specs=[row(D_MODEL), row(D_FF), row(D_FF), row(D_FF), row(D_MODEL), per_seq, per_seq, per_seq, vec, vec],
        out_shape=[jax.ShapeDtypeStruct((t, D_MODEL), F32), jax.ShapeDtypeStruct((t, D_FF), BF16),
                   jax.ShapeDtypeStruct((t, D_FF), BF16), jax.ShapeDtypeStruct((t, D_FF), BF16),
                   jax.ShapeDtypeStruct((t, D_MODEL), BF16), seq_shape, seq_shape, seq_shape, vec_shape, vec_shape],
        args=(dy, x, f, a, b, sc, sh, g, lg, lb, w1, w3, w2), ride=ride)
    dw2 = mm_tn(tag + "_dw2", s, df, D_FF // 2, D_MODEL, tm, BF16)
    dw1 = mm_tn(tag + "_dw1", h, da, D_MODEL, D_FF // 2, tm, BF16)
    dw3 = mm_tn(tag + "_dw3", h, db, D_MODEL, D_FF // 2, tm, BF16)
    return dx, (dsh, dsc, dg), (dw1, dw3, dw2, dlg, dlb), landed


def kernel(x, c, w_ada, b_ada, ffn1_w1, ffn1_w3, ffn1_w2, ln1_g, ln1_b, w_in, conv_w, conv_b, dt_bias, a_log, d_ssd, ssd_norm_w, s5_a_re, s5_a_im, s5_log_dt, s5_b_re, s5_b_im, s5_c_re, s5_c_im, s5_d, w_glu, b_glu, w_out, ln2_g, ln2_b, ffn2_w1, ffn2_w3, ffn2_w2, ln3_g, ln3_b, loss_target, m_w_ada, m_b_ada, m_ffn1_w1, m_ffn1_w3, m_ffn1_w2, m_ln1_g, m_ln1_b, m_w_in, m_conv_w, m_conv_b, m_dt_bias, m_a_log, m_d_ssd, m_ssd_norm_w, m_s5_a_re, m_s5_a_im, m_s5_log_dt, m_s5_b_re, m_s5_b_im, m_s5_c_re, m_s5_c_im, m_s5_d, m_w_glu, m_b_glu, m_w_out, m_ln2_g, m_ln2_b, m_ffn2_w1, m_ffn2_w3, m_ffn2_w2, m_ln3_g, m_ln3_b, v_w_ada, v_b_ada, v_ffn1_w1, v_ffn1_w3, v_ffn1_w2, v_ln1_g, v_ln1_b, v_w_in, v_conv_w, v_conv_b, v_dt_bias, v_a_log, v_d_ssd, v_ssd_norm_w, v_s5_a_re, v_s5_a_im, v_s5_log_dt, v_s5_b_re, v_s5_b_im, v_s5_c_re, v_s5_c_im, v_s5_d, v_w_glu, v_b_glu, v_w_out, v_ln2_g, v_ln2_b, v_ffn2_w1, v_ffn2_w3, v_ffn2_w2, v_ln3_g, v_ln3_b):
    given = dict(locals())
    bsz, seq, _ = x.shape
    t = bsz * seq
    tm = min(512, seq)
    me = 4 * lax.axis_index("x") + 2 * lax.axis_index("y") + lax.axis_index("c")
    x0 = x.reshape(t, D_MODEL)
    target = loss_target.reshape(t, D_MODEL)

    g_col1, g_row1, g_c = exchange(
        "gather_ffn1", [jnp.stack([ffn1_w1[0], ffn1_w3[0]]).astype(BF16), ffn1_w2[0].astype(BF16), c], [True] * 3)
    f1w1, f1w3 = [_cols_to_full(g_col1[:, k]) for k in range(2)]
    f1w2 = g_row1.reshape(D_FF, D_MODEL)
    c_all = g_c.reshape(N_DEV * bsz, D_MODEL)

    n_loc = w_ada.shape[2]
    b_loc = lax.dynamic_slice(b_ada, (0, me * n_loc), (1, n_loc))
    mod_cols = ada_fwd(c_all, w_ada[0], b_loc)
    g_mod, = exchange("gather_mod", [mod_cols], [True])
    mine = lax.dynamic_slice(g_mod, (0, me * bsz, 0), (N_DEV, bsz, n_loc))
    mod = jnp.transpose(mine, (1, 0, 2)).reshape(bsz, N_MOD, 1, D_MODEL)
    sh1, sc1, g1, sh2, sc2, g2, sh3, sc3, g3 = [mod[:, k] for k in range(N_MOD)]

    x1, res1, (g_win, g_glu, g_out, g_conv) = _ffn_fwd(
        "ffn1", x0, sc1, sh1, g1, f1w1, f1w3, f1w2, ln1_g, ln1_b, seq, tm,
        ride=([w_in[0].astype(BF16), w_glu[0].astype(BF16), w_out[0].astype(BF16), conv_w[0]], [True] * 4))
    win = _cols_to_full(g_win)
    wglu = g_glu.reshape(S5_WIDTH, S5_WIDTH).astype(F32)
    wout = g_out.reshape(D_MODEL, D_MODEL)
    wo_ssd, wo_s5 = wout[:SSD_WIDTH], wout[SSD_WIDTH:]
    convw = jnp.transpose(g_conv, (1, 0, 2)).reshape(CONV_K, CONV_CH)
    w_z, w_xbc = win[:, :SSD_WIDTH], win[:, SSD_WIDTH:SSD_WIDTH + CONV_CH]
    w_dt = win[:, SSD_WIDTH + CONV_CH:SSD_WIDTH + CONV_CH + SSD_HEADS]
    w_u = win[:, SSD_WIDTH + CONV_CH + SSD_HEADS:]
    dt_pad = [jnp.pad(w_dt[:, HEADS_PER_GROUP * g:HEADS_PER_GROUP * (g + 1)], ((0, 0), (0, LANE - HEADS_PER_GROUP)))
              for g in range(SSD_GROUPS)]
    w_dtp = jnp.concatenate(dt_pad, axis=1)
    w_proj = jnp.concatenate([w_xbc, w_z, w_u, w_dtp], axis=1)

    h2, = rowwise_fwd("mix_mod", f_modulate, [x1], [sc2, sh2], [], [(D_MODEL, BF16)], seq, tm)
    proj = mm_nn("mix_proj", [h2], [w_proj], tm, P_COLS // 2)
    xc = conv_fwd(proj, convw, conv_b, seq, tm)
    dtb = jnp.pad(dt_bias.reshape(SSD_GROUPS, 1, HEADS_PER_GROUP), ((0, 0), (0, 0), (0, LANE - HEADS_PER_GROUP)))
    alog = jnp.pad(a_log.reshape(SSD_GROUPS, 1, HEADS_PER_GROUP), ((0, 0), (0, 0), (0, LANE - HEADS_PER_GROUP)))
    dcol = jnp.pad(d_ssd.reshape(SSD_GROUPS, HEADS_PER_GROUP, 1), ((0, 0), (0, LANE - HEADS_PER_GROUP), (0, 0)))
    nw = ssd_norm_w.reshape(SSD_GROUPS, 1, GROUP_COLS)
    y_ssd, hprev = ssd_fwd(xc, proj, dtb, alog, dcol, nw, bsz, seq)

    a_re2, a_im2, ldt2 = s5_a_re[0], s5_a_im[0], s5_log_dt.reshape(S5_GROUPS, 1)
    ab_re, ab_im, f_re, f_im = _whole(_disc_a, "s5_disc_a", [a_re2, a_im2, ldt2], [(S5_GROUPS, S5_STATE)] * 4)
    b_re2, b_im2 = s5_b_re.reshape(S5_COLS, S5_GROUP_CH), s5_b_im.reshape(S5_COLS, S5_GROUP_CH)
    fr_col, fi_col = f_re.reshape(S5_COLS, 1), f_im.reshape(S5_COLS, 1)
    bb_re, bb_im = _whole(_disc_b, "s5_disc_b", [fr_col, fi_col, b_re2, b_im2], [(S5_COLS, S5_GROUP_CH)] * 2)
    wb_re, wb_im = _blockdiag_b(bb_re).astype(BF16), _blockdiag_b(bb_im).astype(BF16)
    wc_re, wc_im = _blockdiag_c(s5_c_re[0]).astype(BF16), _blockdiag_c(s5_c_im[0]).astype(BF16)
    dt5 = jnp.exp(ldt2)
    lam_re, lam_im = (dt5 * a_re2).reshape(1, S5_COLS), (dt5 * a_im2).reshape(1, S5_COLS)
    pw_re, pw_im, up_re, up_im, dn_re, dn_im = s5_tables(lam_re, lam_im, S5_CHUNK)
    d5 = s5_d.reshape(S5_Q, 1, LANE)
    (y5, xr_all, xi_all), (g_col2, g_row2) = s5_fwd(
        proj, wb_re, wb_im, wc_re, wc_im, pw_re, pw_im, up_re, up_im, d5, bsz, seq,
        ride=([jnp.stack([ffn2_w1[0], ffn2_w3[0]]).astype(BF16), ffn2_w2[0].astype(BF16)], [True] * 2))
    f2w1, f2w3 = [_cols_to_full(g_col2[:, k]) for k in range(2)]
    f2w2 = g_row2.reshape(D_FF, D_MODEL)
    o5, = rowwise_fwd("s5_glu", f_glu, [y5], [], [wglu, b_glu], [(S5_WIDTH, F32)], seq, tm)

    mix = mm_nn("mix_out", [y_ssd, o5], [wo_ssd, wo_s5], tm, D_MODEL)
    x2, = rowwise_fwd("mix_ln", _res_ln(1.0), [x1, mix], [g2], [ln2_g, ln2_b], [(D_MODEL, F32)], seq, tm)

    x3, res3, _ = _ffn_fwd("ffn2", x2, sc3, sh3, g3, f2w1, f2w3, f2w2, ln3_g, ln3_b, seq, tm)
    dy, loss_loc = loss_head(x3, target, tm)

    dx2, dmod3, (d_f2w1, d_f2w3, d_f2w2, d_ln3g, d_ln3b), _ = _ffn_bwd(
        "ffn2", dy, x2, sc3, sh3, g3, f2w1, f2w3, f2w2, ln3_g, ln3_b, res3, seq, tm)

    (dx1_a, dmix), (dg2,), (d_ln2g, d_ln2b) = rowwise_bwd(
        "mix_ln_b", _res_ln(1.0), [x1, mix], [g2], [ln2_g, ln2_b], [dx2], seq, tm, [F32, BF16])
    d_wo = jnp.concatenate([mm_tn("mix_dwo_ssd", y_ssd, dmix, SSD_WIDTH, D_MODEL, tm, BF16),
                            mm_tn("mix_dwo_s5", o5, dmix, S5_WIDTH, D_MODEL, tm, BF16)], axis=0)
    dy_ssd = mm_nt("mix_dy_ssd", [dmix], [wo_ssd], tm, SSD_WIDTH)
    do5 = mm_nt("mix_do5", [dmix], [wo_s5], tm, S5_WIDTH)

    (dy5,), _, (d_wglu, d_bglu) = rowwise_bwd("s5_glu_b", f_glu, [y5], [], [wglu, b_glu], [do5], seq, tm, [F32])
    (du, dwbr, dwbi, dwcr, dwci, dab_re, dab_im, dd5), (s_col2, s_row2) = s5_bwd(
        proj, wb_re, wb_im, wc_re, wc_im, pw_re, pw_im, dn_re, dn_im, d5, xr_all, xi_all, dy5, bsz, seq,
        ride=([jnp.stack([_full_to_cols(d_f2w1), _full_to_cols(d_f2w3)], axis=1),
               d_f2w2.reshape(N_DEV, D_FF // N_DEV, D_MODEL)], [False] * 2))
    dbb_re, dbb_im = _unblock_b(dwbr), _unblock_b(dwbi)
    dfr_col, dfi_col, d_b_re, d_b_im = _whole_vjp(_disc_b, "s5_disc_b_b", [fr_col, fi_col, b_re2, b_im2],
                                                  [dbb_re, dbb_im])
    d_a_re, d_a_im, d_ldt = _whole_vjp(
        _disc_a, "s5_disc_a_b", [a_re2, a_im2, ldt2],
        [dab_re.reshape(S5_GROUPS, S5_STATE), dab_im.reshape(S5_GROUPS, S5_STATE),
         dfr_col.reshape(S5_GROUPS, S5_STATE), dfi_col.reshape(S5_GROUPS, S5_STATE)])
    d_c_re, d_c_im = _unblock_c(dwcr), _unblock_c(dwci)

    dxs, dbm, dcm, ddt, dz, ddtb, dalog, ddcol, dnw = ssd_bwd(xc, proj, dtb, alog, dcol, nw, hprev, dy_ssd, bsz, seq)
    dpre, d_convw, d_convb = conv_bwd_pre(proj, convw, conv_b, dxs, dbm, dcm, seq, tm)
    dxbc = conv_bwd_x(dpre, convw, seq, tm)

    dh2 = mm_nt("mix_dh", [dxbc, dz, du, ddt], [w_xbc, w_z, w_u, w_dtp], tm, D_MODEL)
    dw_xbc = mm_tn("mix_dw_xbc", h2, dxbc, D_MODEL, CONV_CH, tm, BF16)
    dw_z = mm_tn("mix_dw_z", h2, dz, D_MODEL, SSD_WIDTH, tm, BF16)
    dw_u = mm_tn("mix_dw_u", h2, du, D_MODEL, S5_WIDTH, tm, BF16)
    dw_dt = mm_tn("mix_dw_dt", h2, ddt, D_MODEL, 2 * LANE, tm, BF16)
    dw_dt8 = jnp.concatenate([dw_dt[:, LANE * g:LANE * g + HEADS_PER_GROUP] for g in range(SSD_GROUPS)], axis=1)
    d_win = jnp.concatenate([dw_z, dw_xbc, dw_dt8, dw_u], axis=1)
    (dx1,), (dsc2, dsh2), _ = rowwise_bwd("mix_mod_b", f_modulate, [x1], [sc2, sh2], [], [dh2], seq, tm, [F32],
                                          add_rows={0: dx1_a})

    dx0, dmod1, (d_f1w1, d_f1w3, d_f1w2, d_ln1g, d_ln1b), (s_win, s_glu, s_out) = _ffn_bwd(
        "ffn1", dx1, x0, sc1, sh1, g1, f1w1, f1w3, f1w2, ln1_g, ln1_b, res1, seq, tm,
        ride=([_full_to_cols(d_win), d_wglu.reshape(N_DEV, S5_WIDTH // N_DEV, S5_WIDTH).astype(BF16),
               d_wo.reshape(N_DEV, D_MODEL // N_DEV, D_MODEL)], [False] * 3))

    dmod = jnp.concatenate(list(dmod1) + [dsh2, dsc2, dg2] + list(dmod3), axis=1).reshape(bsz, N_MOD * D_MODEL)
    small = {
        "ln1_g": d_ln1g, "ln1_b": d_ln1b, "conv_w": d_convw, "conv_b": d_convb,
        "dt_bias": ddtb[:, 0, :HEADS_PER_GROUP].reshape(1, SSD_HEADS),
        "a_log": dalog[:, 0, :HEADS_PER_GROUP].reshape(1, SSD_HEADS),
        "d_ssd": ddcol[:, :HEADS_PER_GROUP, 0].reshape(1, SSD_HEADS),
        "ssd_norm_w": dnw.reshape(1, SSD_WIDTH),
        "s5_a_re": d_a_re[None], "s5_a_im": d_a_im[None], "s5_log_dt": d_ldt.reshape(1, S5_GROUPS),
        "s5_b_re": d_b_re.reshape(s5_b_re.shape), "s5_b_im": d_b_im.reshape(s5_b_im.shape),
        "s5_c_re": d_c_re[None], "s5_c_im": d_c_im[None], "s5_d": dd5.reshape(1, S5_WIDTH),
        "b_glu": d_bglu, "ln2_g": d_ln2g, "ln2_b": d_ln2b, "ln3_g": d_ln3g, "ln3_b": d_ln3b,
    }
    names = list(small)
    n_small = sum(math.prod(small[k].shape) for k in names)
    rows = -(-n_small // (8 * LANE)) * 8
    packed = _pack([small[k] for k in names], rows)
    s_col1, s_row1, s_small, s_dmod = exchange(
        "sum_grads",
        [jnp.stack([_full_to_cols(d_f1w1), _full_to_cols(d_f1w3)], axis=1),
         d_f1w2.reshape(N_DEV, D_FF // N_DEV, D_MODEL), packed, dmod],
        [False] * 2 + [True] * 2)

    out = {"loss": lax.psum(loss_loc, ("x", "y", "c")), "grad_x": dx0.reshape(x.shape)}

    def put(name, res, shape):
        for key, val in zip(("grad_", "delta_", "new_m_", "new_v_"), res):
            out[key + name] = val.reshape(shape)

    for name, slots, k in (("ffn1_w1", s_col1, 0), ("ffn1_w3", s_col1, 1), ("ffn2_w1", s_col2, 0), ("ffn2_w3", s_col2, 1)):
        w = given[name]
        put(name, adamw("adam_" + name, slots, w[0], given["m_" + name][0], given["v_" + name][0], 256, sel=k), w.shape)
    for name, slots in (("ffn1_w2", s_row1), ("ffn2_w2", s_row2)):
        w = given[name]
        put(name, adamw("adam_" + name, slots, w[0], given["m_" + name][0], given["v_" + name][0], 176), w.shape)
    put("w_in", adamw("adam_w_in", s_win, w_in[0], m_w_in[0], v_w_in[0], 256), w_in.shape)
    put("w_glu", adamw("adam_w_glu", s_glu, w_glu[0], m_w_glu[0], v_w_glu[0], 64), w_glu.shape)
    put("w_out", adamw("adam_w_out", s_out, w_out[0], m_w_out[0], v_w_out[0], 128), w_out.shape)

    dmod_all = s_dmod.reshape(N_DEV * bsz, N_MOD * D_MODEL)
    g_bada, g_wada = ada_bwd(c_all, dmod_all, lax.dynamic_slice(dmod_all, (0, me * n_loc), (N_DEV * bsz, n_loc)))
    put("w_ada", adamw("adam_w_ada", g_wada, w_ada[0], m_w_ada[0], v_w_ada[0], 256), w_ada.shape)
    put("b_ada", adamw("adam_b_ada", g_bada, b_ada, m_b_ada, v_b_ada, 1), b_ada.shape)

    zero_cw = jnp.zeros((CONV_K, CONV_CH), F32)
    pw, pm, pv = [_pack([zero_cw if k == "conv_w" else given[pre + k] for k in names], rows) for pre in ("", "m_", "v_")]
    res_small = adamw("adam_small", s_small, pw, pm, pv, rows)
    parts = [_unpack(r, [small[k].shape for k in names]) for r in res_small]
    for i, k in enumerate(names):
        if k != "conv_w":
            put(k, [p[i] for p in parts], given[k].shape)
    g_cw = lax.dynamic_slice(parts[0][names.index("conv_w")], (0, me * LANE), (CONV_K, LANE))
    put("conv_w", adamw("adam_conv_w", g_cw, conv_w[0], m_conv_w[0], v_conv_w[0], CONV_K), conv_w.shape)

    order = ["w_ada", "b_ada", "ffn1_w1", "ffn1_w3", "ffn1_w2", "ln1_g", "ln1_b", "w_in", "conv_w", "conv_b", "dt_bias",
             "a_log", "d_ssd", "ssd_norm_w", "s5_a_re", "s5_a_im", "s5_log_dt", "s5_b_re", "s5_b_im", "s5_c_re",
             "s5_c_im", "s5_d", "w_glu", "b_glu", "w_out", "ln2_g", "ln2_b", "ffn2_w1", "ffn2_w3", "ffn2_w2", "ln3_g",
             "ln3_b"]
    return (out["loss"], out["grad_x"], *[out[p + n] for p in ("grad_", "delta_", "new_m_", "new_v_") for n in order])
```

```python
import functools
import math

import jax
import jax.numpy as jnp
from jax import lax
from jax.experimental import pallas as pl
from jax.experimental.pallas import tpu as pltpu

F32 = jnp.float32
BF16 = jnp.bfloat16
HI = lax.Precision.HIGHEST
MESH = pl.DeviceIdType.MESH

N_DEV = 8
D_MODEL = 1024
D_FF = 2816
N_MOD = 9
SSD_WIDTH = 512
SSD_HEADS = 8
SSD_HEAD_DIM = 64
SSD_GROUPS = 2
SSD_STATE = 128
SSD_CHUNK = 128
GROUP_COLS = SSD_WIDTH // SSD_GROUPS
HEADS_PER_GROUP = SSD_HEADS // SSD_GROUPS
CONV_K = 4
CONV_CH = 1024
S5_WIDTH = 512
S5_GROUPS = 32
S5_GROUP_CH = 16
S5_STATE = 64
S5_COLS = S5_GROUPS * S5_STATE
S5_Q = 4
S5_CHUNK = 256
ALPHA = 2.0 ** 0.25
LN_EPS = 1e-5
LANE = 128
HALO = 8

P_XBC, P_Z, P_U, P_DT = 0, 1024, 1536, 2048
P_COLS = 2048 + SSD_GROUPS * LANE

ADAM_LR, ADAM_B1, ADAM_B2, ADAM_EPS, ADAM_WD, ADAM_STEP = 0.001, 0.9, 0.999, 1e-08, 0.01, 10

VMEM_LIMIT = 56 * 1024 * 1024


def _cp(*sem):
    return pltpu.CompilerParams(dimension_semantics=sem if sem else None, vmem_limit_bytes=VMEM_LIMIT)


def _dg(a, b, ca, cb):
    return lax.dot_general(a.astype(BF16), b.astype(BF16), (((ca,), (cb,)), ((), ())), preferred_element_type=F32)


@jax.custom_vjp
def bdot_nn(a, b):
    return _dg(a, b, 1, 0)


bdot_nn.defvjp(lambda a, b: (_dg(a, b, 1, 0), (a, b)),
               lambda r, g: (_dg(g, r[1], 1, 1), _dg(r[0], g, 0, 0)))


@jax.custom_vjp
def bdot_nt(a, b):
    return _dg(a, b, 1, 1)


bdot_nt.defvjp(lambda a, b: (_dg(a, b, 1, 1), (a, b)),
               lambda r, g: (_dg(g, r[1], 1, 0), _dg(g, r[0], 0, 0)))


@jax.custom_vjp
def bdot_tn(a, b):
    return _dg(a, b, 0, 0)


bdot_tn.defvjp(lambda a, b: (_dg(a, b, 0, 0), (a, b)),
               lambda r, g: (_dg(r[1], g, 1, 1), _dg(r[0], g, 1, 0)))


def _take_col(z):
    @jax.custom_vjp
    def take(x):
        return x[:, z:z + 1]

    def bwd(shape, g):
        hot = (lax.broadcasted_iota(jnp.int32, (1, shape[1]), 1) == z).astype(F32)
        return (g * hot,)

    take.defvjp(lambda x: (x[:, z:z + 1], x.shape), bwd)
    return take


def _take_row(z):
    @jax.custom_vjp
    def take(x):
        return x[z:z + 1, :]

    def bwd(shape, g):
        hot = (lax.broadcasted_iota(jnp.int32, (shape[0], 1), 0) == z).astype(F32)
        return (hot * g,)

    take.defvjp(lambda x: (x[z:z + 1, :], x.shape), bwd)
    return take


def _view(a):
    return a if isinstance(a, tuple) else (a, 0, a.shape[1])


def _col_spec(view, rows, width, index):
    _, off, _ = view
    assert off % width == 0
    return pl.BlockSpec((rows, width), lambda *g: (index(*g)[0], off // width + index(*g)[1]))


def _rw_in_specs(rows, bps, gps, tm, tps):
    specs = [_col_spec(_view(r), tm, _view(r)[2], lambda i: (i, 0)) for r in rows]
    specs += [pl.BlockSpec((1, 1, b.shape[2]), lambda i: (i // tps, 0, 0)) for b in bps]
    specs += [pl.BlockSpec(g.shape, lambda i, nd=g.ndim: (0,) * nd) for g in gps]
    return specs


def _rw_vals(refs, nr, nb, ng):
    vals = [r[...] for r in refs[:nr]]
    vals += [b[0] for b in refs[nr:nr + nb]]
    vals += [g[...] for g in refs[nr + nb:nr + nb + ng]]
    return vals


def rowwise_fwd(name, f, rows, bps, gps, outs, seq, tm):
    t = _view(rows[0])[0].shape[0]
    tps = seq // tm
    nr, nb, ng = len(rows), len(bps), len(gps)

    def body(*refs):
        res = f(*_rw_vals(refs, nr, nb, ng))
        for o, v in zip(refs[nr + nb + ng:], res):
            o[...] = v.astype(o.dtype)

    return pl.pallas_call(
        body, name=name, grid=(t // tm,),
        in_specs=_rw_in_specs(rows, bps, gps, tm, tps),
        out_specs=[pl.BlockSpec((tm, c), lambda i: (i, 0)) for c, _ in outs],
        out_shape=[jax.ShapeDtypeStruct((t, c), d) for c, d in outs],
        compiler_params=_cp("arbitrary"),
    )(*[_view(r)[0] for r in rows], *bps, *gps)


def rowwise_bwd(name, f, rows, bps, gps, douts, seq, tm, row_grads, add_rows=None):
    add_rows = add_rows or {}
    t = _view(rows[0])[0].shape[0]
    tps = seq // tm
    nr, nb, ng, nd = len(rows), len(bps), len(gps), len(douts)
    want = [k for k in range(nr) if row_grads[k] is not None]
    adds = sorted(add_rows)
    n_in = nr + nb + ng + nd + len(adds)

    def body(*refs):
        vals = _rw_vals(refs, nr, nb, ng)
        dvals = tuple(r[...] for r in refs[nr + nb + ng:nr + nb + ng + nd])
        add_refs = dict(zip(adds, refs[nr + nb + ng + nd:n_in]))
        out_refs = refs[n_in:]
        _, pull = jax.vjp(f, *vals)
        grads = pull(dvals)
        i = pl.program_id(0)
        for o, k in zip(out_refs, want):
            g = grads[k]
            if k in add_refs:
                g = g + add_refs[k][...]
            o[...] = g.astype(o.dtype)
        for j in range(nb):
            o = out_refs[len(want) + j]

            @pl.when(i % tps == 0)
            def _(o=o):
                o[...] = jnp.zeros_like(o)

            o[0] = o[0] + grads[nr + j]
        for j in range(ng):
            o = out_refs[len(want) + nb + j]

            @pl.when(i == 0)
            def _(o=o):
                o[...] = jnp.zeros_like(o)

            o[...] = o[...] + grads[nr + nb + j]

    in_specs = _rw_in_specs(rows, bps, gps, tm, tps)
    in_specs += [pl.BlockSpec((tm, d.shape[1]), lambda i: (i, 0)) for d in douts]
    in_specs += [pl.BlockSpec((tm, add_rows[k].shape[1]), lambda i: (i, 0)) for k in adds]
    out_specs = [pl.BlockSpec((tm, _view(rows[k])[2]), lambda i: (i, 0)) for k in want]
    out_shape = [jax.ShapeDtypeStruct((t, _view(rows[k])[2]), row_grads[k]) for k in want]
    out_specs += [pl.BlockSpec((1, 1, b.shape[2]), lambda i: (i // tps, 0, 0)) for b in bps]
    out_shape += [jax.ShapeDtypeStruct(b.shape, F32) for b in bps]
    out_specs += [pl.BlockSpec(g.shape, lambda i, n=g.ndim: (0,) * n) for g in gps]
    out_shape += [jax.ShapeDtypeStruct(g.shape, F32) for g in gps]
    res = pl.pallas_call(
        body, name=name, grid=(t // tm,), in_specs=in_specs, out_specs=out_specs, out_shape=out_shape,
        compiler_params=_cp("arbitrary"),
    )(*[_view(r)[0] for r in rows], *bps, *gps, *douts, *[add_rows[k] for k in adds])
    nw = len(want)
    return res[:nw], res[nw:nw + nb], res[nw + nb:]


def mm_nn(name, xs, ws, tm, tn, out_dtype=F32):
    views = [_view(x) for x in xs]
    t, n, k = views[0][0].shape[0], ws[0].shape[1], len(xs)

    def body(*refs):
        acc = _dg(refs[0][...], refs[k][...], 1, 0)
        for i in range(1, k):
            acc = acc + _dg(refs[i][...], refs[k + i][...], 1, 0)
        refs[2 * k][...] = acc.astype(out_dtype)

    in_specs = [_col_spec(v, tm, v[2], lambda i, j: (i, 0)) for v in views]
    in_specs += [pl.BlockSpec((w.shape[0], tn), lambda i, j: (0, j)) for w in ws]
    return pl.pallas_call(
        body, name=name, grid=(t // tm, n // tn), in_specs=in_specs,
        out_specs=pl.BlockSpec((tm, tn), lambda i, j: (i, j)),
        out_shape=jax.ShapeDtypeStruct((t, n), out_dtype),
        compiler_params=_cp("parallel", "parallel"),
    )(*[v[0] for v in views], *ws)


def mm_nt(name, dys, ws, tm, tk, out_dtype=F32):
    views = [_view(d) for d in dys]
    t, kk, k = views[0][0].shape[0], ws[0].shape[0], len(dys)

    def body(*refs):
        acc = _dg(refs[0][...], refs[k][...], 1, 1)
        for i in range(1, k):
            acc = acc + _dg(refs[i][...], refs[k + i][...], 1, 1)
        refs[2 * k][...] = acc.astype(out_dtype)

    in_specs = [_col_spec(v, tm, v[2], lambda i, j: (i, 0)) for v in views]
    in_specs += [pl.BlockSpec((tk, w.shape[1]), lambda i, j: (j, 0)) for w in ws]
    return pl.pallas_call(
        body, name=name, grid=(t // tm, kk // tk), in_specs=in_specs,
        out_specs=pl.BlockSpec((tm, tk), lambda i, j: (i, j)),
        out_shape=jax.ShapeDtypeStruct((t, kk), out_dtype),
        compiler_params=_cp("parallel", "parallel"),
    )(*[v[0] for v in views], *ws)


def mm_tn(name, x, dy, tk, tn, tt, out_dtype=F32, ride=None):
    xv, dv = _view(x), _view(dy)
    t, kk, n = xv[0].shape[0], xv[2], dv[2]
    steps = t // tt

    def body(x_ref, d_ref, o_ref, acc_ref):
        @pl.when(pl.program_id(2) == 0)
        def _():
            acc_ref[...] = jnp.zeros_like(acc_ref)

        acc_ref[...] += _dg(x_ref[...], d_ref[...], 0, 0)

        @pl.when(pl.program_id(2) == steps - 1)
        def _():
            o_ref[...] = acc_ref[...].astype(out_dtype)

    in_specs = [_col_spec(xv, tt, tk, lambda a, b, c: (c, a)), _col_spec(dv, tt, tn, lambda a, b, c: (c, b))]
    out_spec = pl.BlockSpec((tk, tn), lambda a, b, c: (a, b))
    out_shape = jax.ShapeDtypeStruct((kk, n), out_dtype)
    if ride is not None:
        (res,), landed = hosted_call(body, name=name, grid=(kk // tk, n // tn, steps), in_specs=in_specs,
                                     out_specs=[out_spec], out_shape=[out_shape], scratch=[pltpu.VMEM((tk, tn), F32)],
                                     args=(xv[0], dv[0]), ride=ride)
        return res, landed
    return pl.pallas_call(
        body, name=name, grid=(kk // tk, n // tn, steps), in_specs=in_specs, out_specs=out_spec, out_shape=out_shape,
        scratch_shapes=[pltpu.VMEM((tk, tn), F32)],
        compiler_params=_cp("parallel", "parallel", "arbitrary"),
    )(xv[0], dv[0])


def _silu(x):
    return x * jax.nn.sigmoid(x)


def f_modulate(x, sc, sh):
    return (x * (1.0 + sc) + sh,)


def _res_ln(coef):
    def f(x, y, g, lg, lb):
        r = ALPHA * x + (coef * g) * y
        mu = jnp.mean(r, axis=-1, keepdims=True)
        d = r - mu
        var = jnp.mean(d * d, axis=-1, keepdims=True)
        return (d * lax.rsqrt(var + LN_EPS) * lg + lb,)
    return f


def f_glu(y, w, b):
    g = jax.nn.gelu(y)
    return (g * jax.nn.sigmoid(bdot_nn(g, w) + b),)


def loss_head(y, target, tm):
    t, d = y.shape

    def body(y_ref, t_ref, dy_ref, l_ref):
        @pl.when(pl.program_id(0) == 0)
        def _():
            l_ref[...] = jnp.zeros_like(l_ref)

        e = y_ref[...] - t_ref[...]
        dy_ref[...] = e * (1.0 / d)
        l_ref[...] += 0.5 * jnp.sum(jnp.mean(e * e, axis=-1, keepdims=True), axis=0, keepdims=True)

    dy, l = pl.pallas_call(
        body, name="loss_head", grid=(t // tm,),
        in_specs=[pl.BlockSpec((tm, d), lambda i: (i, 0))] * 2,
        out_specs=[pl.BlockSpec((tm, d), lambda i: (i, 0)), pl.BlockSpec((1, 1), lambda i: (0, 0))],
        out_shape=[jax.ShapeDtypeStruct((t, d), F32), jax.ShapeDtypeStruct((1, 1), F32)],
        compiler_params=_cp("arbitrary"),
    )(y, target)
    return dy, l[0, 0]


def _shift_down(x, halo, k):
    if k == 0:
        return x
    r = pltpu.roll(x, k, 0)
    hr = pltpu.roll(halo, k, 0)
    row = lax.broadcasted_iota(jnp.int32, (HALO, 1), 0)
    top = jnp.where(row < k, hr, r[:HALO])
    return jnp.concatenate([top, r[HALO:]], axis=0)


def _shift_up(x, halo, k):
    if k == 0:
        return x
    n = x.shape[0]
    r = pltpu.roll(x, n - k, 0)
    hr = pltpu.roll(halo, HALO - k, 0)
    row = lax.broadcasted_iota(jnp.int32, (HALO, 1), 0)
    bot = jnp.where(row >= HALO - k, hr, r[n - HALO:])
    return jnp.concatenate([r[:n - HALO], bot], axis=0)


def _conv_pre(x, halo, w, b):
    acc = x * w[CONV_K - 1:CONV_K, :] + b
    for k in range(1, CONV_K):
        acc = acc + _shift_down(x, halo, k) * w[CONV_K - 1 - k:CONV_K - k, :]
    return acc


def _rows_before(width, tm):
    return pl.BlockSpec((HALO, width), lambda i: (jnp.maximum(i * (tm // HALO) - 1, 0), 0))


def conv_fwd(proj, w, b, seq, tm):
    t = proj.shape[0]
    tps = seq // tm

    def body(x_ref, h_ref, w_ref, b_ref, o_ref):
        first = (pl.program_id(0) % tps == 0)
        halo = jnp.where(first, 0.0, h_ref[...])
        o_ref[...] = _silu(_conv_pre(x_ref[...], halo, w_ref[...], b_ref[...]))

    return pl.pallas_call(
        body, name="conv_fwd", grid=(t // tm,),
        in_specs=[pl.BlockSpec((tm, CONV_CH), lambda i: (i, 0)), _rows_before(CONV_CH, tm),
                  pl.BlockSpec((CONV_K, CONV_CH), lambda i: (0, 0)), pl.BlockSpec((1, CONV_CH), lambda i: (0, 0))],
        out_specs=pl.BlockSpec((tm, CONV_CH), lambda i: (i, 0)),
        out_shape=jax.ShapeDtypeStruct((t, CONV_CH), F32),
        compiler_params=_cp("arbitrary"),
    )(proj, proj, w, b)


def conv_bwd_pre(proj, w, b, dxs, dbm, dcm, seq, tm):
    t = proj.shape[0]
    tps = seq // tm

    def body(x_ref, h_ref, w_ref, b_ref, d1, d2, d3, dp_ref, dw_ref, db_ref):
        i = pl.program_id(0)
        halo = jnp.where(i % tps == 0, 0.0, h_ref[...])
        x = x_ref[...]
        pre = _conv_pre(x, halo, w_ref[...], b_ref[...])
        sg = jax.nn.sigmoid(pre)
        dout = jnp.concatenate([d1[...], d2[...], d3[...]], axis=1)
        dp = dout * (sg * (1.0 + pre * (1.0 - sg)))
        dp_ref[...] = dp

        @pl.when(i == 0)
        def _():
            dw_ref[...] = jnp.zeros_like(dw_ref)
            db_ref[...] = jnp.zeros_like(db_ref)

        db_ref[...] += jnp.sum(dp, axis=0, keepdims=True)
        for k in range(CONV_K):
            j = CONV_K - 1 - k
            dw_ref[j:j + 1, :] += jnp.sum(dp * _shift_down(x, halo, k), axis=0, keepdims=True)

    return pl.pallas_call(
        body, name="conv_bwd_pre", grid=(t // tm,),
        in_specs=[pl.BlockSpec((tm, CONV_CH), lambda i: (i, 0)), _rows_before(CONV_CH, tm),
                  pl.BlockSpec((CONV_K, CONV_CH), lambda i: (0, 0)), pl.BlockSpec((1, CONV_CH), lambda i: (0, 0)),
                  pl.BlockSpec((tm, 512), lambda i: (i, 0)), pl.BlockSpec((tm, 256), lambda i: (i, 0)),
                  pl.BlockSpec((tm, 256), lambda i: (i, 0))],
        out_specs=[pl.BlockSpec((tm, CONV_CH), lambda i: (i, 0)), pl.BlockSpec((CONV_K, CONV_CH), lambda i: (0, 0)),
                   pl.BlockSpec((1, CONV_CH), lambda i: (0, 0))],
        out_shape=[jax.ShapeDtypeStruct((t, CONV_CH), F32), jax.ShapeDtypeStruct((CONV_K, CONV_CH), F32),
                   jax.ShapeDtypeStruct((1, CONV_CH), F32)],
        compiler_params=_cp("arbitrary"),
    )(proj, proj, w, b, dxs, dbm, dcm)


def conv_bwd_x(dpre, w, seq, tm):
    t = dpre.shape[0]
    tps = seq // tm
    blocks = tm // HALO
    last = t // HALO - 1

    def body(d_ref, h_ref, w_ref, o_ref):
        halo = jnp.where(pl.program_id(0) % tps == tps - 1, 0.0, h_ref[...])
        d = d_ref[...]
        w = w_ref[...]
        acc = d * w[CONV_K - 1:CONV_K, :]
        for k in range(1, CONV_K):
            acc = acc + _shift_up(d, halo, k) * w[CONV_K - 1 - k:CONV_K - k, :]
        o_ref[...] = acc

    return pl.pallas_call(
        body, name="conv_bwd_x", grid=(t // tm,),
        in_specs=[pl.BlockSpec((tm, CONV_CH), lambda i: (i, 0)),
                  pl.BlockSpec((HALO, CONV_CH), lambda i: (jnp.minimum((i + 1) * blocks, last), 0)),
                  pl.BlockSpec((CONV_K, CONV_CH), lambda i: (0, 0))],
        out_specs=pl.BlockSpec((tm, CONV_CH), lambda i: (i, 0)),
        out_shape=jax.ShapeDtypeStruct((t, CONV_CH), F32),
        compiler_params=_cp("arbitrary"),
    )(dpre, dpre, w)


def _softplus(x):
    return jnp.maximum(x, 0.0) + jnp.log1p(jnp.exp(-jnp.abs(x)))


def _ssd_chunk(xs, bg, cg, dtr, zz, hp, dtb, alog, dcol, nw):
    l = xs.shape[0]
    row = lax.broadcasted_iota(jnp.int32, (l, l), 0)
    col = lax.broadcasted_iota(jnp.int32, (l, l), 1)
    causal = row >= col
    tril = causal.astype(F32)
    expand = (lax.broadcasted_iota(jnp.int32, (LANE, GROUP_COLS), 1) // SSD_HEAD_DIM
              == lax.broadcasted_iota(jnp.int32, (LANE, GROUP_COLS), 0)).astype(F32)
    head_of_col = lax.broadcasted_iota(jnp.int32, (1, GROUP_COLS), 1) // SSD_HEAD_DIM
    last_row = (lax.broadcasted_iota(jnp.int32, (l, 1), 0) == l - 1).astype(F32)

    dtc = _softplus(dtr + dtb)
    a_c = dtc * (-jnp.exp(alog))
    acs_c = jnp.dot(tril, a_c, precision=HI, preferred_element_type=F32)
    dt_e = jnp.dot(dtc, expand, precision=HI, preferred_element_type=F32)
    acs_e = jnp.dot(acs_c, expand, precision=HI, preferred_element_type=F32)
    alast_e = jnp.sum(acs_e * last_row, axis=0, keepdims=True)
    x = xs * dt_e
    states = bdot_tn(bg, x * jnp.exp(alast_e - acs_e))
    h_next = jnp.exp(alast_e) * hp + states
    d_e = jnp.sum(dcol * expand, axis=0, keepdims=True)
    y = bdot_nn(cg, hp) * jnp.exp(acs_e) + d_e * xs
    cb = bdot_nt(cg, bg)
    acs_t = acs_c.T
    for z in range(HEADS_PER_GROUP):
        seg = _take_col(z)(acs_c) - _take_row(z)(acs_t)
        lmat = jnp.exp(jnp.where(causal, seg, -1e30))
        y = y + bdot_nn(cb * lmat, x * (head_of_col == z).astype(F32))
    yz = y * _silu(zz)
    ms = jnp.mean(yz * yz, axis=-1, keepdims=True)
    return yz * lax.rsqrt(ms + LN_EPS) * nw, h_next


def _ssd_in_specs(nc, rev):
    def tok(b, c):
        return b * nc + (nc - 1 - c if rev else c)

    whole = lambda *shape: pl.BlockSpec(shape, lambda b, c: (0,) * len(shape))
    both = SSD_GROUPS * SSD_STATE
    return [
        pl.BlockSpec((SSD_CHUNK, SSD_WIDTH), lambda b, c: (tok(b, c), 0)),
        pl.BlockSpec((SSD_CHUNK, both), lambda b, c: (tok(b, c), SSD_WIDTH // both)),
        pl.BlockSpec((SSD_CHUNK, both), lambda b, c: (tok(b, c), SSD_WIDTH // both + 1)),
        pl.BlockSpec((SSD_CHUNK, SSD_GROUPS * LANE), lambda b, c: (tok(b, c), P_DT // (SSD_GROUPS * LANE))),
        pl.BlockSpec((SSD_CHUNK, SSD_WIDTH), lambda b, c: (tok(b, c), P_Z // SSD_WIDTH)),
        whole(SSD_GROUPS, 1, LANE), whole(SSD_GROUPS, 1, LANE), whole(SSD_GROUPS, LANE, 1),
        whole(SSD_GROUPS, 1, GROUP_COLS),
    ], tok


def _group(ref, g, width):
    return ref[:, g * width:(g + 1) * width]


def ssd_fwd(xc, proj, dtb, alog, dcol, nw, bsz, seq):
    t = xc.shape[0]
    nc = seq // SSD_CHUNK
    in_specs, tok = _ssd_in_specs(nc, False)

    def body(xs, bm, cm, dtr, zz, dtb_r, alog_r, dcol_r, nw_r, y_ref, hp_ref, h_scr):
        @pl.when(pl.program_id(1) == 0)
        def _():
            h_scr[...] = jnp.zeros_like(h_scr)

        for g in range(SSD_GROUPS):
            hp = h_scr[g]
            hp_ref[g, 0, 0] = hp
            y, hn = _ssd_chunk(_group(xs, g, GROUP_COLS), _group(bm, g, SSD_STATE), _group(cm, g, SSD_STATE),
                               _group(dtr, g, LANE), _group(zz, g, GROUP_COLS), hp,
                               dtb_r[g], alog_r[g], dcol_r[g], nw_r[g])
            y_ref[:, g * GROUP_COLS:(g + 1) * GROUP_COLS] = y
            h_scr[g] = hn

    return pl.pallas_call(
        body, name="ssd_fwd", grid=(bsz, nc), in_specs=in_specs,
        out_specs=[pl.BlockSpec((SSD_CHUNK, SSD_WIDTH), lambda b, c: (tok(b, c), 0)),
                   pl.BlockSpec((SSD_GROUPS, 1, 1, SSD_STATE, GROUP_COLS), lambda b, c: (0, b, c, 0, 0))],
        out_shape=[jax.ShapeDtypeStruct((t, SSD_WIDTH), F32),
                   jax.ShapeDtypeStruct((SSD_GROUPS, bsz, nc, SSD_STATE, GROUP_COLS), F32)],
        scratch_shapes=[pltpu.VMEM((SSD_GROUPS, SSD_STATE, GROUP_COLS), F32)],
        compiler_params=_cp("arbitrary", "arbitrary"),
    )(xc, xc, xc, proj, proj, dtb, alog, dcol, nw)


def ssd_bwd(xc, proj, dtb, alog, dcol, nw, hprev, dy, bsz, seq):
    t = xc.shape[0]
    nc = seq // SSD_CHUNK
    in_specs, tok = _ssd_in_specs(nc, True)
    in_specs += [pl.BlockSpec((SSD_GROUPS, 1, 1, SSD_STATE, GROUP_COLS), lambda b, c: (0, b, nc - 1 - c, 0, 0)),
                 pl.BlockSpec((SSD_CHUNK, SSD_WIDTH), lambda b, c: (tok(b, c), 0))]

    def body(xs, bm, cm, dtr, zz, dtb_r, alog_r, dcol_r, nw_r, hp_ref, dy_ref,
             dxs, dbm, dcm, ddt, dzz, ddtb, dalog, ddcol, dnw, dh_scr):
        b, c = pl.program_id(0), pl.program_id(1)

        @pl.when(c == 0)
        def _():
            dh_scr[...] = jnp.zeros_like(dh_scr)

        @pl.when((b == 0) & (c == 0))
        def _():
            for r in (ddtb, dalog, ddcol, dnw):
                r[...] = jnp.zeros_like(r)

        for g in range(SSD_GROUPS):
            wide = slice(g * GROUP_COLS, (g + 1) * GROUP_COLS)
            state = slice(g * SSD_STATE, (g + 1) * SSD_STATE)
            _, pull = jax.vjp(_ssd_chunk, xs[:, wide], bm[:, state], cm[:, state], _group(dtr, g, LANE), zz[:, wide],
                              hp_ref[g, 0, 0], dtb_r[g], alog_r[g], dcol_r[g], nw_r[g])
            d = pull((dy_ref[:, wide], dh_scr[g]))
            dxs[:, wide], dbm[:, state], dcm[:, state], dzz[:, wide] = d[0], d[1], d[2], d[4]
            ddt[:, g * LANE:(g + 1) * LANE] = d[3]
            dh_scr[g] = d[5]
            ddtb[g] += d[6]
            dalog[g] += d[7]
            ddcol[g] += d[8]
            dnw[g] += d[9]

    def tile(w):
        return pl.BlockSpec((SSD_CHUNK, w), lambda b, c: (tok(b, c), 0))

    whole = lambda *shape: pl.BlockSpec(shape, lambda b, c: (0,) * len(shape))
    return pl.pallas_call(
        body, name="ssd_bwd", grid=(bsz, nc), in_specs=in_specs,
        out_specs=[tile(SSD_WIDTH), tile(2 * SSD_STATE), tile(2 * SSD_STATE), tile(2 * LANE), tile(SSD_WIDTH),
                   whole(SSD_GROUPS, 1, LANE), whole(SSD_GROUPS, 1, LANE), whole(SSD_GROUPS, LANE, 1),
                   whole(SSD_GROUPS, 1, GROUP_COLS)],
        out_shape=[jax.ShapeDtypeStruct((t, SSD_WIDTH), F32), jax.ShapeDtypeStruct((t, 2 * SSD_STATE), F32),
                   jax.ShapeDtypeStruct((t, 2 * SSD_STATE), F32), jax.ShapeDtypeStruct((t, 2 * LANE), F32),
                   jax.ShapeDtypeStruct((t, SSD_WIDTH), F32),
                   jax.ShapeDtypeStruct((SSD_GROUPS, 1, LANE), F32), jax.ShapeDtypeStruct((SSD_GROUPS, 1, LANE), F32),
                   jax.ShapeDtypeStruct((SSD_GROUPS, LANE, 1), F32),
                   jax.ShapeDtypeStruct((SSD_GROUPS, 1, GROUP_COLS), F32)],
        scratch_shapes=[pltpu.VMEM((SSD_GROUPS, SSD_STATE, GROUP_COLS), F32)],
        compiler_params=_cp("arbitrary", "arbitrary"),
    )(xc, xc, xc, proj, proj, dtb, alog, dcol, nw, hprev, dy)


def _disc_a(a_re, a_im, log_dt):
    dt = jnp.exp(log_dt)
    mag = jnp.exp(dt * a_re)
    ab_re, ab_im = mag * jnp.cos(dt * a_im), mag * jnp.sin(dt * a_im)
    den = a_re * a_re + a_im * a_im
    nr, ni = ab_re - 1.0, ab_im
    f_re, f_im = (nr * a_re + ni * a_im) / den, (ni * a_re - nr * a_im) / den
    return ab_re, ab_im, f_re, f_im


def _disc_b(f_re, f_im, b_re, b_im):
    return f_re * b_re - f_im * b_im, f_re * b_im + f_im * b_re


def _whole(f, name, args, outs):
    def body(*refs):
        res = f(*[r[...] for r in refs[:len(args)]])
        for o, v in zip(refs[len(args):], res):
            o[...] = v

    return pl.pallas_call(body, name=name, out_shape=[jax.ShapeDtypeStruct(s, F32) for s in outs])(*args)


def _whole_vjp(f, name, args, cts):
    def body(*refs):
        vals = [r[...] for r in refs[:len(args)]]
        _, pull = jax.vjp(f, *vals)
        res = pull(tuple(r[...] for r in refs[len(args):len(args) + len(cts)]))
        for o, v in zip(refs[len(args) + len(cts):], res):
            o[...] = v

    return pl.pallas_call(body, name=name, out_shape=[jax.ShapeDtypeStruct(a.shape, F32) for a in args])(*args, *cts)


S5_SUB = 8
S5_STEPS = 3


def s5_tables(lam_re, lam_im):
    rows = S5_STEPS * S5_SUB

    def body(lr_ref, li_ref, sf_re, sf_im, sb_re, sb_im, cf_re, cf_im, cb_re, cb_im):
        lr, li = lr_ref[...], li_ref[...]

        def power(k):
            m = jnp.exp(k * lr)
            return m * jnp.cos(k * li), m * jnp.sin(k * li)

        srow = lax.broadcasted_iota(jnp.int32, (rows, 1), 0)
        k = jnp.left_shift(1, srow // S5_SUB)
        tt = srow % S5_SUB
        pr, pi = power(k.astype(F32))
        fwd, bwd = tt >= k, tt < S5_SUB - k
        sf_re[...], sf_im[...] = jnp.where(fwd, pr, 0.0), jnp.where(fwd, pi, 0.0)
        sb_re[...], sb_im[...] = jnp.where(bwd, pr, 0.0), jnp.where(bwd, pi, 0.0)
        trow = lax.broadcasted_iota(jnp.int32, (S5_SUB, 1), 0)
        cf_re[...], cf_im[...] = power((trow + 1).astype(F32))
        cb_re[...], cb_im[...] = power((S5_SUB - trow).astype(F32))

    shp = [jax.ShapeDtypeStruct((rows, S5_COLS), F32)] * 4 + [jax.ShapeDtypeStruct((S5_SUB, S5_COLS), F32)] * 4
    return pl.pallas_call(body, name="s5_tables", out_shape=shp)(lam_re, lam_im)


def _s5_coefs(steps_re, steps_im, carry_re, carry_im, reverse):
    sign = -1.0 if reverse else 1.0
    steps = [(steps_re[s * S5_SUB:(s + 1) * S5_SUB, :], sign * steps_im[s * S5_SUB:(s + 1) * S5_SUB, :])
             for s in range(S5_STEPS)]
    return steps, (carry_re[...], sign * carry_im[...])


def _s5_block_scan(ar, ai, coefs, cr, ci, reverse):
    steps, (qr, qi) = coefs
    for s, (pr, pi) in enumerate(steps):
        shift = S5_SUB - (1 << s) if reverse else (1 << s)
        sr, si = pltpu.roll(ar, shift, 0), pltpu.roll(ai, shift, 0)
        ar, ai = ar + pr * sr - pi * si, ai + pr * si + pi * sr
    br, bi = jnp.broadcast_to(cr, ar.shape), jnp.broadcast_to(ci, ai.shape)
    return ar + qr * br - qi * bi, ai + qr * bi + qi * br


def _s5_specs(n5, rev):
    def tok(q, b, c):
        return b * n5 + (n5 - 1 - c if rev else c)

    qcols = S5_COLS // S5_Q
    specs = [
        pl.BlockSpec((S5_CHUNK, LANE), lambda q, b, c: (tok(q, b, c), P_U // LANE + q)),
        pl.BlockSpec((1, LANE, qcols), lambda q, b, c: (q, 0, 0)),
        pl.BlockSpec((1, LANE, qcols), lambda q, b, c: (q, 0, 0)),
        pl.BlockSpec((1, qcols, LANE), lambda q, b, c: (q, 0, 0)),
        pl.BlockSpec((1, qcols, LANE), lambda q, b, c: (q, 0, 0)),
        pl.BlockSpec((S5_STEPS * S5_SUB, qcols), lambda q, b, c: (0, q)),
        pl.BlockSpec((S5_STEPS * S5_SUB, qcols), lambda q, b, c: (0, q)),
        pl.BlockSpec((S5_SUB, qcols), lambda q, b, c: (0, q)),
        pl.BlockSpec((S5_SUB, qcols), lambda q, b, c: (0, q)),
        pl.BlockSpec((1, 1, LANE), lambda q, b, c: (q, 0, 0)),
    ]
    return specs, tok, qcols


def s5_fwd(proj, wb_re, wb_im, wc_re, wc_im, sf_re, sf_im, cf_re, cf_im, dvec, bsz, seq, ride=None):
    t = proj.shape[0]
    n5 = seq // S5_CHUNK
    in_specs, tok, qcols = _s5_specs(n5, False)

    def body(u_ref, wbr, wbi, wcr, wci, sfr, sfi, cfr, cfi, d_ref, y_ref, xr_ref, xi_ref, cr_scr, ci_scr):
        @pl.when(pl.program_id(2) == 0)
        def _():
            cr_scr[...] = jnp.zeros_like(cr_scr)
            ci_scr[...] = jnp.zeros_like(ci_scr)

        u = u_ref[...]
        bur, bui = _dg(u, wbr[0], 1, 0), _dg(u, wbi[0], 1, 0)
        coefs = _s5_coefs(sfr, sfi, cfr, cfi, False)
        cr, ci = cr_scr[...], ci_scr[...]
        for r in range(S5_CHUNK // S5_SUB):
            rows = slice(r * S5_SUB, (r + 1) * S5_SUB)
            xr, xi = _s5_block_scan(bur[rows], bui[rows], coefs, cr, ci, False)
            xr_ref[rows, :], xi_ref[rows, :] = xr, xi
            cr, ci = xr[S5_SUB - 1:, :], xi[S5_SUB - 1:, :]
        cr_scr[...], ci_scr[...] = cr, ci
        y_ref[...] = _dg(xr_ref[...], wcr[0], 1, 0) - _dg(xi_ref[...], wci[0], 1, 0) + u * d_ref[0]

    def tile(w):
        return pl.BlockSpec((S5_CHUNK, w), lambda q, b, c: (tok(q, b, c), q))

    return hosted_call(
        body, name="s5_fwd", grid=(S5_Q, bsz, n5), in_specs=in_specs,
        out_specs=[tile(LANE), tile(qcols), tile(qcols)],
        out_shape=[jax.ShapeDtypeStruct((t, S5_WIDTH), F32), jax.ShapeDtypeStruct((t, S5_COLS), F32),
                   jax.ShapeDtypeStruct((t, S5_COLS), F32)],
        scratch=[pltpu.VMEM((1, qcols), F32)] * 2,
        args=(proj, wb_re, wb_im, wc_re, wc_im, sf_re, sf_im, cf_re, cf_im, dvec), ride=ride)


def s5_bwd(proj, wb_re, wb_im, wc_re, wc_im, sb_re, sb_im, cb_re, cb_im, dvec, xr_all, xi_all, dy, bsz, seq,
           ride=None):
    t = proj.shape[0]
    n5 = seq // S5_CHUNK
    in_specs, tok, qcols = _s5_specs(n5, True)
    blocks = S5_CHUNK // HALO

    def prev_rows(q, b, c):
        return (jnp.maximum(tok(q, b, c) * blocks - 1, 0), q)

    in_specs += [pl.BlockSpec((S5_CHUNK, qcols), lambda q, b, c: (tok(q, b, c), q)),
                 pl.BlockSpec((S5_CHUNK, qcols), lambda q, b, c: (tok(q, b, c), q)),
                 pl.BlockSpec((HALO, qcols), prev_rows), pl.BlockSpec((HALO, qcols), prev_rows),
                 pl.BlockSpec((S5_CHUNK, LANE), lambda q, b, c: (tok(q, b, c), q))]

    def body(u_ref, wbr, wbi, wcr, wci, sbr, sbi, cbr, cbi, d_ref, xr_ref, xi_ref, pr_ref, pi_ref, dy_ref,
             du_ref, dwbr, dwbi, dwcr, dwci, dar, dai, dd_ref, gr_scr, gi_scr, gr_all, gi_all):
        b, c = pl.program_id(1), pl.program_id(2)

        @pl.when(c == 0)
        def _():
            gr_scr[...] = jnp.zeros_like(gr_scr)
            gi_scr[...] = jnp.zeros_like(gi_scr)

        @pl.when((b == 0) & (c == 0))
        def _():
            for r in (dwbr, dwbi, dwcr, dwci, dar, dai, dd_ref):
                r[...] = jnp.zeros_like(r)

        u, dy_v = u_ref[...], dy_ref[...]
        g0r, g0i = _dg(dy_v, wcr[0], 1, 1), -_dg(dy_v, wci[0], 1, 1)
        coefs = _s5_coefs(sbr, sbi, cbr, cbi, True)
        cr, ci = gr_scr[...], gi_scr[...]
        for r in reversed(range(S5_CHUNK // S5_SUB)):
            rows = slice(r * S5_SUB, (r + 1) * S5_SUB)
            br, bi = _s5_block_scan(g0r[rows], g0i[rows], coefs, cr, ci, True)
            gr_all[rows, :], gi_all[rows, :] = br, bi
            cr, ci = br[:1, :], bi[:1, :]
        gr_scr[...], gi_scr[...] = cr, ci
        gr, gi = gr_all[...], gi_all[...]

        row = lax.broadcasted_iota(jnp.int32, (S5_CHUNK, 1), 0)
        xr, xi = xr_ref[...], xi_ref[...]
        is_first = (c == n5 - 1)
        hr = jnp.where(is_first, 0.0, pr_ref[...][HALO - 1:, :])
        hi = jnp.where(is_first, 0.0, pi_ref[...][HALO - 1:, :])
        xpr = jnp.where(row >= 1, pltpu.roll(xr, 1, 0), hr)
        xpi = jnp.where(row >= 1, pltpu.roll(xi, 1, 0), hi)
        dar[0] += jnp.sum(xpr * gr + xpi * gi, axis=0, keepdims=True)
        dai[0] += jnp.sum(xpr * gi - xpi * gr, axis=0, keepdims=True)
        du_ref[...] = _dg(gr, wbr[0], 1, 1) + _dg(gi, wbi[0], 1, 1) + dy_v * d_ref[0]
        dwbr[0] += _dg(u, gr, 0, 0)
        dwbi[0] += _dg(u, gi, 0, 0)
        dwcr[0] += _dg(xr, dy_v, 0, 0)
        dwci[0] -= _dg(xi, dy_v, 0, 0)
        dd_ref[0] += jnp.sum(dy_v * u, axis=0, keepdims=True)

    def acc(shape):
        return pl.BlockSpec((1,) + shape, lambda q, b, c: (q, 0, 0))

    return hosted_call(
        body, name="s5_bwd", grid=(S5_Q, bsz, n5), in_specs=in_specs,
        out_specs=[pl.BlockSpec((S5_CHUNK, LANE), lambda q, b, c: (tok(q, b, c), q)),
                   acc((LANE, qcols)), acc((LANE, qcols)), acc((qcols, LANE)), acc((qcols, LANE)),
                   acc((1, qcols)), acc((1, qcols)), acc((1, LANE))],
        out_shape=[jax.ShapeDtypeStruct((t, S5_WIDTH), F32),
                   jax.ShapeDtypeStruct((S5_Q, LANE, qcols), F32), jax.ShapeDtypeStruct((S5_Q, LANE, qcols), F32),
                   jax.ShapeDtypeStruct((S5_Q, qcols, LANE), F32), jax.ShapeDtypeStruct((S5_Q, qcols, LANE), F32),
                   jax.ShapeDtypeStruct((S5_Q, 1, qcols), F32), jax.ShapeDtypeStruct((S5_Q, 1, qcols), F32),
                   jax.ShapeDtypeStruct((S5_Q, 1, LANE), F32)],
        scratch=[pltpu.VMEM((1, qcols), F32)] * 2 + [pltpu.VMEM((S5_CHUNK, qcols), F32)] * 2,
        args=(proj, wb_re, wb_im, wc_re, wc_im, sb_re, sb_im, cb_re, cb_im, dvec, xr_all, xi_all, xr_all, xi_all, dy),
        ride=ride)


def _blockdiag_b(bb):
    b4 = bb.reshape(S5_Q, 8, S5_STATE, S5_GROUP_CH)
    eye = jnp.eye(8, dtype=bb.dtype)
    w = jnp.einsum("qgph,gk->qghkp", b4, eye)
    return w.reshape(S5_Q, LANE, S5_COLS // S5_Q)


def _unblock_b(dw):
    d = dw.reshape(S5_Q, 8, S5_GROUP_CH, 8, S5_STATE)
    d = jnp.einsum("qghgp->qgph", d)
    return d.reshape(S5_COLS, S5_GROUP_CH)


def _blockdiag_c(cc):
    c4 = cc.reshape(S5_Q, 8, S5_GROUP_CH, S5_STATE)
    eye = jnp.eye(8, dtype=cc.dtype)
    w = jnp.einsum("qghp,gk->qgpkh", c4, eye)
    return w.reshape(S5_Q, S5_COLS // S5_Q, LANE)


def _unblock_c(dw):
    d = dw.reshape(S5_Q, 8, S5_STATE, 8, S5_GROUP_CH)
    d = jnp.einsum("qgpgh->qghp", d)
    return d.reshape(S5_GROUPS, S5_GROUP_CH, S5_STATE)


def ada_fwd(c_all, w_loc, b_loc):
    def body(c_ref, w_ref, b_ref, o_ref):
        o_ref[...] = _dg(_silu(c_ref[...]), w_ref[...], 1, 0) + b_ref[...]

    return pl.pallas_call(body, name="ada_fwd",
                          out_shape=jax.ShapeDtypeStruct((c_all.shape[0], w_loc.shape[1]), F32),
                          compiler_params=_cp())(c_all, w_loc, b_loc)


def ada_bwd(c_all, dmod_all, dmod_cols):
    def body(c_ref, da_ref, dc_ref, gb_ref, gw_ref):
        gb_ref[...] = jnp.sum(da_ref[...], axis=0, keepdims=True)
        gw_ref[...] = _dg(_silu(c_ref[...]), dc_ref[...], 0, 0)

    return pl.pallas_call(body, name="ada_bwd",
                          out_shape=[jax.ShapeDtypeStruct((1, dmod_all.shape[1]), F32),
                                     jax.ShapeDtypeStruct((c_all.shape[1], dmod_cols.shape[1]), F32)],
                          compiler_params=_cp())(c_all, dmod_all, dmod_cols)


_FLIPS = [(0, 0, 1), (1, 0, 0), (0, 1, 0), (1, 1, 0), (1, 0, 1), (0, 1, 1), (1, 1, 1)]


def _exchange_ops(srcs, outs, sems, gather):
    n = len(srcs)
    send_sems, recv_sems, loc_sems = sems
    x, y, c = lax.axis_index("x"), lax.axis_index("y"), lax.axis_index("c")
    me = 4 * x + 2 * y + c
    peers = []
    for fx, fy, fc in _FLIPS:
        px, py, pc = (1 - x if fx else x), (1 - y if fy else y), (1 - c if fc else c)
        peers.append(((px, py, pc), 4 * px + 2 * py + pc))

    def copy(k, j, slot_src, slot_dst):
        src = srcs[k] if gather[k] else srcs[k].at[slot_src]
        return pltpu.make_async_remote_copy(src_ref=src, dst_ref=outs[k].at[slot_dst],
                                            send_sem=send_sems.at[k, j], recv_sem=recv_sems.at[k, j],
                                            device_id=peers[j][0], device_id_type=MESH)

    def local(k):
        own = srcs[k] if gather[k] else srcs[k].at[me]
        return pltpu.make_async_copy(own, outs[k].at[me], loc_sems.at[k])

    def start():
        for k in range(n):
            for j in range(N_DEV - 1):
                copy(k, j, peers[j][1], me).start()
            local(k).start()

    def wait():
        for k in range(n):
            for j in range(N_DEV - 1):
                copy(k, j, me, peers[j][1]).wait_recv()
        for k in range(n):
            for j in range(N_DEV - 1):
                copy(k, j, peers[j][1], me).wait_send()
            local(k).wait()

    return start, wait


def _exchange_parts(arrs, gather):
    n = len(arrs)
    any_spec = pl.BlockSpec(memory_space=pl.ANY)
    shapes = [jax.ShapeDtypeStruct(((N_DEV,) + a.shape) if g else a.shape, a.dtype) for a, g in zip(arrs, gather)]
    sems = [pltpu.SemaphoreType.DMA((n, N_DEV - 1)), pltpu.SemaphoreType.DMA((n, N_DEV - 1)),
            pltpu.SemaphoreType.DMA((n,))]
    return [any_spec] * n, shapes, sems


def exchange(name, arrs, gather):
    n = len(arrs)
    specs, shapes, sems = _exchange_parts(arrs, gather)

    def body(*refs):
        start, wait = _exchange_ops(refs[:n], refs[n:2 * n], refs[2 * n:], gather)
        start()
        wait()

    return pl.pallas_call(
        body, name=name, in_specs=specs, out_specs=specs, out_shape=shapes, scratch_shapes=sems,
        compiler_params=pltpu.CompilerParams(has_side_effects=True),
    )(*arrs)


def hosted_call(body, *, name, grid, in_specs, out_specs, out_shape, args, scratch=(), ride=None):
    sem = ("arbitrary",) * len(grid)
    if ride is None:
        res = pl.pallas_call(body, name=name, grid=grid, in_specs=in_specs, out_specs=out_specs, out_shape=out_shape,
                             scratch_shapes=list(scratch), compiler_params=_cp(*sem))(*args)
        return list(res), []
    arrs, gather = ride
    n, n_in, n_out, n_scr = len(arrs), len(in_specs), len(out_specs), len(scratch)
    specs, shapes, sems = _exchange_parts(arrs, gather)

    def both(*refs):
        ins, srcs = refs[:n_in], refs[n_in:n_in + n]
        outs, landed = refs[n_in + n:n_in + n + n_out], refs[n_in + n + n_out:n_in + 2 * n + n_out]
        scr, ex_sems = refs[n_in + 2 * n + n_out:n_in + 2 * n + n_out + n_scr], refs[n_in + 2 * n + n_out + n_scr:]
        start, wait = _exchange_ops(srcs, landed, ex_sems, gather)
        first = functools.reduce(lambda a, b: a & b, [pl.program_id(d) == 0 for d in range(len(grid))])
        last = functools.reduce(lambda a, b: a & b, [pl.program_id(d) == grid[d] - 1 for d in range(len(grid))])
        pl.when(first)(start)
        body(*ins, *outs, *scr)
        pl.when(last)(wait)

    res = pl.pallas_call(
        both, name=name, grid=grid, in_specs=list(in_specs) + specs, out_specs=list(out_specs) + specs,
        out_shape=list(out_shape) + shapes, scratch_shapes=list(scratch) + sems, compiler_params=_cp(*sem),
    )(*args, *arrs)
    return list(res[:n_out]), list(res[n_out:])


def adamw(name, g, w, m, v, tr, sel=None):
    slots = g.ndim >= 3
    r, c = w.shape
    c1, c2 = 1.0 - ADAM_B1 ** ADAM_STEP, 1.0 - ADAM_B2 ** ADAM_STEP

    def body(g_ref, w_ref, m_ref, v_ref, go, do, mo, vo):
        if slots:
            gg = g_ref[0].astype(F32)
            for j in range(1, N_DEV):
                gg = gg + g_ref[j].astype(F32)
        else:
            gg = g_ref[...]
        mn = ADAM_B1 * m_ref[...] + (1.0 - ADAM_B1) * gg
        vn = ADAM_B2 * v_ref[...] + (1.0 - ADAM_B2) * (gg * gg)
        go[...], mo[...], vo[...] = gg, mn, vn
        do[...] = -ADAM_LR * ((mn / c1) / (jnp.sqrt(vn / c2) + ADAM_EPS) + ADAM_WD * w_ref[...])

    blk = pl.BlockSpec((tr, c), lambda i: (i, 0))
    if g.ndim == 4:
        gspec = pl.BlockSpec((N_DEV, None, tr, c), lambda i: (0, sel, i, 0))
    else:
        gspec = pl.BlockSpec((N_DEV, tr, c), lambda i: (0, i, 0)) if slots else blk
    return pl.pallas_call(
        body, name=name, grid=(r // tr,), in_specs=[gspec, blk, blk, blk], out_specs=[blk] * 4,
        out_shape=[jax.ShapeDtypeStruct((r, c), F32)] * 4, compiler_params=_cp("parallel"),
    )(g, w, m, v)


def _lane_rows(n):
    return -(-n // (8 * LANE)) * 8


def _pack(arrs):
    pieces = []
    for a in arrs:
        n = math.prod(a.shape)
        flat = a.reshape(-1).astype(F32)
        pieces.append(jnp.pad(flat, (0, _lane_rows(n) * LANE - n)).reshape(_lane_rows(n), LANE))
    return jnp.concatenate(pieces, axis=0)


def _unpack(buf, shapes):
    out, off = [], 0
    for s in shapes:
        n = math.prod(s)
        out.append(buf[off:off + _lane_rows(n)].reshape(-1)[:n].reshape(s))
        off += _lane_rows(n)
    return out


def _cols_to_full(g):
    return jnp.transpose(g, (1, 0, 2)).reshape(g.shape[1], N_DEV * g.shape[2])


def _full_to_cols(w):
    r, c = w.shape
    return jnp.transpose(w.reshape(r, N_DEV, c // N_DEV), (1, 0, 2))


FF_CHUNK = D_FF // 2
FFN_TM = 256


def _resident(shape):
    return pl.BlockSpec(shape, lambda i: (0,) * len(shape), pipeline_mode=pl.Buffered(1))


def _ffn_fwd(tag, x, sc, sh, g, w1, w3, w2, lg, lb, seq, tm, ride=None):
    t = x.shape[0]
    tm = min(FFN_TM, tm)
    tps = seq // tm
    ln = _res_ln(0.5)

    def body(x_ref, sc_ref, sh_ref, g_ref, lg_ref, lb_ref, w1_ref, w3_ref, w2_ref, y_ref, h_ref, a_ref, b_ref, f_ref):
        xv = x_ref[...]
        h = (xv * (1.0 + sc_ref[0]) + sh_ref[0]).astype(BF16)
        h_ref[...] = h
        acc = jnp.zeros((tm, D_MODEL), F32)
        for j in range(D_FF // FF_CHUNK):
            sl = slice(j * FF_CHUNK, (j + 1) * FF_CHUNK)
            a = _dg(h, w1_ref[:, sl], 1, 0)
            b = _dg(h, w3_ref[:, sl], 1, 0)
            a_ref[:, sl] = a
            b_ref[:, sl] = b
            acc = acc + _dg(_silu(a) * b, w2_ref[sl, :], 1, 0)
        f_ref[...] = acc
        y_ref[...] = ln(xv, acc, g_ref[0], lg_ref[...], lb_ref[...])[0]

    row = lambda c: pl.BlockSpec((tm, c), lambda i: (i, 0))
    per_seq = pl.BlockSpec((1, 1, D_MODEL), lambda i: (i // tps, 0, 0))
    vec = pl.BlockSpec((1, D_MODEL), lambda i: (0, 0))
    (y, h, a, b, f), landed = hosted_call(
        body, name=tag + "_fwd", grid=(t // tm,),
        in_specs=[row(D_MODEL), per_seq, per_seq, per_seq, vec, vec,
                  _resident((D_MODEL, D_FF)), _resident((D_MODEL, D_FF)), _resident((D_FF, D_MODEL))],
        out_specs=[row(D_MODEL), row(D_MODEL), row(D_FF), row(D_FF), row(D_MODEL)],
        out_shape=[jax.ShapeDtypeStruct((t, D_MODEL), F32), jax.ShapeDtypeStruct((t, D_MODEL), BF16),
                   jax.ShapeDtypeStruct((t, D_FF), F32), jax.ShapeDtypeStruct((t, D_FF), F32),
                   jax.ShapeDtypeStruct((t, D_MODEL), F32)],
        args=(x, sc, sh, g, lg, lb, w1, w3, w2), ride=ride)
    return y, (h, a, b, f), landed


def _ffn_bwd(tag, dy, x, sc, sh, g, w1, w3, w2, lg, lb, res, seq, tm, ride=None, chain=None):
    h, a, b, f = res
    t = x.shape[0]
    tmk = min(FFN_TM, tm)
    tps = seq // tmk
    ln = _res_ln(0.5)

    def body(dy_ref, x_ref, f_ref, a_ref, b_ref, sc_ref, sh_ref, g_ref, lg_ref, lb_ref, w1_ref, w3_ref, w2_ref,
             dx_ref, da_ref, db_ref, s_ref, df_ref, dsc_ref, dsh_ref, dg_ref, dlg_ref, dlb_ref):
        i = pl.program_id(0)

        @pl.when(i % tps == 0)
        def _():
            for r in (dsc_ref, dsh_ref, dg_ref):
                r[...] = jnp.zeros_like(r)

        @pl.when(i == 0)
        def _():
            dlg_ref[...] = jnp.zeros_like(dlg_ref)
            dlb_ref[...] = jnp.zeros_like(dlb_ref)

        xv = x_ref[...]
        _, pull = jax.vjp(ln, xv, f_ref[...], g_ref[0], lg_ref[...], lb_ref[...])
        dx_res, df, dg, dlg, dlb = pull((dy_ref[...],))
        dfb = df.astype(BF16)
        df_ref[...] = dfb
        dh = jnp.zeros((tmk, D_MODEL), F32)
        for j in range(D_FF // FF_CHUNK):
            sl = slice(j * FF_CHUNK, (j + 1) * FF_CHUNK)
            ds = _dg(dfb, w2_ref[sl, :], 1, 1)
            av, bv = a_ref[:, sl], b_ref[:, sl]
            sg = jax.nn.sigmoid(av)
            si = av * sg
            s_ref[:, sl] = (si * bv).astype(BF16)
            da = (ds * bv * (sg * (1.0 + av * (1.0 - sg)))).astype(BF16)
            db = (ds * si).astype(BF16)
            da_ref[:, sl] = da
            db_ref[:, sl] = db
            dh = dh + _dg(da, w1_ref[:, sl], 1, 1) + _dg(db, w3_ref[:, sl], 1, 1)
        dx_ref[...] = dx_res + dh * (1.0 + sc_ref[0])
        dsc_ref[0] += jnp.sum(dh * xv, axis=0, keepdims=True)
        dsh_ref[0] += jnp.sum(dh, axis=0, keepdims=True)
        dg_ref[0] += dg
        dlg_ref[...] += dlg
        dlb_ref[...] += dlb

    row = lambda c: pl.BlockSpec((tmk, c), lambda i: (i, 0))
    per_seq = pl.BlockSpec((1, 1, D_MODEL), lambda i: (i // tps, 0, 0))
    vec = pl.BlockSpec((1, D_MODEL), lambda i: (0, 0))
    seq_shape = jax.ShapeDtypeStruct(sc.shape, F32)
    vec_shape = jax.ShapeDtypeStruct((1, D_MODEL), F32)
    (dx, da, db, s, df, dsc, dsh, dg, dlg, dlb), landed = hosted_call(
        body, name=tag + "_bwd", grid=(t // tmk,),
        in_specs=[row(D_MODEL), row(D_MODEL), row(D_MODEL), row(D_FF), row(D_FF), per_seq, per_seq, per_seq, vec, vec,
                  _resident((D_MODEL, D_FF)), _resident((D_MODEL, D_FF)), _resident((D_FF, D_MODEL))],
        out_specs=[row(D_MODEL), row(D_FF), row(D_FF), row(D_FF), row(D_MODEL), per_seq, per_seq, per_seq, vec, vec],
        out_shape=[jax.ShapeDtypeStruct((t, D_MODEL), F32), jax.ShapeDtypeStruct((t, D_FF), BF16),
                   jax.ShapeDtypeStruct((t, D_FF), BF16), jax.ShapeDtypeStruct((t, D_FF), BF16),
                   jax.ShapeDtypeStruct((t, D_MODEL), BF16), seq_shape, seq_shape, seq_shape, vec_shape, vec_shape],
        args=(dy, x, f, a, b, sc, sh, g, lg, lb, w1, w3, w2), ride=ride)
    if chain is None:
        dw2 = mm_tn(tag + "_dw2", s, df, D_FF // 2, D_MODEL, tm, BF16)
        dw1 = mm_tn(tag + "_dw1", h, da, D_MODEL, D_FF // 2, tm, BF16)
        dw3 = mm_tn(tag + "_dw3", h, db, D_MODEL, D_FF // 2, tm, BF16)
        return dx, (dsh, dsc, dg), (dw1, dw3, dw2, dlg, dlb), landed
    dw2, landed = mm_tn(tag + "_dw2", s, df, D_FF // 2, D_MODEL, tm, BF16, ride=chain)
    dw1, (s_w2,) = mm_tn(tag + "_dw1", h, da, D_MODEL, D_FF // 2, tm, BF16,
                         ride=([dw2.reshape(N_DEV, D_FF // N_DEV, D_MODEL)], [False]))
    dw3, (s_w1,) = mm_tn(tag + "_dw3", h, db, D_MODEL, D_FF // 2, tm, BF16, ride=([_full_to_cols(dw1)], [False]))
    return dx, (dsh, dsc, dg), (s_w1, dw3, s_w2, dlg, dlb), landed


def kernel(x, c, w_ada, b_ada, ffn1_w1, ffn1_w3, ffn1_w2, ln1_g, ln1_b, w_in, conv_w, conv_b, dt_bias, a_log, d_ssd, ssd_norm_w, s5_a_re, s5_a_im, s5_log_dt, s5_b_re, s5_b_im, s5_c_re, s5_c_im, s5_d, w_glu, b_glu, w_out, ln2_g, ln2_b, ffn2_w1, ffn2_w3, ffn2_w2, ln3_g, ln3_b, loss_target, m_w_ada, m_b_ada, m_ffn1_w1, m_ffn1_w3, m_ffn1_w2, m_ln1_g, m_ln1_b, m_w_in, m_conv_w, m_conv_b, m_dt_bias, m_a_log, m_d_ssd, m_ssd_norm_w, m_s5_a_re, m_s5_a_im, m_s5_log_dt, m_s5_b_re, m_s5_b_im, m_s5_c_re, m_s5_c_im, m_s5_d, m_w_glu, m_b_glu, m_w_out, m_ln2_g, m_ln2_b, m_ffn2_w1, m_ffn2_w3, m_ffn2_w2, m_ln3_g, m_ln3_b, v_w_ada, v_b_ada, v_ffn1_w1, v_ffn1_w3, v_ffn1_w2, v_ln1_g, v_ln1_b, v_w_in, v_conv_w, v_conv_b, v_dt_bias, v_a_log, v_d_ssd, v_ssd_norm_w, v_s5_a_re, v_s5_a_im, v_s5_log_dt, v_s5_b_re, v_s5_b_im, v_s5_c_re, v_s5_c_im, v_s5_d, v_w_glu, v_b_glu, v_w_out, v_ln2_g, v_ln2_b, v_ffn2_w1, v_ffn2_w3, v_ffn2_w2, v_ln3_g, v_ln3_b):
    given = dict(locals())
    bsz, seq, _ = x.shape
    t = bsz * seq
    tm = min(512, seq)
    me = 4 * lax.axis_index("x") + 2 * lax.axis_index("y") + lax.axis_index("c")
    x0 = x.reshape(t, D_MODEL)
    target = loss_target.reshape(t, D_MODEL)

    g_col1, g_row1, g_c = exchange(
        "gather_ffn1", [jnp.stack([ffn1_w1[0], ffn1_w3[0]]).astype(BF16), ffn1_w2[0].astype(BF16), c], [True] * 3)
    f1w1, f1w3 = [_cols_to_full(g_col1[:, k]) for k in range(2)]
    f1w2 = g_row1.reshape(D_FF, D_MODEL)
    c_all = g_c.reshape(N_DEV * bsz, D_MODEL)

    n_loc = w_ada.shape[2]
    b_loc = lax.dynamic_slice(b_ada, (0, me * n_loc), (1, n_loc))
    mod_cols = ada_fwd(c_all, w_ada[0], b_loc)
    g_mod, = exchange("gather_mod", [mod_cols], [True])
    mine = lax.dynamic_slice(g_mod, (0, me * bsz, 0), (N_DEV, bsz, n_loc))
    mod = jnp.transpose(mine, (1, 0, 2)).reshape(bsz, N_MOD, 1, D_MODEL)
    sh1, sc1, g1, sh2, sc2, g2, sh3, sc3, g3 = [mod[:, k] for k in range(N_MOD)]

    x1, res1, (g_win, g_glu, g_out, g_conv) = _ffn_fwd(
        "ffn1", x0, sc1, sh1, g1, f1w1, f1w3, f1w2, ln1_g, ln1_b, seq, tm,
        ride=([w_in[0].astype(BF16), w_glu[0].astype(BF16), w_out[0].astype(BF16), conv_w[0]], [True] * 4))
    win = _cols_to_full(g_win)
    wglu = g_glu.reshape(S5_WIDTH, S5_WIDTH).astype(F32)
    wout = g_out.reshape(D_MODEL, D_MODEL)
    wo_ssd, wo_s5 = wout[:SSD_WIDTH], wout[SSD_WIDTH:]
    convw = jnp.transpose(g_conv, (1, 0, 2)).reshape(CONV_K, CONV_CH)
    w_z, w_xbc = win[:, :SSD_WIDTH], win[:, SSD_WIDTH:SSD_WIDTH + CONV_CH]
    w_dt = win[:, SSD_WIDTH + CONV_CH:SSD_WIDTH + CONV_CH + SSD_HEADS]
    w_u = win[:, SSD_WIDTH + CONV_CH + SSD_HEADS:]
    dt_pad = [jnp.pad(w_dt[:, HEADS_PER_GROUP * g:HEADS_PER_GROUP * (g + 1)], ((0, 0), (0, LANE - HEADS_PER_GROUP)))
              for g in range(SSD_GROUPS)]
    w_dtp = jnp.concatenate(dt_pad, axis=1)
    w_proj = jnp.concatenate([w_xbc, w_z, w_u, w_dtp], axis=1)

    h2, = rowwise_fwd("mix_mod", f_modulate, [x1], [sc2, sh2], [], [(D_MODEL, BF16)], seq, tm)
    proj = mm_nn("mix_proj", [h2], [w_proj], tm, P_COLS // 2)
    xc = conv_fwd(proj, convw, conv_b, seq, tm)
    dtb = jnp.pad(dt_bias.reshape(SSD_GROUPS, 1, HEADS_PER_GROUP), ((0, 0), (0, 0), (0, LANE - HEADS_PER_GROUP)))
    alog = jnp.pad(a_log.reshape(SSD_GROUPS, 1, HEADS_PER_GROUP), ((0, 0), (0, 0), (0, LANE - HEADS_PER_GROUP)))
    dcol = jnp.pad(d_ssd.reshape(SSD_GROUPS, HEADS_PER_GROUP, 1), ((0, 0), (0, LANE - HEADS_PER_GROUP), (0, 0)))
    nw = ssd_norm_w.reshape(SSD_GROUPS, 1, GROUP_COLS)
    y_ssd, hprev = ssd_fwd(xc, proj, dtb, alog, dcol, nw, bsz, seq)

    a_re2, a_im2, ldt2 = s5_a_re[0], s5_a_im[0], s5_log_dt.reshape(S5_GROUPS, 1)
    ab_re, ab_im, f_re, f_im = _whole(_disc_a, "s5_disc_a", [a_re2, a_im2, ldt2], [(S5_GROUPS, S5_STATE)] * 4)
    b_re2, b_im2 = s5_b_re.reshape(S5_COLS, S5_GROUP_CH), s5_b_im.reshape(S5_COLS, S5_GROUP_CH)
    fr_col, fi_col = f_re.reshape(S5_COLS, 1), f_im.reshape(S5_COLS, 1)
    bb_re, bb_im = _whole(_disc_b, "s5_disc_b", [fr_col, fi_col, b_re2, b_im2], [(S5_COLS, S5_GROUP_CH)] * 2)
    wb_re, wb_im = _blockdiag_b(bb_re).astype(BF16), _blockdiag_b(bb_im).astype(BF16)
    wc_re, wc_im = _blockdiag_c(s5_c_re[0]).astype(BF16), _blockdiag_c(s5_c_im[0]).astype(BF16)
    dt5 = jnp.exp(ldt2)
    lam_re, lam_im = (dt5 * a_re2).reshape(1, S5_COLS), (dt5 * a_im2).reshape(1, S5_COLS)
    sf_re, sf_im, sb_re, sb_im, cf_re, cf_im, cb_re, cb_im = s5_tables(lam_re, lam_im)
    d5 = s5_d.reshape(S5_Q, 1, LANE)
    (y5, xr_all, xi_all), (g_col2, g_row2) = s5_fwd(
        proj, wb_re, wb_im, wc_re, wc_im, sf_re, sf_im, cf_re, cf_im, d5, bsz, seq,
        ride=([jnp.stack([ffn2_w1[0], ffn2_w3[0]]).astype(BF16), ffn2_w2[0].astype(BF16)], [True] * 2))
    f2w1, f2w3 = [_cols_to_full(g_col2[:, k]) for k in range(2)]
    f2w2 = g_row2.reshape(D_FF, D_MODEL)
    o5, = rowwise_fwd("s5_glu", f_glu, [y5], [], [wglu, b_glu], [(S5_WIDTH, F32)], seq, tm)

    mix = mm_nn("mix_out", [y_ssd, o5], [wo_ssd, wo_s5], tm, D_MODEL)
    x2, = rowwise_fwd("mix_ln", _res_ln(1.0), [x1, mix], [g2], [ln2_g, ln2_b], [(D_MODEL, F32)], seq, tm)

    x3, res3, _ = _ffn_fwd("ffn2", x2, sc3, sh3, g3, f2w1, f2w3, f2w2, ln3_g, ln3_b, seq, tm)
    dy, loss_loc = loss_head(x3, target, tm)

    dx2, dmod3, (d_f2w1, d_f2w3, d_f2w2, d_ln3g, d_ln3b), _ = _ffn_bwd(
        "ffn2", dy, x2, sc3, sh3, g3, f2w1, f2w3, f2w2, ln3_g, ln3_b, res3, seq, tm)

    (dx1_a, dmix), (dg2,), (d_ln2g, d_ln2b) = rowwise_bwd(
        "mix_ln_b", _res_ln(1.0), [x1, mix], [g2], [ln2_g, ln2_b], [dx2], seq, tm, [F32, BF16])
    d_wo = jnp.concatenate([mm_tn("mix_dwo_ssd", y_ssd, dmix, SSD_WIDTH, D_MODEL, tm, BF16),
                            mm_tn("mix_dwo_s5", o5, dmix, S5_WIDTH, D_MODEL, tm, BF16)], axis=0)
    dy_ssd = mm_nt("mix_dy_ssd", [dmix], [wo_ssd], tm, SSD_WIDTH)
    do5 = mm_nt("mix_do5", [dmix], [wo_s5], tm, S5_WIDTH)

    (dy5,), _, (d_wglu, d_bglu) = rowwise_bwd("s5_glu_b", f_glu, [y5], [], [wglu, b_glu], [do5], seq, tm, [F32])
    (du, dwbr, dwbi, dwcr, dwci, dab_re, dab_im, dd5), (s_col2, s_row2) = s5_bwd(
        proj, wb_re, wb_im, wc_re, wc_im, sb_re, sb_im, cb_re, cb_im, d5, xr_all, xi_all, dy5, bsz, seq,
        ride=([jnp.stack([_full_to_cols(d_f2w1), _full_to_cols(d_f2w3)], axis=1),
               d_f2w2.reshape(N_DEV, D_FF // N_DEV, D_MODEL)], [False] * 2))
    dbb_re, dbb_im = _unblock_b(dwbr), _unblock_b(dwbi)
    dfr_col, dfi_col, d_b_re, d_b_im = _whole_vjp(_disc_b, "s5_disc_b_b", [fr_col, fi_col, b_re2, b_im2],
                                                  [dbb_re, dbb_im])
    d_a_re, d_a_im, d_ldt = _whole_vjp(
        _disc_a, "s5_disc_a_b", [a_re2, a_im2, ldt2],
        [dab_re.reshape(S5_GROUPS, S5_STATE), dab_im.reshape(S5_GROUPS, S5_STATE),
         dfr_col.reshape(S5_GROUPS, S5_STATE), dfi_col.reshape(S5_GROUPS, S5_STATE)])
    d_c_re, d_c_im = _unblock_c(dwcr), _unblock_c(dwci)

    dxs, dbm, dcm, ddt, dz, ddtb, dalog, ddcol, dnw = ssd_bwd(xc, proj, dtb, alog, dcol, nw, hprev, dy_ssd, bsz, seq)
    dpre, d_convw, d_convb = conv_bwd_pre(proj, convw, conv_b, dxs, dbm, dcm, seq, tm)
    dxbc = conv_bwd_x(dpre, convw, seq, tm)

    dh2 = mm_nt("mix_dh", [dxbc, dz, du, ddt], [w_xbc, w_z, w_u, w_dtp], tm, D_MODEL)
    dw_xbc = mm_tn("mix_dw_xbc", h2, dxbc, D_MODEL, CONV_CH, tm, BF16)
    dw_z = mm_tn("mix_dw_z", h2, dz, D_MODEL, SSD_WIDTH, tm, BF16)
    dw_u = mm_tn("mix_dw_u", h2, du, D_MODEL, S5_WIDTH, tm, BF16)
    dw_dt = mm_tn("mix_dw_dt", h2, ddt, D_MODEL, 2 * LANE, tm, BF16)
    dw_dt8 = jnp.concatenate([dw_dt[:, LANE * g:LANE * g + HEADS_PER_GROUP] for g in range(SSD_GROUPS)], axis=1)
    d_win = jnp.concatenate([dw_z, dw_xbc, dw_dt8, dw_u], axis=1)
    (dx1,), (dsc2, dsh2), _ = rowwise_bwd("mix_mod_b", f_modulate, [x1], [sc2, sh2], [], [dh2], seq, tm, [F32],
                                          add_rows={0: dx1_a})

    dx0, dmod1, (s_f1w1, d_f1w3, s_f1w2, d_ln1g, d_ln1b), (s_win, s_glu, s_out) = _ffn_bwd(
        "ffn1", dx1, x0, sc1, sh1, g1, f1w1, f1w3, f1w2, ln1_g, ln1_b, res1, seq, tm,
        chain=([_full_to_cols(d_win), d_wglu.reshape(N_DEV, S5_WIDTH // N_DEV, S5_WIDTH).astype(BF16),
                d_wo.reshape(N_DEV, D_MODEL // N_DEV, D_MODEL)], [False] * 3))

    dmod = jnp.concatenate(list(dmod1) + [dsh2, dsc2, dg2] + list(dmod3), axis=1).reshape(bsz, N_MOD * D_MODEL)
    small = {
        "ln1_g": d_ln1g, "ln1_b": d_ln1b, "conv_w": d_convw, "conv_b": d_convb,
        "dt_bias": ddtb[:, 0, :HEADS_PER_GROUP].reshape(1, SSD_HEADS),
        "a_log": dalog[:, 0, :HEADS_PER_GROUP].reshape(1, SSD_HEADS),
        "d_ssd": ddcol[:, :HEADS_PER_GROUP, 0].reshape(1, SSD_HEADS),
        "ssd_norm_w": dnw.reshape(1, SSD_WIDTH),
        "s5_a_re": d_a_re[None], "s5_a_im": d_a_im[None], "s5_log_dt": d_ldt.reshape(1, S5_GROUPS),
        "s5_b_re": d_b_re.reshape(s5_b_re.shape), "s5_b_im": d_b_im.reshape(s5_b_im.shape),
        "s5_c_re": d_c_re[None], "s5_c_im": d_c_im[None], "s5_d": dd5.reshape(1, S5_WIDTH),
        "b_glu": d_bglu, "ln2_g": d_ln2g, "ln2_b": d_ln2b, "ln3_g": d_ln3g, "ln3_b": d_ln3b,
    }
    small["loss"] = loss_loc.reshape(1, 1)
    names = list(small)
    shapes = [small[k].shape for k in names]
    packed = _pack([small[k] for k in names])
    s_f1w3, s_small, s_dmod = exchange("sum_grads", [_full_to_cols(d_f1w3), packed, dmod], [False, True, True])

    out = {"grad_x": dx0.reshape(x.shape)}

    def put(name, res, shape):
        for key, val in zip(("grad_", "delta_", "new_m_", "new_v_"), res):
            out[key + name] = val.reshape(shape)

    for name, slots, k in (("ffn1_w1", s_f1w1, None), ("ffn1_w3", s_f1w3, None), ("ffn2_w1", s_col2, 0),
                           ("ffn2_w3", s_col2, 1)):
        w = given[name]
        put(name, adamw("adam_" + name, slots, w[0], given["m_" + name][0], given["v_" + name][0], 256, sel=k), w.shape)
    for name, slots in (("ffn1_w2", s_f1w2), ("ffn2_w2", s_row2)):
        w = given[name]
        put(name, adamw("adam_" + name, slots, w[0], given["m_" + name][0], given["v_" + name][0], 176), w.shape)
    put("w_in", adamw("adam_w_in", s_win, w_in[0], m_w_in[0], v_w_in[0], 256), w_in.shape)
    put("w_glu", adamw("adam_w_glu", s_glu, w_glu[0], m_w_glu[0], v_w_glu[0], 64), w_glu.shape)
    put("w_out", adamw("adam_w_out", s_out, w_out[0], m_w_out[0], v_w_out[0], 128), w_out.shape)

    dmod_all = s_dmod.reshape(N_DEV * bsz, N_MOD * D_MODEL)
    g_bada, g_wada = ada_bwd(c_all, dmod_all, lax.dynamic_slice(dmod_all, (0, me * n_loc), (N_DEV * bsz, n_loc)))
    put("w_ada", adamw("adam_w_ada", g_wada, w_ada[0], m_w_ada[0], v_w_ada[0], 256), w_ada.shape)
    put("b_ada", adamw("adam_b_ada", g_bada, b_ada, m_b_ada, v_b_ada, 1), b_ada.shape)

    not_params = {"conv_w": jnp.zeros((CONV_K, CONV_CH), F32), "loss": jnp.zeros((1, 1), F32)}
    pw, pm, pv = [_pack([not_params[k] if k in not_params else given[pre + k] for k in names]) for pre in ("", "m_", "v_")]
    res_small = adamw("adam_small", s_small, pw, pm, pv, packed.shape[0])
    parts = [_unpack(r, shapes) for r in res_small]
    for i, k in enumerate(names):
        if k not in not_params:
            put(k, [p[i] for p in parts], given[k].shape)
    out["loss"] = parts[0][names.index("loss")][0, 0]
    g_cw = lax.dynamic_slice(parts[0][names.index("conv_w")], (0, me * LANE), (CONV_K, LANE))
    put("conv_w", adamw("adam_conv_w", g_cw, conv_w[0], m_conv_w[0], v_conv_w[0], CONV_K), conv_w.shape)

    order = ["w_ada", "b_ada", "ffn1_w1", "ffn1_w3", "ffn1_w2", "ln1_g", "ln1_b", "w_in", "conv_w", "conv_b", "dt_bias",
             "a_log", "d_ssd", "ssd_norm_w", "s5_a_re", "s5_a_im", "s5_log_dt", "s5_b_re", "s5_b_im", "s5_c_re",
             "s5_c_im", "s5_d", "w_glu", "b_glu", "w_out", "ln2_g", "ln2_b", "ffn2_w1", "ffn2_w3", "ffn2_w2", "ln3_g",
             "ln3_b"]
    return (out["loss"], out["grad_x"], *[out[p + n] for p in ("grad_", "delta_", "new_m_", "new_v_") for n in order])
```

```python
import functools
import math

import jax
import jax.numpy as jnp
from jax import lax
from jax.experimental import pallas as pl
from jax.experimental.pallas import tpu as pltpu

F32 = jnp.float32
BF16 = jnp.bfloat16
HI = lax.Precision.HIGHEST
MESH = pl.DeviceIdType.MESH

N_DEV = 8
D_MODEL = 1024
D_FF = 2816
N_MOD = 9
SSD_WIDTH = 512
SSD_HEADS = 8
SSD_HEAD_DIM = 64
SSD_GROUPS = 2
SSD_STATE = 128
SSD_CHUNK = 128
GROUP_COLS = SSD_WIDTH // SSD_GROUPS
HEADS_PER_GROUP = SSD_HEADS // SSD_GROUPS
CONV_K = 4
CONV_CH = 1024
S5_WIDTH = 512
S5_GROUPS = 32
S5_GROUP_CH = 16
S5_STATE = 64
S5_COLS = S5_GROUPS * S5_STATE
S5_Q = 4
S5_CHUNK = 256
ALPHA = 2.0 ** 0.25
LN_EPS = 1e-5
LANE = 128
HALO = 8

P_XBC, P_Z, P_U, P_DT = 0, 1024, 1536, 2048
P_COLS = 2048 + SSD_GROUPS * LANE

ADAM_LR, ADAM_B1, ADAM_B2, ADAM_EPS, ADAM_WD, ADAM_STEP = 0.001, 0.9, 0.999, 1e-08, 0.01, 10

VMEM_LIMIT = 56 * 1024 * 1024


def _cp(*sem):
    return pltpu.CompilerParams(dimension_semantics=sem if sem else None, vmem_limit_bytes=VMEM_LIMIT)


def _dg(a, b, ca, cb):
    return lax.dot_general(a.astype(BF16), b.astype(BF16), (((ca,), (cb,)), ((), ())), preferred_element_type=F32)


@jax.custom_vjp
def bdot_nn(a, b):
    return _dg(a, b, 1, 0)


bdot_nn.defvjp(lambda a, b: (_dg(a, b, 1, 0), (a, b)),
               lambda r, g: (_dg(g, r[1], 1, 1), _dg(r[0], g, 0, 0)))


@jax.custom_vjp
def bdot_nt(a, b):
    return _dg(a, b, 1, 1)


bdot_nt.defvjp(lambda a, b: (_dg(a, b, 1, 1), (a, b)),
               lambda r, g: (_dg(g, r[1], 1, 0), _dg(g, r[0], 0, 0)))


@jax.custom_vjp
def bdot_tn(a, b):
    return _dg(a, b, 0, 0)


bdot_tn.defvjp(lambda a, b: (_dg(a, b, 0, 0), (a, b)),
               lambda r, g: (_dg(r[1], g, 1, 1), _dg(r[0], g, 1, 0)))


def _take_col(z):
    @jax.custom_vjp
    def take(x):
        return x[:, z:z + 1]

    def bwd(shape, g):
        hot = (lax.broadcasted_iota(jnp.int32, (1, shape[1]), 1) == z).astype(F32)
        return (g * hot,)

    take.defvjp(lambda x: (x[:, z:z + 1], x.shape), bwd)
    return take


def _take_row(z):
    @jax.custom_vjp
    def take(x):
        return x[z:z + 1, :]

    def bwd(shape, g):
        hot = (lax.broadcasted_iota(jnp.int32, (shape[0], 1), 0) == z).astype(F32)
        return (hot * g,)

    take.defvjp(lambda x: (x[z:z + 1, :], x.shape), bwd)
    return take


def _view(a):
    return a if isinstance(a, tuple) else (a, 0, a.shape[1])


def _col_spec(view, rows, width, index):
    _, off, _ = view
    assert off % width == 0
    return pl.BlockSpec((rows, width), lambda *g: (index(*g)[0], off // width + index(*g)[1]))


def _rw_in_specs(rows, bps, gps, tm, tps):
    specs = [_col_spec(_view(r), tm, _view(r)[2], lambda i: (i, 0)) for r in rows]
    specs += [pl.BlockSpec((1, 1, b.shape[2]), lambda i: (i // tps, 0, 0)) for b in bps]
    specs += [pl.BlockSpec(g.shape, lambda i, nd=g.ndim: (0,) * nd) for g in gps]
    return specs


def _rw_vals(refs, nr, nb, ng):
    vals = [r[...] for r in refs[:nr]]
    vals += [b[0] for b in refs[nr:nr + nb]]
    vals += [g[...] for g in refs[nr + nb:nr + nb + ng]]
    return vals


def rowwise_fwd(name, f, rows, bps, gps, outs, seq, tm):
    t = _view(rows[0])[0].shape[0]
    tps = seq // tm
    nr, nb, ng = len(rows), len(bps), len(gps)

    def body(*refs):
        res = f(*_rw_vals(refs, nr, nb, ng))
        for o, v in zip(refs[nr + nb + ng:], res):
            o[...] = v.astype(o.dtype)

    return pl.pallas_call(
        body, name=name, grid=(t // tm,),
        in_specs=_rw_in_specs(rows, bps, gps, tm, tps),
        out_specs=[pl.BlockSpec((tm, c), lambda i: (i, 0)) for c, _ in outs],
        out_shape=[jax.ShapeDtypeStruct((t, c), d) for c, d in outs],
        compiler_params=_cp("arbitrary"),
    )(*[_view(r)[0] for r in rows], *bps, *gps)


def rowwise_bwd(name, f, rows, bps, gps, douts, seq, tm, row_grads, add_rows=None):
    add_rows = add_rows or {}
    t = _view(rows[0])[0].shape[0]
    tps = seq // tm
    nr, nb, ng, nd = len(rows), len(bps), len(gps), len(douts)
    want = [k for k in range(nr) if row_grads[k] is not None]
    adds = sorted(add_rows)
    n_in = nr + nb + ng + nd + len(adds)

    def body(*refs):
        vals = _rw_vals(refs, nr, nb, ng)
        dvals = tuple(r[...] for r in refs[nr + nb + ng:nr + nb + ng + nd])
        add_refs = dict(zip(adds, refs[nr + nb + ng + nd:n_in]))
        out_refs = refs[n_in:]
        _, pull = jax.vjp(f, *vals)
        grads = pull(dvals)
        i = pl.program_id(0)
        for o, k in zip(out_refs, want):
            g = grads[k]
            if k in add_refs:
                g = g + add_refs[k][...]
            o[...] = g.astype(o.dtype)
        for j in range(nb):
            o = out_refs[len(want) + j]

            @pl.when(i % tps == 0)
            def _(o=o):
                o[...] = jnp.zeros_like(o)

            o[0] = o[0] + grads[nr + j]
        for j in range(ng):
            o = out_refs[len(want) + nb + j]

            @pl.when(i == 0)
            def _(o=o):
                o[...] = jnp.zeros_like(o)

            o[...] = o[...] + grads[nr + nb + j]

    in_specs = _rw_in_specs(rows, bps, gps, tm, tps)
    in_specs += [pl.BlockSpec((tm, d.shape[1]), lambda i: (i, 0)) for d in douts]
    in_specs += [pl.BlockSpec((tm, add_rows[k].shape[1]), lambda i: (i, 0)) for k in adds]
    out_specs = [pl.BlockSpec((tm, _view(rows[k])[2]), lambda i: (i, 0)) for k in want]
    out_shape = [jax.ShapeDtypeStruct((t, _view(rows[k])[2]), row_grads[k]) for k in want]
    out_specs += [pl.BlockSpec((1, 1, b.shape[2]), lambda i: (i // tps, 0, 0)) for b in bps]
    out_shape += [jax.ShapeDtypeStruct(b.shape, F32) for b in bps]
    out_specs += [pl.BlockSpec(g.shape, lambda i, n=g.ndim: (0,) * n) for g in gps]
    out_shape += [jax.ShapeDtypeStruct(g.shape, F32) for g in gps]
    res = pl.pallas_call(
        body, name=name, grid=(t // tm,), in_specs=in_specs, out_specs=out_specs, out_shape=out_shape,
        compiler_params=_cp("arbitrary"),
    )(*[_view(r)[0] for r in rows], *bps, *gps, *douts, *[add_rows[k] for k in adds])
    nw = len(want)
    return res[:nw], res[nw:nw + nb], res[nw + nb:]


def mm_nn(name, xs, ws, tm, tn, out_dtype=F32):
    views = [_view(x) for x in xs]
    t, n, k = views[0][0].shape[0], ws[0].shape[1], len(xs)

    def body(*refs):
        acc = _dg(refs[0][...], refs[k][...], 1, 0)
        for i in range(1, k):
            acc = acc + _dg(refs[i][...], refs[k + i][...], 1, 0)
        refs[2 * k][...] = acc.astype(out_dtype)

    in_specs = [_col_spec(v, tm, v[2], lambda i, j: (i, 0)) for v in views]
    in_specs += [pl.BlockSpec((w.shape[0], tn), lambda i, j: (0, j)) for w in ws]
    return pl.pallas_call(
        body, name=name, grid=(t // tm, n // tn), in_specs=in_specs,
        out_specs=pl.BlockSpec((tm, tn), lambda i, j: (i, j)),
        out_shape=jax.ShapeDtypeStruct((t, n), out_dtype),
        compiler_params=_cp("parallel", "parallel"),
    )(*[v[0] for v in views], *ws)


def mm_nt(name, dys, ws, tm, tk, out_dtype=F32, ride=None):
    views = [_view(d) for d in dys]
    t, kk, k = views[0][0].shape[0], ws[0].shape[0], len(dys)

    def body(*refs):
        acc = _dg(refs[0][...], refs[k][...], 1, 1)
        for i in range(1, k):
            acc = acc + _dg(refs[i][...], refs[k + i][...], 1, 1)
        refs[2 * k][...] = acc.astype(out_dtype)

    in_specs = [_col_spec(v, tm, v[2], lambda i, j: (i, 0)) for v in views]
    in_specs += [pl.BlockSpec((tk, w.shape[1]), lambda i, j: (j, 0)) for w in ws]
    out_spec = pl.BlockSpec((tm, tk), lambda i, j: (i, j))
    out_shape = jax.ShapeDtypeStruct((t, kk), out_dtype)
    if ride is not None:
        (res,), landed = hosted_call(body, name=name, grid=(t // tm, kk // tk), in_specs=in_specs, out_specs=[out_spec],
                                     out_shape=[out_shape], args=(*[v[0] for v in views], *ws), ride=ride)
        return res, landed
    return pl.pallas_call(
        body, name=name, grid=(t // tm, kk // tk), in_specs=in_specs, out_specs=out_spec, out_shape=out_shape,
        compiler_params=_cp("parallel", "parallel"),
    )(*[v[0] for v in views], *ws)


def mm_tn(name, x, dy, tk, tn, tt, out_dtype=F32, ride=None):
    xv, dv = _view(x), _view(dy)
    t, kk, n = xv[0].shape[0], xv[2], dv[2]
    steps = t // tt

    def body(x_ref, d_ref, o_ref, acc_ref):
        @pl.when(pl.program_id(2) == 0)
        def _():
            acc_ref[...] = jnp.zeros_like(acc_ref)

        acc_ref[...] += _dg(x_ref[...], d_ref[...], 0, 0)

        @pl.when(pl.program_id(2) == steps - 1)
        def _():
            o_ref[...] = acc_ref[...].astype(out_dtype)

    in_specs = [_col_spec(xv, tt, tk, lambda a, b, c: (c, a)), _col_spec(dv, tt, tn, lambda a, b, c: (c, b))]
    out_spec = pl.BlockSpec((tk, tn), lambda a, b, c: (a, b))
    out_shape = jax.ShapeDtypeStruct((kk, n), out_dtype)
    if ride is not None:
        (res,), landed = hosted_call(body, name=name, grid=(kk // tk, n // tn, steps), in_specs=in_specs,
                                     out_specs=[out_spec], out_shape=[out_shape], scratch=[pltpu.VMEM((tk, tn), F32)],
                                     args=(xv[0], dv[0]), ride=ride)
        return res, landed
    return pl.pallas_call(
        body, name=name, grid=(kk // tk, n // tn, steps), in_specs=in_specs, out_specs=out_spec, out_shape=out_shape,
        scratch_shapes=[pltpu.VMEM((tk, tn), F32)],
        compiler_params=_cp("parallel", "parallel", "arbitrary"),
    )(xv[0], dv[0])


def _silu(x):
    return x * jax.nn.sigmoid(x)


def f_modulate(x, sc, sh):
    return (x * (1.0 + sc) + sh,)


def _res_ln(coef):
    def f(x, y, g, lg, lb):
        r = ALPHA * x + (coef * g) * y
        mu = jnp.mean(r, axis=-1, keepdims=True)
        d = r - mu
        var = jnp.mean(d * d, axis=-1, keepdims=True)
        return (d * lax.rsqrt(var + LN_EPS) * lg + lb,)
    return f


def f_glu(y, w, b):
    g = jax.nn.gelu(y)
    return (g * jax.nn.sigmoid(bdot_nn(g, w) + b),)


def loss_head(y, target, tm):
    t, d = y.shape

    def body(y_ref, t_ref, dy_ref, l_ref):
        @pl.when(pl.program_id(0) == 0)
        def _():
            l_ref[...] = jnp.zeros_like(l_ref)

        e = y_ref[...] - t_ref[...]
        dy_ref[...] = e * (1.0 / d)
        l_ref[...] += 0.5 * jnp.sum(jnp.mean(e * e, axis=-1, keepdims=True), axis=0, keepdims=True)

    dy, l = pl.pallas_call(
        body, name="loss_head", grid=(t // tm,),
        in_specs=[pl.BlockSpec((tm, d), lambda i: (i, 0))] * 2,
        out_specs=[pl.BlockSpec((tm, d), lambda i: (i, 0)), pl.BlockSpec((1, 1), lambda i: (0, 0))],
        out_shape=[jax.ShapeDtypeStruct((t, d), F32), jax.ShapeDtypeStruct((1, 1), F32)],
        compiler_params=_cp("arbitrary"),
    )(y, target)
    return dy, l[0, 0]


def _shift_down(x, halo, k):
    if k == 0:
        return x
    r = pltpu.roll(x, k, 0)
    hr = pltpu.roll(halo, k, 0)
    row = lax.broadcasted_iota(jnp.int32, (HALO, 1), 0)
    top = jnp.where(row < k, hr, r[:HALO])
    return jnp.concatenate([top, r[HALO:]], axis=0)


def _shift_up(x, halo, k):
    if k == 0:
        return x
    n = x.shape[0]
    r = pltpu.roll(x, n - k, 0)
    hr = pltpu.roll(halo, HALO - k, 0)
    row = lax.broadcasted_iota(jnp.int32, (HALO, 1), 0)
    bot = jnp.where(row >= HALO - k, hr, r[n - HALO:])
    return jnp.concatenate([r[:n - HALO], bot], axis=0)


def _conv_pre(x, halo, w, b):
    acc = x * w[CONV_K - 1:CONV_K, :] + b
    for k in range(1, CONV_K):
        acc = acc + _shift_down(x, halo, k) * w[CONV_K - 1 - k:CONV_K - k, :]
    return acc


def _rows_before(width, tm):
    return pl.BlockSpec((HALO, width), lambda i: (jnp.maximum(i * (tm // HALO) - 1, 0), 0))


def conv_fwd(proj, w, b, seq, tm):
    t = proj.shape[0]
    tps = seq // tm

    def body(x_ref, h_ref, w_ref, b_ref, o_ref):
        first = (pl.program_id(0) % tps == 0)
        halo = jnp.where(first, 0.0, h_ref[...])
        o_ref[...] = _silu(_conv_pre(x_ref[...], halo, w_ref[...], b_ref[...]))

    return pl.pallas_call(
        body, name="conv_fwd", grid=(t // tm,),
        in_specs=[pl.BlockSpec((tm, CONV_CH), lambda i: (i, 0)), _rows_before(CONV_CH, tm),
                  pl.BlockSpec((CONV_K, CONV_CH), lambda i: (0, 0)), pl.BlockSpec((1, CONV_CH), lambda i: (0, 0))],
        out_specs=pl.BlockSpec((tm, CONV_CH), lambda i: (i, 0)),
        out_shape=jax.ShapeDtypeStruct((t, CONV_CH), F32),
        compiler_params=_cp("arbitrary"),
    )(proj, proj, w, b)


def conv_bwd_pre(proj, w, b, dxs, dbm, dcm, seq, tm):
    t = proj.shape[0]
    tps = seq // tm

    def body(x_ref, h_ref, w_ref, b_ref, d1, d2, d3, dp_ref, dw_ref, db_ref):
        i = pl.program_id(0)
        halo = jnp.where(i % tps == 0, 0.0, h_ref[...])
        x = x_ref[...]
        pre = _conv_pre(x, halo, w_ref[...], b_ref[...])
        sg = jax.nn.sigmoid(pre)
        dout = jnp.concatenate([d1[...], d2[...], d3[...]], axis=1)
        dp = dout * (sg * (1.0 + pre * (1.0 - sg)))
        dp_ref[...] = dp

        @pl.when(i == 0)
        def _():
            dw_ref[...] = jnp.zeros_like(dw_ref)
            db_ref[...] = jnp.zeros_like(db_ref)

        db_ref[...] += jnp.sum(dp, axis=0, keepdims=True)
        for k in range(CONV_K):
            j = CONV_K - 1 - k
            dw_ref[j:j + 1, :] += jnp.sum(dp * _shift_down(x, halo, k), axis=0, keepdims=True)

    return pl.pallas_call(
        body, name="conv_bwd_pre", grid=(t // tm,),
        in_specs=[pl.BlockSpec((tm, CONV_CH), lambda i: (i, 0)), _rows_before(CONV_CH, tm),
                  pl.BlockSpec((CONV_K, CONV_CH), lambda i: (0, 0)), pl.BlockSpec((1, CONV_CH), lambda i: (0, 0)),
                  pl.BlockSpec((tm, 512), lambda i: (i, 0)), pl.BlockSpec((tm, 256), lambda i: (i, 0)),
                  pl.BlockSpec((tm, 256), lambda i: (i, 0))],
        out_specs=[pl.BlockSpec((tm, CONV_CH), lambda i: (i, 0)), pl.BlockSpec((CONV_K, CONV_CH), lambda i: (0, 0)),
                   pl.BlockSpec((1, CONV_CH), lambda i: (0, 0))],
        out_shape=[jax.ShapeDtypeStruct((t, CONV_CH), F32), jax.ShapeDtypeStruct((CONV_K, CONV_CH), F32),
                   jax.ShapeDtypeStruct((1, CONV_CH), F32)],
        compiler_params=_cp("arbitrary"),
    )(proj, proj, w, b, dxs, dbm, dcm)


def conv_bwd_x(dpre, w, seq, tm):
    t = dpre.shape[0]
    tps = seq // tm
    blocks = tm // HALO
    last = t // HALO - 1

    def body(d_ref, h_ref, w_ref, o_ref):
        halo = jnp.where(pl.program_id(0) % tps == tps - 1, 0.0, h_ref[...])
        d = d_ref[...]
        w = w_ref[...]
        acc = d * w[CONV_K - 1:CONV_K, :]
        for k in range(1, CONV_K):
            acc = acc + _shift_up(d, halo, k) * w[CONV_K - 1 - k:CONV_K - k, :]
        o_ref[...] = acc

    return pl.pallas_call(
        body, name="conv_bwd_x", grid=(t // tm,),
        in_specs=[pl.BlockSpec((tm, CONV_CH), lambda i: (i, 0)),
                  pl.BlockSpec((HALO, CONV_CH), lambda i: (jnp.minimum((i + 1) * blocks, last), 0)),
                  pl.BlockSpec((CONV_K, CONV_CH), lambda i: (0, 0))],
        out_specs=pl.BlockSpec((tm, CONV_CH), lambda i: (i, 0)),
        out_shape=jax.ShapeDtypeStruct((t, CONV_CH), F32),
        compiler_params=_cp("arbitrary"),
    )(dpre, dpre, w)


def _softplus(x):
    return jnp.maximum(x, 0.0) + jnp.log1p(jnp.exp(-jnp.abs(x)))


def _ssd_chunk(xs, bg, cg, dtr, zz, hp, dtb, alog, dcol, nw):
    l = xs.shape[0]
    row = lax.broadcasted_iota(jnp.int32, (l, l), 0)
    col = lax.broadcasted_iota(jnp.int32, (l, l), 1)
    causal = row >= col
    tril = causal.astype(F32)
    expand = (lax.broadcasted_iota(jnp.int32, (LANE, GROUP_COLS), 1) // SSD_HEAD_DIM
              == lax.broadcasted_iota(jnp.int32, (LANE, GROUP_COLS), 0)).astype(F32)
    head_of_col = lax.broadcasted_iota(jnp.int32, (1, GROUP_COLS), 1) // SSD_HEAD_DIM
    last_row = (lax.broadcasted_iota(jnp.int32, (l, 1), 0) == l - 1).astype(F32)

    dtc = _softplus(dtr + dtb)
    a_c = dtc * (-jnp.exp(alog))
    acs_c = jnp.dot(tril, a_c, precision=HI, preferred_element_type=F32)
    dt_e = jnp.dot(dtc, expand, precision=HI, preferred_element_type=F32)
    acs_e = jnp.dot(acs_c, expand, precision=HI, preferred_element_type=F32)
    alast_e = jnp.sum(acs_e * last_row, axis=0, keepdims=True)
    x = xs * dt_e
    states = bdot_tn(bg, x * jnp.exp(alast_e - acs_e))
    h_next = jnp.exp(alast_e) * hp + states
    d_e = jnp.sum(dcol * expand, axis=0, keepdims=True)
    y = bdot_nn(cg, hp) * jnp.exp(acs_e) + d_e * xs
    cb = bdot_nt(cg, bg)
    acs_t = acs_c.T
    for z in range(HEADS_PER_GROUP):
        seg = _take_col(z)(acs_c) - _take_row(z)(acs_t)
        lmat = jnp.exp(jnp.where(causal, seg, -1e30))
        y = y + bdot_nn(cb * lmat, x * (head_of_col == z).astype(F32))
    yz = y * _silu(zz)
    ms = jnp.mean(yz * yz, axis=-1, keepdims=True)
    return yz * lax.rsqrt(ms + LN_EPS) * nw, h_next


def _ssd_in_specs(nc, rev):
    def tok(b, c):
        return b * nc + (nc - 1 - c if rev else c)

    whole = lambda *shape: pl.BlockSpec(shape, lambda b, c: (0,) * len(shape))
    both = SSD_GROUPS * SSD_STATE
    return [
        pl.BlockSpec((SSD_CHUNK, SSD_WIDTH), lambda b, c: (tok(b, c), 0)),
        pl.BlockSpec((SSD_CHUNK, both), lambda b, c: (tok(b, c), SSD_WIDTH // both)),
        pl.BlockSpec((SSD_CHUNK, both), lambda b, c: (tok(b, c), SSD_WIDTH // both + 1)),
        pl.BlockSpec((SSD_CHUNK, SSD_GROUPS * LANE), lambda b, c: (tok(b, c), P_DT // (SSD_GROUPS * LANE))),
        pl.BlockSpec((SSD_CHUNK, SSD_WIDTH), lambda b, c: (tok(b, c), P_Z // SSD_WIDTH)),
        whole(SSD_GROUPS, 1, LANE), whole(SSD_GROUPS, 1, LANE), whole(SSD_GROUPS, LANE, 1),
        whole(SSD_GROUPS, 1, GROUP_COLS),
    ], tok


def _group(ref, g, width):
    return ref[:, g * width:(g + 1) * width]


def ssd_fwd(xc, proj, dtb, alog, dcol, nw, bsz, seq):
    t = xc.shape[0]
    nc = seq // SSD_CHUNK
    in_specs, tok = _ssd_in_specs(nc, False)

    def body(xs, bm, cm, dtr, zz, dtb_r, alog_r, dcol_r, nw_r, y_ref, hp_ref, h_scr):
        @pl.when(pl.program_id(1) == 0)
        def _():
            h_scr[...] = jnp.zeros_like(h_scr)

        for g in range(SSD_GROUPS):
            hp = h_scr[g]
            hp_ref[g, 0, 0] = hp
            y, hn = _ssd_chunk(_group(xs, g, GROUP_COLS), _group(bm, g, SSD_STATE), _group(cm, g, SSD_STATE),
                               _group(dtr, g, LANE), _group(zz, g, GROUP_COLS), hp,
                               dtb_r[g], alog_r[g], dcol_r[g], nw_r[g])
            y_ref[:, g * GROUP_COLS:(g + 1) * GROUP_COLS] = y
            h_scr[g] = hn

    return pl.pallas_call(
        body, name="ssd_fwd", grid=(bsz, nc), in_specs=in_specs,
        out_specs=[pl.BlockSpec((SSD_CHUNK, SSD_WIDTH), lambda b, c: (tok(b, c), 0)),
                   pl.BlockSpec((SSD_GROUPS, 1, 1, SSD_STATE, GROUP_COLS), lambda b, c: (0, b, c, 0, 0))],
        out_shape=[jax.ShapeDtypeStruct((t, SSD_WIDTH), F32),
                   jax.ShapeDtypeStruct((SSD_GROUPS, bsz, nc, SSD_STATE, GROUP_COLS), F32)],
        scratch_shapes=[pltpu.VMEM((SSD_GROUPS, SSD_STATE, GROUP_COLS), F32)],
        compiler_params=_cp("arbitrary", "arbitrary"),
    )(xc, xc, xc, proj, proj, dtb, alog, dcol, nw)


def ssd_bwd(xc, proj, dtb, alog, dcol, nw, hprev, dy, bsz, seq):
    t = xc.shape[0]
    nc = seq // SSD_CHUNK
    in_specs, tok = _ssd_in_specs(nc, True)
    in_specs += [pl.BlockSpec((SSD_GROUPS, 1, 1, SSD_STATE, GROUP_COLS), lambda b, c: (0, b, nc - 1 - c, 0, 0)),
                 pl.BlockSpec((SSD_CHUNK, SSD_WIDTH), lambda b, c: (tok(b, c), 0))]

    def body(xs, bm, cm, dtr, zz, dtb_r, alog_r, dcol_r, nw_r, hp_ref, dy_ref,
             dxs, dbm, dcm, ddt, dzz, ddtb, dalog, ddcol, dnw, dh_scr):
        b, c = pl.program_id(0), pl.program_id(1)

        @pl.when(c == 0)
        def _():
            dh_scr[...] = jnp.zeros_like(dh_scr)

        @pl.when((b == 0) & (c == 0))
        def _():
            for r in (ddtb, dalog, ddcol, dnw):
                r[...] = jnp.zeros_like(r)

        for g in range(SSD_GROUPS):
            wide = slice(g * GROUP_COLS, (g + 1) * GROUP_COLS)
            state = slice(g * SSD_STATE, (g + 1) * SSD_STATE)
            _, pull = jax.vjp(_ssd_chunk, xs[:, wide], bm[:, state], cm[:, state], _group(dtr, g, LANE), zz[:, wide],
                              hp_ref[g, 0, 0], dtb_r[g], alog_r[g], dcol_r[g], nw_r[g])
            d = pull((dy_ref[:, wide], dh_scr[g]))
            dxs[:, wide], dbm[:, state], dcm[:, state], dzz[:, wide] = d[0], d[1], d[2], d[4]
            ddt[:, g * LANE:(g + 1) * LANE] = d[3]
            dh_scr[g] = d[5]
            ddtb[g] += d[6]
            dalog[g] += d[7]
            ddcol[g] += d[8]
            dnw[g] += d[9]

    def tile(w):
        return pl.BlockSpec((SSD_CHUNK, w), lambda b, c: (tok(b, c), 0))

    whole = lambda *shape: pl.BlockSpec(shape, lambda b, c: (0,) * len(shape))
    return pl.pallas_call(
        body, name="ssd_bwd", grid=(bsz, nc), in_specs=in_specs,
        out_specs=[tile(SSD_WIDTH), tile(2 * SSD_STATE), tile(2 * SSD_STATE), tile(2 * LANE), tile(SSD_WIDTH),
                   whole(SSD_GROUPS, 1, LANE), whole(SSD_GROUPS, 1, LANE), whole(SSD_GROUPS, LANE, 1),
                   whole(SSD_GROUPS, 1, GROUP_COLS)],
        out_shape=[jax.ShapeDtypeStruct((t, SSD_WIDTH), F32), jax.ShapeDtypeStruct((t, 2 * SSD_STATE), F32),
                   jax.ShapeDtypeStruct((t, 2 * SSD_STATE), F32), jax.ShapeDtypeStruct((t, 2 * LANE), F32),
                   jax.ShapeDtypeStruct((t, SSD_WIDTH), F32),
                   jax.ShapeDtypeStruct((SSD_GROUPS, 1, LANE), F32), jax.ShapeDtypeStruct((SSD_GROUPS, 1, LANE), F32),
                   jax.ShapeDtypeStruct((SSD_GROUPS, LANE, 1), F32),
                   jax.ShapeDtypeStruct((SSD_GROUPS, 1, GROUP_COLS), F32)],
        scratch_shapes=[pltpu.VMEM((SSD_GROUPS, SSD_STATE, GROUP_COLS), F32)],
        compiler_params=_cp("arbitrary", "arbitrary"),
    )(xc, xc, xc, proj, proj, dtb, alog, dcol, nw, hprev, dy)


def _disc_a(a_re, a_im, log_dt):
    dt = jnp.exp(log_dt)
    mag = jnp.exp(dt * a_re)
    ab_re, ab_im = mag * jnp.cos(dt * a_im), mag * jnp.sin(dt * a_im)
    den = a_re * a_re + a_im * a_im
    nr, ni = ab_re - 1.0, ab_im
    f_re, f_im = (nr * a_re + ni * a_im) / den, (ni * a_re - nr * a_im) / den
    return ab_re, ab_im, f_re, f_im


def _disc_b(f_re, f_im, b_re, b_im):
    return f_re * b_re - f_im * b_im, f_re * b_im + f_im * b_re


def _whole(f, name, args, outs):
    def body(*refs):
        res = f(*[r[...] for r in refs[:len(args)]])
        for o, v in zip(refs[len(args):], res):
            o[...] = v

    return pl.pallas_call(body, name=name, out_shape=[jax.ShapeDtypeStruct(s, F32) for s in outs])(*args)


def _whole_vjp(f, name, args, cts):
    def body(*refs):
        vals = [r[...] for r in refs[:len(args)]]
        _, pull = jax.vjp(f, *vals)
        res = pull(tuple(r[...] for r in refs[len(args):len(args) + len(cts)]))
        for o, v in zip(refs[len(args) + len(cts):], res):
            o[...] = v

    return pl.pallas_call(body, name=name, out_shape=[jax.ShapeDtypeStruct(a.shape, F32) for a in args])(*args, *cts)


S5_SUB = 8
S5_STEPS = 3


def s5_tables(lam_re, lam_im):
    rows = S5_STEPS * S5_SUB

    def body(lr_ref, li_ref, sf_re, sf_im, sb_re, sb_im, cf_re, cf_im, cb_re, cb_im):
        lr, li = lr_ref[...], li_ref[...]

        def power(k):
            m = jnp.exp(k * lr)
            return m * jnp.cos(k * li), m * jnp.sin(k * li)

        srow = lax.broadcasted_iota(jnp.int32, (rows, 1), 0)
        k = jnp.left_shift(1, srow // S5_SUB)
        tt = srow % S5_SUB
        pr, pi = power(k.astype(F32))
        fwd, bwd = tt >= k, tt < S5_SUB - k
        sf_re[...], sf_im[...] = jnp.where(fwd, pr, 0.0), jnp.where(fwd, pi, 0.0)
        sb_re[...], sb_im[...] = jnp.where(bwd, pr, 0.0), jnp.where(bwd, pi, 0.0)
        trow = lax.broadcasted_iota(jnp.int32, (S5_SUB, 1), 0)
        cf_re[...], cf_im[...] = power((trow + 1).astype(F32))
        cb_re[...], cb_im[...] = power((S5_SUB - trow).astype(F32))

    shp = [jax.ShapeDtypeStruct((rows, S5_COLS), F32)] * 4 + [jax.ShapeDtypeStruct((S5_SUB, S5_COLS), F32)] * 4
    return pl.pallas_call(body, name="s5_tables", out_shape=shp)(lam_re, lam_im)


def _s5_coefs(steps_re, steps_im, carry_re, carry_im, reverse):
    sign = -1.0 if reverse else 1.0
    steps = [(steps_re[s * S5_SUB:(s + 1) * S5_SUB, :], sign * steps_im[s * S5_SUB:(s + 1) * S5_SUB, :])
             for s in range(S5_STEPS)]
    return steps, (carry_re[...], sign * carry_im[...])


def _s5_block_scan(ar, ai, coefs, cr, ci, reverse):
    steps, (qr, qi) = coefs
    for s, (pr, pi) in enumerate(steps):
        shift = S5_SUB - (1 << s) if reverse else (1 << s)
        sr, si = pltpu.roll(ar, shift, 0), pltpu.roll(ai, shift, 0)
        ar, ai = ar + pr * sr - pi * si, ai + pr * si + pi * sr
    br, bi = jnp.broadcast_to(cr, ar.shape), jnp.broadcast_to(ci, ai.shape)
    return ar + qr * br - qi * bi, ai + qr * bi + qi * br


def _s5_specs(n5, rev):
    def tok(q, b, c):
        return b * n5 + (n5 - 1 - c if rev else c)

    qcols = S5_COLS // S5_Q
    specs = [
        pl.BlockSpec((S5_CHUNK, LANE), lambda q, b, c: (tok(q, b, c), P_U // LANE + q)),
        pl.BlockSpec((1, LANE, qcols), lambda q, b, c: (q, 0, 0)),
        pl.BlockSpec((1, LANE, qcols), lambda q, b, c: (q, 0, 0)),
        pl.BlockSpec((1, qcols, LANE), lambda q, b, c: (q, 0, 0)),
        pl.BlockSpec((1, qcols, LANE), lambda q, b, c: (q, 0, 0)),
        pl.BlockSpec((S5_STEPS * S5_SUB, qcols), lambda q, b, c: (0, q)),
        pl.BlockSpec((S5_STEPS * S5_SUB, qcols), lambda q, b, c: (0, q)),
        pl.BlockSpec((S5_SUB, qcols), lambda q, b, c: (0, q)),
        pl.BlockSpec((S5_SUB, qcols), lambda q, b, c: (0, q)),
        pl.BlockSpec((1, 1, LANE), lambda q, b, c: (q, 0, 0)),
    ]
    return specs, tok, qcols


def s5_fwd(proj, wb_re, wb_im, wc_re, wc_im, sf_re, sf_im, cf_re, cf_im, dvec, bsz, seq, ride=None):
    t = proj.shape[0]
    n5 = seq // S5_CHUNK
    in_specs, tok, qcols = _s5_specs(n5, False)

    def body(u_ref, wbr, wbi, wcr, wci, sfr, sfi, cfr, cfi, d_ref, y_ref, xr_ref, xi_ref, cr_scr, ci_scr):
        @pl.when(pl.program_id(2) == 0)
        def _():
            cr_scr[...] = jnp.zeros_like(cr_scr)
            ci_scr[...] = jnp.zeros_like(ci_scr)

        u = u_ref[...]
        bur, bui = _dg(u, wbr[0], 1, 0), _dg(u, wbi[0], 1, 0)
        coefs = _s5_coefs(sfr, sfi, cfr, cfi, False)
        cr, ci = cr_scr[...], ci_scr[...]
        for r in range(S5_CHUNK // S5_SUB):
            rows = slice(r * S5_SUB, (r + 1) * S5_SUB)
            xr, xi = _s5_block_scan(bur[rows], bui[rows], coefs, cr, ci, False)
            xr_ref[rows, :], xi_ref[rows, :] = xr, xi
            cr, ci = xr[S5_SUB - 1:, :], xi[S5_SUB - 1:, :]
        cr_scr[...], ci_scr[...] = cr, ci
        y_ref[...] = _dg(xr_ref[...], wcr[0], 1, 0) - _dg(xi_ref[...], wci[0], 1, 0) + u * d_ref[0]

    def tile(w):
        return pl.BlockSpec((S5_CHUNK, w), lambda q, b, c: (tok(q, b, c), q))

    return hosted_call(
        body, name="s5_fwd", grid=(S5_Q, bsz, n5), in_specs=in_specs,
        out_specs=[tile(LANE), tile(qcols), tile(qcols)],
        out_shape=[jax.ShapeDtypeStruct((t, S5_WIDTH), F32), jax.ShapeDtypeStruct((t, S5_COLS), F32),
                   jax.ShapeDtypeStruct((t, S5_COLS), F32)],
        scratch=[pltpu.VMEM((1, qcols), F32)] * 2,
        args=(proj, wb_re, wb_im, wc_re, wc_im, sf_re, sf_im, cf_re, cf_im, dvec), ride=ride)


def s5_bwd(proj, wb_re, wb_im, wc_re, wc_im, sb_re, sb_im, cb_re, cb_im, dvec, xr_all, xi_all, dy, bsz, seq,
           ride=None):
    t = proj.shape[0]
    n5 = seq // S5_CHUNK
    in_specs, tok, qcols = _s5_specs(n5, True)
    blocks = S5_CHUNK // HALO

    def prev_rows(q, b, c):
        return (jnp.maximum(tok(q, b, c) * blocks - 1, 0), q)

    in_specs += [pl.BlockSpec((S5_CHUNK, qcols), lambda q, b, c: (tok(q, b, c), q)),
                 pl.BlockSpec((S5_CHUNK, qcols), lambda q, b, c: (tok(q, b, c), q)),
                 pl.BlockSpec((HALO, qcols), prev_rows), pl.BlockSpec((HALO, qcols), prev_rows),
                 pl.BlockSpec((S5_CHUNK, LANE), lambda q, b, c: (tok(q, b, c), q))]

    def body(u_ref, wbr, wbi, wcr, wci, sbr, sbi, cbr, cbi, d_ref, xr_ref, xi_ref, pr_ref, pi_ref, dy_ref,
             du_ref, dwbr, dwbi, dwcr, dwci, dar, dai, dd_ref, gr_scr, gi_scr, gr_all, gi_all):
        b, c = pl.program_id(1), pl.program_id(2)

        @pl.when(c == 0)
        def _():
            gr_scr[...] = jnp.zeros_like(gr_scr)
            gi_scr[...] = jnp.zeros_like(gi_scr)

        @pl.when((b == 0) & (c == 0))
        def _():
            for r in (dwbr, dwbi, dwcr, dwci, dar, dai, dd_ref):
                r[...] = jnp.zeros_like(r)

        u, dy_v = u_ref[...], dy_ref[...]
        g0r, g0i = _dg(dy_v, wcr[0], 1, 1), -_dg(dy_v, wci[0], 1, 1)
        coefs = _s5_coefs(sbr, sbi, cbr, cbi, True)
        cr, ci = gr_scr[...], gi_scr[...]
        for r in reversed(range(S5_CHUNK // S5_SUB)):
            rows = slice(r * S5_SUB, (r + 1) * S5_SUB)
            br, bi = _s5_block_scan(g0r[rows], g0i[rows], coefs, cr, ci, True)
            gr_all[rows, :], gi_all[rows, :] = br, bi
            cr, ci = br[:1, :], bi[:1, :]
        gr_scr[...], gi_scr[...] = cr, ci
        gr, gi = gr_all[...], gi_all[...]

        row = lax.broadcasted_iota(jnp.int32, (S5_CHUNK, 1), 0)
        xr, xi = xr_ref[...], xi_ref[...]
        is_first = (c == n5 - 1)
        hr = jnp.where(is_first, 0.0, pr_ref[...][HALO - 1:, :])
        hi = jnp.where(is_first, 0.0, pi_ref[...][HALO - 1:, :])
        xpr = jnp.where(row >= 1, pltpu.roll(xr, 1, 0), hr)
        xpi = jnp.where(row >= 1, pltpu.roll(xi, 1, 0), hi)
        dar[0] += jnp.sum(xpr * gr + xpi * gi, axis=0, keepdims=True)
        dai[0] += jnp.sum(xpr * gi - xpi * gr, axis=0, keepdims=True)
        du_ref[...] = _dg(gr, wbr[0], 1, 1) + _dg(gi, wbi[0], 1, 1) + dy_v * d_ref[0]
        dwbr[0] += _dg(u, gr, 0, 0)
        dwbi[0] += _dg(u, gi, 0, 0)
        dwcr[0] += _dg(xr, dy_v, 0, 0)
        dwci[0] -= _dg(xi, dy_v, 0, 0)
        dd_ref[0] += jnp.sum(dy_v * u, axis=0, keepdims=True)

    def acc(shape):
        return pl.BlockSpec((1,) + shape, lambda q, b, c: (q, 0, 0))

    return hosted_call(
        body, name="s5_bwd", grid=(S5_Q, bsz, n5), in_specs=in_specs,
        out_specs=[pl.BlockSpec((S5_CHUNK, LANE), lambda q, b, c: (tok(q, b, c), q)),
                   acc((LANE, qcols)), acc((LANE, qcols)), acc((qcols, LANE)), acc((qcols, LANE)),
                   acc((1, qcols)), acc((1, qcols)), acc((1, LANE))],
        out_shape=[jax.ShapeDtypeStruct((t, S5_WIDTH), F32),
                   jax.ShapeDtypeStruct((S5_Q, LANE, qcols), F32), jax.ShapeDtypeStruct((S5_Q, LANE, qcols), F32),
                   jax.ShapeDtypeStruct((S5_Q, qcols, LANE), F32), jax.ShapeDtypeStruct((S5_Q, qcols, LANE), F32),
                   jax.ShapeDtypeStruct((S5_Q, 1, qcols), F32), jax.ShapeDtypeStruct((S5_Q, 1, qcols), F32),
                   jax.ShapeDtypeStruct((S5_Q, 1, LANE), F32)],
        scratch=[pltpu.VMEM((1, qcols), F32)] * 2 + [pltpu.VMEM((S5_CHUNK, qcols), F32)] * 2,
        args=(proj, wb_re, wb_im, wc_re, wc_im, sb_re, sb_im, cb_re, cb_im, dvec, xr_all, xi_all, xr_all, xi_all, dy),
        ride=ride)


def _blockdiag_b(bb):
    b4 = bb.reshape(S5_Q, 8, S5_STATE, S5_GROUP_CH)
    eye = jnp.eye(8, dtype=bb.dtype)
    w = jnp.einsum("qgph,gk->qghkp", b4, eye)
    return w.reshape(S5_Q, LANE, S5_COLS // S5_Q)


def _unblock_b(dw):
    d = dw.reshape(S5_Q, 8, S5_GROUP_CH, 8, S5_STATE)
    d = jnp.einsum("qghgp->qgph", d)
    return d.reshape(S5_COLS, S5_GROUP_CH)


def _blockdiag_c(cc):
    c4 = cc.reshape(S5_Q, 8, S5_GROUP_CH, S5_STATE)
    eye = jnp.eye(8, dtype=cc.dtype)
    w = jnp.einsum("qghp,gk->qgpkh", c4, eye)
    return w.reshape(S5_Q, S5_COLS // S5_Q, LANE)


def _unblock_c(dw):
    d = dw.reshape(S5_Q, 8, S5_STATE, 8, S5_GROUP_CH)
    d = jnp.einsum("qgpgh->qghp", d)
    return d.reshape(S5_GROUPS, S5_GROUP_CH, S5_STATE)


def ada_fwd(c_all, w_loc, b_loc):
    def body(c_ref, w_ref, b_ref, o_ref):
        o_ref[...] = _dg(_silu(c_ref[...]), w_ref[...], 1, 0) + b_ref[...]

    return pl.pallas_call(body, name="ada_fwd",
                          out_shape=jax.ShapeDtypeStruct((c_all.shape[0], w_loc.shape[1]), F32),
                          compiler_params=_cp())(c_all, w_loc, b_loc)


def ada_bwd(c_all, dmod_all, dmod_cols):
    def body(c_ref, da_ref, dc_ref, gb_ref, gw_ref):
        gb_ref[...] = jnp.sum(da_ref[...], axis=0, keepdims=True)
        gw_ref[...] = _dg(_silu(c_ref[...]), dc_ref[...], 0, 0)

    return pl.pallas_call(body, name="ada_bwd",
                          out_shape=[jax.ShapeDtypeStruct((1, dmod_all.shape[1]), F32),
                                     jax.ShapeDtypeStruct((c_all.shape[1], dmod_cols.shape[1]), F32)],
                          compiler_params=_cp())(c_all, dmod_all, dmod_cols)


_FLIPS = [(0, 0, 1), (1, 0, 0), (0, 1, 0), (1, 1, 0), (1, 0, 1), (0, 1, 1), (1, 1, 1)]


def _exchange_ops(srcs, outs, sems, gather):
    n = len(srcs)
    send_sems, recv_sems, loc_sems = sems
    x, y, c = lax.axis_index("x"), lax.axis_index("y"), lax.axis_index("c")
    me = 4 * x + 2 * y + c
    peers = []
    for fx, fy, fc in _FLIPS:
        px, py, pc = (1 - x if fx else x), (1 - y if fy else y), (1 - c if fc else c)
        peers.append(((px, py, pc), 4 * px + 2 * py + pc))

    def copy(k, j, slot_src, slot_dst):
        src = srcs[k] if gather[k] else srcs[k].at[slot_src]
        return pltpu.make_async_remote_copy(src_ref=src, dst_ref=outs[k].at[slot_dst],
                                            send_sem=send_sems.at[k, j], recv_sem=recv_sems.at[k, j],
                                            device_id=peers[j][0], device_id_type=MESH)

    def local(k):
        own = srcs[k] if gather[k] else srcs[k].at[me]
        return pltpu.make_async_copy(own, outs[k].at[me], loc_sems.at[k])

    def start():
        for k in range(n):
            for j in range(N_DEV - 1):
                copy(k, j, peers[j][1], me).start()
            local(k).start()

    def wait():
        for k in range(n):
            for j in range(N_DEV - 1):
                copy(k, j, me, peers[j][1]).wait_recv()
        for k in range(n):
            for j in range(N_DEV - 1):
                copy(k, j, peers[j][1], me).wait_send()
            local(k).wait()

    return start, wait


def _gather_two_level(srcs, outs, sems):
    n = len(srcs)
    send_sems, recv_sems, loc_sems = sems
    x, y, c = lax.axis_index("x"), lax.axis_index("y"), lax.axis_index("c")
    slot = lambda px, py, pc: 4 * px + 2 * py + pc
    me, sibling = (x, y, c), (x, y, 1 - c)
    chips = [(1 - x, y), (x, 1 - y), (1 - x, 1 - y)]

    def copy(k, j, block, to, own=False):
        return pltpu.make_async_remote_copy(src_ref=srcs[k] if own else outs[k].at[slot(*block)],
                                            dst_ref=outs[k].at[slot(*block)],
                                            send_sem=send_sems.at[k, j], recv_sem=recv_sems.at[k, j],
                                            device_id=to, device_id_type=MESH)

    locs = [pltpu.make_async_copy(srcs[k], outs[k].at[slot(*me)], loc_sems.at[k]) for k in range(n)]
    for k in range(n):
        locs[k].start()
        copy(k, 0, me, sibling, own=True).start()
        for j, chip in enumerate(chips):
            copy(k, 1 + j, me, (*chip, c), own=True).start()
    for j, chip in enumerate(chips):
        for k in range(n):
            copy(k, 1 + j, (*chip, c), me).wait_recv()
            copy(k, 4 + j, (*chip, c), sibling).start()
    for k in range(n):
        copy(k, 0, sibling, me).wait_recv()
        for j, chip in enumerate(chips):
            copy(k, 4 + j, (*chip, 1 - c), me).wait_recv()
    for k in range(n):
        copy(k, 0, me, sibling, own=True).wait_send()
        for j, chip in enumerate(chips):
            copy(k, 1 + j, me, (*chip, c), own=True).wait_send()
            copy(k, 4 + j, (*chip, c), sibling).wait_send()
        locs[k].wait()


def gather_two_level(name, arrs):
    n = len(arrs)
    specs, shapes, sems = _exchange_parts(arrs, [True] * n)

    def body(*refs):
        _gather_two_level(refs[:n], refs[n:2 * n], refs[2 * n:])

    return pl.pallas_call(
        body, name=name, in_specs=specs, out_specs=specs, out_shape=shapes, scratch_shapes=sems,
        compiler_params=pltpu.CompilerParams(has_side_effects=True),
    )(*arrs)


def _exchange_parts(arrs, gather):
    n = len(arrs)
    any_spec = pl.BlockSpec(memory_space=pl.ANY)
    shapes = [jax.ShapeDtypeStruct(((N_DEV,) + a.shape) if g else a.shape, a.dtype) for a, g in zip(arrs, gather)]
    sems = [pltpu.SemaphoreType.DMA((n, N_DEV - 1)), pltpu.SemaphoreType.DMA((n, N_DEV - 1)),
            pltpu.SemaphoreType.DMA((n,))]
    return [any_spec] * n, shapes, sems


def exchange(name, arrs, gather):
    n = len(arrs)
    specs, shapes, sems = _exchange_parts(arrs, gather)

    def body(*refs):
        start, wait = _exchange_ops(refs[:n], refs[n:2 * n], refs[2 * n:], gather)
        start()
        wait()

    return pl.pallas_call(
        body, name=name, in_specs=specs, out_specs=specs, out_shape=shapes, scratch_shapes=sems,
        compiler_params=pltpu.CompilerParams(has_side_effects=True),
    )(*arrs)


def hosted_call(body, *, name, grid, in_specs, out_specs, out_shape, args, scratch=(), ride=None):
    sem = ("arbitrary",) * len(grid)
    if ride is None:
        res = pl.pallas_call(body, name=name, grid=grid, in_specs=in_specs, out_specs=out_specs, out_shape=out_shape,
                             scratch_shapes=list(scratch), compiler_params=_cp(*sem))(*args)
        return list(res), []
    arrs, gather = ride
    n, n_in, n_out, n_scr = len(arrs), len(in_specs), len(out_specs), len(scratch)
    specs, shapes, sems = _exchange_parts(arrs, gather)

    def both(*refs):
        ins, srcs = refs[:n_in], refs[n_in:n_in + n]
        outs, landed = refs[n_in + n:n_in + n + n_out], refs[n_in + n + n_out:n_in + 2 * n + n_out]
        scr, ex_sems = refs[n_in + 2 * n + n_out:n_in + 2 * n + n_out + n_scr], refs[n_in + 2 * n + n_out + n_scr:]
        start, wait = _exchange_ops(srcs, landed, ex_sems, gather)
        first = functools.reduce(lambda a, b: a & b, [pl.program_id(d) == 0 for d in range(len(grid))])
        last = functools.reduce(lambda a, b: a & b, [pl.program_id(d) == grid[d] - 1 for d in range(len(grid))])
        pl.when(first)(start)
        body(*ins, *outs, *scr)
        pl.when(last)(wait)

    res = pl.pallas_call(
        both, name=name, grid=grid, in_specs=list(in_specs) + specs, out_specs=list(out_specs) + specs,
        out_shape=list(out_shape) + shapes, scratch_shapes=list(scratch) + sems, compiler_params=_cp(*sem),
    )(*args, *arrs)
    return list(res[:n_out]), list(res[n_out:])


def adamw(name, g, w, m, v, tr, sel=None):
    slots = g.ndim >= 3
    r, c = w.shape
    c1, c2 = 1.0 - ADAM_B1 ** ADAM_STEP, 1.0 - ADAM_B2 ** ADAM_STEP

    def body(g_ref, w_ref, m_ref, v_ref, go, do, mo, vo):
        if slots:
            gg = g_ref[0].astype(F32)
            for j in range(1, N_DEV):
                gg = gg + g_ref[j].astype(F32)
        else:
            gg = g_ref[...]
        mn = ADAM_B1 * m_ref[...] + (1.0 - ADAM_B1) * gg
        vn = ADAM_B2 * v_ref[...] + (1.0 - ADAM_B2) * (gg * gg)
        go[...], mo[...], vo[...] = gg, mn, vn
        do[...] = -ADAM_LR * ((mn / c1) / (jnp.sqrt(vn / c2) + ADAM_EPS) + ADAM_WD * w_ref[...])

    blk = pl.BlockSpec((tr, c), lambda i: (i, 0))
    if g.ndim == 4:
        gspec = pl.BlockSpec((N_DEV, None, tr, c), lambda i: (0, sel, i, 0))
    else:
        gspec = pl.BlockSpec((N_DEV, tr, c), lambda i: (0, i, 0)) if slots else blk
    return pl.pallas_call(
        body, name=name, grid=(r // tr,), in_specs=[gspec, blk, blk, blk], out_specs=[blk] * 4,
        out_shape=[jax.ShapeDtypeStruct((r, c), F32)] * 4, compiler_params=_cp("parallel"),
    )(g, w, m, v)


def _lane_rows(n):
    return -(-n // (8 * LANE)) * 8


def _pack(arrs):
    pieces = []
    for a in arrs:
        n = math.prod(a.shape)
        flat = a.reshape(-1).astype(F32)
        pieces.append(jnp.pad(flat, (0, _lane_rows(n) * LANE - n)).reshape(_lane_rows(n), LANE))
    return jnp.concatenate(pieces, axis=0)


def _unpack(buf, shapes):
    out, off = [], 0
    for s in shapes:
        n = math.prod(s)
        out.append(buf[off:off + _lane_rows(n)].reshape(-1)[:n].reshape(s))
        off += _lane_rows(n)
    return out


def _cols_to_full(g):
    return jnp.transpose(g, (1, 0, 2)).reshape(g.shape[1], N_DEV * g.shape[2])


def _full_to_cols(w):
    r, c = w.shape
    return jnp.transpose(w.reshape(r, N_DEV, c // N_DEV), (1, 0, 2))


FF_CHUNK = D_FF // 2
FFN_TM = 256


def _resident(shape):
    return pl.BlockSpec(shape, lambda i: (0,) * len(shape), pipeline_mode=pl.Buffered(1))


def _ffn_fwd(tag, x, sc, sh, g, w1, w3, w2, lg, lb, seq, tm, ride=None):
    t = x.shape[0]
    tm = min(FFN_TM, tm)
    tps = seq // tm
    ln = _res_ln(0.5)

    def body(x_ref, sc_ref, sh_ref, g_ref, lg_ref, lb_ref, w1_ref, w3_ref, w2_ref, y_ref, h_ref, a_ref, b_ref, f_ref):
        xv = x_ref[...]
        h = (xv * (1.0 + sc_ref[0]) + sh_ref[0]).astype(BF16)
        h_ref[...] = h
        acc = jnp.zeros((tm, D_MODEL), F32)
        for j in range(D_FF // FF_CHUNK):
            sl = slice(j * FF_CHUNK, (j + 1) * FF_CHUNK)
            a = _dg(h, w1_ref[:, sl], 1, 0)
            b = _dg(h, w3_ref[:, sl], 1, 0)
            a_ref[:, sl] = a
            b_ref[:, sl] = b
            acc = acc + _dg(_silu(a) * b, w2_ref[sl, :], 1, 0)
        f_ref[...] = acc
        y_ref[...] = ln(xv, acc, g_ref[0], lg_ref[...], lb_ref[...])[0]

    row = lambda c: pl.BlockSpec((tm, c), lambda i: (i, 0))
    per_seq = pl.BlockSpec((1, 1, D_MODEL), lambda i: (i // tps, 0, 0))
    vec = pl.BlockSpec((1, D_MODEL), lambda i: (0, 0))
    (y, h, a, b, f), landed = hosted_call(
        body, name=tag + "_fwd", grid=(t // tm,),
        in_specs=[row(D_MODEL), per_seq, per_seq, per_seq, vec, vec,
                  _resident((D_MODEL, D_FF)), _resident((D_MODEL, D_FF)), _resident((D_FF, D_MODEL))],
        out_specs=[row(D_MODEL), row(D_MODEL), row(D_FF), row(D_FF), row(D_MODEL)],
        out_shape=[jax.ShapeDtypeStruct((t, D_MODEL), F32), jax.ShapeDtypeStruct((t, D_MODEL), BF16),
                   jax.ShapeDtypeStruct((t, D_FF), F32), jax.ShapeDtypeStruct((t, D_FF), F32),
                   jax.ShapeDtypeStruct((t, D_MODEL), F32)],
        args=(x, sc, sh, g, lg, lb, w1, w3, w2), ride=ride)
    return y, (h, a, b, f), landed


def _ffn_bwd(tag, dy, x, sc, sh, g, w1, w3, w2, lg, lb, res, seq, tm, ride=None, chain=None):
    h, a, b, f = res
    t = x.shape[0]
    tmk = min(FFN_TM, tm)
    tps = seq // tmk
    ln = _res_ln(0.5)

    def body(dy_ref, x_ref, f_ref, a_ref, b_ref, sc_ref, sh_ref, g_ref, lg_ref, lb_ref, w1_ref, w3_ref, w2_ref,
             dx_ref, da_ref, db_ref, s_ref, df_ref, dsc_ref, dsh_ref, dg_ref, dlg_ref, dlb_ref):
        i = pl.program_id(0)

        @pl.when(i % tps == 0)
        def _():
            for r in (dsc_ref, dsh_ref, dg_ref):
                r[...] = jnp.zeros_like(r)

        @pl.when(i == 0)
        def _():
            dlg_ref[...] = jnp.zeros_like(dlg_ref)
            dlb_ref[...] = jnp.zeros_like(dlb_ref)

        xv = x_ref[...]
        _, pull = jax.vjp(ln, xv, f_ref[...], g_ref[0], lg_ref[...], lb_ref[...])
        dx_res, df, dg, dlg, dlb = pull((dy_ref[...],))
        dfb = df.astype(BF16)
        df_ref[...] = dfb
        dh = jnp.zeros((tmk, D_MODEL), F32)
        for j in range(D_FF // FF_CHUNK):
            sl = slice(j * FF_CHUNK, (j + 1) * FF_CHUNK)
            ds = _dg(dfb, w2_ref[sl, :], 1, 1)
            av, bv = a_ref[:, sl], b_ref[:, sl]
            sg = jax.nn.sigmoid(av)
            si = av * sg
            s_ref[:, sl] = (si * bv).astype(BF16)
            da = (ds * bv * (sg * (1.0 + av * (1.0 - sg)))).astype(BF16)
            db = (ds * si).astype(BF16)
            da_ref[:, sl] = da
            db_ref[:, sl] = db
            dh = dh + _dg(da, w1_ref[:, sl], 1, 1) + _dg(db, w3_ref[:, sl], 1, 1)
        dx_ref[...] = dx_res + dh * (1.0 + sc_ref[0])
        dsc_ref[0] += jnp.sum(dh * xv, axis=0, keepdims=True)
        dsh_ref[0] += jnp.sum(dh, axis=0, keepdims=True)
        dg_ref[0] += dg
        dlg_ref[...] += dlg
        dlb_ref[...] += dlb

    row = lambda c: pl.BlockSpec((tmk, c), lambda i: (i, 0))
    per_seq = pl.BlockSpec((1, 1, D_MODEL), lambda i: (i // tps, 0, 0))
    vec = pl.BlockSpec((1, D_MODEL), lambda i: (0, 0))
    seq_shape = jax.ShapeDtypeStruct(sc.shape, F32)
    vec_shape = jax.ShapeDtypeStruct((1, D_MODEL), F32)
    (dx, da, db, s, df, dsc, dsh, dg, dlg, dlb), landed = hosted_call(
        body, name=tag + "_bwd", grid=(t // tmk,),
        in_specs=[row(D_MODEL), row(D_MODEL), row(D_MODEL), row(D_FF), row(D_FF), per_seq, per_seq, per_seq, vec, vec,
                  _resident((D_MODEL, D_FF)), _resident((D_MODEL, D_FF)), _resident((D_FF, D_MODEL))],
        out_specs=[row(D_MODEL), row(D_FF), row(D_FF), row(D_FF), row(D_MODEL), per_seq, per_seq, per_seq, vec, vec],
        out_shape=[jax.ShapeDtypeStruct((t, D_MODEL), F32), jax.ShapeDtypeStruct((t, D_FF), BF16),
                   jax.ShapeDtypeStruct((t, D_FF), BF16), jax.ShapeDtypeStruct((t, D_FF), BF16),
                   jax.ShapeDtypeStruct((t, D_MODEL), BF16), seq_shape, seq_shape, seq_shape, vec_shape, vec_shape],
        args=(dy, x, f, a, b, sc, sh, g, lg, lb, w1, w3, w2), ride=ride)
    if chain is None:
        dw2 = mm_tn(tag + "_dw2", s, df, D_FF // 2, D_MODEL, tm, BF16)
        dw1 = mm_tn(tag + "_dw1", h, da, D_MODEL, D_FF // 2, tm, BF16)
        dw3 = mm_tn(tag + "_dw3", h, db, D_MODEL, D_FF // 2, tm, BF16)
        return dx, (dsh, dsc, dg), (dw1, dw3, dw2, dlg, dlb), landed
    dw2, landed = mm_tn(tag + "_dw2", s, df, D_FF // 2, D_MODEL, tm, BF16, ride=chain((dsh, dsc, dg), dlg, dlb))
    dw1, (s_w2,) = mm_tn(tag + "_dw1", h, da, D_MODEL, D_FF // 2, tm, BF16,
                         ride=([dw2.reshape(N_DEV, D_FF // N_DEV, D_MODEL)], [False]))
    dw3, (s_w1,) = mm_tn(tag + "_dw3", h, db, D_MODEL, D_FF // 2, tm, BF16, ride=([_full_to_cols(dw1)], [False]))
    return dx, (s_w1, dw3, s_w2), landed


def kernel(x, c, w_ada, b_ada, ffn1_w1, ffn1_w3, ffn1_w2, ln1_g, ln1_b, w_in, conv_w, conv_b, dt_bias, a_log, d_ssd, ssd_norm_w, s5_a_re, s5_a_im, s5_log_dt, s5_b_re, s5_b_im, s5_c_re, s5_c_im, s5_d, w_glu, b_glu, w_out, ln2_g, ln2_b, ffn2_w1, ffn2_w3, ffn2_w2, ln3_g, ln3_b, loss_target, m_w_ada, m_b_ada, m_ffn1_w1, m_ffn1_w3, m_ffn1_w2, m_ln1_g, m_ln1_b, m_w_in, m_conv_w, m_conv_b, m_dt_bias, m_a_log, m_d_ssd, m_ssd_norm_w, m_s5_a_re, m_s5_a_im, m_s5_log_dt, m_s5_b_re, m_s5_b_im, m_s5_c_re, m_s5_c_im, m_s5_d, m_w_glu, m_b_glu, m_w_out, m_ln2_g, m_ln2_b, m_ffn2_w1, m_ffn2_w3, m_ffn2_w2, m_ln3_g, m_ln3_b, v_w_ada, v_b_ada, v_ffn1_w1, v_ffn1_w3, v_ffn1_w2, v_ln1_g, v_ln1_b, v_w_in, v_conv_w, v_conv_b, v_dt_bias, v_a_log, v_d_ssd, v_ssd_norm_w, v_s5_a_re, v_s5_a_im, v_s5_log_dt, v_s5_b_re, v_s5_b_im, v_s5_c_re, v_s5_c_im, v_s5_d, v_w_glu, v_b_glu, v_w_out, v_ln2_g, v_ln2_b, v_ffn2_w1, v_ffn2_w3, v_ffn2_w2, v_ln3_g, v_ln3_b):
    given = dict(locals())
    bsz, seq, _ = x.shape
    t = bsz * seq
    tm = min(512, seq)
    me = 4 * lax.axis_index("x") + 2 * lax.axis_index("y") + lax.axis_index("c")
    x0 = x.reshape(t, D_MODEL)
    target = loss_target.reshape(t, D_MODEL)

    g_col1, g_row1, g_c = gather_two_level(
        "gather_ffn1", [jnp.stack([ffn1_w1[0], ffn1_w3[0]]).astype(BF16), ffn1_w2[0].astype(BF16), c])
    f1w1, f1w3 = [_cols_to_full(g_col1[:, k]) for k in range(2)]
    f1w2 = g_row1.reshape(D_FF, D_MODEL)
    c_all = g_c.reshape(N_DEV * bsz, D_MODEL)

    n_loc = w_ada.shape[2]
    b_loc = lax.dynamic_slice(b_ada, (0, me * n_loc), (1, n_loc))
    mod_cols = ada_fwd(c_all, w_ada[0], b_loc)
    g_mod, = exchange("gather_mod", [mod_cols], [True])
    mine = lax.dynamic_slice(g_mod, (0, me * bsz, 0), (N_DEV, bsz, n_loc))
    mod = jnp.transpose(mine, (1, 0, 2)).reshape(bsz, N_MOD, 1, D_MODEL)
    sh1, sc1, g1, sh2, sc2, g2, sh3, sc3, g3 = [mod[:, k] for k in range(N_MOD)]

    x1, res1, (g_win, g_glu, g_out, g_conv) = _ffn_fwd(
        "ffn1", x0, sc1, sh1, g1, f1w1, f1w3, f1w2, ln1_g, ln1_b, seq, tm,
        ride=([w_in[0].astype(BF16), w_glu[0].astype(BF16), w_out[0].astype(BF16), conv_w[0]], [True] * 4))
    win = _cols_to_full(g_win)
    wglu = g_glu.reshape(S5_WIDTH, S5_WIDTH).astype(F32)
    wout = g_out.reshape(D_MODEL, D_MODEL)
    wo_ssd, wo_s5 = wout[:SSD_WIDTH], wout[SSD_WIDTH:]
    convw = jnp.transpose(g_conv, (1, 0, 2)).reshape(CONV_K, CONV_CH)
    w_z, w_xbc = win[:, :SSD_WIDTH], win[:, SSD_WIDTH:SSD_WIDTH + CONV_CH]
    w_dt = win[:, SSD_WIDTH + CONV_CH:SSD_WIDTH + CONV_CH + SSD_HEADS]
    w_u = win[:, SSD_WIDTH + CONV_CH + SSD_HEADS:]
    dt_pad = [jnp.pad(w_dt[:, HEADS_PER_GROUP * g:HEADS_PER_GROUP * (g + 1)], ((0, 0), (0, LANE - HEADS_PER_GROUP)))
              for g in range(SSD_GROUPS)]
    w_dtp = jnp.concatenate(dt_pad, axis=1)
    w_proj = jnp.concatenate([w_xbc, w_z, w_u, w_dtp], axis=1)

    h2, = rowwise_fwd("mix_mod", f_modulate, [x1], [sc2, sh2], [], [(D_MODEL, BF16)], seq, tm)
    proj = mm_nn("mix_proj", [h2], [w_proj], tm, P_COLS // 2)
    xc = conv_fwd(proj, convw, conv_b, seq, tm)
    dtb = jnp.pad(dt_bias.reshape(SSD_GROUPS, 1, HEADS_PER_GROUP), ((0, 0), (0, 0), (0, LANE - HEADS_PER_GROUP)))
    alog = jnp.pad(a_log.reshape(SSD_GROUPS, 1, HEADS_PER_GROUP), ((0, 0), (0, 0), (0, LANE - HEADS_PER_GROUP)))
    dcol = jnp.pad(d_ssd.reshape(SSD_GROUPS, HEADS_PER_GROUP, 1), ((0, 0), (0, LANE - HEADS_PER_GROUP), (0, 0)))
    nw = ssd_norm_w.reshape(SSD_GROUPS, 1, GROUP_COLS)
    y_ssd, hprev = ssd_fwd(xc, proj, dtb, alog, dcol, nw, bsz, seq)

    a_re2, a_im2, ldt2 = s5_a_re[0], s5_a_im[0], s5_log_dt.reshape(S5_GROUPS, 1)
    ab_re, ab_im, f_re, f_im = _whole(_disc_a, "s5_disc_a", [a_re2, a_im2, ldt2], [(S5_GROUPS, S5_STATE)] * 4)
    b_re2, b_im2 = s5_b_re.reshape(S5_COLS, S5_GROUP_CH), s5_b_im.reshape(S5_COLS, S5_GROUP_CH)
    fr_col, fi_col = f_re.reshape(S5_COLS, 1), f_im.reshape(S5_COLS, 1)
    bb_re, bb_im = _whole(_disc_b, "s5_disc_b", [fr_col, fi_col, b_re2, b_im2], [(S5_COLS, S5_GROUP_CH)] * 2)
    wb_re, wb_im = _blockdiag_b(bb_re).astype(BF16), _blockdiag_b(bb_im).astype(BF16)
    wc_re, wc_im = _blockdiag_c(s5_c_re[0]).astype(BF16), _blockdiag_c(s5_c_im[0]).astype(BF16)
    dt5 = jnp.exp(ldt2)
    lam_re, lam_im = (dt5 * a_re2).reshape(1, S5_COLS), (dt5 * a_im2).reshape(1, S5_COLS)
    sf_re, sf_im, sb_re, sb_im, cf_re, cf_im, cb_re, cb_im = s5_tables(lam_re, lam_im)
    d5 = s5_d.reshape(S5_Q, 1, LANE)
    (y5, xr_all, xi_all), (g_col2, g_row2) = s5_fwd(
        proj, wb_re, wb_im, wc_re, wc_im, sf_re, sf_im, cf_re, cf_im, d5, bsz, seq,
        ride=([jnp.stack([ffn2_w1[0], ffn2_w3[0]]).astype(BF16), ffn2_w2[0].astype(BF16)], [True] * 2))
    f2w1, f2w3 = [_cols_to_full(g_col2[:, k]) for k in range(2)]
    f2w2 = g_row2.reshape(D_FF, D_MODEL)
    o5, = rowwise_fwd("s5_glu", f_glu, [y5], [], [wglu, b_glu], [(S5_WIDTH, F32)], seq, tm)

    mix = mm_nn("mix_out", [y_ssd, o5], [wo_ssd, wo_s5], tm, D_MODEL)
    x2, = rowwise_fwd("mix_ln", _res_ln(1.0), [x1, mix], [g2], [ln2_g, ln2_b], [(D_MODEL, F32)], seq, tm)

    x3, res3, _ = _ffn_fwd("ffn2", x2, sc3, sh3, g3, f2w1, f2w3, f2w2, ln3_g, ln3_b, seq, tm)
    dy, loss_loc = loss_head(x3, target, tm)

    dx2, dmod3, (d_f2w1, d_f2w3, d_f2w2, d_ln3g, d_ln3b), _ = _ffn_bwd(
        "ffn2", dy, x2, sc3, sh3, g3, f2w1, f2w3, f2w2, ln3_g, ln3_b, res3, seq, tm)

    (dx1_a, dmix), (dg2,), (d_ln2g, d_ln2b) = rowwise_bwd(
        "mix_ln_b", _res_ln(1.0), [x1, mix], [g2], [ln2_g, ln2_b], [dx2], seq, tm, [F32, BF16])
    d_wo = jnp.concatenate([mm_tn("mix_dwo_ssd", y_ssd, dmix, SSD_WIDTH, D_MODEL, tm, BF16),
                            mm_tn("mix_dwo_s5", o5, dmix, S5_WIDTH, D_MODEL, tm, BF16)], axis=0)
    dy_ssd = mm_nt("mix_dy_ssd", [dmix], [wo_ssd], tm, SSD_WIDTH)
    do5 = mm_nt("mix_do5", [dmix], [wo_s5], tm, S5_WIDTH)

    (dy5,), _, (d_wglu, d_bglu) = rowwise_bwd("s5_glu_b", f_glu, [y5], [], [wglu, b_glu], [do5], seq, tm, [F32])
    (du, dwbr, dwbi, dwcr, dwci, dab_re, dab_im, dd5), (s_col2, s_row2, s_out, s_glu) = s5_bwd(
        proj, wb_re, wb_im, wc_re, wc_im, sb_re, sb_im, cb_re, cb_im, d5, xr_all, xi_all, dy5, bsz, seq,
        ride=([jnp.stack([_full_to_cols(d_f2w1), _full_to_cols(d_f2w3)], axis=1),
               d_f2w2.reshape(N_DEV, D_FF // N_DEV, D_MODEL), d_wo.reshape(N_DEV, D_MODEL // N_DEV, D_MODEL),
               d_wglu.reshape(N_DEV, S5_WIDTH // N_DEV, S5_WIDTH).astype(BF16)], [False] * 4))
    dbb_re, dbb_im = _unblock_b(dwbr), _unblock_b(dwbi)
    dfr_col, dfi_col, d_b_re, d_b_im = _whole_vjp(_disc_b, "s5_disc_b_b", [fr_col, fi_col, b_re2, b_im2],
                                                  [dbb_re, dbb_im])
    d_a_re, d_a_im, d_ldt = _whole_vjp(
        _disc_a, "s5_disc_a_b", [a_re2, a_im2, ldt2],
        [dab_re.reshape(S5_GROUPS, S5_STATE), dab_im.reshape(S5_GROUPS, S5_STATE),
         dfr_col.reshape(S5_GROUPS, S5_STATE), dfi_col.reshape(S5_GROUPS, S5_STATE)])
    d_c_re, d_c_im = _unblock_c(dwcr), _unblock_c(dwci)

    dxs, dbm, dcm, ddt, dz, ddtb, dalog, ddcol, dnw = ssd_bwd(xc, proj, dtb, alog, dcol, nw, hprev, dy_ssd, bsz, seq)
    dpre, d_convw, d_convb = conv_bwd_pre(proj, convw, conv_b, dxs, dbm, dcm, seq, tm)
    dxbc = conv_bwd_x(dpre, convw, seq, tm)

    dw_xbc = mm_tn("mix_dw_xbc", h2, dxbc, D_MODEL, CONV_CH, tm, BF16)
    dw_z = mm_tn("mix_dw_z", h2, dz, D_MODEL, SSD_WIDTH, tm, BF16)
    dw_u = mm_tn("mix_dw_u", h2, du, D_MODEL, S5_WIDTH, tm, BF16)
    dw_dt = mm_tn("mix_dw_dt", h2, ddt, D_MODEL, 2 * LANE, tm, BF16)
    dw_dt8 = jnp.concatenate([dw_dt[:, LANE * g:LANE * g + HEADS_PER_GROUP] for g in range(SSD_GROUPS)], axis=1)
    d_win = jnp.concatenate([dw_z, dw_xbc, dw_dt8, dw_u], axis=1)
    dh2, (s_win,) = mm_nt("mix_dh", [dxbc, dz, du, ddt], [w_xbc, w_z, w_u, w_dtp], tm, D_MODEL,
                          ride=([_full_to_cols(d_win)], [False]))
    (dx1,), (dsc2, dsh2), _ = rowwise_bwd("mix_mod_b", f_modulate, [x1], [sc2, sh2], [], [dh2], seq, tm, [F32],
                                          add_rows={0: dx1_a})

    packing = {}

    def small_and_dmod(dmod1, d_ln1g, d_ln1b):
        dmod = jnp.concatenate(list(dmod1) + [dsh2, dsc2, dg2] + list(dmod3), axis=1).reshape(bsz, N_MOD * D_MODEL)
        small = _small_grads(d_ln1g, d_ln1b)
        packing["names"] = list(small)
        packing["shapes"] = [small[k].shape for k in small]
        return [_pack(list(small.values())), dmod], [True, True]

    def _small_grads(d_ln1g, d_ln1b):
        return {
            "ln1_g": d_ln1g, "ln1_b": d_ln1b, "conv_w": d_convw, "conv_b": d_convb,
            "dt_bias": ddtb[:, 0, :HEADS_PER_GROUP].reshape(1, SSD_HEADS),
            "a_log": dalog[:, 0, :HEADS_PER_GROUP].reshape(1, SSD_HEADS),
            "d_ssd": ddcol[:, :HEADS_PER_GROUP, 0].reshape(1, SSD_HEADS),
            "ssd_norm_w": dnw.reshape(1, SSD_WIDTH),
            "s5_a_re": d_a_re[None], "s5_a_im": d_a_im[None], "s5_log_dt": d_ldt.reshape(1, S5_GROUPS),
            "s5_b_re": d_b_re.reshape(s5_b_re.shape), "s5_b_im": d_b_im.reshape(s5_b_im.shape),
            "s5_c_re": d_c_re[None], "s5_c_im": d_c_im[None], "s5_d": dd5.reshape(1, S5_WIDTH),
            "b_glu": d_bglu, "ln2_g": d_ln2g, "ln2_b": d_ln2b, "ln3_g": d_ln3g, "ln3_b": d_ln3b,
            "loss": loss_loc.reshape(1, 1),
        }

    dx0, (s_f1w1, d_f1w3, s_f1w2), (s_small, s_dmod) = _ffn_bwd(
        "ffn1", dx1, x0, sc1, sh1, g1, f1w1, f1w3, f1w2, ln1_g, ln1_b, res1, seq, tm, chain=small_and_dmod)
    names, shapes = packing["names"], packing["shapes"]
    s_f1w3, = exchange("sum_grads", [_full_to_cols(d_f1w3)], [False])

    out = {"grad_x": dx0.reshape(x.shape)}

    def put(name, res, shape):
        for key, val in zip(("grad_", "delta_", "new_m_", "new_v_"), res):
            out[key + name] = val.reshape(shape)

    for name, slots, k in (("ffn1_w1", s_f1w1, None), ("ffn1_w3", s_f1w3, None), ("ffn2_w1", s_col2, 0),
                           ("ffn2_w3", s_col2, 1)):
        w = given[name]
        put(name, adamw("adam_" + name, slots, w[0], given["m_" + name][0], given["v_" + name][0], 256, sel=k), w.shape)
    for name, slots in (("ffn1_w2", s_f1w2), ("ffn2_w2", s_row2)):
        w = given[name]
        put(name, adamw("adam_" + name, slots, w[0], given["m_" + name][0], given["v_" + name][0], 176), w.shape)
    put("w_in", adamw("adam_w_in", s_win, w_in[0], m_w_in[0], v_w_in[0], 256), w_in.shape)
    put("w_glu", adamw("adam_w_glu", s_glu, w_glu[0], m_w_glu[0], v_w_glu[0], 64), w_glu.shape)
    put("w_out", adamw("adam_w_out", s_out, w_out[0], m_w_out[0], v_w_out[0], 128), w_out.shape)

    dmod_all = s_dmod.reshape(N_DEV * bsz, N_MOD * D_MODEL)
    g_bada, g_wada = ada_bwd(c_all, dmod_all, lax.dynamic_slice(dmod_all, (0, me * n_loc), (N_DEV * bsz, n_loc)))
    put("w_ada", adamw("adam_w_ada", g_wada, w_ada[0], m_w_ada[0], v_w_ada[0], 256), w_ada.shape)
    put("b_ada", adamw("adam_b_ada", g_bada, b_ada, m_b_ada, v_b_ada, 1), b_ada.shape)

    not_params = {"conv_w": jnp.zeros((CONV_K, CONV_CH), F32), "loss": jnp.zeros((1, 1), F32)}
    pw, pm, pv = [_pack([not_params[k] if k in not_params else given[pre + k] for k in names]) for pre in ("", "m_", "v_")]
    res_small = adamw("adam_small", s_small, pw, pm, pv, pw.shape[0])
    parts = [_unpack(r, shapes) for r in res_small]
    for i, k in enumerate(names):
        if k not in not_params:
            put(k, [p[i] for p in parts], given[k].shape)
    out["loss"] = parts[0][names.index("loss")][0, 0]
    g_cw = lax.dynamic_slice(parts[0][names.index("conv_w")], (0, me * LANE), (CONV_K, LANE))
    put("conv_w", adamw("adam_conv_w", g_cw, conv_w[0], m_conv_w[0], v_conv_w[0], CONV_K), conv_w.shape)

    order = ["w_ada", "b_ada", "ffn1_w1", "ffn1_w3", "ffn1_w2", "ln1_g", "ln1_b", "w_in", "conv_w", "conv_b", "dt_bias",
             "a_log", "d_ssd", "ssd_norm_w", "s5_a_re", "s5_a_im", "s5_log_dt", "s5_b_re", "s5_b_im", "s5_c_re",
             "s5_c_im", "s5_d", "w_glu", "b_glu", "w_out", "ln2_g", "ln2_b", "ffn2_w1", "ffn2_w3", "ffn2_w2", "ln3_g",
             "ln3_b"]
    return (out["loss"], out["grad_x"], *[out[p + n] for p in ("grad_", "delta_", "new_m_", "new_v_") for n in order])
```

```python
import functools
import math

import jax
import jax.numpy as jnp
from jax import lax
from jax.experimental import pallas as pl
from jax.experimental.pallas import tpu as pltpu

F32 = jnp.float32
BF16 = jnp.bfloat16
HI = lax.Precision.HIGHEST
MESH = pl.DeviceIdType.MESH

N_DEV = 8
D_MODEL = 1024
D_FF = 2816
N_MOD = 9
SSD_WIDTH = 512
SSD_HEADS = 8
SSD_HEAD_DIM = 64
SSD_GROUPS = 2
SSD_STATE = 128
SSD_CHUNK = 128
GROUP_COLS = SSD_WIDTH // SSD_GROUPS
HEADS_PER_GROUP = SSD_HEADS // SSD_GROUPS
CONV_K = 4
CONV_CH = 1024
S5_WIDTH = 512
S5_GROUPS = 32
S5_GROUP_CH = 16
S5_STATE = 64
S5_COLS = S5_GROUPS * S5_STATE
S5_Q = 4
S5_CHUNK = 512
ALPHA = 2.0 ** 0.25
LN_EPS = 1e-5
LANE = 128
HALO = 8

P_XBC, P_Z, P_U, P_DT = 0, 1024, 1536, 2048
P_COLS = 2048 + SSD_GROUPS * LANE

ADAM_LR, ADAM_B1, ADAM_B2, ADAM_EPS, ADAM_WD, ADAM_STEP = 0.001, 0.9, 0.999, 1e-08, 0.01, 10

VMEM_LIMIT = 56 * 1024 * 1024


def _cp(*sem):
    return pltpu.CompilerParams(dimension_semantics=sem if sem else None, vmem_limit_bytes=VMEM_LIMIT)


def _dg(a, b, ca, cb):
    return lax.dot_general(a.astype(BF16), b.astype(BF16), (((ca,), (cb,)), ((), ())), preferred_element_type=F32)


@jax.custom_vjp
def bdot_nn(a, b):
    return _dg(a, b, 1, 0)


bdot_nn.defvjp(lambda a, b: (_dg(a, b, 1, 0), (a, b)),
               lambda r, g: (_dg(g, r[1], 1, 1), _dg(r[0], g, 0, 0)))


@jax.custom_vjp
def bdot_nt(a, b):
    return _dg(a, b, 1, 1)


bdot_nt.defvjp(lambda a, b: (_dg(a, b, 1, 1), (a, b)),
               lambda r, g: (_dg(g, r[1], 1, 0), _dg(g, r[0], 0, 0)))


@jax.custom_vjp
def bdot_tn(a, b):
    return _dg(a, b, 0, 0)


bdot_tn.defvjp(lambda a, b: (_dg(a, b, 0, 0), (a, b)),
               lambda r, g: (_dg(r[1], g, 1, 1), _dg(r[0], g, 1, 0)))


def _take_col(z):
    @jax.custom_vjp
    def take(x):
        return x[:, z:z + 1]

    def bwd(shape, g):
        hot = (lax.broadcasted_iota(jnp.int32, (1, shape[1]), 1) == z).astype(F32)
        return (g * hot,)

    take.defvjp(lambda x: (x[:, z:z + 1], x.shape), bwd)
    return take


def _take_row(z):
    @jax.custom_vjp
    def take(x):
        return x[z:z + 1, :]

    def bwd(shape, g):
        hot = (lax.broadcasted_iota(jnp.int32, (shape[0], 1), 0) == z).astype(F32)
        return (hot * g,)

    take.defvjp(lambda x: (x[z:z + 1, :], x.shape), bwd)
    return take


def _view(a):
    return a if isinstance(a, tuple) else (a, 0, a.shape[1])


def _col_spec(view, rows, width, index):
    _, off, _ = view
    assert off % width == 0
    return pl.BlockSpec((rows, width), lambda *g: (index(*g)[0], off // width + index(*g)[1]))


def _rw_in_specs(rows, bps, gps, tm, tps):
    specs = [_col_spec(_view(r), tm, _view(r)[2], lambda i: (i, 0)) for r in rows]
    specs += [pl.BlockSpec((1, 1, b.shape[2]), lambda i: (i // tps, 0, 0)) for b in bps]
    specs += [pl.BlockSpec(g.shape, lambda i, nd=g.ndim: (0,) * nd) for g in gps]
    return specs


def _rw_vals(refs, nr, nb, ng):
    vals = [r[...] for r in refs[:nr]]
    vals += [b[0] for b in refs[nr:nr + nb]]
    vals += [g[...] for g in refs[nr + nb:nr + nb + ng]]
    return vals


def rowwise_fwd(name, f, rows, bps, gps, outs, seq, tm):
    t = _view(rows[0])[0].shape[0]
    tps = seq // tm
    nr, nb, ng = len(rows), len(bps), len(gps)

    def body(*refs):
        res = f(*_rw_vals(refs, nr, nb, ng))
        for o, v in zip(refs[nr + nb + ng:], res):
            o[...] = v.astype(o.dtype)

    return pl.pallas_call(
        body, name=name, grid=(t // tm,),
        in_specs=_rw_in_specs(rows, bps, gps, tm, tps),
        out_specs=[pl.BlockSpec((tm, c), lambda i: (i, 0)) for c, _ in outs],
        out_shape=[jax.ShapeDtypeStruct((t, c), d) for c, d in outs],
        compiler_params=_cp("arbitrary"),
    )(*[_view(r)[0] for r in rows], *bps, *gps)


def rowwise_bwd(name, f, rows, bps, gps, douts, seq, tm, row_grads, add_rows=None):
    add_rows = add_rows or {}
    t = _view(rows[0])[0].shape[0]
    tps = seq // tm
    nr, nb, ng, nd = len(rows), len(bps), len(gps), len(douts)
    want = [k for k in range(nr) if row_grads[k] is not None]
    adds = sorted(add_rows)
    n_in = nr + nb + ng + nd + len(adds)

    def body(*refs):
        vals = _rw_vals(refs, nr, nb, ng)
        dvals = tuple(r[...] for r in refs[nr + nb + ng:nr + nb + ng + nd])
        add_refs = dict(zip(adds, refs[nr + nb + ng + nd:n_in]))
        out_refs = refs[n_in:]
        _, pull = jax.vjp(f, *vals)
        grads = pull(dvals)
        i = pl.program_id(0)
        for o, k in zip(out_refs, want):
            g = grads[k]
            if k in add_refs:
                g = g + add_refs[k][...]
            o[...] = g.astype(o.dtype)
        for j in range(nb):
            o = out_refs[len(want) + j]

            @pl.when(i % tps == 0)
            def _(o=o):
                o[...] = jnp.zeros_like(o)

            o[0] = o[0] + grads[nr + j]
        for j in range(ng):
            o = out_refs[len(want) + nb + j]

            @pl.when(i == 0)
            def _(o=o):
                o[...] = jnp.zeros_like(o)

            o[...] = o[...] + grads[nr + nb + j]

    in_specs = _rw_in_specs(rows, bps, gps, tm, tps)
    in_specs += [pl.BlockSpec((tm, d.shape[1]), lambda i: (i, 0)) for d in douts]
    in_specs += [pl.BlockSpec((tm, add_rows[k].shape[1]), lambda i: (i, 0)) for k in adds]
    out_specs = [pl.BlockSpec((tm, _view(rows[k])[2]), lambda i: (i, 0)) for k in want]
    out_shape = [jax.ShapeDtypeStruct((t, _view(rows[k])[2]), row_grads[k]) for k in want]
    out_specs += [pl.BlockSpec((1, 1, b.shape[2]), lambda i: (i // tps, 0, 0)) for b in bps]
    out_shape += [jax.ShapeDtypeStruct(b.shape, F32) for b in bps]
    out_specs += [pl.BlockSpec(g.shape, lambda i, n=g.ndim: (0,) * n) for g in gps]
    out_shape += [jax.ShapeDtypeStruct(g.shape, F32) for g in gps]
    res = pl.pallas_call(
        body, name=name, grid=(t // tm,), in_specs=in_specs, out_specs=out_specs, out_shape=out_shape,
        compiler_params=_cp("arbitrary"),
    )(*[_view(r)[0] for r in rows], *bps, *gps, *douts, *[add_rows[k] for k in adds])
    nw = len(want)
    return res[:nw], res[nw:nw + nb], res[nw + nb:]


def mm_nn(name, xs, ws, tm, tn, out_dtype=F32):
    views = [_view(x) for x in xs]
    t, n, k = views[0][0].shape[0], ws[0].shape[1], len(xs)

    def body(*refs):
        acc = _dg(refs[0][...], refs[k][...], 1, 0)
        for i in range(1, k):
            acc = acc + _dg(refs[i][...], refs[k + i][...], 1, 0)
        refs[2 * k][...] = acc.astype(out_dtype)

    in_specs = [_col_spec(v, tm, v[2], lambda i, j: (i, 0)) for v in views]
    in_specs += [pl.BlockSpec((w.shape[0], tn), lambda i, j: (0, j)) for w in ws]
    return pl.pallas_call(
        body, name=name, grid=(t // tm, n // tn), in_specs=in_specs,
        out_specs=pl.BlockSpec((tm, tn), lambda i, j: (i, j)),
        out_shape=jax.ShapeDtypeStruct((t, n), out_dtype),
        compiler_params=_cp("parallel", "parallel"),
    )(*[v[0] for v in views], *ws)


def mm_nt(name, dys, ws, tm, tk, out_dtype=F32, ride=None):
    views = [_view(d) for d in dys]
    t, kk, k = views[0][0].shape[0], ws[0].shape[0], len(dys)

    def body(*refs):
        acc = _dg(refs[0][...], refs[k][...], 1, 1)
        for i in range(1, k):
            acc = acc + _dg(refs[i][...], refs[k + i][...], 1, 1)
        refs[2 * k][...] = acc.astype(out_dtype)

    in_specs = [_col_spec(v, tm, v[2], lambda i, j: (i, 0)) for v in views]
    in_specs += [pl.BlockSpec((tk, w.shape[1]), lambda i, j: (j, 0)) for w in ws]
    out_spec = pl.BlockSpec((tm, tk), lambda i, j: (i, j))
    out_shape = jax.ShapeDtypeStruct((t, kk), out_dtype)
    if ride is not None:
        (res,), landed = hosted_call(body, name=name, grid=(t // tm, kk // tk), in_specs=in_specs, out_specs=[out_spec],
                                     out_shape=[out_shape], args=(*[v[0] for v in views], *ws), ride=ride)
        return res, landed
    return pl.pallas_call(
        body, name=name, grid=(t // tm, kk // tk), in_specs=in_specs, out_specs=out_spec, out_shape=out_shape,
        compiler_params=_cp("parallel", "parallel"),
    )(*[v[0] for v in views], *ws)


def mm_tn(name, x, dy, tk, tn, tt, out_dtype=F32, ride=None):
    xv, dv = _view(x), _view(dy)
    t, kk, n = xv[0].shape[0], xv[2], dv[2]
    steps = t // tt

    def body(x_ref, d_ref, o_ref, acc_ref):
        @pl.when(pl.program_id(2) == 0)
        def _():
            acc_ref[...] = jnp.zeros_like(acc_ref)

        acc_ref[...] += _dg(x_ref[...], d_ref[...], 0, 0)

        @pl.when(pl.program_id(2) == steps - 1)
        def _():
            o_ref[...] = acc_ref[...].astype(out_dtype)

    in_specs = [_col_spec(xv, tt, tk, lambda a, b, c: (c, a)), _col_spec(dv, tt, tn, lambda a, b, c: (c, b))]
    out_spec = pl.BlockSpec((tk, tn), lambda a, b, c: (a, b))
    out_shape = jax.ShapeDtypeStruct((kk, n), out_dtype)
    if ride is not None:
        (res,), landed = hosted_call(body, name=name, grid=(kk // tk, n // tn, steps), in_specs=in_specs,
                                     out_specs=[out_spec], out_shape=[out_shape], scratch=[pltpu.VMEM((tk, tn), F32)],
                                     args=(xv[0], dv[0]), ride=ride)
        return res, landed
    return pl.pallas_call(
        body, name=name, grid=(kk // tk, n // tn, steps), in_specs=in_specs, out_specs=out_spec, out_shape=out_shape,
        scratch_shapes=[pltpu.VMEM((tk, tn), F32)],
        compiler_params=_cp("parallel", "parallel", "arbitrary"),
    )(xv[0], dv[0])


def _silu(x):
    return x * jax.nn.sigmoid(x)


def f_modulate(x, sc, sh):
    return (x * (1.0 + sc) + sh,)


def _res_ln(coef):
    def f(x, y, g, lg, lb):
        r = ALPHA * x + (coef * g) * y
        mu = jnp.mean(r, axis=-1, keepdims=True)
        d = r - mu
        var = jnp.mean(d * d, axis=-1, keepdims=True)
        return (d * lax.rsqrt(var + LN_EPS) * lg + lb,)
    return f


def f_glu(y, w, b):
    g = jax.nn.gelu(y)
    return (g * jax.nn.sigmoid(bdot_nn(g, w) + b),)


def _shift_down(x, halo, k):
    if k == 0:
        return x
    r = pltpu.roll(x, k, 0)
    hr = pltpu.roll(halo, k, 0)
    row = lax.broadcasted_iota(jnp.int32, (HALO, 1), 0)
    top = jnp.where(row < k, hr, r[:HALO])
    return jnp.concatenate([top, r[HALO:]], axis=0)


def _shift_up(x, halo, k):
    if k == 0:
        return x
    n = x.shape[0]
    r = pltpu.roll(x, n - k, 0)
    hr = pltpu.roll(halo, HALO - k, 0)
    row = lax.broadcasted_iota(jnp.int32, (HALO, 1), 0)
    bot = jnp.where(row >= HALO - k, hr, r[n - HALO:])
    return jnp.concatenate([r[:n - HALO], bot], axis=0)


def _conv_pre(x, halo, w, b):
    acc = x * w[CONV_K - 1:CONV_K, :] + b
    for k in range(1, CONV_K):
        acc = acc + _shift_down(x, halo, k) * w[CONV_K - 1 - k:CONV_K - k, :]
    return acc


def _rows_before(width, tm):
    return pl.BlockSpec((HALO, width), lambda i: (jnp.maximum(i * (tm // HALO) - 1, 0), 0))


def conv_fwd(proj, w, b, seq, tm):
    t = proj.shape[0]
    tps = seq // tm

    def body(x_ref, h_ref, w_ref, b_ref, o_ref):
        first = (pl.program_id(0) % tps == 0)
        halo = jnp.where(first, 0.0, h_ref[...])
        o_ref[...] = _silu(_conv_pre(x_ref[...], halo, w_ref[...], b_ref[...]))

    return pl.pallas_call(
        body, name="conv_fwd", grid=(t // tm,),
        in_specs=[pl.BlockSpec((tm, CONV_CH), lambda i: (i, 0)), _rows_before(CONV_CH, tm),
                  pl.BlockSpec((CONV_K, CONV_CH), lambda i: (0, 0)), pl.BlockSpec((1, CONV_CH), lambda i: (0, 0))],
        out_specs=pl.BlockSpec((tm, CONV_CH), lambda i: (i, 0)),
        out_shape=jax.ShapeDtypeStruct((t, CONV_CH), F32),
        compiler_params=_cp("arbitrary"),
    )(proj, proj, w, b)


def conv_bwd_pre(proj, w, b, dxs, dbm, dcm, seq, tm):
    t = proj.shape[0]
    tps = seq // tm

    def body(x_ref, h_ref, w_ref, b_ref, d1, d2, d3, dp_ref, dw_ref, db_ref):
        i = pl.program_id(0)
        halo = jnp.where(i % tps == 0, 0.0, h_ref[...])
        x = x_ref[...]
        pre = _conv_pre(x, halo, w_ref[...], b_ref[...])
        sg = jax.nn.sigmoid(pre)
        dout = jnp.concatenate([d1[...], d2[...], d3[...]], axis=1)
        dp = dout * (sg * (1.0 + pre * (1.0 - sg)))
        dp_ref[...] = dp

        @pl.when(i == 0)
        def _():
            dw_ref[...] = jnp.zeros_like(dw_ref)
            db_ref[...] = jnp.zeros_like(db_ref)

        db_ref[...] += jnp.sum(dp, axis=0, keepdims=True)
        for k in range(CONV_K):
            j = CONV_K - 1 - k
            dw_ref[j:j + 1, :] += jnp.sum(dp * _shift_down(x, halo, k), axis=0, keepdims=True)

    return pl.pallas_call(
        body, name="conv_bwd_pre", grid=(t // tm,),
        in_specs=[pl.BlockSpec((tm, CONV_CH), lambda i: (i, 0)), _rows_before(CONV_CH, tm),
                  pl.BlockSpec((CONV_K, CONV_CH), lambda i: (0, 0)), pl.BlockSpec((1, CONV_CH), lambda i: (0, 0)),
                  pl.BlockSpec((tm, 512), lambda i: (i, 0)), pl.BlockSpec((tm, 256), lambda i: (i, 0)),
                  pl.BlockSpec((tm, 256), lambda i: (i, 0))],
        out_specs=[pl.BlockSpec((tm, CONV_CH), lambda i: (i, 0)), pl.BlockSpec((CONV_K, CONV_CH), lambda i: (0, 0)),
                   pl.BlockSpec((1, CONV_CH), lambda i: (0, 0))],
        out_shape=[jax.ShapeDtypeStruct((t, CONV_CH), F32), jax.ShapeDtypeStruct((CONV_K, CONV_CH), F32),
                   jax.ShapeDtypeStruct((1, CONV_CH), F32)],
        compiler_params=_cp("arbitrary"),
    )(proj, proj, w, b, dxs, dbm, dcm)


def conv_bwd_x(dpre, w, seq, tm):
    t = dpre.shape[0]
    tps = seq // tm
    blocks = tm // HALO
    last = t // HALO - 1

    def body(d_ref, h_ref, w_ref, o_ref):
        halo = jnp.where(pl.program_id(0) % tps == tps - 1, 0.0, h_ref[...])
        d = d_ref[...]
        w = w_ref[...]
        acc = d * w[CONV_K - 1:CONV_K, :]
        for k in range(1, CONV_K):
            acc = acc + _shift_up(d, halo, k) * w[CONV_K - 1 - k:CONV_K - k, :]
        o_ref[...] = acc

    return pl.pallas_call(
        body, name="conv_bwd_x", grid=(t // tm,),
        in_specs=[pl.BlockSpec((tm, CONV_CH), lambda i: (i, 0)),
                  pl.BlockSpec((HALO, CONV_CH), lambda i: (jnp.minimum((i + 1) * blocks, last), 0)),
                  pl.BlockSpec((CONV_K, CONV_CH), lambda i: (0, 0))],
        out_specs=pl.BlockSpec((tm, CONV_CH), lambda i: (i, 0)),
        out_shape=jax.ShapeDtypeStruct((t, CONV_CH), F32),
        compiler_params=_cp("arbitrary"),
    )(dpre, dpre, w)


def _softplus(x):
    return jnp.maximum(x, 0.0) + jnp.log1p(jnp.exp(-jnp.abs(x)))


def _ssd_chunk(xs, bg, cg, dtr, zz, hp, dtb, alog, dcol, nw):
    l = xs.shape[0]
    row = lax.broadcasted_iota(jnp.int32, (l, l), 0)
    col = lax.broadcasted_iota(jnp.int32, (l, l), 1)
    causal = row >= col
    tril = causal.astype(F32)
    expand = (lax.broadcasted_iota(jnp.int32, (LANE, GROUP_COLS), 1) // SSD_HEAD_DIM
              == lax.broadcasted_iota(jnp.int32, (LANE, GROUP_COLS), 0)).astype(F32)
    head_of_col = lax.broadcasted_iota(jnp.int32, (1, GROUP_COLS), 1) // SSD_HEAD_DIM
    last_row = (lax.broadcasted_iota(jnp.int32, (l, 1), 0) == l - 1).astype(F32)

    dtc = _softplus(dtr + dtb)
    a_c = dtc * (-jnp.exp(alog))
    acs_c = jnp.dot(tril, a_c, precision=HI, preferred_element_type=F32)
    dt_e = jnp.dot(dtc, expand, precision=HI, preferred_element_type=F32)
    acs_e = jnp.dot(acs_c, expand, precision=HI, preferred_element_type=F32)
    alast_e = jnp.sum(acs_e * last_row, axis=0, keepdims=True)
    x = xs * dt_e
    states = bdot_tn(bg, x * jnp.exp(alast_e - acs_e))
    h_next = jnp.exp(alast_e) * hp + states
    d_e = jnp.sum(dcol * expand, axis=0, keepdims=True)
    y = bdot_nn(cg, hp) * jnp.exp(acs_e) + d_e * xs
    cb = bdot_nt(cg, bg)
    acs_t = acs_c.T
    for z in range(HEADS_PER_GROUP):
        seg = _take_col(z)(acs_c) - _take_row(z)(acs_t)
        lmat = jnp.exp(jnp.where(causal, seg, -1e30))
        y = y + bdot_nn(cb * lmat, x * (head_of_col == z).astype(F32))
    yz = y * _silu(zz)
    ms = jnp.mean(yz * yz, axis=-1, keepdims=True)
    return yz * lax.rsqrt(ms + LN_EPS) * nw, h_next


def _ssd_in_specs(nc, rev):
    def tok(b, c):
        return b * nc + (nc - 1 - c if rev else c)

    whole = lambda *shape: pl.BlockSpec(shape, lambda b, c: (0,) * len(shape))
    both = SSD_GROUPS * SSD_STATE
    return [
        pl.BlockSpec((SSD_CHUNK, SSD_WIDTH), lambda b, c: (tok(b, c), 0)),
        pl.BlockSpec((SSD_CHUNK, both), lambda b, c: (tok(b, c), SSD_WIDTH // both)),
        pl.BlockSpec((SSD_CHUNK, both), lambda b, c: (tok(b, c), SSD_WIDTH // both + 1)),
        pl.BlockSpec((SSD_CHUNK, SSD_GROUPS * LANE), lambda b, c: (tok(b, c), P_DT // (SSD_GROUPS * LANE))),
        pl.BlockSpec((SSD_CHUNK, SSD_WIDTH), lambda b, c: (tok(b, c), P_Z // SSD_WIDTH)),
        whole(SSD_GROUPS, 1, LANE), whole(SSD_GROUPS, 1, LANE), whole(SSD_GROUPS, LANE, 1),
        whole(SSD_GROUPS, 1, GROUP_COLS),
    ], tok


def _group(ref, g, width):
    return ref[:, g * width:(g + 1) * width]


def ssd_fwd(xc, proj, dtb, alog, dcol, nw, bsz, seq):
    t = xc.shape[0]
    nc = seq // SSD_CHUNK
    in_specs, tok = _ssd_in_specs(nc, False)

    def body(xs, bm, cm, dtr, zz, dtb_r, alog_r, dcol_r, nw_r, y_ref, hp_ref, h_scr):
        @pl.when(pl.program_id(1) == 0)
        def _():
            h_scr[...] = jnp.zeros_like(h_scr)

        for g in range(SSD_GROUPS):
            hp = h_scr[g]
            hp_ref[g, 0, 0] = hp
            y, hn = _ssd_chunk(_group(xs, g, GROUP_COLS), _group(bm, g, SSD_STATE), _group(cm, g, SSD_STATE),
                               _group(dtr, g, LANE), _group(zz, g, GROUP_COLS), hp,
                               dtb_r[g], alog_r[g], dcol_r[g], nw_r[g])
            y_ref[:, g * GROUP_COLS:(g + 1) * GROUP_COLS] = y
            h_scr[g] = hn

    return pl.pallas_call(
        body, name="ssd_fwd", grid=(bsz, nc), in_specs=in_specs,
        out_specs=[pl.BlockSpec((SSD_CHUNK, SSD_WIDTH), lambda b, c: (tok(b, c), 0)),
                   pl.BlockSpec((SSD_GROUPS, 1, 1, SSD_STATE, GROUP_COLS), lambda b, c: (0, b, c, 0, 0))],
        out_shape=[jax.ShapeDtypeStruct((t, SSD_WIDTH), F32),
                   jax.ShapeDtypeStruct((SSD_GROUPS, bsz, nc, SSD_STATE, GROUP_COLS), F32)],
        scratch_shapes=[pltpu.VMEM((SSD_GROUPS, SSD_STATE, GROUP_COLS), F32)],
        compiler_params=_cp("arbitrary", "arbitrary"),
    )(xc, xc, xc, proj, proj, dtb, alog, dcol, nw)


def ssd_bwd(xc, proj, dtb, alog, dcol, nw, hprev, dy, bsz, seq):
    t = xc.shape[0]
    nc = seq // SSD_CHUNK
    in_specs, tok = _ssd_in_specs(nc, True)
    in_specs += [pl.BlockSpec((SSD_GROUPS, 1, 1, SSD_STATE, GROUP_COLS), lambda b, c: (0, b, nc - 1 - c, 0, 0)),
                 pl.BlockSpec((SSD_CHUNK, SSD_WIDTH), lambda b, c: (tok(b, c), 0))]

    def body(xs, bm, cm, dtr, zz, dtb_r, alog_r, dcol_r, nw_r, hp_ref, dy_ref,
             dxs, dbm, dcm, ddt, dzz, ddtb, dalog, ddcol, dnw, dh_scr):
        b, c = pl.program_id(0), pl.program_id(1)

        @pl.when(c == 0)
        def _():
            dh_scr[...] = jnp.zeros_like(dh_scr)

        @pl.when((b == 0) & (c == 0))
        def _():
            for r in (ddtb, dalog, ddcol, dnw):
                r[...] = jnp.zeros_like(r)

        for g in range(SSD_GROUPS):
            wide = slice(g * GROUP_COLS, (g + 1) * GROUP_COLS)
            state = slice(g * SSD_STATE, (g + 1) * SSD_STATE)
            _, pull = jax.vjp(_ssd_chunk, xs[:, wide], bm[:, state], cm[:, state], _group(dtr, g, LANE), zz[:, wide],
                              hp_ref[g, 0, 0], dtb_r[g], alog_r[g], dcol_r[g], nw_r[g])
            d = pull((dy_ref[:, wide], dh_scr[g]))
            dxs[:, wide], dbm[:, state], dcm[:, state], dzz[:, wide] = d[0], d[1], d[2], d[4]
            ddt[:, g * LANE:(g + 1) * LANE] = d[3]
            dh_scr[g] = d[5]
            ddtb[g] += d[6]
            dalog[g] += d[7]
            ddcol[g] += d[8]
            dnw[g] += d[9]

    def tile(w):
        return pl.BlockSpec((SSD_CHUNK, w), lambda b, c: (tok(b, c), 0))

    whole = lambda *shape: pl.BlockSpec(shape, lambda b, c: (0,) * len(shape))
    return pl.pallas_call(
        body, name="ssd_bwd", grid=(bsz, nc), in_specs=in_specs,
        out_specs=[tile(SSD_WIDTH), tile(2 * SSD_STATE), tile(2 * SSD_STATE), tile(2 * LANE), tile(SSD_WIDTH),
                   whole(SSD_GROUPS, 1, LANE), whole(SSD_GROUPS, 1, LANE), whole(SSD_GROUPS, LANE, 1),
                   whole(SSD_GROUPS, 1, GROUP_COLS)],
        out_shape=[jax.ShapeDtypeStruct((t, SSD_WIDTH), F32), jax.ShapeDtypeStruct((t, 2 * SSD_STATE), F32),
                   jax.ShapeDtypeStruct((t, 2 * SSD_STATE), F32), jax.ShapeDtypeStruct((t, 2 * LANE), F32),
                   jax.ShapeDtypeStruct((t, SSD_WIDTH), F32),
                   jax.ShapeDtypeStruct((SSD_GROUPS, 1, LANE), F32), jax.ShapeDtypeStruct((SSD_GROUPS, 1, LANE), F32),
                   jax.ShapeDtypeStruct((SSD_GROUPS, LANE, 1), F32),
                   jax.ShapeDtypeStruct((SSD_GROUPS, 1, GROUP_COLS), F32)],
        scratch_shapes=[pltpu.VMEM((SSD_GROUPS, SSD_STATE, GROUP_COLS), F32)],
        compiler_params=_cp("arbitrary", "arbitrary"),
    )(xc, xc, xc, proj, proj, dtb, alog, dcol, nw, hprev, dy)


def _disc_a(a_re, a_im, log_dt):
    dt = jnp.exp(log_dt)
    mag = jnp.exp(dt * a_re)
    ab_re, ab_im = mag * jnp.cos(dt * a_im), mag * jnp.sin(dt * a_im)
    den = a_re * a_re + a_im * a_im
    nr, ni = ab_re - 1.0, ab_im
    f_re, f_im = (nr * a_re + ni * a_im) / den, (ni * a_re - nr * a_im) / den
    return ab_re, ab_im, f_re, f_im


def _disc_b(f_re, f_im, b_re, b_im):
    return f_re * b_re - f_im * b_im, f_re * b_im + f_im * b_re


def _whole(f, name, args, outs):
    def body(*refs):
        res = f(*[r[...] for r in refs[:len(args)]])
        for o, v in zip(refs[len(args):], res):
            o[...] = v

    return pl.pallas_call(body, name=name, out_shape=[jax.ShapeDtypeStruct(s, F32) for s in outs])(*args)


def _whole_vjp(f, name, args, cts):
    def body(*refs):
        vals = [r[...] for r in refs[:len(args)]]
        _, pull = jax.vjp(f, *vals)
        res = pull(tuple(r[...] for r in refs[len(args):len(args) + len(cts)]))
        for o, v in zip(refs[len(args) + len(cts):], res):
            o[...] = v

    return pl.pallas_call(body, name=name, out_shape=[jax.ShapeDtypeStruct(a.shape, F32) for a in args])(*args, *cts)


S5_SUB = 8
S5_STEPS = 3


def s5_tables(lam_re, lam_im):
    rows = S5_STEPS * S5_SUB

    def body(lr_ref, li_ref, sf_re, sf_im, sb_re, sb_im, cf_re, cf_im, cb_re, cb_im):
        lr, li = lr_ref[...], li_ref[...]

        def power(k):
            m = jnp.exp(k * lr)
            return m * jnp.cos(k * li), m * jnp.sin(k * li)

        srow = lax.broadcasted_iota(jnp.int32, (rows, 1), 0)
        k = jnp.left_shift(1, srow // S5_SUB)
        tt = srow % S5_SUB
        pr, pi = power(k.astype(F32))
        fwd, bwd = tt >= k, tt < S5_SUB - k
        sf_re[...], sf_im[...] = jnp.where(fwd, pr, 0.0), jnp.where(fwd, pi, 0.0)
        sb_re[...], sb_im[...] = jnp.where(bwd, pr, 0.0), jnp.where(bwd, pi, 0.0)
        trow = lax.broadcasted_iota(jnp.int32, (S5_SUB, 1), 0)
        cf_re[...], cf_im[...] = power((trow + 1).astype(F32))
        cb_re[...], cb_im[...] = power((S5_SUB - trow).astype(F32))

    shp = [jax.ShapeDtypeStruct((rows, S5_COLS), F32)] * 4 + [jax.ShapeDtypeStruct((S5_SUB, S5_COLS), F32)] * 4
    return pl.pallas_call(body, name="s5_tables", out_shape=shp)(lam_re, lam_im)


def _s5_coefs(steps_re, steps_im, carry_re, carry_im, reverse):
    sign = -1.0 if reverse else 1.0
    steps = [(steps_re[s * S5_SUB:(s + 1) * S5_SUB, :], sign * steps_im[s * S5_SUB:(s + 1) * S5_SUB, :])
             for s in range(S5_STEPS)]
    return steps, (carry_re[...], sign * carry_im[...])


def _s5_block_scan(ar, ai, coefs, cr, ci, reverse):
    steps, (qr, qi) = coefs
    for s, (pr, pi) in enumerate(steps):
        shift = S5_SUB - (1 << s) if reverse else (1 << s)
        sr, si = pltpu.roll(ar, shift, 0), pltpu.roll(ai, shift, 0)
        ar, ai = ar + pr * sr - pi * si, ai + pr * si + pi * sr
    br, bi = jnp.broadcast_to(cr, ar.shape), jnp.broadcast_to(ci, ai.shape)
    return ar + qr * br - qi * bi, ai + qr * bi + qi * br


def _s5_specs(n5, rev):
    def tok(q, b, c):
        return b * n5 + (n5 - 1 - c if rev else c)

    qcols = S5_COLS // S5_Q
    specs = [
        pl.BlockSpec((S5_CHUNK, LANE), lambda q, b, c: (tok(q, b, c), P_U // LANE + q)),
        pl.BlockSpec((1, LANE, qcols), lambda q, b, c: (q, 0, 0)),
        pl.BlockSpec((1, LANE, qcols), lambda q, b, c: (q, 0, 0)),
        pl.BlockSpec((1, qcols, LANE), lambda q, b, c: (q, 0, 0)),
        pl.BlockSpec((1, qcols, LANE), lambda q, b, c: (q, 0, 0)),
        pl.BlockSpec((S5_STEPS * S5_SUB, qcols), lambda q, b, c: (0, q)),
        pl.BlockSpec((S5_STEPS * S5_SUB, qcols), lambda q, b, c: (0, q)),
        pl.BlockSpec((S5_SUB, qcols), lambda q, b, c: (0, q)),
        pl.BlockSpec((S5_SUB, qcols), lambda q, b, c: (0, q)),
        pl.BlockSpec((1, 1, LANE), lambda q, b, c: (q, 0, 0)),
    ]
    return specs, tok, qcols


def s5_fwd(proj, wb_re, wb_im, wc_re, wc_im, sf_re, sf_im, cf_re, cf_im, dvec, bsz, seq, ride=None):
    t = proj.shape[0]
    n5 = seq // S5_CHUNK
    in_specs, tok, qcols = _s5_specs(n5, False)

    def body(u_ref, wbr, wbi, wcr, wci, sfr, sfi, cfr, cfi, d_ref, y_ref, xr_ref, xi_ref, cr_scr, ci_scr):
        @pl.when(pl.program_id(2) == 0)
        def _():
            cr_scr[...] = jnp.zeros_like(cr_scr)
            ci_scr[...] = jnp.zeros_like(ci_scr)

        u = u_ref[...]
        bur, bui = _dg(u, wbr[0], 1, 0), _dg(u, wbi[0], 1, 0)
        coefs = _s5_coefs(sfr, sfi, cfr, cfi, False)
        cr, ci = cr_scr[...], ci_scr[...]
        for r in range(S5_CHUNK // S5_SUB):
            rows = slice(r * S5_SUB, (r + 1) * S5_SUB)
            xr, xi = _s5_block_scan(bur[rows], bui[rows], coefs, cr, ci, False)
            xr_ref[rows, :], xi_ref[rows, :] = xr, xi
            cr, ci = xr[S5_SUB - 1:, :], xi[S5_SUB - 1:, :]
        cr_scr[...], ci_scr[...] = cr, ci
        y_ref[...] = _dg(xr_ref[...], wcr[0], 1, 0) - _dg(xi_ref[...], wci[0], 1, 0) + u * d_ref[0]

    def tile(w):
        return pl.BlockSpec((S5_CHUNK, w), lambda q, b, c: (tok(q, b, c), q))

    return hosted_call(
        body, name="s5_fwd", grid=(S5_Q, bsz, n5), in_specs=in_specs,
        out_specs=[tile(LANE), tile(qcols), tile(qcols)],
        out_shape=[jax.ShapeDtypeStruct((t, S5_WIDTH), F32), jax.ShapeDtypeStruct((t, S5_COLS), F32),
                   jax.ShapeDtypeStruct((t, S5_COLS), F32)],
        scratch=[pltpu.VMEM((1, qcols), F32)] * 2,
        args=(proj, wb_re, wb_im, wc_re, wc_im, sf_re, sf_im, cf_re, cf_im, dvec), ride=ride)


def s5_bwd(proj, wb_re, wb_im, wc_re, wc_im, sb_re, sb_im, cb_re, cb_im, dvec, xr_all, xi_all, dy, bsz, seq,
           ride=None):
    t = proj.shape[0]
    n5 = seq // S5_CHUNK
    in_specs, tok, qcols = _s5_specs(n5, True)
    blocks = S5_CHUNK // HALO

    def prev_rows(q, b, c):
        return (jnp.maximum(tok(q, b, c) * blocks - 1, 0), q)

    in_specs += [pl.BlockSpec((S5_CHUNK, qcols), lambda q, b, c: (tok(q, b, c), q)),
                 pl.BlockSpec((S5_CHUNK, qcols), lambda q, b, c: (tok(q, b, c), q)),
                 pl.BlockSpec((HALO, qcols), prev_rows), pl.BlockSpec((HALO, qcols), prev_rows),
                 pl.BlockSpec((S5_CHUNK, LANE), lambda q, b, c: (tok(q, b, c), q))]

    def body(u_ref, wbr, wbi, wcr, wci, sbr, sbi, cbr, cbi, d_ref, xr_ref, xi_ref, pr_ref, pi_ref, dy_ref,
             du_ref, dwbr, dwbi, dwcr, dwci, dar, dai, dd_ref, gr_scr, gi_scr, gr_all, gi_all):
        b, c = pl.program_id(1), pl.program_id(2)

        @pl.when(c == 0)
        def _():
            gr_scr[...] = jnp.zeros_like(gr_scr)
            gi_scr[...] = jnp.zeros_like(gi_scr)

        @pl.when((b == 0) & (c == 0))
        def _():
            for r in (dwbr, dwbi, dwcr, dwci, dar, dai, dd_ref):
                r[...] = jnp.zeros_like(r)

        u, dy_v = u_ref[...], dy_ref[...]
        g0r, g0i = _dg(dy_v, wcr[0], 1, 1), -_dg(dy_v, wci[0], 1, 1)
        coefs = _s5_coefs(sbr, sbi, cbr, cbi, True)
        cr, ci = gr_scr[...], gi_scr[...]
        for r in reversed(range(S5_CHUNK // S5_SUB)):
            rows = slice(r * S5_SUB, (r + 1) * S5_SUB)
            br, bi = _s5_block_scan(g0r[rows], g0i[rows], coefs, cr, ci, True)
            gr_all[rows, :], gi_all[rows, :] = br, bi
            cr, ci = br[:1, :], bi[:1, :]
        gr_scr[...], gi_scr[...] = cr, ci
        gr, gi = gr_all[...], gi_all[...]

        row = lax.broadcasted_iota(jnp.int32, (S5_CHUNK, 1), 0)
        xr, xi = xr_ref[...], xi_ref[...]
        is_first = (c == n5 - 1)
        hr = jnp.where(is_first, 0.0, pr_ref[...][HALO - 1:, :])
        hi = jnp.where(is_first, 0.0, pi_ref[...][HALO - 1:, :])
        xpr = jnp.where(row >= 1, pltpu.roll(xr, 1, 0), hr)
        xpi = jnp.where(row >= 1, pltpu.roll(xi, 1, 0), hi)
        dar[0] += jnp.sum(xpr * gr + xpi * gi, axis=0, keepdims=True)
        dai[0] += jnp.sum(xpr * gi - xpi * gr, axis=0, keepdims=True)
        du_ref[...] = _dg(gr, wbr[0], 1, 1) + _dg(gi, wbi[0], 1, 1) + dy_v * d_ref[0]
        dwbr[0] += _dg(u, gr, 0, 0)
        dwbi[0] += _dg(u, gi, 0, 0)
        dwcr[0] += _dg(xr, dy_v, 0, 0)
        dwci[0] -= _dg(xi, dy_v, 0, 0)
        dd_ref[0] += jnp.sum(dy_v * u, axis=0, keepdims=True)

    def acc(shape):
        return pl.BlockSpec((1,) + shape, lambda q, b, c: (q, 0, 0))

    return hosted_call(
        body, name="s5_bwd", grid=(S5_Q, bsz, n5), in_specs=in_specs,
        out_specs=[pl.BlockSpec((S5_CHUNK, LANE), lambda q, b, c: (tok(q, b, c), q)),
                   acc((LANE, qcols)), acc((LANE, qcols)), acc((qcols, LANE)), acc((qcols, LANE)),
                   acc((1, qcols)), acc((1, qcols)), acc((1, LANE))],
        out_shape=[jax.ShapeDtypeStruct((t, S5_WIDTH), F32),
                   jax.ShapeDtypeStruct((S5_Q, LANE, qcols), F32), jax.ShapeDtypeStruct((S5_Q, LANE, qcols), F32),
                   jax.ShapeDtypeStruct((S5_Q, qcols, LANE), F32), jax.ShapeDtypeStruct((S5_Q, qcols, LANE), F32),
                   jax.ShapeDtypeStruct((S5_Q, 1, qcols), F32), jax.ShapeDtypeStruct((S5_Q, 1, qcols), F32),
                   jax.ShapeDtypeStruct((S5_Q, 1, LANE), F32)],
        scratch=[pltpu.VMEM((1, qcols), F32)] * 2 + [pltpu.VMEM((S5_CHUNK, qcols), F32)] * 2,
        args=(proj, wb_re, wb_im, wc_re, wc_im, sb_re, sb_im, cb_re, cb_im, dvec, xr_all, xi_all, xr_all, xi_all, dy),
        ride=ride)


def _blockdiag_b(bb):
    b4 = bb.reshape(S5_Q, 8, S5_STATE, S5_GROUP_CH)
    eye = jnp.eye(8, dtype=bb.dtype)
    w = jnp.einsum("qgph,gk->qghkp", b4, eye)
    return w.reshape(S5_Q, LANE, S5_COLS // S5_Q)


def _unblock_b(dw):
    d = dw.reshape(S5_Q, 8, S5_GROUP_CH, 8, S5_STATE)
    d = jnp.einsum("qghgp->qgph", d)
    return d.reshape(S5_COLS, S5_GROUP_CH)


def _blockdiag_c(cc):
    c4 = cc.reshape(S5_Q, 8, S5_GROUP_CH, S5_STATE)
    eye = jnp.eye(8, dtype=cc.dtype)
    w = jnp.einsum("qghp,gk->qgpkh", c4, eye)
    return w.reshape(S5_Q, S5_COLS // S5_Q, LANE)


def _unblock_c(dw):
    d = dw.reshape(S5_Q, 8, S5_STATE, 8, S5_GROUP_CH)
    d = jnp.einsum("qgpgh->qghp", d)
    return d.reshape(S5_GROUPS, S5_GROUP_CH, S5_STATE)


def ada_fwd(c_all, w_loc, b_loc):
    def body(c_ref, w_ref, b_ref, o_ref):
        o_ref[...] = _dg(_silu(c_ref[...]), w_ref[...], 1, 0) + b_ref[...]

    return pl.pallas_call(body, name="ada_fwd",
                          out_shape=jax.ShapeDtypeStruct((c_all.shape[0], w_loc.shape[1]), F32),
                          compiler_params=_cp())(c_all, w_loc, b_loc)


def ada_bwd(c_all, dmod_all, dmod_cols):
    def body(c_ref, da_ref, dc_ref, gb_ref, gw_ref):
        gb_ref[...] = jnp.sum(da_ref[...], axis=0, keepdims=True)
        gw_ref[...] = _dg(_silu(c_ref[...]), dc_ref[...], 0, 0)

    return pl.pallas_call(body, name="ada_bwd",
                          out_shape=[jax.ShapeDtypeStruct((1, dmod_all.shape[1]), F32),
                                     jax.ShapeDtypeStruct((c_all.shape[1], dmod_cols.shape[1]), F32)],
                          compiler_params=_cp())(c_all, dmod_all, dmod_cols)


_FLIPS = [(0, 0, 1), (1, 0, 0), (0, 1, 0), (1, 1, 0), (1, 0, 1), (0, 1, 1), (1, 1, 1)]


def _exchange_ops(srcs, outs, sems, gather):
    n = len(srcs)
    send_sems, recv_sems, loc_sems = sems
    x, y, c = lax.axis_index("x"), lax.axis_index("y"), lax.axis_index("c")
    me = 4 * x + 2 * y + c
    peers = []
    for fx, fy, fc in _FLIPS:
        px, py, pc = (1 - x if fx else x), (1 - y if fy else y), (1 - c if fc else c)
        peers.append(((px, py, pc), 4 * px + 2 * py + pc))

    def copy(k, j, slot_src, slot_dst):
        src = srcs[k] if gather[k] else srcs[k].at[slot_src]
        return pltpu.make_async_remote_copy(src_ref=src, dst_ref=outs[k].at[slot_dst],
                                            send_sem=send_sems.at[k, j], recv_sem=recv_sems.at[k, j],
                                            device_id=peers[j][0], device_id_type=MESH)

    def local(k):
        own = srcs[k] if gather[k] else srcs[k].at[me]
        return pltpu.make_async_copy(own, outs[k].at[me], loc_sems.at[k])

    def start():
        for k in range(n):
            for j in range(N_DEV - 1):
                copy(k, j, peers[j][1], me).start()
            local(k).start()

    def wait():
        for k in range(n):
            for j in range(N_DEV - 1):
                copy(k, j, me, peers[j][1]).wait_recv()
        for k in range(n):
            for j in range(N_DEV - 1):
                copy(k, j, peers[j][1], me).wait_send()
            local(k).wait()

    return start, wait


def _gather_two_level(srcs, outs, sems):
    n = len(srcs)
    send_sems, recv_sems, loc_sems = sems
    x, y, c = lax.axis_index("x"), lax.axis_index("y"), lax.axis_index("c")
    slot = lambda px, py, pc: 4 * px + 2 * py + pc
    me, sibling = (x, y, c), (x, y, 1 - c)
    chips = [(1 - x, y), (x, 1 - y), (1 - x, 1 - y)]

    def copy(k, j, block, to, own=False):
        return pltpu.make_async_remote_copy(src_ref=srcs[k] if own else outs[k].at[slot(*block)],
                                            dst_ref=outs[k].at[slot(*block)],
                                            send_sem=send_sems.at[k, j], recv_sem=recv_sems.at[k, j],
                                            device_id=to, device_id_type=MESH)

    locs = [pltpu.make_async_copy(srcs[k], outs[k].at[slot(*me)], loc_sems.at[k]) for k in range(n)]
    for k in range(n):
        locs[k].start()
        copy(k, 0, me, sibling, own=True).start()
        for j, chip in enumerate(chips):
            copy(k, 1 + j, me, (*chip, c), own=True).start()
    for j, chip in enumerate(chips):
        for k in range(n):
            copy(k, 1 + j, (*chip, c), me).wait_recv()
            copy(k, 4 + j, (*chip, c), sibling).start()
    for k in range(n):
        copy(k, 0, sibling, me).wait_recv()
        for j, chip in enumerate(chips):
            copy(k, 4 + j, (*chip, 1 - c), me).wait_recv()
    for k in range(n):
        copy(k, 0, me, sibling, own=True).wait_send()
        for j, chip in enumerate(chips):
            copy(k, 1 + j, me, (*chip, c), own=True).wait_send()
            copy(k, 4 + j, (*chip, c), sibling).wait_send()
        locs[k].wait()


def gather_two_level(name, arrs):
    n = len(arrs)
    specs, shapes, sems = _exchange_parts(arrs, [True] * n)

    def body(*refs):
        _gather_two_level(refs[:n], refs[n:2 * n], refs[2 * n:])

    return pl.pallas_call(
        body, name=name, in_specs=specs, out_specs=specs, out_shape=shapes, scratch_shapes=sems,
        compiler_params=pltpu.CompilerParams(has_side_effects=True),
    )(*arrs)


def _exchange_parts(arrs, gather):
    n = len(arrs)
    any_spec = pl.BlockSpec(memory_space=pl.ANY)
    shapes = [jax.ShapeDtypeStruct(((N_DEV,) + a.shape) if g else a.shape, a.dtype) for a, g in zip(arrs, gather)]
    sems = [pltpu.SemaphoreType.DMA((n, N_DEV - 1)), pltpu.SemaphoreType.DMA((n, N_DEV - 1)),
            pltpu.SemaphoreType.DMA((n,))]
    return [any_spec] * n, shapes, sems


def exchange(name, arrs, gather):
    n = len(arrs)
    specs, shapes, sems = _exchange_parts(arrs, gather)

    def body(*refs):
        start, wait = _exchange_ops(refs[:n], refs[n:2 * n], refs[2 * n:], gather)
        start()
        wait()

    return pl.pallas_call(
        body, name=name, in_specs=specs, out_specs=specs, out_shape=shapes, scratch_shapes=sems,
        compiler_params=pltpu.CompilerParams(has_side_effects=True),
    )(*arrs)


def hosted_call(body, *, name, grid, in_specs, out_specs, out_shape, args, scratch=(), ride=None):
    sem = ("arbitrary",) * len(grid)
    if ride is None:
        res = pl.pallas_call(body, name=name, grid=grid, in_specs=in_specs, out_specs=out_specs, out_shape=out_shape,
                             scratch_shapes=list(scratch), compiler_params=_cp(*sem))(*args)
        return list(res), []
    arrs, gather = ride
    n, n_in, n_out, n_scr = len(arrs), len(in_specs), len(out_specs), len(scratch)
    specs, shapes, sems = _exchange_parts(arrs, gather)

    def both(*refs):
        ins, srcs = refs[:n_in], refs[n_in:n_in + n]
        outs, landed = refs[n_in + n:n_in + n + n_out], refs[n_in + n + n_out:n_in + 2 * n + n_out]
        scr, ex_sems = refs[n_in + 2 * n + n_out:n_in + 2 * n + n_out + n_scr], refs[n_in + 2 * n + n_out + n_scr:]
        start, wait = _exchange_ops(srcs, landed, ex_sems, gather)
        first = functools.reduce(lambda a, b: a & b, [pl.program_id(d) == 0 for d in range(len(grid))])
        last = functools.reduce(lambda a, b: a & b, [pl.program_id(d) == grid[d] - 1 for d in range(len(grid))])
        pl.when(first)(start)
        body(*ins, *outs, *scr)
        pl.when(last)(wait)

    res = pl.pallas_call(
        both, name=name, grid=grid, in_specs=list(in_specs) + specs, out_specs=list(out_specs) + specs,
        out_shape=list(out_shape) + shapes, scratch_shapes=list(scratch) + sems, compiler_params=_cp(*sem),
    )(*args, *arrs)
    return list(res[:n_out]), list(res[n_out:])


def adamw(name, g, w, m, v, tr, sel=None):
    slots = g.ndim >= 3
    r, c = w.shape
    c1, c2 = 1.0 - ADAM_B1 ** ADAM_STEP, 1.0 - ADAM_B2 ** ADAM_STEP

    def body(g_ref, w_ref, m_ref, v_ref, go, do, mo, vo):
        if slots:
            gg = g_ref[0].astype(F32)
            for j in range(1, N_DEV):
                gg = gg + g_ref[j].astype(F32)
        else:
            gg = g_ref[...]
        mn = ADAM_B1 * m_ref[...] + (1.0 - ADAM_B1) * gg
        vn = ADAM_B2 * v_ref[...] + (1.0 - ADAM_B2) * (gg * gg)
        go[...], mo[...], vo[...] = gg, mn, vn
        do[...] = -ADAM_LR * ((mn / c1) / (jnp.sqrt(vn / c2) + ADAM_EPS) + ADAM_WD * w_ref[...])

    blk = pl.BlockSpec((tr, c), lambda i: (i, 0))
    if g.ndim == 4:
        gspec = pl.BlockSpec((N_DEV, None, tr, c), lambda i: (0, sel, i, 0))
    else:
        gspec = pl.BlockSpec((N_DEV, tr, c), lambda i: (0, i, 0)) if slots else blk
    return pl.pallas_call(
        body, name=name, grid=(r // tr,), in_specs=[gspec, blk, blk, blk], out_specs=[blk] * 4,
        out_shape=[jax.ShapeDtypeStruct((r, c), F32)] * 4, compiler_params=_cp("parallel"),
    )(g, w, m, v)


def _lane_rows(n):
    return -(-n // (8 * LANE)) * 8


def _pack(arrs):
    pieces = []
    for a in arrs:
        n = math.prod(a.shape)
        flat = a.reshape(-1).astype(F32)
        pieces.append(jnp.pad(flat, (0, _lane_rows(n) * LANE - n)).reshape(_lane_rows(n), LANE))
    return jnp.concatenate(pieces, axis=0)


def _unpack(buf, shapes):
    out, off = [], 0
    for s in shapes:
        n = math.prod(s)
        out.append(buf[off:off + _lane_rows(n)].reshape(-1)[:n].reshape(s))
        off += _lane_rows(n)
    return out


def _cols_to_full(g):
    return jnp.transpose(g, (1, 0, 2)).reshape(g.shape[1], N_DEV * g.shape[2])


def _full_to_cols(w):
    r, c = w.shape
    return jnp.transpose(w.reshape(r, N_DEV, c // N_DEV), (1, 0, 2))


FF_CHUNK = D_FF // 2
DW_TOKENS = 1024
FFN_TM = 256


def _resident(shape):
    return pl.BlockSpec(shape, lambda i: (0,) * len(shape), pipeline_mode=pl.Buffered(1))


def _ffn_fwd(tag, x, sc, sh, g, w1, w3, w2, lg, lb, seq, tm, ride=None, target=None):
    t = x.shape[0]
    tm = min(FFN_TM, tm)
    tps = seq // tm
    ln = _res_ln(0.5)
    head = target is not None

    def body(x_ref, sc_ref, sh_ref, g_ref, lg_ref, lb_ref, w1_ref, w3_ref, w2_ref, *rest):
        if head:
            t_ref, y_ref, h_ref, a_ref, b_ref, f_ref, l_ref = rest
        else:
            y_ref, h_ref, a_ref, b_ref, f_ref = rest
        xv = x_ref[...]
        h = (xv * (1.0 + sc_ref[0]) + sh_ref[0]).astype(BF16)
        h_ref[...] = h
        acc = jnp.zeros((tm, D_MODEL), F32)
        for j in range(D_FF // FF_CHUNK):
            sl = slice(j * FF_CHUNK, (j + 1) * FF_CHUNK)
            a = _dg(h, w1_ref[:, sl], 1, 0)
            b = _dg(h, w3_ref[:, sl], 1, 0)
            a_ref[:, sl] = a
            b_ref[:, sl] = b
            acc = acc + _dg(_silu(a) * b, w2_ref[sl, :], 1, 0)
        f_ref[...] = acc
        y = ln(xv, acc, g_ref[0], lg_ref[...], lb_ref[...])[0]
        if head:
            @pl.when(pl.program_id(0) == 0)
            def _():
                l_ref[...] = jnp.zeros_like(l_ref)

            e = y - t_ref[...]
            y_ref[...] = e * (1.0 / D_MODEL)
            l_ref[...] += 0.5 * jnp.sum(jnp.mean(e * e, axis=-1, keepdims=True), axis=0, keepdims=True)
        else:
            y_ref[...] = y

    row = lambda c: pl.BlockSpec((tm, c), lambda i: (i, 0))
    per_seq = pl.BlockSpec((1, 1, D_MODEL), lambda i: (i // tps, 0, 0))
    vec = pl.BlockSpec((1, D_MODEL), lambda i: (0, 0))
    res, landed = hosted_call(
        body, name=tag + "_fwd", grid=(t // tm,),
        in_specs=[row(D_MODEL), per_seq, per_seq, per_seq, vec, vec,
                  _resident((D_MODEL, D_FF)), _resident((D_MODEL, D_FF)), _resident((D_FF, D_MODEL))]
        + ([row(D_MODEL)] if head else []),
        out_specs=[row(D_MODEL), row(D_MODEL), row(D_FF), row(D_FF), row(D_MODEL)]
        + ([pl.BlockSpec((1, 1), lambda i: (0, 0))] if head else []),
        out_shape=[jax.ShapeDtypeStruct((t, D_MODEL), F32), jax.ShapeDtypeStruct((t, D_MODEL), BF16),
                   jax.ShapeDtypeStruct((t, D_FF), F32), jax.ShapeDtypeStruct((t, D_FF), F32),
                   jax.ShapeDtypeStruct((t, D_MODEL), F32)] + ([jax.ShapeDtypeStruct((1, 1), F32)] if head else []),
        args=(x, sc, sh, g, lg, lb, w1, w3, w2) + ((target,) if head else ()), ride=ride)
    first = (res[0], res[5][0, 0]) if head else res[0]
    return first, tuple(res[1:5]), landed


def _ffn_bwd(tag, dy, x, sc, sh, g, w1, w3, w2, lg, lb, res, seq, tm, ride=None, chain=None):
    h, a, b, f = res
    t = x.shape[0]
    tmk = min(FFN_TM, tm)
    tps = seq // tmk
    ln = _res_ln(0.5)

    def body(dy_ref, x_ref, f_ref, a_ref, b_ref, sc_ref, sh_ref, g_ref, lg_ref, lb_ref, w1_ref, w3_ref, w2_ref,
             dx_ref, da_ref, db_ref, s_ref, df_ref, dsc_ref, dsh_ref, dg_ref, dlg_ref, dlb_ref):
        i = pl.program_id(0)

        @pl.when(i % tps == 0)
        def _():
            for r in (dsc_ref, dsh_ref, dg_ref):
                r[...] = jnp.zeros_like(r)

        @pl.when(i == 0)
        def _():
            dlg_ref[...] = jnp.zeros_like(dlg_ref)
            dlb_ref[...] = jnp.zeros_like(dlb_ref)

        xv = x_ref[...]
        _, pull = jax.vjp(ln, xv, f_ref[...], g_ref[0], lg_ref[...], lb_ref[...])
        dx_res, df, dg, dlg, dlb = pull((dy_ref[...],))
        dfb = df.astype(BF16)
        df_ref[...] = dfb
        dh = jnp.zeros((tmk, D_MODEL), F32)
        for j in range(D_FF // FF_CHUNK):
            sl = slice(j * FF_CHUNK, (j + 1) * FF_CHUNK)
            ds = _dg(dfb, w2_ref[sl, :], 1, 1)
            av, bv = a_ref[:, sl], b_ref[:, sl]
            sg = jax.nn.sigmoid(av)
            si = av * sg
            s_ref[:, sl] = (si * bv).astype(BF16)
            da = (ds * bv * (sg * (1.0 + av * (1.0 - sg)))).astype(BF16)
            db = (ds * si).astype(BF16)
            da_ref[:, sl] = da
            db_ref[:, sl] = db
            dh = dh + _dg(da, w1_ref[:, sl], 1, 1) + _dg(db, w3_ref[:, sl], 1, 1)
        dx_ref[...] = dx_res + dh * (1.0 + sc_ref[0])
        dsc_ref[0] += jnp.sum(dh * xv, axis=0, keepdims=True)
        dsh_ref[0] += jnp.sum(dh, axis=0, keepdims=True)
        dg_ref[0] += dg
        dlg_ref[...] += dlg
        dlb_ref[...] += dlb

    row = lambda c: pl.BlockSpec((tmk, c), lambda i: (i, 0))
    per_seq = pl.BlockSpec((1, 1, D_MODEL), lambda i: (i // tps, 0, 0))
    vec = pl.BlockSpec((1, D_MODEL), lambda i: (0, 0))
    seq_shape = jax.ShapeDtypeStruct(sc.shape, F32)
    vec_shape = jax.ShapeDtypeStruct((1, D_MODEL), F32)
    (dx, da, db, s, df, dsc, dsh, dg, dlg, dlb), landed = hosted_call(
        body, name=tag + "_bwd", grid=(t // tmk,),
        in_specs=[row(D_MODEL), row(D_MODEL), row(D_MODEL), row(D_FF), row(D_FF), per_seq, per_seq, per_seq, vec, vec,
                  _resident((D_MODEL, D_FF)), _resident((D_MODEL, D_FF)), _resident((D_FF, D_MODEL))],
        out_specs=[row(D_MODEL), row(D_FF), row(D_FF), row(D_FF), row(D_MODEL), per_seq, per_seq, per_seq, vec, vec],
        out_shape=[jax.ShapeDtypeStruct((t, D_MODEL), F32), jax.ShapeDtypeStruct((t, D_FF), BF16),
                   jax.ShapeDtypeStruct((t, D_FF), BF16), jax.ShapeDtypeStruct((t, D_FF), BF16),
                   jax.ShapeDtypeStruct((t, D_MODEL), BF16), seq_shape, seq_shape, seq_shape, vec_shape, vec_shape],
        args=(dy, x, f, a, b, sc, sh, g, lg, lb, w1, w3, w2), ride=ride)
    tt = min(DW_TOKENS, seq)
    if chain is None:
        dw2 = mm_tn(tag + "_dw2", s, df, D_FF // 2, D_MODEL, tt, BF16)
        dw1 = mm_tn(tag + "_dw1", h, da, D_MODEL, D_FF // 2, tt, BF16)
        dw3 = mm_tn(tag + "_dw3", h, db, D_MODEL, D_FF // 2, tt, BF16)
        return dx, (dsh, dsc, dg), (dw1, dw3, dw2, dlg, dlb), landed
    dw2, landed = mm_tn(tag + "_dw2", s, df, D_FF // 2, D_MODEL, tt, BF16, ride=chain((dsh, dsc, dg), dlg, dlb))
    dw1, (s_w2,) = mm_tn(tag + "_dw1", h, da, D_MODEL, D_FF // 2, tt, BF16,
                         ride=([dw2.reshape(N_DEV, D_FF // N_DEV, D_MODEL)], [False]))
    dw3, (s_w1,) = mm_tn(tag + "_dw3", h, db, D_MODEL, D_FF // 2, tt, BF16, ride=([_full_to_cols(dw1)], [False]))
    return dx, (s_w1, dw3, s_w2), landed


def kernel(x, c, w_ada, b_ada, ffn1_w1, ffn1_w3, ffn1_w2, ln1_g, ln1_b, w_in, conv_w, conv_b, dt_bias, a_log, d_ssd, ssd_norm_w, s5_a_re, s5_a_im, s5_log_dt, s5_b_re, s5_b_im, s5_c_re, s5_c_im, s5_d, w_glu, b_glu, w_out, ln2_g, ln2_b, ffn2_w1, ffn2_w3, ffn2_w2, ln3_g, ln3_b, loss_target, m_w_ada, m_b_ada, m_ffn1_w1, m_ffn1_w3, m_ffn1_w2, m_ln1_g, m_ln1_b, m_w_in, m_conv_w, m_conv_b, m_dt_bias, m_a_log, m_d_ssd, m_ssd_norm_w, m_s5_a_re, m_s5_a_im, m_s5_log_dt, m_s5_b_re, m_s5_b_im, m_s5_c_re, m_s5_c_im, m_s5_d, m_w_glu, m_b_glu, m_w_out, m_ln2_g, m_ln2_b, m_ffn2_w1, m_ffn2_w3, m_ffn2_w2, m_ln3_g, m_ln3_b, v_w_ada, v_b_ada, v_ffn1_w1, v_ffn1_w3, v_ffn1_w2, v_ln1_g, v_ln1_b, v_w_in, v_conv_w, v_conv_b, v_dt_bias, v_a_log, v_d_ssd, v_ssd_norm_w, v_s5_a_re, v_s5_a_im, v_s5_log_dt, v_s5_b_re, v_s5_b_im, v_s5_c_re, v_s5_c_im, v_s5_d, v_w_glu, v_b_glu, v_w_out, v_ln2_g, v_ln2_b, v_ffn2_w1, v_ffn2_w3, v_ffn2_w2, v_ln3_g, v_ln3_b):
    given = dict(locals())
    bsz, seq, _ = x.shape
    t = bsz * seq
    tm = min(512, seq)
    me = 4 * lax.axis_index("x") + 2 * lax.axis_index("y") + lax.axis_index("c")
    x0 = x.reshape(t, D_MODEL)
    target = loss_target.reshape(t, D_MODEL)

    g_col1, g_row1, g_c = gather_two_level(
        "gather_ffn1", [jnp.stack([ffn1_w1[0], ffn1_w3[0]]).astype(BF16), ffn1_w2[0].astype(BF16), c])
    f1w1, f1w3 = [_cols_to_full(g_col1[:, k]) for k in range(2)]
    f1w2 = g_row1.reshape(D_FF, D_MODEL)
    c_all = g_c.reshape(N_DEV * bsz, D_MODEL)

    n_loc = w_ada.shape[2]
    b_loc = lax.dynamic_slice(b_ada, (0, me * n_loc), (1, n_loc))
    mod_cols = ada_fwd(c_all, w_ada[0], b_loc)
    g_mod, = exchange("gather_mod", [mod_cols], [True])
    mine = lax.dynamic_slice(g_mod, (0, me * bsz, 0), (N_DEV, bsz, n_loc))
    mod = jnp.transpose(mine, (1, 0, 2)).reshape(bsz, N_MOD, 1, D_MODEL)
    sh1, sc1, g1, sh2, sc2, g2, sh3, sc3, g3 = [mod[:, k] for k in range(N_MOD)]

    x1, res1, (g_win, g_glu, g_out, g_conv) = _ffn_fwd(
        "ffn1", x0, sc1, sh1, g1, f1w1, f1w3, f1w2, ln1_g, ln1_b, seq, tm,
        ride=([w_in[0].astype(BF16), w_glu[0].astype(BF16), w_out[0].astype(BF16), conv_w[0]], [True] * 4))
    win = _cols_to_full(g_win)
    wglu = g_glu.reshape(S5_WIDTH, S5_WIDTH).astype(F32)
    wout = g_out.reshape(D_MODEL, D_MODEL)
    wo_ssd, wo_s5 = wout[:SSD_WIDTH], wout[SSD_WIDTH:]
    convw = jnp.transpose(g_conv, (1, 0, 2)).reshape(CONV_K, CONV_CH)
    w_z, w_xbc = win[:, :SSD_WIDTH], win[:, SSD_WIDTH:SSD_WIDTH + CONV_CH]
    w_dt = win[:, SSD_WIDTH + CONV_CH:SSD_WIDTH + CONV_CH + SSD_HEADS]
    w_u = win[:, SSD_WIDTH + CONV_CH + SSD_HEADS:]
    dt_pad = [jnp.pad(w_dt[:, HEADS_PER_GROUP * g:HEADS_PER_GROUP * (g + 1)], ((0, 0), (0, LANE - HEADS_PER_GROUP)))
              for g in range(SSD_GROUPS)]
    w_dtp = jnp.concatenate(dt_pad, axis=1)
    w_proj = jnp.concatenate([w_xbc, w_z, w_u, w_dtp], axis=1)

    h2, = rowwise_fwd("mix_mod", f_modulate, [x1], [sc2, sh2], [], [(D_MODEL, BF16)], seq, tm)
    proj = mm_nn("mix_proj", [h2], [w_proj], tm, P_COLS // 2)
    xc = conv_fwd(proj, convw, conv_b, seq, tm)
    dtb = jnp.pad(dt_bias.reshape(SSD_GROUPS, 1, HEADS_PER_GROUP), ((0, 0), (0, 0), (0, LANE - HEADS_PER_GROUP)))
    alog = jnp.pad(a_log.reshape(SSD_GROUPS, 1, HEADS_PER_GROUP), ((0, 0), (0, 0), (0, LANE - HEADS_PER_GROUP)))
    dcol = jnp.pad(d_ssd.reshape(SSD_GROUPS, HEADS_PER_GROUP, 1), ((0, 0), (0, LANE - HEADS_PER_GROUP), (0, 0)))
    nw = ssd_norm_w.reshape(SSD_GROUPS, 1, GROUP_COLS)
    y_ssd, hprev = ssd_fwd(xc, proj, dtb, alog, dcol, nw, bsz, seq)

    a_re2, a_im2, ldt2 = s5_a_re[0], s5_a_im[0], s5_log_dt.reshape(S5_GROUPS, 1)
    ab_re, ab_im, f_re, f_im = _whole(_disc_a, "s5_disc_a", [a_re2, a_im2, ldt2], [(S5_GROUPS, S5_STATE)] * 4)
    b_re2, b_im2 = s5_b_re.reshape(S5_COLS, S5_GROUP_CH), s5_b_im.reshape(S5_COLS, S5_GROUP_CH)
    fr_col, fi_col = f_re.reshape(S5_COLS, 1), f_im.reshape(S5_COLS, 1)
    bb_re, bb_im = _whole(_disc_b, "s5_disc_b", [fr_col, fi_col, b_re2, b_im2], [(S5_COLS, S5_GROUP_CH)] * 2)
    wb_re, wb_im = _blockdiag_b(bb_re).astype(BF16), _blockdiag_b(bb_im).astype(BF16)
    wc_re, wc_im = _blockdiag_c(s5_c_re[0]).astype(BF16), _blockdiag_c(s5_c_im[0]).astype(BF16)
    dt5 = jnp.exp(ldt2)
    lam_re, lam_im = (dt5 * a_re2).reshape(1, S5_COLS), (dt5 * a_im2).reshape(1, S5_COLS)
    sf_re, sf_im, sb_re, sb_im, cf_re, cf_im, cb_re, cb_im = s5_tables(lam_re, lam_im)
    d5 = s5_d.reshape(S5_Q, 1, LANE)
    (y5, xr_all, xi_all), (g_col2, g_row2) = s5_fwd(
        proj, wb_re, wb_im, wc_re, wc_im, sf_re, sf_im, cf_re, cf_im, d5, bsz, seq,
        ride=([jnp.stack([ffn2_w1[0], ffn2_w3[0]]).astype(BF16), ffn2_w2[0].astype(BF16)], [True] * 2))
    f2w1, f2w3 = [_cols_to_full(g_col2[:, k]) for k in range(2)]
    f2w2 = g_row2.reshape(D_FF, D_MODEL)
    o5, = rowwise_fwd("s5_glu", f_glu, [y5], [], [wglu, b_glu], [(S5_WIDTH, F32)], seq, tm)

    mix = mm_nn("mix_out", [y_ssd, o5], [wo_ssd, wo_s5], tm, D_MODEL)
    x2, = rowwise_fwd("mix_ln", _res_ln(1.0), [x1, mix], [g2], [ln2_g, ln2_b], [(D_MODEL, F32)], seq, tm)

    (dy, loss_loc), res3, _ = _ffn_fwd("ffn2", x2, sc3, sh3, g3, f2w1, f2w3, f2w2, ln3_g, ln3_b, seq, tm, target=target)

    dx2, dmod3, (d_f2w1, d_f2w3, d_f2w2, d_ln3g, d_ln3b), _ = _ffn_bwd(
        "ffn2", dy, x2, sc3, sh3, g3, f2w1, f2w3, f2w2, ln3_g, ln3_b, res3, seq, tm)

    (dx1_a, dmix), (dg2,), (d_ln2g, d_ln2b) = rowwise_bwd(
        "mix_ln_b", _res_ln(1.0), [x1, mix], [g2], [ln2_g, ln2_b], [dx2], seq, tm, [F32, BF16])
    tw = min(DW_TOKENS, seq)
    d_wo = jnp.concatenate([mm_tn("mix_dwo_ssd", y_ssd, dmix, SSD_WIDTH, D_MODEL, tw, BF16),
                            mm_tn("mix_dwo_s5", o5, dmix, S5_WIDTH, D_MODEL, tw, BF16)], axis=0)
    dy_ssd = mm_nt("mix_dy_ssd", [dmix], [wo_ssd], tm, SSD_WIDTH)
    do5 = mm_nt("mix_do5", [dmix], [wo_s5], tm, S5_WIDTH)

    (dy5,), _, (d_wglu, d_bglu) = rowwise_bwd("s5_glu_b", f_glu, [y5], [], [wglu, b_glu], [do5], seq, tm, [F32])
    (du, dwbr, dwbi, dwcr, dwci, dab_re, dab_im, dd5), (s_col2, s_row2, s_out, s_glu) = s5_bwd(
        proj, wb_re, wb_im, wc_re, wc_im, sb_re, sb_im, cb_re, cb_im, d5, xr_all, xi_all, dy5, bsz, seq,
        ride=([jnp.stack([_full_to_cols(d_f2w1), _full_to_cols(d_f2w3)], axis=1),
               d_f2w2.reshape(N_DEV, D_FF // N_DEV, D_MODEL), d_wo.reshape(N_DEV, D_MODEL // N_DEV, D_MODEL),
               d_wglu.reshape(N_DEV, S5_WIDTH // N_DEV, S5_WIDTH).astype(BF16)], [False] * 4))
    dbb_re, dbb_im = _unblock_b(dwbr), _unblock_b(dwbi)
    dfr_col, dfi_col, d_b_re, d_b_im = _whole_vjp(_disc_b, "s5_disc_b_b", [fr_col, fi_col, b_re2, b_im2],
                                                  [dbb_re, dbb_im])
    d_a_re, d_a_im, d_ldt = _whole_vjp(
        _disc_a, "s5_disc_a_b", [a_re2, a_im2, ldt2],
        [dab_re.reshape(S5_GROUPS, S5_STATE), dab_im.reshape(S5_GROUPS, S5_STATE),
         dfr_col.reshape(S5_GROUPS, S5_STATE), dfi_col.reshape(S5_GROUPS, S5_STATE)])
    d_c_re, d_c_im = _unblock_c(dwcr), _unblock_c(dwci)

    dxs, dbm, dcm, ddt, dz, ddtb, dalog, ddcol, dnw = ssd_bwd(xc, proj, dtb, alog, dcol, nw, hprev, dy_ssd, bsz, seq)
    dpre, d_convw, d_convb = conv_bwd_pre(proj, convw, conv_b, dxs, dbm, dcm, seq, tm)
    dxbc = conv_bwd_x(dpre, convw, seq, tm)

    dw_xbc = mm_tn("mix_dw_xbc", h2, dxbc, D_MODEL, CONV_CH, tw, BF16)
    dw_z = mm_tn("mix_dw_z", h2, dz, D_MODEL, SSD_WIDTH, tw, BF16)
    dw_u = mm_tn("mix_dw_u", h2, du, D_MODEL, S5_WIDTH, tw, BF16)
    dw_dt = mm_tn("mix_dw_dt", h2, ddt, D_MODEL, 2 * LANE, tw, BF16)
    dw_dt8 = jnp.concatenate([dw_dt[:, LANE * g:LANE * g + HEADS_PER_GROUP] for g in range(SSD_GROUPS)], axis=1)
    d_win = jnp.concatenate([dw_z, dw_xbc, dw_dt8, dw_u], axis=1)
    dh2, (s_win,) = mm_nt("mix_dh", [dxbc, dz, du, ddt], [w_xbc, w_z, w_u, w_dtp], tm, D_MODEL,
                          ride=([_full_to_cols(d_win)], [False]))
    (dx1,), (dsc2, dsh2), _ = rowwise_bwd("mix_mod_b", f_modulate, [x1], [sc2, sh2], [], [dh2], seq, tm, [F32],
                                          add_rows={0: dx1_a})

    packing = {}

    def small_and_dmod(dmod1, d_ln1g, d_ln1b):
        dmod = jnp.concatenate(list(dmod1) + [dsh2, dsc2, dg2] + list(dmod3), axis=1).reshape(bsz, N_MOD * D_MODEL)
        small = _small_grads(d_ln1g, d_ln1b)
        packing["names"] = list(small)
        packing["shapes"] = [small[k].shape for k in small]
        return [_pack(list(small.values())), dmod], [True, True]

    def _small_grads(d_ln1g, d_ln1b):
        return {
            "ln1_g": d_ln1g, "ln1_b": d_ln1b, "conv_w": d_convw, "conv_b": d_convb,
            "dt_bias": ddtb[:, 0, :HEADS_PER_GROUP].reshape(1, SSD_HEADS),
            "a_log": dalog[:, 0, :HEADS_PER_GROUP].reshape(1, SSD_HEADS),
            "d_ssd": ddcol[:, :HEADS_PER_GROUP, 0].reshape(1, SSD_HEADS),
            "ssd_norm_w": dnw.reshape(1, SSD_WIDTH),
            "s5_a_re": d_a_re[None], "s5_a_im": d_a_im[None], "s5_log_dt": d_ldt.reshape(1, S5_GROUPS),
            "s5_b_re": d_b_re.reshape(s5_b_re.shape), "s5_b_im": d_b_im.reshape(s5_b_im.shape),
            "s5_c_re": d_c_re[None], "s5_c_im": d_c_im[None], "s5_d": dd5.reshape(1, S5_WIDTH),
            "b_glu": d_bglu, "ln2_g": d_ln2g, "ln2_b": d_ln2b, "ln3_g": d_ln3g, "ln3_b": d_ln3b,
            "loss": loss_loc.reshape(1, 1),
        }

    dx0, (s_f1w1, d_f1w3, s_f1w2), (s_small, s_dmod) = _ffn_bwd(
        "ffn1", dx1, x0, sc1, sh1, g1, f1w1, f1w3, f1w2, ln1_g, ln1_b, res1, seq, tm, chain=small_and_dmod)
    names, shapes = packing["names"], packing["shapes"]
    s_f1w3, = exchange("sum_grads", [_full_to_cols(d_f1w3)], [False])

    out = {"grad_x": dx0.reshape(x.shape)}

    def put(name, res, shape):
        for key, val in zip(("grad_", "delta_", "new_m_", "new_v_"), res):
            out[key + name] = val.reshape(shape)

    for name, slots, k in (("ffn1_w1", s_f1w1, None), ("ffn1_w3", s_f1w3, None), ("ffn2_w1", s_col2, 0),
                           ("ffn2_w3", s_col2, 1)):
        w = given[name]
        put(name, adamw("adam_" + name, slots, w[0], given["m_" + name][0], given["v_" + name][0], 256, sel=k), w.shape)
    for name, slots in (("ffn1_w2", s_f1w2), ("ffn2_w2", s_row2)):
        w = given[name]
        put(name, adamw("adam_" + name, slots, w[0], given["m_" + name][0], given["v_" + name][0], 176), w.shape)
    put("w_in", adamw("adam_w_in", s_win, w_in[0], m_w_in[0], v_w_in[0], 256), w_in.shape)
    put("w_glu", adamw("adam_w_glu", s_glu, w_glu[0], m_w_glu[0], v_w_glu[0], 64), w_glu.shape)
    put("w_out", adamw("adam_w_out", s_out, w_out[0], m_w_out[0], v_w_out[0], 128), w_out.shape)

    dmod_all = s_dmod.reshape(N_DEV * bsz, N_MOD * D_MODEL)
    g_bada, g_wada = ada_bwd(c_all, dmod_all, lax.dynamic_slice(dmod_all, (0, me * n_loc), (N_DEV * bsz, n_loc)))
    put("w_ada", adamw("adam_w_ada", g_wada, w_ada[0], m_w_ada[0], v_w_ada[0], 256), w_ada.shape)
    put("b_ada", adamw("adam_b_ada", g_bada, b_ada, m_b_ada, v_b_ada, 1), b_ada.shape)

    not_params = {"conv_w": jnp.zeros((CONV_K, CONV_CH), F32), "loss": jnp.zeros((1, 1), F32)}
    pw, pm, pv = [_pack([not_params[k] if k in not_params else given[pre + k] for k in names]) for pre in ("", "m_", "v_")]
    res_small = adamw("adam_small", s_small, pw, pm, pv, pw.shape[0])
    parts = [_unpack(r, shapes) for r in res_small]
    for i, k in enumerate(names):
        if k not in not_params:
            put(k, [p[i] for p in parts], given[k].shape)
    out["loss"] = parts[0][names.index("loss")][0, 0]
    g_cw = lax.dynamic_slice(parts[0][names.index("conv_w")], (0, me * LANE), (CONV_K, LANE))
    put("conv_w", adamw("adam_conv_w", g_cw, conv_w[0], m_conv_w[0], v_conv_w[0], CONV_K), conv_w.shape)

    order = ["w_ada", "b_ada", "ffn1_w1", "ffn1_w3", "ffn1_w2", "ln1_g", "ln1_b", "w_in", "conv_w", "conv_b", "dt_bias",
             "a_log", "d_ssd", "ssd_norm_w", "s5_a_re", "s5_a_im", "s5_log_dt", "s5_b_re", "s5_b_im", "s5_c_re",
             "s5_c_im", "s5_d", "w_glu", "b_glu", "w_out", "ln2_g", "ln2_b", "ffn2_w1", "ffn2_w3", "ffn2_w2", "ln3_g",
             "ln3_b"]
    return (out["loss"], out["grad_x"], *[out[p + n] for p in ("grad_", "delta_", "new_m_", "new_v_") for n in order])
```

```python
import functools
import math

import jax
import jax.numpy as jnp
from jax import lax
from jax.experimental import pallas as pl
from jax.experimental.pallas import tpu as pltpu

F32 = jnp.float32
BF16 = jnp.bfloat16
HI = lax.Precision.HIGHEST
MESH = pl.DeviceIdType.MESH

N_DEV = 8
D_MODEL = 1024
D_FF = 2816
N_MOD = 9
SSD_WIDTH = 512
SSD_HEADS = 8
SSD_HEAD_DIM = 64
SSD_GROUPS = 2
SSD_STATE = 128
SSD_CHUNK = 128
GROUP_COLS = SSD_WIDTH // SSD_GROUPS
HEADS_PER_GROUP = SSD_HEADS // SSD_GROUPS
CONV_K = 4
CONV_CH = 1024
S5_WIDTH = 512
S5_GROUPS = 32
S5_GROUP_CH = 16
S5_STATE = 64
S5_COLS = S5_GROUPS * S5_STATE
S5_Q = 4
S5_CHUNK = 512
ALPHA = 2.0 ** 0.25
LN_EPS = 1e-5
LANE = 128
HALO = 8

P_XBC, P_Z, P_U, P_DT = 0, 1024, 1536, 2048
P_COLS = 2048 + SSD_GROUPS * LANE

ADAM_LR, ADAM_B1, ADAM_B2, ADAM_EPS, ADAM_WD, ADAM_STEP = 0.001, 0.9, 0.999, 1e-08, 0.01, 10

VMEM_LIMIT = 56 * 1024 * 1024


def _cp(*sem):
    return pltpu.CompilerParams(dimension_semantics=sem if sem else None, vmem_limit_bytes=VMEM_LIMIT)


def _dg(a, b, ca, cb):
    return lax.dot_general(a.astype(BF16), b.astype(BF16), (((ca,), (cb,)), ((), ())), preferred_element_type=F32)


@jax.custom_vjp
def bdot_nn(a, b):
    return _dg(a, b, 1, 0)


bdot_nn.defvjp(lambda a, b: (_dg(a, b, 1, 0), (a, b)),
               lambda r, g: (_dg(g, r[1], 1, 1), _dg(r[0], g, 0, 0)))


@jax.custom_vjp
def bdot_nt(a, b):
    return _dg(a, b, 1, 1)


bdot_nt.defvjp(lambda a, b: (_dg(a, b, 1, 1), (a, b)),
               lambda r, g: (_dg(g, r[1], 1, 0), _dg(g, r[0], 0, 0)))


@jax.custom_vjp
def bdot_tn(a, b):
    return _dg(a, b, 0, 0)


bdot_tn.defvjp(lambda a, b: (_dg(a, b, 0, 0), (a, b)),
               lambda r, g: (_dg(r[1], g, 1, 1), _dg(r[0], g, 1, 0)))


def _take_col(z):
    @jax.custom_vjp
    def take(x):
        return x[:, z:z + 1]

    def bwd(shape, g):
        hot = (lax.broadcasted_iota(jnp.int32, (1, shape[1]), 1) == z).astype(F32)
        return (g * hot,)

    take.defvjp(lambda x: (x[:, z:z + 1], x.shape), bwd)
    return take


def _take_row(z):
    @jax.custom_vjp
    def take(x):
        return x[z:z + 1, :]

    def bwd(shape, g):
        hot = (lax.broadcasted_iota(jnp.int32, (shape[0], 1), 0) == z).astype(F32)
        return (hot * g,)

    take.defvjp(lambda x: (x[z:z + 1, :], x.shape), bwd)
    return take


def _view(a):
    return a if isinstance(a, tuple) else (a, 0, a.shape[1])


def _col_spec(view, rows, width, index):
    _, off, _ = view
    assert off % width == 0
    return pl.BlockSpec((rows, width), lambda *g: (index(*g)[0], off // width + index(*g)[1]))


def _rw_in_specs(rows, bps, gps, tm, tps):
    specs = [_col_spec(_view(r), tm, _view(r)[2], lambda i: (i, 0)) for r in rows]
    specs += [pl.BlockSpec((1, 1, b.shape[2]), lambda i: (i // tps, 0, 0)) for b in bps]
    specs += [pl.BlockSpec(g.shape, lambda i, nd=g.ndim: (0,) * nd) for g in gps]
    return specs


def _rw_vals(refs, nr, nb, ng):
    vals = [r[...] for r in refs[:nr]]
    vals += [b[0] for b in refs[nr:nr + nb]]
    vals += [g[...] for g in refs[nr + nb:nr + nb + ng]]
    return vals


def rowwise_fwd(name, f, rows, bps, gps, outs, seq, tm):
    t = _view(rows[0])[0].shape[0]
    tps = seq // tm
    nr, nb, ng = len(rows), len(bps), len(gps)

    def body(*refs):
        res = f(*_rw_vals(refs, nr, nb, ng))
        for o, v in zip(refs[nr + nb + ng:], res):
            o[...] = v.astype(o.dtype)

    return pl.pallas_call(
        body, name=name, grid=(t // tm,),
        in_specs=_rw_in_specs(rows, bps, gps, tm, tps),
        out_specs=[pl.BlockSpec((tm, c), lambda i: (i, 0)) for c, _ in outs],
        out_shape=[jax.ShapeDtypeStruct((t, c), d) for c, d in outs],
        compiler_params=_cp("arbitrary"),
    )(*[_view(r)[0] for r in rows], *bps, *gps)


def rowwise_bwd(name, f, rows, bps, gps, douts, seq, tm, row_grads, add_rows=None):
    add_rows = add_rows or {}
    t = _view(rows[0])[0].shape[0]
    tps = seq // tm
    nr, nb, ng, nd = len(rows), len(bps), len(gps), len(douts)
    want = [k for k in range(nr) if row_grads[k] is not None]
    adds = sorted(add_rows)
    n_in = nr + nb + ng + nd + len(adds)

    def body(*refs):
        vals = _rw_vals(refs, nr, nb, ng)
        dvals = tuple(r[...] for r in refs[nr + nb + ng:nr + nb + ng + nd])
        add_refs = dict(zip(adds, refs[nr + nb + ng + nd:n_in]))
        out_refs = refs[n_in:]
        _, pull = jax.vjp(f, *vals)
        grads = pull(dvals)
        i = pl.program_id(0)
        for o, k in zip(out_refs, want):
            g = grads[k]
            if k in add_refs:
                g = g + add_refs[k][...]
            o[...] = g.astype(o.dtype)
        for j in range(nb):
            o = out_refs[len(want) + j]

            @pl.when(i % tps == 0)
            def _(o=o):
                o[...] = jnp.zeros_like(o)

            o[0] = o[0] + grads[nr + j]
        for j in range(ng):
            o = out_refs[len(want) + nb + j]

            @pl.when(i == 0)
            def _(o=o):
                o[...] = jnp.zeros_like(o)

            o[...] = o[...] + grads[nr + nb + j]

    in_specs = _rw_in_specs(rows, bps, gps, tm, tps)
    in_specs += [_col_spec(_view(d), tm, _view(d)[2], lambda i: (i, 0)) for d in douts]
    in_specs += [pl.BlockSpec((tm, add_rows[k].shape[1]), lambda i: (i, 0)) for k in adds]
    out_specs = [pl.BlockSpec((tm, _view(rows[k])[2]), lambda i: (i, 0)) for k in want]
    out_shape = [jax.ShapeDtypeStruct((t, _view(rows[k])[2]), row_grads[k]) for k in want]
    out_specs += [pl.BlockSpec((1, 1, b.shape[2]), lambda i: (i // tps, 0, 0)) for b in bps]
    out_shape += [jax.ShapeDtypeStruct(b.shape, F32) for b in bps]
    out_specs += [pl.BlockSpec(g.shape, lambda i, n=g.ndim: (0,) * n) for g in gps]
    out_shape += [jax.ShapeDtypeStruct(g.shape, F32) for g in gps]
    res = pl.pallas_call(
        body, name=name, grid=(t // tm,), in_specs=in_specs, out_specs=out_specs, out_shape=out_shape,
        compiler_params=_cp("arbitrary"),
    )(*[_view(r)[0] for r in rows], *bps, *gps, *[_view(d)[0] for d in douts], *[add_rows[k] for k in adds])
    nw = len(want)
    return res[:nw], res[nw:nw + nb], res[nw + nb:]


def mm_nn(name, xs, ws, tm, tn, out_dtype=F32):
    views = [_view(x) for x in xs]
    t, n, k = views[0][0].shape[0], ws[0].shape[1], len(xs)

    def body(*refs):
        acc = _dg(refs[0][...], refs[k][...], 1, 0)
        for i in range(1, k):
            acc = acc + _dg(refs[i][...], refs[k + i][...], 1, 0)
        refs[2 * k][...] = acc.astype(out_dtype)

    in_specs = [_col_spec(v, tm, v[2], lambda i, j: (i, 0)) for v in views]
    in_specs += [pl.BlockSpec((w.shape[0], tn), lambda i, j: (0, j)) for w in ws]
    return pl.pallas_call(
        body, name=name, grid=(t // tm, n // tn), in_specs=in_specs,
        out_specs=pl.BlockSpec((tm, tn), lambda i, j: (i, j)),
        out_shape=jax.ShapeDtypeStruct((t, n), out_dtype),
        compiler_params=_cp("parallel", "parallel"),
    )(*[v[0] for v in views], *ws)


def mm_nt(name, dys, ws, tm, tk, out_dtype=F32, ride=None):
    views = [_view(d) for d in dys]
    t, kk, k = views[0][0].shape[0], ws[0].shape[0], len(dys)

    def body(*refs):
        acc = _dg(refs[0][...], refs[k][...], 1, 1)
        for i in range(1, k):
            acc = acc + _dg(refs[i][...], refs[k + i][...], 1, 1)
        refs[2 * k][...] = acc.astype(out_dtype)

    in_specs = [_col_spec(v, tm, v[2], lambda i, j: (i, 0)) for v in views]
    in_specs += [pl.BlockSpec((tk, w.shape[1]), lambda i, j: (j, 0)) for w in ws]
    out_spec = pl.BlockSpec((tm, tk), lambda i, j: (i, j))
    out_shape = jax.ShapeDtypeStruct((t, kk), out_dtype)
    if ride is not None:
        (res,), landed = hosted_call(body, name=name, grid=(t // tm, kk // tk), in_specs=in_specs, out_specs=[out_spec],
                                     out_shape=[out_shape], args=(*[v[0] for v in views], *ws), ride=ride)
        return res, landed
    return pl.pallas_call(
        body, name=name, grid=(t // tm, kk // tk), in_specs=in_specs, out_specs=out_spec, out_shape=out_shape,
        compiler_params=_cp("parallel", "parallel"),
    )(*[v[0] for v in views], *ws)


def mm_tn(name, x, dy, tk, tn, tt, out_dtype=F32, ride=None):
    xv, dv = _view(x), _view(dy)
    t, kk, n = xv[0].shape[0], xv[2], dv[2]
    steps = t // tt

    def body(x_ref, d_ref, o_ref, acc_ref):
        @pl.when(pl.program_id(2) == 0)
        def _():
            acc_ref[...] = jnp.zeros_like(acc_ref)

        acc_ref[...] += _dg(x_ref[...], d_ref[...], 0, 0)

        @pl.when(pl.program_id(2) == steps - 1)
        def _():
            o_ref[...] = acc_ref[...].astype(out_dtype)

    in_specs = [_col_spec(xv, tt, tk, lambda a, b, c: (c, a)), _col_spec(dv, tt, tn, lambda a, b, c: (c, b))]
    out_spec = pl.BlockSpec((tk, tn), lambda a, b, c: (a, b))
    out_shape = jax.ShapeDtypeStruct((kk, n), out_dtype)
    if ride is not None:
        (res,), landed = hosted_call(body, name=name, grid=(kk // tk, n // tn, steps), in_specs=in_specs,
                                     out_specs=[out_spec], out_shape=[out_shape], scratch=[pltpu.VMEM((tk, tn), F32)],
                                     args=(xv[0], dv[0]), ride=ride)
        return res, landed
    return pl.pallas_call(
        body, name=name, grid=(kk // tk, n // tn, steps), in_specs=in_specs, out_specs=out_spec, out_shape=out_shape,
        scratch_shapes=[pltpu.VMEM((tk, tn), F32)],
        compiler_params=_cp("parallel", "parallel", "arbitrary"),
    )(xv[0], dv[0])


def _silu(x):
    return x * jax.nn.sigmoid(x)


def f_modulate(x, sc, sh):
    return (x * (1.0 + sc) + sh,)


def _res_ln(coef):
    def f(x, y, g, lg, lb):
        r = ALPHA * x + (coef * g) * y
        mu = jnp.mean(r, axis=-1, keepdims=True)
        d = r - mu
        var = jnp.mean(d * d, axis=-1, keepdims=True)
        return (d * lax.rsqrt(var + LN_EPS) * lg + lb,)
    return f


def f_glu(y, w, b):
    g = jax.nn.gelu(y)
    return (g * jax.nn.sigmoid(bdot_nn(g, w) + b),)


def _shift_down(x, halo, k):
    if k == 0:
        return x
    r = pltpu.roll(x, k, 0)
    hr = pltpu.roll(halo, k, 0)
    row = lax.broadcasted_iota(jnp.int32, (HALO, 1), 0)
    top = jnp.where(row < k, hr, r[:HALO])
    return jnp.concatenate([top, r[HALO:]], axis=0)


def _shift_up(x, halo, k):
    if k == 0:
        return x
    n = x.shape[0]
    r = pltpu.roll(x, n - k, 0)
    hr = pltpu.roll(halo, HALO - k, 0)
    row = lax.broadcasted_iota(jnp.int32, (HALO, 1), 0)
    bot = jnp.where(row >= HALO - k, hr, r[n - HALO:])
    return jnp.concatenate([r[:n - HALO], bot], axis=0)


def _conv_pre(x, halo, w, b):
    acc = x * w[CONV_K - 1:CONV_K, :] + b
    for k in range(1, CONV_K):
        acc = acc + _shift_down(x, halo, k) * w[CONV_K - 1 - k:CONV_K - k, :]
    return acc


def _rows_before(width, tm):
    return pl.BlockSpec((HALO, width), lambda i: (jnp.maximum(i * (tm // HALO) - 1, 0), 0))


def conv_fwd(proj, w, b, seq, tm):
    t = proj.shape[0]
    tps = seq // tm

    def body(x_ref, h_ref, w_ref, b_ref, o_ref):
        first = (pl.program_id(0) % tps == 0)
        halo = jnp.where(first, 0.0, h_ref[...])
        o_ref[...] = _silu(_conv_pre(x_ref[...], halo, w_ref[...], b_ref[...]))

    return pl.pallas_call(
        body, name="conv_fwd", grid=(t // tm,),
        in_specs=[pl.BlockSpec((tm, CONV_CH), lambda i: (i, 0)), _rows_before(CONV_CH, tm),
                  pl.BlockSpec((CONV_K, CONV_CH), lambda i: (0, 0)), pl.BlockSpec((1, CONV_CH), lambda i: (0, 0))],
        out_specs=pl.BlockSpec((tm, CONV_CH), lambda i: (i, 0)),
        out_shape=jax.ShapeDtypeStruct((t, CONV_CH), F32),
        compiler_params=_cp("arbitrary"),
    )(proj, proj, w, b)


def conv_bwd_pre(proj, w, b, dxs, dbm, dcm, seq, tm):
    t = proj.shape[0]
    tps = seq // tm

    def body(x_ref, h_ref, w_ref, b_ref, d1, d2, d3, dp_ref, dw_ref, db_ref):
        i = pl.program_id(0)
        halo = jnp.where(i % tps == 0, 0.0, h_ref[...])
        x = x_ref[...]
        pre = _conv_pre(x, halo, w_ref[...], b_ref[...])
        sg = jax.nn.sigmoid(pre)
        dout = jnp.concatenate([d1[...], d2[...], d3[...]], axis=1)
        dp = dout * (sg * (1.0 + pre * (1.0 - sg)))
        dp_ref[...] = dp

        @pl.when(i == 0)
        def _():
            dw_ref[...] = jnp.zeros_like(dw_ref)
            db_ref[...] = jnp.zeros_like(db_ref)

        db_ref[...] += jnp.sum(dp, axis=0, keepdims=True)
        for k in range(CONV_K):
            j = CONV_K - 1 - k
            dw_ref[j:j + 1, :] += jnp.sum(dp * _shift_down(x, halo, k), axis=0, keepdims=True)

    return pl.pallas_call(
        body, name="conv_bwd_pre", grid=(t // tm,),
        in_specs=[pl.BlockSpec((tm, CONV_CH), lambda i: (i, 0)), _rows_before(CONV_CH, tm),
                  pl.BlockSpec((CONV_K, CONV_CH), lambda i: (0, 0)), pl.BlockSpec((1, CONV_CH), lambda i: (0, 0)),
                  pl.BlockSpec((tm, 512), lambda i: (i, 0)), pl.BlockSpec((tm, 256), lambda i: (i, 0)),
                  pl.BlockSpec((tm, 256), lambda i: (i, 0))],
        out_specs=[pl.BlockSpec((tm, CONV_CH), lambda i: (i, 0)), pl.BlockSpec((CONV_K, CONV_CH), lambda i: (0, 0)),
                   pl.BlockSpec((1, CONV_CH), lambda i: (0, 0))],
        out_shape=[jax.ShapeDtypeStruct((t, CONV_CH), F32), jax.ShapeDtypeStruct((CONV_K, CONV_CH), F32),
                   jax.ShapeDtypeStruct((1, CONV_CH), F32)],
        compiler_params=_cp("arbitrary"),
    )(proj, proj, w, b, dxs, dbm, dcm)


def conv_bwd_x(dpre, w, seq, tm):
    t = dpre.shape[0]
    tps = seq // tm
    blocks = tm // HALO
    last = t // HALO - 1

    def body(d_ref, h_ref, w_ref, o_ref):
        halo = jnp.where(pl.program_id(0) % tps == tps - 1, 0.0, h_ref[...])
        d = d_ref[...]
        w = w_ref[...]
        acc = d * w[CONV_K - 1:CONV_K, :]
        for k in range(1, CONV_K):
            acc = acc + _shift_up(d, halo, k) * w[CONV_K - 1 - k:CONV_K - k, :]
        o_ref[...] = acc

    return pl.pallas_call(
        body, name="conv_bwd_x", grid=(t // tm,),
        in_specs=[pl.BlockSpec((tm, CONV_CH), lambda i: (i, 0)),
                  pl.BlockSpec((HALO, CONV_CH), lambda i: (jnp.minimum((i + 1) * blocks, last), 0)),
                  pl.BlockSpec((CONV_K, CONV_CH), lambda i: (0, 0))],
        out_specs=pl.BlockSpec((tm, CONV_CH), lambda i: (i, 0)),
        out_shape=jax.ShapeDtypeStruct((t, CONV_CH), F32),
        compiler_params=_cp("arbitrary"),
    )(dpre, dpre, w)


def _softplus(x):
    return jnp.maximum(x, 0.0) + jnp.log1p(jnp.exp(-jnp.abs(x)))


def _ssd_chunk(xs, bg, cg, dtr, zz, hp, dtb, alog, dcol, nw):
    l = xs.shape[0]
    row = lax.broadcasted_iota(jnp.int32, (l, l), 0)
    col = lax.broadcasted_iota(jnp.int32, (l, l), 1)
    causal = row >= col
    tril = causal.astype(F32)
    expand = (lax.broadcasted_iota(jnp.int32, (LANE, GROUP_COLS), 1) // SSD_HEAD_DIM
              == lax.broadcasted_iota(jnp.int32, (LANE, GROUP_COLS), 0)).astype(F32)
    head_of_col = lax.broadcasted_iota(jnp.int32, (1, GROUP_COLS), 1) // SSD_HEAD_DIM
    last_row = (lax.broadcasted_iota(jnp.int32, (l, 1), 0) == l - 1).astype(F32)

    dtc = _softplus(dtr + dtb)
    a_c = dtc * (-jnp.exp(alog))
    acs_c = jnp.dot(tril, a_c, precision=HI, preferred_element_type=F32)
    dt_e = jnp.dot(dtc, expand, precision=HI, preferred_element_type=F32)
    acs_e = jnp.dot(acs_c, expand, precision=HI, preferred_element_type=F32)
    alast_e = jnp.sum(acs_e * last_row, axis=0, keepdims=True)
    x = xs * dt_e
    states = bdot_tn(bg, x * jnp.exp(alast_e - acs_e))
    h_next = jnp.exp(alast_e) * hp + states
    d_e = jnp.sum(dcol * expand, axis=0, keepdims=True)
    y = bdot_nn(cg, hp) * jnp.exp(acs_e) + d_e * xs
    cb = bdot_nt(cg, bg)
    acs_t = acs_c.T
    for z in range(HEADS_PER_GROUP):
        seg = _take_col(z)(acs_c) - _take_row(z)(acs_t)
        lmat = jnp.exp(jnp.where(causal, seg, -1e30))
        y = y + bdot_nn(cb * lmat, x * (head_of_col == z).astype(F32))
    yz = y * _silu(zz)
    ms = jnp.mean(yz * yz, axis=-1, keepdims=True)
    return yz * lax.rsqrt(ms + LN_EPS) * nw, h_next


SSD_SUB = 2
SSD_ROWS = SSD_SUB * SSD_CHUNK


def _ssd_in_specs(steps, rev):
    def tok(b, c):
        return b * steps + (steps - 1 - c if rev else c)

    whole = lambda *shape: pl.BlockSpec(shape, lambda b, c: (0,) * len(shape))
    both = SSD_GROUPS * SSD_STATE
    return [
        pl.BlockSpec((SSD_ROWS, SSD_WIDTH), lambda b, c: (tok(b, c), 0)),
        pl.BlockSpec((SSD_ROWS, both), lambda b, c: (tok(b, c), SSD_WIDTH // both)),
        pl.BlockSpec((SSD_ROWS, both), lambda b, c: (tok(b, c), SSD_WIDTH // both + 1)),
        pl.BlockSpec((SSD_ROWS, SSD_GROUPS * LANE), lambda b, c: (tok(b, c), P_DT // (SSD_GROUPS * LANE))),
        pl.BlockSpec((SSD_ROWS, SSD_WIDTH), lambda b, c: (tok(b, c), P_Z // SSD_WIDTH)),
        whole(SSD_GROUPS, 1, LANE), whole(SSD_GROUPS, 1, LANE), whole(SSD_GROUPS, LANE, 1),
        whole(SSD_GROUPS, 1, GROUP_COLS),
    ], tok


def _piece(ref, s, g, width):
    return ref[s * SSD_CHUNK:(s + 1) * SSD_CHUNK, g * width:(g + 1) * width]


def ssd_fwd(xc, proj, dtb, alog, dcol, nw, bsz, seq, ride=None):
    t = xc.shape[0]
    nc = seq // SSD_CHUNK
    steps = nc // SSD_SUB
    in_specs, tok = _ssd_in_specs(steps, False)

    def body(xs, bm, cm, dtr, zz, dtb_r, alog_r, dcol_r, nw_r, y_ref, hp_ref, h_scr):
        @pl.when(pl.program_id(1) == 0)
        def _():
            h_scr[...] = jnp.zeros_like(h_scr)

        for g in range(SSD_GROUPS):
            h = h_scr[g]
            for s in range(SSD_SUB):
                hp_ref[g, 0, s] = h
                y, h = _ssd_chunk(_piece(xs, s, g, GROUP_COLS), _piece(bm, s, g, SSD_STATE),
                                  _piece(cm, s, g, SSD_STATE), _piece(dtr, s, g, LANE), _piece(zz, s, g, GROUP_COLS), h,
                                  dtb_r[g], alog_r[g], dcol_r[g], nw_r[g])
                y_ref[s * SSD_CHUNK:(s + 1) * SSD_CHUNK, g * GROUP_COLS:(g + 1) * GROUP_COLS] = y
            h_scr[g] = h

    return hosted_call(
        body, name="ssd_fwd", grid=(bsz, steps), in_specs=in_specs,
        out_specs=[pl.BlockSpec((SSD_ROWS, SSD_WIDTH), lambda b, c: (tok(b, c), 0)),
                   pl.BlockSpec((SSD_GROUPS, 1, SSD_SUB, SSD_STATE, GROUP_COLS), lambda b, c: (0, b, c, 0, 0))],
        out_shape=[jax.ShapeDtypeStruct((t, SSD_WIDTH), F32),
                   jax.ShapeDtypeStruct((SSD_GROUPS, bsz, nc, SSD_STATE, GROUP_COLS), F32)],
        scratch=[pltpu.VMEM((SSD_GROUPS, SSD_STATE, GROUP_COLS), F32)],
        args=(xc, xc, xc, proj, proj, dtb, alog, dcol, nw), ride=ride)


def ssd_bwd(xc, proj, dtb, alog, dcol, nw, hprev, dy, bsz, seq):
    t = xc.shape[0]
    nc = seq // SSD_CHUNK
    steps = nc // SSD_SUB
    in_specs, tok = _ssd_in_specs(steps, True)
    in_specs += [pl.BlockSpec((SSD_GROUPS, 1, SSD_SUB, SSD_STATE, GROUP_COLS), lambda b, c: (0, b, steps - 1 - c, 0, 0)),
                 pl.BlockSpec((SSD_ROWS, SSD_WIDTH), lambda b, c: (tok(b, c), 0))]

    def body(xs, bm, cm, dtr, zz, dtb_r, alog_r, dcol_r, nw_r, hp_ref, dy_ref,
             dxs, dbm, dcm, ddt, dzz, ddtb, dalog, ddcol, dnw, dh_scr):
        b, c = pl.program_id(0), pl.program_id(1)

        @pl.when(c == 0)
        def _():
            dh_scr[...] = jnp.zeros_like(dh_scr)

        @pl.when((b == 0) & (c == 0))
        def _():
            for r in (ddtb, dalog, ddcol, dnw):
                r[...] = jnp.zeros_like(r)

        for g in range(SSD_GROUPS):
            wide = slice(g * GROUP_COLS, (g + 1) * GROUP_COLS)
            state = slice(g * SSD_STATE, (g + 1) * SSD_STATE)
            dh = dh_scr[g]
            for s in reversed(range(SSD_SUB)):
                rows = slice(s * SSD_CHUNK, (s + 1) * SSD_CHUNK)
                _, pull = jax.vjp(_ssd_chunk, xs[rows, wide], bm[rows, state], cm[rows, state],
                                  _piece(dtr, s, g, LANE), zz[rows, wide], hp_ref[g, 0, s],
                                  dtb_r[g], alog_r[g], dcol_r[g], nw_r[g])
                d = pull((dy_ref[rows, wide], dh))
                dxs[rows, wide], dbm[rows, state], dcm[rows, state], dzz[rows, wide] = d[0], d[1], d[2], d[4]
                ddt[rows, g * LANE:(g + 1) * LANE] = d[3]
                dh = d[5]
                ddtb[g] += d[6]
                dalog[g] += d[7]
                ddcol[g] += d[8]
                dnw[g] += d[9]
            dh_scr[g] = dh

    def tile(w):
        return pl.BlockSpec((SSD_ROWS, w), lambda b, c: (tok(b, c), 0))

    whole = lambda *shape: pl.BlockSpec(shape, lambda b, c: (0,) * len(shape))
    return pl.pallas_call(
        body, name="ssd_bwd", grid=(bsz, steps), in_specs=in_specs,
        out_specs=[tile(SSD_WIDTH), tile(2 * SSD_STATE), tile(2 * SSD_STATE), tile(2 * LANE), tile(SSD_WIDTH),
                   whole(SSD_GROUPS, 1, LANE), whole(SSD_GROUPS, 1, LANE), whole(SSD_GROUPS, LANE, 1),
                   whole(SSD_GROUPS, 1, GROUP_COLS)],
        out_shape=[jax.ShapeDtypeStruct((t, SSD_WIDTH), F32), jax.ShapeDtypeStruct((t, 2 * SSD_STATE), F32),
                   jax.ShapeDtypeStruct((t, 2 * SSD_STATE), F32), jax.ShapeDtypeStruct((t, 2 * LANE), F32),
                   jax.ShapeDtypeStruct((t, SSD_WIDTH), F32),
                   jax.ShapeDtypeStruct((SSD_GROUPS, 1, LANE), F32), jax.ShapeDtypeStruct((SSD_GROUPS, 1, LANE), F32),
                   jax.ShapeDtypeStruct((SSD_GROUPS, LANE, 1), F32),
                   jax.ShapeDtypeStruct((SSD_GROUPS, 1, GROUP_COLS), F32)],
        scratch_shapes=[pltpu.VMEM((SSD_GROUPS, SSD_STATE, GROUP_COLS), F32)],
        compiler_params=_cp("arbitrary", "arbitrary"),
    )(xc, xc, xc, proj, proj, dtb, alog, dcol, nw, hprev, dy)


def _disc_a(a_re, a_im, log_dt):
    dt = jnp.exp(log_dt)
    mag = jnp.exp(dt * a_re)
    ab_re, ab_im = mag * jnp.cos(dt * a_im), mag * jnp.sin(dt * a_im)
    den = a_re * a_re + a_im * a_im
    nr, ni = ab_re - 1.0, ab_im
    f_re, f_im = (nr * a_re + ni * a_im) / den, (ni * a_re - nr * a_im) / den
    return ab_re, ab_im, f_re, f_im


def _disc_b(f_re, f_im, b_re, b_im):
    return f_re * b_re - f_im * b_im, f_re * b_im + f_im * b_re


def _whole(f, name, args, outs):
    def body(*refs):
        res = f(*[r[...] for r in refs[:len(args)]])
        for o, v in zip(refs[len(args):], res):
            o[...] = v

    return pl.pallas_call(body, name=name, out_shape=[jax.ShapeDtypeStruct(s, F32) for s in outs])(*args)


def _whole_vjp(f, name, args, cts):
    def body(*refs):
        vals = [r[...] for r in refs[:len(args)]]
        _, pull = jax.vjp(f, *vals)
        res = pull(tuple(r[...] for r in refs[len(args):len(args) + len(cts)]))
        for o, v in zip(refs[len(args) + len(cts):], res):
            o[...] = v

    return pl.pallas_call(body, name=name, out_shape=[jax.ShapeDtypeStruct(a.shape, F32) for a in args])(*args, *cts)


S5_SUB = 8
S5_STEPS = 3


def s5_tables(lam_re, lam_im):
    rows = S5_STEPS * S5_SUB

    def body(lr_ref, li_ref, sf_re, sf_im, sb_re, sb_im, cf_re, cf_im, cb_re, cb_im):
        lr, li = lr_ref[...], li_ref[...]

        def power(k):
            m = jnp.exp(k * lr)
            return m * jnp.cos(k * li), m * jnp.sin(k * li)

        srow = lax.broadcasted_iota(jnp.int32, (rows, 1), 0)
        k = jnp.left_shift(1, srow // S5_SUB)
        tt = srow % S5_SUB
        pr, pi = power(k.astype(F32))
        fwd, bwd = tt >= k, tt < S5_SUB - k
        sf_re[...], sf_im[...] = jnp.where(fwd, pr, 0.0), jnp.where(fwd, pi, 0.0)
        sb_re[...], sb_im[...] = jnp.where(bwd, pr, 0.0), jnp.where(bwd, pi, 0.0)
        trow = lax.broadcasted_iota(jnp.int32, (S5_SUB, 1), 0)
        cf_re[...], cf_im[...] = power((trow + 1).astype(F32))
        cb_re[...], cb_im[...] = power((S5_SUB - trow).astype(F32))

    shp = [jax.ShapeDtypeStruct((rows, S5_COLS), F32)] * 4 + [jax.ShapeDtypeStruct((S5_SUB, S5_COLS), F32)] * 4
    return pl.pallas_call(body, name="s5_tables", out_shape=shp)(lam_re, lam_im)


def _s5_coefs(steps_re, steps_im, carry_re, carry_im, reverse):
    sign = -1.0 if reverse else 1.0
    steps = [(steps_re[s * S5_SUB:(s + 1) * S5_SUB, :], sign * steps_im[s * S5_SUB:(s + 1) * S5_SUB, :])
             for s in range(S5_STEPS)]
    return steps, (carry_re[...], sign * carry_im[...])


def _s5_block_scan(ar, ai, coefs, cr, ci, reverse):
    steps, (qr, qi) = coefs
    for s, (pr, pi) in enumerate(steps):
        shift = S5_SUB - (1 << s) if reverse else (1 << s)
        sr, si = pltpu.roll(ar, shift, 0), pltpu.roll(ai, shift, 0)
        ar, ai = ar + pr * sr - pi * si, ai + pr * si + pi * sr
    br, bi = jnp.broadcast_to(cr, ar.shape), jnp.broadcast_to(ci, ai.shape)
    return ar + qr * br - qi * bi, ai + qr * bi + qi * br


def _s5_specs(n5, rev):
    def tok(q, b, c):
        return b * n5 + (n5 - 1 - c if rev else c)

    qcols = S5_COLS // S5_Q
    specs = [
        pl.BlockSpec((S5_CHUNK, LANE), lambda q, b, c: (tok(q, b, c), P_U // LANE + q)),
        pl.BlockSpec((1, LANE, qcols), lambda q, b, c: (q, 0, 0)),
        pl.BlockSpec((1, LANE, qcols), lambda q, b, c: (q, 0, 0)),
        pl.BlockSpec((1, qcols, LANE), lambda q, b, c: (q, 0, 0)),
        pl.BlockSpec((1, qcols, LANE), lambda q, b, c: (q, 0, 0)),
        pl.BlockSpec((S5_STEPS * S5_SUB, qcols), lambda q, b, c: (0, q)),
        pl.BlockSpec((S5_STEPS * S5_SUB, qcols), lambda q, b, c: (0, q)),
        pl.BlockSpec((S5_SUB, qcols), lambda q, b, c: (0, q)),
        pl.BlockSpec((S5_SUB, qcols), lambda q, b, c: (0, q)),
        pl.BlockSpec((1, 1, LANE), lambda q, b, c: (q, 0, 0)),
    ]
    return specs, tok, qcols


def s5_fwd(proj, wb_re, wb_im, wc_re, wc_im, sf_re, sf_im, cf_re, cf_im, dvec, bsz, seq, ride=None):
    t = proj.shape[0]
    n5 = seq // S5_CHUNK
    in_specs, tok, qcols = _s5_specs(n5, False)

    def body(u_ref, wbr, wbi, wcr, wci, sfr, sfi, cfr, cfi, d_ref, y_ref, xr_ref, xi_ref, cr_scr, ci_scr):
        @pl.when(pl.program_id(2) == 0)
        def _():
            cr_scr[...] = jnp.zeros_like(cr_scr)
            ci_scr[...] = jnp.zeros_like(ci_scr)

        u = u_ref[...]
        bur, bui = _dg(u, wbr[0], 1, 0), _dg(u, wbi[0], 1, 0)
        coefs = _s5_coefs(sfr, sfi, cfr, cfi, False)
        cr, ci = cr_scr[...], ci_scr[...]
        for r in range(S5_CHUNK // S5_SUB):
            rows = slice(r * S5_SUB, (r + 1) * S5_SUB)
            xr, xi = _s5_block_scan(bur[rows], bui[rows], coefs, cr, ci, False)
            xr_ref[rows, :], xi_ref[rows, :] = xr, xi
            cr, ci = xr[S5_SUB - 1:, :], xi[S5_SUB - 1:, :]
        cr_scr[...], ci_scr[...] = cr, ci
        y_ref[...] = _dg(xr_ref[...], wcr[0], 1, 0) - _dg(xi_ref[...], wci[0], 1, 0) + u * d_ref[0]

    def tile(w):
        return pl.BlockSpec((S5_CHUNK, w), lambda q, b, c: (tok(q, b, c), q))

    return hosted_call(
        body, name="s5_fwd", grid=(S5_Q, bsz, n5), in_specs=in_specs,
        out_specs=[tile(LANE), tile(qcols), tile(qcols)],
        out_shape=[jax.ShapeDtypeStruct((t, S5_WIDTH), F32), jax.ShapeDtypeStruct((t, S5_COLS), F32),
                   jax.ShapeDtypeStruct((t, S5_COLS), F32)],
        scratch=[pltpu.VMEM((1, qcols), F32)] * 2,
        args=(proj, wb_re, wb_im, wc_re, wc_im, sf_re, sf_im, cf_re, cf_im, dvec), ride=ride)


def s5_bwd(proj, wb_re, wb_im, wc_re, wc_im, sb_re, sb_im, cb_re, cb_im, dvec, xr_all, xi_all, dy, bsz, seq,
           ride=None):
    t = proj.shape[0]
    n5 = seq // S5_CHUNK
    in_specs, tok, qcols = _s5_specs(n5, True)
    blocks = S5_CHUNK // HALO

    def prev_rows(q, b, c):
        return (jnp.maximum(tok(q, b, c) * blocks - 1, 0), q)

    in_specs += [pl.BlockSpec((S5_CHUNK, qcols), lambda q, b, c: (tok(q, b, c), q)),
                 pl.BlockSpec((S5_CHUNK, qcols), lambda q, b, c: (tok(q, b, c), q)),
                 pl.BlockSpec((HALO, qcols), prev_rows), pl.BlockSpec((HALO, qcols), prev_rows),
                 pl.BlockSpec((S5_CHUNK, LANE), lambda q, b, c: (tok(q, b, c), q))]

    def body(u_ref, wbr, wbi, wcr, wci, sbr, sbi, cbr, cbi, d_ref, xr_ref, xi_ref, pr_ref, pi_ref, dy_ref,
             du_ref, dwbr, dwbi, dwcr, dwci, dar, dai, dd_ref, gr_scr, gi_scr, gr_all, gi_all):
        b, c = pl.program_id(1), pl.program_id(2)

        @pl.when(c == 0)
        def _():
            gr_scr[...] = jnp.zeros_like(gr_scr)
            gi_scr[...] = jnp.zeros_like(gi_scr)

        @pl.when((b == 0) & (c == 0))
        def _():
            for r in (dwbr, dwbi, dwcr, dwci, dar, dai, dd_ref):
                r[...] = jnp.zeros_like(r)

        u, dy_v = u_ref[...], dy_ref[...]
        g0r, g0i = _dg(dy_v, wcr[0], 1, 1), -_dg(dy_v, wci[0], 1, 1)
        coefs = _s5_coefs(sbr, sbi, cbr, cbi, True)
        cr, ci = gr_scr[...], gi_scr[...]
        for r in reversed(range(S5_CHUNK // S5_SUB)):
            rows = slice(r * S5_SUB, (r + 1) * S5_SUB)
            br, bi = _s5_block_scan(g0r[rows], g0i[rows], coefs, cr, ci, True)
            gr_all[rows, :], gi_all[rows, :] = br, bi
            cr, ci = br[:1, :], bi[:1, :]
        gr_scr[...], gi_scr[...] = cr, ci
        gr, gi = gr_all[...], gi_all[...]

        row = lax.broadcasted_iota(jnp.int32, (S5_CHUNK, 1), 0)
        xr, xi = xr_ref[...], xi_ref[...]
        is_first = (c == n5 - 1)
        hr = jnp.where(is_first, 0.0, pr_ref[...][HALO - 1:, :])
        hi = jnp.where(is_first, 0.0, pi_ref[...][HALO - 1:, :])
        xpr = jnp.where(row >= 1, pltpu.roll(xr, 1, 0), hr)
        xpi = jnp.where(row >= 1, pltpu.roll(xi, 1, 0), hi)
        dar[0] += jnp.sum(xpr * gr + xpi * gi, axis=0, keepdims=True)
        dai[0] += jnp.sum(xpr * gi - xpi * gr, axis=0, keepdims=True)
        du_ref[...] = _dg(gr, wbr[0], 1, 1) + _dg(gi, wbi[0], 1, 1) + dy_v * d_ref[0]
        dwbr[0] += _dg(u, gr, 0, 0)
        dwbi[0] += _dg(u, gi, 0, 0)
        dwcr[0] += _dg(xr, dy_v, 0, 0)
        dwci[0] -= _dg(xi, dy_v, 0, 0)
        dd_ref[0] += jnp.sum(dy_v * u, axis=0, keepdims=True)

    def acc(shape):
        return pl.BlockSpec((1,) + shape, lambda q, b, c: (q, 0, 0))

    return hosted_call(
        body, name="s5_bwd", grid=(S5_Q, bsz, n5), in_specs=in_specs,
        out_specs=[pl.BlockSpec((S5_CHUNK, LANE), lambda q, b, c: (tok(q, b, c), q)),
                   acc((LANE, qcols)), acc((LANE, qcols)), acc((qcols, LANE)), acc((qcols, LANE)),
                   acc((1, qcols)), acc((1, qcols)), acc((1, LANE))],
        out_shape=[jax.ShapeDtypeStruct((t, S5_WIDTH), F32),
                   jax.ShapeDtypeStruct((S5_Q, LANE, qcols), F32), jax.ShapeDtypeStruct((S5_Q, LANE, qcols), F32),
                   jax.ShapeDtypeStruct((S5_Q, qcols, LANE), F32), jax.ShapeDtypeStruct((S5_Q, qcols, LANE), F32),
                   jax.ShapeDtypeStruct((S5_Q, 1, qcols), F32), jax.ShapeDtypeStruct((S5_Q, 1, qcols), F32),
                   jax.ShapeDtypeStruct((S5_Q, 1, LANE), F32)],
        scratch=[pltpu.VMEM((1, qcols), F32)] * 2 + [pltpu.VMEM((S5_CHUNK, qcols), F32)] * 2,
        args=(proj, wb_re, wb_im, wc_re, wc_im, sb_re, sb_im, cb_re, cb_im, dvec, xr_all, xi_all, xr_all, xi_all, dy),
        ride=ride)


def _blockdiag_b(bb):
    b4 = bb.reshape(S5_Q, 8, S5_STATE, S5_GROUP_CH)
    eye = jnp.eye(8, dtype=bb.dtype)
    w = jnp.einsum("qgph,gk->qghkp", b4, eye)
    return w.reshape(S5_Q, LANE, S5_COLS // S5_Q)


def _unblock_b(dw):
    d = dw.reshape(S5_Q, 8, S5_GROUP_CH, 8, S5_STATE)
    d = jnp.einsum("qghgp->qgph", d)
    return d.reshape(S5_COLS, S5_GROUP_CH)


def _blockdiag_c(cc):
    c4 = cc.reshape(S5_Q, 8, S5_GROUP_CH, S5_STATE)
    eye = jnp.eye(8, dtype=cc.dtype)
    w = jnp.einsum("qghp,gk->qgpkh", c4, eye)
    return w.reshape(S5_Q, S5_COLS // S5_Q, LANE)


def _unblock_c(dw):
    d = dw.reshape(S5_Q, 8, S5_STATE, 8, S5_GROUP_CH)
    d = jnp.einsum("qgpgh->qghp", d)
    return d.reshape(S5_GROUPS, S5_GROUP_CH, S5_STATE)


def ada_fwd(c_all, w_loc, b_loc):
    def body(c_ref, w_ref, b_ref, o_ref):
        o_ref[...] = _dg(_silu(c_ref[...]), w_ref[...], 1, 0) + b_ref[...]

    return pl.pallas_call(body, name="ada_fwd",
                          out_shape=jax.ShapeDtypeStruct((c_all.shape[0], w_loc.shape[1]), F32),
                          compiler_params=_cp())(c_all, w_loc, b_loc)


def ada_bwd(c_all, dmod_all, dmod_cols):
    def body(c_ref, da_ref, dc_ref, gb_ref, gw_ref):
        gb_ref[...] = jnp.sum(da_ref[...], axis=0, keepdims=True)
        gw_ref[...] = _dg(_silu(c_ref[...]), dc_ref[...], 0, 0)

    return pl.pallas_call(body, name="ada_bwd",
                          out_shape=[jax.ShapeDtypeStruct((1, dmod_all.shape[1]), F32),
                                     jax.ShapeDtypeStruct((c_all.shape[1], dmod_cols.shape[1]), F32)],
                          compiler_params=_cp())(c_all, dmod_all, dmod_cols)


_FLIPS = [(0, 0, 1), (1, 0, 0), (0, 1, 0), (1, 1, 0), (1, 0, 1), (0, 1, 1), (1, 1, 1)]


def _exchange_ops(srcs, outs, sems, gather):
    n = len(srcs)
    send_sems, recv_sems, loc_sems = sems
    x, y, c = lax.axis_index("x"), lax.axis_index("y"), lax.axis_index("c")
    me = 4 * x + 2 * y + c
    peers = []
    for fx, fy, fc in _FLIPS:
        px, py, pc = (1 - x if fx else x), (1 - y if fy else y), (1 - c if fc else c)
        peers.append(((px, py, pc), 4 * px + 2 * py + pc))

    def copy(k, j, slot_src, slot_dst):
        src = srcs[k] if gather[k] else srcs[k].at[slot_src]
        return pltpu.make_async_remote_copy(src_ref=src, dst_ref=outs[k].at[slot_dst],
                                            send_sem=send_sems.at[k, j], recv_sem=recv_sems.at[k, j],
                                            device_id=peers[j][0], device_id_type=MESH)

    def local(k):
        own = srcs[k] if gather[k] else srcs[k].at[me]
        return pltpu.make_async_copy(own, outs[k].at[me], loc_sems.at[k])

    def start():
        for k in range(n):
            for j in range(N_DEV - 1):
                copy(k, j, peers[j][1], me).start()
            local(k).start()

    def wait():
        for k in range(n):
            for j in range(N_DEV - 1):
                copy(k, j, me, peers[j][1]).wait_recv()
        for k in range(n):
            for j in range(N_DEV - 1):
                copy(k, j, peers[j][1], me).wait_send()
            local(k).wait()

    return start, wait


def _gather_two_level(srcs, outs, sems):
    n = len(srcs)
    send_sems, recv_sems, loc_sems = sems
    x, y, c = lax.axis_index("x"), lax.axis_index("y"), lax.axis_index("c")
    slot = lambda px, py, pc: 4 * px + 2 * py + pc
    me, sibling = (x, y, c), (x, y, 1 - c)
    chips = [(1 - x, y), (x, 1 - y), (1 - x, 1 - y)]

    def copy(k, j, block, to, own=False):
        return pltpu.make_async_remote_copy(src_ref=srcs[k] if own else outs[k].at[slot(*block)],
                                            dst_ref=outs[k].at[slot(*block)],
                                            send_sem=send_sems.at[k, j], recv_sem=recv_sems.at[k, j],
                                            device_id=to, device_id_type=MESH)

    locs = [pltpu.make_async_copy(srcs[k], outs[k].at[slot(*me)], loc_sems.at[k]) for k in range(n)]
    for k in range(n):
        locs[k].start()
        copy(k, 0, me, sibling, own=True).start()
        for j, chip in enumerate(chips):
            copy(k, 1 + j, me, (*chip, c), own=True).start()
    for j, chip in enumerate(chips):
        for k in range(n):
            copy(k, 1 + j, (*chip, c), me).wait_recv()
            copy(k, 4 + j, (*chip, c), sibling).start()
    for k in range(n):
        copy(k, 0, sibling, me).wait_recv()
        for j, chip in enumerate(chips):
            copy(k, 4 + j, (*chip, 1 - c), me).wait_recv()
    for k in range(n):
        copy(k, 0, me, sibling, own=True).wait_send()
        for j, chip in enumerate(chips):
            copy(k, 1 + j, me, (*chip, c), own=True).wait_send()
            copy(k, 4 + j, (*chip, c), sibling).wait_send()
        locs[k].wait()


def gather_two_level(name, arrs):
    n = len(arrs)
    specs, shapes, sems = _exchange_parts(arrs, [True] * n)

    def body(*refs):
        _gather_two_level(refs[:n], refs[n:2 * n], refs[2 * n:])

    return pl.pallas_call(
        body, name=name, in_specs=specs, out_specs=specs, out_shape=shapes, scratch_shapes=sems,
        compiler_params=pltpu.CompilerParams(has_side_effects=True),
    )(*arrs)


def _exchange_parts(arrs, gather):
    n = len(arrs)
    any_spec = pl.BlockSpec(memory_space=pl.ANY)
    shapes = [jax.ShapeDtypeStruct(((N_DEV,) + a.shape) if g else a.shape, a.dtype) for a, g in zip(arrs, gather)]
    sems = [pltpu.SemaphoreType.DMA((n, N_DEV - 1)), pltpu.SemaphoreType.DMA((n, N_DEV - 1)),
            pltpu.SemaphoreType.DMA((n,))]
    return [any_spec] * n, shapes, sems


def exchange(name, arrs, gather):
    n = len(arrs)
    specs, shapes, sems = _exchange_parts(arrs, gather)

    def body(*refs):
        start, wait = _exchange_ops(refs[:n], refs[n:2 * n], refs[2 * n:], gather)
        start()
        wait()

    return pl.pallas_call(
        body, name=name, in_specs=specs, out_specs=specs, out_shape=shapes, scratch_shapes=sems,
        compiler_params=pltpu.CompilerParams(has_side_effects=True),
    )(*arrs)


def hosted_call(body, *, name, grid, in_specs, out_specs, out_shape, args, scratch=(), ride=None):
    sem = ("arbitrary",) * len(grid)
    if ride is None:
        res = pl.pallas_call(body, name=name, grid=grid, in_specs=in_specs, out_specs=out_specs, out_shape=out_shape,
                             scratch_shapes=list(scratch), compiler_params=_cp(*sem))(*args)
        return list(res), []
    arrs, gather = ride
    n, n_in, n_out, n_scr = len(arrs), len(in_specs), len(out_specs), len(scratch)
    specs, shapes, sems = _exchange_parts(arrs, gather)

    def both(*refs):
        ins, srcs = refs[:n_in], refs[n_in:n_in + n]
        outs, landed = refs[n_in + n:n_in + n + n_out], refs[n_in + n + n_out:n_in + 2 * n + n_out]
        scr, ex_sems = refs[n_in + 2 * n + n_out:n_in + 2 * n + n_out + n_scr], refs[n_in + 2 * n + n_out + n_scr:]
        start, wait = _exchange_ops(srcs, landed, ex_sems, gather)
        first = functools.reduce(lambda a, b: a & b, [pl.program_id(d) == 0 for d in range(len(grid))])
        last = functools.reduce(lambda a, b: a & b, [pl.program_id(d) == grid[d] - 1 for d in range(len(grid))])
        pl.when(first)(start)
        body(*ins, *outs, *scr)
        pl.when(last)(wait)

    res = pl.pallas_call(
        both, name=name, grid=grid, in_specs=list(in_specs) + specs, out_specs=list(out_specs) + specs,
        out_shape=list(out_shape) + shapes, scratch_shapes=list(scratch) + sems, compiler_params=_cp(*sem),
    )(*args, *arrs)
    return list(res[:n_out]), list(res[n_out:])


def adamw(name, g, w, m, v, tr, sel=None):
    slots = g.ndim >= 3
    r, c = w.shape
    c1, c2 = 1.0 - ADAM_B1 ** ADAM_STEP, 1.0 - ADAM_B2 ** ADAM_STEP

    def body(g_ref, w_ref, m_ref, v_ref, go, do, mo, vo):
        if slots:
            gg = g_ref[0].astype(F32)
            for j in range(1, N_DEV):
                gg = gg + g_ref[j].astype(F32)
        else:
            gg = g_ref[...]
        mn = ADAM_B1 * m_ref[...] + (1.0 - ADAM_B1) * gg
        vn = ADAM_B2 * v_ref[...] + (1.0 - ADAM_B2) * (gg * gg)
        go[...], mo[...], vo[...] = gg, mn, vn
        do[...] = -ADAM_LR * ((mn / c1) / (jnp.sqrt(vn / c2) + ADAM_EPS) + ADAM_WD * w_ref[...])

    blk = pl.BlockSpec((tr, c), lambda i: (i, 0))
    if g.ndim == 4:
        gspec = pl.BlockSpec((N_DEV, None, tr, c), lambda i: (0, sel, i, 0))
    else:
        gspec = pl.BlockSpec((N_DEV, tr, c), lambda i: (0, i, 0)) if slots else blk
    return pl.pallas_call(
        body, name=name, grid=(r // tr,), in_specs=[gspec, blk, blk, blk], out_specs=[blk] * 4,
        out_shape=[jax.ShapeDtypeStruct((r, c), F32)] * 4, compiler_params=_cp("parallel"),
    )(g, w, m, v)


def _lane_rows(n):
    return -(-n // (8 * LANE)) * 8


def _pack(arrs):
    pieces = []
    for a in arrs:
        n = math.prod(a.shape)
        flat = a.reshape(-1).astype(F32)
        pieces.append(jnp.pad(flat, (0, _lane_rows(n) * LANE - n)).reshape(_lane_rows(n), LANE))
    return jnp.concatenate(pieces, axis=0)


def _unpack(buf, shapes):
    out, off = [], 0
    for s in shapes:
        n = math.prod(s)
        out.append(buf[off:off + _lane_rows(n)].reshape(-1)[:n].reshape(s))
        off += _lane_rows(n)
    return out


def _cols_to_full(g):
    return jnp.transpose(g, (1, 0, 2)).reshape(g.shape[1], N_DEV * g.shape[2])


def _full_to_cols(w):
    r, c = w.shape
    return jnp.transpose(w.reshape(r, N_DEV, c // N_DEV), (1, 0, 2))


FF_CHUNK = D_FF // 2
DW_TOKENS = 1024
FFN_TM = 256


def _resident(shape):
    return pl.BlockSpec(shape, lambda i: (0,) * len(shape), pipeline_mode=pl.Buffered(1))


def _ffn_fwd(tag, x, sc, sh, g, w1, w3, w2, lg, lb, seq, tm, ride=None, target=None):
    t = x.shape[0]
    tm = min(FFN_TM, tm)
    tps = seq // tm
    ln = _res_ln(0.5)
    head = target is not None

    def body(x_ref, sc_ref, sh_ref, g_ref, lg_ref, lb_ref, w1_ref, w3_ref, w2_ref, *rest):
        if head:
            t_ref, y_ref, h_ref, a_ref, b_ref, f_ref, l_ref = rest
        else:
            y_ref, h_ref, a_ref, b_ref, f_ref = rest
        xv = x_ref[...]
        h = (xv * (1.0 + sc_ref[0]) + sh_ref[0]).astype(BF16)
        h_ref[...] = h
        acc = jnp.zeros((tm, D_MODEL), F32)
        for j in range(D_FF // FF_CHUNK):
            sl = slice(j * FF_CHUNK, (j + 1) * FF_CHUNK)
            a = _dg(h, w1_ref[:, sl], 1, 0)
            b = _dg(h, w3_ref[:, sl], 1, 0)
            a_ref[:, sl] = a
            b_ref[:, sl] = b
            acc = acc + _dg(_silu(a) * b, w2_ref[sl, :], 1, 0)
        f_ref[...] = acc
        y = ln(xv, acc, g_ref[0], lg_ref[...], lb_ref[...])[0]
        if head:
            @pl.when(pl.program_id(0) == 0)
            def _():
                l_ref[...] = jnp.zeros_like(l_ref)

            e = y - t_ref[...]
            y_ref[...] = e * (1.0 / D_MODEL)
            l_ref[...] += 0.5 * jnp.sum(jnp.mean(e * e, axis=-1, keepdims=True), axis=0, keepdims=True)
        else:
            y_ref[...] = y

    row = lambda c: pl.BlockSpec((tm, c), lambda i: (i, 0))
    per_seq = pl.BlockSpec((1, 1, D_MODEL), lambda i: (i // tps, 0, 0))
    vec = pl.BlockSpec((1, D_MODEL), lambda i: (0, 0))
    res, landed = hosted_call(
        body, name=tag + "_fwd", grid=(t // tm,),
        in_specs=[row(D_MODEL), per_seq, per_seq, per_seq, vec, vec,
                  _resident((D_MODEL, D_FF)), _resident((D_MODEL, D_FF)), _resident((D_FF, D_MODEL))]
        + ([row(D_MODEL)] if head else []),
        out_specs=[row(D_MODEL), row(D_MODEL), row(D_FF), row(D_FF), row(D_MODEL)]
        + ([pl.BlockSpec((1, 1), lambda i: (0, 0))] if head else []),
        out_shape=[jax.ShapeDtypeStruct((t, D_MODEL), F32), jax.ShapeDtypeStruct((t, D_MODEL), BF16),
                   jax.ShapeDtypeStruct((t, D_FF), F32), jax.ShapeDtypeStruct((t, D_FF), F32),
                   jax.ShapeDtypeStruct((t, D_MODEL), F32)] + ([jax.ShapeDtypeStruct((1, 1), F32)] if head else []),
        args=(x, sc, sh, g, lg, lb, w1, w3, w2) + ((target,) if head else ()), ride=ride)
    first = (res[0], res[5][0, 0]) if head else res[0]
    return first, tuple(res[1:5]), landed


def _ffn_bwd(tag, dy, x, sc, sh, g, w1, w3, w2, lg, lb, res, seq, tm, ride=None, chain=None):
    h, a, b, f = res
    t = x.shape[0]
    tmk = min(FFN_TM, tm)
    tps = seq // tmk
    ln = _res_ln(0.5)

    def body(dy_ref, x_ref, f_ref, a_ref, b_ref, sc_ref, sh_ref, g_ref, lg_ref, lb_ref, w1_ref, w3_ref, w2_ref,
             dx_ref, da_ref, db_ref, s_ref, df_ref, dsc_ref, dsh_ref, dg_ref, dlg_ref, dlb_ref):
        i = pl.program_id(0)

        @pl.when(i % tps == 0)
        def _():
            for r in (dsc_ref, dsh_ref, dg_ref):
                r[...] = jnp.zeros_like(r)

        @pl.when(i == 0)
        def _():
            dlg_ref[...] = jnp.zeros_like(dlg_ref)
            dlb_ref[...] = jnp.zeros_like(dlb_ref)

        xv = x_ref[...]
        _, pull = jax.vjp(ln, xv, f_ref[...], g_ref[0], lg_ref[...], lb_ref[...])
        dx_res, df, dg, dlg, dlb = pull((dy_ref[...],))
        dfb = df.astype(BF16)
        df_ref[...] = dfb
        dh = jnp.zeros((tmk, D_MODEL), F32)
        for j in range(D_FF // FF_CHUNK):
            sl = slice(j * FF_CHUNK, (j + 1) * FF_CHUNK)
            ds = _dg(dfb, w2_ref[sl, :], 1, 1)
            av, bv = a_ref[:, sl], b_ref[:, sl]
            sg = jax.nn.sigmoid(av)
            si = av * sg
            s_ref[:, sl] = (si * bv).astype(BF16)
            da = (ds * bv * (sg * (1.0 + av * (1.0 - sg)))).astype(BF16)
            db = (ds * si).astype(BF16)
            da_ref[:, sl] = da
            db_ref[:, sl] = db
            dh = dh + _dg(da, w1_ref[:, sl], 1, 1) + _dg(db, w3_ref[:, sl], 1, 1)
        dx_ref[...] = dx_res + dh * (1.0 + sc_ref[0])
        dsc_ref[0] += jnp.sum(dh * xv, axis=0, keepdims=True)
        dsh_ref[0] += jnp.sum(dh, axis=0, keepdims=True)
        dg_ref[0] += dg
        dlg_ref[...] += dlg
        dlb_ref[...] += dlb

    row = lambda c: pl.BlockSpec((tmk, c), lambda i: (i, 0))
    per_seq = pl.BlockSpec((1, 1, D_MODEL), lambda i: (i // tps, 0, 0))
    vec = pl.BlockSpec((1, D_MODEL), lambda i: (0, 0))
    seq_shape = jax.ShapeDtypeStruct(sc.shape, F32)
    vec_shape = jax.ShapeDtypeStruct((1, D_MODEL), F32)
    (dx, da, db, s, df, dsc, dsh, dg, dlg, dlb), landed = hosted_call(
        body, name=tag + "_bwd", grid=(t // tmk,),
        in_specs=[row(D_MODEL), row(D_MODEL), row(D_MODEL), row(D_FF), row(D_FF), per_seq, per_seq, per_seq, vec, vec,
                  _resident((D_MODEL, D_FF)), _resident((D_MODEL, D_FF)), _resident((D_FF, D_MODEL))],
        out_specs=[row(D_MODEL), row(D_FF), row(D_FF), row(D_FF), row(D_MODEL), per_seq, per_seq, per_seq, vec, vec],
        out_shape=[jax.ShapeDtypeStruct((t, D_MODEL), F32), jax.ShapeDtypeStruct((t, D_FF), BF16),
                   jax.ShapeDtypeStruct((t, D_FF), BF16), jax.ShapeDtypeStruct((t, D_FF), BF16),
                   jax.ShapeDtypeStruct((t, D_MODEL), BF16), seq_shape, seq_shape, seq_shape, vec_shape, vec_shape],
        args=(dy, x, f, a, b, sc, sh, g, lg, lb, w1, w3, w2), ride=ride)
    tt = min(DW_TOKENS, seq)
    if chain is None:
        dw2, landed = mm_tn(tag + "_dw2", s, df, D_FF // 2, D_MODEL, tt, BF16), []
    else:
        dw2, landed = mm_tn(tag + "_dw2", s, df, D_FF // 2, D_MODEL, tt, BF16, ride=chain((dsh, dsc, dg), dlg, dlb))
    dw1, (s_w2,) = mm_tn(tag + "_dw1", h, da, D_MODEL, D_FF // 2, tt, BF16,
                         ride=([dw2.reshape(N_DEV, D_FF // N_DEV, D_MODEL)], [False]))
    dw3, (s_w1,) = mm_tn(tag + "_dw3", h, db, D_MODEL, D_FF // 2, tt, BF16, ride=([_full_to_cols(dw1)], [False]))
    return dx, (dsh, dsc, dg), (s_w1, dw3, s_w2, dlg, dlb), landed


def kernel(x, c, w_ada, b_ada, ffn1_w1, ffn1_w3, ffn1_w2, ln1_g, ln1_b, w_in, conv_w, conv_b, dt_bias, a_log, d_ssd, ssd_norm_w, s5_a_re, s5_a_im, s5_log_dt, s5_b_re, s5_b_im, s5_c_re, s5_c_im, s5_d, w_glu, b_glu, w_out, ln2_g, ln2_b, ffn2_w1, ffn2_w3, ffn2_w2, ln3_g, ln3_b, loss_target, m_w_ada, m_b_ada, m_ffn1_w1, m_ffn1_w3, m_ffn1_w2, m_ln1_g, m_ln1_b, m_w_in, m_conv_w, m_conv_b, m_dt_bias, m_a_log, m_d_ssd, m_ssd_norm_w, m_s5_a_re, m_s5_a_im, m_s5_log_dt, m_s5_b_re, m_s5_b_im, m_s5_c_re, m_s5_c_im, m_s5_d, m_w_glu, m_b_glu, m_w_out, m_ln2_g, m_ln2_b, m_ffn2_w1, m_ffn2_w3, m_ffn2_w2, m_ln3_g, m_ln3_b, v_w_ada, v_b_ada, v_ffn1_w1, v_ffn1_w3, v_ffn1_w2, v_ln1_g, v_ln1_b, v_w_in, v_conv_w, v_conv_b, v_dt_bias, v_a_log, v_d_ssd, v_ssd_norm_w, v_s5_a_re, v_s5_a_im, v_s5_log_dt, v_s5_b_re, v_s5_b_im, v_s5_c_re, v_s5_c_im, v_s5_d, v_w_glu, v_b_glu, v_w_out, v_ln2_g, v_ln2_b, v_ffn2_w1, v_ffn2_w3, v_ffn2_w2, v_ln3_g, v_ln3_b):
    given = dict(locals())
    bsz, seq, _ = x.shape
    t = bsz * seq
    tm = min(512, seq)
    me = 4 * lax.axis_index("x") + 2 * lax.axis_index("y") + lax.axis_index("c")
    x0 = x.reshape(t, D_MODEL)
    target = loss_target.reshape(t, D_MODEL)

    g_col1, g_row1, g_c = gather_two_level(
        "gather_ffn1", [jnp.stack([ffn1_w1[0], ffn1_w3[0]]).astype(BF16), ffn1_w2[0].astype(BF16), c])
    f1w1, f1w3 = [_cols_to_full(g_col1[:, k]) for k in range(2)]
    f1w2 = g_row1.reshape(D_FF, D_MODEL)
    c_all = g_c.reshape(N_DEV * bsz, D_MODEL)

    n_loc = w_ada.shape[2]
    b_loc = lax.dynamic_slice(b_ada, (0, me * n_loc), (1, n_loc))
    mod_cols = ada_fwd(c_all, w_ada[0], b_loc)
    g_mod, = exchange("gather_mod", [mod_cols], [True])
    mine = lax.dynamic_slice(g_mod, (0, me * bsz, 0), (N_DEV, bsz, n_loc))
    mod = jnp.transpose(mine, (1, 0, 2)).reshape(bsz, N_MOD, 1, D_MODEL)
    sh1, sc1, g1, sh2, sc2, g2, sh3, sc3, g3 = [mod[:, k] for k in range(N_MOD)]

    x1, res1, (g_win, g_glu, g_out, g_conv, g_f2w1) = _ffn_fwd(
        "ffn1", x0, sc1, sh1, g1, f1w1, f1w3, f1w2, ln1_g, ln1_b, seq, tm,
        ride=([w_in[0].astype(BF16), w_glu[0].astype(BF16), w_out[0].astype(BF16), conv_w[0], ffn2_w1[0].astype(BF16)],
              [True] * 5))
    win = _cols_to_full(g_win)
    wglu = g_glu.reshape(S5_WIDTH, S5_WIDTH).astype(F32)
    wout = g_out.reshape(D_MODEL, D_MODEL)
    wo_ssd, wo_s5 = wout[:SSD_WIDTH], wout[SSD_WIDTH:]
    convw = jnp.transpose(g_conv, (1, 0, 2)).reshape(CONV_K, CONV_CH)
    w_z, w_xbc = win[:, :SSD_WIDTH], win[:, SSD_WIDTH:SSD_WIDTH + CONV_CH]
    w_dt = win[:, SSD_WIDTH + CONV_CH:SSD_WIDTH + CONV_CH + SSD_HEADS]
    w_u = win[:, SSD_WIDTH + CONV_CH + SSD_HEADS:]
    dt_pad = [jnp.pad(w_dt[:, HEADS_PER_GROUP * g:HEADS_PER_GROUP * (g + 1)], ((0, 0), (0, LANE - HEADS_PER_GROUP)))
              for g in range(SSD_GROUPS)]
    w_dtp = jnp.concatenate(dt_pad, axis=1)
    w_proj = jnp.concatenate([w_xbc, w_z, w_u, w_dtp], axis=1)

    h2, = rowwise_fwd("mix_mod", f_modulate, [x1], [sc2, sh2], [], [(D_MODEL, BF16)], seq, tm)
    proj = mm_nn("mix_proj", [h2], [w_proj], tm, P_COLS // 2)
    xc = conv_fwd(proj, convw, conv_b, seq, tm)
    dtb = jnp.pad(dt_bias.reshape(SSD_GROUPS, 1, HEADS_PER_GROUP), ((0, 0), (0, 0), (0, LANE - HEADS_PER_GROUP)))
    alog = jnp.pad(a_log.reshape(SSD_GROUPS, 1, HEADS_PER_GROUP), ((0, 0), (0, 0), (0, LANE - HEADS_PER_GROUP)))
    dcol = jnp.pad(d_ssd.reshape(SSD_GROUPS, HEADS_PER_GROUP, 1), ((0, 0), (0, LANE - HEADS_PER_GROUP), (0, 0)))
    nw = ssd_norm_w.reshape(SSD_GROUPS, 1, GROUP_COLS)
    (y_ssd, hprev), (g_f2w3,) = ssd_fwd(xc, proj, dtb, alog, dcol, nw, bsz, seq,
                                        ride=([ffn2_w3[0].astype(BF16)], [True]))

    a_re2, a_im2, ldt2 = s5_a_re[0], s5_a_im[0], s5_log_dt.reshape(S5_GROUPS, 1)
    ab_re, ab_im, f_re, f_im = _whole(_disc_a, "s5_disc_a", [a_re2, a_im2, ldt2], [(S5_GROUPS, S5_STATE)] * 4)
    b_re2, b_im2 = s5_b_re.reshape(S5_COLS, S5_GROUP_CH), s5_b_im.reshape(S5_COLS, S5_GROUP_CH)
    fr_col, fi_col = f_re.reshape(S5_COLS, 1), f_im.reshape(S5_COLS, 1)
    bb_re, bb_im = _whole(_disc_b, "s5_disc_b", [fr_col, fi_col, b_re2, b_im2], [(S5_COLS, S5_GROUP_CH)] * 2)
    wb_re, wb_im = _blockdiag_b(bb_re).astype(BF16), _blockdiag_b(bb_im).astype(BF16)
    wc_re, wc_im = _blockdiag_c(s5_c_re[0]).astype(BF16), _blockdiag_c(s5_c_im[0]).astype(BF16)
    dt5 = jnp.exp(ldt2)
    lam_re, lam_im = (dt5 * a_re2).reshape(1, S5_COLS), (dt5 * a_im2).reshape(1, S5_COLS)
    sf_re, sf_im, sb_re, sb_im, cf_re, cf_im, cb_re, cb_im = s5_tables(lam_re, lam_im)
    d5 = s5_d.reshape(S5_Q, 1, LANE)
    (y5, xr_all, xi_all), (g_f2w2,) = s5_fwd(
        proj, wb_re, wb_im, wc_re, wc_im, sf_re, sf_im, cf_re, cf_im, d5, bsz, seq,
        ride=([ffn2_w2[0].astype(BF16)], [True]))
    f2w1, f2w3, f2w2 = _cols_to_full(g_f2w1), _cols_to_full(g_f2w3), g_f2w2.reshape(D_FF, D_MODEL)
    o5, = rowwise_fwd("s5_glu", f_glu, [y5], [], [wglu, b_glu], [(S5_WIDTH, F32)], seq, tm)

    mix = mm_nn("mix_out", [y_ssd, o5], [wo_ssd, wo_s5], tm, D_MODEL)
    x2, = rowwise_fwd("mix_ln", _res_ln(1.0), [x1, mix], [g2], [ln2_g, ln2_b], [(D_MODEL, F32)], seq, tm)

    (dy, loss_loc), res3, _ = _ffn_fwd("ffn2", x2, sc3, sh3, g3, f2w1, f2w3, f2w2, ln3_g, ln3_b, seq, tm, target=target)

    dx2, dmod3, (s_f2w1, d_f2w3, s_f2w2, d_ln3g, d_ln3b), _ = _ffn_bwd(
        "ffn2", dy, x2, sc3, sh3, g3, f2w1, f2w3, f2w2, ln3_g, ln3_b, res3, seq, tm)

    (dx1_a, dmix), (dg2,), (d_ln2g, d_ln2b) = rowwise_bwd(
        "mix_ln_b", _res_ln(1.0), [x1, mix], [g2], [ln2_g, ln2_b], [dx2], seq, tm, [F32, BF16])
    tw = min(DW_TOKENS, seq)
    d_wo = jnp.concatenate([mm_tn("mix_dwo_ssd", y_ssd, dmix, SSD_WIDTH, D_MODEL, tw, BF16),
                            mm_tn("mix_dwo_s5", o5, dmix, S5_WIDTH, D_MODEL, tw, BF16)], axis=0)
    dy_mixed = mm_nt("mix_dy", [dmix], [wout], tm, D_MODEL)
    dy_ssd, do5 = dy_mixed, (dy_mixed, SSD_WIDTH, S5_WIDTH)

    (dy5,), _, (d_wglu, d_bglu) = rowwise_bwd("s5_glu_b", f_glu, [y5], [], [wglu, b_glu], [do5], seq, tm, [F32])
    (du, dwbr, dwbi, dwcr, dwci, dab_re, dab_im, dd5), (s_f2w3, s_out, s_glu) = s5_bwd(
        proj, wb_re, wb_im, wc_re, wc_im, sb_re, sb_im, cb_re, cb_im, d5, xr_all, xi_all, dy5, bsz, seq,
        ride=([_full_to_cols(d_f2w3), d_wo.reshape(N_DEV, D_MODEL // N_DEV, D_MODEL),
               d_wglu.reshape(N_DEV, S5_WIDTH // N_DEV, S5_WIDTH).astype(BF16)], [False] * 3))
    dbb_re, dbb_im = _unblock_b(dwbr), _unblock_b(dwbi)
    dfr_col, dfi_col, d_b_re, d_b_im = _whole_vjp(_disc_b, "s5_disc_b_b", [fr_col, fi_col, b_re2, b_im2],
                                                  [dbb_re, dbb_im])
    d_a_re, d_a_im, d_ldt = _whole_vjp(
        _disc_a, "s5_disc_a_b", [a_re2, a_im2, ldt2],
        [dab_re.reshape(S5_GROUPS, S5_STATE), dab_im.reshape(S5_GROUPS, S5_STATE),
         dfr_col.reshape(S5_GROUPS, S5_STATE), dfi_col.reshape(S5_GROUPS, S5_STATE)])
    d_c_re, d_c_im = _unblock_c(dwcr), _unblock_c(dwci)

    dxs, dbm, dcm, ddt, dz, ddtb, dalog, ddcol, dnw = ssd_bwd(xc, proj, dtb, alog, dcol, nw, hprev, dy_ssd, bsz, seq)
    dpre, d_convw, d_convb = conv_bwd_pre(proj, convw, conv_b, dxs, dbm, dcm, seq, tm)
    dxbc = conv_bwd_x(dpre, convw, seq, tm)

    dw_xbc = mm_tn("mix_dw_xbc", h2, dxbc, D_MODEL, CONV_CH, tw, BF16)
    dw_z = mm_tn("mix_dw_z", h2, dz, D_MODEL, SSD_WIDTH, tw, BF16)
    dw_u = mm_tn("mix_dw_u", h2, du, D_MODEL, S5_WIDTH, tw, BF16)
    dw_dt = mm_tn("mix_dw_dt", h2, ddt, D_MODEL, 2 * LANE, tw, BF16)
    dw_dt8 = jnp.concatenate([dw_dt[:, LANE * g:LANE * g + HEADS_PER_GROUP] for g in range(SSD_GROUPS)], axis=1)
    d_win = jnp.concatenate([dw_z, dw_xbc, dw_dt8, dw_u], axis=1)
    dh2, (s_win,) = mm_nt("mix_dh", [dxbc, dz, du, ddt], [w_xbc, w_z, w_u, w_dtp], tm, D_MODEL,
                          ride=([_full_to_cols(d_win)], [False]))
    (dx1,), (dsc2, dsh2), _ = rowwise_bwd("mix_mod_b", f_modulate, [x1], [sc2, sh2], [], [dh2], seq, tm, [F32],
                                          add_rows={0: dx1_a})

    packing = {}

    def small_and_dmod(dmod1, d_ln1g, d_ln1b):
        dmod = jnp.concatenate(list(dmod1) + [dsh2, dsc2, dg2] + list(dmod3), axis=1).reshape(bsz, N_MOD * D_MODEL)
        small = _small_grads(d_ln1g, d_ln1b)
        packing["names"] = list(small)
        packing["shapes"] = [small[k].shape for k in small]
        return [_pack(list(small.values())), dmod], [True, True]

    def _small_grads(d_ln1g, d_ln1b):
        return {
            "ln1_g": d_ln1g, "ln1_b": d_ln1b, "conv_w": d_convw, "conv_b": d_convb,
            "dt_bias": ddtb[:, 0, :HEADS_PER_GROUP].reshape(1, SSD_HEADS),
            "a_log": dalog[:, 0, :HEADS_PER_GROUP].reshape(1, SSD_HEADS),
            "d_ssd": ddcol[:, :HEADS_PER_GROUP, 0].reshape(1, SSD_HEADS),
            "ssd_norm_w": dnw.reshape(1, SSD_WIDTH),
            "s5_a_re": d_a_re[None], "s5_a_im": d_a_im[None], "s5_log_dt": d_ldt.reshape(1, S5_GROUPS),
            "s5_b_re": d_b_re.reshape(s5_b_re.shape), "s5_b_im": d_b_im.reshape(s5_b_im.shape),
            "s5_c_re": d_c_re[None], "s5_c_im": d_c_im[None], "s5_d": dd5.reshape(1, S5_WIDTH),
            "b_glu": d_bglu, "ln2_g": d_ln2g, "ln2_b": d_ln2b, "ln3_g": d_ln3g, "ln3_b": d_ln3b,
            "loss": loss_loc.reshape(1, 1),
        }

    dx0, _, (s_f1w1, d_f1w3, s_f1w2, _, _), (s_small, s_dmod) = _ffn_bwd(
        "ffn1", dx1, x0, sc1, sh1, g1, f1w1, f1w3, f1w2, ln1_g, ln1_b, res1, seq, tm, chain=small_and_dmod)
    names, shapes = packing["names"], packing["shapes"]
    s_f1w3, = exchange("sum_grads", [_full_to_cols(d_f1w3)], [False])

    out = {"grad_x": dx0.reshape(x.shape)}

    def put(name, res, shape):
        for key, val in zip(("grad_", "delta_", "new_m_", "new_v_"), res):
            out[key + name] = val.reshape(shape)

    for name, slots in (("ffn1_w1", s_f1w1), ("ffn1_w3", s_f1w3), ("ffn2_w1", s_f2w1), ("ffn2_w3", s_f2w3)):
        w = given[name]
        put(name, adamw("adam_" + name, slots, w[0], given["m_" + name][0], given["v_" + name][0], 256), w.shape)
    for name, slots in (("ffn1_w2", s_f1w2), ("ffn2_w2", s_f2w2)):
        w = given[name]
        put(name, adamw("adam_" + name, slots, w[0], given["m_" + name][0], given["v_" + name][0], 176), w.shape)
    put("w_in", adamw("adam_w_in", s_win, w_in[0], m_w_in[0], v_w_in[0], 256), w_in.shape)
    put("w_glu", adamw("adam_w_glu", s_glu, w_glu[0], m_w_glu[0], v_w_glu[0], 64), w_glu.shape)
    put("w_out", adamw("adam_w_out", s_out, w_out[0], m_w_out[0], v_w_out[0], 128), w_out.shape)

    dmod_all = s_dmod.reshape(N_DEV * bsz, N_MOD * D_MODEL)
    g_bada, g_wada = ada_bwd(c_all, dmod_all, lax.dynamic_slice(dmod_all, (0, me * n_loc), (N_DEV * bsz, n_loc)))
    put("w_ada", adamw("adam_w_ada", g_wada, w_ada[0], m_w_ada[0], v_w_ada[0], 256), w_ada.shape)
    put("b_ada", adamw("adam_b_ada", g_bada, b_ada, m_b_ada, v_b_ada, 1), b_ada.shape)

    not_params = {"conv_w": jnp.zeros((CONV_K, CONV_CH), F32), "loss": jnp.zeros((1, 1), F32)}
    pw, pm, pv = [_pack([not_params[k] if k in not_params else given[pre + k] for k in names]) for pre in ("", "m_", "v_")]
    res_small = adamw("adam_small", s_small, pw, pm, pv, pw.shape[0])
    parts = [_unpack(r, shapes) for r in res_small]
    for i, k in enumerate(names):
        if k not in not_params:
            put(k, [p[i] for p in parts], given[k].shape)
    out["loss"] = parts[0][names.index("loss")][0, 0]
    g_cw = lax.dynamic_slice(parts[0][names.index("conv_w")], (0, me * LANE), (CONV_K, LANE))
    put("conv_w", adamw("adam_conv_w", g_cw, conv_w[0], m_conv_w[0], v_conv_w[0], CONV_K), conv_w.shape)

    order = ["w_ada", "b_ada", "ffn1_w1", "ffn1_w3", "ffn1_w2", "ln1_g", "ln1_b", "w_in", "conv_w", "conv_b", "dt_bias",
             "a_log", "d_ssd", "ssd_norm_w", "s5_a_re", "s5_a_im", "s5_log_dt", "s5_b_re", "s5_b_im", "s5_c_re",
             "s5_c_im", "s5_d", "w_glu", "b_glu", "w_out", "ln2_g", "ln2_b", "ffn2_w1", "ffn2_w3", "ffn2_w2", "ln3_g",
             "ln3_b"]
    return (out["loss"], out["grad_x"], *[out[p + n] for p in ("grad_", "delta_", "new_m_", "new_v_") for n in order])
```

```python
import functools
import math

import jax
import jax.numpy as jnp
from jax import lax
from jax.experimental import pallas as pl
from jax.experimental.pallas import tpu as pltpu

F32 = jnp.float32
BF16 = jnp.bfloat16
HI = lax.Precision.HIGHEST
MESH = pl.DeviceIdType.MESH

N_DEV = 8
D_MODEL = 1024
D_FF = 2816
N_MOD = 9
SSD_WIDTH = 512
SSD_HEADS = 8
SSD_HEAD_DIM = 64
SSD_GROUPS = 2
SSD_STATE = 128
SSD_CHUNK = 128
GROUP_COLS = SSD_WIDTH // SSD_GROUPS
HEADS_PER_GROUP = SSD_HEADS // SSD_GROUPS
CONV_K = 4
CONV_CH = 1024
S5_WIDTH = 512
S5_GROUPS = 32
S5_GROUP_CH = 16
S5_STATE = 64
S5_COLS = S5_GROUPS * S5_STATE
S5_Q = 4
S5_CHUNK = 512
ALPHA = 2.0 ** 0.25
LN_EPS = 1e-5
LANE = 128
HALO = 8

P_XBC, P_Z, P_U, P_DT = 0, 1024, 1536, 2048
P_COLS = 2048 + SSD_GROUPS * LANE
IN_COLS = SSD_WIDTH + CONV_CH + SSD_HEADS + S5_WIDTH

ADAM_LR, ADAM_B1, ADAM_B2, ADAM_EPS, ADAM_WD, ADAM_STEP = 0.001, 0.9, 0.999, 1e-08, 0.01, 10

VMEM_LIMIT = 56 * 1024 * 1024


def _cp(*sem):
    return pltpu.CompilerParams(dimension_semantics=sem if sem else None, vmem_limit_bytes=VMEM_LIMIT)


def _dg(a, b, ca, cb):
    return lax.dot_general(a.astype(BF16), b.astype(BF16), (((ca,), (cb,)), ((), ())), preferred_element_type=F32)


@jax.custom_vjp
def bdot_nn(a, b):
    return _dg(a, b, 1, 0)


bdot_nn.defvjp(lambda a, b: (_dg(a, b, 1, 0), (a, b)),
               lambda r, g: (_dg(g, r[1], 1, 1), _dg(r[0], g, 0, 0)))


@jax.custom_vjp
def bdot_nt(a, b):
    return _dg(a, b, 1, 1)


bdot_nt.defvjp(lambda a, b: (_dg(a, b, 1, 1), (a, b)),
               lambda r, g: (_dg(g, r[1], 1, 0), _dg(g, r[0], 0, 0)))


@jax.custom_vjp
def bdot_tn(a, b):
    return _dg(a, b, 0, 0)


bdot_tn.defvjp(lambda a, b: (_dg(a, b, 0, 0), (a, b)),
               lambda r, g: (_dg(r[1], g, 1, 1), _dg(r[0], g, 1, 0)))


def _take_col(z):
    @jax.custom_vjp
    def take(x):
        return x[:, z:z + 1]

    def bwd(shape, g):
        hot = (lax.broadcasted_iota(jnp.int32, (1, shape[1]), 1) == z).astype(F32)
        return (g * hot,)

    take.defvjp(lambda x: (x[:, z:z + 1], x.shape), bwd)
    return take


def _take_row(z):
    @jax.custom_vjp
    def take(x):
        return x[z:z + 1, :]

    def bwd(shape, g):
        hot = (lax.broadcasted_iota(jnp.int32, (shape[0], 1), 0) == z).astype(F32)
        return (hot * g,)

    take.defvjp(lambda x: (x[z:z + 1, :], x.shape), bwd)
    return take


def _view(a):
    return a if isinstance(a, tuple) else (a, 0, a.shape[1])


def _col_spec(view, rows, width, index):
    _, off, _ = view
    assert off % width == 0
    return pl.BlockSpec((rows, width), lambda *g: (index(*g)[0], off // width + index(*g)[1]))


def _rw_in_specs(rows, bps, gps, tm, tps):
    specs = [_col_spec(_view(r), tm, _view(r)[2], lambda i: (i, 0)) for r in rows]
    specs += [pl.BlockSpec((1, 1, b.shape[2]), lambda i: (i // tps, 0, 0)) for b in bps]
    specs += [pl.BlockSpec(g.shape, lambda i, nd=g.ndim: (0,) * nd) for g in gps]
    return specs


def _rw_vals(refs, nr, nb, ng):
    vals = [r[...] for r in refs[:nr]]
    vals += [b[0] for b in refs[nr:nr + nb]]
    vals += [g[...] for g in refs[nr + nb:nr + nb + ng]]
    return vals


def rowwise_fwd(name, f, rows, bps, gps, outs, seq, tm):
    t = _view(rows[0])[0].shape[0]
    tps = seq // tm
    nr, nb, ng = len(rows), len(bps), len(gps)

    def body(*refs):
        res = f(*_rw_vals(refs, nr, nb, ng))
        for o, v in zip(refs[nr + nb + ng:], res):
            o[...] = v.astype(o.dtype)

    return pl.pallas_call(
        body, name=name, grid=(t // tm,),
        in_specs=_rw_in_specs(rows, bps, gps, tm, tps),
        out_specs=[pl.BlockSpec((tm, c), lambda i: (i, 0)) for c, _ in outs],
        out_shape=[jax.ShapeDtypeStruct((t, c), d) for c, d in outs],
        compiler_params=_cp("arbitrary"),
    )(*[_view(r)[0] for r in rows], *bps, *gps)


def rowwise_bwd(name, f, rows, bps, gps, douts, seq, tm, row_grads, add_rows=None):
    add_rows = add_rows or {}
    t = _view(rows[0])[0].shape[0]
    tps = seq // tm
    nr, nb, ng, nd = len(rows), len(bps), len(gps), len(douts)
    want = [k for k in range(nr) if row_grads[k] is not None]
    adds = sorted(add_rows)
    n_in = nr + nb + ng + nd + len(adds)

    def body(*refs):
        vals = _rw_vals(refs, nr, nb, ng)
        dvals = tuple(r[...] for r in refs[nr + nb + ng:nr + nb + ng + nd])
        add_refs = dict(zip(adds, refs[nr + nb + ng + nd:n_in]))
        out_refs = refs[n_in:]
        _, pull = jax.vjp(f, *vals)
        grads = pull(dvals)
        i = pl.program_id(0)
        for o, k in zip(out_refs, want):
            g = grads[k]
            if k in add_refs:
                g = g + add_refs[k][...]
            o[...] = g.astype(o.dtype)
        for j in range(nb):
            o = out_refs[len(want) + j]

            @pl.when(i % tps == 0)
            def _(o=o):
                o[...] = jnp.zeros_like(o)

            o[0] = o[0] + grads[nr + j]
        for j in range(ng):
            o = out_refs[len(want) + nb + j]

            @pl.when(i == 0)
            def _(o=o):
                o[...] = jnp.zeros_like(o)

            o[...] = o[...] + grads[nr + nb + j]

    in_specs = _rw_in_specs(rows, bps, gps, tm, tps)
    in_specs += [_col_spec(_view(d), tm, _view(d)[2], lambda i: (i, 0)) for d in douts]
    in_specs += [pl.BlockSpec((tm, add_rows[k].shape[1]), lambda i: (i, 0)) for k in adds]
    out_specs = [pl.BlockSpec((tm, _view(rows[k])[2]), lambda i: (i, 0)) for k in want]
    out_shape = [jax.ShapeDtypeStruct((t, _view(rows[k])[2]), row_grads[k]) for k in want]
    out_specs += [pl.BlockSpec((1, 1, b.shape[2]), lambda i: (i // tps, 0, 0)) for b in bps]
    out_shape += [jax.ShapeDtypeStruct(b.shape, F32) for b in bps]
    out_specs += [pl.BlockSpec(g.shape, lambda i, n=g.ndim: (0,) * n) for g in gps]
    out_shape += [jax.ShapeDtypeStruct(g.shape, F32) for g in gps]
    res = pl.pallas_call(
        body, name=name, grid=(t // tm,), in_specs=in_specs, out_specs=out_specs, out_shape=out_shape,
        compiler_params=_cp("arbitrary"),
    )(*[_view(r)[0] for r in rows], *bps, *gps, *[_view(d)[0] for d in douts], *[add_rows[k] for k in adds])
    nw = len(want)
    return res[:nw], res[nw:nw + nb], res[nw + nb:]


def mm_nn(name, xs, ws, tm, tn, out_dtype=F32, ride=None):
    views = [_view(x) for x in xs]
    t, n, k = views[0][0].shape[0], ws[0].shape[1], len(xs)

    def body(*refs):
        acc = _dg(refs[0][...], refs[k][...], 1, 0)
        for i in range(1, k):
            acc = acc + _dg(refs[i][...], refs[k + i][...], 1, 0)
        refs[2 * k][...] = acc.astype(out_dtype)

    in_specs = [_col_spec(v, tm, v[2], lambda i, j: (i, 0)) for v in views]
    in_specs += [pl.BlockSpec((w.shape[0], tn), lambda i, j: (0, j)) for w in ws]
    out_spec = pl.BlockSpec((tm, tn), lambda i, j: (i, j))
    out_shape = jax.ShapeDtypeStruct((t, n), out_dtype)
    if ride is not None:
        (res,), landed = hosted_call(body, name=name, grid=(t // tm, n // tn), in_specs=in_specs, out_specs=[out_spec],
                                     out_shape=[out_shape], args=(*[v[0] for v in views], *ws), ride=ride)
        return res, landed
    return pl.pallas_call(
        body, name=name, grid=(t // tm, n // tn), in_specs=in_specs, out_specs=out_spec, out_shape=out_shape,
        compiler_params=_cp("parallel", "parallel"),
    )(*[v[0] for v in views], *ws)


def mm_nt(name, dys, ws, tm, tk, out_dtype=F32, ride=None):
    views = [_view(d) for d in dys]
    t, kk, k = views[0][0].shape[0], ws[0].shape[0], len(dys)

    def body(*refs):
        acc = _dg(refs[0][...], refs[k][...], 1, 1)
        for i in range(1, k):
            acc = acc + _dg(refs[i][...], refs[k + i][...], 1, 1)
        refs[2 * k][...] = acc.astype(out_dtype)

    in_specs = [_col_spec(v, tm, v[2], lambda i, j: (i, 0)) for v in views]
    in_specs += [pl.BlockSpec((tk, w.shape[1]), lambda i, j: (j, 0)) for w in ws]
    out_spec = pl.BlockSpec((tm, tk), lambda i, j: (i, j))
    out_shape = jax.ShapeDtypeStruct((t, kk), out_dtype)
    if ride is not None:
        (res,), landed = hosted_call(body, name=name, grid=(t // tm, kk // tk), in_specs=in_specs, out_specs=[out_spec],
                                     out_shape=[out_shape], args=(*[v[0] for v in views], *ws), ride=ride)
        return res, landed
    return pl.pallas_call(
        body, name=name, grid=(t // tm, kk // tk), in_specs=in_specs, out_specs=out_spec, out_shape=out_shape,
        compiler_params=_cp("parallel", "parallel"),
    )(*[v[0] for v in views], *ws)


def mm_tn(name, x, dy, tk, tn, tt, out_dtype=F32, ride=None):
    xv, dv = _view(x), _view(dy)
    t, kk, n = xv[0].shape[0], xv[2], dv[2]
    steps = t // tt

    def body(x_ref, d_ref, o_ref, acc_ref):
        @pl.when(pl.program_id(2) == 0)
        def _():
            acc_ref[...] = jnp.zeros_like(acc_ref)

        acc_ref[...] += _dg(x_ref[...], d_ref[...], 0, 0)

        @pl.when(pl.program_id(2) == steps - 1)
        def _():
            o_ref[...] = acc_ref[...].astype(out_dtype)

    in_specs = [_col_spec(xv, tt, tk, lambda a, b, c: (c, a)), _col_spec(dv, tt, tn, lambda a, b, c: (c, b))]
    out_spec = pl.BlockSpec((tk, tn), lambda a, b, c: (a, b))
    out_shape = jax.ShapeDtypeStruct((kk, n), out_dtype)
    if ride is not None:
        (res,), landed = hosted_call(body, name=name, grid=(kk // tk, n // tn, steps), in_specs=in_specs,
                                     out_specs=[out_spec], out_shape=[out_shape], scratch=[pltpu.VMEM((tk, tn), F32)],
                                     args=(xv[0], dv[0]), ride=ride)
        return res, landed
    return pl.pallas_call(
        body, name=name, grid=(kk // tk, n // tn, steps), in_specs=in_specs, out_specs=out_spec, out_shape=out_shape,
        scratch_shapes=[pltpu.VMEM((tk, tn), F32)],
        compiler_params=_cp("parallel", "parallel", "arbitrary"),
    )(xv[0], dv[0])


def _silu(x):
    return x * jax.nn.sigmoid(x)


def f_modulate(x, sc, sh):
    return (x * (1.0 + sc) + sh,)


def _res_ln(coef):
    def f(x, y, g, lg, lb):
        r = ALPHA * x + (coef * g) * y
        mu = jnp.mean(r, axis=-1, keepdims=True)
        d = r - mu
        var = jnp.mean(d * d, axis=-1, keepdims=True)
        return (d * lax.rsqrt(var + LN_EPS) * lg + lb,)
    return f


def f_glu(y, w, b):
    g = jax.nn.gelu(y)
    return (g * jax.nn.sigmoid(bdot_nn(g, w) + b),)


def _shift_down(x, halo, k):
    if k == 0:
        return x
    r = pltpu.roll(x, k, 0)
    hr = pltpu.roll(halo, k, 0)
    row = lax.broadcasted_iota(jnp.int32, (HALO, 1), 0)
    top = jnp.where(row < k, hr, r[:HALO])
    return jnp.concatenate([top, r[HALO:]], axis=0)


def _shift_up(x, halo, k):
    if k == 0:
        return x
    n = x.shape[0]
    r = pltpu.roll(x, n - k, 0)
    hr = pltpu.roll(halo, HALO - k, 0)
    row = lax.broadcasted_iota(jnp.int32, (HALO, 1), 0)
    bot = jnp.where(row >= HALO - k, hr, r[n - HALO:])
    return jnp.concatenate([r[:n - HALO], bot], axis=0)


def _conv_pre(x, halo, w, b):
    acc = x * w[CONV_K - 1:CONV_K, :] + b
    for k in range(1, CONV_K):
        acc = acc + _shift_down(x, halo, k) * w[CONV_K - 1 - k:CONV_K - k, :]
    return acc


def _rows_before(width, tm):
    return pl.BlockSpec((HALO, width), lambda i: (jnp.maximum(i * (tm // HALO) - 1, 0), 0))


def conv_fwd(proj, w, b, seq, tm):
    t = proj.shape[0]
    tps = seq // tm

    def body(x_ref, h_ref, w_ref, b_ref, o_ref):
        first = (pl.program_id(0) % tps == 0)
        halo = jnp.where(first, 0.0, h_ref[...])
        o_ref[...] = _silu(_conv_pre(x_ref[...], halo, w_ref[...], b_ref[...]))

    return pl.pallas_call(
        body, name="conv_fwd", grid=(t // tm,),
        in_specs=[pl.BlockSpec((tm, CONV_CH), lambda i: (i, 0)), _rows_before(CONV_CH, tm),
                  pl.BlockSpec((CONV_K, CONV_CH), lambda i: (0, 0)), pl.BlockSpec((1, CONV_CH), lambda i: (0, 0))],
        out_specs=pl.BlockSpec((tm, CONV_CH), lambda i: (i, 0)),
        out_shape=jax.ShapeDtypeStruct((t, CONV_CH), F32),
        compiler_params=_cp("arbitrary"),
    )(proj, proj, w, b)


def conv_bwd_pre(proj, w, b, dxs, dbm, dcm, seq, tm):
    t = proj.shape[0]
    tps = seq // tm

    def body(x_ref, h_ref, w_ref, b_ref, d1, d2, d3, dp_ref, dw_ref, db_ref):
        i = pl.program_id(0)
        halo = jnp.where(i % tps == 0, 0.0, h_ref[...])
        x = x_ref[...]
        pre = _conv_pre(x, halo, w_ref[...], b_ref[...])
        sg = jax.nn.sigmoid(pre)
        dout = jnp.concatenate([d1[...], d2[...], d3[...]], axis=1)
        dp = dout * (sg * (1.0 + pre * (1.0 - sg)))
        dp_ref[...] = dp

        @pl.when(i == 0)
        def _():
            dw_ref[...] = jnp.zeros_like(dw_ref)
            db_ref[...] = jnp.zeros_like(db_ref)

        db_ref[...] += jnp.sum(dp, axis=0, keepdims=True)
        for k in range(CONV_K):
            j = CONV_K - 1 - k
            dw_ref[j:j + 1, :] += jnp.sum(dp * _shift_down(x, halo, k), axis=0, keepdims=True)

    return pl.pallas_call(
        body, name="conv_bwd_pre", grid=(t // tm,),
        in_specs=[pl.BlockSpec((tm, CONV_CH), lambda i: (i, 0)), _rows_before(CONV_CH, tm),
                  pl.BlockSpec((CONV_K, CONV_CH), lambda i: (0, 0)), pl.BlockSpec((1, CONV_CH), lambda i: (0, 0)),
                  pl.BlockSpec((tm, 512), lambda i: (i, 0)), pl.BlockSpec((tm, 256), lambda i: (i, 0)),
                  pl.BlockSpec((tm, 256), lambda i: (i, 0))],
        out_specs=[pl.BlockSpec((tm, CONV_CH), lambda i: (i, 0)), pl.BlockSpec((CONV_K, CONV_CH), lambda i: (0, 0)),
                   pl.BlockSpec((1, CONV_CH), lambda i: (0, 0))],
        out_shape=[jax.ShapeDtypeStruct((t, CONV_CH), F32), jax.ShapeDtypeStruct((CONV_K, CONV_CH), F32),
                   jax.ShapeDtypeStruct((1, CONV_CH), F32)],
        compiler_params=_cp("arbitrary"),
    )(proj, proj, w, b, dxs, dbm, dcm)


def conv_bwd_x(dpre, w, seq, tm):
    t = dpre.shape[0]
    tps = seq // tm
    blocks = tm // HALO
    last = t // HALO - 1

    def body(d_ref, h_ref, w_ref, o_ref):
        halo = jnp.where(pl.program_id(0) % tps == tps - 1, 0.0, h_ref[...])
        d = d_ref[...]
        w = w_ref[...]
        acc = d * w[CONV_K - 1:CONV_K, :]
        for k in range(1, CONV_K):
            acc = acc + _shift_up(d, halo, k) * w[CONV_K - 1 - k:CONV_K - k, :]
        o_ref[...] = acc

    return pl.pallas_call(
        body, name="conv_bwd_x", grid=(t // tm,),
        in_specs=[pl.BlockSpec((tm, CONV_CH), lambda i: (i, 0)),
                  pl.BlockSpec((HALO, CONV_CH), lambda i: (jnp.minimum((i + 1) * blocks, last), 0)),
                  pl.BlockSpec((CONV_K, CONV_CH), lambda i: (0, 0))],
        out_specs=pl.BlockSpec((tm, CONV_CH), lambda i: (i, 0)),
        out_shape=jax.ShapeDtypeStruct((t, CONV_CH), F32),
        compiler_params=_cp("arbitrary"),
    )(dpre, dpre, w)


def _softplus(x):
    return jnp.maximum(x, 0.0) + jnp.log1p(jnp.exp(-jnp.abs(x)))


def _ssd_chunk(xs, bg, cg, dtr, zz, hp, dtb, alog, dcol, nw):
    l = xs.shape[0]
    row = lax.broadcasted_iota(jnp.int32, (l, l), 0)
    col = lax.broadcasted_iota(jnp.int32, (l, l), 1)
    causal = row >= col
    tril = causal.astype(F32)
    expand = (lax.broadcasted_iota(jnp.int32, (LANE, GROUP_COLS), 1) // SSD_HEAD_DIM
              == lax.broadcasted_iota(jnp.int32, (LANE, GROUP_COLS), 0)).astype(F32)
    head_of_col = lax.broadcasted_iota(jnp.int32, (1, GROUP_COLS), 1) // SSD_HEAD_DIM
    last_row = (lax.broadcasted_iota(jnp.int32, (l, 1), 0) == l - 1).astype(F32)

    dtc = _softplus(dtr + dtb)
    a_c = dtc * (-jnp.exp(alog))
    acs_c = jnp.dot(tril, a_c, precision=HI, preferred_element_type=F32)
    dt_e = jnp.dot(dtc, expand, precision=HI, preferred_element_type=F32)
    acs_e = jnp.dot(acs_c, expand, precision=HI, preferred_element_type=F32)
    alast_e = jnp.sum(acs_e * last_row, axis=0, keepdims=True)
    x = xs * dt_e
    states = bdot_tn(bg, x * jnp.exp(alast_e - acs_e))
    h_next = jnp.exp(alast_e) * hp + states
    d_e = jnp.sum(dcol * expand, axis=0, keepdims=True)
    y = bdot_nn(cg, hp) * jnp.exp(acs_e) + d_e * xs
    cb = bdot_nt(cg, bg)
    acs_t = acs_c.T
    for z in range(HEADS_PER_GROUP):
        seg = _take_col(z)(acs_c) - _take_row(z)(acs_t)
        lmat = jnp.exp(jnp.where(causal, seg, -1e30))
        y = y + bdot_nn(cb * lmat, x * (head_of_col == z).astype(F32))
    yz = y * _silu(zz)
    ms = jnp.mean(yz * yz, axis=-1, keepdims=True)
    return yz * lax.rsqrt(ms + LN_EPS) * nw, h_next


SSD_SUB = 2
SSD_ROWS = SSD_SUB * SSD_CHUNK


def _ssd_in_specs(steps, rev):
    def tok(b, c):
        return b * steps + (steps - 1 - c if rev else c)

    whole = lambda *shape: pl.BlockSpec(shape, lambda b, c: (0,) * len(shape))
    both = SSD_GROUPS * SSD_STATE
    return [
        pl.BlockSpec((SSD_ROWS, SSD_WIDTH), lambda b, c: (tok(b, c), 0)),
        pl.BlockSpec((SSD_ROWS, both), lambda b, c: (tok(b, c), SSD_WIDTH // both)),
        pl.BlockSpec((SSD_ROWS, both), lambda b, c: (tok(b, c), SSD_WIDTH // both + 1)),
        pl.BlockSpec((SSD_ROWS, SSD_GROUPS * LANE), lambda b, c: (tok(b, c), P_DT // (SSD_GROUPS * LANE))),
        pl.BlockSpec((SSD_ROWS, SSD_WIDTH), lambda b, c: (tok(b, c), P_Z // SSD_WIDTH)),
        whole(SSD_GROUPS, 1, LANE), whole(SSD_GROUPS, 1, LANE), whole(SSD_GROUPS, LANE, 1),
        whole(SSD_GROUPS, 1, GROUP_COLS),
    ], tok


def _piece(ref, s, g, width):
    return ref[s * SSD_CHUNK:(s + 1) * SSD_CHUNK, g * width:(g + 1) * width]


def ssd_fwd(xc, proj, dtb, alog, dcol, nw, bsz, seq, ride=None):
    t = xc.shape[0]
    nc = seq // SSD_CHUNK
    steps = nc // SSD_SUB
    in_specs, tok = _ssd_in_specs(steps, False)

    def body(xs, bm, cm, dtr, zz, dtb_r, alog_r, dcol_r, nw_r, y_ref, hp_ref, h_scr):
        @pl.when(pl.program_id(1) == 0)
        def _():
            h_scr[...] = jnp.zeros_like(h_scr)

        for g in range(SSD_GROUPS):
            h = h_scr[g]
            for s in range(SSD_SUB):
                hp_ref[g, 0, s] = h
                y, h = _ssd_chunk(_piece(xs, s, g, GROUP_COLS), _piece(bm, s, g, SSD_STATE),
                                  _piece(cm, s, g, SSD_STATE), _piece(dtr, s, g, LANE), _piece(zz, s, g, GROUP_COLS), h,
                                  dtb_r[g], alog_r[g], dcol_r[g], nw_r[g])
                y_ref[s * SSD_CHUNK:(s + 1) * SSD_CHUNK, g * GROUP_COLS:(g + 1) * GROUP_COLS] = y
            h_scr[g] = h

    return hosted_call(
        body, name="ssd_fwd", grid=(bsz, steps), in_specs=in_specs,
        out_specs=[pl.BlockSpec((SSD_ROWS, SSD_WIDTH), lambda b, c: (tok(b, c), 0)),
                   pl.BlockSpec((SSD_GROUPS, 1, SSD_SUB, SSD_STATE, GROUP_COLS), lambda b, c: (0, b, c, 0, 0))],
        out_shape=[jax.ShapeDtypeStruct((t, SSD_WIDTH), F32),
                   jax.ShapeDtypeStruct((SSD_GROUPS, bsz, nc, SSD_STATE, GROUP_COLS), F32)],
        scratch=[pltpu.VMEM((SSD_GROUPS, SSD_STATE, GROUP_COLS), F32)],
        args=(xc, xc, xc, proj, proj, dtb, alog, dcol, nw), ride=ride)


def ssd_bwd(xc, proj, dtb, alog, dcol, nw, hprev, dy, bsz, seq):
    t = xc.shape[0]
    nc = seq // SSD_CHUNK
    steps = nc // SSD_SUB
    in_specs, tok = _ssd_in_specs(steps, True)
    in_specs += [pl.BlockSpec((SSD_GROUPS, 1, SSD_SUB, SSD_STATE, GROUP_COLS), lambda b, c: (0, b, steps - 1 - c, 0, 0)),
                 pl.BlockSpec((SSD_ROWS, SSD_WIDTH), lambda b, c: (tok(b, c), 0))]

    def body(xs, bm, cm, dtr, zz, dtb_r, alog_r, dcol_r, nw_r, hp_ref, dy_ref,
             dxs, dbm, dcm, ddt, dzz, ddtb, dalog, ddcol, dnw, dh_scr):
        b, c = pl.program_id(0), pl.program_id(1)

        @pl.when(c == 0)
        def _():
            dh_scr[...] = jnp.zeros_like(dh_scr)

        @pl.when((b == 0) & (c == 0))
        def _():
            for r in (ddtb, dalog, ddcol, dnw):
                r[...] = jnp.zeros_like(r)

        for g in range(SSD_GROUPS):
            wide = slice(g * GROUP_COLS, (g + 1) * GROUP_COLS)
            state = slice(g * SSD_STATE, (g + 1) * SSD_STATE)
            dh = dh_scr[g]
            for s in reversed(range(SSD_SUB)):
                rows = slice(s * SSD_CHUNK, (s + 1) * SSD_CHUNK)
                _, pull = jax.vjp(_ssd_chunk, xs[rows, wide], bm[rows, state], cm[rows, state],
                                  _piece(dtr, s, g, LANE), zz[rows, wide], hp_ref[g, 0, s],
                                  dtb_r[g], alog_r[g], dcol_r[g], nw_r[g])
                d = pull((dy_ref[rows, wide], dh))
                dxs[rows, wide], dbm[rows, state], dcm[rows, state], dzz[rows, wide] = d[0], d[1], d[2], d[4]
                ddt[rows, g * LANE:(g + 1) * LANE] = d[3]
                dh = d[5]
                ddtb[g] += d[6]
                dalog[g] += d[7]
                ddcol[g] += d[8]
                dnw[g] += d[9]
            dh_scr[g] = dh

    def tile(w):
        return pl.BlockSpec((SSD_ROWS, w), lambda b, c: (tok(b, c), 0))

    whole = lambda *shape: pl.BlockSpec(shape, lambda b, c: (0,) * len(shape))
    return pl.pallas_call(
        body, name="ssd_bwd", grid=(bsz, steps), in_specs=in_specs,
        out_specs=[tile(SSD_WIDTH), tile(2 * SSD_STATE), tile(2 * SSD_STATE), tile(2 * LANE), tile(SSD_WIDTH),
                   whole(SSD_GROUPS, 1, LANE), whole(SSD_GROUPS, 1, LANE), whole(SSD_GROUPS, LANE, 1),
                   whole(SSD_GROUPS, 1, GROUP_COLS)],
        out_shape=[jax.ShapeDtypeStruct((t, SSD_WIDTH), F32), jax.ShapeDtypeStruct((t, 2 * SSD_STATE), F32),
                   jax.ShapeDtypeStruct((t, 2 * SSD_STATE), F32), jax.ShapeDtypeStruct((t, 2 * LANE), F32),
                   jax.ShapeDtypeStruct((t, SSD_WIDTH), F32),
                   jax.ShapeDtypeStruct((SSD_GROUPS, 1, LANE), F32), jax.ShapeDtypeStruct((SSD_GROUPS, 1, LANE), F32),
                   jax.ShapeDtypeStruct((SSD_GROUPS, LANE, 1), F32),
                   jax.ShapeDtypeStruct((SSD_GROUPS, 1, GROUP_COLS), F32)],
        scratch_shapes=[pltpu.VMEM((SSD_GROUPS, SSD_STATE, GROUP_COLS), F32)],
        compiler_params=_cp("arbitrary", "arbitrary"),
    )(xc, xc, xc, proj, proj, dtb, alog, dcol, nw, hprev, dy)


def _disc_a(a_re, a_im, log_dt):
    dt = jnp.exp(log_dt)
    mag = jnp.exp(dt * a_re)
    ab_re, ab_im = mag * jnp.cos(dt * a_im), mag * jnp.sin(dt * a_im)
    den = a_re * a_re + a_im * a_im
    nr, ni = ab_re - 1.0, ab_im
    f_re, f_im = (nr * a_re + ni * a_im) / den, (ni * a_re - nr * a_im) / den
    return ab_re, ab_im, f_re, f_im


def _disc_b(f_re, f_im, b_re, b_im):
    return f_re * b_re - f_im * b_im, f_re * b_im + f_im * b_re


def _whole(f, name, args, outs):
    def body(*refs):
        res = f(*[r[...] for r in refs[:len(args)]])
        for o, v in zip(refs[len(args):], res):
            o[...] = v

    return pl.pallas_call(body, name=name, out_shape=[jax.ShapeDtypeStruct(s, F32) for s in outs])(*args)


def _whole_vjp(f, name, args, cts):
    def body(*refs):
        vals = [r[...] for r in refs[:len(args)]]
        _, pull = jax.vjp(f, *vals)
        res = pull(tuple(r[...] for r in refs[len(args):len(args) + len(cts)]))
        for o, v in zip(refs[len(args) + len(cts):], res):
            o[...] = v

    return pl.pallas_call(body, name=name, out_shape=[jax.ShapeDtypeStruct(a.shape, F32) for a in args])(*args, *cts)


S5_SUB = 8
S5_STEPS = 3


def s5_tables(lam_re, lam_im):
    rows = S5_STEPS * S5_SUB

    def body(lr_ref, li_ref, sf_re, sf_im, sb_re, sb_im, cf_re, cf_im, cb_re, cb_im):
        lr, li = lr_ref[...], li_ref[...]

        def power(k):
            m = jnp.exp(k * lr)
            return m * jnp.cos(k * li), m * jnp.sin(k * li)

        srow = lax.broadcasted_iota(jnp.int32, (rows, 1), 0)
        k = jnp.left_shift(1, srow // S5_SUB)
        tt = srow % S5_SUB
        pr, pi = power(k.astype(F32))
        fwd, bwd = tt >= k, tt < S5_SUB - k
        sf_re[...], sf_im[...] = jnp.where(fwd, pr, 0.0), jnp.where(fwd, pi, 0.0)
        sb_re[...], sb_im[...] = jnp.where(bwd, pr, 0.0), jnp.where(bwd, pi, 0.0)
        trow = lax.broadcasted_iota(jnp.int32, (S5_SUB, 1), 0)
        cf_re[...], cf_im[...] = power((trow + 1).astype(F32))
        cb_re[...], cb_im[...] = power((S5_SUB - trow).astype(F32))

    shp = [jax.ShapeDtypeStruct((rows, S5_COLS), F32)] * 4 + [jax.ShapeDtypeStruct((S5_SUB, S5_COLS), F32)] * 4
    return pl.pallas_call(body, name="s5_tables", out_shape=shp)(lam_re, lam_im)


def _s5_coefs(steps_re, steps_im, carry_re, carry_im, reverse):
    sign = -1.0 if reverse else 1.0
    steps = [(steps_re[s * S5_SUB:(s + 1) * S5_SUB, :], sign * steps_im[s * S5_SUB:(s + 1) * S5_SUB, :])
             for s in range(S5_STEPS)]
    return steps, (carry_re[...], sign * carry_im[...])


def _s5_block_scan(ar, ai, coefs, cr, ci, reverse):
    steps, (qr, qi) = coefs
    for s, (pr, pi) in enumerate(steps):
        shift = S5_SUB - (1 << s) if reverse else (1 << s)
        sr, si = pltpu.roll(ar, shift, 0), pltpu.roll(ai, shift, 0)
        ar, ai = ar + pr * sr - pi * si, ai + pr * si + pi * sr
    br, bi = jnp.broadcast_to(cr, ar.shape), jnp.broadcast_to(ci, ai.shape)
    return ar + qr * br - qi * bi, ai + qr * bi + qi * br


def _s5_specs(n5, rev):
    def tok(q, b, c):
        return b * n5 + (n5 - 1 - c if rev else c)

    qcols = S5_COLS // S5_Q
    specs = [
        pl.BlockSpec((S5_CHUNK, LANE), lambda q, b, c: (tok(q, b, c), P_U // LANE + q)),
        pl.BlockSpec((1, LANE, qcols), lambda q, b, c: (q, 0, 0)),
        pl.BlockSpec((1, LANE, qcols), lambda q, b, c: (q, 0, 0)),
        pl.BlockSpec((1, qcols, LANE), lambda q, b, c: (q, 0, 0)),
        pl.BlockSpec((1, qcols, LANE), lambda q, b, c: (q, 0, 0)),
        pl.BlockSpec((S5_STEPS * S5_SUB, qcols), lambda q, b, c: (0, q)),
        pl.BlockSpec((S5_STEPS * S5_SUB, qcols), lambda q, b, c: (0, q)),
        pl.BlockSpec((S5_SUB, qcols), lambda q, b, c: (0, q)),
        pl.BlockSpec((S5_SUB, qcols), lambda q, b, c: (0, q)),
        pl.BlockSpec((1, 1, LANE), lambda q, b, c: (q, 0, 0)),
    ]
    return specs, tok, qcols


def s5_fwd(proj, wb_re, wb_im, wc_re, wc_im, sf_re, sf_im, cf_re, cf_im, dvec, bsz, seq, ride=None):
    t = proj.shape[0]
    n5 = seq // S5_CHUNK
    in_specs, tok, qcols = _s5_specs(n5, False)

    def body(u_ref, wbr, wbi, wcr, wci, sfr, sfi, cfr, cfi, d_ref, y_ref, xr_ref, xi_ref, cr_scr, ci_scr):
        @pl.when(pl.program_id(2) == 0)
        def _():
            cr_scr[...] = jnp.zeros_like(cr_scr)
            ci_scr[...] = jnp.zeros_like(ci_scr)

        u = u_ref[...]
        bur, bui = _dg(u, wbr[0], 1, 0), _dg(u, wbi[0], 1, 0)
        coefs = _s5_coefs(sfr, sfi, cfr, cfi, False)
        cr, ci = cr_scr[...], ci_scr[...]
        for r in range(S5_CHUNK // S5_SUB):
            rows = slice(r * S5_SUB, (r + 1) * S5_SUB)
            xr, xi = _s5_block_scan(bur[rows], bui[rows], coefs, cr, ci, False)
            xr_ref[rows, :], xi_ref[rows, :] = xr, xi
            cr, ci = xr[S5_SUB - 1:, :], xi[S5_SUB - 1:, :]
        cr_scr[...], ci_scr[...] = cr, ci
        y_ref[...] = _dg(xr_ref[...], wcr[0], 1, 0) - _dg(xi_ref[...], wci[0], 1, 0) + u * d_ref[0]

    def tile(w):
        return pl.BlockSpec((S5_CHUNK, w), lambda q, b, c: (tok(q, b, c), q))

    return hosted_call(
        body, name="s5_fwd", grid=(S5_Q, bsz, n5), in_specs=in_specs,
        out_specs=[tile(LANE), tile(qcols), tile(qcols)],
        out_shape=[jax.ShapeDtypeStruct((t, S5_WIDTH), F32), jax.ShapeDtypeStruct((t, S5_COLS), F32),
                   jax.ShapeDtypeStruct((t, S5_COLS), F32)],
        scratch=[pltpu.VMEM((1, qcols), F32)] * 2,
        args=(proj, wb_re, wb_im, wc_re, wc_im, sf_re, sf_im, cf_re, cf_im, dvec), ride=ride)


def s5_bwd(proj, wb_re, wb_im, wc_re, wc_im, sb_re, sb_im, cb_re, cb_im, dvec, xr_all, xi_all, dy, bsz, seq,
           ride=None):
    t = proj.shape[0]
    n5 = seq // S5_CHUNK
    in_specs, tok, qcols = _s5_specs(n5, True)
    blocks = S5_CHUNK // HALO

    def prev_rows(q, b, c):
        return (jnp.maximum(tok(q, b, c) * blocks - 1, 0), q)

    in_specs += [pl.BlockSpec((S5_CHUNK, qcols), lambda q, b, c: (tok(q, b, c), q)),
                 pl.BlockSpec((S5_CHUNK, qcols), lambda q, b, c: (tok(q, b, c), q)),
                 pl.BlockSpec((HALO, qcols), prev_rows), pl.BlockSpec((HALO, qcols), prev_rows),
                 pl.BlockSpec((S5_CHUNK, LANE), lambda q, b, c: (tok(q, b, c), q))]

    def body(u_ref, wbr, wbi, wcr, wci, sbr, sbi, cbr, cbi, d_ref, xr_ref, xi_ref, pr_ref, pi_ref, dy_ref,
             du_ref, dwbr, dwbi, dwcr, dwci, dar, dai, dd_ref, gr_scr, gi_scr, gr_all, gi_all):
        b, c = pl.program_id(1), pl.program_id(2)

        @pl.when(c == 0)
        def _():
            gr_scr[...] = jnp.zeros_like(gr_scr)
            gi_scr[...] = jnp.zeros_like(gi_scr)

        @pl.when((b == 0) & (c == 0))
        def _():
            for r in (dwbr, dwbi, dwcr, dwci, dar, dai, dd_ref):
                r[...] = jnp.zeros_like(r)

        u, dy_v = u_ref[...], dy_ref[...]
        g0r, g0i = _dg(dy_v, wcr[0], 1, 1), -_dg(dy_v, wci[0], 1, 1)
        coefs = _s5_coefs(sbr, sbi, cbr, cbi, True)
        cr, ci = gr_scr[...], gi_scr[...]
        for r in reversed(range(S5_CHUNK // S5_SUB)):
            rows = slice(r * S5_SUB, (r + 1) * S5_SUB)
            br, bi = _s5_block_scan(g0r[rows], g0i[rows], coefs, cr, ci, True)
            gr_all[rows, :], gi_all[rows, :] = br, bi
            cr, ci = br[:1, :], bi[:1, :]
        gr_scr[...], gi_scr[...] = cr, ci
        gr, gi = gr_all[...], gi_all[...]

        row = lax.broadcasted_iota(jnp.int32, (S5_CHUNK, 1), 0)
        xr, xi = xr_ref[...], xi_ref[...]
        is_first = (c == n5 - 1)
        hr = jnp.where(is_first, 0.0, pr_ref[...][HALO - 1:, :])
        hi = jnp.where(is_first, 0.0, pi_ref[...][HALO - 1:, :])
        xpr = jnp.where(row >= 1, pltpu.roll(xr, 1, 0), hr)
        xpi = jnp.where(row >= 1, pltpu.roll(xi, 1, 0), hi)
        dar[0] += jnp.sum(xpr * gr + xpi * gi, axis=0, keepdims=True)
        dai[0] += jnp.sum(xpr * gi - xpi * gr, axis=0, keepdims=True)
        du_ref[...] = _dg(gr, wbr[0], 1, 1) + _dg(gi, wbi[0], 1, 1) + dy_v * d_ref[0]
        dwbr[0] += _dg(u, gr, 0, 0)
        dwbi[0] += _dg(u, gi, 0, 0)
        dwcr[0] += _dg(xr, dy_v, 0, 0)
        dwci[0] -= _dg(xi, dy_v, 0, 0)
        dd_ref[0] += jnp.sum(dy_v * u, axis=0, keepdims=True)

    def acc(shape):
        return pl.BlockSpec((1,) + shape, lambda q, b, c: (q, 0, 0))

    return hosted_call(
        body, name="s5_bwd", grid=(S5_Q, bsz, n5), in_specs=in_specs,
        out_specs=[pl.BlockSpec((S5_CHUNK, LANE), lambda q, b, c: (tok(q, b, c), q)),
                   acc((LANE, qcols)), acc((LANE, qcols)), acc((qcols, LANE)), acc((qcols, LANE)),
                   acc((1, qcols)), acc((1, qcols)), acc((1, LANE))],
        out_shape=[jax.ShapeDtypeStruct((t, S5_WIDTH), F32),
                   jax.ShapeDtypeStruct((S5_Q, LANE, qcols), F32), jax.ShapeDtypeStruct((S5_Q, LANE, qcols), F32),
                   jax.ShapeDtypeStruct((S5_Q, qcols, LANE), F32), jax.ShapeDtypeStruct((S5_Q, qcols, LANE), F32),
                   jax.ShapeDtypeStruct((S5_Q, 1, qcols), F32), jax.ShapeDtypeStruct((S5_Q, 1, qcols), F32),
                   jax.ShapeDtypeStruct((S5_Q, 1, LANE), F32)],
        scratch=[pltpu.VMEM((1, qcols), F32)] * 2 + [pltpu.VMEM((S5_CHUNK, qcols), F32)] * 2,
        args=(proj, wb_re, wb_im, wc_re, wc_im, sb_re, sb_im, cb_re, cb_im, dvec, xr_all, xi_all, xr_all, xi_all, dy),
        ride=ride)


def _blockdiag_b(bb):
    b4 = bb.reshape(S5_Q, 8, S5_STATE, S5_GROUP_CH)
    eye = jnp.eye(8, dtype=bb.dtype)
    w = jnp.einsum("qgph,gk->qghkp", b4, eye)
    return w.reshape(S5_Q, LANE, S5_COLS // S5_Q)


def _unblock_b(dw):
    d = dw.reshape(S5_Q, 8, S5_GROUP_CH, 8, S5_STATE)
    d = jnp.einsum("qghgp->qgph", d)
    return d.reshape(S5_COLS, S5_GROUP_CH)


def _blockdiag_c(cc):
    c4 = cc.reshape(S5_Q, 8, S5_GROUP_CH, S5_STATE)
    eye = jnp.eye(8, dtype=cc.dtype)
    w = jnp.einsum("qghp,gk->qgpkh", c4, eye)
    return w.reshape(S5_Q, S5_COLS // S5_Q, LANE)


def _unblock_c(dw):
    d = dw.reshape(S5_Q, 8, S5_STATE, 8, S5_GROUP_CH)
    d = jnp.einsum("qgpgh->qghp", d)
    return d.reshape(S5_GROUPS, S5_GROUP_CH, S5_STATE)


def ada_fwd(c_all, w_loc, b_loc):
    def body(c_ref, w_ref, b_ref, o_ref):
        o_ref[...] = _dg(_silu(c_ref[...]), w_ref[...], 1, 0) + b_ref[...]

    return pl.pallas_call(body, name="ada_fwd",
                          out_shape=jax.ShapeDtypeStruct((c_all.shape[0], w_loc.shape[1]), F32),
                          compiler_params=_cp())(c_all, w_loc, b_loc)


def ada_bwd(c_all, dmod_all, dmod_cols):
    def body(c_ref, da_ref, dc_ref, gb_ref, gw_ref):
        gb_ref[...] = jnp.sum(da_ref[...], axis=0, keepdims=True)
        gw_ref[...] = _dg(_silu(c_ref[...]), dc_ref[...], 0, 0)

    return pl.pallas_call(body, name="ada_bwd",
                          out_shape=[jax.ShapeDtypeStruct((1, dmod_all.shape[1]), F32),
                                     jax.ShapeDtypeStruct((c_all.shape[1], dmod_cols.shape[1]), F32)],
                          compiler_params=_cp())(c_all, dmod_all, dmod_cols)


_FLIPS = [(0, 0, 1), (1, 0, 0), (0, 1, 0), (1, 1, 0), (1, 0, 1), (0, 1, 1), (1, 1, 1)]


def _exchange_ops(srcs, outs, sems, gather):
    n = len(srcs)
    send_sems, recv_sems, loc_sems = sems
    x, y, c = lax.axis_index("x"), lax.axis_index("y"), lax.axis_index("c")
    me = 4 * x + 2 * y + c
    peers = []
    for fx, fy, fc in _FLIPS:
        px, py, pc = (1 - x if fx else x), (1 - y if fy else y), (1 - c if fc else c)
        peers.append(((px, py, pc), 4 * px + 2 * py + pc))

    def copy(k, j, slot_src, slot_dst):
        src = srcs[k] if gather[k] else srcs[k].at[slot_src]
        return pltpu.make_async_remote_copy(src_ref=src, dst_ref=outs[k].at[slot_dst],
                                            send_sem=send_sems.at[k, j], recv_sem=recv_sems.at[k, j],
                                            device_id=peers[j][0], device_id_type=MESH)

    def local(k):
        own = srcs[k] if gather[k] else srcs[k].at[me]
        return pltpu.make_async_copy(own, outs[k].at[me], loc_sems.at[k])

    def start():
        for k in range(n):
            for j in range(N_DEV - 1):
                copy(k, j, peers[j][1], me).start()
            local(k).start()

    def wait():
        for k in range(n):
            for j in range(N_DEV - 1):
                copy(k, j, me, peers[j][1]).wait_recv()
        for k in range(n):
            for j in range(N_DEV - 1):
                copy(k, j, peers[j][1], me).wait_send()
            local(k).wait()

    return start, wait


def _gather_two_level(srcs, outs, sems):
    n = len(srcs)
    send_sems, recv_sems, loc_sems = sems
    x, y, c = lax.axis_index("x"), lax.axis_index("y"), lax.axis_index("c")
    slot = lambda px, py, pc: 4 * px + 2 * py + pc
    me, sibling = (x, y, c), (x, y, 1 - c)
    chips = [(1 - x, y), (x, 1 - y), (1 - x, 1 - y)]

    def copy(k, j, block, to, own=False):
        return pltpu.make_async_remote_copy(src_ref=srcs[k] if own else outs[k].at[slot(*block)],
                                            dst_ref=outs[k].at[slot(*block)],
                                            send_sem=send_sems.at[k, j], recv_sem=recv_sems.at[k, j],
                                            device_id=to, device_id_type=MESH)

    locs = [pltpu.make_async_copy(srcs[k], outs[k].at[slot(*me)], loc_sems.at[k]) for k in range(n)]
    for k in range(n):
        locs[k].start()
        copy(k, 0, me, sibling, own=True).start()
        for j, chip in enumerate(chips):
            copy(k, 1 + j, me, (*chip, c), own=True).start()
    for j, chip in enumerate(chips):
        for k in range(n):
            copy(k, 1 + j, (*chip, c), me).wait_recv()
            copy(k, 4 + j, (*chip, c), sibling).start()
    for k in range(n):
        copy(k, 0, sibling, me).wait_recv()
        for j, chip in enumerate(chips):
            copy(k, 4 + j, (*chip, 1 - c), me).wait_recv()
    for k in range(n):
        copy(k, 0, me, sibling, own=True).wait_send()
        for j, chip in enumerate(chips):
            copy(k, 1 + j, me, (*chip, c), own=True).wait_send()
            copy(k, 4 + j, (*chip, c), sibling).wait_send()
        locs[k].wait()


def gather_two_level(name, arrs):
    n = len(arrs)
    specs, shapes, sems = _exchange_parts(arrs, [True] * n)

    def body(*refs):
        _gather_two_level(refs[:n], refs[n:2 * n], refs[2 * n:])

    return pl.pallas_call(
        body, name=name, in_specs=specs, out_specs=specs, out_shape=shapes, scratch_shapes=sems,
        compiler_params=pltpu.CompilerParams(has_side_effects=True),
    )(*arrs)


def _exchange_parts(arrs, gather):
    n = len(arrs)
    any_spec = pl.BlockSpec(memory_space=pl.ANY)
    shapes = [jax.ShapeDtypeStruct(((N_DEV,) + a.shape) if g else a.shape, a.dtype) for a, g in zip(arrs, gather)]
    sems = [pltpu.SemaphoreType.DMA((n, N_DEV - 1)), pltpu.SemaphoreType.DMA((n, N_DEV - 1)),
            pltpu.SemaphoreType.DMA((n,))]
    return [any_spec] * n, shapes, sems


def exchange(name, arrs, gather):
    n = len(arrs)
    specs, shapes, sems = _exchange_parts(arrs, gather)

    def body(*refs):
        start, wait = _exchange_ops(refs[:n], refs[n:2 * n], refs[2 * n:], gather)
        start()
        wait()

    return pl.pallas_call(
        body, name=name, in_specs=specs, out_specs=specs, out_shape=shapes, scratch_shapes=sems,
        compiler_params=pltpu.CompilerParams(has_side_effects=True),
    )(*arrs)


def hosted_call(body, *, name, grid, in_specs, out_specs, out_shape, args, scratch=(), ride=None):
    sem = ("arbitrary",) * len(grid)
    if ride is None:
        res = pl.pallas_call(body, name=name, grid=grid, in_specs=in_specs, out_specs=out_specs, out_shape=out_shape,
                             scratch_shapes=list(scratch), compiler_params=_cp(*sem))(*args)
        return list(res), []
    arrs, gather = ride
    n, n_in, n_out, n_scr = len(arrs), len(in_specs), len(out_specs), len(scratch)
    specs, shapes, sems = _exchange_parts(arrs, gather)

    def both(*refs):
        ins, srcs = refs[:n_in], refs[n_in:n_in + n]
        outs, landed = refs[n_in + n:n_in + n + n_out], refs[n_in + n + n_out:n_in + 2 * n + n_out]
        scr, ex_sems = refs[n_in + 2 * n + n_out:n_in + 2 * n + n_out + n_scr], refs[n_in + 2 * n + n_out + n_scr:]
        start, wait = _exchange_ops(srcs, landed, ex_sems, gather)
        first = functools.reduce(lambda a, b: a & b, [pl.program_id(d) == 0 for d in range(len(grid))])
        last = functools.reduce(lambda a, b: a & b, [pl.program_id(d) == grid[d] - 1 for d in range(len(grid))])
        pl.when(first)(start)
        body(*ins, *outs, *scr)
        pl.when(last)(wait)

    res = pl.pallas_call(
        both, name=name, grid=grid, in_specs=list(in_specs) + specs, out_specs=list(out_specs) + specs,
        out_shape=list(out_shape) + shapes, scratch_shapes=list(scratch) + sems, compiler_params=_cp(*sem),
    )(*args, *arrs)
    return list(res[:n_out]), list(res[n_out:])


def sum_slots(name, slots, tr):
    _, r, c = slots.shape

    def body(s_ref, o_ref):
        acc = s_ref[0].astype(F32)
        for j in range(1, N_DEV):
            acc = acc + s_ref[j].astype(F32)
        o_ref[...] = acc

    return pl.pallas_call(
        body, name=name, grid=(r // tr,), in_specs=[pl.BlockSpec((N_DEV, tr, c), lambda i: (0, i, 0))],
        out_specs=pl.BlockSpec((tr, c), lambda i: (i, 0)), out_shape=jax.ShapeDtypeStruct((r, c), F32),
        compiler_params=_cp("parallel"),
    )(slots)


def adamw(name, g, w, m, v, tr, sel=None):
    slots = g.ndim >= 3
    r, c = w.shape
    c1, c2 = 1.0 - ADAM_B1 ** ADAM_STEP, 1.0 - ADAM_B2 ** ADAM_STEP

    def body(g_ref, w_ref, m_ref, v_ref, go, do, mo, vo):
        if slots:
            gg = g_ref[0].astype(F32)
            for j in range(1, N_DEV):
                gg = gg + g_ref[j].astype(F32)
        else:
            gg = g_ref[...]
        mn = ADAM_B1 * m_ref[...] + (1.0 - ADAM_B1) * gg
        vn = ADAM_B2 * v_ref[...] + (1.0 - ADAM_B2) * (gg * gg)
        go[...], mo[...], vo[...] = gg, mn, vn
        do[...] = -ADAM_LR * ((mn / c1) / (jnp.sqrt(vn / c2) + ADAM_EPS) + ADAM_WD * w_ref[...])

    blk = pl.BlockSpec((tr, c), lambda i: (i, 0))
    if g.ndim == 4:
        gspec = pl.BlockSpec((N_DEV, None, tr, c), lambda i: (0, sel, i, 0))
    else:
        gspec = pl.BlockSpec((N_DEV, tr, c), lambda i: (0, i, 0)) if slots else blk
    return pl.pallas_call(
        body, name=name, grid=(r // tr,), in_specs=[gspec, blk, blk, blk], out_specs=[blk] * 4,
        out_shape=[jax.ShapeDtypeStruct((r, c), F32)] * 4, compiler_params=_cp("parallel"),
    )(g, w, m, v)


def _lane_rows(n):
    return -(-n // (8 * LANE)) * 8


def _pack(arrs):
    pieces = []
    for a in arrs:
        n = math.prod(a.shape)
        flat = a.reshape(-1).astype(F32)
        pieces.append(jnp.pad(flat, (0, _lane_rows(n) * LANE - n)).reshape(_lane_rows(n), LANE))
    return jnp.concatenate(pieces, axis=0)


def _unpack(buf, shapes):
    out, off = [], 0
    for s in shapes:
        n = math.prod(s)
        out.append(buf[off:off + _lane_rows(n)].reshape(-1)[:n].reshape(s))
        off += _lane_rows(n)
    return out


def _cols_to_full(g):
    return jnp.transpose(g, (1, 0, 2)).reshape(g.shape[1], N_DEV * g.shape[2])


def _full_to_cols(w):
    r, c = w.shape
    return jnp.transpose(w.reshape(r, N_DEV, c // N_DEV), (1, 0, 2))


FF_CHUNK = D_FF // 2
DW_TOKENS = 1024
FFN_TM = 256


def _resident(shape):
    return pl.BlockSpec(shape, lambda i: (0,) * len(shape), pipeline_mode=pl.Buffered(1))


def _ffn_fwd(tag, x, sc, sh, g, w1, w3, w2, lg, lb, seq, tm, ride=None, target=None):
    t = x.shape[0]
    tm = min(FFN_TM, tm)
    tps = seq // tm
    ln = _res_ln(0.5)
    head = target is not None

    def body(x_ref, sc_ref, sh_ref, g_ref, lg_ref, lb_ref, w1_ref, w3_ref, w2_ref, *rest):
        if head:
            t_ref, y_ref, h_ref, a_ref, b_ref, f_ref, l_ref = rest
        else:
            y_ref, h_ref, a_ref, b_ref, f_ref = rest
        xv = x_ref[...]
        h = (xv * (1.0 + sc_ref[0]) + sh_ref[0]).astype(BF16)
        h_ref[...] = h
        acc = jnp.zeros((tm, D_MODEL), F32)
        for j in range(D_FF // FF_CHUNK):
            sl = slice(j * FF_CHUNK, (j + 1) * FF_CHUNK)
            a = _dg(h, w1_ref[sl, :], 1, 1)
            b = _dg(h, w3_ref[sl, :], 1, 1)
            a_ref[:, sl] = a
            b_ref[:, sl] = b
            acc = acc + _dg(_silu(a) * b, w2_ref[sl, :], 1, 0)
        f_ref[...] = acc
        y = ln(xv, acc, g_ref[0], lg_ref[...], lb_ref[...])[0]
        if head:
            @pl.when(pl.program_id(0) == 0)
            def _():
                l_ref[...] = jnp.zeros_like(l_ref)

            e = y - t_ref[...]
            y_ref[...] = e * (1.0 / D_MODEL)
            l_ref[...] += 0.5 * jnp.sum(jnp.mean(e * e, axis=-1, keepdims=True), axis=0, keepdims=True)
        else:
            y_ref[...] = y

    row = lambda c: pl.BlockSpec((tm, c), lambda i: (i, 0))
    per_seq = pl.BlockSpec((1, 1, D_MODEL), lambda i: (i // tps, 0, 0))
    vec = pl.BlockSpec((1, D_MODEL), lambda i: (0, 0))
    res, landed = hosted_call(
        body, name=tag + "_fwd", grid=(t // tm,),
        in_specs=[row(D_MODEL), per_seq, per_seq, per_seq, vec, vec,
                  _resident((D_FF, D_MODEL)), _resident((D_FF, D_MODEL)), _resident((D_FF, D_MODEL))]
        + ([row(D_MODEL)] if head else []),
        out_specs=[row(D_MODEL), row(D_MODEL), row(D_FF), row(D_FF), row(D_MODEL)]
        + ([pl.BlockSpec((1, 1), lambda i: (0, 0))] if head else []),
        out_shape=[jax.ShapeDtypeStruct((t, D_MODEL), F32), jax.ShapeDtypeStruct((t, D_MODEL), BF16),
                   jax.ShapeDtypeStruct((t, D_FF), F32), jax.ShapeDtypeStruct((t, D_FF), F32),
                   jax.ShapeDtypeStruct((t, D_MODEL), F32)] + ([jax.ShapeDtypeStruct((1, 1), F32)] if head else []),
        args=(x, sc, sh, g, lg, lb, w1, w3, w2) + ((target,) if head else ()), ride=ride)
    first = (res[0], res[5][0, 0]) if head else res[0]
    return first, tuple(res[1:5]), landed


def _ffn_bwd(tag, dy, x, sc, sh, g, w1, w3, w2, lg, lb, res, seq, tm, ride=None, chain=None):
    h, a, b, f = res
    t = x.shape[0]
    tmk = min(FFN_TM, tm)
    tps = seq // tmk
    ln = _res_ln(0.5)

    def body(dy_ref, x_ref, f_ref, a_ref, b_ref, sc_ref, sh_ref, g_ref, lg_ref, lb_ref, w1_ref, w3_ref, w2_ref,
             dx_ref, da_ref, db_ref, s_ref, df_ref, dsc_ref, dsh_ref, dg_ref, dlg_ref, dlb_ref):
        i = pl.program_id(0)

        @pl.when(i % tps == 0)
        def _():
            for r in (dsc_ref, dsh_ref, dg_ref):
                r[...] = jnp.zeros_like(r)

        @pl.when(i == 0)
        def _():
            dlg_ref[...] = jnp.zeros_like(dlg_ref)
            dlb_ref[...] = jnp.zeros_like(dlb_ref)

        xv = x_ref[...]
        _, pull = jax.vjp(ln, xv, f_ref[...], g_ref[0], lg_ref[...], lb_ref[...])
        dx_res, df, dg, dlg, dlb = pull((dy_ref[...],))
        dfb = df.astype(BF16)
        df_ref[...] = dfb
        dh = jnp.zeros((tmk, D_MODEL), F32)
        for j in range(D_FF // FF_CHUNK):
            sl = slice(j * FF_CHUNK, (j + 1) * FF_CHUNK)
            ds = _dg(dfb, w2_ref[sl, :], 1, 1)
            av, bv = a_ref[:, sl], b_ref[:, sl]
            sg = jax.nn.sigmoid(av)
            si = av * sg
            s_ref[:, sl] = (si * bv).astype(BF16)
            da = (ds * bv * (sg * (1.0 + av * (1.0 - sg)))).astype(BF16)
            db = (ds * si).astype(BF16)
            da_ref[:, sl] = da
            db_ref[:, sl] = db
            dh = dh + _dg(da, w1_ref[sl, :], 1, 0) + _dg(db, w3_ref[sl, :], 1, 0)
        dx_ref[...] = dx_res + dh * (1.0 + sc_ref[0])
        dsc_ref[0] += jnp.sum(dh * xv, axis=0, keepdims=True)
        dsh_ref[0] += jnp.sum(dh, axis=0, keepdims=True)
        dg_ref[0] += dg
        dlg_ref[...] += dlg
        dlb_ref[...] += dlb

    row = lambda c: pl.BlockSpec((tmk, c), lambda i: (i, 0))
    per_seq = pl.BlockSpec((1, 1, D_MODEL), lambda i: (i // tps, 0, 0))
    vec = pl.BlockSpec((1, D_MODEL), lambda i: (0, 0))
    seq_shape = jax.ShapeDtypeStruct(sc.shape, F32)
    vec_shape = jax.ShapeDtypeStruct((1, D_MODEL), F32)
    (dx, da, db, s, df, dsc, dsh, dg, dlg, dlb), landed = hosted_call(
        body, name=tag + "_bwd", grid=(t // tmk,),
        in_specs=[row(D_MODEL), row(D_MODEL), row(D_MODEL), row(D_FF), row(D_FF), per_seq, per_seq, per_seq, vec, vec,
                  _resident((D_FF, D_MODEL)), _resident((D_FF, D_MODEL)), _resident((D_FF, D_MODEL))],
        out_specs=[row(D_MODEL), row(D_FF), row(D_FF), row(D_FF), row(D_MODEL), per_seq, per_seq, per_seq, vec, vec],
        out_shape=[jax.ShapeDtypeStruct((t, D_MODEL), F32), jax.ShapeDtypeStruct((t, D_FF), BF16),
                   jax.ShapeDtypeStruct((t, D_FF), BF16), jax.ShapeDtypeStruct((t, D_FF), BF16),
                   jax.ShapeDtypeStruct((t, D_MODEL), BF16), seq_shape, seq_shape, seq_shape, vec_shape, vec_shape],
        args=(dy, x, f, a, b, sc, sh, g, lg, lb, w1, w3, w2), ride=ride)
    tt = min(DW_TOKENS, seq)
    shards = lambda dw: dw.reshape(N_DEV, D_FF // N_DEV, D_MODEL)
    if chain is None:
        dw2, landed = mm_tn(tag + "_dw2", s, df, D_FF // 2, D_MODEL, tt, BF16), []
    else:
        dw2, landed = mm_tn(tag + "_dw2", s, df, D_FF // 2, D_MODEL, tt, BF16, ride=chain((dsh, dsc, dg), dlg, dlb))
    dw1, (s_w2,) = mm_tn(tag + "_dw1", da, h, D_FF // 2, D_MODEL, tt, BF16, ride=([shards(dw2)], [False]))
    dw3, (s_w1,) = mm_tn(tag + "_dw3", db, h, D_FF // 2, D_MODEL, tt, BF16, ride=([shards(dw1)], [False]))
    return dx, (dsh, dsc, dg), (s_w1, shards(dw3), s_w2, dlg, dlb), landed


def kernel(x, c, w_ada, b_ada, ffn1_w1, ffn1_w3, ffn1_w2, ln1_g, ln1_b, w_in, conv_w, conv_b, dt_bias, a_log, d_ssd, ssd_norm_w, s5_a_re, s5_a_im, s5_log_dt, s5_b_re, s5_b_im, s5_c_re, s5_c_im, s5_d, w_glu, b_glu, w_out, ln2_g, ln2_b, ffn2_w1, ffn2_w3, ffn2_w2, ln3_g, ln3_b, loss_target, m_w_ada, m_b_ada, m_ffn1_w1, m_ffn1_w3, m_ffn1_w2, m_ln1_g, m_ln1_b, m_w_in, m_conv_w, m_conv_b, m_dt_bias, m_a_log, m_d_ssd, m_ssd_norm_w, m_s5_a_re, m_s5_a_im, m_s5_log_dt, m_s5_b_re, m_s5_b_im, m_s5_c_re, m_s5_c_im, m_s5_d, m_w_glu, m_b_glu, m_w_out, m_ln2_g, m_ln2_b, m_ffn2_w1, m_ffn2_w3, m_ffn2_w2, m_ln3_g, m_ln3_b, v_w_ada, v_b_ada, v_ffn1_w1, v_ffn1_w3, v_ffn1_w2, v_ln1_g, v_ln1_b, v_w_in, v_conv_w, v_conv_b, v_dt_bias, v_a_log, v_d_ssd, v_ssd_norm_w, v_s5_a_re, v_s5_a_im, v_s5_log_dt, v_s5_b_re, v_s5_b_im, v_s5_c_re, v_s5_c_im, v_s5_d, v_w_glu, v_b_glu, v_w_out, v_ln2_g, v_ln2_b, v_ffn2_w1, v_ffn2_w3, v_ffn2_w2, v_ln3_g, v_ln3_b):
    given = dict(locals())
    bsz, seq, _ = x.shape
    t = bsz * seq
    tm = min(512, seq)
    me = 4 * lax.axis_index("x") + 2 * lax.axis_index("y") + lax.axis_index("c")
    x0 = x.reshape(t, D_MODEL)
    target = loss_target.reshape(t, D_MODEL)

    tr16 = lambda w: w[0].T.astype(BF16)
    whole = lambda g: g.reshape(N_DEV * g.shape[1], g.shape[2])
    g_f1w1, g_f1w3, g_f1w2, g_c = gather_two_level(
        "gather_ffn1", [tr16(ffn1_w1), tr16(ffn1_w3), ffn1_w2[0].astype(BF16), c])
    f1w1, f1w3, f1w2 = whole(g_f1w1), whole(g_f1w3), whole(g_f1w2)
    c_all = whole(g_c)

    n_loc = w_ada.shape[2]
    b_loc = lax.dynamic_slice(b_ada, (0, me * n_loc), (1, n_loc))
    mod_cols = ada_fwd(c_all, w_ada[0], b_loc)
    g_mod, = exchange("gather_mod", [mod_cols], [True])
    mine = lax.dynamic_slice(g_mod, (0, me * bsz, 0), (N_DEV, bsz, n_loc))
    mod = jnp.transpose(mine, (1, 0, 2)).reshape(bsz, N_MOD, 1, D_MODEL)
    sh1, sc1, g1, sh2, sc2, g2, sh3, sc3, g3 = [mod[:, k] for k in range(N_MOD)]

    x1, res1, (g_win, g_glu, g_out, g_conv, g_f2w1) = _ffn_fwd(
        "ffn1", x0, sc1, sh1, g1, f1w1, f1w3, f1w2, ln1_g, ln1_b, seq, tm,
        ride=([tr16(w_in), w_glu[0].astype(BF16), w_out[0].astype(BF16), conv_w[0], tr16(ffn2_w1)], [True] * 5))
    win = whole(g_win)
    wglu = whole(g_glu).astype(F32)
    wout = whole(g_out)
    wo_ssd, wo_s5 = wout[:SSD_WIDTH], wout[SSD_WIDTH:]
    convw = jnp.transpose(g_conv, (1, 0, 2)).reshape(CONV_K, CONV_CH)
    w_z, w_xbc = win[:SSD_WIDTH], win[SSD_WIDTH:SSD_WIDTH + CONV_CH]
    w_dt = win[SSD_WIDTH + CONV_CH:SSD_WIDTH + CONV_CH + SSD_HEADS]
    w_u = win[SSD_WIDTH + CONV_CH + SSD_HEADS:]
    dt_pad = [jnp.pad(w_dt[HEADS_PER_GROUP * g:HEADS_PER_GROUP * (g + 1)], ((0, LANE - HEADS_PER_GROUP), (0, 0)))
              for g in range(SSD_GROUPS)]
    w_dtp = jnp.concatenate(dt_pad, axis=0)
    w_proj = jnp.concatenate([w_xbc, w_z, w_u, w_dtp], axis=0)

    h2, = rowwise_fwd("mix_mod", f_modulate, [x1], [sc2, sh2], [], [(D_MODEL, BF16)], seq, tm)
    proj = mm_nt("mix_proj", [h2], [w_proj], tm, P_COLS // 2)
    xc = conv_fwd(proj, convw, conv_b, seq, tm)
    dtb = jnp.pad(dt_bias.reshape(SSD_GROUPS, 1, HEADS_PER_GROUP), ((0, 0), (0, 0), (0, LANE - HEADS_PER_GROUP)))
    alog = jnp.pad(a_log.reshape(SSD_GROUPS, 1, HEADS_PER_GROUP), ((0, 0), (0, 0), (0, LANE - HEADS_PER_GROUP)))
    dcol = jnp.pad(d_ssd.reshape(SSD_GROUPS, HEADS_PER_GROUP, 1), ((0, 0), (0, LANE - HEADS_PER_GROUP), (0, 0)))
    nw = ssd_norm_w.reshape(SSD_GROUPS, 1, GROUP_COLS)
    (y_ssd, hprev), (g_f2w3,) = ssd_fwd(xc, proj, dtb, alog, dcol, nw, bsz, seq,
                                        ride=([tr16(ffn2_w3)], [True]))

    a_re2, a_im2, ldt2 = s5_a_re[0], s5_a_im[0], s5_log_dt.reshape(S5_GROUPS, 1)
    ab_re, ab_im, f_re, f_im = _whole(_disc_a, "s5_disc_a", [a_re2, a_im2, ldt2], [(S5_GROUPS, S5_STATE)] * 4)
    b_re2, b_im2 = s5_b_re.reshape(S5_COLS, S5_GROUP_CH), s5_b_im.reshape(S5_COLS, S5_GROUP_CH)
    fr_col, fi_col = f_re.reshape(S5_COLS, 1), f_im.reshape(S5_COLS, 1)
    bb_re, bb_im = _whole(_disc_b, "s5_disc_b", [fr_col, fi_col, b_re2, b_im2], [(S5_COLS, S5_GROUP_CH)] * 2)
    wb_re, wb_im = _blockdiag_b(bb_re).astype(BF16), _blockdiag_b(bb_im).astype(BF16)
    wc_re, wc_im = _blockdiag_c(s5_c_re[0]).astype(BF16), _blockdiag_c(s5_c_im[0]).astype(BF16)
    dt5 = jnp.exp(ldt2)
    lam_re, lam_im = (dt5 * a_re2).reshape(1, S5_COLS), (dt5 * a_im2).reshape(1, S5_COLS)
    sf_re, sf_im, sb_re, sb_im, cf_re, cf_im, cb_re, cb_im = s5_tables(lam_re, lam_im)
    d5 = s5_d.reshape(S5_Q, 1, LANE)
    (y5, xr_all, xi_all), (g_f2w2,) = s5_fwd(
        proj, wb_re, wb_im, wc_re, wc_im, sf_re, sf_im, cf_re, cf_im, d5, bsz, seq,
        ride=([ffn2_w2[0].astype(BF16)], [True]))
    f2w1, f2w3, f2w2 = whole(g_f2w1), whole(g_f2w3), whole(g_f2w2)
    o5, = rowwise_fwd("s5_glu", f_glu, [y5], [], [wglu, b_glu], [(S5_WIDTH, F32)], seq, tm)

    mix = mm_nn("mix_out", [y_ssd, o5], [wo_ssd, wo_s5], tm, D_MODEL)
    x2, = rowwise_fwd("mix_ln", _res_ln(1.0), [x1, mix], [g2], [ln2_g, ln2_b], [(D_MODEL, F32)], seq, tm)

    (dy, loss_loc), res3, _ = _ffn_fwd("ffn2", x2, sc3, sh3, g3, f2w1, f2w3, f2w2, ln3_g, ln3_b, seq, tm, target=target)

    dx2, dmod3, (s_f2w1, d_f2w3, s_f2w2, d_ln3g, d_ln3b), _ = _ffn_bwd(
        "ffn2", dy, x2, sc3, sh3, g3, f2w1, f2w3, f2w2, ln3_g, ln3_b, res3, seq, tm)

    (dx1_a, dmix), (dg2,), (d_ln2g, d_ln2b) = rowwise_bwd(
        "mix_ln_b", _res_ln(1.0), [x1, mix], [g2], [ln2_g, ln2_b], [dx2], seq, tm, [F32, BF16])
    tw = min(DW_TOKENS, seq)
    d_wo = jnp.concatenate([mm_tn("mix_dwo_ssd", y_ssd, dmix, SSD_WIDTH, D_MODEL, tw, BF16),
                            mm_tn("mix_dwo_s5", o5, dmix, S5_WIDTH, D_MODEL, tw, BF16)], axis=0)
    dy_mixed = mm_nt("mix_dy", [dmix], [wout], tm, D_MODEL)
    dy_ssd, do5 = dy_mixed, (dy_mixed, SSD_WIDTH, S5_WIDTH)

    (dy5,), _, (d_wglu, d_bglu) = rowwise_bwd("s5_glu_b", f_glu, [y5], [], [wglu, b_glu], [do5], seq, tm, [F32])
    (du, dwbr, dwbi, dwcr, dwci, dab_re, dab_im, dd5), (s_f2w3, s_out, s_glu) = s5_bwd(
        proj, wb_re, wb_im, wc_re, wc_im, sb_re, sb_im, cb_re, cb_im, d5, xr_all, xi_all, dy5, bsz, seq,
        ride=([d_f2w3, d_wo.reshape(N_DEV, D_MODEL // N_DEV, D_MODEL),
               d_wglu.reshape(N_DEV, S5_WIDTH // N_DEV, S5_WIDTH).astype(BF16)], [False] * 3))
    dbb_re, dbb_im = _unblock_b(dwbr), _unblock_b(dwbi)
    dfr_col, dfi_col, d_b_re, d_b_im = _whole_vjp(_disc_b, "s5_disc_b_b", [fr_col, fi_col, b_re2, b_im2],
                                                  [dbb_re, dbb_im])
    d_a_re, d_a_im, d_ldt = _whole_vjp(
        _disc_a, "s5_disc_a_b", [a_re2, a_im2, ldt2],
        [dab_re.reshape(S5_GROUPS, S5_STATE), dab_im.reshape(S5_GROUPS, S5_STATE),
         dfr_col.reshape(S5_GROUPS, S5_STATE), dfi_col.reshape(S5_GROUPS, S5_STATE)])
    d_c_re, d_c_im = _unblock_c(dwcr), _unblock_c(dwci)

    dxs, dbm, dcm, ddt, dz, ddtb, dalog, ddcol, dnw = ssd_bwd(xc, proj, dtb, alog, dcol, nw, hprev, dy_ssd, bsz, seq)
    dpre, d_convw, d_convb = conv_bwd_pre(proj, convw, conv_b, dxs, dbm, dcm, seq, tm)
    dxbc = conv_bwd_x(dpre, convw, seq, tm)

    dw_xbc = mm_tn("mix_dw_xbc", dxbc, h2, CONV_CH, D_MODEL, tw, BF16)
    dw_z = mm_tn("mix_dw_z", dz, h2, SSD_WIDTH, D_MODEL, tw, BF16)
    dw_u = mm_tn("mix_dw_u", du, h2, S5_WIDTH, D_MODEL, tw, BF16)
    dw_dt = mm_tn("mix_dw_dt", ddt, h2, 2 * LANE, D_MODEL, tw, BF16)
    dw_dt8 = jnp.concatenate([dw_dt[LANE * g:LANE * g + HEADS_PER_GROUP] for g in range(SSD_GROUPS)], axis=0)
    d_win = jnp.concatenate([dw_z, dw_xbc, dw_dt8, dw_u], axis=0)
    dh2, (s_win,) = mm_nn("mix_dh", [dxbc, dz, du, ddt], [w_xbc, w_z, w_u, w_dtp], tm, D_MODEL,
                          ride=([d_win.reshape(N_DEV, IN_COLS // N_DEV, D_MODEL)], [False]))
    (dx1,), (dsc2, dsh2), _ = rowwise_bwd("mix_mod_b", f_modulate, [x1], [sc2, sh2], [], [dh2], seq, tm, [F32],
                                          add_rows={0: dx1_a})

    packing = {}

    def small_and_dmod(dmod1, d_ln1g, d_ln1b):
        dmod = jnp.concatenate(list(dmod1) + [dsh2, dsc2, dg2] + list(dmod3), axis=1).reshape(bsz, N_MOD * D_MODEL)
        small = _small_grads(d_ln1g, d_ln1b)
        packing["names"] = list(small)
        packing["shapes"] = [small[k].shape for k in small]
        return [_pack(list(small.values())), dmod], [True, True]

    def _small_grads(d_ln1g, d_ln1b):
        return {
            "ln1_g": d_ln1g, "ln1_b": d_ln1b, "conv_w": d_convw, "conv_b": d_convb,
            "dt_bias": ddtb[:, 0, :HEADS_PER_GROUP].reshape(1, SSD_HEADS),
            "a_log": dalog[:, 0, :HEADS_PER_GROUP].reshape(1, SSD_HEADS),
            "d_ssd": ddcol[:, :HEADS_PER_GROUP, 0].reshape(1, SSD_HEADS),
            "ssd_norm_w": dnw.reshape(1, SSD_WIDTH),
            "s5_a_re": d_a_re[None], "s5_a_im": d_a_im[None], "s5_log_dt": d_ldt.reshape(1, S5_GROUPS),
            "s5_b_re": d_b_re.reshape(s5_b_re.shape), "s5_b_im": d_b_im.reshape(s5_b_im.shape),
            "s5_c_re": d_c_re[None], "s5_c_im": d_c_im[None], "s5_d": dd5.reshape(1, S5_WIDTH),
            "b_glu": d_bglu, "ln2_g": d_ln2g, "ln2_b": d_ln2b, "ln3_g": d_ln3g, "ln3_b": d_ln3b,
            "loss": loss_loc.reshape(1, 1),
        }

    dx0, _, (s_f1w1, d_f1w3, s_f1w2, _, _), (s_small, s_dmod) = _ffn_bwd(
        "ffn1", dx1, x0, sc1, sh1, g1, f1w1, f1w3, f1w2, ln1_g, ln1_b, res1, seq, tm, chain=small_and_dmod)
    names, shapes = packing["names"], packing["shapes"]
    s_f1w3, = exchange("sum_grads", [d_f1w3], [False])

    out = {"grad_x": dx0.reshape(x.shape)}

    def put(name, res, shape):
        for key, val in zip(("grad_", "delta_", "new_m_", "new_v_"), res):
            out[key + name] = val.reshape(shape)

    for name, slots, tr in (("ffn1_w1", s_f1w1, 176), ("ffn1_w3", s_f1w3, 176), ("ffn2_w1", s_f2w1, 176),
                            ("ffn2_w3", s_f2w3, 176), ("w_in", s_win, IN_COLS // N_DEV)):
        w = given[name]
        grad = sum_slots("sum_" + name, slots, tr).T
        put(name, adamw("adam_" + name, grad, w[0], given["m_" + name][0], given["v_" + name][0], 256), w.shape)
    for name, slots in (("ffn1_w2", s_f1w2), ("ffn2_w2", s_f2w2)):
        w = given[name]
        put(name, adamw("adam_" + name, slots, w[0], given["m_" + name][0], given["v_" + name][0], 176), w.shape)
    put("w_glu", adamw("adam_w_glu", s_glu, w_glu[0], m_w_glu[0], v_w_glu[0], 64), w_glu.shape)
    put("w_out", adamw("adam_w_out", s_out, w_out[0], m_w_out[0], v_w_out[0], 128), w_out.shape)

    dmod_all = s_dmod.reshape(N_DEV * bsz, N_MOD * D_MODEL)
    g_bada, g_wada = ada_bwd(c_all, dmod_all, lax.dynamic_slice(dmod_all, (0, me * n_loc), (N_DEV * bsz, n_loc)))
    put("w_ada", adamw("adam_w_ada", g_wada, w_ada[0], m_w_ada[0], v_w_ada[0], 256), w_ada.shape)
    put("b_ada", adamw("adam_b_ada", g_bada, b_ada, m_b_ada, v_b_ada, 1), b_ada.shape)

    not_params = {"conv_w": jnp.zeros((CONV_K, CONV_CH), F32), "loss": jnp.zeros((1, 1), F32)}
    pw, pm, pv = [_pack([not_params[k] if k in not_params else given[pre + k] for k in names]) for pre in ("", "m_", "v_")]
    res_small = adamw("adam_small", s_small, pw, pm, pv, pw.shape[0])
    parts = [_unpack(r, shapes) for r in res_small]
    for i, k in enumerate(names):
        if k not in not_params:
            put(k, [p[i] for p in parts], given[k].shape)
    out["loss"] = parts[0][names.index("loss")][0, 0]
    g_cw = lax.dynamic_slice(parts[0][names.index("conv_w")], (0, me * LANE), (CONV_K, LANE))
    put("conv_w", adamw("adam_conv_w", g_cw, conv_w[0], m_conv_w[0], v_conv_w[0], CONV_K), conv_w.shape)

    order = ["w_ada", "b_ada", "ffn1_w1", "ffn1_w3", "ffn1_w2", "ln1_g", "ln1_b", "w_in", "conv_w", "conv_b", "dt_bias",
             "a_log", "d_ssd", "ssd_norm_w", "s5_a_re", "s5_a_im", "s5_log_dt", "s5_b_re", "s5_b_im", "s5_c_re",
             "s5_c_im", "s5_d", "w_glu", "b_glu", "w_out", "ln2_g", "ln2_b", "ffn2_w1", "ffn2_w3", "ffn2_w2", "ln3_g",
             "ln3_b"]
    return (out["loss"], out["grad_x"], *[out[p + n] for p in ("grad_", "delta_", "new_m_", "new_v_") for n in order])
```

```python
import functools
import math

import jax
import jax.numpy as jnp
from jax import lax
from jax.experimental import pallas as pl
from jax.experimental.pallas import tpu as pltpu

F32 = jnp.float32
BF16 = jnp.bfloat16
HI = lax.Precision.HIGHEST
MESH = pl.DeviceIdType.MESH

N_DEV = 8
D_MODEL = 1024
D_FF = 2816
N_MOD = 9
SSD_WIDTH = 512
SSD_HEADS = 8
SSD_HEAD_DIM = 64
SSD_GROUPS = 2
SSD_STATE = 128
SSD_CHUNK = 128
GROUP_COLS = SSD_WIDTH // SSD_GROUPS
HEADS_PER_GROUP = SSD_HEADS // SSD_GROUPS
CONV_K = 4
CONV_CH = 1024
S5_WIDTH = 512
S5_GROUPS = 32
S5_GROUP_CH = 16
S5_STATE = 64
S5_COLS = S5_GROUPS * S5_STATE
S5_Q = 4
S5_CHUNK = 1024
ALPHA = 2.0 ** 0.25
LN_EPS = 1e-5
LANE = 128
HALO = 8

P_XBC, P_Z, P_U, P_DT = 0, 1024, 1536, 2048
P_COLS = 2048 + SSD_GROUPS * LANE
IN_COLS = SSD_WIDTH + CONV_CH + SSD_HEADS + S5_WIDTH

ADAM_LR, ADAM_B1, ADAM_B2, ADAM_EPS, ADAM_WD, ADAM_STEP = 0.001, 0.9, 0.999, 1e-08, 0.01, 10

VMEM_LIMIT = 56 * 1024 * 1024


def _cp(*sem):
    return pltpu.CompilerParams(dimension_semantics=sem if sem else None, vmem_limit_bytes=VMEM_LIMIT)


def _dg(a, b, ca, cb):
    return lax.dot_general(a.astype(BF16), b.astype(BF16), (((ca,), (cb,)), ((), ())), preferred_element_type=F32)


@jax.custom_vjp
def bdot_nn(a, b):
    return _dg(a, b, 1, 0)


bdot_nn.defvjp(lambda a, b: (_dg(a, b, 1, 0), (a, b)),
               lambda r, g: (_dg(g, r[1], 1, 1), _dg(r[0], g, 0, 0)))


@jax.custom_vjp
def bdot_nt(a, b):
    return _dg(a, b, 1, 1)


bdot_nt.defvjp(lambda a, b: (_dg(a, b, 1, 1), (a, b)),
               lambda r, g: (_dg(g, r[1], 1, 0), _dg(g, r[0], 0, 0)))


@jax.custom_vjp
def bdot_tn(a, b):
    return _dg(a, b, 0, 0)


bdot_tn.defvjp(lambda a, b: (_dg(a, b, 0, 0), (a, b)),
               lambda r, g: (_dg(r[1], g, 1, 1), _dg(r[0], g, 1, 0)))


def _take_col(z):
    @jax.custom_vjp
    def take(x):
        return x[:, z:z + 1]

    def bwd(shape, g):
        hot = (lax.broadcasted_iota(jnp.int32, (1, shape[1]), 1) == z).astype(F32)
        return (g * hot,)

    take.defvjp(lambda x: (x[:, z:z + 1], x.shape), bwd)
    return take


def _take_row(z):
    @jax.custom_vjp
    def take(x):
        return x[z:z + 1, :]

    def bwd(shape, g):
        hot = (lax.broadcasted_iota(jnp.int32, (shape[0], 1), 0) == z).astype(F32)
        return (hot * g,)

    take.defvjp(lambda x: (x[z:z + 1, :], x.shape), bwd)
    return take


def _view(a):
    return a if isinstance(a, tuple) else (a, 0, a.shape[1])


def _col_spec(view, rows, width, index):
    _, off, _ = view
    assert off % width == 0
    return pl.BlockSpec((rows, width), lambda *g: (index(*g)[0], off // width + index(*g)[1]))


def _rw_in_specs(rows, bps, gps, tm, tps):
    specs = [_col_spec(_view(r), tm, _view(r)[2], lambda i: (i, 0)) for r in rows]
    specs += [pl.BlockSpec((1, 1, b.shape[2]), lambda i: (i // tps, 0, 0)) for b in bps]
    specs += [pl.BlockSpec(g.shape, lambda i, nd=g.ndim: (0,) * nd) for g in gps]
    return specs


def _rw_vals(refs, nr, nb, ng):
    vals = [r[...] for r in refs[:nr]]
    vals += [b[0] for b in refs[nr:nr + nb]]
    vals += [g[...] for g in refs[nr + nb:nr + nb + ng]]
    return vals


def rowwise_fwd(name, f, rows, bps, gps, outs, seq, tm):
    t = _view(rows[0])[0].shape[0]
    tps = seq // tm
    nr, nb, ng = len(rows), len(bps), len(gps)

    def body(*refs):
        res = f(*_rw_vals(refs, nr, nb, ng))
        for o, v in zip(refs[nr + nb + ng:], res):
            o[...] = v.astype(o.dtype)

    return pl.pallas_call(
        body, name=name, grid=(t // tm,),
        in_specs=_rw_in_specs(rows, bps, gps, tm, tps),
        out_specs=[pl.BlockSpec((tm, c), lambda i: (i, 0)) for c, _ in outs],
        out_shape=[jax.ShapeDtypeStruct((t, c), d) for c, d in outs],
        compiler_params=_cp("arbitrary"),
    )(*[_view(r)[0] for r in rows], *bps, *gps)


def rowwise_bwd(name, f, rows, bps, gps, douts, seq, tm, row_grads, add_rows=None):
    add_rows = add_rows or {}
    t = _view(rows[0])[0].shape[0]
    tps = seq // tm
    nr, nb, ng, nd = len(rows), len(bps), len(gps), len(douts)
    want = [k for k in range(nr) if row_grads[k] is not None]
    adds = sorted(add_rows)
    n_in = nr + nb + ng + nd + len(adds)

    def body(*refs):
        vals = _rw_vals(refs, nr, nb, ng)
        dvals = tuple(r[...] for r in refs[nr + nb + ng:nr + nb + ng + nd])
        add_refs = dict(zip(adds, refs[nr + nb + ng + nd:n_in]))
        out_refs = refs[n_in:]
        _, pull = jax.vjp(f, *vals)
        grads = pull(dvals)
        i = pl.program_id(0)
        for o, k in zip(out_refs, want):
            g = grads[k]
            if k in add_refs:
                g = g + add_refs[k][...]
            o[...] = g.astype(o.dtype)
        for j in range(nb):
            o = out_refs[len(want) + j]

            @pl.when(i % tps == 0)
            def _(o=o):
                o[...] = jnp.zeros_like(o)

            o[0] = o[0] + grads[nr + j]
        for j in range(ng):
            o = out_refs[len(want) + nb + j]

            @pl.when(i == 0)
            def _(o=o):
                o[...] = jnp.zeros_like(o)

            o[...] = o[...] + grads[nr + nb + j]

    in_specs = _rw_in_specs(rows, bps, gps, tm, tps)
    in_specs += [_col_spec(_view(d), tm, _view(d)[2], lambda i: (i, 0)) for d in douts]
    in_specs += [pl.BlockSpec((tm, add_rows[k].shape[1]), lambda i: (i, 0)) for k in adds]
    out_specs = [pl.BlockSpec((tm, _view(rows[k])[2]), lambda i: (i, 0)) for k in want]
    out_shape = [jax.ShapeDtypeStruct((t, _view(rows[k])[2]), row_grads[k]) for k in want]
    out_specs += [pl.BlockSpec((1, 1, b.shape[2]), lambda i: (i // tps, 0, 0)) for b in bps]
    out_shape += [jax.ShapeDtypeStruct(b.shape, F32) for b in bps]
    out_specs += [pl.BlockSpec(g.shape, lambda i, n=g.ndim: (0,) * n) for g in gps]
    out_shape += [jax.ShapeDtypeStruct(g.shape, F32) for g in gps]
    res = pl.pallas_call(
        body, name=name, grid=(t // tm,), in_specs=in_specs, out_specs=out_specs, out_shape=out_shape,
        compiler_params=_cp("arbitrary"),
    )(*[_view(r)[0] for r in rows], *bps, *gps, *[_view(d)[0] for d in douts], *[add_rows[k] for k in adds])
    nw = len(want)
    return res[:nw], res[nw:nw + nb], res[nw + nb:]


def mm_nn(name, xs, ws, tm, tn, out_dtype=F32, ride=None):
    views = [_view(x) for x in xs]
    t, n, k = views[0][0].shape[0], ws[0].shape[1], len(xs)

    def body(*refs):
        acc = _dg(refs[0][...], refs[k][...], 1, 0)
        for i in range(1, k):
            acc = acc + _dg(refs[i][...], refs[k + i][...], 1, 0)
        refs[2 * k][...] = acc.astype(out_dtype)

    in_specs = [_col_spec(v, tm, v[2], lambda i, j: (i, 0)) for v in views]
    in_specs += [pl.BlockSpec((w.shape[0], tn), lambda i, j: (0, j)) for w in ws]
    out_spec = pl.BlockSpec((tm, tn), lambda i, j: (i, j))
    out_shape = jax.ShapeDtypeStruct((t, n), out_dtype)
    if ride is not None:
        (res,), landed = hosted_call(body, name=name, grid=(t // tm, n // tn), in_specs=in_specs, out_specs=[out_spec],
                                     out_shape=[out_shape], args=(*[v[0] for v in views], *ws), ride=ride)
        return res, landed
    return pl.pallas_call(
        body, name=name, grid=(t // tm, n // tn), in_specs=in_specs, out_specs=out_spec, out_shape=out_shape,
        compiler_params=_cp("parallel", "parallel"),
    )(*[v[0] for v in views], *ws)


def mm_nt(name, dys, ws, tm, tk, out_dtype=F32, ride=None):
    views = [_view(d) for d in dys]
    t, kk, k = views[0][0].shape[0], ws[0].shape[0], len(dys)

    def body(*refs):
        acc = _dg(refs[0][...], refs[k][...], 1, 1)
        for i in range(1, k):
            acc = acc + _dg(refs[i][...], refs[k + i][...], 1, 1)
        refs[2 * k][...] = acc.astype(out_dtype)

    in_specs = [_col_spec(v, tm, v[2], lambda i, j: (i, 0)) for v in views]
    in_specs += [pl.BlockSpec((tk, w.shape[1]), lambda i, j: (j, 0)) for w in ws]
    out_spec = pl.BlockSpec((tm, tk), lambda i, j: (i, j))
    out_shape = jax.ShapeDtypeStruct((t, kk), out_dtype)
    if ride is not None:
        (res,), landed = hosted_call(body, name=name, grid=(t // tm, kk // tk), in_specs=in_specs, out_specs=[out_spec],
                                     out_shape=[out_shape], args=(*[v[0] for v in views], *ws), ride=ride)
        return res, landed
    return pl.pallas_call(
        body, name=name, grid=(t // tm, kk // tk), in_specs=in_specs, out_specs=out_spec, out_shape=out_shape,
        compiler_params=_cp("parallel", "parallel"),
    )(*[v[0] for v in views], *ws)


def mm_tn(name, x, dy, tk, tn, tt, out_dtype=F32, ride=None):
    xv, dv = _view(x), _view(dy)
    t, kk, n = xv[0].shape[0], xv[2], dv[2]
    steps = t // tt

    def body(x_ref, d_ref, o_ref, acc_ref):
        @pl.when(pl.program_id(2) == 0)
        def _():
            acc_ref[...] = jnp.zeros_like(acc_ref)

        acc_ref[...] += _dg(x_ref[...], d_ref[...], 0, 0)

        @pl.when(pl.program_id(2) == steps - 1)
        def _():
            o_ref[...] = acc_ref[...].astype(out_dtype)

    in_specs = [_col_spec(xv, tt, tk, lambda a, b, c: (c, a)), _col_spec(dv, tt, tn, lambda a, b, c: (c, b))]
    out_spec = pl.BlockSpec((tk, tn), lambda a, b, c: (a, b))
    out_shape = jax.ShapeDtypeStruct((kk, n), out_dtype)
    if ride is not None:
        (res,), landed = hosted_call(body, name=name, grid=(kk // tk, n // tn, steps), in_specs=in_specs,
                                     out_specs=[out_spec], out_shape=[out_shape], scratch=[pltpu.VMEM((tk, tn), F32)],
                                     args=(xv[0], dv[0]), ride=ride)
        return res, landed
    return pl.pallas_call(
        body, name=name, grid=(kk // tk, n // tn, steps), in_specs=in_specs, out_specs=out_spec, out_shape=out_shape,
        scratch_shapes=[pltpu.VMEM((tk, tn), F32)],
        compiler_params=_cp("parallel", "parallel", "arbitrary"),
    )(xv[0], dv[0])


def _silu(x):
    return x * jax.nn.sigmoid(x)


def f_modulate(x, sc, sh):
    return (x * (1.0 + sc) + sh,)


def _res_ln(coef):
    def f(x, y, g, lg, lb):
        r = ALPHA * x + (coef * g) * y
        mu = jnp.mean(r, axis=-1, keepdims=True)
        d = r - mu
        var = jnp.mean(d * d, axis=-1, keepdims=True)
        return (d * lax.rsqrt(var + LN_EPS) * lg + lb,)
    return f


def f_glu(y, w, b):
    g = jax.nn.gelu(y)
    return (g * jax.nn.sigmoid(bdot_nn(g, w) + b),)


def _shift_down(x, halo, k):
    if k == 0:
        return x
    r = pltpu.roll(x, k, 0)
    hr = pltpu.roll(halo, k, 0)
    row = lax.broadcasted_iota(jnp.int32, (HALO, 1), 0)
    top = jnp.where(row < k, hr, r[:HALO])
    return jnp.concatenate([top, r[HALO:]], axis=0)


def _shift_up(x, halo, k):
    if k == 0:
        return x
    n = x.shape[0]
    r = pltpu.roll(x, n - k, 0)
    hr = pltpu.roll(halo, HALO - k, 0)
    row = lax.broadcasted_iota(jnp.int32, (HALO, 1), 0)
    bot = jnp.where(row >= HALO - k, hr, r[n - HALO:])
    return jnp.concatenate([r[:n - HALO], bot], axis=0)


def _conv_pre(x, halo, w, b):
    acc = x * w[CONV_K - 1:CONV_K, :] + b
    for k in range(1, CONV_K):
        acc = acc + _shift_down(x, halo, k) * w[CONV_K - 1 - k:CONV_K - k, :]
    return acc


def _rows_before(width, tm):
    return pl.BlockSpec((HALO, width), lambda i: (jnp.maximum(i * (tm // HALO) - 1, 0), 0))


def conv_fwd(proj, w, b, seq, tm):
    t = proj.shape[0]
    tps = seq // tm

    def body(x_ref, h_ref, w_ref, b_ref, o_ref):
        first = (pl.program_id(0) % tps == 0)
        halo = jnp.where(first, 0.0, h_ref[...])
        o_ref[...] = _silu(_conv_pre(x_ref[...], halo, w_ref[...], b_ref[...]))

    return pl.pallas_call(
        body, name="conv_fwd", grid=(t // tm,),
        in_specs=[pl.BlockSpec((tm, CONV_CH), lambda i: (i, 0)), _rows_before(CONV_CH, tm),
                  pl.BlockSpec((CONV_K, CONV_CH), lambda i: (0, 0)), pl.BlockSpec((1, CONV_CH), lambda i: (0, 0))],
        out_specs=pl.BlockSpec((tm, CONV_CH), lambda i: (i, 0)),
        out_shape=jax.ShapeDtypeStruct((t, CONV_CH), F32),
        compiler_params=_cp("arbitrary"),
    )(proj, proj, w, b)


def conv_bwd_pre(proj, w, b, dxs, dbm, dcm, seq, tm):
    t = proj.shape[0]
    tps = seq // tm

    def body(x_ref, h_ref, w_ref, b_ref, d1, d2, d3, dp_ref, dw_ref, db_ref):
        i = pl.program_id(0)
        halo = jnp.where(i % tps == 0, 0.0, h_ref[...])
        x = x_ref[...]
        pre = _conv_pre(x, halo, w_ref[...], b_ref[...])
        sg = jax.nn.sigmoid(pre)
        dout = jnp.concatenate([d1[...], d2[...], d3[...]], axis=1)
        dp = dout * (sg * (1.0 + pre * (1.0 - sg)))
        dp_ref[...] = dp

        @pl.when(i == 0)
        def _():
            dw_ref[...] = jnp.zeros_like(dw_ref)
            db_ref[...] = jnp.zeros_like(db_ref)

        db_ref[...] += jnp.sum(dp, axis=0, keepdims=True)
        for k in range(CONV_K):
            j = CONV_K - 1 - k
            dw_ref[j:j + 1, :] += jnp.sum(dp * _shift_down(x, halo, k), axis=0, keepdims=True)

    return pl.pallas_call(
        body, name="conv_bwd_pre", grid=(t // tm,),
        in_specs=[pl.BlockSpec((tm, CONV_CH), lambda i: (i, 0)), _rows_before(CONV_CH, tm),
                  pl.BlockSpec((CONV_K, CONV_CH), lambda i: (0, 0)), pl.BlockSpec((1, CONV_CH), lambda i: (0, 0)),
                  pl.BlockSpec((tm, 512), lambda i: (i, 0)), pl.BlockSpec((tm, 256), lambda i: (i, 0)),
                  pl.BlockSpec((tm, 256), lambda i: (i, 0))],
        out_specs=[pl.BlockSpec((tm, CONV_CH), lambda i: (i, 0)), pl.BlockSpec((CONV_K, CONV_CH), lambda i: (0, 0)),
                   pl.BlockSpec((1, CONV_CH), lambda i: (0, 0))],
        out_shape=[jax.ShapeDtypeStruct((t, CONV_CH), F32), jax.ShapeDtypeStruct((CONV_K, CONV_CH), F32),
                   jax.ShapeDtypeStruct((1, CONV_CH), F32)],
        compiler_params=_cp("arbitrary"),
    )(proj, proj, w, b, dxs, dbm, dcm)


def conv_bwd_x(dpre, w, seq, tm):
    t = dpre.shape[0]
    tps = seq // tm
    blocks = tm // HALO
    last = t // HALO - 1

    def body(d_ref, h_ref, w_ref, o_ref):
        halo = jnp.where(pl.program_id(0) % tps == tps - 1, 0.0, h_ref[...])
        d = d_ref[...]
        w = w_ref[...]
        acc = d * w[CONV_K - 1:CONV_K, :]
        for k in range(1, CONV_K):
            acc = acc + _shift_up(d, halo, k) * w[CONV_K - 1 - k:CONV_K - k, :]
        o_ref[...] = acc

    return pl.pallas_call(
        body, name="conv_bwd_x", grid=(t // tm,),
        in_specs=[pl.BlockSpec((tm, CONV_CH), lambda i: (i, 0)),
                  pl.BlockSpec((HALO, CONV_CH), lambda i: (jnp.minimum((i + 1) * blocks, last), 0)),
                  pl.BlockSpec((CONV_K, CONV_CH), lambda i: (0, 0))],
        out_specs=pl.BlockSpec((tm, CONV_CH), lambda i: (i, 0)),
        out_shape=jax.ShapeDtypeStruct((t, CONV_CH), F32),
        compiler_params=_cp("arbitrary"),
    )(dpre, dpre, w)


def _softplus(x):
    return jnp.maximum(x, 0.0) + jnp.log1p(jnp.exp(-jnp.abs(x)))


def _ssd_chunk(xs, bg, cg, dtr, zz, hp, dtb, alog, dcol, nw):
    l = xs.shape[0]
    row = lax.broadcasted_iota(jnp.int32, (l, l), 0)
    col = lax.broadcasted_iota(jnp.int32, (l, l), 1)
    causal = row >= col
    tril = causal.astype(F32)
    expand = (lax.broadcasted_iota(jnp.int32, (LANE, GROUP_COLS), 1) // SSD_HEAD_DIM
              == lax.broadcasted_iota(jnp.int32, (LANE, GROUP_COLS), 0)).astype(F32)
    head_of_col = lax.broadcasted_iota(jnp.int32, (1, GROUP_COLS), 1) // SSD_HEAD_DIM
    last_row = (lax.broadcasted_iota(jnp.int32, (l, 1), 0) == l - 1).astype(F32)

    dtc = _softplus(dtr + dtb)
    a_c = dtc * (-jnp.exp(alog))
    acs_c = jnp.dot(tril, a_c, precision=HI, preferred_element_type=F32)
    dt_e = jnp.dot(dtc, expand, precision=HI, preferred_element_type=F32)
    acs_e = jnp.dot(acs_c, expand, precision=HI, preferred_element_type=F32)
    alast_e = jnp.sum(acs_e * last_row, axis=0, keepdims=True)
    x = xs * dt_e
    states = bdot_tn(bg, x * jnp.exp(alast_e - acs_e))
    h_next = jnp.exp(alast_e) * hp + states
    d_e = jnp.sum(dcol * expand, axis=0, keepdims=True)
    y = bdot_nn(cg, hp) * jnp.exp(acs_e) + d_e * xs
    cb = bdot_nt(cg, bg)
    acs_t = acs_c.T
    for z in range(HEADS_PER_GROUP):
        seg = _take_col(z)(acs_c) - _take_row(z)(acs_t)
        lmat = jnp.exp(jnp.where(causal, seg, -1e30))
        y = y + bdot_nn(cb * lmat, x * (head_of_col == z).astype(F32))
    yz = y * _silu(zz)
    ms = jnp.mean(yz * yz, axis=-1, keepdims=True)
    return yz * lax.rsqrt(ms + LN_EPS) * nw, h_next


SSD_SUB = 2
SSD_ROWS = SSD_SUB * SSD_CHUNK


def _ssd_in_specs(steps, rev):
    def tok(b, c):
        return b * steps + (steps - 1 - c if rev else c)

    whole = lambda *shape: pl.BlockSpec(shape, lambda b, c: (0,) * len(shape))
    both = SSD_GROUPS * SSD_STATE
    return [
        pl.BlockSpec((SSD_ROWS, SSD_WIDTH), lambda b, c: (tok(b, c), 0)),
        pl.BlockSpec((SSD_ROWS, both), lambda b, c: (tok(b, c), SSD_WIDTH // both)),
        pl.BlockSpec((SSD_ROWS, both), lambda b, c: (tok(b, c), SSD_WIDTH // both + 1)),
        pl.BlockSpec((SSD_ROWS, SSD_GROUPS * LANE), lambda b, c: (tok(b, c), P_DT // (SSD_GROUPS * LANE))),
        pl.BlockSpec((SSD_ROWS, SSD_WIDTH), lambda b, c: (tok(b, c), P_Z // SSD_WIDTH)),
        whole(SSD_GROUPS, 1, LANE), whole(SSD_GROUPS, 1, LANE), whole(SSD_GROUPS, LANE, 1),
        whole(SSD_GROUPS, 1, GROUP_COLS),
    ], tok


def _piece(ref, s, g, width):
    return ref[s * SSD_CHUNK:(s + 1) * SSD_CHUNK, g * width:(g + 1) * width]


def ssd_fwd(xc, proj, dtb, alog, dcol, nw, bsz, seq, ride=None):
    t = xc.shape[0]
    nc = seq // SSD_CHUNK
    steps = nc // SSD_SUB
    in_specs, tok = _ssd_in_specs(steps, False)

    def body(xs, bm, cm, dtr, zz, dtb_r, alog_r, dcol_r, nw_r, y_ref, hp_ref, h_scr):
        @pl.when(pl.program_id(1) == 0)
        def _():
            h_scr[...] = jnp.zeros_like(h_scr)

        for g in range(SSD_GROUPS):
            h = h_scr[g]
            for s in range(SSD_SUB):
                hp_ref[g, 0, s] = h
                y, h = _ssd_chunk(_piece(xs, s, g, GROUP_COLS), _piece(bm, s, g, SSD_STATE),
                                  _piece(cm, s, g, SSD_STATE), _piece(dtr, s, g, LANE), _piece(zz, s, g, GROUP_COLS), h,
                                  dtb_r[g], alog_r[g], dcol_r[g], nw_r[g])
                y_ref[s * SSD_CHUNK:(s + 1) * SSD_CHUNK, g * GROUP_COLS:(g + 1) * GROUP_COLS] = y
            h_scr[g] = h

    return hosted_call(
        body, name="ssd_fwd", grid=(bsz, steps), in_specs=in_specs,
        out_specs=[pl.BlockSpec((SSD_ROWS, SSD_WIDTH), lambda b, c: (tok(b, c), 0)),
                   pl.BlockSpec((SSD_GROUPS, 1, SSD_SUB, SSD_STATE, GROUP_COLS), lambda b, c: (0, b, c, 0, 0))],
        out_shape=[jax.ShapeDtypeStruct((t, SSD_WIDTH), F32),
                   jax.ShapeDtypeStruct((SSD_GROUPS, bsz, nc, SSD_STATE, GROUP_COLS), F32)],
        scratch=[pltpu.VMEM((SSD_GROUPS, SSD_STATE, GROUP_COLS), F32)],
        args=(xc, xc, xc, proj, proj, dtb, alog, dcol, nw), ride=ride)


def ssd_bwd(xc, proj, dtb, alog, dcol, nw, hprev, dy, bsz, seq):
    t = xc.shape[0]
    nc = seq // SSD_CHUNK
    steps = nc // SSD_SUB
    in_specs, tok = _ssd_in_specs(steps, True)
    in_specs += [pl.BlockSpec((SSD_GROUPS, 1, SSD_SUB, SSD_STATE, GROUP_COLS), lambda b, c: (0, b, steps - 1 - c, 0, 0)),
                 pl.BlockSpec((SSD_ROWS, SSD_WIDTH), lambda b, c: (tok(b, c), 0))]

    def body(xs, bm, cm, dtr, zz, dtb_r, alog_r, dcol_r, nw_r, hp_ref, dy_ref,
             dxs, dbm, dcm, ddt, dzz, ddtb, dalog, ddcol, dnw, dh_scr):
        b, c = pl.program_id(0), pl.program_id(1)

        @pl.when(c == 0)
        def _():
            dh_scr[...] = jnp.zeros_like(dh_scr)

        @pl.when((b == 0) & (c == 0))
        def _():
            for r in (ddtb, dalog, ddcol, dnw):
                r[...] = jnp.zeros_like(r)

        for g in range(SSD_GROUPS):
            wide = slice(g * GROUP_COLS, (g + 1) * GROUP_COLS)
            state = slice(g * SSD_STATE, (g + 1) * SSD_STATE)
            dh = dh_scr[g]
            for s in reversed(range(SSD_SUB)):
                rows = slice(s * SSD_CHUNK, (s + 1) * SSD_CHUNK)
                _, pull = jax.vjp(_ssd_chunk, xs[rows, wide], bm[rows, state], cm[rows, state],
                                  _piece(dtr, s, g, LANE), zz[rows, wide], hp_ref[g, 0, s],
                                  dtb_r[g], alog_r[g], dcol_r[g], nw_r[g])
                d = pull((dy_ref[rows, wide], dh))
                dxs[rows, wide], dbm[rows, state], dcm[rows, state], dzz[rows, wide] = d[0], d[1], d[2], d[4]
                ddt[rows, g * LANE:(g + 1) * LANE] = d[3]
                dh = d[5]
                ddtb[g] += d[6]
                dalog[g] += d[7]
                ddcol[g] += d[8]
                dnw[g] += d[9]
            dh_scr[g] = dh

    def tile(w):
        return pl.BlockSpec((SSD_ROWS, w), lambda b, c: (tok(b, c), 0))

    whole = lambda *shape: pl.BlockSpec(shape, lambda b, c: (0,) * len(shape))
    return pl.pallas_call(
        body, name="ssd_bwd", grid=(bsz, steps), in_specs=in_specs,
        out_specs=[tile(SSD_WIDTH), tile(2 * SSD_STATE), tile(2 * SSD_STATE), tile(2 * LANE), tile(SSD_WIDTH),
                   whole(SSD_GROUPS, 1, LANE), whole(SSD_GROUPS, 1, LANE), whole(SSD_GROUPS, LANE, 1),
                   whole(SSD_GROUPS, 1, GROUP_COLS)],
        out_shape=[jax.ShapeDtypeStruct((t, SSD_WIDTH), F32), jax.ShapeDtypeStruct((t, 2 * SSD_STATE), F32),
                   jax.ShapeDtypeStruct((t, 2 * SSD_STATE), F32), jax.ShapeDtypeStruct((t, 2 * LANE), F32),
                   jax.ShapeDtypeStruct((t, SSD_WIDTH), F32),
                   jax.ShapeDtypeStruct((SSD_GROUPS, 1, LANE), F32), jax.ShapeDtypeStruct((SSD_GROUPS, 1, LANE), F32),
                   jax.ShapeDtypeStruct((SSD_GROUPS, LANE, 1), F32),
                   jax.ShapeDtypeStruct((SSD_GROUPS, 1, GROUP_COLS), F32)],
        scratch_shapes=[pltpu.VMEM((SSD_GROUPS, SSD_STATE, GROUP_COLS), F32)],
        compiler_params=_cp("arbitrary", "arbitrary"),
    )(xc, xc, xc, proj, proj, dtb, alog, dcol, nw, hprev, dy)


def _disc_a(a_re, a_im, log_dt):
    dt = jnp.exp(log_dt)
    mag = jnp.exp(dt * a_re)
    ab_re, ab_im = mag * jnp.cos(dt * a_im), mag * jnp.sin(dt * a_im)
    den = a_re * a_re + a_im * a_im
    nr, ni = ab_re - 1.0, ab_im
    f_re, f_im = (nr * a_re + ni * a_im) / den, (ni * a_re - nr * a_im) / den
    return ab_re, ab_im, f_re, f_im


def _disc_b(f_re, f_im, b_re, b_im):
    return f_re * b_re - f_im * b_im, f_re * b_im + f_im * b_re


def _whole(f, name, args, outs):
    def body(*refs):
        res = f(*[r[...] for r in refs[:len(args)]])
        for o, v in zip(refs[len(args):], res):
            o[...] = v

    return pl.pallas_call(body, name=name, out_shape=[jax.ShapeDtypeStruct(s, F32) for s in outs])(*args)


def _whole_vjp(f, name, args, cts):
    def body(*refs):
        vals = [r[...] for r in refs[:len(args)]]
        _, pull = jax.vjp(f, *vals)
        res = pull(tuple(r[...] for r in refs[len(args):len(args) + len(cts)]))
        for o, v in zip(refs[len(args) + len(cts):], res):
            o[...] = v

    return pl.pallas_call(body, name=name, out_shape=[jax.ShapeDtypeStruct(a.shape, F32) for a in args])(*args, *cts)


S5_SUB = 8
S5_STEPS = 3


def s5_tables(lam_re, lam_im):
    rows = S5_STEPS * S5_SUB

    def body(lr_ref, li_ref, sf_re, sf_im, sb_re, sb_im, cf_re, cf_im, cb_re, cb_im):
        lr, li = lr_ref[...], li_ref[...]

        def power(k):
            m = jnp.exp(k * lr)
            return m * jnp.cos(k * li), m * jnp.sin(k * li)

        srow = lax.broadcasted_iota(jnp.int32, (rows, 1), 0)
        k = jnp.left_shift(1, srow // S5_SUB)
        tt = srow % S5_SUB
        pr, pi = power(k.astype(F32))
        fwd, bwd = tt >= k, tt < S5_SUB - k
        sf_re[...], sf_im[...] = jnp.where(fwd, pr, 0.0), jnp.where(fwd, pi, 0.0)
        sb_re[...], sb_im[...] = jnp.where(bwd, pr, 0.0), jnp.where(bwd, pi, 0.0)
        trow = lax.broadcasted_iota(jnp.int32, (S5_SUB, 1), 0)
        cf_re[...], cf_im[...] = power((trow + 1).astype(F32))
        cb_re[...], cb_im[...] = power((S5_SUB - trow).astype(F32))

    shp = [jax.ShapeDtypeStruct((rows, S5_COLS), F32)] * 4 + [jax.ShapeDtypeStruct((S5_SUB, S5_COLS), F32)] * 4
    return pl.pallas_call(body, name="s5_tables", out_shape=shp)(lam_re, lam_im)


def _s5_coefs(steps_re, steps_im, carry_re, carry_im, reverse):
    sign = -1.0 if reverse else 1.0
    steps = [(steps_re[s * S5_SUB:(s + 1) * S5_SUB, :], sign * steps_im[s * S5_SUB:(s + 1) * S5_SUB, :])
             for s in range(S5_STEPS)]
    return steps, (carry_re[...], sign * carry_im[...])


def _s5_block_scan(ar, ai, coefs, cr, ci, reverse):
    steps, (qr, qi) = coefs
    for s, (pr, pi) in enumerate(steps):
        shift = S5_SUB - (1 << s) if reverse else (1 << s)
        sr, si = pltpu.roll(ar, shift, 0), pltpu.roll(ai, shift, 0)
        ar, ai = ar + pr * sr - pi * si, ai + pr * si + pi * sr
    br, bi = jnp.broadcast_to(cr, ar.shape), jnp.broadcast_to(ci, ai.shape)
    return ar + qr * br - qi * bi, ai + qr * bi + qi * br


def _s5_specs(n5, rev):
    def tok(q, b, c):
        return b * n5 + (n5 - 1 - c if rev else c)

    qcols = S5_COLS // S5_Q
    specs = [
        pl.BlockSpec((S5_CHUNK, LANE), lambda q, b, c: (tok(q, b, c), P_U // LANE + q)),
        pl.BlockSpec((1, LANE, qcols), lambda q, b, c: (q, 0, 0)),
        pl.BlockSpec((1, LANE, qcols), lambda q, b, c: (q, 0, 0)),
        pl.BlockSpec((1, qcols, LANE), lambda q, b, c: (q, 0, 0)),
        pl.BlockSpec((1, qcols, LANE), lambda q, b, c: (q, 0, 0)),
        pl.BlockSpec((S5_STEPS * S5_SUB, qcols), lambda q, b, c: (0, q)),
        pl.BlockSpec((S5_STEPS * S5_SUB, qcols), lambda q, b, c: (0, q)),
        pl.BlockSpec((S5_SUB, qcols), lambda q, b, c: (0, q)),
        pl.BlockSpec((S5_SUB, qcols), lambda q, b, c: (0, q)),
        pl.BlockSpec((1, 1, LANE), lambda q, b, c: (q, 0, 0)),
    ]
    return specs, tok, qcols


def s5_fwd(proj, wb_re, wb_im, wc_re, wc_im, sf_re, sf_im, cf_re, cf_im, dvec, bsz, seq, ride=None):
    t = proj.shape[0]
    n5 = seq // S5_CHUNK
    in_specs, tok, qcols = _s5_specs(n5, False)

    def body(u_ref, wbr, wbi, wcr, wci, sfr, sfi, cfr, cfi, d_ref, y_ref, xr_ref, xi_ref, cr_scr, ci_scr):
        @pl.when(pl.program_id(2) == 0)
        def _():
            cr_scr[...] = jnp.zeros_like(cr_scr)
            ci_scr[...] = jnp.zeros_like(ci_scr)

        u = u_ref[...]
        bur, bui = _dg(u, wbr[0], 1, 0), _dg(u, wbi[0], 1, 0)
        coefs = _s5_coefs(sfr, sfi, cfr, cfi, False)
        cr, ci = cr_scr[...], ci_scr[...]
        for r in range(S5_CHUNK // S5_SUB):
            rows = slice(r * S5_SUB, (r + 1) * S5_SUB)
            xr, xi = _s5_block_scan(bur[rows], bui[rows], coefs, cr, ci, False)
            xr_ref[rows, :], xi_ref[rows, :] = xr, xi
            cr, ci = xr[S5_SUB - 1:, :], xi[S5_SUB - 1:, :]
        cr_scr[...], ci_scr[...] = cr, ci
        y_ref[...] = _dg(xr_ref[...], wcr[0], 1, 0) - _dg(xi_ref[...], wci[0], 1, 0) + u * d_ref[0]

    def tile(w):
        return pl.BlockSpec((S5_CHUNK, w), lambda q, b, c: (tok(q, b, c), q))

    return hosted_call(
        body, name="s5_fwd", grid=(S5_Q, bsz, n5), in_specs=in_specs,
        out_specs=[tile(LANE), tile(qcols), tile(qcols)],
        out_shape=[jax.ShapeDtypeStruct((t, S5_WIDTH), F32), jax.ShapeDtypeStruct((t, S5_COLS), F32),
                   jax.ShapeDtypeStruct((t, S5_COLS), F32)],
        scratch=[pltpu.VMEM((1, qcols), F32)] * 2,
        args=(proj, wb_re, wb_im, wc_re, wc_im, sf_re, sf_im, cf_re, cf_im, dvec), ride=ride)


def s5_bwd(proj, wb_re, wb_im, wc_re, wc_im, sb_re, sb_im, cb_re, cb_im, dvec, xr_all, xi_all, dy, bsz, seq,
           ride=None):
    t = proj.shape[0]
    n5 = seq // S5_CHUNK
    in_specs, tok, qcols = _s5_specs(n5, True)
    blocks = S5_CHUNK // HALO

    def prev_rows(q, b, c):
        return (jnp.maximum(tok(q, b, c) * blocks - 1, 0), q)

    in_specs += [pl.BlockSpec((S5_CHUNK, qcols), lambda q, b, c: (tok(q, b, c), q)),
                 pl.BlockSpec((S5_CHUNK, qcols), lambda q, b, c: (tok(q, b, c), q)),
                 pl.BlockSpec((HALO, qcols), prev_rows), pl.BlockSpec((HALO, qcols), prev_rows),
                 pl.BlockSpec((S5_CHUNK, LANE), lambda q, b, c: (tok(q, b, c), q))]

    def body(u_ref, wbr, wbi, wcr, wci, sbr, sbi, cbr, cbi, d_ref, xr_ref, xi_ref, pr_ref, pi_ref, dy_ref,
             du_ref, dwbr, dwbi, dwcr, dwci, dar, dai, dd_ref, gr_scr, gi_scr, gr_all, gi_all):
        b, c = pl.program_id(1), pl.program_id(2)

        @pl.when(c == 0)
        def _():
            gr_scr[...] = jnp.zeros_like(gr_scr)
            gi_scr[...] = jnp.zeros_like(gi_scr)

        @pl.when((b == 0) & (c == 0))
        def _():
            for r in (dwbr, dwbi, dwcr, dwci, dar, dai, dd_ref):
                r[...] = jnp.zeros_like(r)

        u, dy_v = u_ref[...], dy_ref[...]
        g0r, g0i = _dg(dy_v, wcr[0], 1, 1), -_dg(dy_v, wci[0], 1, 1)
        coefs = _s5_coefs(sbr, sbi, cbr, cbi, True)
        cr, ci = gr_scr[...], gi_scr[...]
        for r in reversed(range(S5_CHUNK // S5_SUB)):
            rows = slice(r * S5_SUB, (r + 1) * S5_SUB)
            br, bi = _s5_block_scan(g0r[rows], g0i[rows], coefs, cr, ci, True)
            gr_all[rows, :], gi_all[rows, :] = br, bi
            cr, ci = br[:1, :], bi[:1, :]
        gr_scr[...], gi_scr[...] = cr, ci
        gr, gi = gr_all[...], gi_all[...]

        row = lax.broadcasted_iota(jnp.int32, (S5_CHUNK, 1), 0)
        xr, xi = xr_ref[...], xi_ref[...]
        is_first = (c == n5 - 1)
        hr = jnp.where(is_first, 0.0, pr_ref[...][HALO - 1:, :])
        hi = jnp.where(is_first, 0.0, pi_ref[...][HALO - 1:, :])
        xpr = jnp.where(row >= 1, pltpu.roll(xr, 1, 0), hr)
        xpi = jnp.where(row >= 1, pltpu.roll(xi, 1, 0), hi)
        dar[0] += jnp.sum(xpr * gr + xpi * gi, axis=0, keepdims=True)
        dai[0] += jnp.sum(xpr * gi - xpi * gr, axis=0, keepdims=True)
        du_ref[...] = _dg(gr, wbr[0], 1, 1) + _dg(gi, wbi[0], 1, 1) + dy_v * d_ref[0]
        dwbr[0] += _dg(u, gr, 0, 0)
        dwbi[0] += _dg(u, gi, 0, 0)
        dwcr[0] += _dg(xr, dy_v, 0, 0)
        dwci[0] -= _dg(xi, dy_v, 0, 0)
        dd_ref[0] += jnp.sum(dy_v * u, axis=0, keepdims=True)

    def acc(shape):
        return pl.BlockSpec((1,) + shape, lambda q, b, c: (q, 0, 0))

    return hosted_call(
        body, name="s5_bwd", grid=(S5_Q, bsz, n5), in_specs=in_specs,
        out_specs=[pl.BlockSpec((S5_CHUNK, LANE), lambda q, b, c: (tok(q, b, c), q)),
                   acc((LANE, qcols)), acc((LANE, qcols)), acc((qcols, LANE)), acc((qcols, LANE)),
                   acc((1, qcols)), acc((1, qcols)), acc((1, LANE))],
        out_shape=[jax.ShapeDtypeStruct((t, S5_WIDTH), F32),
                   jax.ShapeDtypeStruct((S5_Q, LANE, qcols), F32), jax.ShapeDtypeStruct((S5_Q, LANE, qcols), F32),
                   jax.ShapeDtypeStruct((S5_Q, qcols, LANE), F32), jax.ShapeDtypeStruct((S5_Q, qcols, LANE), F32),
                   jax.ShapeDtypeStruct((S5_Q, 1, qcols), F32), jax.ShapeDtypeStruct((S5_Q, 1, qcols), F32),
                   jax.ShapeDtypeStruct((S5_Q, 1, LANE), F32)],
        scratch=[pltpu.VMEM((1, qcols), F32)] * 2 + [pltpu.VMEM((S5_CHUNK, qcols), F32)] * 2,
        args=(proj, wb_re, wb_im, wc_re, wc_im, sb_re, sb_im, cb_re, cb_im, dvec, xr_all, xi_all, xr_all, xi_all, dy),
        ride=ride)


def _blockdiag_b(bb):
    b4 = bb.reshape(S5_Q, 8, S5_STATE, S5_GROUP_CH)
    eye = jnp.eye(8, dtype=bb.dtype)
    w = jnp.einsum("qgph,gk->qghkp", b4, eye)
    return w.reshape(S5_Q, LANE, S5_COLS // S5_Q)


def _unblock_b(dw):
    d = dw.reshape(S5_Q, 8, S5_GROUP_CH, 8, S5_STATE)
    d = jnp.einsum("qghgp->qgph", d)
    return d.reshape(S5_COLS, S5_GROUP_CH)


def _blockdiag_c(cc):
    c4 = cc.reshape(S5_Q, 8, S5_GROUP_CH, S5_STATE)
    eye = jnp.eye(8, dtype=cc.dtype)
    w = jnp.einsum("qghp,gk->qgpkh", c4, eye)
    return w.reshape(S5_Q, S5_COLS // S5_Q, LANE)


def _unblock_c(dw):
    d = dw.reshape(S5_Q, 8, S5_STATE, 8, S5_GROUP_CH)
    d = jnp.einsum("qgpgh->qghp", d)
    return d.reshape(S5_GROUPS, S5_GROUP_CH, S5_STATE)


def ada_fwd(c_all, w_loc, b_loc):
    def body(c_ref, w_ref, b_ref, o_ref):
        o_ref[...] = _dg(_silu(c_ref[...]), w_ref[...], 1, 0) + b_ref[...]

    return pl.pallas_call(body, name="ada_fwd",
                          out_shape=jax.ShapeDtypeStruct((c_all.shape[0], w_loc.shape[1]), F32),
                          compiler_params=_cp())(c_all, w_loc, b_loc)


def ada_bwd(c_all, dmod_all, dmod_cols):
    def body(c_ref, da_ref, dc_ref, gb_ref, gw_ref):
        gb_ref[...] = jnp.sum(da_ref[...], axis=0, keepdims=True)
        gw_ref[...] = _dg(_silu(c_ref[...]), dc_ref[...], 0, 0)

    return pl.pallas_call(body, name="ada_bwd",
                          out_shape=[jax.ShapeDtypeStruct((1, dmod_all.shape[1]), F32),
                                     jax.ShapeDtypeStruct((c_all.shape[1], dmod_cols.shape[1]), F32)],
                          compiler_params=_cp())(c_all, dmod_all, dmod_cols)


_FLIPS = [(0, 0, 1), (1, 0, 0), (0, 1, 0), (1, 1, 0), (1, 0, 1), (0, 1, 1), (1, 1, 1)]


def _exchange_ops(srcs, outs, sems, gather):
    n = len(srcs)
    send_sems, recv_sems, loc_sems = sems
    x, y, c = lax.axis_index("x"), lax.axis_index("y"), lax.axis_index("c")
    me = 4 * x + 2 * y + c
    peers = []
    for fx, fy, fc in _FLIPS:
        px, py, pc = (1 - x if fx else x), (1 - y if fy else y), (1 - c if fc else c)
        peers.append(((px, py, pc), 4 * px + 2 * py + pc))

    def copy(k, j, slot_src, slot_dst):
        src = srcs[k] if gather[k] else srcs[k].at[slot_src]
        return pltpu.make_async_remote_copy(src_ref=src, dst_ref=outs[k].at[slot_dst],
                                            send_sem=send_sems.at[k, j], recv_sem=recv_sems.at[k, j],
                                            device_id=peers[j][0], device_id_type=MESH)

    def local(k):
        own = srcs[k] if gather[k] else srcs[k].at[me]
        return pltpu.make_async_copy(own, outs[k].at[me], loc_sems.at[k])

    def start():
        for k in range(n):
            for j in range(N_DEV - 1):
                copy(k, j, peers[j][1], me).start()
            local(k).start()

    def wait():
        for k in range(n):
            for j in range(N_DEV - 1):
                copy(k, j, me, peers[j][1]).wait_recv()
        for k in range(n):
            for j in range(N_DEV - 1):
                copy(k, j, peers[j][1], me).wait_send()
            local(k).wait()

    return start, wait


def _gather_two_level(srcs, outs, sems):
    n = len(srcs)
    send_sems, recv_sems, loc_sems = sems
    x, y, c = lax.axis_index("x"), lax.axis_index("y"), lax.axis_index("c")
    slot = lambda px, py, pc: 4 * px + 2 * py + pc
    me, sibling = (x, y, c), (x, y, 1 - c)
    chips = [(1 - x, y), (x, 1 - y), (1 - x, 1 - y)]

    def copy(k, j, block, to, own=False):
        return pltpu.make_async_remote_copy(src_ref=srcs[k] if own else outs[k].at[slot(*block)],
                                            dst_ref=outs[k].at[slot(*block)],
                                            send_sem=send_sems.at[k, j], recv_sem=recv_sems.at[k, j],
                                            device_id=to, device_id_type=MESH)

    locs = [pltpu.make_async_copy(srcs[k], outs[k].at[slot(*me)], loc_sems.at[k]) for k in range(n)]
    for k in range(n):
        locs[k].start()
        copy(k, 0, me, sibling, own=True).start()
        for j, chip in enumerate(chips):
            copy(k, 1 + j, me, (*chip, c), own=True).start()
    for j, chip in enumerate(chips):
        for k in range(n):
            copy(k, 1 + j, (*chip, c), me).wait_recv()
            copy(k, 4 + j, (*chip, c), sibling).start()
    for k in range(n):
        copy(k, 0, sibling, me).wait_recv()
        for j, chip in enumerate(chips):
            copy(k, 4 + j, (*chip, 1 - c), me).wait_recv()
    for k in range(n):
        copy(k, 0, me, sibling, own=True).wait_send()
        for j, chip in enumerate(chips):
            copy(k, 1 + j, me, (*chip, c), own=True).wait_send()
            copy(k, 4 + j, (*chip, c), sibling).wait_send()
        locs[k].wait()


def gather_two_level(name, arrs):
    n = len(arrs)
    specs, shapes, sems = _exchange_parts(arrs, [True] * n)

    def body(*refs):
        _gather_two_level(refs[:n], refs[n:2 * n], refs[2 * n:])

    return pl.pallas_call(
        body, name=name, in_specs=specs, out_specs=specs, out_shape=shapes, scratch_shapes=sems,
        compiler_params=pltpu.CompilerParams(has_side_effects=True),
    )(*arrs)


def _exchange_parts(arrs, gather):
    n = len(arrs)
    any_spec = pl.BlockSpec(memory_space=pl.ANY)
    shapes = [jax.ShapeDtypeStruct(((N_DEV,) + a.shape) if g else a.shape, a.dtype) for a, g in zip(arrs, gather)]
    sems = [pltpu.SemaphoreType.DMA((n, N_DEV - 1)), pltpu.SemaphoreType.DMA((n, N_DEV - 1)),
            pltpu.SemaphoreType.DMA((n,))]
    return [any_spec] * n, shapes, sems


def exchange(name, arrs, gather):
    n = len(arrs)
    specs, shapes, sems = _exchange_parts(arrs, gather)

    def body(*refs):
        start, wait = _exchange_ops(refs[:n], refs[n:2 * n], refs[2 * n:], gather)
        start()
        wait()

    return pl.pallas_call(
        body, name=name, in_specs=specs, out_specs=specs, out_shape=shapes, scratch_shapes=sems,
        compiler_params=pltpu.CompilerParams(has_side_effects=True),
    )(*arrs)


def hosted_call(body, *, name, grid, in_specs, out_specs, out_shape, args, scratch=(), ride=None):
    sem = ("arbitrary",) * len(grid)
    if ride is None:
        res = pl.pallas_call(body, name=name, grid=grid, in_specs=in_specs, out_specs=out_specs, out_shape=out_shape,
                             scratch_shapes=list(scratch), compiler_params=_cp(*sem))(*args)
        return list(res), []
    arrs, gather = ride
    n, n_in, n_out, n_scr = len(arrs), len(in_specs), len(out_specs), len(scratch)
    specs, shapes, sems = _exchange_parts(arrs, gather)

    def both(*refs):
        ins, srcs = refs[:n_in], refs[n_in:n_in + n]
        outs, landed = refs[n_in + n:n_in + n + n_out], refs[n_in + n + n_out:n_in + 2 * n + n_out]
        scr, ex_sems = refs[n_in + 2 * n + n_out:n_in + 2 * n + n_out + n_scr], refs[n_in + 2 * n + n_out + n_scr:]
        start, wait = _exchange_ops(srcs, landed, ex_sems, gather)
        first = functools.reduce(lambda a, b: a & b, [pl.program_id(d) == 0 for d in range(len(grid))])
        last = functools.reduce(lambda a, b: a & b, [pl.program_id(d) == grid[d] - 1 for d in range(len(grid))])
        pl.when(first)(start)
        body(*ins, *outs, *scr)
        pl.when(last)(wait)

    res = pl.pallas_call(
        both, name=name, grid=grid, in_specs=list(in_specs) + specs, out_specs=list(out_specs) + specs,
        out_shape=list(out_shape) + shapes, scratch_shapes=list(scratch) + sems, compiler_params=_cp(*sem),
    )(*args, *arrs)
    return list(res[:n_out]), list(res[n_out:])


def sum_slots(name, slots, tr):
    _, r, c = slots.shape

    def body(s_ref, o_ref):
        acc = s_ref[0].astype(F32)
        for j in range(1, N_DEV):
            acc = acc + s_ref[j].astype(F32)
        o_ref[...] = acc

    return pl.pallas_call(
        body, name=name, grid=(r // tr,), in_specs=[pl.BlockSpec((N_DEV, tr, c), lambda i: (0, i, 0))],
        out_specs=pl.BlockSpec((tr, c), lambda i: (i, 0)), out_shape=jax.ShapeDtypeStruct((r, c), F32),
        compiler_params=_cp("parallel"),
    )(slots)


def adamw(name, g, w, m, v, tr, sel=None):
    slots = g.ndim >= 3
    r, c = w.shape
    c1, c2 = 1.0 - ADAM_B1 ** ADAM_STEP, 1.0 - ADAM_B2 ** ADAM_STEP

    def body(g_ref, w_ref, m_ref, v_ref, go, do, mo, vo):
        if slots:
            gg = g_ref[0].astype(F32)
            for j in range(1, N_DEV):
                gg = gg + g_ref[j].astype(F32)
        else:
            gg = g_ref[...]
        mn = ADAM_B1 * m_ref[...] + (1.0 - ADAM_B1) * gg
        vn = ADAM_B2 * v_ref[...] + (1.0 - ADAM_B2) * (gg * gg)
        go[...], mo[...], vo[...] = gg, mn, vn
        do[...] = -ADAM_LR * ((mn / c1) / (jnp.sqrt(vn / c2) + ADAM_EPS) + ADAM_WD * w_ref[...])

    blk = pl.BlockSpec((tr, c), lambda i: (i, 0))
    if g.ndim == 4:
        gspec = pl.BlockSpec((N_DEV, None, tr, c), lambda i: (0, sel, i, 0))
    else:
        gspec = pl.BlockSpec((N_DEV, tr, c), lambda i: (0, i, 0)) if slots else blk
    return pl.pallas_call(
        body, name=name, grid=(r // tr,), in_specs=[gspec, blk, blk, blk], out_specs=[blk] * 4,
        out_shape=[jax.ShapeDtypeStruct((r, c), F32)] * 4, compiler_params=_cp("parallel"),
    )(g, w, m, v)


def _lane_rows(n):
    return -(-n // (8 * LANE)) * 8


def _pack(arrs):
    pieces = []
    for a in arrs:
        n = math.prod(a.shape)
        flat = a.reshape(-1).astype(F32)
        pieces.append(jnp.pad(flat, (0, _lane_rows(n) * LANE - n)).reshape(_lane_rows(n), LANE))
    return jnp.concatenate(pieces, axis=0)


def _unpack(buf, shapes):
    out, off = [], 0
    for s in shapes:
        n = math.prod(s)
        out.append(buf[off:off + _lane_rows(n)].reshape(-1)[:n].reshape(s))
        off += _lane_rows(n)
    return out


FF_CHUNK = D_FF
DW_TOKENS = 2048
FFN_TM = 256


def _resident(shape):
    return pl.BlockSpec(shape, lambda i: (0,) * len(shape), pipeline_mode=pl.Buffered(1))


def _ffn_fwd(tag, x, sc, sh, g, w1, w3, w2, lg, lb, seq, tm, ride=None, target=None):
    t = x.shape[0]
    tm = min(FFN_TM, tm)
    tps = seq // tm
    ln = _res_ln(0.5)
    head = target is not None

    def body(x_ref, sc_ref, sh_ref, g_ref, lg_ref, lb_ref, w1_ref, w3_ref, w2_ref, *rest):
        if head:
            t_ref, y_ref, h_ref, a_ref, b_ref, f_ref, l_ref = rest
        else:
            y_ref, h_ref, a_ref, b_ref, f_ref = rest
        xv = x_ref[...]
        h = (xv * (1.0 + sc_ref[0]) + sh_ref[0]).astype(BF16)
        h_ref[...] = h
        acc = jnp.zeros((tm, D_MODEL), F32)
        for j in range(D_FF // FF_CHUNK):
            sl = slice(j * FF_CHUNK, (j + 1) * FF_CHUNK)
            a = _dg(h, w1_ref[sl, :], 1, 1)
            b = _dg(h, w3_ref[sl, :], 1, 1)
            a_ref[:, sl] = a
            b_ref[:, sl] = b
            acc = acc + _dg(_silu(a) * b, w2_ref[sl, :], 1, 0)
        f_ref[...] = acc
        y = ln(xv, acc, g_ref[0], lg_ref[...], lb_ref[...])[0]
        if head:
            @pl.when(pl.program_id(0) == 0)
            def _():
                l_ref[...] = jnp.zeros_like(l_ref)

            e = y - t_ref[...]
            y_ref[...] = e * (1.0 / D_MODEL)
            l_ref[...] += 0.5 * jnp.sum(jnp.mean(e * e, axis=-1, keepdims=True), axis=0, keepdims=True)
        else:
            y_ref[...] = y

    row = lambda c: pl.BlockSpec((tm, c), lambda i: (i, 0))
    per_seq = pl.BlockSpec((1, 1, D_MODEL), lambda i: (i // tps, 0, 0))
    vec = pl.BlockSpec((1, D_MODEL), lambda i: (0, 0))
    res, landed = hosted_call(
        body, name=tag + "_fwd", grid=(t // tm,),
        in_specs=[row(D_MODEL), per_seq, per_seq, per_seq, vec, vec,
                  _resident((D_FF, D_MODEL)), _resident((D_FF, D_MODEL)), _resident((D_FF, D_MODEL))]
        + ([row(D_MODEL)] if head else []),
        out_specs=[row(D_MODEL), row(D_MODEL), row(D_FF), row(D_FF), row(D_MODEL)]
        + ([pl.BlockSpec((1, 1), lambda i: (0, 0))] if head else []),
        out_shape=[jax.ShapeDtypeStruct((t, D_MODEL), F32), jax.ShapeDtypeStruct((t, D_MODEL), BF16),
                   jax.ShapeDtypeStruct((t, D_FF), F32), jax.ShapeDtypeStruct((t, D_FF), F32),
                   jax.ShapeDtypeStruct((t, D_MODEL), F32)] + ([jax.ShapeDtypeStruct((1, 1), F32)] if head else []),
        args=(x, sc, sh, g, lg, lb, w1, w3, w2) + ((target,) if head else ()), ride=ride)
    first = (res[0], res[5][0, 0]) if head else res[0]
    return first, tuple(res[1:5]), landed


def _ffn_bwd(tag, dy, x, sc, sh, g, w1, w3, w2, lg, lb, res, seq, tm, ride=None, chain=None):
    h, a, b, f = res
    t = x.shape[0]
    tmk = min(FFN_TM, tm)
    tps = seq // tmk
    ln = _res_ln(0.5)

    def body(dy_ref, x_ref, f_ref, a_ref, b_ref, sc_ref, sh_ref, g_ref, lg_ref, lb_ref, w1_ref, w3_ref, w2_ref,
             dx_ref, da_ref, db_ref, s_ref, df_ref, dsc_ref, dsh_ref, dg_ref, dlg_ref, dlb_ref):
        i = pl.program_id(0)

        @pl.when(i % tps == 0)
        def _():
            for r in (dsc_ref, dsh_ref, dg_ref):
                r[...] = jnp.zeros_like(r)

        @pl.when(i == 0)
        def _():
            dlg_ref[...] = jnp.zeros_like(dlg_ref)
            dlb_ref[...] = jnp.zeros_like(dlb_ref)

        xv = x_ref[...]
        _, pull = jax.vjp(ln, xv, f_ref[...], g_ref[0], lg_ref[...], lb_ref[...])
        dx_res, df, dg, dlg, dlb = pull((dy_ref[...],))
        dfb = df.astype(BF16)
        df_ref[...] = dfb
        dh = jnp.zeros((tmk, D_MODEL), F32)
        for j in range(D_FF // FF_CHUNK):
            sl = slice(j * FF_CHUNK, (j + 1) * FF_CHUNK)
            ds = _dg(dfb, w2_ref[sl, :], 1, 1)
            av, bv = a_ref[:, sl], b_ref[:, sl]
            sg = jax.nn.sigmoid(av)
            si = av * sg
            s_ref[:, sl] = (si * bv).astype(BF16)
            da = (ds * bv * (sg * (1.0 + av * (1.0 - sg)))).astype(BF16)
            db = (ds * si).astype(BF16)
            da_ref[:, sl] = da
            db_ref[:, sl] = db
            dh = dh + _dg(da, w1_ref[sl, :], 1, 0) + _dg(db, w3_ref[sl, :], 1, 0)
        dx_ref[...] = dx_res + dh * (1.0 + sc_ref[0])
        dsc_ref[0] += jnp.sum(dh * xv, axis=0, keepdims=True)
        dsh_ref[0] += jnp.sum(dh, axis=0, keepdims=True)
        dg_ref[0] += dg
        dlg_ref[...] += dlg
        dlb_ref[...] += dlb

    row = lambda c: pl.BlockSpec((tmk, c), lambda i: (i, 0))
    per_seq = pl.BlockSpec((1, 1, D_MODEL), lambda i: (i // tps, 0, 0))
    vec = pl.BlockSpec((1, D_MODEL), lambda i: (0, 0))
    seq_shape = jax.ShapeDtypeStruct(sc.shape, F32)
    vec_shape = jax.ShapeDtypeStruct((1, D_MODEL), F32)
    (dx, da, db, s, df, dsc, dsh, dg, dlg, dlb), landed = hosted_call(
        body, name=tag + "_bwd", grid=(t // tmk,),
        in_specs=[row(D_MODEL), row(D_MODEL), row(D_MODEL), row(D_FF), row(D_FF), per_seq, per_seq, per_seq, vec, vec,
                  _resident((D_FF, D_MODEL)), _resident((D_FF, D_MODEL)), _resident((D_FF, D_MODEL))],
        out_specs=[row(D_MODEL), row(D_FF), row(D_FF), row(D_FF), row(D_MODEL), per_seq, per_seq, per_seq, vec, vec],
        out_shape=[jax.ShapeDtypeStruct((t, D_MODEL), F32), jax.ShapeDtypeStruct((t, D_FF), BF16),
                   jax.ShapeDtypeStruct((t, D_FF), BF16), jax.ShapeDtypeStruct((t, D_FF), BF16),
                   jax.ShapeDtypeStruct((t, D_MODEL), BF16), seq_shape, seq_shape, seq_shape, vec_shape, vec_shape],
        args=(dy, x, f, a, b, sc, sh, g, lg, lb, w1, w3, w2), ride=ride)
    tt = min(DW_TOKENS, seq)
    shards = lambda dw: dw.reshape(N_DEV, D_FF // N_DEV, D_MODEL)
    if chain is None:
        dw2, landed = mm_tn(tag + "_dw2", s, df, D_FF // 2, D_MODEL, tt, BF16), []
    else:
        dw2, landed = mm_tn(tag + "_dw2", s, df, D_FF // 2, D_MODEL, tt, BF16, ride=chain((dsh, dsc, dg), dlg, dlb))
    dw1, (s_w2,) = mm_tn(tag + "_dw1", da, h, D_FF // 2, D_MODEL, tt, BF16, ride=([shards(dw2)], [False]))
    dw3, (s_w1,) = mm_tn(tag + "_dw3", db, h, D_FF // 2, D_MODEL, tt, BF16, ride=([shards(dw1)], [False]))
    return dx, (dsh, dsc, dg), (s_w1, shards(dw3), s_w2, dlg, dlb), landed


def kernel(x, c, w_ada, b_ada, ffn1_w1, ffn1_w3, ffn1_w2, ln1_g, ln1_b, w_in, conv_w, conv_b, dt_bias, a_log, d_ssd, ssd_norm_w, s5_a_re, s5_a_im, s5_log_dt, s5_b_re, s5_b_im, s5_c_re, s5_c_im, s5_d, w_glu, b_glu, w_out, ln2_g, ln2_b, ffn2_w1, ffn2_w3, ffn2_w2, ln3_g, ln3_b, loss_target, m_w_ada, m_b_ada, m_ffn1_w1, m_ffn1_w3, m_ffn1_w2, m_ln1_g, m_ln1_b, m_w_in, m_conv_w, m_conv_b, m_dt_bias, m_a_log, m_d_ssd, m_ssd_norm_w, m_s5_a_re, m_s5_a_im, m_s5_log_dt, m_s5_b_re, m_s5_b_im, m_s5_c_re, m_s5_c_im, m_s5_d, m_w_glu, m_b_glu, m_w_out, m_ln2_g, m_ln2_b, m_ffn2_w1, m_ffn2_w3, m_ffn2_w2, m_ln3_g, m_ln3_b, v_w_ada, v_b_ada, v_ffn1_w1, v_ffn1_w3, v_ffn1_w2, v_ln1_g, v_ln1_b, v_w_in, v_conv_w, v_conv_b, v_dt_bias, v_a_log, v_d_ssd, v_ssd_norm_w, v_s5_a_re, v_s5_a_im, v_s5_log_dt, v_s5_b_re, v_s5_b_im, v_s5_c_re, v_s5_c_im, v_s5_d, v_w_glu, v_b_glu, v_w_out, v_ln2_g, v_ln2_b, v_ffn2_w1, v_ffn2_w3, v_ffn2_w2, v_ln3_g, v_ln3_b):
    given = dict(locals())
    bsz, seq, _ = x.shape
    t = bsz * seq
    tm = min(512, seq)
    me = 4 * lax.axis_index("x") + 2 * lax.axis_index("y") + lax.axis_index("c")
    x0 = x.reshape(t, D_MODEL)
    target = loss_target.reshape(t, D_MODEL)

    tr16 = lambda w: w[0].T.astype(BF16)
    whole = lambda g: g.reshape(N_DEV * g.shape[1], g.shape[2])
    g_f1w1, g_f1w3, g_f1w2, g_c = gather_two_level(
        "gather_ffn1", [tr16(ffn1_w1), tr16(ffn1_w3), ffn1_w2[0].astype(BF16), c])
    f1w1, f1w3, f1w2 = whole(g_f1w1), whole(g_f1w3), whole(g_f1w2)
    c_all = whole(g_c)

    n_loc = w_ada.shape[2]
    b_loc = lax.dynamic_slice(b_ada, (0, me * n_loc), (1, n_loc))
    mod_cols = ada_fwd(c_all, w_ada[0], b_loc)
    g_mod, = exchange("gather_mod", [mod_cols], [True])
    mine = lax.dynamic_slice(g_mod, (0, me * bsz, 0), (N_DEV, bsz, n_loc))
    mod = jnp.transpose(mine, (1, 0, 2)).reshape(bsz, N_MOD, 1, D_MODEL)
    sh1, sc1, g1, sh2, sc2, g2, sh3, sc3, g3 = [mod[:, k] for k in range(N_MOD)]

    x1, res1, (g_win, g_glu, g_out, g_conv, g_f2w1) = _ffn_fwd(
        "ffn1", x0, sc1, sh1, g1, f1w1, f1w3, f1w2, ln1_g, ln1_b, seq, tm,
        ride=([tr16(w_in), w_glu[0].astype(BF16), w_out[0].astype(BF16), conv_w[0], tr16(ffn2_w1)], [True] * 5))
    win = whole(g_win)
    wglu = whole(g_glu).astype(F32)
    wout = whole(g_out)
    wo_ssd, wo_s5 = wout[:SSD_WIDTH], wout[SSD_WIDTH:]
    convw = jnp.transpose(g_conv, (1, 0, 2)).reshape(CONV_K, CONV_CH)
    w_z, w_xbc = win[:SSD_WIDTH], win[SSD_WIDTH:SSD_WIDTH + CONV_CH]
    w_dt = win[SSD_WIDTH + CONV_CH:SSD_WIDTH + CONV_CH + SSD_HEADS]
    w_u = win[SSD_WIDTH + CONV_CH + SSD_HEADS:]
    dt_pad = [jnp.pad(w_dt[HEADS_PER_GROUP * g:HEADS_PER_GROUP * (g + 1)], ((0, LANE - HEADS_PER_GROUP), (0, 0)))
              for g in range(SSD_GROUPS)]
    w_dtp = jnp.concatenate(dt_pad, axis=0)
    w_proj = jnp.concatenate([w_xbc, w_z, w_u, w_dtp], axis=0)

    h2, = rowwise_fwd("mix_mod", f_modulate, [x1], [sc2, sh2], [], [(D_MODEL, BF16)], seq, tm)
    proj = mm_nt("mix_proj", [h2], [w_proj], 2 * tm, P_COLS // 2)
    xc = conv_fwd(proj, convw, conv_b, seq, tm)
    dtb = jnp.pad(dt_bias.reshape(SSD_GROUPS, 1, HEADS_PER_GROUP), ((0, 0), (0, 0), (0, LANE - HEADS_PER_GROUP)))
    alog = jnp.pad(a_log.reshape(SSD_GROUPS, 1, HEADS_PER_GROUP), ((0, 0), (0, 0), (0, LANE - HEADS_PER_GROUP)))
    dcol = jnp.pad(d_ssd.reshape(SSD_GROUPS, HEADS_PER_GROUP, 1), ((0, 0), (0, LANE - HEADS_PER_GROUP), (0, 0)))
    nw = ssd_norm_w.reshape(SSD_GROUPS, 1, GROUP_COLS)
    (y_ssd, hprev), (g_f2w3,) = ssd_fwd(xc, proj, dtb, alog, dcol, nw, bsz, seq,
                                        ride=([tr16(ffn2_w3)], [True]))

    a_re2, a_im2, ldt2 = s5_a_re[0], s5_a_im[0], s5_log_dt.reshape(S5_GROUPS, 1)
    ab_re, ab_im, f_re, f_im = _whole(_disc_a, "s5_disc_a", [a_re2, a_im2, ldt2], [(S5_GROUPS, S5_STATE)] * 4)
    b_re2, b_im2 = s5_b_re.reshape(S5_COLS, S5_GROUP_CH), s5_b_im.reshape(S5_COLS, S5_GROUP_CH)
    fr_col, fi_col = f_re.reshape(S5_COLS, 1), f_im.reshape(S5_COLS, 1)
    bb_re, bb_im = _whole(_disc_b, "s5_disc_b", [fr_col, fi_col, b_re2, b_im2], [(S5_COLS, S5_GROUP_CH)] * 2)
    wb_re, wb_im = _blockdiag_b(bb_re).astype(BF16), _blockdiag_b(bb_im).astype(BF16)
    wc_re, wc_im = _blockdiag_c(s5_c_re[0]).astype(BF16), _blockdiag_c(s5_c_im[0]).astype(BF16)
    dt5 = jnp.exp(ldt2)
    lam_re, lam_im = (dt5 * a_re2).reshape(1, S5_COLS), (dt5 * a_im2).reshape(1, S5_COLS)
    sf_re, sf_im, sb_re, sb_im, cf_re, cf_im, cb_re, cb_im = s5_tables(lam_re, lam_im)
    d5 = s5_d.reshape(S5_Q, 1, LANE)
    (y5, xr_all, xi_all), (g_f2w2,) = s5_fwd(
        proj, wb_re, wb_im, wc_re, wc_im, sf_re, sf_im, cf_re, cf_im, d5, bsz, seq,
        ride=([ffn2_w2[0].astype(BF16)], [True]))
    f2w1, f2w3, f2w2 = whole(g_f2w1), whole(g_f2w3), whole(g_f2w2)
    o5, = rowwise_fwd("s5_glu", f_glu, [y5], [], [wglu, b_glu], [(S5_WIDTH, F32)], seq, tm)

    mix = mm_nn("mix_out", [y_ssd, o5], [wo_ssd, wo_s5], tm, D_MODEL)
    x2, = rowwise_fwd("mix_ln", _res_ln(1.0), [x1, mix], [g2], [ln2_g, ln2_b], [(D_MODEL, F32)], seq, tm)

    (dy, loss_loc), res3, _ = _ffn_fwd("ffn2", x2, sc3, sh3, g3, f2w1, f2w3, f2w2, ln3_g, ln3_b, seq, tm, target=target)

    dx2, dmod3, (s_f2w1, d_f2w3, s_f2w2, d_ln3g, d_ln3b), _ = _ffn_bwd(
        "ffn2", dy, x2, sc3, sh3, g3, f2w1, f2w3, f2w2, ln3_g, ln3_b, res3, seq, tm)

    (dx1_a, dmix), (dg2,), (d_ln2g, d_ln2b) = rowwise_bwd(
        "mix_ln_b", _res_ln(1.0), [x1, mix], [g2], [ln2_g, ln2_b], [dx2], seq, tm, [F32, BF16])
    tw = min(DW_TOKENS, seq)
    d_wo = jnp.concatenate([mm_tn("mix_dwo_ssd", y_ssd, dmix, SSD_WIDTH, D_MODEL, tw, BF16),
                            mm_tn("mix_dwo_s5", o5, dmix, S5_WIDTH, D_MODEL, tw, BF16)], axis=0)
    dy_mixed = mm_nt("mix_dy", [dmix], [wout], tm, D_MODEL)
    dy_ssd, do5 = dy_mixed, (dy_mixed, SSD_WIDTH, S5_WIDTH)

    (dy5,), _, (d_wglu, d_bglu) = rowwise_bwd("s5_glu_b", f_glu, [y5], [], [wglu, b_glu], [do5], seq, tm, [F32])
    (du, dwbr, dwbi, dwcr, dwci, dab_re, dab_im, dd5), (s_f2w3, s_out, s_glu) = s5_bwd(
        proj, wb_re, wb_im, wc_re, wc_im, sb_re, sb_im, cb_re, cb_im, d5, xr_all, xi_all, dy5, bsz, seq,
        ride=([d_f2w3, d_wo.reshape(N_DEV, D_MODEL // N_DEV, D_MODEL),
               d_wglu.reshape(N_DEV, S5_WIDTH // N_DEV, S5_WIDTH).astype(BF16)], [False] * 3))
    dbb_re, dbb_im = _unblock_b(dwbr), _unblock_b(dwbi)
    dfr_col, dfi_col, d_b_re, d_b_im = _whole_vjp(_disc_b, "s5_disc_b_b", [fr_col, fi_col, b_re2, b_im2],
                                                  [dbb_re, dbb_im])
    d_a_re, d_a_im, d_ldt = _whole_vjp(
        _disc_a, "s5_disc_a_b", [a_re2, a_im2, ldt2],
        [dab_re.reshape(S5_GROUPS, S5_STATE), dab_im.reshape(S5_GROUPS, S5_STATE),
         dfr_col.reshape(S5_GROUPS, S5_STATE), dfi_col.reshape(S5_GROUPS, S5_STATE)])
    d_c_re, d_c_im = _unblock_c(dwcr), _unblock_c(dwci)

    dxs, dbm, dcm, ddt, dz, ddtb, dalog, ddcol, dnw = ssd_bwd(xc, proj, dtb, alog, dcol, nw, hprev, dy_ssd, bsz, seq)
    dpre, d_convw, d_convb = conv_bwd_pre(proj, convw, conv_b, dxs, dbm, dcm, seq, tm)
    dxbc = conv_bwd_x(dpre, convw, seq, tm)

    dw_xbc = mm_tn("mix_dw_xbc", dxbc, h2, CONV_CH, D_MODEL, tw, BF16)
    dw_z = mm_tn("mix_dw_z", dz, h2, SSD_WIDTH, D_MODEL, tw, BF16)
    dw_u = mm_tn("mix_dw_u", du, h2, S5_WIDTH, D_MODEL, tw, BF16)
    dw_dt = mm_tn("mix_dw_dt", ddt, h2, 2 * LANE, D_MODEL, tw, BF16)
    dw_dt8 = jnp.concatenate([dw_dt[LANE * g:LANE * g + HEADS_PER_GROUP] for g in range(SSD_GROUPS)], axis=0)
    d_win = jnp.concatenate([dw_z, dw_xbc, dw_dt8, dw_u], axis=0)
    dh2, (s_win,) = mm_nn("mix_dh", [dxbc, dz, du, ddt], [w_xbc, w_z, w_u, w_dtp], tm, D_MODEL,
                          ride=([d_win.reshape(N_DEV, IN_COLS // N_DEV, D_MODEL)], [False]))
    (dx1,), (dsc2, dsh2), _ = rowwise_bwd("mix_mod_b", f_modulate, [x1], [sc2, sh2], [], [dh2], seq, tm, [F32],
                                          add_rows={0: dx1_a})

    packing = {}

    def small_and_dmod(dmod1, d_ln1g, d_ln1b):
        dmod = jnp.concatenate(list(dmod1) + [dsh2, dsc2, dg2] + list(dmod3), axis=1).reshape(bsz, N_MOD * D_MODEL)
        small = _small_grads(d_ln1g, d_ln1b)
        packing["names"] = list(small)
        packing["shapes"] = [small[k].shape for k in small]
        return [_pack(list(small.values())), dmod], [True, True]

    def _small_grads(d_ln1g, d_ln1b):
        return {
            "ln1_g": d_ln1g, "ln1_b": d_ln1b, "conv_w": d_convw, "conv_b": d_convb,
            "dt_bias": ddtb[:, 0, :HEADS_PER_GROUP].reshape(1, SSD_HEADS),
            "a_log": dalog[:, 0, :HEADS_PER_GROUP].reshape(1, SSD_HEADS),
            "d_ssd": ddcol[:, :HEADS_PER_GROUP, 0].reshape(1, SSD_HEADS),
            "ssd_norm_w": dnw.reshape(1, SSD_WIDTH),
            "s5_a_re": d_a_re[None], "s5_a_im": d_a_im[None], "s5_log_dt": d_ldt.reshape(1, S5_GROUPS),
            "s5_b_re": d_b_re.reshape(s5_b_re.shape), "s5_b_im": d_b_im.reshape(s5_b_im.shape),
            "s5_c_re": d_c_re[None], "s5_c_im": d_c_im[None], "s5_d": dd5.reshape(1, S5_WIDTH),
            "b_glu": d_bglu, "ln2_g": d_ln2g, "ln2_b": d_ln2b, "ln3_g": d_ln3g, "ln3_b": d_ln3b,
            "loss": loss_loc.reshape(1, 1),
        }

    dx0, _, (s_f1w1, d_f1w3, s_f1w2, _, _), (s_small, s_dmod) = _ffn_bwd(
        "ffn1", dx1, x0, sc1, sh1, g1, f1w1, f1w3, f1w2, ln1_g, ln1_b, res1, seq, tm, chain=small_and_dmod)
    names, shapes = packing["names"], packing["shapes"]
    s_f1w3, = exchange("sum_grads", [d_f1w3], [False])

    out = {"grad_x": dx0.reshape(x.shape)}

    def put(name, res, shape):
        for key, val in zip(("grad_", "delta_", "new_m_", "new_v_"), res):
            out[key + name] = val.reshape(shape)

    for name, slots, tr in (("ffn1_w1", s_f1w1, 176), ("ffn1_w3", s_f1w3, 176), ("ffn2_w1", s_f2w1, 176),
                            ("ffn2_w3", s_f2w3, 176), ("w_in", s_win, IN_COLS // N_DEV)):
        w = given[name]
        grad = sum_slots("sum_" + name, slots, tr).T
        put(name, adamw("adam_" + name, grad, w[0], given["m_" + name][0], given["v_" + name][0], 256), w.shape)
    for name, slots in (("ffn1_w2", s_f1w2), ("ffn2_w2", s_f2w2)):
        w = given[name]
        put(name, adamw("adam_" + name, slots, w[0], given["m_" + name][0], given["v_" + name][0], 176), w.shape)
    put("w_glu", adamw("adam_w_glu", s_glu, w_glu[0], m_w_glu[0], v_w_glu[0], 64), w_glu.shape)
    put("w_out", adamw("adam_w_out", s_out, w_out[0], m_w_out[0], v_w_out[0], 128), w_out.shape)

    dmod_all = s_dmod.reshape(N_DEV * bsz, N_MOD * D_MODEL)
    g_bada, g_wada = ada_bwd(c_all, dmod_all, lax.dynamic_slice(dmod_all, (0, me * n_loc), (N_DEV * bsz, n_loc)))
    put("w_ada", adamw("adam_w_ada", g_wada, w_ada[0], m_w_ada[0], v_w_ada[0], 256), w_ada.shape)
    put("b_ada", adamw("adam_b_ada", g_bada, b_ada, m_b_ada, v_b_ada, 1), b_ada.shape)

    not_params = {"conv_w": jnp.zeros((CONV_K, CONV_CH), F32), "loss": jnp.zeros((1, 1), F32)}
    pw, pm, pv = [_pack([not_params[k] if k in not_params else given[pre + k] for k in names]) for pre in ("", "m_", "v_")]
    res_small = adamw("adam_small", s_small, pw, pm, pv, pw.shape[0])
    parts = [_unpack(r, shapes) for r in res_small]
    for i, k in enumerate(names):
        if k not in not_params:
            put(k, [p[i] for p in parts], given[k].shape)
    out["loss"] = parts[0][names.index("loss")][0, 0]
    g_cw = lax.dynamic_slice(parts[0][names.index("conv_w")], (0, me * LANE), (CONV_K, LANE))
    put("conv_w", adamw("adam_conv_w", g_cw, conv_w[0], m_conv_w[0], v_conv_w[0], CONV_K), conv_w.shape)

    order = ["w_ada", "b_ada", "ffn1_w1", "ffn1_w3", "ffn1_w2", "ln1_g", "ln1_b", "w_in", "conv_w", "conv_b", "dt_bias",
             "a_log", "d_ssd", "ssd_norm_w", "s5_a_re", "s5_a_im", "s5_log_dt", "s5_b_re", "s5_b_im", "s5_c_re",
             "s5_c_im", "s5_d", "w_glu", "b_glu", "w_out", "ln2_g", "ln2_b", "ffn2_w1", "ffn2_w3", "ffn2_w2", "ln3_g",
             "ln3_b"]
    return (out["loss"], out["grad_x"], *[out[p + n] for p in ("grad_", "delta_", "new_m_", "new_v_") for n in order])
```

```python
import functools
import math

import jax
import jax.numpy as jnp
from jax import lax
from jax.experimental import pallas as pl
from jax.experimental.pallas import tpu as pltpu

F32 = jnp.float32
BF16 = jnp.bfloat16
MESH = pl.DeviceIdType.MESH

N_DEV = 8
D_MODEL = 1024
D_FF = 2816
N_MOD = 9
SSD_WIDTH = 512
SSD_HEADS = 8
SSD_HEAD_DIM = 64
SSD_GROUPS = 2
SSD_STATE = 128
SSD_CHUNK = 128
GROUP_COLS = SSD_WIDTH // SSD_GROUPS
HEADS_PER_GROUP = SSD_HEADS // SSD_GROUPS
CONV_K = 4
CONV_CH = 1024
S5_WIDTH = 512
S5_GROUPS = 32
S5_GROUP_CH = 16
S5_STATE = 64
S5_COLS = S5_GROUPS * S5_STATE
S5_Q = 4
S5_CHUNK = 1024
ALPHA = 2.0 ** 0.25
LN_EPS = 1e-5
LANE = 128
HALO = 8

P_XBC, P_Z, P_U, P_DT = 0, 1024, 1536, 2048
P_COLS = 2048 + SSD_GROUPS * LANE
IN_COLS = SSD_WIDTH + CONV_CH + SSD_HEADS + S5_WIDTH

ADAM_LR, ADAM_B1, ADAM_B2, ADAM_EPS, ADAM_WD, ADAM_STEP = 0.001, 0.9, 0.999, 1e-08, 0.01, 10

VMEM_LIMIT = 56 * 1024 * 1024


def _cp(*sem):
    return pltpu.CompilerParams(dimension_semantics=sem if sem else None, vmem_limit_bytes=VMEM_LIMIT)


def _dg(a, b, ca, cb):
    return lax.dot_general(a.astype(BF16), b.astype(BF16), (((ca,), (cb,)), ((), ())), preferred_element_type=F32)


@jax.custom_vjp
def bdot_nn(a, b):
    return _dg(a, b, 1, 0)


bdot_nn.defvjp(lambda a, b: (_dg(a, b, 1, 0), (a, b)),
               lambda r, g: (_dg(g, r[1], 1, 1), _dg(r[0], g, 0, 0)))


@jax.custom_vjp
def bdot_nt(a, b):
    return _dg(a, b, 1, 1)


bdot_nt.defvjp(lambda a, b: (_dg(a, b, 1, 1), (a, b)),
               lambda r, g: (_dg(g, r[1], 1, 0), _dg(g, r[0], 0, 0)))


@jax.custom_vjp
def bdot_tn(a, b):
    return _dg(a, b, 0, 0)


bdot_tn.defvjp(lambda a, b: (_dg(a, b, 0, 0), (a, b)),
               lambda r, g: (_dg(r[1], g, 1, 1), _dg(r[0], g, 1, 0)))


def _split3(x):
    def top(v):
        bits = lax.bitcast_convert_type(v, jnp.int32) & jnp.int32(-65536)
        return lax.bitcast_convert_type(bits, F32)

    hi = top(x)
    r1 = x - hi
    mid = top(r1)
    return hi.astype(BF16), mid.astype(BF16), (r1 - mid).astype(BF16)


def _dot3(a, b, ca, cb, split_a):
    dims = (((ca,), (cb,)), ((), ()))
    if split_a:
        c = b.astype(BF16)
        return sum(lax.dot_general(p, c, dims, preferred_element_type=F32) for p in _split3(a))
    c = a.astype(BF16)
    return sum(lax.dot_general(c, p, dims, preferred_element_type=F32) for p in _split3(b))


@jax.custom_vjp
def mask_dot_left(c, x):
    return _dot3(c, x, 1, 0, False)


mask_dot_left.defvjp(lambda c, x: (_dot3(c, x, 1, 0, False), c),
                     lambda c, g: (jnp.zeros_like(c), _dot3(c, g, 0, 0, False)))


@jax.custom_vjp
def mask_dot_right(x, c):
    return _dot3(x, c, 1, 0, True)


mask_dot_right.defvjp(lambda x, c: (_dot3(x, c, 1, 0, True), c),
                      lambda c, g: (_dot3(g, c, 1, 1, True), jnp.zeros_like(c)))


def _take_col(z):
    @jax.custom_vjp
    def take(x):
        return x[:, z:z + 1]

    def bwd(shape, g):
        hot = (lax.broadcasted_iota(jnp.int32, (1, shape[1]), 1) == z).astype(F32)
        return (g * hot,)

    take.defvjp(lambda x: (x[:, z:z + 1], x.shape), bwd)
    return take


def _take_row(z):
    @jax.custom_vjp
    def take(x):
        return x[z:z + 1, :]

    def bwd(shape, g):
        hot = (lax.broadcasted_iota(jnp.int32, (shape[0], 1), 0) == z).astype(F32)
        return (hot * g,)

    take.defvjp(lambda x: (x[z:z + 1, :], x.shape), bwd)
    return take


def _view(a):
    return a if isinstance(a, tuple) else (a, 0, a.shape[1])


def _col_spec(view, rows, width, index):
    _, off, _ = view
    assert off % width == 0
    return pl.BlockSpec((rows, width), lambda *g: (index(*g)[0], off // width + index(*g)[1]))


def _rw_in_specs(rows, bps, gps, tm, tps):
    specs = [_col_spec(_view(r), tm, _view(r)[2], lambda i: (i, 0)) for r in rows]
    specs += [pl.BlockSpec((1, 1, b.shape[2]), lambda i: (i // tps, 0, 0)) for b in bps]
    specs += [pl.BlockSpec(g.shape, lambda i, nd=g.ndim: (0,) * nd) for g in gps]
    return specs


def _rw_vals(refs, nr, nb, ng):
    vals = [r[...] for r in refs[:nr]]
    vals += [b[0] for b in refs[nr:nr + nb]]
    vals += [g[...] for g in refs[nr + nb:nr + nb + ng]]
    return vals


def rowwise_fwd(name, f, rows, bps, gps, outs, seq, tm):
    t = _view(rows[0])[0].shape[0]
    tps = seq // tm
    nr, nb, ng = len(rows), len(bps), len(gps)

    def body(*refs):
        res = f(*_rw_vals(refs, nr, nb, ng))
        for o, v in zip(refs[nr + nb + ng:], res):
            o[...] = v.astype(o.dtype)

    return pl.pallas_call(
        body, name=name, grid=(t // tm,),
        in_specs=_rw_in_specs(rows, bps, gps, tm, tps),
        out_specs=[pl.BlockSpec((tm, c), lambda i: (i, 0)) for c, _ in outs],
        out_shape=[jax.ShapeDtypeStruct((t, c), d) for c, d in outs],
        compiler_params=_cp("arbitrary"),
    )(*[_view(r)[0] for r in rows], *bps, *gps)


def rowwise_bwd(name, f, rows, bps, gps, douts, seq, tm, row_grads, add_rows=None):
    add_rows = add_rows or {}
    t = _view(rows[0])[0].shape[0]
    tps = seq // tm
    nr, nb, ng, nd = len(rows), len(bps), len(gps), len(douts)
    want = [k for k in range(nr) if row_grads[k] is not None]
    adds = sorted(add_rows)
    n_in = nr + nb + ng + nd + len(adds)

    def body(*refs):
        vals = _rw_vals(refs, nr, nb, ng)
        dvals = tuple(r[...] for r in refs[nr + nb + ng:nr + nb + ng + nd])
        add_refs = dict(zip(adds, refs[nr + nb + ng + nd:n_in]))
        out_refs = refs[n_in:]
        _, pull = jax.vjp(f, *vals)
        grads = pull(dvals)
        i = pl.program_id(0)
        for o, k in zip(out_refs, want):
            g = grads[k]
            if k in add_refs:
                g = g + add_refs[k][...]
            o[...] = g.astype(o.dtype)
        for j in range(nb):
            o = out_refs[len(want) + j]

            @pl.when(i % tps == 0)
            def _(o=o):
                o[...] = jnp.zeros_like(o)

            o[0] = o[0] + grads[nr + j]
        for j in range(ng):
            o = out_refs[len(want) + nb + j]

            @pl.when(i == 0)
            def _(o=o):
                o[...] = jnp.zeros_like(o)

            o[...] = o[...] + grads[nr + nb + j]

    in_specs = _rw_in_specs(rows, bps, gps, tm, tps)
    in_specs += [_col_spec(_view(d), tm, _view(d)[2], lambda i: (i, 0)) for d in douts]
    in_specs += [pl.BlockSpec((tm, add_rows[k].shape[1]), lambda i: (i, 0)) for k in adds]
    out_specs = [pl.BlockSpec((tm, _view(rows[k])[2]), lambda i: (i, 0)) for k in want]
    out_shape = [jax.ShapeDtypeStruct((t, _view(rows[k])[2]), row_grads[k]) for k in want]
    out_specs += [pl.BlockSpec((1, 1, b.shape[2]), lambda i: (i // tps, 0, 0)) for b in bps]
    out_shape += [jax.ShapeDtypeStruct(b.shape, F32) for b in bps]
    out_specs += [pl.BlockSpec(g.shape, lambda i, n=g.ndim: (0,) * n) for g in gps]
    out_shape += [jax.ShapeDtypeStruct(g.shape, F32) for g in gps]
    res = pl.pallas_call(
        body, name=name, grid=(t // tm,), in_specs=in_specs, out_specs=out_specs, out_shape=out_shape,
        compiler_params=_cp("arbitrary"),
    )(*[_view(r)[0] for r in rows], *bps, *gps, *[_view(d)[0] for d in douts], *[add_rows[k] for k in adds])
    nw = len(want)
    return res[:nw], res[nw:nw + nb], res[nw + nb:]


def mm_nn(name, xs, ws, tm, tn, out_dtype=F32, ride=None):
    views = [_view(x) for x in xs]
    t, n, k = views[0][0].shape[0], ws[0].shape[1], len(xs)

    def body(*refs):
        acc = _dg(refs[0][...], refs[k][...], 1, 0)
        for i in range(1, k):
            acc = acc + _dg(refs[i][...], refs[k + i][...], 1, 0)
        refs[2 * k][...] = acc.astype(out_dtype)

    in_specs = [_col_spec(v, tm, v[2], lambda i, j: (i, 0)) for v in views]
    in_specs += [pl.BlockSpec((w.shape[0], tn), lambda i, j: (0, j)) for w in ws]
    out_spec = pl.BlockSpec((tm, tn), lambda i, j: (i, j))
    out_shape = jax.ShapeDtypeStruct((t, n), out_dtype)
    if ride is not None:
        (res,), landed = hosted_call(body, name=name, grid=(t // tm, n // tn), in_specs=in_specs, out_specs=[out_spec],
                                     out_shape=[out_shape], args=(*[v[0] for v in views], *ws), ride=ride)
        return res, landed
    return pl.pallas_call(
        body, name=name, grid=(t // tm, n // tn), in_specs=in_specs, out_specs=out_spec, out_shape=out_shape,
        compiler_params=_cp("parallel", "parallel"),
    )(*[v[0] for v in views], *ws)


def mm_nt(name, dys, ws, tm, tk, out_dtype=F32, ride=None):
    views = [_view(d) for d in dys]
    t, kk, k = views[0][0].shape[0], ws[0].shape[0], len(dys)

    def body(*refs):
        acc = _dg(refs[0][...], refs[k][...], 1, 1)
        for i in range(1, k):
            acc = acc + _dg(refs[i][...], refs[k + i][...], 1, 1)
        refs[2 * k][...] = acc.astype(out_dtype)

    in_specs = [_col_spec(v, tm, v[2], lambda i, j: (i, 0)) for v in views]
    in_specs += [pl.BlockSpec((tk, w.shape[1]), lambda i, j: (j, 0)) for w in ws]
    out_spec = pl.BlockSpec((tm, tk), lambda i, j: (i, j))
    out_shape = jax.ShapeDtypeStruct((t, kk), out_dtype)
    if ride is not None:
        (res,), landed = hosted_call(body, name=name, grid=(t // tm, kk // tk), in_specs=in_specs, out_specs=[out_spec],
                                     out_shape=[out_shape], args=(*[v[0] for v in views], *ws), ride=ride)
        return res, landed
    return pl.pallas_call(
        body, name=name, grid=(t // tm, kk // tk), in_specs=in_specs, out_specs=out_spec, out_shape=out_shape,
        compiler_params=_cp("parallel", "parallel"),
    )(*[v[0] for v in views], *ws)


def mm_tn(name, x, dy, tk, tn, tt, out_dtype=F32, ride=None):
    xv, dv = _view(x), _view(dy)
    t, kk, n = xv[0].shape[0], xv[2], dv[2]
    steps = t // tt

    def body(x_ref, d_ref, o_ref, acc_ref):
        @pl.when(pl.program_id(2) == 0)
        def _():
            acc_ref[...] = jnp.zeros_like(acc_ref)

        acc_ref[...] += _dg(x_ref[...], d_ref[...], 0, 0)

        @pl.when(pl.program_id(2) == steps - 1)
        def _():
            o_ref[...] = acc_ref[...].astype(out_dtype)

    in_specs = [_col_spec(xv, tt, tk, lambda a, b, c: (c, a)), _col_spec(dv, tt, tn, lambda a, b, c: (c, b))]
    out_spec = pl.BlockSpec((tk, tn), lambda a, b, c: (a, b))
    out_shape = jax.ShapeDtypeStruct((kk, n), out_dtype)
    if ride is not None:
        (res,), landed = hosted_call(body, name=name, grid=(kk // tk, n // tn, steps), in_specs=in_specs,
                                     out_specs=[out_spec], out_shape=[out_shape], scratch=[pltpu.VMEM((tk, tn), F32)],
                                     args=(xv[0], dv[0]), ride=ride)
        return res, landed
    return pl.pallas_call(
        body, name=name, grid=(kk // tk, n // tn, steps), in_specs=in_specs, out_specs=out_spec, out_shape=out_shape,
        scratch_shapes=[pltpu.VMEM((tk, tn), F32)],
        compiler_params=_cp("parallel", "parallel", "arbitrary"),
    )(xv[0], dv[0])


def _silu(x):
    return x * jax.nn.sigmoid(x)


def f_modulate(x, sc, sh):
    return (x * (1.0 + sc) + sh,)


def _res_ln(coef):
    def f(x, y, g, lg, lb):
        r = ALPHA * x + (coef * g) * y
        mu = jnp.mean(r, axis=-1, keepdims=True)
        d = r - mu
        var = jnp.mean(d * d, axis=-1, keepdims=True)
        return (d * lax.rsqrt(var + LN_EPS) * lg + lb,)
    return f


def f_glu(y, w, b):
    g = jax.nn.gelu(y)
    return (g * jax.nn.sigmoid(bdot_nn(g, w) + b),)


def _shift_down(x, halo, k):
    if k == 0:
        return x
    r = pltpu.roll(x, k, 0)
    hr = pltpu.roll(halo, k, 0)
    row = lax.broadcasted_iota(jnp.int32, (HALO, 1), 0)
    top = jnp.where(row < k, hr, r[:HALO])
    return jnp.concatenate([top, r[HALO:]], axis=0)


def _shift_up(x, halo, k):
    if k == 0:
        return x
    n = x.shape[0]
    r = pltpu.roll(x, n - k, 0)
    hr = pltpu.roll(halo, HALO - k, 0)
    row = lax.broadcasted_iota(jnp.int32, (HALO, 1), 0)
    bot = jnp.where(row >= HALO - k, hr, r[n - HALO:])
    return jnp.concatenate([r[:n - HALO], bot], axis=0)


def _conv_pre(x, halo, w, b):
    acc = x * w[CONV_K - 1:CONV_K, :] + b
    for k in range(1, CONV_K):
        acc = acc + _shift_down(x, halo, k) * w[CONV_K - 1 - k:CONV_K - k, :]
    return acc


def _rows_before(width, tm):
    return pl.BlockSpec((HALO, width), lambda i: (jnp.maximum(i * (tm // HALO) - 1, 0), 0))


def conv_fwd(proj, w, b, seq, tm):
    t = proj.shape[0]
    tps = seq // tm

    def body(x_ref, h_ref, w_ref, b_ref, o_ref):
        first = (pl.program_id(0) % tps == 0)
        halo = jnp.where(first, 0.0, h_ref[...])
        o_ref[...] = _silu(_conv_pre(x_ref[...], halo, w_ref[...], b_ref[...]))

    return pl.pallas_call(
        body, name="conv_fwd", grid=(t // tm,),
        in_specs=[pl.BlockSpec((tm, CONV_CH), lambda i: (i, 0)), _rows_before(CONV_CH, tm),
                  pl.BlockSpec((CONV_K, CONV_CH), lambda i: (0, 0)), pl.BlockSpec((1, CONV_CH), lambda i: (0, 0))],
        out_specs=pl.BlockSpec((tm, CONV_CH), lambda i: (i, 0)),
        out_shape=jax.ShapeDtypeStruct((t, CONV_CH), F32),
        compiler_params=_cp("arbitrary"),
    )(proj, proj, w, b)


def conv_bwd_pre(proj, w, b, dxs, dbm, dcm, seq, tm):
    t = proj.shape[0]
    tps = seq // tm

    def body(x_ref, h_ref, w_ref, b_ref, d1, d2, d3, dp_ref, dw_ref, db_ref):
        i = pl.program_id(0)
        halo = jnp.where(i % tps == 0, 0.0, h_ref[...])
        x = x_ref[...]
        pre = _conv_pre(x, halo, w_ref[...], b_ref[...])
        sg = jax.nn.sigmoid(pre)
        dout = jnp.concatenate([d1[...], d2[...], d3[...]], axis=1)
        dp = dout * (sg * (1.0 + pre * (1.0 - sg)))
        dp_ref[...] = dp

        @pl.when(i == 0)
        def _():
            dw_ref[...] = jnp.zeros_like(dw_ref)
            db_ref[...] = jnp.zeros_like(db_ref)

        db_ref[...] += jnp.sum(dp, axis=0, keepdims=True)
        for k in range(CONV_K):
            j = CONV_K - 1 - k
            dw_ref[j:j + 1, :] += jnp.sum(dp * _shift_down(x, halo, k), axis=0, keepdims=True)

    return pl.pallas_call(
        body, name="conv_bwd_pre", grid=(t // tm,),
        in_specs=[pl.BlockSpec((tm, CONV_CH), lambda i: (i, 0)), _rows_before(CONV_CH, tm),
                  pl.BlockSpec((CONV_K, CONV_CH), lambda i: (0, 0)), pl.BlockSpec((1, CONV_CH), lambda i: (0, 0)),
                  pl.BlockSpec((tm, 512), lambda i: (i, 0)), pl.BlockSpec((tm, 256), lambda i: (i, 0)),
                  pl.BlockSpec((tm, 256), lambda i: (i, 0))],
        out_specs=[pl.BlockSpec((tm, CONV_CH), lambda i: (i, 0)), pl.BlockSpec((CONV_K, CONV_CH), lambda i: (0, 0)),
                   pl.BlockSpec((1, CONV_CH), lambda i: (0, 0))],
        out_shape=[jax.ShapeDtypeStruct((t, CONV_CH), F32), jax.ShapeDtypeStruct((CONV_K, CONV_CH), F32),
                   jax.ShapeDtypeStruct((1, CONV_CH), F32)],
        compiler_params=_cp("arbitrary"),
    )(proj, proj, w, b, dxs, dbm, dcm)


def conv_bwd_x(dpre, w, seq, tm):
    t = dpre.shape[0]
    tps = seq // tm
    blocks = tm // HALO
    last = t // HALO - 1

    def body(d_ref, h_ref, w_ref, o_ref):
        halo = jnp.where(pl.program_id(0) % tps == tps - 1, 0.0, h_ref[...])
        d = d_ref[...]
        w = w_ref[...]
        acc = d * w[CONV_K - 1:CONV_K, :]
        for k in range(1, CONV_K):
            acc = acc + _shift_up(d, halo, k) * w[CONV_K - 1 - k:CONV_K - k, :]
        o_ref[...] = acc

    return pl.pallas_call(
        body, name="conv_bwd_x", grid=(t // tm,),
        in_specs=[pl.BlockSpec((tm, CONV_CH), lambda i: (i, 0)),
                  pl.BlockSpec((HALO, CONV_CH), lambda i: (jnp.minimum((i + 1) * blocks, last), 0)),
                  pl.BlockSpec((CONV_K, CONV_CH), lambda i: (0, 0))],
        out_specs=pl.BlockSpec((tm, CONV_CH), lambda i: (i, 0)),
        out_shape=jax.ShapeDtypeStruct((t, CONV_CH), F32),
        compiler_params=_cp("arbitrary"),
    )(dpre, dpre, w)


def _softplus(x):
    return jnp.maximum(x, 0.0) + jnp.log1p(jnp.exp(-jnp.abs(x)))


def _ssd_chunk(xs, bg, cg, dtr, zz, hp, dtb, alog, dcol, nw):
    l = xs.shape[0]
    row = lax.broadcasted_iota(jnp.int32, (l, l), 0)
    col = lax.broadcasted_iota(jnp.int32, (l, l), 1)
    causal = row >= col
    tril = causal.astype(F32)
    expand = (lax.broadcasted_iota(jnp.int32, (LANE, GROUP_COLS), 1) // SSD_HEAD_DIM
              == lax.broadcasted_iota(jnp.int32, (LANE, GROUP_COLS), 0)).astype(F32)
    head_of_col = lax.broadcasted_iota(jnp.int32, (1, GROUP_COLS), 1) // SSD_HEAD_DIM
    last_row = (lax.broadcasted_iota(jnp.int32, (l, 1), 0) == l - 1).astype(F32)

    dtc = _softplus(dtr + dtb)
    a_c = dtc * (-jnp.exp(alog))
    acs_c = mask_dot_left(tril, a_c)
    dt_e = mask_dot_right(dtc, expand)
    acs_e = mask_dot_right(acs_c, expand)
    alast_e = jnp.sum(acs_e * last_row, axis=0, keepdims=True)
    x = xs * dt_e
    states = bdot_tn(bg, x * jnp.exp(alast_e - acs_e))
    h_next = jnp.exp(alast_e) * hp + states
    d_e = jnp.sum(dcol * expand, axis=0, keepdims=True)
    y = bdot_nn(cg, hp) * jnp.exp(acs_e) + d_e * xs
    cb = bdot_nt(cg, bg)
    acs_t = acs_c.T
    for z in range(HEADS_PER_GROUP):
        seg = _take_col(z)(acs_c) - _take_row(z)(acs_t)
        lmat = jnp.exp(jnp.where(causal, seg, -1e30))
        y = y + bdot_nn(cb * lmat, x * (head_of_col == z).astype(F32))
    yz = y * _silu(zz)
    ms = jnp.mean(yz * yz, axis=-1, keepdims=True)
    return yz * lax.rsqrt(ms + LN_EPS) * nw, h_next


SSD_SUB = 2
SSD_ROWS = SSD_SUB * SSD_CHUNK


def _ssd_in_specs(steps, rev):
    def tok(b, c):
        return b * steps + (steps - 1 - c if rev else c)

    whole = lambda *shape: pl.BlockSpec(shape, lambda b, c: (0,) * len(shape))
    both = SSD_GROUPS * SSD_STATE
    return [
        pl.BlockSpec((SSD_ROWS, SSD_WIDTH), lambda b, c: (tok(b, c), 0)),
        pl.BlockSpec((SSD_ROWS, both), lambda b, c: (tok(b, c), SSD_WIDTH // both)),
        pl.BlockSpec((SSD_ROWS, both), lambda b, c: (tok(b, c), SSD_WIDTH // both + 1)),
        pl.BlockSpec((SSD_ROWS, SSD_GROUPS * LANE), lambda b, c: (tok(b, c), P_DT // (SSD_GROUPS * LANE))),
        pl.BlockSpec((SSD_ROWS, SSD_WIDTH), lambda b, c: (tok(b, c), P_Z // SSD_WIDTH)),
        whole(SSD_GROUPS, 1, LANE), whole(SSD_GROUPS, 1, LANE), whole(SSD_GROUPS, LANE, 1),
        whole(SSD_GROUPS, 1, GROUP_COLS),
    ], tok


def _piece(ref, s, g, width):
    return ref[s * SSD_CHUNK:(s + 1) * SSD_CHUNK, g * width:(g + 1) * width]


def ssd_fwd(xc, proj, dtb, alog, dcol, nw, bsz, seq, ride=None):
    t = xc.shape[0]
    nc = seq // SSD_CHUNK
    steps = nc // SSD_SUB
    in_specs, tok = _ssd_in_specs(steps, False)

    def body(xs, bm, cm, dtr, zz, dtb_r, alog_r, dcol_r, nw_r, y_ref, hp_ref, h_scr):
        @pl.when(pl.program_id(1) == 0)
        def _():
            h_scr[...] = jnp.zeros_like(h_scr)

        for g in range(SSD_GROUPS):
            h = h_scr[g]
            for s in range(SSD_SUB):
                hp_ref[g, 0, s] = h
                y, h = _ssd_chunk(_piece(xs, s, g, GROUP_COLS), _piece(bm, s, g, SSD_STATE),
                                  _piece(cm, s, g, SSD_STATE), _piece(dtr, s, g, LANE), _piece(zz, s, g, GROUP_COLS), h,
                                  dtb_r[g], alog_r[g], dcol_r[g], nw_r[g])
                y_ref[s * SSD_CHUNK:(s + 1) * SSD_CHUNK, g * GROUP_COLS:(g + 1) * GROUP_COLS] = y
            h_scr[g] = h

    return hosted_call(
        body, name="ssd_fwd", grid=(bsz, steps), in_specs=in_specs,
        out_specs=[pl.BlockSpec((SSD_ROWS, SSD_WIDTH), lambda b, c: (tok(b, c), 0)),
                   pl.BlockSpec((SSD_GROUPS, 1, SSD_SUB, SSD_STATE, GROUP_COLS), lambda b, c: (0, b, c, 0, 0))],
        out_shape=[jax.ShapeDtypeStruct((t, SSD_WIDTH), F32),
                   jax.ShapeDtypeStruct((SSD_GROUPS, bsz, nc, SSD_STATE, GROUP_COLS), F32)],
        scratch=[pltpu.VMEM((SSD_GROUPS, SSD_STATE, GROUP_COLS), F32)],
        args=(xc, xc, xc, proj, proj, dtb, alog, dcol, nw), ride=ride)


def ssd_bwd(xc, proj, dtb, alog, dcol, nw, hprev, dy, bsz, seq):
    t = xc.shape[0]
    nc = seq // SSD_CHUNK
    steps = nc // SSD_SUB
    in_specs, tok = _ssd_in_specs(steps, True)
    in_specs += [pl.BlockSpec((SSD_GROUPS, 1, SSD_SUB, SSD_STATE, GROUP_COLS), lambda b, c: (0, b, steps - 1 - c, 0, 0)),
                 pl.BlockSpec((SSD_ROWS, SSD_WIDTH), lambda b, c: (tok(b, c), 0))]

    def body(xs, bm, cm, dtr, zz, dtb_r, alog_r, dcol_r, nw_r, hp_ref, dy_ref,
             dxs, dbm, dcm, ddt, dzz, ddtb, dalog, ddcol, dnw, dh_scr):
        b, c = pl.program_id(0), pl.program_id(1)

        @pl.when(c == 0)
        def _():
            dh_scr[...] = jnp.zeros_like(dh_scr)

        @pl.when((b == 0) & (c == 0))
        def _():
            for r in (ddtb, dalog, ddcol, dnw):
                r[...] = jnp.zeros_like(r)

        for g in range(SSD_GROUPS):
            wide = slice(g * GROUP_COLS, (g + 1) * GROUP_COLS)
            state = slice(g * SSD_STATE, (g + 1) * SSD_STATE)
            dh = dh_scr[g]
            for s in reversed(range(SSD_SUB)):
                rows = slice(s * SSD_CHUNK, (s + 1) * SSD_CHUNK)
                _, pull = jax.vjp(_ssd_chunk, xs[rows, wide], bm[rows, state], cm[rows, state],
                                  _piece(dtr, s, g, LANE), zz[rows, wide], hp_ref[g, 0, s],
                                  dtb_r[g], alog_r[g], dcol_r[g], nw_r[g])
                d = pull((dy_ref[rows, wide], dh))
                dxs[rows, wide], dbm[rows, state], dcm[rows, state], dzz[rows, wide] = d[0], d[1], d[2], d[4]
                ddt[rows, g * LANE:(g + 1) * LANE] = d[3]
                dh = d[5]
                ddtb[g] += d[6]
                dalog[g] += d[7]
                ddcol[g] += d[8]
                dnw[g] += d[9]
            dh_scr[g] = dh

    def tile(w):
        return pl.BlockSpec((SSD_ROWS, w), lambda b, c: (tok(b, c), 0))

    whole = lambda *shape: pl.BlockSpec(shape, lambda b, c: (0,) * len(shape))
    return pl.pallas_call(
        body, name="ssd_bwd", grid=(bsz, steps), in_specs=in_specs,
        out_specs=[tile(SSD_WIDTH), tile(2 * SSD_STATE), tile(2 * SSD_STATE), tile(2 * LANE), tile(SSD_WIDTH),
                   whole(SSD_GROUPS, 1, LANE), whole(SSD_GROUPS, 1, LANE), whole(SSD_GROUPS, LANE, 1),
                   whole(SSD_GROUPS, 1, GROUP_COLS)],
        out_shape=[jax.ShapeDtypeStruct((t, SSD_WIDTH), F32), jax.ShapeDtypeStruct((t, 2 * SSD_STATE), F32),
                   jax.ShapeDtypeStruct((t, 2 * SSD_STATE), F32), jax.ShapeDtypeStruct((t, 2 * LANE), F32),
                   jax.ShapeDtypeStruct((t, SSD_WIDTH), F32),
                   jax.ShapeDtypeStruct((SSD_GROUPS, 1, LANE), F32), jax.ShapeDtypeStruct((SSD_GROUPS, 1, LANE), F32),
                   jax.ShapeDtypeStruct((SSD_GROUPS, LANE, 1), F32),
                   jax.ShapeDtypeStruct((SSD_GROUPS, 1, GROUP_COLS), F32)],
        scratch_shapes=[pltpu.VMEM((SSD_GROUPS, SSD_STATE, GROUP_COLS), F32)],
        compiler_params=_cp("arbitrary", "arbitrary"),
    )(xc, xc, xc, proj, proj, dtb, alog, dcol, nw, hprev, dy)


def _disc_a(a_re, a_im, log_dt):
    dt = jnp.exp(log_dt)
    mag = jnp.exp(dt * a_re)
    ab_re, ab_im = mag * jnp.cos(dt * a_im), mag * jnp.sin(dt * a_im)
    den = a_re * a_re + a_im * a_im
    nr, ni = ab_re - 1.0, ab_im
    f_re, f_im = (nr * a_re + ni * a_im) / den, (ni * a_re - nr * a_im) / den
    return ab_re, ab_im, f_re, f_im


def _disc_b(f_re, f_im, b_re, b_im):
    return f_re * b_re - f_im * b_im, f_re * b_im + f_im * b_re


def _whole(f, name, args, outs):
    def body(*refs):
        res = f(*[r[...] for r in refs[:len(args)]])
        for o, v in zip(refs[len(args):], res):
            o[...] = v

    return pl.pallas_call(body, name=name, out_shape=[jax.ShapeDtypeStruct(s, F32) for s in outs])(*args)


def _whole_vjp(f, name, args, cts):
    def body(*refs):
        vals = [r[...] for r in refs[:len(args)]]
        _, pull = jax.vjp(f, *vals)
        res = pull(tuple(r[...] for r in refs[len(args):len(args) + len(cts)]))
        for o, v in zip(refs[len(args) + len(cts):], res):
            o[...] = v

    return pl.pallas_call(body, name=name, out_shape=[jax.ShapeDtypeStruct(a.shape, F32) for a in args])(*args, *cts)


S5_SUB = 8
S5_STEPS = 3


def s5_tables(lam_re, lam_im):
    rows = S5_STEPS * S5_SUB

    def body(lr_ref, li_ref, sf_re, sf_im, sb_re, sb_im, cf_re, cf_im, cb_re, cb_im):
        lr, li = lr_ref[...], li_ref[...]

        def power(k):
            m = jnp.exp(k * lr)
            return m * jnp.cos(k * li), m * jnp.sin(k * li)

        srow = lax.broadcasted_iota(jnp.int32, (rows, 1), 0)
        k = jnp.left_shift(1, srow // S5_SUB)
        tt = srow % S5_SUB
        pr, pi = power(k.astype(F32))
        fwd, bwd = tt >= k, tt < S5_SUB - k
        sf_re[...], sf_im[...] = jnp.where(fwd, pr, 0.0), jnp.where(fwd, pi, 0.0)
        sb_re[...], sb_im[...] = jnp.where(bwd, pr, 0.0), jnp.where(bwd, pi, 0.0)
        trow = lax.broadcasted_iota(jnp.int32, (S5_SUB, 1), 0)
        cf_re[...], cf_im[...] = power((trow + 1).astype(F32))
        cb_re[...], cb_im[...] = power((S5_SUB - trow).astype(F32))

    shp = [jax.ShapeDtypeStruct((rows, S5_COLS), F32)] * 4 + [jax.ShapeDtypeStruct((S5_SUB, S5_COLS), F32)] * 4
    return pl.pallas_call(body, name="s5_tables", out_shape=shp)(lam_re, lam_im)


def _s5_coefs(steps_re, steps_im, carry_re, carry_im, reverse):
    sign = -1.0 if reverse else 1.0
    steps = [(steps_re[s * S5_SUB:(s + 1) * S5_SUB, :], sign * steps_im[s * S5_SUB:(s + 1) * S5_SUB, :])
             for s in range(S5_STEPS)]
    return steps, (carry_re[...], sign * carry_im[...])


def _s5_block_scan(ar, ai, coefs, cr, ci, reverse):
    steps, (qr, qi) = coefs
    for s, (pr, pi) in enumerate(steps):
        shift = S5_SUB - (1 << s) if reverse else (1 << s)
        sr, si = pltpu.roll(ar, shift, 0), pltpu.roll(ai, shift, 0)
        ar, ai = ar + pr * sr - pi * si, ai + pr * si + pi * sr
    br, bi = jnp.broadcast_to(cr, ar.shape), jnp.broadcast_to(ci, ai.shape)
    return ar + qr * br - qi * bi, ai + qr * bi + qi * br


def _s5_specs(n5, rev):
    def tok(q, b, c):
        return b * n5 + (n5 - 1 - c if rev else c)

    qcols = S5_COLS // S5_Q
    specs = [
        pl.BlockSpec((S5_CHUNK, LANE), lambda q, b, c: (tok(q, b, c), P_U // LANE + q)),
        pl.BlockSpec((1, LANE, qcols), lambda q, b, c: (q, 0, 0)),
        pl.BlockSpec((1, LANE, qcols), lambda q, b, c: (q, 0, 0)),
        pl.BlockSpec((1, qcols, LANE), lambda q, b, c: (q, 0, 0)),
        pl.BlockSpec((1, qcols, LANE), lambda q, b, c: (q, 0, 0)),
        pl.BlockSpec((S5_STEPS * S5_SUB, qcols), lambda q, b, c: (0, q)),
        pl.BlockSpec((S5_STEPS * S5_SUB, qcols), lambda q, b, c: (0, q)),
        pl.BlockSpec((S5_SUB, qcols), lambda q, b, c: (0, q)),
        pl.BlockSpec((S5_SUB, qcols), lambda q, b, c: (0, q)),
        pl.BlockSpec((1, 1, LANE), lambda q, b, c: (q, 0, 0)),
    ]
    return specs, tok, qcols


def s5_fwd(proj, wb_re, wb_im, wc_re, wc_im, sf_re, sf_im, cf_re, cf_im, dvec, bsz, seq, ride=None):
    t = proj.shape[0]
    n5 = seq // S5_CHUNK
    in_specs, tok, qcols = _s5_specs(n5, False)

    def body(u_ref, wbr, wbi, wcr, wci, sfr, sfi, cfr, cfi, d_ref, y_ref, xr_ref, xi_ref, cr_scr, ci_scr):
        @pl.when(pl.program_id(2) == 0)
        def _():
            cr_scr[...] = jnp.zeros_like(cr_scr)
            ci_scr[...] = jnp.zeros_like(ci_scr)

        u = u_ref[...]
        bur, bui = _dg(u, wbr[0], 1, 0), _dg(u, wbi[0], 1, 0)
        coefs = _s5_coefs(sfr, sfi, cfr, cfi, False)
        cr, ci = cr_scr[...], ci_scr[...]
        for r in range(S5_CHUNK // S5_SUB):
            rows = slice(r * S5_SUB, (r + 1) * S5_SUB)
            xr, xi = _s5_block_scan(bur[rows], bui[rows], coefs, cr, ci, False)
            xr_ref[rows, :], xi_ref[rows, :] = xr, xi
            cr, ci = xr[S5_SUB - 1:, :], xi[S5_SUB - 1:, :]
        cr_scr[...], ci_scr[...] = cr, ci
        y_ref[...] = _dg(xr_ref[...], wcr[0], 1, 0) - _dg(xi_ref[...], wci[0], 1, 0) + u * d_ref[0]

    def tile(w):
        return pl.BlockSpec((S5_CHUNK, w), lambda q, b, c: (tok(q, b, c), q))

    return hosted_call(
        body, name="s5_fwd", grid=(S5_Q, bsz, n5), in_specs=in_specs,
        out_specs=[tile(LANE), tile(qcols), tile(qcols)],
        out_shape=[jax.ShapeDtypeStruct((t, S5_WIDTH), F32), jax.ShapeDtypeStruct((t, S5_COLS), F32),
                   jax.ShapeDtypeStruct((t, S5_COLS), F32)],
        scratch=[pltpu.VMEM((1, qcols), F32)] * 2,
        args=(proj, wb_re, wb_im, wc_re, wc_im, sf_re, sf_im, cf_re, cf_im, dvec), ride=ride)


def s5_bwd(proj, wb_re, wb_im, wc_re, wc_im, sb_re, sb_im, cb_re, cb_im, dvec, xr_all, xi_all, dy, bsz, seq,
           ride=None):
    t = proj.shape[0]
    n5 = seq // S5_CHUNK
    in_specs, tok, qcols = _s5_specs(n5, True)
    blocks = S5_CHUNK // HALO

    def prev_rows(q, b, c):
        return (jnp.maximum(tok(q, b, c) * blocks - 1, 0), q)

    in_specs += [pl.BlockSpec((S5_CHUNK, qcols), lambda q, b, c: (tok(q, b, c), q)),
                 pl.BlockSpec((S5_CHUNK, qcols), lambda q, b, c: (tok(q, b, c), q)),
                 pl.BlockSpec((HALO, qcols), prev_rows), pl.BlockSpec((HALO, qcols), prev_rows),
                 pl.BlockSpec((S5_CHUNK, LANE), lambda q, b, c: (tok(q, b, c), q))]

    def body(u_ref, wbr, wbi, wcr, wci, sbr, sbi, cbr, cbi, d_ref, xr_ref, xi_ref, pr_ref, pi_ref, dy_ref,
             du_ref, dwbr, dwbi, dwcr, dwci, dar, dai, dd_ref, gr_scr, gi_scr, gr_all, gi_all):
        b, c = pl.program_id(1), pl.program_id(2)

        @pl.when(c == 0)
        def _():
            gr_scr[...] = jnp.zeros_like(gr_scr)
            gi_scr[...] = jnp.zeros_like(gi_scr)

        @pl.when((b == 0) & (c == 0))
        def _():
            for r in (dwbr, dwbi, dwcr, dwci, dar, dai, dd_ref):
                r[...] = jnp.zeros_like(r)

        u, dy_v = u_ref[...], dy_ref[...]
        g0r, g0i = _dg(dy_v, wcr[0], 1, 1), -_dg(dy_v, wci[0], 1, 1)
        coefs = _s5_coefs(sbr, sbi, cbr, cbi, True)
        cr, ci = gr_scr[...], gi_scr[...]
        for r in reversed(range(S5_CHUNK // S5_SUB)):
            rows = slice(r * S5_SUB, (r + 1) * S5_SUB)
            br, bi = _s5_block_scan(g0r[rows], g0i[rows], coefs, cr, ci, True)
            gr_all[rows, :], gi_all[rows, :] = br, bi
            cr, ci = br[:1, :], bi[:1, :]
        gr_scr[...], gi_scr[...] = cr, ci
        gr, gi = gr_all[...], gi_all[...]

        row = lax.broadcasted_iota(jnp.int32, (S5_CHUNK, 1), 0)
        xr, xi = xr_ref[...], xi_ref[...]
        is_first = (c == n5 - 1)
        hr = jnp.where(is_first, 0.0, pr_ref[...][HALO - 1:, :])
        hi = jnp.where(is_first, 0.0, pi_ref[...][HALO - 1:, :])
        xpr = jnp.where(row >= 1, pltpu.roll(xr, 1, 0), hr)
        xpi = jnp.where(row >= 1, pltpu.roll(xi, 1, 0), hi)
        dar[0] += jnp.sum(xpr * gr + xpi * gi, axis=0, keepdims=True)
        dai[0] += jnp.sum(xpr * gi - xpi * gr, axis=0, keepdims=True)
        du_ref[...] = _dg(gr, wbr[0], 1, 1) + _dg(gi, wbi[0], 1, 1) + dy_v * d_ref[0]
        dwbr[0] += _dg(u, gr, 0, 0)
        dwbi[0] += _dg(u, gi, 0, 0)
        dwcr[0] += _dg(xr, dy_v, 0, 0)
        dwci[0] -= _dg(xi, dy_v, 0, 0)
        dd_ref[0] += jnp.sum(dy_v * u, axis=0, keepdims=True)

    def acc(shape):
        return pl.BlockSpec((1,) + shape, lambda q, b, c: (q, 0, 0))

    return hosted_call(
        body, name="s5_bwd", grid=(S5_Q, bsz, n5), in_specs=in_specs,
        out_specs=[pl.BlockSpec((S5_CHUNK, LANE), lambda q, b, c: (tok(q, b, c), q)),
                   acc((LANE, qcols)), acc((LANE, qcols)), acc((qcols, LANE)), acc((qcols, LANE)),
                   acc((1, qcols)), acc((1, qcols)), acc((1, LANE))],
        out_shape=[jax.ShapeDtypeStruct((t, S5_WIDTH), F32),
                   jax.ShapeDtypeStruct((S5_Q, LANE, qcols), F32), jax.ShapeDtypeStruct((S5_Q, LANE, qcols), F32),
                   jax.ShapeDtypeStruct((S5_Q, qcols, LANE), F32), jax.ShapeDtypeStruct((S5_Q, qcols, LANE), F32),
                   jax.ShapeDtypeStruct((S5_Q, 1, qcols), F32), jax.ShapeDtypeStruct((S5_Q, 1, qcols), F32),
                   jax.ShapeDtypeStruct((S5_Q, 1, LANE), F32)],
        scratch=[pltpu.VMEM((1, qcols), F32)] * 2 + [pltpu.VMEM((S5_CHUNK, qcols), F32)] * 2,
        args=(proj, wb_re, wb_im, wc_re, wc_im, sb_re, sb_im, cb_re, cb_im, dvec, xr_all, xi_all, xr_all, xi_all, dy),
        ride=ride)


def _blockdiag_b(bb):
    b4 = bb.reshape(S5_Q, 8, S5_STATE, S5_GROUP_CH)
    eye = jnp.eye(8, dtype=bb.dtype)
    w = jnp.einsum("qgph,gk->qghkp", b4, eye)
    return w.reshape(S5_Q, LANE, S5_COLS // S5_Q)


def _unblock_b(dw):
    d = dw.reshape(S5_Q, 8, S5_GROUP_CH, 8, S5_STATE)
    d = jnp.einsum("qghgp->qgph", d)
    return d.reshape(S5_COLS, S5_GROUP_CH)


def _blockdiag_c(cc):
    c4 = cc.reshape(S5_Q, 8, S5_GROUP_CH, S5_STATE)
    eye = jnp.eye(8, dtype=cc.dtype)
    w = jnp.einsum("qghp,gk->qgpkh", c4, eye)
    return w.reshape(S5_Q, S5_COLS // S5_Q, LANE)


def _unblock_c(dw):
    d = dw.reshape(S5_Q, 8, S5_STATE, 8, S5_GROUP_CH)
    d = jnp.einsum("qgpgh->qghp", d)
    return d.reshape(S5_GROUPS, S5_GROUP_CH, S5_STATE)


def ada_fwd(c_all, w_loc, b_loc):
    def body(c_ref, w_ref, b_ref, o_ref):
        o_ref[...] = _dg(_silu(c_ref[...]), w_ref[...], 1, 0) + b_ref[...]

    return pl.pallas_call(body, name="ada_fwd",
                          out_shape=jax.ShapeDtypeStruct((c_all.shape[0], w_loc.shape[1]), F32),
                          compiler_params=_cp())(c_all, w_loc, b_loc)


def ada_bwd(c_all, dmod_all, dmod_cols):
    def body(c_ref, da_ref, dc_ref, gb_ref, gw_ref):
        gb_ref[...] = jnp.sum(da_ref[...], axis=0, keepdims=True)
        gw_ref[...] = _dg(_silu(c_ref[...]), dc_ref[...], 0, 0)

    return pl.pallas_call(body, name="ada_bwd",
                          out_shape=[jax.ShapeDtypeStruct((1, dmod_all.shape[1]), F32),
                                     jax.ShapeDtypeStruct((c_all.shape[1], dmod_cols.shape[1]), F32)],
                          compiler_params=_cp())(c_all, dmod_all, dmod_cols)


_FLIPS = [(0, 0, 1), (1, 0, 0), (0, 1, 0), (1, 1, 0), (1, 0, 1), (0, 1, 1), (1, 1, 1)]


def _exchange_ops(srcs, outs, sems, gather):
    n = len(srcs)
    send_sems, recv_sems, loc_sems = sems
    x, y, c = lax.axis_index("x"), lax.axis_index("y"), lax.axis_index("c")
    me = 4 * x + 2 * y + c
    peers = []
    for fx, fy, fc in _FLIPS:
        px, py, pc = (1 - x if fx else x), (1 - y if fy else y), (1 - c if fc else c)
        peers.append(((px, py, pc), 4 * px + 2 * py + pc))

    def copy(k, j, slot_src, slot_dst):
        src = srcs[k] if gather[k] else srcs[k].at[slot_src]
        return pltpu.make_async_remote_copy(src_ref=src, dst_ref=outs[k].at[slot_dst],
                                            send_sem=send_sems.at[k, j], recv_sem=recv_sems.at[k, j],
                                            device_id=peers[j][0], device_id_type=MESH)

    def local(k):
        own = srcs[k] if gather[k] else srcs[k].at[me]
        return pltpu.make_async_copy(own, outs[k].at[me], loc_sems.at[k])

    def start():
        for k in range(n):
            for j in range(N_DEV - 1):
                copy(k, j, peers[j][1], me).start()
            local(k).start()

    def wait():
        for k in range(n):
            for j in range(N_DEV - 1):
                copy(k, j, me, peers[j][1]).wait_recv()
        for k in range(n):
            for j in range(N_DEV - 1):
                copy(k, j, peers[j][1], me).wait_send()
            local(k).wait()

    return start, wait


def _gather_two_level(srcs, outs, sems):
    n = len(srcs)
    send_sems, recv_sems, loc_sems = sems
    x, y, c = lax.axis_index("x"), lax.axis_index("y"), lax.axis_index("c")
    slot = lambda px, py, pc: 4 * px + 2 * py + pc
    me, sibling = (x, y, c), (x, y, 1 - c)
    chips = [(1 - x, y), (x, 1 - y), (1 - x, 1 - y)]

    def copy(k, j, block, to, own=False):
        return pltpu.make_async_remote_copy(src_ref=srcs[k] if own else outs[k].at[slot(*block)],
                                            dst_ref=outs[k].at[slot(*block)],
                                            send_sem=send_sems.at[k, j], recv_sem=recv_sems.at[k, j],
                                            device_id=to, device_id_type=MESH)

    locs = [pltpu.make_async_copy(srcs[k], outs[k].at[slot(*me)], loc_sems.at[k]) for k in range(n)]
    for k in range(n):
        locs[k].start()
        copy(k, 0, me, sibling, own=True).start()
        for j, chip in enumerate(chips):
            copy(k, 1 + j, me, (*chip, c), own=True).start()
    for j, chip in enumerate(chips):
        for k in range(n):
            copy(k, 1 + j, (*chip, c), me).wait_recv()
            copy(k, 4 + j, (*chip, c), sibling).start()
    for k in range(n):
        copy(k, 0, sibling, me).wait_recv()
        for j, chip in enumerate(chips):
            copy(k, 4 + j, (*chip, 1 - c), me).wait_recv()
    for k in range(n):
        copy(k, 0, me, sibling, own=True).wait_send()
        for j, chip in enumerate(chips):
            copy(k, 1 + j, me, (*chip, c), own=True).wait_send()
            copy(k, 4 + j, (*chip, c), sibling).wait_send()
        locs[k].wait()


def gather_two_level(name, arrs):
    n = len(arrs)
    specs, shapes, sems = _exchange_parts(arrs, [True] * n)

    def body(*refs):
        _gather_two_level(refs[:n], refs[n:2 * n], refs[2 * n:])

    return pl.pallas_call(
        body, name=name, in_specs=specs, out_specs=specs, out_shape=shapes, scratch_shapes=sems,
        compiler_params=pltpu.CompilerParams(has_side_effects=True),
    )(*arrs)


def _exchange_parts(arrs, gather):
    n = len(arrs)
    any_spec = pl.BlockSpec(memory_space=pl.ANY)
    shapes = [jax.ShapeDtypeStruct(((N_DEV,) + a.shape) if g else a.shape, a.dtype) for a, g in zip(arrs, gather)]
    sems = [pltpu.SemaphoreType.DMA((n, N_DEV - 1)), pltpu.SemaphoreType.DMA((n, N_DEV - 1)),
            pltpu.SemaphoreType.DMA((n,))]
    return [any_spec] * n, shapes, sems


def exchange(name, arrs, gather):
    n = len(arrs)
    specs, shapes, sems = _exchange_parts(arrs, gather)

    def body(*refs):
        start, wait = _exchange_ops(refs[:n], refs[n:2 * n], refs[2 * n:], gather)
        start()
        wait()

    return pl.pallas_call(
        body, name=name, in_specs=specs, out_specs=specs, out_shape=shapes, scratch_shapes=sems,
        compiler_params=pltpu.CompilerParams(has_side_effects=True),
    )(*arrs)


def hosted_call(body, *, name, grid, in_specs, out_specs, out_shape, args, scratch=(), ride=None):
    sem = ("arbitrary",) * len(grid)
    if ride is None:
        res = pl.pallas_call(body, name=name, grid=grid, in_specs=in_specs, out_specs=out_specs, out_shape=out_shape,
                             scratch_shapes=list(scratch), compiler_params=_cp(*sem))(*args)
        return list(res), []
    arrs, gather = ride
    n, n_in, n_out, n_scr = len(arrs), len(in_specs), len(out_specs), len(scratch)
    specs, shapes, sems = _exchange_parts(arrs, gather)

    def both(*refs):
        ins, srcs = refs[:n_in], refs[n_in:n_in + n]
        outs, landed = refs[n_in + n:n_in + n + n_out], refs[n_in + n + n_out:n_in + 2 * n + n_out]
        scr, ex_sems = refs[n_in + 2 * n + n_out:n_in + 2 * n + n_out + n_scr], refs[n_in + 2 * n + n_out + n_scr:]
        start, wait = _exchange_ops(srcs, landed, ex_sems, gather)
        first = functools.reduce(lambda a, b: a & b, [pl.program_id(d) == 0 for d in range(len(grid))])
        last = functools.reduce(lambda a, b: a & b, [pl.program_id(d) == grid[d] - 1 for d in range(len(grid))])
        pl.when(first)(start)
        body(*ins, *outs, *scr)
        pl.when(last)(wait)

    res = pl.pallas_call(
        both, name=name, grid=grid, in_specs=list(in_specs) + specs, out_specs=list(out_specs) + specs,
        out_shape=list(out_shape) + shapes, scratch_shapes=list(scratch) + sems, compiler_params=_cp(*sem),
    )(*args, *arrs)
    return list(res[:n_out]), list(res[n_out:])


def sum_slots(name, slots, tr):
    _, r, c = slots.shape

    def body(s_ref, o_ref):
        acc = s_ref[0].astype(F32)
        for j in range(1, N_DEV):
            acc = acc + s_ref[j].astype(F32)
        o_ref[...] = acc

    return pl.pallas_call(
        body, name=name, grid=(r // tr,), in_specs=[pl.BlockSpec((N_DEV, tr, c), lambda i: (0, i, 0))],
        out_specs=pl.BlockSpec((tr, c), lambda i: (i, 0)), out_shape=jax.ShapeDtypeStruct((r, c), F32),
        compiler_params=_cp("parallel"),
    )(slots)


def adamw(name, g, w, m, v, tr, sel=None):
    slots = g.ndim >= 3
    r, c = w.shape
    c1, c2 = 1.0 - ADAM_B1 ** ADAM_STEP, 1.0 - ADAM_B2 ** ADAM_STEP

    def body(g_ref, w_ref, m_ref, v_ref, go, do, mo, vo):
        if slots:
            gg = g_ref[0].astype(F32)
            for j in range(1, N_DEV):
                gg = gg + g_ref[j].astype(F32)
        else:
            gg = g_ref[...]
        mn = ADAM_B1 * m_ref[...] + (1.0 - ADAM_B1) * gg
        vn = ADAM_B2 * v_ref[...] + (1.0 - ADAM_B2) * (gg * gg)
        go[...], mo[...], vo[...] = gg, mn, vn
        do[...] = -ADAM_LR * ((mn / c1) / (jnp.sqrt(vn / c2) + ADAM_EPS) + ADAM_WD * w_ref[...])

    blk = pl.BlockSpec((tr, c), lambda i: (i, 0))
    if g.ndim == 4:
        gspec = pl.BlockSpec((N_DEV, None, tr, c), lambda i: (0, sel, i, 0))
    else:
        gspec = pl.BlockSpec((N_DEV, tr, c), lambda i: (0, i, 0)) if slots else blk
    return pl.pallas_call(
        body, name=name, grid=(r // tr,), in_specs=[gspec, blk, blk, blk], out_specs=[blk] * 4,
        out_shape=[jax.ShapeDtypeStruct((r, c), F32)] * 4, compiler_params=_cp("parallel"),
    )(g, w, m, v)


def _lane_rows(n):
    return -(-n // (8 * LANE)) * 8


def _pack(arrs):
    pieces = []
    for a in arrs:
        n = math.prod(a.shape)
        flat = a.reshape(-1).astype(F32)
        pieces.append(jnp.pad(flat, (0, _lane_rows(n) * LANE - n)).reshape(_lane_rows(n), LANE))
    return jnp.concatenate(pieces, axis=0)


def _unpack(buf, shapes):
    out, off = [], 0
    for s in shapes:
        n = math.prod(s)
        out.append(buf[off:off + _lane_rows(n)].reshape(-1)[:n].reshape(s))
        off += _lane_rows(n)
    return out


FF_CHUNK = D_FF
DW_TOKENS = 2048
FFN_TM = 256


def _resident(shape):
    return pl.BlockSpec(shape, lambda i: (0,) * len(shape), pipeline_mode=pl.Buffered(1))


def _ffn_fwd(tag, x, sc, sh, g, w1, w3, w2, lg, lb, seq, tm, ride=None, target=None):
    t = x.shape[0]
    tm = min(FFN_TM, tm)
    tps = seq // tm
    ln = _res_ln(0.5)
    head = target is not None

    def body(x_ref, sc_ref, sh_ref, g_ref, lg_ref, lb_ref, w1_ref, w3_ref, w2_ref, *rest):
        if head:
            t_ref, y_ref, h_ref, a_ref, b_ref, f_ref, l_ref = rest
        else:
            y_ref, h_ref, a_ref, b_ref, f_ref = rest
        xv = x_ref[...]
        h = (xv * (1.0 + sc_ref[0]) + sh_ref[0]).astype(BF16)
        h_ref[...] = h
        acc = jnp.zeros((tm, D_MODEL), F32)
        for j in range(D_FF // FF_CHUNK):
            sl = slice(j * FF_CHUNK, (j + 1) * FF_CHUNK)
            a = _dg(h, w1_ref[sl, :], 1, 1)
            b = _dg(h, w3_ref[sl, :], 1, 1)
            a_ref[:, sl] = a
            b_ref[:, sl] = b
            acc = acc + _dg(_silu(a) * b, w2_ref[sl, :], 1, 0)
        f_ref[...] = acc
        y = ln(xv, acc, g_ref[0], lg_ref[...], lb_ref[...])[0]
        if head:
            @pl.when(pl.program_id(0) == 0)
            def _():
                l_ref[...] = jnp.zeros_like(l_ref)

            e = y - t_ref[...]
            y_ref[...] = e * (1.0 / D_MODEL)
            l_ref[...] += 0.5 * jnp.sum(jnp.mean(e * e, axis=-1, keepdims=True), axis=0, keepdims=True)
        else:
            y_ref[...] = y

    row = lambda c: pl.BlockSpec((tm, c), lambda i: (i, 0))
    per_seq = pl.BlockSpec((1, 1, D_MODEL), lambda i: (i // tps, 0, 0))
    vec = pl.BlockSpec((1, D_MODEL), lambda i: (0, 0))
    res, landed = hosted_call(
        body, name=tag + "_fwd", grid=(t // tm,),
        in_specs=[row(D_MODEL), per_seq, per_seq, per_seq, vec, vec,
                  _resident((D_FF, D_MODEL)), _resident((D_FF, D_MODEL)), _resident((D_FF, D_MODEL))]
        + ([row(D_MODEL)] if head else []),
        out_specs=[row(D_MODEL), row(D_MODEL), row(D_FF), row(D_FF), row(D_MODEL)]
        + ([pl.BlockSpec((1, 1), lambda i: (0, 0))] if head else []),
        out_shape=[jax.ShapeDtypeStruct((t, D_MODEL), F32), jax.ShapeDtypeStruct((t, D_MODEL), BF16),
                   jax.ShapeDtypeStruct((t, D_FF), F32), jax.ShapeDtypeStruct((t, D_FF), F32),
                   jax.ShapeDtypeStruct((t, D_MODEL), F32)] + ([jax.ShapeDtypeStruct((1, 1), F32)] if head else []),
        args=(x, sc, sh, g, lg, lb, w1, w3, w2) + ((target,) if head else ()), ride=ride)
    first = (res[0], res[5][0, 0]) if head else res[0]
    return first, tuple(res[1:5]), landed


def _ffn_bwd(tag, dy, x, sc, sh, g, w1, w3, w2, lg, lb, res, seq, tm, ride=None, chain=None, split_last=False):
    h, a, b, f = res
    t = x.shape[0]
    tmk = min(FFN_TM, tm)
    tps = seq // tmk
    ln = _res_ln(0.5)

    def body(dy_ref, x_ref, f_ref, a_ref, b_ref, sc_ref, sh_ref, g_ref, lg_ref, lb_ref, w1_ref, w3_ref, w2_ref,
             dx_ref, da_ref, db_ref, s_ref, df_ref, dsc_ref, dsh_ref, dg_ref, dlg_ref, dlb_ref):
        i = pl.program_id(0)

        @pl.when(i % tps == 0)
        def _():
            for r in (dsc_ref, dsh_ref, dg_ref):
                r[...] = jnp.zeros_like(r)

        @pl.when(i == 0)
        def _():
            dlg_ref[...] = jnp.zeros_like(dlg_ref)
            dlb_ref[...] = jnp.zeros_like(dlb_ref)

        xv = x_ref[...]
        _, pull = jax.vjp(ln, xv, f_ref[...], g_ref[0], lg_ref[...], lb_ref[...])
        dx_res, df, dg, dlg, dlb = pull((dy_ref[...],))
        dfb = df.astype(BF16)
        df_ref[...] = dfb
        dh = jnp.zeros((tmk, D_MODEL), F32)
        for j in range(D_FF // FF_CHUNK):
            sl = slice(j * FF_CHUNK, (j + 1) * FF_CHUNK)
            ds = _dg(dfb, w2_ref[sl, :], 1, 1)
            av, bv = a_ref[:, sl], b_ref[:, sl]
            sg = jax.nn.sigmoid(av)
            si = av * sg
            s_ref[:, sl] = (si * bv).astype(BF16)
            da = (ds * bv * (sg * (1.0 + av * (1.0 - sg)))).astype(BF16)
            db = (ds * si).astype(BF16)
            da_ref[:, sl] = da
            db_ref[:, sl] = db
            dh = dh + _dg(da, w1_ref[sl, :], 1, 0) + _dg(db, w3_ref[sl, :], 1, 0)
        dx_ref[...] = dx_res + dh * (1.0 + sc_ref[0])
        dsc_ref[0] += jnp.sum(dh * xv, axis=0, keepdims=True)
        dsh_ref[0] += jnp.sum(dh, axis=0, keepdims=True)
        dg_ref[0] += dg
        dlg_ref[...] += dlg
        dlb_ref[...] += dlb

    row = lambda c: pl.BlockSpec((tmk, c), lambda i: (i, 0))
    per_seq = pl.BlockSpec((1, 1, D_MODEL), lambda i: (i // tps, 0, 0))
    vec = pl.BlockSpec((1, D_MODEL), lambda i: (0, 0))
    seq_shape = jax.ShapeDtypeStruct(sc.shape, F32)
    vec_shape = jax.ShapeDtypeStruct((1, D_MODEL), F32)
    (dx, da, db, s, df, dsc, dsh, dg, dlg, dlb), landed = hosted_call(
        body, name=tag + "_bwd", grid=(t // tmk,),
        in_specs=[row(D_MODEL), row(D_MODEL), row(D_MODEL), row(D_FF), row(D_FF), per_seq, per_seq, per_seq, vec, vec,
                  _resident((D_FF, D_MODEL)), _resident((D_FF, D_MODEL)), _resident((D_FF, D_MODEL))],
        out_specs=[row(D_MODEL), row(D_FF), row(D_FF), row(D_FF), row(D_MODEL), per_seq, per_seq, per_seq, vec, vec],
        out_shape=[jax.ShapeDtypeStruct((t, D_MODEL), F32), jax.ShapeDtypeStruct((t, D_FF), BF16),
                   jax.ShapeDtypeStruct((t, D_FF), BF16), jax.ShapeDtypeStruct((t, D_FF), BF16),
                   jax.ShapeDtypeStruct((t, D_MODEL), BF16), seq_shape, seq_shape, seq_shape, vec_shape, vec_shape],
        args=(dy, x, f, a, b, sc, sh, g, lg, lb, w1, w3, w2), ride=ride)
    tt = min(DW_TOKENS, seq)
    shards = lambda dw: dw.reshape(N_DEV, D_FF // N_DEV, D_MODEL)
    if chain is None:
        dw2, landed = mm_tn(tag + "_dw2", s, df, D_FF // 2, D_MODEL, tt, BF16), []
    else:
        dw2, landed = mm_tn(tag + "_dw2", s, df, D_FF // 2, D_MODEL, tt, BF16, ride=chain((dsh, dsc, dg), dlg, dlb))
    dw1, (s_w2,) = mm_tn(tag + "_dw1", da, h, D_FF // 2, D_MODEL, tt, BF16, ride=([shards(dw2)], [False]))
    if not split_last:
        dw3, (s_w1,) = mm_tn(tag + "_dw3", db, h, D_FF // 2, D_MODEL, tt, BF16, ride=([shards(dw1)], [False]))
        return dx, (dsh, dsc, dg), (s_w1, shards(dw3), s_w2, dlg, dlb), landed
    half = D_MODEL // 2
    dw3a, (s_w1,) = mm_tn(tag + "_dw3a", db, (h, 0, half), D_FF // 2, half, tt, BF16, ride=([shards(dw1)], [False]))
    dw3b, (s_w3a,) = mm_tn(tag + "_dw3b", db, (h, half, half), D_FF // 2, half, tt, BF16,
                           ride=([dw3a.reshape(N_DEV, D_FF // N_DEV, half)], [False]))
    return dx, (dsh, dsc, dg), (s_w1, (s_w3a, dw3b.reshape(N_DEV, D_FF // N_DEV, half)), s_w2, dlg, dlb), landed


def kernel(x, c, w_ada, b_ada, ffn1_w1, ffn1_w3, ffn1_w2, ln1_g, ln1_b, w_in, conv_w, conv_b, dt_bias, a_log, d_ssd, ssd_norm_w, s5_a_re, s5_a_im, s5_log_dt, s5_b_re, s5_b_im, s5_c_re, s5_c_im, s5_d, w_glu, b_glu, w_out, ln2_g, ln2_b, ffn2_w1, ffn2_w3, ffn2_w2, ln3_g, ln3_b, loss_target, m_w_ada, m_b_ada, m_ffn1_w1, m_ffn1_w3, m_ffn1_w2, m_ln1_g, m_ln1_b, m_w_in, m_conv_w, m_conv_b, m_dt_bias, m_a_log, m_d_ssd, m_ssd_norm_w, m_s5_a_re, m_s5_a_im, m_s5_log_dt, m_s5_b_re, m_s5_b_im, m_s5_c_re, m_s5_c_im, m_s5_d, m_w_glu, m_b_glu, m_w_out, m_ln2_g, m_ln2_b, m_ffn2_w1, m_ffn2_w3, m_ffn2_w2, m_ln3_g, m_ln3_b, v_w_ada, v_b_ada, v_ffn1_w1, v_ffn1_w3, v_ffn1_w2, v_ln1_g, v_ln1_b, v_w_in, v_conv_w, v_conv_b, v_dt_bias, v_a_log, v_d_ssd, v_ssd_norm_w, v_s5_a_re, v_s5_a_im, v_s5_log_dt, v_s5_b_re, v_s5_b_im, v_s5_c_re, v_s5_c_im, v_s5_d, v_w_glu, v_b_glu, v_w_out, v_ln2_g, v_ln2_b, v_ffn2_w1, v_ffn2_w3, v_ffn2_w2, v_ln3_g, v_ln3_b):
    given = dict(locals())
    bsz, seq, _ = x.shape
    t = bsz * seq
    tm = min(512, seq)
    me = 4 * lax.axis_index("x") + 2 * lax.axis_index("y") + lax.axis_index("c")
    x0 = x.reshape(t, D_MODEL)
    target = loss_target.reshape(t, D_MODEL)

    tr16 = lambda w: w[0].T.astype(BF16)
    whole = lambda g: g.reshape(N_DEV * g.shape[1], g.shape[2])
    g_f1w1, g_f1w3, g_f1w2, g_c = gather_two_level(
        "gather_ffn1", [tr16(ffn1_w1), tr16(ffn1_w3), ffn1_w2[0].astype(BF16), c])
    f1w1, f1w3, f1w2 = whole(g_f1w1), whole(g_f1w3), whole(g_f1w2)
    c_all = whole(g_c)

    n_loc = w_ada.shape[2]
    b_loc = lax.dynamic_slice(b_ada, (0, me * n_loc), (1, n_loc))
    mod_cols = ada_fwd(c_all, w_ada[0], b_loc)
    g_mod, = exchange("gather_mod", [mod_cols], [True])
    mine = lax.dynamic_slice(g_mod, (0, me * bsz, 0), (N_DEV, bsz, n_loc))
    mod = jnp.transpose(mine, (1, 0, 2)).reshape(bsz, N_MOD, 1, D_MODEL)
    sh1, sc1, g1, sh2, sc2, g2, sh3, sc3, g3 = [mod[:, k] for k in range(N_MOD)]

    x1, res1, (g_win, g_glu, g_out, g_conv, g_f2w1) = _ffn_fwd(
        "ffn1", x0, sc1, sh1, g1, f1w1, f1w3, f1w2, ln1_g, ln1_b, seq, tm,
        ride=([tr16(w_in), w_glu[0].astype(BF16), w_out[0].astype(BF16), conv_w[0], tr16(ffn2_w1)], [True] * 5))
    win = whole(g_win)
    wglu = whole(g_glu).astype(F32)
    wout = whole(g_out)
    wo_ssd, wo_s5 = wout[:SSD_WIDTH], wout[SSD_WIDTH:]
    convw = jnp.transpose(g_conv, (1, 0, 2)).reshape(CONV_K, CONV_CH)
    w_z, w_xbc = win[:SSD_WIDTH], win[SSD_WIDTH:SSD_WIDTH + CONV_CH]
    w_dt = win[SSD_WIDTH + CONV_CH:SSD_WIDTH + CONV_CH + SSD_HEADS]
    w_u = win[SSD_WIDTH + CONV_CH + SSD_HEADS:]
    dt_pad = [jnp.pad(w_dt[HEADS_PER_GROUP * g:HEADS_PER_GROUP * (g + 1)], ((0, LANE - HEADS_PER_GROUP), (0, 0)))
              for g in range(SSD_GROUPS)]
    w_dtp = jnp.concatenate(dt_pad, axis=0)
    w_proj = jnp.concatenate([w_xbc, w_z, w_u, w_dtp], axis=0)

    h2, = rowwise_fwd("mix_mod", f_modulate, [x1], [sc2, sh2], [], [(D_MODEL, BF16)], seq, tm)
    proj = mm_nt("mix_proj", [h2], [w_proj], 2 * tm, P_COLS // 2)
    xc = conv_fwd(proj, convw, conv_b, seq, tm)
    dtb = jnp.pad(dt_bias.reshape(SSD_GROUPS, 1, HEADS_PER_GROUP), ((0, 0), (0, 0), (0, LANE - HEADS_PER_GROUP)))
    alog = jnp.pad(a_log.reshape(SSD_GROUPS, 1, HEADS_PER_GROUP), ((0, 0), (0, 0), (0, LANE - HEADS_PER_GROUP)))
    dcol = jnp.pad(d_ssd.reshape(SSD_GROUPS, HEADS_PER_GROUP, 1), ((0, 0), (0, LANE - HEADS_PER_GROUP), (0, 0)))
    nw = ssd_norm_w.reshape(SSD_GROUPS, 1, GROUP_COLS)
    (y_ssd, hprev), (g_f2w3,) = ssd_fwd(xc, proj, dtb, alog, dcol, nw, bsz, seq,
                                        ride=([tr16(ffn2_w3)], [True]))

    a_re2, a_im2, ldt2 = s5_a_re[0], s5_a_im[0], s5_log_dt.reshape(S5_GROUPS, 1)
    ab_re, ab_im, f_re, f_im = _whole(_disc_a, "s5_disc_a", [a_re2, a_im2, ldt2], [(S5_GROUPS, S5_STATE)] * 4)
    b_re2, b_im2 = s5_b_re.reshape(S5_COLS, S5_GROUP_CH), s5_b_im.reshape(S5_COLS, S5_GROUP_CH)
    fr_col, fi_col = f_re.reshape(S5_COLS, 1), f_im.reshape(S5_COLS, 1)
    bb_re, bb_im = _whole(_disc_b, "s5_disc_b", [fr_col, fi_col, b_re2, b_im2], [(S5_COLS, S5_GROUP_CH)] * 2)
    wb_re, wb_im = _blockdiag_b(bb_re).astype(BF16), _blockdiag_b(bb_im).astype(BF16)
    wc_re, wc_im = _blockdiag_c(s5_c_re[0]).astype(BF16), _blockdiag_c(s5_c_im[0]).astype(BF16)
    dt5 = jnp.exp(ldt2)
    lam_re, lam_im = (dt5 * a_re2).reshape(1, S5_COLS), (dt5 * a_im2).reshape(1, S5_COLS)
    sf_re, sf_im, sb_re, sb_im, cf_re, cf_im, cb_re, cb_im = s5_tables(lam_re, lam_im)
    d5 = s5_d.reshape(S5_Q, 1, LANE)
    (y5, xr_all, xi_all), (g_f2w2,) = s5_fwd(
        proj, wb_re, wb_im, wc_re, wc_im, sf_re, sf_im, cf_re, cf_im, d5, bsz, seq,
        ride=([ffn2_w2[0].astype(BF16)], [True]))
    f2w1, f2w3, f2w2 = whole(g_f2w1), whole(g_f2w3), whole(g_f2w2)
    o5, = rowwise_fwd("s5_glu", f_glu, [y5], [], [wglu, b_glu], [(S5_WIDTH, F32)], seq, tm)

    mix = mm_nn("mix_out", [y_ssd, o5], [wo_ssd, wo_s5], tm, D_MODEL)
    x2, = rowwise_fwd("mix_ln", _res_ln(1.0), [x1, mix], [g2], [ln2_g, ln2_b], [(D_MODEL, F32)], seq, tm)

    (dy, loss_loc), res3, _ = _ffn_fwd("ffn2", x2, sc3, sh3, g3, f2w1, f2w3, f2w2, ln3_g, ln3_b, seq, tm, target=target)

    dx2, dmod3, (s_f2w1, d_f2w3, s_f2w2, d_ln3g, d_ln3b), _ = _ffn_bwd(
        "ffn2", dy, x2, sc3, sh3, g3, f2w1, f2w3, f2w2, ln3_g, ln3_b, res3, seq, tm)

    (dx1_a, dmix), (dg2,), (d_ln2g, d_ln2b) = rowwise_bwd(
        "mix_ln_b", _res_ln(1.0), [x1, mix], [g2], [ln2_g, ln2_b], [dx2], seq, tm, [F32, BF16])
    tw = min(DW_TOKENS, seq)
    d_wo = jnp.concatenate([mm_tn("mix_dwo_ssd", y_ssd, dmix, SSD_WIDTH, D_MODEL, tw, BF16),
                            mm_tn("mix_dwo_s5", o5, dmix, S5_WIDTH, D_MODEL, tw, BF16)], axis=0)
    dy_mixed = mm_nt("mix_dy", [dmix], [wout], tm, D_MODEL)
    dy_ssd, do5 = dy_mixed, (dy_mixed, SSD_WIDTH, S5_WIDTH)

    (dy5,), _, (d_wglu, d_bglu) = rowwise_bwd("s5_glu_b", f_glu, [y5], [], [wglu, b_glu], [do5], seq, tm, [F32])
    (du, dwbr, dwbi, dwcr, dwci, dab_re, dab_im, dd5), (s_f2w3, s_out, s_glu) = s5_bwd(
        proj, wb_re, wb_im, wc_re, wc_im, sb_re, sb_im, cb_re, cb_im, d5, xr_all, xi_all, dy5, bsz, seq,
        ride=([d_f2w3, d_wo.reshape(N_DEV, D_MODEL // N_DEV, D_MODEL),
               d_wglu.reshape(N_DEV, S5_WIDTH // N_DEV, S5_WIDTH).astype(BF16)], [False] * 3))
    dbb_re, dbb_im = _unblock_b(dwbr), _unblock_b(dwbi)
    dfr_col, dfi_col, d_b_re, d_b_im = _whole_vjp(_disc_b, "s5_disc_b_b", [fr_col, fi_col, b_re2, b_im2],
                                                  [dbb_re, dbb_im])
    d_a_re, d_a_im, d_ldt = _whole_vjp(
        _disc_a, "s5_disc_a_b", [a_re2, a_im2, ldt2],
        [dab_re.reshape(S5_GROUPS, S5_STATE), dab_im.reshape(S5_GROUPS, S5_STATE),
         dfr_col.reshape(S5_GROUPS, S5_STATE), dfi_col.reshape(S5_GROUPS, S5_STATE)])
    d_c_re, d_c_im = _unblock_c(dwcr), _unblock_c(dwci)

    dxs, dbm, dcm, ddt, dz, ddtb, dalog, ddcol, dnw = ssd_bwd(xc, proj, dtb, alog, dcol, nw, hprev, dy_ssd, bsz, seq)
    dpre, d_convw, d_convb = conv_bwd_pre(proj, convw, conv_b, dxs, dbm, dcm, seq, tm)
    dxbc = conv_bwd_x(dpre, convw, seq, tm)

    dw_xbc = mm_tn("mix_dw_xbc", dxbc, h2, CONV_CH, D_MODEL, tw, BF16)
    dw_z = mm_tn("mix_dw_z", dz, h2, SSD_WIDTH, D_MODEL, tw, BF16)
    dw_u = mm_tn("mix_dw_u", du, h2, S5_WIDTH, D_MODEL, tw, BF16)
    dw_dt = mm_tn("mix_dw_dt", ddt, h2, 2 * LANE, D_MODEL, tw, BF16)
    dw_dt8 = jnp.concatenate([dw_dt[LANE * g:LANE * g + HEADS_PER_GROUP] for g in range(SSD_GROUPS)], axis=0)
    d_win = jnp.concatenate([dw_z, dw_xbc, dw_dt8, dw_u], axis=0)
    dh2, (s_win,) = mm_nn("mix_dh", [dxbc, dz, du, ddt], [w_xbc, w_z, w_u, w_dtp], tm, D_MODEL,
                          ride=([d_win.reshape(N_DEV, IN_COLS // N_DEV, D_MODEL)], [False]))
    (dx1,), (dsc2, dsh2), _ = rowwise_bwd("mix_mod_b", f_modulate, [x1], [sc2, sh2], [], [dh2], seq, tm, [F32],
                                          add_rows={0: dx1_a})

    packing = {}

    def small_and_dmod(dmod1, d_ln1g, d_ln1b):
        dmod = jnp.concatenate(list(dmod1) + [dsh2, dsc2, dg2] + list(dmod3), axis=1).reshape(bsz, N_MOD * D_MODEL)
        small = _small_grads(d_ln1g, d_ln1b)
        packing["names"] = list(small)
        packing["shapes"] = [small[k].shape for k in small]
        return [_pack(list(small.values())), dmod], [True, True]

    def _small_grads(d_ln1g, d_ln1b):
        return {
            "ln1_g": d_ln1g, "ln1_b": d_ln1b, "conv_w": d_convw, "conv_b": d_convb,
            "dt_bias": ddtb[:, 0, :HEADS_PER_GROUP].reshape(1, SSD_HEADS),
            "a_log": dalog[:, 0, :HEADS_PER_GROUP].reshape(1, SSD_HEADS),
            "d_ssd": ddcol[:, :HEADS_PER_GROUP, 0].reshape(1, SSD_HEADS),
            "ssd_norm_w": dnw.reshape(1, SSD_WIDTH),
            "s5_a_re": d_a_re[None], "s5_a_im": d_a_im[None], "s5_log_dt": d_ldt.reshape(1, S5_GROUPS),
            "s5_b_re": d_b_re.reshape(s5_b_re.shape), "s5_b_im": d_b_im.reshape(s5_b_im.shape),
            "s5_c_re": d_c_re[None], "s5_c_im": d_c_im[None], "s5_d": dd5.reshape(1, S5_WIDTH),
            "b_glu": d_bglu, "ln2_g": d_ln2g, "ln2_b": d_ln2b, "ln3_g": d_ln3g, "ln3_b": d_ln3b,
            "loss": loss_loc.reshape(1, 1),
        }

    dx0, _, (s_f1w1, (s_f1w3a, d_f1w3b), s_f1w2, _, _), (s_small, s_dmod) = _ffn_bwd(
        "ffn1", dx1, x0, sc1, sh1, g1, f1w1, f1w3, f1w2, ln1_g, ln1_b, res1, seq, tm, chain=small_and_dmod,
        split_last=True)
    names, shapes = packing["names"], packing["shapes"]
    s_f1w3b, = exchange("sum_grads", [d_f1w3b], [False])
    g_f1w3 = jnp.concatenate([sum_slots("sum_ffn1_w3a", s_f1w3a, 176), sum_slots("sum_ffn1_w3b", s_f1w3b, 176)], axis=1)

    out = {"grad_x": dx0.reshape(x.shape)}

    def put(name, res, shape):
        for key, val in zip(("grad_", "delta_", "new_m_", "new_v_"), res):
            out[key + name] = val.reshape(shape)

    for name, slots, tr in (("ffn1_w1", s_f1w1, 176), ("ffn1_w3", None, 0), ("ffn2_w1", s_f2w1, 176),
                            ("ffn2_w3", s_f2w3, 176), ("w_in", s_win, IN_COLS // N_DEV)):
        w = given[name]
        grad = (g_f1w3 if slots is None else sum_slots("sum_" + name, slots, tr)).T
        put(name, adamw("adam_" + name, grad, w[0], given["m_" + name][0], given["v_" + name][0], 256), w.shape)
    for name, slots in (("ffn1_w2", s_f1w2), ("ffn2_w2", s_f2w2)):
        w = given[name]
        put(name, adamw("adam_" + name, slots, w[0], given["m_" + name][0], given["v_" + name][0], 176), w.shape)
    put("w_glu", adamw("adam_w_glu", s_glu, w_glu[0], m_w_glu[0], v_w_glu[0], 64), w_glu.shape)
    put("w_out", adamw("adam_w_out", s_out, w_out[0], m_w_out[0], v_w_out[0], 128), w_out.shape)

    dmod_all = s_dmod.reshape(N_DEV * bsz, N_MOD * D_MODEL)
    g_bada, g_wada = ada_bwd(c_all, dmod_all, lax.dynamic_slice(dmod_all, (0, me * n_loc), (N_DEV * bsz, n_loc)))
    put("w_ada", adamw("adam_w_ada", g_wada, w_ada[0], m_w_ada[0], v_w_ada[0], 256), w_ada.shape)
    put("b_ada", adamw("adam_b_ada", g_bada, b_ada, m_b_ada, v_b_ada, 1), b_ada.shape)

    not_params = {"conv_w": jnp.zeros((CONV_K, CONV_CH), F32), "loss": jnp.zeros((1, 1), F32)}
    pw, pm, pv = [_pack([not_params[k] if k in not_params else given[pre + k] for k in names]) for pre in ("", "m_", "v_")]
    res_small = adamw("adam_small", s_small, pw, pm, pv, pw.shape[0])
    parts = [_unpack(r, shapes) for r in res_small]
    for i, k in enumerate(names):
        if k not in not_params:
            put(k, [p[i] for p in parts], given[k].shape)
    out["loss"] = parts[0][names.index("loss")][0, 0]
    g_cw = lax.dynamic_slice(parts[0][names.index("conv_w")], (0, me * LANE), (CONV_K, LANE))
    put("conv_w", adamw("adam_conv_w", g_cw, conv_w[0], m_conv_w[0], v_conv_w[0], CONV_K), conv_w.shape)

    order = ["w_ada", "b_ada", "ffn1_w1", "ffn1_w3", "ffn1_w2", "ln1_g", "ln1_b", "w_in", "conv_w", "conv_b", "dt_bias",
             "a_log", "d_ssd", "ssd_norm_w", "s5_a_re", "s5_a_im", "s5_log_dt", "s5_b_re", "s5_b_im", "s5_c_re",
             "s5_c_im", "s5_d", "w_glu", "b_glu", "w_out", "ln2_g", "ln2_b", "ffn2_w1", "ffn2_w3", "ffn2_w2", "ln3_g",
             "ln3_b"]
    return (out["loss"], out["grad_x"], *[out[p + n] for p in ("grad_", "delta_", "new_m_", "new_v_") for n in order])
```

```python
import functools
import math

import jax
import jax.numpy as jnp
from jax import lax
from jax.experimental import pallas as pl
from jax.experimental.pallas import tpu as pltpu

F32 = jnp.float32
BF16 = jnp.bfloat16
MESH = pl.DeviceIdType.MESH

N_DEV = 8
D_MODEL = 1024
D_FF = 2816
N_MOD = 9
SSD_WIDTH = 512
SSD_HEADS = 8
SSD_HEAD_DIM = 64
SSD_GROUPS = 2
SSD_STATE = 128
SSD_CHUNK = 128
GROUP_COLS = SSD_WIDTH // SSD_GROUPS
HEADS_PER_GROUP = SSD_HEADS // SSD_GROUPS
CONV_K = 4
CONV_CH = 1024
S5_WIDTH = 512
S5_GROUPS = 32
S5_GROUP_CH = 16
S5_STATE = 64
S5_COLS = S5_GROUPS * S5_STATE
S5_Q = 4
S5_CHUNK = 2048
ALPHA = 2.0 ** 0.25
LN_EPS = 1e-5
LANE = 128
HALO = 8

P_XBC, P_Z, P_U, P_DT = 0, 1024, 1536, 2048
P_COLS = 2048 + SSD_GROUPS * LANE
IN_COLS = SSD_WIDTH + CONV_CH + SSD_HEADS + S5_WIDTH

ADAM_LR, ADAM_B1, ADAM_B2, ADAM_EPS, ADAM_WD, ADAM_STEP = 0.001, 0.9, 0.999, 1e-08, 0.01, 10

VMEM_LIMIT = 56 * 1024 * 1024


def _cp(*sem):
    return pltpu.CompilerParams(dimension_semantics=sem if sem else None, vmem_limit_bytes=VMEM_LIMIT)


def _dg(a, b, ca, cb):
    return lax.dot_general(a.astype(BF16), b.astype(BF16), (((ca,), (cb,)), ((), ())), preferred_element_type=F32)


@jax.custom_vjp
def bdot_nn(a, b):
    return _dg(a, b, 1, 0)


bdot_nn.defvjp(lambda a, b: (_dg(a, b, 1, 0), (a, b)),
               lambda r, g: (_dg(g, r[1], 1, 1), _dg(r[0], g, 0, 0)))


@jax.custom_vjp
def bdot_nt(a, b):
    return _dg(a, b, 1, 1)


bdot_nt.defvjp(lambda a, b: (_dg(a, b, 1, 1), (a, b)),
               lambda r, g: (_dg(g, r[1], 1, 0), _dg(g, r[0], 0, 0)))


@jax.custom_vjp
def bdot_tn(a, b):
    return _dg(a, b, 0, 0)


bdot_tn.defvjp(lambda a, b: (_dg(a, b, 0, 0), (a, b)),
               lambda r, g: (_dg(r[1], g, 1, 1), _dg(r[0], g, 1, 0)))


def _split3(x):
    def top(v):
        bits = lax.bitcast_convert_type(v, jnp.int32) & jnp.int32(-65536)
        return lax.bitcast_convert_type(bits, F32)

    hi = top(x)
    r1 = x - hi
    mid = top(r1)
    return hi.astype(BF16), mid.astype(BF16), (r1 - mid).astype(BF16)


def _dot3(a, b, ca, cb, split_a):
    dims = (((ca,), (cb,)), ((), ()))
    if split_a:
        c = b.astype(BF16)
        return sum(lax.dot_general(p, c, dims, preferred_element_type=F32) for p in _split3(a))
    c = a.astype(BF16)
    return sum(lax.dot_general(c, p, dims, preferred_element_type=F32) for p in _split3(b))


@jax.custom_vjp
def mask_dot_left(c, x):
    return _dot3(c, x, 1, 0, False)


mask_dot_left.defvjp(lambda c, x: (_dot3(c, x, 1, 0, False), c),
                     lambda c, g: (jnp.zeros_like(c), _dot3(c, g, 0, 0, False)))


@jax.custom_vjp
def mask_dot_right(x, c):
    return _dot3(x, c, 1, 0, True)


mask_dot_right.defvjp(lambda x, c: (_dot3(x, c, 1, 0, True), c),
                      lambda c, g: (_dot3(g, c, 1, 1, True), jnp.zeros_like(c)))


def _take_col(z):
    @jax.custom_vjp
    def take(x):
        return x[:, z:z + 1]

    def bwd(shape, g):
        hot = (lax.broadcasted_iota(jnp.int32, (1, shape[1]), 1) == z).astype(F32)
        return (g * hot,)

    take.defvjp(lambda x: (x[:, z:z + 1], x.shape), bwd)
    return take


def _take_row(z):
    @jax.custom_vjp
    def take(x):
        return x[z:z + 1, :]

    def bwd(shape, g):
        hot = (lax.broadcasted_iota(jnp.int32, (shape[0], 1), 0) == z).astype(F32)
        return (hot * g,)

    take.defvjp(lambda x: (x[z:z + 1, :], x.shape), bwd)
    return take


def _view(a):
    return a if isinstance(a, tuple) else (a, 0, a.shape[1])


def _col_spec(view, rows, width, index):
    _, off, _ = view
    assert off % width == 0
    return pl.BlockSpec((rows, width), lambda *g: (index(*g)[0], off // width + index(*g)[1]))


def _rw_in_specs(rows, bps, gps, tm, tps):
    specs = [_col_spec(_view(r), tm, _view(r)[2], lambda i: (i, 0)) for r in rows]
    specs += [pl.BlockSpec((1, 1, b.shape[2]), lambda i: (i // tps, 0, 0)) for b in bps]
    specs += [pl.BlockSpec(g.shape, lambda i, nd=g.ndim: (0,) * nd) for g in gps]
    return specs


def _rw_vals(refs, nr, nb, ng):
    vals = [r[...] for r in refs[:nr]]
    vals += [b[0] for b in refs[nr:nr + nb]]
    vals += [g[...] for g in refs[nr + nb:nr + nb + ng]]
    return vals


def rowwise_fwd(name, f, rows, bps, gps, outs, seq, tm):
    t = _view(rows[0])[0].shape[0]
    tps = seq // tm
    nr, nb, ng = len(rows), len(bps), len(gps)

    def body(*refs):
        res = f(*_rw_vals(refs, nr, nb, ng))
        for o, v in zip(refs[nr + nb + ng:], res):
            o[...] = v.astype(o.dtype)

    return pl.pallas_call(
        body, name=name, grid=(t // tm,),
        in_specs=_rw_in_specs(rows, bps, gps, tm, tps),
        out_specs=[pl.BlockSpec((tm, c), lambda i: (i, 0)) for c, _ in outs],
        out_shape=[jax.ShapeDtypeStruct((t, c), d) for c, d in outs],
        compiler_params=_cp("arbitrary"),
    )(*[_view(r)[0] for r in rows], *bps, *gps)


def rowwise_bwd(name, f, rows, bps, gps, douts, seq, tm, row_grads, add_rows=None):
    add_rows = add_rows or {}
    t = _view(rows[0])[0].shape[0]
    tps = seq // tm
    nr, nb, ng, nd = len(rows), len(bps), len(gps), len(douts)
    want = [k for k in range(nr) if row_grads[k] is not None]
    adds = sorted(add_rows)
    n_in = nr + nb + ng + nd + len(adds)

    def body(*refs):
        vals = _rw_vals(refs, nr, nb, ng)
        dvals = tuple(r[...] for r in refs[nr + nb + ng:nr + nb + ng + nd])
        add_refs = dict(zip(adds, refs[nr + nb + ng + nd:n_in]))
        out_refs = refs[n_in:]
        _, pull = jax.vjp(f, *vals)
        grads = pull(dvals)
        i = pl.program_id(0)
        for o, k in zip(out_refs, want):
            g = grads[k]
            if k in add_refs:
                g = g + add_refs[k][...]
            o[...] = g.astype(o.dtype)
        for j in range(nb):
            o = out_refs[len(want) + j]

            @pl.when(i % tps == 0)
            def _(o=o):
                o[...] = jnp.zeros_like(o)

            o[0] = o[0] + grads[nr + j]
        for j in range(ng):
            o = out_refs[len(want) + nb + j]

            @pl.when(i == 0)
            def _(o=o):
                o[...] = jnp.zeros_like(o)

            o[...] = o[...] + grads[nr + nb + j]

    in_specs = _rw_in_specs(rows, bps, gps, tm, tps)
    in_specs += [_col_spec(_view(d), tm, _view(d)[2], lambda i: (i, 0)) for d in douts]
    in_specs += [pl.BlockSpec((tm, add_rows[k].shape[1]), lambda i: (i, 0)) for k in adds]
    out_specs = [pl.BlockSpec((tm, _view(rows[k])[2]), lambda i: (i, 0)) for k in want]
    out_shape = [jax.ShapeDtypeStruct((t, _view(rows[k])[2]), row_grads[k]) for k in want]
    out_specs += [pl.BlockSpec((1, 1, b.shape[2]), lambda i: (i // tps, 0, 0)) for b in bps]
    out_shape += [jax.ShapeDtypeStruct(b.shape, F32) for b in bps]
    out_specs += [pl.BlockSpec(g.shape, lambda i, n=g.ndim: (0,) * n) for g in gps]
    out_shape += [jax.ShapeDtypeStruct(g.shape, F32) for g in gps]
    res = pl.pallas_call(
        body, name=name, grid=(t // tm,), in_specs=in_specs, out_specs=out_specs, out_shape=out_shape,
        compiler_params=_cp("arbitrary"),
    )(*[_view(r)[0] for r in rows], *bps, *gps, *[_view(d)[0] for d in douts], *[add_rows[k] for k in adds])
    nw = len(want)
    return res[:nw], res[nw:nw + nb], res[nw + nb:]


def mm_nn(name, xs, ws, tm, tn, out_dtype=F32, ride=None):
    views = [_view(x) for x in xs]
    t, n, k = views[0][0].shape[0], ws[0].shape[1], len(xs)

    def body(*refs):
        acc = _dg(refs[0][...], refs[k][...], 1, 0)
        for i in range(1, k):
            acc = acc + _dg(refs[i][...], refs[k + i][...], 1, 0)
        refs[2 * k][...] = acc.astype(out_dtype)

    in_specs = [_col_spec(v, tm, v[2], lambda i, j: (i, 0)) for v in views]
    in_specs += [pl.BlockSpec((w.shape[0], tn), lambda i, j: (0, j)) for w in ws]
    out_spec = pl.BlockSpec((tm, tn), lambda i, j: (i, j))
    out_shape = jax.ShapeDtypeStruct((t, n), out_dtype)
    if ride is not None:
        (res,), landed = hosted_call(body, name=name, grid=(t // tm, n // tn), in_specs=in_specs, out_specs=[out_spec],
                                     out_shape=[out_shape], args=(*[v[0] for v in views], *ws), ride=ride)
        return res, landed
    return pl.pallas_call(
        body, name=name, grid=(t // tm, n // tn), in_specs=in_specs, out_specs=out_spec, out_shape=out_shape,
        compiler_params=_cp("parallel", "parallel"),
    )(*[v[0] for v in views], *ws)


def mm_nt(name, dys, ws, tm, tk, out_dtype=F32, ride=None):
    views = [_view(d) for d in dys]
    t, kk, k = views[0][0].shape[0], ws[0].shape[0], len(dys)

    def body(*refs):
        acc = _dg(refs[0][...], refs[k][...], 1, 1)
        for i in range(1, k):
            acc = acc + _dg(refs[i][...], refs[k + i][...], 1, 1)
        refs[2 * k][...] = acc.astype(out_dtype)

    in_specs = [_col_spec(v, tm, v[2], lambda i, j: (i, 0)) for v in views]
    in_specs += [pl.BlockSpec((tk, w.shape[1]), lambda i, j: (j, 0)) for w in ws]
    out_spec = pl.BlockSpec((tm, tk), lambda i, j: (i, j))
    out_shape = jax.ShapeDtypeStruct((t, kk), out_dtype)
    if ride is not None:
        (res,), landed = hosted_call(body, name=name, grid=(t // tm, kk // tk), in_specs=in_specs, out_specs=[out_spec],
                                     out_shape=[out_shape], args=(*[v[0] for v in views], *ws), ride=ride)
        return res, landed
    return pl.pallas_call(
        body, name=name, grid=(t // tm, kk // tk), in_specs=in_specs, out_specs=out_spec, out_shape=out_shape,
        compiler_params=_cp("parallel", "parallel"),
    )(*[v[0] for v in views], *ws)


def mm_tn(name, x, dy, tk, tn, tt, out_dtype=F32, ride=None):
    xv, dv = _view(x), _view(dy)
    t, kk, n = xv[0].shape[0], xv[2], dv[2]
    steps = t // tt

    def body(x_ref, d_ref, o_ref, acc_ref):
        @pl.when(pl.program_id(2) == 0)
        def _():
            acc_ref[...] = jnp.zeros_like(acc_ref)

        acc_ref[...] += _dg(x_ref[...], d_ref[...], 0, 0)

        @pl.when(pl.program_id(2) == steps - 1)
        def _():
            o_ref[...] = acc_ref[...].astype(out_dtype)

    in_specs = [_col_spec(xv, tt, tk, lambda a, b, c: (c, a)), _col_spec(dv, tt, tn, lambda a, b, c: (c, b))]
    out_spec = pl.BlockSpec((tk, tn), lambda a, b, c: (a, b))
    out_shape = jax.ShapeDtypeStruct((kk, n), out_dtype)
    if ride is not None:
        (res,), landed = hosted_call(body, name=name, grid=(kk // tk, n // tn, steps), in_specs=in_specs,
                                     out_specs=[out_spec], out_shape=[out_shape], scratch=[pltpu.VMEM((tk, tn), F32)],
                                     args=(xv[0], dv[0]), ride=ride)
        return res, landed
    return pl.pallas_call(
        body, name=name, grid=(kk // tk, n // tn, steps), in_specs=in_specs, out_specs=out_spec, out_shape=out_shape,
        scratch_shapes=[pltpu.VMEM((tk, tn), F32)],
        compiler_params=_cp("parallel", "parallel", "arbitrary"),
    )(xv[0], dv[0])


def _silu(x):
    return x * jax.nn.sigmoid(x)


def f_modulate(x, sc, sh):
    return (x * (1.0 + sc) + sh,)


def _res_ln(coef):
    def f(x, y, g, lg, lb):
        r = ALPHA * x + (coef * g) * y
        mu = jnp.mean(r, axis=-1, keepdims=True)
        d = r - mu
        var = jnp.mean(d * d, axis=-1, keepdims=True)
        return (d * lax.rsqrt(var + LN_EPS) * lg + lb,)
    return f


def f_glu(y, w, b):
    g = jax.nn.gelu(y)
    return (g * jax.nn.sigmoid(bdot_nn(g, w) + b),)


def _shift_down(x, halo, k):
    if k == 0:
        return x
    r = pltpu.roll(x, k, 0)
    hr = pltpu.roll(halo, k, 0)
    row = lax.broadcasted_iota(jnp.int32, (HALO, 1), 0)
    top = jnp.where(row < k, hr, r[:HALO])
    return jnp.concatenate([top, r[HALO:]], axis=0)


def _shift_up(x, halo, k):
    if k == 0:
        return x
    n = x.shape[0]
    r = pltpu.roll(x, n - k, 0)
    hr = pltpu.roll(halo, HALO - k, 0)
    row = lax.broadcasted_iota(jnp.int32, (HALO, 1), 0)
    bot = jnp.where(row >= HALO - k, hr, r[n - HALO:])
    return jnp.concatenate([r[:n - HALO], bot], axis=0)


def _conv_pre(x, halo, w, b):
    acc = x * w[CONV_K - 1:CONV_K, :] + b
    for k in range(1, CONV_K):
        acc = acc + _shift_down(x, halo, k) * w[CONV_K - 1 - k:CONV_K - k, :]
    return acc


def _rows_before(width, tm):
    return pl.BlockSpec((HALO, width), lambda i: (jnp.maximum(i * (tm // HALO) - 1, 0), 0))


def conv_fwd(proj, w, b, seq, tm):
    t = proj.shape[0]
    tps = seq // tm

    def body(x_ref, h_ref, w_ref, b_ref, o_ref):
        first = (pl.program_id(0) % tps == 0)
        halo = jnp.where(first, 0.0, h_ref[...])
        o_ref[...] = _silu(_conv_pre(x_ref[...], halo, w_ref[...], b_ref[...]))

    return pl.pallas_call(
        body, name="conv_fwd", grid=(t // tm,),
        in_specs=[pl.BlockSpec((tm, CONV_CH), lambda i: (i, 0)), _rows_before(CONV_CH, tm),
                  pl.BlockSpec((CONV_K, CONV_CH), lambda i: (0, 0)), pl.BlockSpec((1, CONV_CH), lambda i: (0, 0))],
        out_specs=pl.BlockSpec((tm, CONV_CH), lambda i: (i, 0)),
        out_shape=jax.ShapeDtypeStruct((t, CONV_CH), F32),
        compiler_params=_cp("arbitrary"),
    )(proj, proj, w, b)


def conv_bwd_pre(proj, w, b, dxs, dbm, dcm, seq, tm):
    t = proj.shape[0]
    tps = seq // tm

    def body(x_ref, h_ref, w_ref, b_ref, d1, d2, d3, dp_ref, dw_ref, db_ref):
        i = pl.program_id(0)
        halo = jnp.where(i % tps == 0, 0.0, h_ref[...])
        x = x_ref[...]
        pre = _conv_pre(x, halo, w_ref[...], b_ref[...])
        sg = jax.nn.sigmoid(pre)
        dout = jnp.concatenate([d1[...], d2[...], d3[...]], axis=1)
        dp = dout * (sg * (1.0 + pre * (1.0 - sg)))
        dp_ref[...] = dp

        @pl.when(i == 0)
        def _():
            dw_ref[...] = jnp.zeros_like(dw_ref)
            db_ref[...] = jnp.zeros_like(db_ref)

        db_ref[...] += jnp.sum(dp, axis=0, keepdims=True)
        for k in range(CONV_K):
            j = CONV_K - 1 - k
            dw_ref[j:j + 1, :] += jnp.sum(dp * _shift_down(x, halo, k), axis=0, keepdims=True)

    return pl.pallas_call(
        body, name="conv_bwd_pre", grid=(t // tm,),
        in_specs=[pl.BlockSpec((tm, CONV_CH), lambda i: (i, 0)), _rows_before(CONV_CH, tm),
                  pl.BlockSpec((CONV_K, CONV_CH), lambda i: (0, 0)), pl.BlockSpec((1, CONV_CH), lambda i: (0, 0)),
                  pl.BlockSpec((tm, 512), lambda i: (i, 0)), pl.BlockSpec((tm, 256), lambda i: (i, 0)),
                  pl.BlockSpec((tm, 256), lambda i: (i, 0))],
        out_specs=[pl.BlockSpec((tm, CONV_CH), lambda i: (i, 0)), pl.BlockSpec((CONV_K, CONV_CH), lambda i: (0, 0)),
                   pl.BlockSpec((1, CONV_CH), lambda i: (0, 0))],
        out_shape=[jax.ShapeDtypeStruct((t, CONV_CH), F32), jax.ShapeDtypeStruct((CONV_K, CONV_CH), F32),
                   jax.ShapeDtypeStruct((1, CONV_CH), F32)],
        compiler_params=_cp("arbitrary"),
    )(proj, proj, w, b, dxs, dbm, dcm)


def conv_bwd_x(dpre, w, seq, tm):
    t = dpre.shape[0]
    tps = seq // tm
    blocks = tm // HALO
    last = t // HALO - 1

    def body(d_ref, h_ref, w_ref, o_ref):
        halo = jnp.where(pl.program_id(0) % tps == tps - 1, 0.0, h_ref[...])
        d = d_ref[...]
        w = w_ref[...]
        acc = d * w[CONV_K - 1:CONV_K, :]
        for k in range(1, CONV_K):
            acc = acc + _shift_up(d, halo, k) * w[CONV_K - 1 - k:CONV_K - k, :]
        o_ref[...] = acc

    return pl.pallas_call(
        body, name="conv_bwd_x", grid=(t // tm,),
        in_specs=[pl.BlockSpec((tm, CONV_CH), lambda i: (i, 0)),
                  pl.BlockSpec((HALO, CONV_CH), lambda i: (jnp.minimum((i + 1) * blocks, last), 0)),
                  pl.BlockSpec((CONV_K, CONV_CH), lambda i: (0, 0))],
        out_specs=pl.BlockSpec((tm, CONV_CH), lambda i: (i, 0)),
        out_shape=jax.ShapeDtypeStruct((t, CONV_CH), F32),
        compiler_params=_cp("arbitrary"),
    )(dpre, dpre, w)


def _softplus(x):
    return jnp.maximum(x, 0.0) + jnp.log1p(jnp.exp(-jnp.abs(x)))


def _ssd_chunk(xs, bg, cg, dtr, zz, hp, dtb, alog, dcol, nw):
    l = xs.shape[0]
    row = lax.broadcasted_iota(jnp.int32, (l, l), 0)
    col = lax.broadcasted_iota(jnp.int32, (l, l), 1)
    causal = row >= col
    tril = causal.astype(F32)
    expand = (lax.broadcasted_iota(jnp.int32, (LANE, GROUP_COLS), 1) // SSD_HEAD_DIM
              == lax.broadcasted_iota(jnp.int32, (LANE, GROUP_COLS), 0)).astype(F32)
    head_of_col = lax.broadcasted_iota(jnp.int32, (1, GROUP_COLS), 1) // SSD_HEAD_DIM
    last_row = (lax.broadcasted_iota(jnp.int32, (l, 1), 0) == l - 1).astype(F32)

    dtc = _softplus(dtr + dtb)
    a_c = dtc * (-jnp.exp(alog))
    acs_c = mask_dot_left(tril, a_c)
    dt_e = mask_dot_right(dtc, expand)
    acs_e = mask_dot_right(acs_c, expand)
    alast_e = jnp.sum(acs_e * last_row, axis=0, keepdims=True)
    x = xs * dt_e
    states = bdot_tn(bg, x * jnp.exp(alast_e - acs_e))
    h_next = jnp.exp(alast_e) * hp + states
    d_e = jnp.sum(dcol * expand, axis=0, keepdims=True)
    y = bdot_nn(cg, hp) * jnp.exp(acs_e) + d_e * xs
    cb = bdot_nt(cg, bg)
    acs_t = acs_c.T
    for z in range(HEADS_PER_GROUP):
        seg = _take_col(z)(acs_c) - _take_row(z)(acs_t)
        lmat = jnp.exp(jnp.where(causal, seg, -1e30))
        y = y + bdot_nn(cb * lmat, x * (head_of_col == z).astype(F32))
    yz = y * _silu(zz)
    ms = jnp.mean(yz * yz, axis=-1, keepdims=True)
    return yz * lax.rsqrt(ms + LN_EPS) * nw, h_next


SSD_SUB = 4
SSD_ROWS = SSD_SUB * SSD_CHUNK


def _ssd_in_specs(steps, rev):
    def tok(b, c):
        return b * steps + (steps - 1 - c if rev else c)

    whole = lambda *shape: pl.BlockSpec(shape, lambda b, c: (0,) * len(shape))
    both = SSD_GROUPS * SSD_STATE
    return [
        pl.BlockSpec((SSD_ROWS, SSD_WIDTH), lambda b, c: (tok(b, c), 0)),
        pl.BlockSpec((SSD_ROWS, both), lambda b, c: (tok(b, c), SSD_WIDTH // both)),
        pl.BlockSpec((SSD_ROWS, both), lambda b, c: (tok(b, c), SSD_WIDTH // both + 1)),
        pl.BlockSpec((SSD_ROWS, SSD_GROUPS * LANE), lambda b, c: (tok(b, c), P_DT // (SSD_GROUPS * LANE))),
        pl.BlockSpec((SSD_ROWS, SSD_WIDTH), lambda b, c: (tok(b, c), P_Z // SSD_WIDTH)),
        whole(SSD_GROUPS, 1, LANE), whole(SSD_GROUPS, 1, LANE), whole(SSD_GROUPS, LANE, 1),
        whole(SSD_GROUPS, 1, GROUP_COLS),
    ], tok


def _piece(ref, s, g, width):
    return ref[s * SSD_CHUNK:(s + 1) * SSD_CHUNK, g * width:(g + 1) * width]


def ssd_fwd(xc, proj, dtb, alog, dcol, nw, bsz, seq, ride=None):
    t = xc.shape[0]
    nc = seq // SSD_CHUNK
    steps = nc // SSD_SUB
    in_specs, tok = _ssd_in_specs(steps, False)

    def body(xs, bm, cm, dtr, zz, dtb_r, alog_r, dcol_r, nw_r, y_ref, hp_ref, h_scr):
        @pl.when(pl.program_id(1) == 0)
        def _():
            h_scr[...] = jnp.zeros_like(h_scr)

        for g in range(SSD_GROUPS):
            h = h_scr[g]
            for s in range(SSD_SUB):
                hp_ref[g, 0, s] = h
                y, h = _ssd_chunk(_piece(xs, s, g, GROUP_COLS), _piece(bm, s, g, SSD_STATE),
                                  _piece(cm, s, g, SSD_STATE), _piece(dtr, s, g, LANE), _piece(zz, s, g, GROUP_COLS), h,
                                  dtb_r[g], alog_r[g], dcol_r[g], nw_r[g])
                y_ref[s * SSD_CHUNK:(s + 1) * SSD_CHUNK, g * GROUP_COLS:(g + 1) * GROUP_COLS] = y
            h_scr[g] = h

    return hosted_call(
        body, name="ssd_fwd", grid=(bsz, steps), in_specs=in_specs,
        out_specs=[pl.BlockSpec((SSD_ROWS, SSD_WIDTH), lambda b, c: (tok(b, c), 0)),
                   pl.BlockSpec((SSD_GROUPS, 1, SSD_SUB, SSD_STATE, GROUP_COLS), lambda b, c: (0, b, c, 0, 0))],
        out_shape=[jax.ShapeDtypeStruct((t, SSD_WIDTH), F32),
                   jax.ShapeDtypeStruct((SSD_GROUPS, bsz, nc, SSD_STATE, GROUP_COLS), F32)],
        scratch=[pltpu.VMEM((SSD_GROUPS, SSD_STATE, GROUP_COLS), F32)],
        args=(xc, xc, xc, proj, proj, dtb, alog, dcol, nw), ride=ride)


def ssd_bwd(xc, proj, dtb, alog, dcol, nw, hprev, dy, bsz, seq):
    t = xc.shape[0]
    nc = seq // SSD_CHUNK
    steps = nc // SSD_SUB
    in_specs, tok = _ssd_in_specs(steps, True)
    in_specs += [pl.BlockSpec((SSD_GROUPS, 1, SSD_SUB, SSD_STATE, GROUP_COLS), lambda b, c: (0, b, steps - 1 - c, 0, 0)),
                 pl.BlockSpec((SSD_ROWS, SSD_WIDTH), lambda b, c: (tok(b, c), 0))]

    def body(xs, bm, cm, dtr, zz, dtb_r, alog_r, dcol_r, nw_r, hp_ref, dy_ref,
             dxs, dbm, dcm, ddt, dzz, ddtb, dalog, ddcol, dnw, dh_scr):
        b, c = pl.program_id(0), pl.program_id(1)

        @pl.when(c == 0)
        def _():
            dh_scr[...] = jnp.zeros_like(dh_scr)

        @pl.when((b == 0) & (c == 0))
        def _():
            for r in (ddtb, dalog, ddcol, dnw):
                r[...] = jnp.zeros_like(r)

        for g in range(SSD_GROUPS):
            wide = slice(g * GROUP_COLS, (g + 1) * GROUP_COLS)
            state = slice(g * SSD_STATE, (g + 1) * SSD_STATE)
            dh = dh_scr[g]
            for s in reversed(range(SSD_SUB)):
                rows = slice(s * SSD_CHUNK, (s + 1) * SSD_CHUNK)
                _, pull = jax.vjp(_ssd_chunk, xs[rows, wide], bm[rows, state], cm[rows, state],
                                  _piece(dtr, s, g, LANE), zz[rows, wide], hp_ref[g, 0, s],
                                  dtb_r[g], alog_r[g], dcol_r[g], nw_r[g])
                d = pull((dy_ref[rows, wide], dh))
                dxs[rows, wide], dbm[rows, state], dcm[rows, state], dzz[rows, wide] = d[0], d[1], d[2], d[4]
                ddt[rows, g * LANE:(g + 1) * LANE] = d[3]
                dh = d[5]
                ddtb[g] += d[6]
                dalog[g] += d[7]
                ddcol[g] += d[8]
                dnw[g] += d[9]
            dh_scr[g] = dh

    def tile(w):
        return pl.BlockSpec((SSD_ROWS, w), lambda b, c: (tok(b, c), 0))

    whole = lambda *shape: pl.BlockSpec(shape, lambda b, c: (0,) * len(shape))
    return pl.pallas_call(
        body, name="ssd_bwd", grid=(bsz, steps), in_specs=in_specs,
        out_specs=[tile(SSD_WIDTH), tile(2 * SSD_STATE), tile(2 * SSD_STATE), tile(2 * LANE), tile(SSD_WIDTH),
                   whole(SSD_GROUPS, 1, LANE), whole(SSD_GROUPS, 1, LANE), whole(SSD_GROUPS, LANE, 1),
                   whole(SSD_GROUPS, 1, GROUP_COLS)],
        out_shape=[jax.ShapeDtypeStruct((t, SSD_WIDTH), F32), jax.ShapeDtypeStruct((t, 2 * SSD_STATE), F32),
                   jax.ShapeDtypeStruct((t, 2 * SSD_STATE), F32), jax.ShapeDtypeStruct((t, 2 * LANE), F32),
                   jax.ShapeDtypeStruct((t, SSD_WIDTH), F32),
                   jax.ShapeDtypeStruct((SSD_GROUPS, 1, LANE), F32), jax.ShapeDtypeStruct((SSD_GROUPS, 1, LANE), F32),
                   jax.ShapeDtypeStruct((SSD_GROUPS, LANE, 1), F32),
                   jax.ShapeDtypeStruct((SSD_GROUPS, 1, GROUP_COLS), F32)],
        scratch_shapes=[pltpu.VMEM((SSD_GROUPS, SSD_STATE, GROUP_COLS), F32)],
        compiler_params=_cp("arbitrary", "arbitrary"),
    )(xc, xc, xc, proj, proj, dtb, alog, dcol, nw, hprev, dy)


def _disc_a(a_re, a_im, log_dt):
    dt = jnp.exp(log_dt)
    mag = jnp.exp(dt * a_re)
    ab_re, ab_im = mag * jnp.cos(dt * a_im), mag * jnp.sin(dt * a_im)
    den = a_re * a_re + a_im * a_im
    nr, ni = ab_re - 1.0, ab_im
    f_re, f_im = (nr * a_re + ni * a_im) / den, (ni * a_re - nr * a_im) / den
    return ab_re, ab_im, f_re, f_im


def _disc_b(f_re, f_im, b_re, b_im):
    return f_re * b_re - f_im * b_im, f_re * b_im + f_im * b_re


def _whole(f, name, args, outs):
    def body(*refs):
        res = f(*[r[...] for r in refs[:len(args)]])
        for o, v in zip(refs[len(args):], res):
            o[...] = v

    return pl.pallas_call(body, name=name, out_shape=[jax.ShapeDtypeStruct(s, F32) for s in outs])(*args)


def _whole_vjp(f, name, args, cts):
    def body(*refs):
        vals = [r[...] for r in refs[:len(args)]]
        _, pull = jax.vjp(f, *vals)
        res = pull(tuple(r[...] for r in refs[len(args):len(args) + len(cts)]))
        for o, v in zip(refs[len(args) + len(cts):], res):
            o[...] = v

    return pl.pallas_call(body, name=name, out_shape=[jax.ShapeDtypeStruct(a.shape, F32) for a in args])(*args, *cts)


S5_SUB = 8
S5_STEPS = 3


def s5_tables(lam_re, lam_im):
    rows = S5_STEPS * S5_SUB

    def body(lr_ref, li_ref, sf_re, sf_im, sb_re, sb_im, cf_re, cf_im, cb_re, cb_im):
        lr, li = lr_ref[...], li_ref[...]

        def power(k):
            m = jnp.exp(k * lr)
            return m * jnp.cos(k * li), m * jnp.sin(k * li)

        srow = lax.broadcasted_iota(jnp.int32, (rows, 1), 0)
        k = jnp.left_shift(1, srow // S5_SUB)
        tt = srow % S5_SUB
        pr, pi = power(k.astype(F32))
        fwd, bwd = tt >= k, tt < S5_SUB - k
        sf_re[...], sf_im[...] = jnp.where(fwd, pr, 0.0), jnp.where(fwd, pi, 0.0)
        sb_re[...], sb_im[...] = jnp.where(bwd, pr, 0.0), jnp.where(bwd, pi, 0.0)
        trow = lax.broadcasted_iota(jnp.int32, (S5_SUB, 1), 0)
        cf_re[...], cf_im[...] = power((trow + 1).astype(F32))
        cb_re[...], cb_im[...] = power((S5_SUB - trow).astype(F32))

    shp = [jax.ShapeDtypeStruct((rows, S5_COLS), F32)] * 4 + [jax.ShapeDtypeStruct((S5_SUB, S5_COLS), F32)] * 4
    return pl.pallas_call(body, name="s5_tables", out_shape=shp)(lam_re, lam_im)


def _s5_coefs(steps_re, steps_im, carry_re, carry_im, reverse):
    sign = -1.0 if reverse else 1.0
    steps = [(steps_re[s * S5_SUB:(s + 1) * S5_SUB, :], sign * steps_im[s * S5_SUB:(s + 1) * S5_SUB, :])
             for s in range(S5_STEPS)]
    return steps, (carry_re[...], sign * carry_im[...])


def _s5_block_scan(ar, ai, coefs, cr, ci, reverse):
    steps, (qr, qi) = coefs
    for s, (pr, pi) in enumerate(steps):
        shift = S5_SUB - (1 << s) if reverse else (1 << s)
        sr, si = pltpu.roll(ar, shift, 0), pltpu.roll(ai, shift, 0)
        ar, ai = ar + pr * sr - pi * si, ai + pr * si + pi * sr
    br, bi = jnp.broadcast_to(cr, ar.shape), jnp.broadcast_to(ci, ai.shape)
    return ar + qr * br - qi * bi, ai + qr * bi + qi * br


def _s5_specs(n5, rev):
    def tok(q, b, c):
        return b * n5 + (n5 - 1 - c if rev else c)

    qcols = S5_COLS // S5_Q
    specs = [
        pl.BlockSpec((S5_CHUNK, LANE), lambda q, b, c: (tok(q, b, c), P_U // LANE + q)),
        pl.BlockSpec((1, LANE, qcols), lambda q, b, c: (q, 0, 0)),
        pl.BlockSpec((1, LANE, qcols), lambda q, b, c: (q, 0, 0)),
        pl.BlockSpec((1, qcols, LANE), lambda q, b, c: (q, 0, 0)),
        pl.BlockSpec((1, qcols, LANE), lambda q, b, c: (q, 0, 0)),
        pl.BlockSpec((S5_STEPS * S5_SUB, qcols), lambda q, b, c: (0, q)),
        pl.BlockSpec((S5_STEPS * S5_SUB, qcols), lambda q, b, c: (0, q)),
        pl.BlockSpec((S5_SUB, qcols), lambda q, b, c: (0, q)),
        pl.BlockSpec((S5_SUB, qcols), lambda q, b, c: (0, q)),
        pl.BlockSpec((1, 1, LANE), lambda q, b, c: (q, 0, 0)),
    ]
    return specs, tok, qcols


def s5_fwd(proj, wb_re, wb_im, wc_re, wc_im, sf_re, sf_im, cf_re, cf_im, dvec, bsz, seq, ride=None):
    t = proj.shape[0]
    n5 = seq // S5_CHUNK
    in_specs, tok, qcols = _s5_specs(n5, False)

    def body(u_ref, wbr, wbi, wcr, wci, sfr, sfi, cfr, cfi, d_ref, y_ref, xr_ref, xi_ref, cr_scr, ci_scr):
        @pl.when(pl.program_id(2) == 0)
        def _():
            cr_scr[...] = jnp.zeros_like(cr_scr)
            ci_scr[...] = jnp.zeros_like(ci_scr)

        u = u_ref[...]
        bur, bui = _dg(u, wbr[0], 1, 0), _dg(u, wbi[0], 1, 0)
        coefs = _s5_coefs(sfr, sfi, cfr, cfi, False)
        cr, ci = cr_scr[...], ci_scr[...]
        for r in range(S5_CHUNK // S5_SUB):
            rows = slice(r * S5_SUB, (r + 1) * S5_SUB)
            xr, xi = _s5_block_scan(bur[rows], bui[rows], coefs, cr, ci, False)
            xr_ref[rows, :], xi_ref[rows, :] = xr, xi
            cr, ci = xr[S5_SUB - 1:, :], xi[S5_SUB - 1:, :]
        cr_scr[...], ci_scr[...] = cr, ci
        y_ref[...] = _dg(xr_ref[...], wcr[0], 1, 0) - _dg(xi_ref[...], wci[0], 1, 0) + u * d_ref[0]

    def tile(w):
        return pl.BlockSpec((S5_CHUNK, w), lambda q, b, c: (tok(q, b, c), q))

    return hosted_call(
        body, name="s5_fwd", grid=(S5_Q, bsz, n5), in_specs=in_specs,
        out_specs=[tile(LANE), tile(qcols), tile(qcols)],
        out_shape=[jax.ShapeDtypeStruct((t, S5_WIDTH), F32), jax.ShapeDtypeStruct((t, S5_COLS), F32),
                   jax.ShapeDtypeStruct((t, S5_COLS), F32)],
        scratch=[pltpu.VMEM((1, qcols), F32)] * 2,
        args=(proj, wb_re, wb_im, wc_re, wc_im, sf_re, sf_im, cf_re, cf_im, dvec), ride=ride)


def s5_bwd(proj, wb_re, wb_im, wc_re, wc_im, sb_re, sb_im, cb_re, cb_im, dvec, xr_all, xi_all, dy, bsz, seq,
           ride=None):
    t = proj.shape[0]
    n5 = seq // S5_CHUNK
    in_specs, tok, qcols = _s5_specs(n5, True)
    blocks = S5_CHUNK // HALO

    def prev_rows(q, b, c):
        return (jnp.maximum(tok(q, b, c) * blocks - 1, 0), q)

    in_specs += [pl.BlockSpec((S5_CHUNK, qcols), lambda q, b, c: (tok(q, b, c), q)),
                 pl.BlockSpec((S5_CHUNK, qcols), lambda q, b, c: (tok(q, b, c), q)),
                 pl.BlockSpec((HALO, qcols), prev_rows), pl.BlockSpec((HALO, qcols), prev_rows),
                 pl.BlockSpec((S5_CHUNK, LANE), lambda q, b, c: (tok(q, b, c), q))]

    def body(u_ref, wbr, wbi, wcr, wci, sbr, sbi, cbr, cbi, d_ref, xr_ref, xi_ref, pr_ref, pi_ref, dy_ref,
             du_ref, dwbr, dwbi, dwcr, dwci, dar, dai, dd_ref, gr_scr, gi_scr, gr_all, gi_all):
        b, c = pl.program_id(1), pl.program_id(2)

        @pl.when(c == 0)
        def _():
            gr_scr[...] = jnp.zeros_like(gr_scr)
            gi_scr[...] = jnp.zeros_like(gi_scr)

        @pl.when((b == 0) & (c == 0))
        def _():
            for r in (dwbr, dwbi, dwcr, dwci, dar, dai, dd_ref):
                r[...] = jnp.zeros_like(r)

        u, dy_v = u_ref[...], dy_ref[...]
        g0r, g0i = _dg(dy_v, wcr[0], 1, 1), -_dg(dy_v, wci[0], 1, 1)
        coefs = _s5_coefs(sbr, sbi, cbr, cbi, True)
        cr, ci = gr_scr[...], gi_scr[...]
        for r in reversed(range(S5_CHUNK // S5_SUB)):
            rows = slice(r * S5_SUB, (r + 1) * S5_SUB)
            br, bi = _s5_block_scan(g0r[rows], g0i[rows], coefs, cr, ci, True)
            gr_all[rows, :], gi_all[rows, :] = br, bi
            cr, ci = br[:1, :], bi[:1, :]
        gr_scr[...], gi_scr[...] = cr, ci
        gr, gi = gr_all[...], gi_all[...]

        row = lax.broadcasted_iota(jnp.int32, (S5_CHUNK, 1), 0)
        xr, xi = xr_ref[...], xi_ref[...]
        is_first = (c == n5 - 1)
        hr = jnp.where(is_first, 0.0, pr_ref[...][HALO - 1:, :])
        hi = jnp.where(is_first, 0.0, pi_ref[...][HALO - 1:, :])
        xpr = jnp.where(row >= 1, pltpu.roll(xr, 1, 0), hr)
        xpi = jnp.where(row >= 1, pltpu.roll(xi, 1, 0), hi)
        dar[0] += jnp.sum(xpr * gr + xpi * gi, axis=0, keepdims=True)
        dai[0] += jnp.sum(xpr * gi - xpi * gr, axis=0, keepdims=True)
        du_ref[...] = _dg(gr, wbr[0], 1, 1) + _dg(gi, wbi[0], 1, 1) + dy_v * d_ref[0]
        dwbr[0] += _dg(u, gr, 0, 0)
        dwbi[0] += _dg(u, gi, 0, 0)
        dwcr[0] += _dg(xr, dy_v, 0, 0)
        dwci[0] -= _dg(xi, dy_v, 0, 0)
        dd_ref[0] += jnp.sum(dy_v * u, axis=0, keepdims=True)

    def acc(shape):
        return pl.BlockSpec((1,) + shape, lambda q, b, c: (q, 0, 0))

    return hosted_call(
        body, name="s5_bwd", grid=(S5_Q, bsz, n5), in_specs=in_specs,
        out_specs=[pl.BlockSpec((S5_CHUNK, LANE), lambda q, b, c: (tok(q, b, c), q)),
                   acc((LANE, qcols)), acc((LANE, qcols)), acc((qcols, LANE)), acc((qcols, LANE)),
                   acc((1, qcols)), acc((1, qcols)), acc((1, LANE))],
        out_shape=[jax.ShapeDtypeStruct((t, S5_WIDTH), F32),
                   jax.ShapeDtypeStruct((S5_Q, LANE, qcols), F32), jax.ShapeDtypeStruct((S5_Q, LANE, qcols), F32),
                   jax.ShapeDtypeStruct((S5_Q, qcols, LANE), F32), jax.ShapeDtypeStruct((S5_Q, qcols, LANE), F32),
                   jax.ShapeDtypeStruct((S5_Q, 1, qcols), F32), jax.ShapeDtypeStruct((S5_Q, 1, qcols), F32),
                   jax.ShapeDtypeStruct((S5_Q, 1, LANE), F32)],
        scratch=[pltpu.VMEM((1, qcols), F32)] * 2 + [pltpu.VMEM((S5_CHUNK, qcols), F32)] * 2,
        args=(proj, wb_re, wb_im, wc_re, wc_im, sb_re, sb_im, cb_re, cb_im, dvec, xr_all, xi_all, xr_all, xi_all, dy),
        ride=ride)


def _blockdiag_b(bb):
    b4 = bb.reshape(S5_Q, 8, S5_STATE, S5_GROUP_CH)
    eye = jnp.eye(8, dtype=bb.dtype)
    w = jnp.einsum("qgph,gk->qghkp", b4, eye)
    return w.reshape(S5_Q, LANE, S5_COLS // S5_Q)


def _unblock_b(dw):
    d = dw.reshape(S5_Q, 8, S5_GROUP_CH, 8, S5_STATE)
    d = jnp.einsum("qghgp->qgph", d)
    return d.reshape(S5_COLS, S5_GROUP_CH)


def _blockdiag_c(cc):
    c4 = cc.reshape(S5_Q, 8, S5_GROUP_CH, S5_STATE)
    eye = jnp.eye(8, dtype=cc.dtype)
    w = jnp.einsum("qghp,gk->qgpkh", c4, eye)
    return w.reshape(S5_Q, S5_COLS // S5_Q, LANE)


def _unblock_c(dw):
    d = dw.reshape(S5_Q, 8, S5_STATE, 8, S5_GROUP_CH)
    d = jnp.einsum("qgpgh->qghp", d)
    return d.reshape(S5_GROUPS, S5_GROUP_CH, S5_STATE)


def ada_fwd(c_all, w_loc, b_loc):
    def body(c_ref, w_ref, b_ref, o_ref):
        o_ref[...] = _dg(_silu(c_ref[...]), w_ref[...], 1, 0) + b_ref[...]

    return pl.pallas_call(body, name="ada_fwd",
                          out_shape=jax.ShapeDtypeStruct((c_all.shape[0], w_loc.shape[1]), F32),
                          compiler_params=_cp())(c_all, w_loc, b_loc)


def ada_bwd(c_all, dmod_all, dmod_cols):
    def body(c_ref, da_ref, dc_ref, gb_ref, gw_ref):
        gb_ref[...] = jnp.sum(da_ref[...], axis=0, keepdims=True)
        gw_ref[...] = _dg(_silu(c_ref[...]), dc_ref[...], 0, 0)

    return pl.pallas_call(body, name="ada_bwd",
                          out_shape=[jax.ShapeDtypeStruct((1, dmod_all.shape[1]), F32),
                                     jax.ShapeDtypeStruct((c_all.shape[1], dmod_cols.shape[1]), F32)],
                          compiler_params=_cp())(c_all, dmod_all, dmod_cols)


_FLIPS = [(0, 0, 1), (1, 0, 0), (0, 1, 0), (1, 1, 0), (1, 0, 1), (0, 1, 1), (1, 1, 1)]


def _exchange_ops(srcs, outs, sems, gather):
    n = len(srcs)
    send_sems, recv_sems, loc_sems = sems
    x, y, c = lax.axis_index("x"), lax.axis_index("y"), lax.axis_index("c")
    me = 4 * x + 2 * y + c
    peers = []
    for fx, fy, fc in _FLIPS:
        px, py, pc = (1 - x if fx else x), (1 - y if fy else y), (1 - c if fc else c)
        peers.append(((px, py, pc), 4 * px + 2 * py + pc))

    def copy(k, j, slot_src, slot_dst):
        src = srcs[k] if gather[k] else srcs[k].at[slot_src]
        return pltpu.make_async_remote_copy(src_ref=src, dst_ref=outs[k].at[slot_dst],
                                            send_sem=send_sems.at[k, j], recv_sem=recv_sems.at[k, j],
                                            device_id=peers[j][0], device_id_type=MESH)

    def local(k):
        own = srcs[k] if gather[k] else srcs[k].at[me]
        return pltpu.make_async_copy(own, outs[k].at[me], loc_sems.at[k])

    def start():
        for k in range(n):
            for j in range(N_DEV - 1):
                copy(k, j, peers[j][1], me).start()
            local(k).start()

    def wait():
        for k in range(n):
            for j in range(N_DEV - 1):
                copy(k, j, me, peers[j][1]).wait_recv()
        for k in range(n):
            for j in range(N_DEV - 1):
                copy(k, j, peers[j][1], me).wait_send()
            local(k).wait()

    return start, wait


def _gather_two_level(srcs, outs, sems):
    n = len(srcs)
    send_sems, recv_sems, loc_sems = sems
    x, y, c = lax.axis_index("x"), lax.axis_index("y"), lax.axis_index("c")
    slot = lambda px, py, pc: 4 * px + 2 * py + pc
    me, sibling = (x, y, c), (x, y, 1 - c)
    chips = [(1 - x, y), (x, 1 - y), (1 - x, 1 - y)]

    def copy(k, j, block, to, own=False):
        return pltpu.make_async_remote_copy(src_ref=srcs[k] if own else outs[k].at[slot(*block)],
                                            dst_ref=outs[k].at[slot(*block)],
                                            send_sem=send_sems.at[k, j], recv_sem=recv_sems.at[k, j],
                                            device_id=to, device_id_type=MESH)

    locs = [pltpu.make_async_copy(srcs[k], outs[k].at[slot(*me)], loc_sems.at[k]) for k in range(n)]
    for k in range(n):
        locs[k].start()
        copy(k, 0, me, sibling, own=True).start()
        for j, chip in enumerate(chips):
            copy(k, 1 + j, me, (*chip, c), own=True).start()
    for j, chip in enumerate(chips):
        for k in range(n):
            copy(k, 1 + j, (*chip, c), me).wait_recv()
            copy(k, 4 + j, (*chip, c), sibling).start()
    for k in range(n):
        copy(k, 0, sibling, me).wait_recv()
        for j, chip in enumerate(chips):
            copy(k, 4 + j, (*chip, 1 - c), me).wait_recv()
    for k in range(n):
        copy(k, 0, me, sibling, own=True).wait_send()
        for j, chip in enumerate(chips):
            copy(k, 1 + j, me, (*chip, c), own=True).wait_send()
            copy(k, 4 + j, (*chip, c), sibling).wait_send()
        locs[k].wait()


def gather_two_level(name, arrs):
    n = len(arrs)
    specs, shapes, sems = _exchange_parts(arrs, [True] * n)

    def body(*refs):
        _gather_two_level(refs[:n], refs[n:2 * n], refs[2 * n:])

    return pl.pallas_call(
        body, name=name, in_specs=specs, out_specs=specs, out_shape=shapes, scratch_shapes=sems,
        compiler_params=pltpu.CompilerParams(has_side_effects=True),
    )(*arrs)


def _exchange_parts(arrs, gather):
    n = len(arrs)
    any_spec = pl.BlockSpec(memory_space=pl.ANY)
    shapes = [jax.ShapeDtypeStruct(((N_DEV,) + a.shape) if g else a.shape, a.dtype) for a, g in zip(arrs, gather)]
    sems = [pltpu.SemaphoreType.DMA((n, N_DEV - 1)), pltpu.SemaphoreType.DMA((n, N_DEV - 1)),
            pltpu.SemaphoreType.DMA((n,))]
    return [any_spec] * n, shapes, sems


def exchange(name, arrs, gather):
    n = len(arrs)
    specs, shapes, sems = _exchange_parts(arrs, gather)

    def body(*refs):
        start, wait = _exchange_ops(refs[:n], refs[n:2 * n], refs[2 * n:], gather)
        start()
        wait()

    return pl.pallas_call(
        body, name=name, in_specs=specs, out_specs=specs, out_shape=shapes, scratch_shapes=sems,
        compiler_params=pltpu.CompilerParams(has_side_effects=True),
    )(*arrs)


def hosted_call(body, *, name, grid, in_specs, out_specs, out_shape, args, scratch=(), ride=None):
    sem = ("arbitrary",) * len(grid)
    if ride is None:
        res = pl.pallas_call(body, name=name, grid=grid, in_specs=in_specs, out_specs=out_specs, out_shape=out_shape,
                             scratch_shapes=list(scratch), compiler_params=_cp(*sem))(*args)
        return list(res), []
    arrs, gather = ride
    n, n_in, n_out, n_scr = len(arrs), len(in_specs), len(out_specs), len(scratch)
    specs, shapes, sems = _exchange_parts(arrs, gather)

    def both(*refs):
        ins, srcs = refs[:n_in], refs[n_in:n_in + n]
        outs, landed = refs[n_in + n:n_in + n + n_out], refs[n_in + n + n_out:n_in + 2 * n + n_out]
        scr, ex_sems = refs[n_in + 2 * n + n_out:n_in + 2 * n + n_out + n_scr], refs[n_in + 2 * n + n_out + n_scr:]
        start, wait = _exchange_ops(srcs, landed, ex_sems, gather)
        first = functools.reduce(lambda a, b: a & b, [pl.program_id(d) == 0 for d in range(len(grid))])
        last = functools.reduce(lambda a, b: a & b, [pl.program_id(d) == grid[d] - 1 for d in range(len(grid))])
        pl.when(first)(start)
        body(*ins, *outs, *scr)
        pl.when(last)(wait)

    res = pl.pallas_call(
        both, name=name, grid=grid, in_specs=list(in_specs) + specs, out_specs=list(out_specs) + specs,
        out_shape=list(out_shape) + shapes, scratch_shapes=list(scratch) + sems, compiler_params=_cp(*sem),
    )(*args, *arrs)
    return list(res[:n_out]), list(res[n_out:])


def sum_slots(name, slots, tr):
    _, r, c = slots.shape

    def body(s_ref, o_ref):
        acc = s_ref[0].astype(F32)
        for j in range(1, N_DEV):
            acc = acc + s_ref[j].astype(F32)
        o_ref[...] = acc

    return pl.pallas_call(
        body, name=name, grid=(r // tr,), in_specs=[pl.BlockSpec((N_DEV, tr, c), lambda i: (0, i, 0))],
        out_specs=pl.BlockSpec((tr, c), lambda i: (i, 0)), out_shape=jax.ShapeDtypeStruct((r, c), F32),
        compiler_params=_cp("parallel"),
    )(slots)


def adamw(name, g, w, m, v, tr):
    slots = g.ndim == 3
    r, c = w.shape
    c1, c2 = 1.0 - ADAM_B1 ** ADAM_STEP, 1.0 - ADAM_B2 ** ADAM_STEP

    def body(g_ref, w_ref, m_ref, v_ref, go, do, mo, vo):
        if slots:
            gg = g_ref[0].astype(F32)
            for j in range(1, N_DEV):
                gg = gg + g_ref[j].astype(F32)
        else:
            gg = g_ref[...]
        mn = ADAM_B1 * m_ref[...] + (1.0 - ADAM_B1) * gg
        vn = ADAM_B2 * v_ref[...] + (1.0 - ADAM_B2) * (gg * gg)
        go[...], mo[...], vo[...] = gg, mn, vn
        do[...] = -ADAM_LR * ((mn / c1) / (jnp.sqrt(vn / c2) + ADAM_EPS) + ADAM_WD * w_ref[...])

    blk = pl.BlockSpec((tr, c), lambda i: (i, 0))
    gspec = pl.BlockSpec((N_DEV, tr, c), lambda i: (0, i, 0)) if slots else blk
    return pl.pallas_call(
        body, name=name, grid=(r // tr,), in_specs=[gspec, blk, blk, blk], out_specs=[blk] * 4,
        out_shape=[jax.ShapeDtypeStruct((r, c), F32)] * 4, compiler_params=_cp("parallel"),
    )(g, w, m, v)


def _lane_rows(n):
    return -(-n // (8 * LANE)) * 8


def _pack(arrs):
    pieces = []
    for a in arrs:
        n = math.prod(a.shape)
        flat = a.reshape(-1).astype(F32)
        pieces.append(jnp.pad(flat, (0, _lane_rows(n) * LANE - n)).reshape(_lane_rows(n), LANE))
    return jnp.concatenate(pieces, axis=0)


def _unpack(buf, shapes):
    out, off = [], 0
    for s in shapes:
        n = math.prod(s)
        out.append(buf[off:off + _lane_rows(n)].reshape(-1)[:n].reshape(s))
        off += _lane_rows(n)
    return out


FF_CHUNK = D_FF
DW_TOKENS = 2048
FFN_TM = 256


def _resident(shape):
    return pl.BlockSpec(shape, lambda i: (0,) * len(shape), pipeline_mode=pl.Buffered(1))


def _ffn_fwd(tag, x, sc, sh, g, w1, w3, w2, lg, lb, seq, tm, ride=None, target=None):
    t = x.shape[0]
    tm = min(FFN_TM, tm)
    tps = seq // tm
    ln = _res_ln(0.5)
    head = target is not None

    def body(x_ref, sc_ref, sh_ref, g_ref, lg_ref, lb_ref, w1_ref, w3_ref, w2_ref, *rest):
        if head:
            t_ref, y_ref, h_ref, a_ref, b_ref, f_ref, l_ref = rest
        else:
            y_ref, h_ref, a_ref, b_ref, f_ref = rest
        xv = x_ref[...]
        h = (xv * (1.0 + sc_ref[0]) + sh_ref[0]).astype(BF16)
        h_ref[...] = h
        acc = jnp.zeros((tm, D_MODEL), F32)
        for j in range(D_FF // FF_CHUNK):
            sl = slice(j * FF_CHUNK, (j + 1) * FF_CHUNK)
            a = _dg(h, w1_ref[sl, :], 1, 1)
            b = _dg(h, w3_ref[sl, :], 1, 1)
            a_ref[:, sl] = a
            b_ref[:, sl] = b
            acc = acc + _dg(_silu(a) * b, w2_ref[sl, :], 1, 0)
        f_ref[...] = acc
        y = ln(xv, acc, g_ref[0], lg_ref[...], lb_ref[...])[0]
        if head:
            @pl.when(pl.program_id(0) == 0)
            def _():
                l_ref[...] = jnp.zeros_like(l_ref)

            e = y - t_ref[...]
            y_ref[...] = e * (1.0 / D_MODEL)
            l_ref[...] += 0.5 * jnp.sum(jnp.mean(e * e, axis=-1, keepdims=True), axis=0, keepdims=True)
        else:
            y_ref[...] = y

    row = lambda c: pl.BlockSpec((tm, c), lambda i: (i, 0))
    per_seq = pl.BlockSpec((1, 1, D_MODEL), lambda i: (i // tps, 0, 0))
    vec = pl.BlockSpec((1, D_MODEL), lambda i: (0, 0))
    res, landed = hosted_call(
        body, name=tag + "_fwd", grid=(t // tm,),
        in_specs=[row(D_MODEL), per_seq, per_seq, per_seq, vec, vec,
                  _resident((D_FF, D_MODEL)), _resident((D_FF, D_MODEL)), _resident((D_FF, D_MODEL))]
        + ([row(D_MODEL)] if head else []),
        out_specs=[row(D_MODEL), row(D_MODEL), row(D_FF), row(D_FF), row(D_MODEL)]
        + ([pl.BlockSpec((1, 1), lambda i: (0, 0))] if head else []),
        out_shape=[jax.ShapeDtypeStruct((t, D_MODEL), F32), jax.ShapeDtypeStruct((t, D_MODEL), BF16),
                   jax.ShapeDtypeStruct((t, D_FF), F32), jax.ShapeDtypeStruct((t, D_FF), F32),
                   jax.ShapeDtypeStruct((t, D_MODEL), F32)] + ([jax.ShapeDtypeStruct((1, 1), F32)] if head else []),
        args=(x, sc, sh, g, lg, lb, w1, w3, w2) + ((target,) if head else ()), ride=ride)
    first = (res[0], res[5][0, 0]) if head else res[0]
    return first, tuple(res[1:5]), landed


def _ffn_bwd(tag, dy, x, sc, sh, g, w1, w3, w2, lg, lb, res, seq, tm, ride=None, chain=None):
    h, a, b, f = res
    t = x.shape[0]
    tmk = min(FFN_TM, tm)
    tps = seq // tmk
    ln = _res_ln(0.5)

    def body(dy_ref, x_ref, f_ref, a_ref, b_ref, sc_ref, sh_ref, g_ref, lg_ref, lb_ref, w1_ref, w3_ref, w2_ref,
             dx_ref, da_ref, db_ref, s_ref, df_ref, dsc_ref, dsh_ref, dg_ref, dlg_ref, dlb_ref):
        i = pl.program_id(0)

        @pl.when(i % tps == 0)
        def _():
            for r in (dsc_ref, dsh_ref, dg_ref):
                r[...] = jnp.zeros_like(r)

        @pl.when(i == 0)
        def _():
            dlg_ref[...] = jnp.zeros_like(dlg_ref)
            dlb_ref[...] = jnp.zeros_like(dlb_ref)

        xv = x_ref[...]
        _, pull = jax.vjp(ln, xv, f_ref[...], g_ref[0], lg_ref[...], lb_ref[...])
        dx_res, df, dg, dlg, dlb = pull((dy_ref[...],))
        dfb = df.astype(BF16)
        df_ref[...] = dfb
        dh = jnp.zeros((tmk, D_MODEL), F32)
        for j in range(D_FF // FF_CHUNK):
            sl = slice(j * FF_CHUNK, (j + 1) * FF_CHUNK)
            ds = _dg(dfb, w2_ref[sl, :], 1, 1)
            av, bv = a_ref[:, sl], b_ref[:, sl]
            sg = jax.nn.sigmoid(av)
            si = av * sg
            s_ref[:, sl] = (si * bv).astype(BF16)
            da = (ds * bv * (sg * (1.0 + av * (1.0 - sg)))).astype(BF16)
            db = (ds * si).astype(BF16)
            da_ref[:, sl] = da
            db_ref[:, sl] = db
            dh = dh + _dg(da, w1_ref[sl, :], 1, 0) + _dg(db, w3_ref[sl, :], 1, 0)
        dx_ref[...] = dx_res + dh * (1.0 + sc_ref[0])
        dsc_ref[0] += jnp.sum(dh * xv, axis=0, keepdims=True)
        dsh_ref[0] += jnp.sum(dh, axis=0, keepdims=True)
        dg_ref[0] += dg
        dlg_ref[...] += dlg
        dlb_ref[...] += dlb

    row = lambda c: pl.BlockSpec((tmk, c), lambda i: (i, 0))
    per_seq = pl.BlockSpec((1, 1, D_MODEL), lambda i: (i // tps, 0, 0))
    vec = pl.BlockSpec((1, D_MODEL), lambda i: (0, 0))
    seq_shape = jax.ShapeDtypeStruct(sc.shape, F32)
    vec_shape = jax.ShapeDtypeStruct((1, D_MODEL), F32)
    (dx, da, db, s, df, dsc, dsh, dg, dlg, dlb), landed = hosted_call(
        body, name=tag + "_bwd", grid=(t // tmk,),
        in_specs=[row(D_MODEL), row(D_MODEL), row(D_MODEL), row(D_FF), row(D_FF), per_seq, per_seq, per_seq, vec, vec,
                  _resident((D_FF, D_MODEL)), _resident((D_FF, D_MODEL)), _resident((D_FF, D_MODEL))],
        out_specs=[row(D_MODEL), row(D_FF), row(D_FF), row(D_FF), row(D_MODEL), per_seq, per_seq, per_seq, vec, vec],
        out_shape=[jax.ShapeDtypeStruct((t, D_MODEL), F32), jax.ShapeDtypeStruct((t, D_FF), BF16),
                   jax.ShapeDtypeStruct((t, D_FF), BF16), jax.ShapeDtypeStruct((t, D_FF), BF16),
                   jax.ShapeDtypeStruct((t, D_MODEL), BF16), seq_shape, seq_shape, seq_shape, vec_shape, vec_shape],
        args=(dy, x, f, a, b, sc, sh, g, lg, lb, w1, w3, w2), ride=ride)
    tt = min(DW_TOKENS, seq)
    shards = lambda dw: dw.reshape(N_DEV, D_FF // N_DEV, D_MODEL)
    if chain is None:
        dw2, landed = mm_tn(tag + "_dw2", s, df, D_FF // 2, D_MODEL, tt, BF16), []
    else:
        dw2, landed = mm_tn(tag + "_dw2", s, df, D_FF // 2, D_MODEL, tt, BF16, ride=chain((dsh, dsc, dg), dlg, dlb))
    dw1, (s_w2,) = mm_tn(tag + "_dw1", da, h, D_FF // 2, D_MODEL, tt, BF16, ride=([shards(dw2)], [False]))
    dw3, (s_w1,) = mm_tn(tag + "_dw3", db, h, D_FF // 2, D_MODEL, tt, BF16, ride=([shards(dw1)], [False]))
    return dx, (dsh, dsc, dg), (s_w1, shards(dw3), s_w2, dlg, dlb), landed


def kernel(x, c, w_ada, b_ada, ffn1_w1, ffn1_w3, ffn1_w2, ln1_g, ln1_b, w_in, conv_w, conv_b, dt_bias, a_log, d_ssd, ssd_norm_w, s5_a_re, s5_a_im, s5_log_dt, s5_b_re, s5_b_im, s5_c_re, s5_c_im, s5_d, w_glu, b_glu, w_out, ln2_g, ln2_b, ffn2_w1, ffn2_w3, ffn2_w2, ln3_g, ln3_b, loss_target, m_w_ada, m_b_ada, m_ffn1_w1, m_ffn1_w3, m_ffn1_w2, m_ln1_g, m_ln1_b, m_w_in, m_conv_w, m_conv_b, m_dt_bias, m_a_log, m_d_ssd, m_ssd_norm_w, m_s5_a_re, m_s5_a_im, m_s5_log_dt, m_s5_b_re, m_s5_b_im, m_s5_c_re, m_s5_c_im, m_s5_d, m_w_glu, m_b_glu, m_w_out, m_ln2_g, m_ln2_b, m_ffn2_w1, m_ffn2_w3, m_ffn2_w2, m_ln3_g, m_ln3_b, v_w_ada, v_b_ada, v_ffn1_w1, v_ffn1_w3, v_ffn1_w2, v_ln1_g, v_ln1_b, v_w_in, v_conv_w, v_conv_b, v_dt_bias, v_a_log, v_d_ssd, v_ssd_norm_w, v_s5_a_re, v_s5_a_im, v_s5_log_dt, v_s5_b_re, v_s5_b_im, v_s5_c_re, v_s5_c_im, v_s5_d, v_w_glu, v_b_glu, v_w_out, v_ln2_g, v_ln2_b, v_ffn2_w1, v_ffn2_w3, v_ffn2_w2, v_ln3_g, v_ln3_b):
    given = dict(locals())
    bsz, seq, _ = x.shape
    t = bsz * seq
    tm = min(512, seq)
    me = 4 * lax.axis_index("x") + 2 * lax.axis_index("y") + lax.axis_index("c")
    x0 = x.reshape(t, D_MODEL)
    target = loss_target.reshape(t, D_MODEL)

    tr16 = lambda w: w[0].T.astype(BF16)
    whole = lambda g: g.reshape(N_DEV * g.shape[1], g.shape[2])
    g_f1w1, g_f1w3, g_f1w2, g_c = gather_two_level(
        "gather_ffn1", [tr16(ffn1_w1), tr16(ffn1_w3), ffn1_w2[0].astype(BF16), c])
    f1w1, f1w3, f1w2 = whole(g_f1w1), whole(g_f1w3), whole(g_f1w2)
    c_all = whole(g_c)

    n_loc = w_ada.shape[2]
    b_loc = lax.dynamic_slice(b_ada, (0, me * n_loc), (1, n_loc))
    mod_cols = ada_fwd(c_all, w_ada[0], b_loc)
    g_mod, = exchange("gather_mod", [mod_cols], [True])
    mine = lax.dynamic_slice(g_mod, (0, me * bsz, 0), (N_DEV, bsz, n_loc))
    mod = jnp.transpose(mine, (1, 0, 2)).reshape(bsz, N_MOD, 1, D_MODEL)
    sh1, sc1, g1, sh2, sc2, g2, sh3, sc3, g3 = [mod[:, k] for k in range(N_MOD)]

    x1, res1, (g_win, g_glu, g_out, g_conv, g_f2w1) = _ffn_fwd(
        "ffn1", x0, sc1, sh1, g1, f1w1, f1w3, f1w2, ln1_g, ln1_b, seq, tm,
        ride=([tr16(w_in), w_glu[0].astype(BF16), w_out[0].astype(BF16), conv_w[0], tr16(ffn2_w1)], [True] * 5))
    win = whole(g_win)
    wglu = whole(g_glu).astype(F32)
    wout = whole(g_out)
    wo_ssd, wo_s5 = wout[:SSD_WIDTH], wout[SSD_WIDTH:]
    convw = jnp.transpose(g_conv, (1, 0, 2)).reshape(CONV_K, CONV_CH)
    w_z, w_xbc = win[:SSD_WIDTH], win[SSD_WIDTH:SSD_WIDTH + CONV_CH]
    w_dt = win[SSD_WIDTH + CONV_CH:SSD_WIDTH + CONV_CH + SSD_HEADS]
    w_u = win[SSD_WIDTH + CONV_CH + SSD_HEADS:]
    dt_pad = [jnp.pad(w_dt[HEADS_PER_GROUP * g:HEADS_PER_GROUP * (g + 1)], ((0, LANE - HEADS_PER_GROUP), (0, 0)))
              for g in range(SSD_GROUPS)]
    w_dtp = jnp.concatenate(dt_pad, axis=0)
    w_proj = jnp.concatenate([w_xbc, w_z, w_u, w_dtp], axis=0)

    h2, = rowwise_fwd("mix_mod", f_modulate, [x1], [sc2, sh2], [], [(D_MODEL, BF16)], seq, tm)
    proj = mm_nt("mix_proj", [h2], [w_proj], 2 * tm, P_COLS // 2)
    xc = conv_fwd(proj, convw, conv_b, seq, tm)
    dtb = jnp.pad(dt_bias.reshape(SSD_GROUPS, 1, HEADS_PER_GROUP), ((0, 0), (0, 0), (0, LANE - HEADS_PER_GROUP)))
    alog = jnp.pad(a_log.reshape(SSD_GROUPS, 1, HEADS_PER_GROUP), ((0, 0), (0, 0), (0, LANE - HEADS_PER_GROUP)))
    dcol = jnp.pad(d_ssd.reshape(SSD_GROUPS, HEADS_PER_GROUP, 1), ((0, 0), (0, LANE - HEADS_PER_GROUP), (0, 0)))
    nw = ssd_norm_w.reshape(SSD_GROUPS, 1, GROUP_COLS)
    (y_ssd, hprev), (g_f2w3,) = ssd_fwd(xc, proj, dtb, alog, dcol, nw, bsz, seq,
                                        ride=([tr16(ffn2_w3)], [True]))

    a_re2, a_im2, ldt2 = s5_a_re[0], s5_a_im[0], s5_log_dt.reshape(S5_GROUPS, 1)
    ab_re, ab_im, f_re, f_im = _whole(_disc_a, "s5_disc_a", [a_re2, a_im2, ldt2], [(S5_GROUPS, S5_STATE)] * 4)
    b_re2, b_im2 = s5_b_re.reshape(S5_COLS, S5_GROUP_CH), s5_b_im.reshape(S5_COLS, S5_GROUP_CH)
    fr_col, fi_col = f_re.reshape(S5_COLS, 1), f_im.reshape(S5_COLS, 1)
    bb_re, bb_im = _whole(_disc_b, "s5_disc_b", [fr_col, fi_col, b_re2, b_im2], [(S5_COLS, S5_GROUP_CH)] * 2)
    wb_re, wb_im = _blockdiag_b(bb_re).astype(BF16), _blockdiag_b(bb_im).astype(BF16)
    wc_re, wc_im = _blockdiag_c(s5_c_re[0]).astype(BF16), _blockdiag_c(s5_c_im[0]).astype(BF16)
    dt5 = jnp.exp(ldt2)
    lam_re, lam_im = (dt5 * a_re2).reshape(1, S5_COLS), (dt5 * a_im2).reshape(1, S5_COLS)
    sf_re, sf_im, sb_re, sb_im, cf_re, cf_im, cb_re, cb_im = s5_tables(lam_re, lam_im)
    d5 = s5_d.reshape(S5_Q, 1, LANE)
    (y5, xr_all, xi_all), (g_f2w2,) = s5_fwd(
        proj, wb_re, wb_im, wc_re, wc_im, sf_re, sf_im, cf_re, cf_im, d5, bsz, seq,
        ride=([ffn2_w2[0].astype(BF16)], [True]))
    f2w1, f2w3, f2w2 = whole(g_f2w1), whole(g_f2w3), whole(g_f2w2)
    o5, = rowwise_fwd("s5_glu", f_glu, [y5], [], [wglu, b_glu], [(S5_WIDTH, F32)], seq, tm)

    mix = mm_nn("mix_out", [y_ssd, o5], [wo_ssd, wo_s5], tm, D_MODEL)
    x2, = rowwise_fwd("mix_ln", _res_ln(1.0), [x1, mix], [g2], [ln2_g, ln2_b], [(D_MODEL, F32)], seq, tm)

    (dy, loss_loc), res3, _ = _ffn_fwd("ffn2", x2, sc3, sh3, g3, f2w1, f2w3, f2w2, ln3_g, ln3_b, seq, tm, target=target)

    dx2, dmod3, (s_f2w1, d_f2w3, s_f2w2, d_ln3g, d_ln3b), _ = _ffn_bwd(
        "ffn2", dy, x2, sc3, sh3, g3, f2w1, f2w3, f2w2, ln3_g, ln3_b, res3, seq, tm)

    (dx1_a, dmix), (dg2,), (d_ln2g, d_ln2b) = rowwise_bwd(
        "mix_ln_b", _res_ln(1.0), [x1, mix], [g2], [ln2_g, ln2_b], [dx2], seq, tm, [F32, BF16])
    tw = min(DW_TOKENS, seq)
    d_wo = jnp.concatenate([mm_tn("mix_dwo_ssd", y_ssd, dmix, SSD_WIDTH, D_MODEL, tw, BF16),
                            mm_tn("mix_dwo_s5", o5, dmix, S5_WIDTH, D_MODEL, tw, BF16)], axis=0)
    dy_mixed = mm_nt("mix_dy", [dmix], [wout], tm, D_MODEL)
    dy_ssd, do5 = dy_mixed, (dy_mixed, SSD_WIDTH, S5_WIDTH)

    (dy5,), _, (d_wglu, d_bglu) = rowwise_bwd("s5_glu_b", f_glu, [y5], [], [wglu, b_glu], [do5], seq, tm, [F32])
    (du, dwbr, dwbi, dwcr, dwci, dab_re, dab_im, dd5), (s_f2w3, s_out, s_glu) = s5_bwd(
        proj, wb_re, wb_im, wc_re, wc_im, sb_re, sb_im, cb_re, cb_im, d5, xr_all, xi_all, dy5, bsz, seq,
        ride=([d_f2w3, d_wo.reshape(N_DEV, D_MODEL // N_DEV, D_MODEL),
               d_wglu.reshape(N_DEV, S5_WIDTH // N_DEV, S5_WIDTH).astype(BF16)], [False] * 3))
    dbb_re, dbb_im = _unblock_b(dwbr), _unblock_b(dwbi)
    dfr_col, dfi_col, d_b_re, d_b_im = _whole_vjp(_disc_b, "s5_disc_b_b", [fr_col, fi_col, b_re2, b_im2],
                                                  [dbb_re, dbb_im])
    d_a_re, d_a_im, d_ldt = _whole_vjp(
        _disc_a, "s5_disc_a_b", [a_re2, a_im2, ldt2],
        [dab_re.reshape(S5_GROUPS, S5_STATE), dab_im.reshape(S5_GROUPS, S5_STATE),
         dfr_col.reshape(S5_GROUPS, S5_STATE), dfi_col.reshape(S5_GROUPS, S5_STATE)])
    d_c_re, d_c_im = _unblock_c(dwcr), _unblock_c(dwci)

    dxs, dbm, dcm, ddt, dz, ddtb, dalog, ddcol, dnw = ssd_bwd(xc, proj, dtb, alog, dcol, nw, hprev, dy_ssd, bsz, seq)
    dpre, d_convw, d_convb = conv_bwd_pre(proj, convw, conv_b, dxs, dbm, dcm, seq, tm)
    dxbc = conv_bwd_x(dpre, convw, seq, tm)

    dw_xbc = mm_tn("mix_dw_xbc", dxbc, h2, CONV_CH, D_MODEL, tw, BF16)
    dw_z = mm_tn("mix_dw_z", dz, h2, SSD_WIDTH, D_MODEL, tw, BF16)
    dw_u = mm_tn("mix_dw_u", du, h2, S5_WIDTH, D_MODEL, tw, BF16)
    dw_dt = mm_tn("mix_dw_dt", ddt, h2, 2 * LANE, D_MODEL, tw, BF16)
    dw_dt8 = jnp.concatenate([dw_dt[LANE * g:LANE * g + HEADS_PER_GROUP] for g in range(SSD_GROUPS)], axis=0)
    d_win = jnp.concatenate([dw_z, dw_xbc, dw_dt8, dw_u], axis=0)
    dh2, (s_win,) = mm_nn("mix_dh", [dxbc, dz, du, ddt], [w_xbc, w_z, w_u, w_dtp], tm, D_MODEL,
                          ride=([d_win.reshape(N_DEV, IN_COLS // N_DEV, D_MODEL)], [False]))
    (dx1,), (dsc2, dsh2), _ = rowwise_bwd("mix_mod_b", f_modulate, [x1], [sc2, sh2], [], [dh2], seq, tm, [F32],
                                          add_rows={0: dx1_a})

    packing = {}

    def small_and_dmod(dmod1, d_ln1g, d_ln1b):
        dmod = jnp.concatenate(list(dmod1) + [dsh2, dsc2, dg2] + list(dmod3), axis=1).reshape(bsz, N_MOD * D_MODEL)
        small = _small_grads(d_ln1g, d_ln1b)
        packing["names"] = list(small)
        packing["shapes"] = [small[k].shape for k in small]
        return [_pack(list(small.values())), dmod], [True, True]

    def _small_grads(d_ln1g, d_ln1b):
        return {
            "ln1_g": d_ln1g, "ln1_b": d_ln1b, "conv_w": d_convw, "conv_b": d_convb,
            "dt_bias": ddtb[:, 0, :HEADS_PER_GROUP].reshape(1, SSD_HEADS),
            "a_log": dalog[:, 0, :HEADS_PER_GROUP].reshape(1, SSD_HEADS),
            "d_ssd": ddcol[:, :HEADS_PER_GROUP, 0].reshape(1, SSD_HEADS),
            "ssd_norm_w": dnw.reshape(1, SSD_WIDTH),
            "s5_a_re": d_a_re[None], "s5_a_im": d_a_im[None], "s5_log_dt": d_ldt.reshape(1, S5_GROUPS),
            "s5_b_re": d_b_re.reshape(s5_b_re.shape), "s5_b_im": d_b_im.reshape(s5_b_im.shape),
            "s5_c_re": d_c_re[None], "s5_c_im": d_c_im[None], "s5_d": dd5.reshape(1, S5_WIDTH),
            "b_glu": d_bglu, "ln2_g": d_ln2g, "ln2_b": d_ln2b, "ln3_g": d_ln3g, "ln3_b": d_ln3b,
            "loss": loss_loc.reshape(1, 1),
        }

    dx0, _, (s_f1w1, d_f1w3, s_f1w2, _, _), (s_small, s_dmod) = _ffn_bwd(
        "ffn1", dx1, x0, sc1, sh1, g1, f1w1, f1w3, f1w2, ln1_g, ln1_b, res1, seq, tm, chain=small_and_dmod)
    names, shapes = packing["names"], packing["shapes"]
    s_f1w3, = exchange("sum_grads", [d_f1w3], [False])

    out = {"grad_x": dx0.reshape(x.shape)}

    def put(name, res, shape):
        for key, val in zip(("grad_", "delta_", "new_m_", "new_v_"), res):
            out[key + name] = val.reshape(shape)

    for name, slots, tr in (("ffn1_w1", s_f1w1, 176), ("ffn1_w3", s_f1w3, 176), ("ffn2_w1", s_f2w1, 176),
                            ("ffn2_w3", s_f2w3, 176), ("w_in", s_win, IN_COLS // N_DEV)):
        w = given[name]
        grad = sum_slots("sum_" + name, slots, tr).T
        put(name, adamw("adam_" + name, grad, w[0], given["m_" + name][0], given["v_" + name][0], 256), w.shape)
    for name, slots in (("ffn1_w2", s_f1w2), ("ffn2_w2", s_f2w2)):
        w = given[name]
        put(name, adamw("adam_" + name, slots, w[0], given["m_" + name][0], given["v_" + name][0], 176), w.shape)
    put("w_glu", adamw("adam_w_glu", s_glu, w_glu[0], m_w_glu[0], v_w_glu[0], 64), w_glu.shape)
    put("w_out", adamw("adam_w_out", s_out, w_out[0], m_w_out[0], v_w_out[0], 128), w_out.shape)

    dmod_all = s_dmod.reshape(N_DEV * bsz, N_MOD * D_MODEL)
    g_bada, g_wada = ada_bwd(c_all, dmod_all, lax.dynamic_slice(dmod_all, (0, me * n_loc), (N_DEV * bsz, n_loc)))
    put("w_ada", adamw("adam_w_ada", g_wada, w_ada[0], m_w_ada[0], v_w_ada[0], 256), w_ada.shape)
    put("b_ada", adamw("adam_b_ada", g_bada, b_ada, m_b_ada, v_b_ada, 1), b_ada.shape)

    not_params = {"conv_w": jnp.zeros((CONV_K, CONV_CH), F32), "loss": jnp.zeros((1, 1), F32)}
    pw, pm, pv = [_pack([not_params[k] if k in not_params else given[pre + k] for k in names]) for pre in ("", "m_", "v_")]
    res_small = adamw("adam_small", s_small, pw, pm, pv, pw.shape[0])
    parts = [_unpack(r, shapes) for r in res_small]
    for i, k in enumerate(names):
        if k not in not_params:
            put(k, [p[i] for p in parts], given[k].shape)
    out["loss"] = parts[0][names.index("loss")][0, 0]
    g_cw = lax.dynamic_slice(parts[0][names.index("conv_w")], (0, me * LANE), (CONV_K, LANE))
    put("conv_w", adamw("adam_conv_w", g_cw, conv_w[0], m_conv_w[0], v_conv_w[0], CONV_K), conv_w.shape)

    order = ["w_ada", "b_ada", "ffn1_w1", "ffn1_w3", "ffn1_w2", "ln1_g", "ln1_b", "w_in", "conv_w", "conv_b", "dt_bias",
             "a_log", "d_ssd", "ssd_norm_w", "s5_a_re", "s5_a_im", "s5_log_dt", "s5_b_re", "s5_b_im", "s5_c_re",
             "s5_c_im", "s5_d", "w_glu", "b_glu", "w_out", "ln2_g", "ln2_b", "ffn2_w1", "ffn2_w3", "ffn2_w2", "ln3_g",
             "ln3_b"]
    return (out["loss"], out["grad_x"], *[out[p + n] for p in ("grad_", "delta_", "new_m_", "new_v_") for n in order])
```

```python
import functools
import math

import jax
import jax.numpy as jnp
from jax import lax
from jax.experimental import pallas as pl
from jax.experimental.pallas import tpu as pltpu

F32 = jnp.float32
BF16 = jnp.bfloat16
MESH = pl.DeviceIdType.MESH

N_DEV = 8
D_MODEL = 1024
D_FF = 2816
N_MOD = 9
SSD_WIDTH = 512
SSD_HEADS = 8
SSD_HEAD_DIM = 64
SSD_GROUPS = 2
SSD_STATE = 128
SSD_CHUNK = 128
GROUP_COLS = SSD_WIDTH // SSD_GROUPS
HEADS_PER_GROUP = SSD_HEADS // SSD_GROUPS
CONV_K = 4
CONV_CH = 1024
S5_WIDTH = 512
S5_GROUPS = 32
S5_GROUP_CH = 16
S5_STATE = 64
S5_COLS = S5_GROUPS * S5_STATE
S5_Q = 4
S5_CHUNK = 2048
ALPHA = 2.0 ** 0.25
LN_EPS = 1e-5
LANE = 128
HALO = 8

P_XBC, P_Z, P_U, P_DT = 0, 1024, 1536, 2048
P_COLS = 2048 + SSD_GROUPS * LANE
IN_COLS = SSD_WIDTH + CONV_CH + SSD_HEADS + S5_WIDTH

ADAM_LR, ADAM_B1, ADAM_B2, ADAM_EPS, ADAM_WD, ADAM_STEP = 0.001, 0.9, 0.999, 1e-08, 0.01, 10

VMEM_LIMIT = 56 * 1024 * 1024


def _cp(*sem):
    return pltpu.CompilerParams(dimension_semantics=sem if sem else None, vmem_limit_bytes=VMEM_LIMIT)


def _dg(a, b, ca, cb):
    return lax.dot_general(a.astype(BF16), b.astype(BF16), (((ca,), (cb,)), ((), ())), preferred_element_type=F32)


@jax.custom_vjp
def bdot_nn(a, b):
    return _dg(a, b, 1, 0)


bdot_nn.defvjp(lambda a, b: (_dg(a, b, 1, 0), (a, b)),
               lambda r, g: (_dg(g, r[1], 1, 1), _dg(r[0], g, 0, 0)))


@jax.custom_vjp
def bdot_nt(a, b):
    return _dg(a, b, 1, 1)


bdot_nt.defvjp(lambda a, b: (_dg(a, b, 1, 1), (a, b)),
               lambda r, g: (_dg(g, r[1], 1, 0), _dg(g, r[0], 0, 0)))


@jax.custom_vjp
def bdot_tn(a, b):
    return _dg(a, b, 0, 0)


bdot_tn.defvjp(lambda a, b: (_dg(a, b, 0, 0), (a, b)),
               lambda r, g: (_dg(r[1], g, 1, 1), _dg(r[0], g, 1, 0)))


def _split3(x):
    def top(v):
        bits = lax.bitcast_convert_type(v, jnp.int32) & jnp.int32(-65536)
        return lax.bitcast_convert_type(bits, F32)

    hi = top(x)
    r1 = x - hi
    mid = top(r1)
    return hi.astype(BF16), mid.astype(BF16), (r1 - mid).astype(BF16)


def _dot3(a, b, ca, cb, split_a):
    dims = (((ca,), (cb,)), ((), ()))
    if split_a:
        c = b.astype(BF16)
        return sum(lax.dot_general(p, c, dims, preferred_element_type=F32) for p in _split3(a))
    c = a.astype(BF16)
    return sum(lax.dot_general(c, p, dims, preferred_element_type=F32) for p in _split3(b))


@jax.custom_vjp
def mask_dot_left(c, x):
    return _dot3(c, x, 1, 0, False)


mask_dot_left.defvjp(lambda c, x: (_dot3(c, x, 1, 0, False), c),
                     lambda c, g: (jnp.zeros_like(c), _dot3(c, g, 0, 0, False)))


@jax.custom_vjp
def mask_dot_right(x, c):
    return _dot3(x, c, 1, 0, True)


mask_dot_right.defvjp(lambda x, c: (_dot3(x, c, 1, 0, True), c),
                      lambda c, g: (_dot3(g, c, 1, 1, True), jnp.zeros_like(c)))


def _take_col(z):
    @jax.custom_vjp
    def take(x):
        return x[:, z:z + 1]

    def bwd(shape, g):
        hot = (lax.broadcasted_iota(jnp.int32, (1, shape[1]), 1) == z).astype(F32)
        return (g * hot,)

    take.defvjp(lambda x: (x[:, z:z + 1], x.shape), bwd)
    return take


def _take_row(z):
    @jax.custom_vjp
    def take(x):
        return x[z:z + 1, :]

    def bwd(shape, g):
        hot = (lax.broadcasted_iota(jnp.int32, (shape[0], 1), 0) == z).astype(F32)
        return (hot * g,)

    take.defvjp(lambda x: (x[z:z + 1, :], x.shape), bwd)
    return take


def _view(a):
    return a if isinstance(a, tuple) else (a, 0, a.shape[1])


def _col_spec(view, rows, width, index):
    _, off, _ = view
    assert off % width == 0
    return pl.BlockSpec((rows, width), lambda *g: (index(*g)[0], off // width + index(*g)[1]))


def _rw_in_specs(rows, bps, gps, tm, tps):
    specs = [_col_spec(_view(r), tm, _view(r)[2], lambda i: (i, 0)) for r in rows]
    specs += [pl.BlockSpec((1, 1, b.shape[2]), lambda i: (i // tps, 0, 0)) for b in bps]
    specs += [pl.BlockSpec(g.shape, lambda i, nd=g.ndim: (0,) * nd) for g in gps]
    return specs


def _rw_vals(refs, nr, nb, ng):
    vals = [r[...] for r in refs[:nr]]
    vals += [b[0] for b in refs[nr:nr + nb]]
    vals += [g[...] for g in refs[nr + nb:nr + nb + ng]]
    return vals


def rowwise_fwd(name, f, rows, bps, gps, outs, seq, tm):
    t = _view(rows[0])[0].shape[0]
    tps = seq // tm
    nr, nb, ng = len(rows), len(bps), len(gps)

    def body(*refs):
        res = f(*_rw_vals(refs, nr, nb, ng))
        for o, v in zip(refs[nr + nb + ng:], res):
            o[...] = v.astype(o.dtype)

    return pl.pallas_call(
        body, name=name, grid=(t // tm,),
        in_specs=_rw_in_specs(rows, bps, gps, tm, tps),
        out_specs=[pl.BlockSpec((tm, c), lambda i: (i, 0)) for c, _ in outs],
        out_shape=[jax.ShapeDtypeStruct((t, c), d) for c, d in outs],
        compiler_params=_cp("arbitrary"),
    )(*[_view(r)[0] for r in rows], *bps, *gps)


def rowwise_bwd(name, f, rows, bps, gps, douts, seq, tm, row_grads, add_rows=None):
    add_rows = add_rows or {}
    t = _view(rows[0])[0].shape[0]
    tps = seq // tm
    nr, nb, ng, nd = len(rows), len(bps), len(gps), len(douts)
    want = [k for k in range(nr) if row_grads[k] is not None]
    adds = sorted(add_rows)
    n_in = nr + nb + ng + nd + len(adds)

    def body(*refs):
        vals = _rw_vals(refs, nr, nb, ng)
        dvals = tuple(r[...] for r in refs[nr + nb + ng:nr + nb + ng + nd])
        add_refs = dict(zip(adds, refs[nr + nb + ng + nd:n_in]))
        out_refs = refs[n_in:]
        _, pull = jax.vjp(f, *vals)
        grads = pull(dvals)
        i = pl.program_id(0)
        for o, k in zip(out_refs, want):
            g = grads[k]
            if k in add_refs:
                g = g + add_refs[k][...]
            o[...] = g.astype(o.dtype)
        for j in range(nb):
            o = out_refs[len(want) + j]

            @pl.when(i % tps == 0)
            def _(o=o):
                o[...] = jnp.zeros_like(o)

            o[0] = o[0] + grads[nr + j]
        for j in range(ng):
            o = out_refs[len(want) + nb + j]

            @pl.when(i == 0)
            def _(o=o):
                o[...] = jnp.zeros_like(o)

            o[...] = o[...] + grads[nr + nb + j]

    in_specs = _rw_in_specs(rows, bps, gps, tm, tps)
    in_specs += [_col_spec(_view(d), tm, _view(d)[2], lambda i: (i, 0)) for d in douts]
    in_specs += [pl.BlockSpec((tm, add_rows[k].shape[1]), lambda i: (i, 0)) for k in adds]
    out_specs = [pl.BlockSpec((tm, _view(rows[k])[2]), lambda i: (i, 0)) for k in want]
    out_shape = [jax.ShapeDtypeStruct((t, _view(rows[k])[2]), row_grads[k]) for k in want]
    out_specs += [pl.BlockSpec((1, 1, b.shape[2]), lambda i: (i // tps, 0, 0)) for b in bps]
    out_shape += [jax.ShapeDtypeStruct(b.shape, F32) for b in bps]
    out_specs += [pl.BlockSpec(g.shape, lambda i, n=g.ndim: (0,) * n) for g in gps]
    out_shape += [jax.ShapeDtypeStruct(g.shape, F32) for g in gps]
    res = pl.pallas_call(
        body, name=name, grid=(t // tm,), in_specs=in_specs, out_specs=out_specs, out_shape=out_shape,
        compiler_params=_cp("arbitrary"),
    )(*[_view(r)[0] for r in rows], *bps, *gps, *[_view(d)[0] for d in douts], *[add_rows[k] for k in adds])
    nw = len(want)
    return res[:nw], res[nw:nw + nb], res[nw + nb:]


def mm_nn(name, xs, ws, tm, tn, out_dtype=F32, ride=None):
    views = [_view(x) for x in xs]
    t, n, k = views[0][0].shape[0], ws[0].shape[1], len(xs)

    def body(*refs):
        acc = _dg(refs[0][...], refs[k][...], 1, 0)
        for i in range(1, k):
            acc = acc + _dg(refs[i][...], refs[k + i][...], 1, 0)
        refs[2 * k][...] = acc.astype(out_dtype)

    in_specs = [_col_spec(v, tm, v[2], lambda i, j: (i, 0)) for v in views]
    in_specs += [pl.BlockSpec((w.shape[0], tn), lambda i, j: (0, j)) for w in ws]
    out_spec = pl.BlockSpec((tm, tn), lambda i, j: (i, j))
    out_shape = jax.ShapeDtypeStruct((t, n), out_dtype)
    if ride is not None:
        (res,), landed = hosted_call(body, name=name, grid=(t // tm, n // tn), in_specs=in_specs, out_specs=[out_spec],
                                     out_shape=[out_shape], args=(*[v[0] for v in views], *ws), ride=ride)
        return res, landed
    return pl.pallas_call(
        body, name=name, grid=(t // tm, n // tn), in_specs=in_specs, out_specs=out_spec, out_shape=out_shape,
        compiler_params=_cp("parallel", "parallel"),
    )(*[v[0] for v in views], *ws)


def mm_nt(name, dys, ws, tm, tk, out_dtype=F32, ride=None):
    views = [_view(d) for d in dys]
    t, kk, k = views[0][0].shape[0], ws[0].shape[0], len(dys)

    def body(*refs):
        acc = _dg(refs[0][...], refs[k][...], 1, 1)
        for i in range(1, k):
            acc = acc + _dg(refs[i][...], refs[k + i][...], 1, 1)
        refs[2 * k][...] = acc.astype(out_dtype)

    in_specs = [_col_spec(v, tm, v[2], lambda i, j: (i, 0)) for v in views]
    in_specs += [pl.BlockSpec((tk, w.shape[1]), lambda i, j: (j, 0)) for w in ws]
    out_spec = pl.BlockSpec((tm, tk), lambda i, j: (i, j))
    out_shape = jax.ShapeDtypeStruct((t, kk), out_dtype)
    if ride is not None:
        (res,), landed = hosted_call(body, name=name, grid=(t // tm, kk // tk), in_specs=in_specs, out_specs=[out_spec],
                                     out_shape=[out_shape], args=(*[v[0] for v in views], *ws), ride=ride)
        return res, landed
    return pl.pallas_call(
        body, name=name, grid=(t // tm, kk // tk), in_specs=in_specs, out_specs=out_spec, out_shape=out_shape,
        compiler_params=_cp("parallel", "parallel"),
    )(*[v[0] for v in views], *ws)


def mm_tn(name, x, dy, tk, tn, tt, out_dtype=F32, ride=None):
    xv, dv = _view(x), _view(dy)
    t, kk, n = xv[0].shape[0], xv[2], dv[2]
    steps = t // tt

    def body(x_ref, d_ref, o_ref, acc_ref):
        @pl.when(pl.program_id(2) == 0)
        def _():
            acc_ref[...] = jnp.zeros_like(acc_ref)

        acc_ref[...] += _dg(x_ref[...], d_ref[...], 0, 0)

        @pl.when(pl.program_id(2) == steps - 1)
        def _():
            o_ref[...] = acc_ref[...].astype(out_dtype)

    in_specs = [_col_spec(xv, tt, tk, lambda a, b, c: (c, a)), _col_spec(dv, tt, tn, lambda a, b, c: (c, b))]
    out_spec = pl.BlockSpec((tk, tn), lambda a, b, c: (a, b))
    out_shape = jax.ShapeDtypeStruct((kk, n), out_dtype)
    if ride is not None:
        (res,), landed = hosted_call(body, name=name, grid=(kk // tk, n // tn, steps), in_specs=in_specs,
                                     out_specs=[out_spec], out_shape=[out_shape], scratch=[pltpu.VMEM((tk, tn), F32)],
                                     args=(xv[0], dv[0]), ride=ride)
        return res, landed
    return pl.pallas_call(
        body, name=name, grid=(kk // tk, n // tn, steps), in_specs=in_specs, out_specs=out_spec, out_shape=out_shape,
        scratch_shapes=[pltpu.VMEM((tk, tn), F32)],
        compiler_params=_cp("parallel", "parallel", "arbitrary"),
    )(xv[0], dv[0])


def _silu(x):
    return x * jax.nn.sigmoid(x)


def f_modulate(x, sc, sh):
    return (x * (1.0 + sc) + sh,)


def _res_ln(coef):
    def f(x, y, g, lg, lb):
        r = ALPHA * x + (coef * g) * y
        mu = jnp.mean(r, axis=-1, keepdims=True)
        d = r - mu
        var = jnp.mean(d * d, axis=-1, keepdims=True)
        return (d * lax.rsqrt(var + LN_EPS) * lg + lb,)
    return f


def f_glu(y, w, b):
    g = jax.nn.gelu(y)
    return (g * jax.nn.sigmoid(bdot_nn(g, w) + b),)


def _shift_down(x, halo, k):
    if k == 0:
        return x
    r = pltpu.roll(x, k, 0)
    hr = pltpu.roll(halo, k, 0)
    row = lax.broadcasted_iota(jnp.int32, (HALO, 1), 0)
    top = jnp.where(row < k, hr, r[:HALO])
    return jnp.concatenate([top, r[HALO:]], axis=0)


def _shift_up(x, halo, k):
    if k == 0:
        return x
    n = x.shape[0]
    r = pltpu.roll(x, n - k, 0)
    hr = pltpu.roll(halo, HALO - k, 0)
    row = lax.broadcasted_iota(jnp.int32, (HALO, 1), 0)
    bot = jnp.where(row >= HALO - k, hr, r[n - HALO:])
    return jnp.concatenate([r[:n - HALO], bot], axis=0)


def _conv_pre(x, halo, w, b):
    acc = x * w[CONV_K - 1:CONV_K, :] + b
    for k in range(1, CONV_K):
        acc = acc + _shift_down(x, halo, k) * w[CONV_K - 1 - k:CONV_K - k, :]
    return acc


def _rows_before(width, tm):
    return pl.BlockSpec((HALO, width), lambda i: (jnp.maximum(i * (tm // HALO) - 1, 0), 0))


def conv_fwd(proj, w, b, seq, tm):
    t = proj.shape[0]
    tps = seq // tm

    def body(x_ref, h_ref, w_ref, b_ref, o_ref):
        first = (pl.program_id(0) % tps == 0)
        halo = jnp.where(first, 0.0, h_ref[...])
        o_ref[...] = _silu(_conv_pre(x_ref[...], halo, w_ref[...], b_ref[...]))

    return pl.pallas_call(
        body, name="conv_fwd", grid=(t // tm,),
        in_specs=[pl.BlockSpec((tm, CONV_CH), lambda i: (i, 0)), _rows_before(CONV_CH, tm),
                  pl.BlockSpec((CONV_K, CONV_CH), lambda i: (0, 0)), pl.BlockSpec((1, CONV_CH), lambda i: (0, 0))],
        out_specs=pl.BlockSpec((tm, CONV_CH), lambda i: (i, 0)),
        out_shape=jax.ShapeDtypeStruct((t, CONV_CH), F32),
        compiler_params=_cp("arbitrary"),
    )(proj, proj, w, b)


def conv_bwd_pre(proj, w, b, dxs, dbm, dcm, seq, tm):
    t = proj.shape[0]
    tps = seq // tm

    def body(x_ref, h_ref, w_ref, b_ref, d1, d2, d3, dp_ref, dw_ref, db_ref):
        i = pl.program_id(0)
        halo = jnp.where(i % tps == 0, 0.0, h_ref[...])
        x = x_ref[...]
        pre = _conv_pre(x, halo, w_ref[...], b_ref[...])
        sg = jax.nn.sigmoid(pre)
        dout = jnp.concatenate([d1[...], d2[...], d3[...]], axis=1)
        dp = dout * (sg * (1.0 + pre * (1.0 - sg)))
        dp_ref[...] = dp

        @pl.when(i == 0)
        def _():
            dw_ref[...] = jnp.zeros_like(dw_ref)
            db_ref[...] = jnp.zeros_like(db_ref)

        db_ref[...] += jnp.sum(dp, axis=0, keepdims=True)
        for k in range(CONV_K):
            j = CONV_K - 1 - k
            dw_ref[j:j + 1, :] += jnp.sum(dp * _shift_down(x, halo, k), axis=0, keepdims=True)

    return pl.pallas_call(
        body, name="conv_bwd_pre", grid=(t // tm,),
        in_specs=[pl.BlockSpec((tm, CONV_CH), lambda i: (i, 0)), _rows_before(CONV_CH, tm),
                  pl.BlockSpec((CONV_K, CONV_CH), lambda i: (0, 0)), pl.BlockSpec((1, CONV_CH), lambda i: (0, 0)),
                  pl.BlockSpec((tm, 512), lambda i: (i, 0)), pl.BlockSpec((tm, 256), lambda i: (i, 0)),
                  pl.BlockSpec((tm, 256), lambda i: (i, 0))],
        out_specs=[pl.BlockSpec((tm, CONV_CH), lambda i: (i, 0)), pl.BlockSpec((CONV_K, CONV_CH), lambda i: (0, 0)),
                   pl.BlockSpec((1, CONV_CH), lambda i: (0, 0))],
        out_shape=[jax.ShapeDtypeStruct((t, CONV_CH), F32), jax.ShapeDtypeStruct((CONV_K, CONV_CH), F32),
                   jax.ShapeDtypeStruct((1, CONV_CH), F32)],
        compiler_params=_cp("arbitrary"),
    )(proj, proj, w, b, dxs, dbm, dcm)


def conv_bwd_x(dpre, w, seq, tm):
    t = dpre.shape[0]
    tps = seq // tm
    blocks = tm // HALO
    last = t // HALO - 1

    def body(d_ref, h_ref, w_ref, o_ref):
        halo = jnp.where(pl.program_id(0) % tps == tps - 1, 0.0, h_ref[...])
        d = d_ref[...]
        w = w_ref[...]
        acc = d * w[CONV_K - 1:CONV_K, :]
        for k in range(1, CONV_K):
            acc = acc + _shift_up(d, halo, k) * w[CONV_K - 1 - k:CONV_K - k, :]
        o_ref[...] = acc

    return pl.pallas_call(
        body, name="conv_bwd_x", grid=(t // tm,),
        in_specs=[pl.BlockSpec((tm, CONV_CH), lambda i: (i, 0)),
                  pl.BlockSpec((HALO, CONV_CH), lambda i: (jnp.minimum((i + 1) * blocks, last), 0)),
                  pl.BlockSpec((CONV_K, CONV_CH), lambda i: (0, 0))],
        out_specs=pl.BlockSpec((tm, CONV_CH), lambda i: (i, 0)),
        out_shape=jax.ShapeDtypeStruct((t, CONV_CH), F32),
        compiler_params=_cp("arbitrary"),
    )(dpre, dpre, w)


def _softplus(x):
    return jnp.maximum(x, 0.0) + jnp.log1p(jnp.exp(-jnp.abs(x)))


def _ssd_chunk(xs, bg, cg, dtr, zz, hp, dtb, alog, dcol, nw):
    l = xs.shape[0]
    row = lax.broadcasted_iota(jnp.int32, (l, l), 0)
    col = lax.broadcasted_iota(jnp.int32, (l, l), 1)
    causal = row >= col
    tril = causal.astype(F32)
    expand = (lax.broadcasted_iota(jnp.int32, (LANE, GROUP_COLS), 1) // SSD_HEAD_DIM
              == lax.broadcasted_iota(jnp.int32, (LANE, GROUP_COLS), 0)).astype(F32)
    head_of_col = lax.broadcasted_iota(jnp.int32, (1, GROUP_COLS), 1) // SSD_HEAD_DIM
    last_row = (lax.broadcasted_iota(jnp.int32, (l, 1), 0) == l - 1).astype(F32)

    dtc = _softplus(dtr + dtb)
    a_c = dtc * (-jnp.exp(alog))
    acs_c = mask_dot_left(tril, a_c)
    dt_e = mask_dot_right(dtc, expand)
    acs_e = mask_dot_right(acs_c, expand)
    alast_e = jnp.sum(acs_e * last_row, axis=0, keepdims=True)
    x = xs * dt_e
    states = bdot_tn(bg, x * jnp.exp(alast_e - acs_e))
    h_next = jnp.exp(alast_e) * hp + states
    d_e = jnp.sum(dcol * expand, axis=0, keepdims=True)
    y = bdot_nn(cg, hp) * jnp.exp(acs_e) + d_e * xs
    cb = bdot_nt(cg, bg)
    acs_t = acs_c.T
    for z in range(HEADS_PER_GROUP):
        seg = _take_col(z)(acs_c) - _take_row(z)(acs_t)
        lmat = jnp.exp(jnp.where(causal, seg, -1e30))
        y = y + bdot_nn(cb * lmat, x * (head_of_col == z).astype(F32))
    yz = y * _silu(zz)
    ms = jnp.mean(yz * yz, axis=-1, keepdims=True)
    return yz * lax.rsqrt(ms + LN_EPS) * nw, h_next


SSD_SUB = 4
SSD_ROWS = SSD_SUB * SSD_CHUNK


def _ssd_in_specs(steps, rev):
    def tok(b, c):
        return b * steps + (steps - 1 - c if rev else c)

    whole = lambda *shape: pl.BlockSpec(shape, lambda b, c: (0,) * len(shape))
    both = SSD_GROUPS * SSD_STATE
    return [
        pl.BlockSpec((SSD_ROWS, SSD_WIDTH), lambda b, c: (tok(b, c), 0)),
        pl.BlockSpec((SSD_ROWS, both), lambda b, c: (tok(b, c), SSD_WIDTH // both)),
        pl.BlockSpec((SSD_ROWS, both), lambda b, c: (tok(b, c), SSD_WIDTH // both + 1)),
        pl.BlockSpec((SSD_ROWS, SSD_GROUPS * LANE), lambda b, c: (tok(b, c), P_DT // (SSD_GROUPS * LANE))),
        pl.BlockSpec((SSD_ROWS, SSD_WIDTH), lambda b, c: (tok(b, c), P_Z // SSD_WIDTH)),
        whole(SSD_GROUPS, 1, LANE), whole(SSD_GROUPS, 1, LANE), whole(SSD_GROUPS, LANE, 1),
        whole(SSD_GROUPS, 1, GROUP_COLS),
    ], tok


def _piece(ref, s, g, width):
    return ref[s * SSD_CHUNK:(s + 1) * SSD_CHUNK, g * width:(g + 1) * width]


def ssd_fwd(xc, proj, dtb, alog, dcol, nw, bsz, seq, ride=None):
    t = xc.shape[0]
    nc = seq // SSD_CHUNK
    steps = nc // SSD_SUB
    in_specs, tok = _ssd_in_specs(steps, False)

    def body(xs, bm, cm, dtr, zz, dtb_r, alog_r, dcol_r, nw_r, y_ref, hp_ref, h_scr):
        @pl.when(pl.program_id(1) == 0)
        def _():
            h_scr[...] = jnp.zeros_like(h_scr)

        for g in range(SSD_GROUPS):
            h = h_scr[g]
            for s in range(SSD_SUB):
                hp_ref[g, 0, s] = h
                y, h = _ssd_chunk(_piece(xs, s, g, GROUP_COLS), _piece(bm, s, g, SSD_STATE),
                                  _piece(cm, s, g, SSD_STATE), _piece(dtr, s, g, LANE), _piece(zz, s, g, GROUP_COLS), h,
                                  dtb_r[g], alog_r[g], dcol_r[g], nw_r[g])
                y_ref[s * SSD_CHUNK:(s + 1) * SSD_CHUNK, g * GROUP_COLS:(g + 1) * GROUP_COLS] = y
            h_scr[g] = h

    return hosted_call(
        body, name="ssd_fwd", grid=(bsz, steps), in_specs=in_specs,
        out_specs=[pl.BlockSpec((SSD_ROWS, SSD_WIDTH), lambda b, c: (tok(b, c), 0)),
                   pl.BlockSpec((SSD_GROUPS, 1, SSD_SUB, SSD_STATE, GROUP_COLS), lambda b, c: (0, b, c, 0, 0))],
        out_shape=[jax.ShapeDtypeStruct((t, SSD_WIDTH), F32),
                   jax.ShapeDtypeStruct((SSD_GROUPS, bsz, nc, SSD_STATE, GROUP_COLS), F32)],
        scratch=[pltpu.VMEM((SSD_GROUPS, SSD_STATE, GROUP_COLS), F32)],
        args=(xc, xc, xc, proj, proj, dtb, alog, dcol, nw), ride=ride)


def ssd_bwd(xc, proj, dtb, alog, dcol, nw, hprev, dy, bsz, seq):
    t = xc.shape[0]
    nc = seq // SSD_CHUNK
    steps = nc // SSD_SUB
    in_specs, tok = _ssd_in_specs(steps, True)
    in_specs += [pl.BlockSpec((SSD_GROUPS, 1, SSD_SUB, SSD_STATE, GROUP_COLS), lambda b, c: (0, b, steps - 1 - c, 0, 0)),
                 pl.BlockSpec((SSD_ROWS, SSD_WIDTH), lambda b, c: (tok(b, c), 0))]

    def body(xs, bm, cm, dtr, zz, dtb_r, alog_r, dcol_r, nw_r, hp_ref, dy_ref,
             dxs, dbm, dcm, ddt, dzz, ddtb, dalog, ddcol, dnw, dh_scr):
        b, c = pl.program_id(0), pl.program_id(1)

        @pl.when(c == 0)
        def _():
            dh_scr[...] = jnp.zeros_like(dh_scr)

        @pl.when((b == 0) & (c == 0))
        def _():
            for r in (ddtb, dalog, ddcol, dnw):
                r[...] = jnp.zeros_like(r)

        for g in range(SSD_GROUPS):
            wide = slice(g * GROUP_COLS, (g + 1) * GROUP_COLS)
            state = slice(g * SSD_STATE, (g + 1) * SSD_STATE)
            dh = dh_scr[g]
            for s in reversed(range(SSD_SUB)):
                rows = slice(s * SSD_CHUNK, (s + 1) * SSD_CHUNK)
                _, pull = jax.vjp(_ssd_chunk, xs[rows, wide], bm[rows, state], cm[rows, state],
                                  _piece(dtr, s, g, LANE), zz[rows, wide], hp_ref[g, 0, s],
                                  dtb_r[g], alog_r[g], dcol_r[g], nw_r[g])
                d = pull((dy_ref[rows, wide], dh))
                dxs[rows, wide], dbm[rows, state], dcm[rows, state], dzz[rows, wide] = d[0], d[1], d[2], d[4]
                ddt[rows, g * LANE:(g + 1) * LANE] = d[3]
                dh = d[5]
                ddtb[g] += d[6]
                dalog[g] += d[7]
                ddcol[g] += d[8]
                dnw[g] += d[9]
            dh_scr[g] = dh

    def tile(w):
        return pl.BlockSpec((SSD_ROWS, w), lambda b, c: (tok(b, c), 0))

    whole = lambda *shape: pl.BlockSpec(shape, lambda b, c: (0,) * len(shape))
    return pl.pallas_call(
        body, name="ssd_bwd", grid=(bsz, steps), in_specs=in_specs,
        out_specs=[tile(SSD_WIDTH), tile(2 * SSD_STATE), tile(2 * SSD_STATE), tile(2 * LANE), tile(SSD_WIDTH),
                   whole(SSD_GROUPS, 1, LANE), whole(SSD_GROUPS, 1, LANE), whole(SSD_GROUPS, LANE, 1),
                   whole(SSD_GROUPS, 1, GROUP_COLS)],
        out_shape=[jax.ShapeDtypeStruct((t, SSD_WIDTH), F32), jax.ShapeDtypeStruct((t, 2 * SSD_STATE), F32),
                   jax.ShapeDtypeStruct((t, 2 * SSD_STATE), F32), jax.ShapeDtypeStruct((t, 2 * LANE), F32),
                   jax.ShapeDtypeStruct((t, SSD_WIDTH), F32),
                   jax.ShapeDtypeStruct((SSD_GROUPS, 1, LANE), F32), jax.ShapeDtypeStruct((SSD_GROUPS, 1, LANE), F32),
                   jax.ShapeDtypeStruct((SSD_GROUPS, LANE, 1), F32),
                   jax.ShapeDtypeStruct((SSD_GROUPS, 1, GROUP_COLS), F32)],
        scratch_shapes=[pltpu.VMEM((SSD_GROUPS, SSD_STATE, GROUP_COLS), F32)],
        compiler_params=_cp("arbitrary", "arbitrary"),
    )(xc, xc, xc, proj, proj, dtb, alog, dcol, nw, hprev, dy)


def _disc_a(a_re, a_im, log_dt):
    dt = jnp.exp(log_dt)
    mag = jnp.exp(dt * a_re)
    ab_re, ab_im = mag * jnp.cos(dt * a_im), mag * jnp.sin(dt * a_im)
    den = a_re * a_re + a_im * a_im
    nr, ni = ab_re - 1.0, ab_im
    f_re, f_im = (nr * a_re + ni * a_im) / den, (ni * a_re - nr * a_im) / den
    return ab_re, ab_im, f_re, f_im


def _disc_b(f_re, f_im, b_re, b_im):
    return f_re * b_re - f_im * b_im, f_re * b_im + f_im * b_re


def _whole(f, name, args, outs):
    def body(*refs):
        res = f(*[r[...] for r in refs[:len(args)]])
        for o, v in zip(refs[len(args):], res):
            o[...] = v

    return pl.pallas_call(body, name=name, out_shape=[jax.ShapeDtypeStruct(s, F32) for s in outs])(*args)


def _whole_vjp(f, name, args, cts):
    def body(*refs):
        vals = [r[...] for r in refs[:len(args)]]
        _, pull = jax.vjp(f, *vals)
        res = pull(tuple(r[...] for r in refs[len(args):len(args) + len(cts)]))
        for o, v in zip(refs[len(args) + len(cts):], res):
            o[...] = v

    return pl.pallas_call(body, name=name, out_shape=[jax.ShapeDtypeStruct(a.shape, F32) for a in args])(*args, *cts)


S5_SUB = 8
S5_STEPS = 3


def s5_tables(lam_re, lam_im):
    rows = S5_STEPS * S5_SUB

    def body(lr_ref, li_ref, sf_re, sf_im, sb_re, sb_im, cf_re, cf_im, cb_re, cb_im):
        lr, li = lr_ref[...], li_ref[...]

        def power(k):
            m = jnp.exp(k * lr)
            return m * jnp.cos(k * li), m * jnp.sin(k * li)

        srow = lax.broadcasted_iota(jnp.int32, (rows, 1), 0)
        k = jnp.left_shift(1, srow // S5_SUB)
        tt = srow % S5_SUB
        pr, pi = power(k.astype(F32))
        fwd, bwd = tt >= k, tt < S5_SUB - k
        sf_re[...], sf_im[...] = jnp.where(fwd, pr, 0.0), jnp.where(fwd, pi, 0.0)
        sb_re[...], sb_im[...] = jnp.where(bwd, pr, 0.0), jnp.where(bwd, pi, 0.0)
        trow = lax.broadcasted_iota(jnp.int32, (S5_SUB, 1), 0)
        cf_re[...], cf_im[...] = power((trow + 1).astype(F32))
        cb_re[...], cb_im[...] = power((S5_SUB - trow).astype(F32))

    shp = [jax.ShapeDtypeStruct((rows, S5_COLS), F32)] * 4 + [jax.ShapeDtypeStruct((S5_SUB, S5_COLS), F32)] * 4
    return pl.pallas_call(body, name="s5_tables", out_shape=shp)(lam_re, lam_im)


def _s5_coefs(steps_re, steps_im, carry_re, carry_im, reverse):
    sign = -1.0 if reverse else 1.0
    steps = [(steps_re[s * S5_SUB:(s + 1) * S5_SUB, :], sign * steps_im[s * S5_SUB:(s + 1) * S5_SUB, :])
             for s in range(S5_STEPS)]
    return steps, (carry_re[...], sign * carry_im[...])


def _s5_block_scan(ar, ai, coefs, cr, ci, reverse):
    steps, (qr, qi) = coefs
    for s, (pr, pi) in enumerate(steps):
        shift = S5_SUB - (1 << s) if reverse else (1 << s)
        sr, si = pltpu.roll(ar, shift, 0), pltpu.roll(ai, shift, 0)
        ar, ai = ar + pr * sr - pi * si, ai + pr * si + pi * sr
    br, bi = jnp.broadcast_to(cr, ar.shape), jnp.broadcast_to(ci, ai.shape)
    return ar + qr * br - qi * bi, ai + qr * bi + qi * br


def _s5_specs(n5, rev):
    def tok(q, b, c):
        return b * n5 + (n5 - 1 - c if rev else c)

    qcols = S5_COLS // S5_Q
    specs = [
        pl.BlockSpec((S5_CHUNK, LANE), lambda q, b, c: (tok(q, b, c), P_U // LANE + q)),
        pl.BlockSpec((1, LANE, qcols), lambda q, b, c: (q, 0, 0)),
        pl.BlockSpec((1, LANE, qcols), lambda q, b, c: (q, 0, 0)),
        pl.BlockSpec((1, qcols, LANE), lambda q, b, c: (q, 0, 0)),
        pl.BlockSpec((1, qcols, LANE), lambda q, b, c: (q, 0, 0)),
        pl.BlockSpec((S5_STEPS * S5_SUB, qcols), lambda q, b, c: (0, q)),
        pl.BlockSpec((S5_STEPS * S5_SUB, qcols), lambda q, b, c: (0, q)),
        pl.BlockSpec((S5_SUB, qcols), lambda q, b, c: (0, q)),
        pl.BlockSpec((S5_SUB, qcols), lambda q, b, c: (0, q)),
        pl.BlockSpec((1, 1, LANE), lambda q, b, c: (q, 0, 0)),
    ]
    return specs, tok, qcols


def s5_fwd(proj, wb_re, wb_im, wc_re, wc_im, sf_re, sf_im, cf_re, cf_im, dvec, bsz, seq, ride=None):
    t = proj.shape[0]
    n5 = seq // S5_CHUNK
    in_specs, tok, qcols = _s5_specs(n5, False)

    def body(u_ref, wbr, wbi, wcr, wci, sfr, sfi, cfr, cfi, d_ref, y_ref, xr_ref, xi_ref, cr_scr, ci_scr):
        @pl.when(pl.program_id(2) == 0)
        def _():
            cr_scr[...] = jnp.zeros_like(cr_scr)
            ci_scr[...] = jnp.zeros_like(ci_scr)

        u = u_ref[...]
        bur, bui = _dg(u, wbr[0], 1, 0), _dg(u, wbi[0], 1, 0)
        coefs = _s5_coefs(sfr, sfi, cfr, cfi, False)
        cr, ci = cr_scr[...], ci_scr[...]
        for r in range(S5_CHUNK // S5_SUB):
            rows = slice(r * S5_SUB, (r + 1) * S5_SUB)
            xr, xi = _s5_block_scan(bur[rows], bui[rows], coefs, cr, ci, False)
            xr_ref[rows, :], xi_ref[rows, :] = xr, xi
            cr, ci = xr[S5_SUB - 1:, :], xi[S5_SUB - 1:, :]
        cr_scr[...], ci_scr[...] = cr, ci
        y_ref[...] = _dg(xr_ref[...], wcr[0], 1, 0) - _dg(xi_ref[...], wci[0], 1, 0) + u * d_ref[0]

    def tile(w):
        return pl.BlockSpec((S5_CHUNK, w), lambda q, b, c: (tok(q, b, c), q))

    return hosted_call(
        body, name="s5_fwd", grid=(S5_Q, bsz, n5), in_specs=in_specs,
        out_specs=[tile(LANE), tile(qcols), tile(qcols)],
        out_shape=[jax.ShapeDtypeStruct((t, S5_WIDTH), F32), jax.ShapeDtypeStruct((t, S5_COLS), F32),
                   jax.ShapeDtypeStruct((t, S5_COLS), F32)],
        scratch=[pltpu.VMEM((1, qcols), F32)] * 2,
        args=(proj, wb_re, wb_im, wc_re, wc_im, sf_re, sf_im, cf_re, cf_im, dvec), ride=ride)


def s5_bwd(proj, wb_re, wb_im, wc_re, wc_im, sb_re, sb_im, cb_re, cb_im, dvec, xr_all, xi_all, dy, bsz, seq,
           ride=None):
    t = proj.shape[0]
    n5 = seq // S5_CHUNK
    in_specs, tok, qcols = _s5_specs(n5, True)
    blocks = S5_CHUNK // HALO

    def prev_rows(q, b, c):
        return (jnp.maximum(tok(q, b, c) * blocks - 1, 0), q)

    in_specs += [pl.BlockSpec((S5_CHUNK, qcols), lambda q, b, c: (tok(q, b, c), q)),
                 pl.BlockSpec((S5_CHUNK, qcols), lambda q, b, c: (tok(q, b, c), q)),
                 pl.BlockSpec((HALO, qcols), prev_rows), pl.BlockSpec((HALO, qcols), prev_rows),
                 pl.BlockSpec((S5_CHUNK, LANE), lambda q, b, c: (tok(q, b, c), q))]

    def body(u_ref, wbr, wbi, wcr, wci, sbr, sbi, cbr, cbi, d_ref, xr_ref, xi_ref, pr_ref, pi_ref, dy_ref,
             du_ref, dwbr, dwbi, dwcr, dwci, dar, dai, dd_ref, gr_scr, gi_scr, gr_all, gi_all):
        b, c = pl.program_id(1), pl.program_id(2)

        @pl.when(c == 0)
        def _():
            gr_scr[...] = jnp.zeros_like(gr_scr)
            gi_scr[...] = jnp.zeros_like(gi_scr)

        @pl.when((b == 0) & (c == 0))
        def _():
            for r in (dwbr, dwbi, dwcr, dwci, dar, dai, dd_ref):
                r[...] = jnp.zeros_like(r)

        u, dy_v = u_ref[...], dy_ref[...]
        g0r, g0i = _dg(dy_v, wcr[0], 1, 1), -_dg(dy_v, wci[0], 1, 1)
        coefs = _s5_coefs(sbr, sbi, cbr, cbi, True)
        cr, ci = gr_scr[...], gi_scr[...]
        for r in reversed(range(S5_CHUNK // S5_SUB)):
            rows = slice(r * S5_SUB, (r + 1) * S5_SUB)
            br, bi = _s5_block_scan(g0r[rows], g0i[rows], coefs, cr, ci, True)
            gr_all[rows, :], gi_all[rows, :] = br, bi
            cr, ci = br[:1, :], bi[:1, :]
        gr_scr[...], gi_scr[...] = cr, ci
        gr, gi = gr_all[...], gi_all[...]

        row = lax.broadcasted_iota(jnp.int32, (S5_CHUNK, 1), 0)
        xr, xi = xr_ref[...], xi_ref[...]
        is_first = (c == n5 - 1)
        hr = jnp.where(is_first, 0.0, pr_ref[...][HALO - 1:, :])
        hi = jnp.where(is_first, 0.0, pi_ref[...][HALO - 1:, :])
        xpr = jnp.where(row >= 1, pltpu.roll(xr, 1, 0), hr)
        xpi = jnp.where(row >= 1, pltpu.roll(xi, 1, 0), hi)
        dar[0] += jnp.sum(xpr * gr + xpi * gi, axis=0, keepdims=True)
        dai[0] += jnp.sum(xpr * gi - xpi * gr, axis=0, keepdims=True)
        du_ref[...] = _dg(gr, wbr[0], 1, 1) + _dg(gi, wbi[0], 1, 1) + dy_v * d_ref[0]
        dwbr[0] += _dg(u, gr, 0, 0)
        dwbi[0] += _dg(u, gi, 0, 0)
        dwcr[0] += _dg(xr, dy_v, 0, 0)
        dwci[0] -= _dg(xi, dy_v, 0, 0)
        dd_ref[0] += jnp.sum(dy_v * u, axis=0, keepdims=True)

    def acc(shape):
        return pl.BlockSpec((1,) + shape, lambda q, b, c: (q, 0, 0))

    return hosted_call(
        body, name="s5_bwd", grid=(S5_Q, bsz, n5), in_specs=in_specs,
        out_specs=[pl.BlockSpec((S5_CHUNK, LANE), lambda q, b, c: (tok(q, b, c), q)),
                   acc((LANE, qcols)), acc((LANE, qcols)), acc((qcols, LANE)), acc((qcols, LANE)),
                   acc((1, qcols)), acc((1, qcols)), acc((1, LANE))],
        out_shape=[jax.ShapeDtypeStruct((t, S5_WIDTH), F32),
                   jax.ShapeDtypeStruct((S5_Q, LANE, qcols), F32), jax.ShapeDtypeStruct((S5_Q, LANE, qcols), F32),
                   jax.ShapeDtypeStruct((S5_Q, qcols, LANE), F32), jax.ShapeDtypeStruct((S5_Q, qcols, LANE), F32),
                   jax.ShapeDtypeStruct((S5_Q, 1, qcols), F32), jax.ShapeDtypeStruct((S5_Q, 1, qcols), F32),
                   jax.ShapeDtypeStruct((S5_Q, 1, LANE), F32)],
        scratch=[pltpu.VMEM((1, qcols), F32)] * 2 + [pltpu.VMEM((S5_CHUNK, qcols), F32)] * 2,
        args=(proj, wb_re, wb_im, wc_re, wc_im, sb_re, sb_im, cb_re, cb_im, dvec, xr_all, xi_all, xr_all, xi_all, dy),
        ride=ride)


def _blockdiag_b(bb):
    b4 = bb.reshape(S5_Q, 8, S5_STATE, S5_GROUP_CH)
    eye = jnp.eye(8, dtype=bb.dtype)
    w = jnp.einsum("qgph,gk->qghkp", b4, eye)
    return w.reshape(S5_Q, LANE, S5_COLS // S5_Q)


def _unblock_b(dw):
    d = dw.reshape(S5_Q, 8, S5_GROUP_CH, 8, S5_STATE)
    d = jnp.einsum("qghgp->qgph", d)
    return d.reshape(S5_COLS, S5_GROUP_CH)


def _blockdiag_c(cc):
    c4 = cc.reshape(S5_Q, 8, S5_GROUP_CH, S5_STATE)
    eye = jnp.eye(8, dtype=cc.dtype)
    w = jnp.einsum("qghp,gk->qgpkh", c4, eye)
    return w.reshape(S5_Q, S5_COLS // S5_Q, LANE)


def _unblock_c(dw):
    d = dw.reshape(S5_Q, 8, S5_STATE, 8, S5_GROUP_CH)
    d = jnp.einsum("qgpgh->qghp", d)
    return d.reshape(S5_GROUPS, S5_GROUP_CH, S5_STATE)


def ada_fwd(c_all, w_loc, b_loc):
    def body(c_ref, w_ref, b_ref, o_ref):
        o_ref[...] = _dg(_silu(c_ref[...]), w_ref[...], 1, 0) + b_ref[...]

    return pl.pallas_call(body, name="ada_fwd",
                          out_shape=jax.ShapeDtypeStruct((c_all.shape[0], w_loc.shape[1]), F32),
                          compiler_params=_cp())(c_all, w_loc, b_loc)


def ada_bwd(c_all, dmod_all, dmod_cols):
    def body(c_ref, da_ref, dc_ref, gb_ref, gw_ref):
        gb_ref[...] = jnp.sum(da_ref[...], axis=0, keepdims=True)
        gw_ref[...] = _dg(_silu(c_ref[...]), dc_ref[...], 0, 0)

    return pl.pallas_call(body, name="ada_bwd",
                          out_shape=[jax.ShapeDtypeStruct((1, dmod_all.shape[1]), F32),
                                     jax.ShapeDtypeStruct((c_all.shape[1], dmod_cols.shape[1]), F32)],
                          compiler_params=_cp())(c_all, dmod_all, dmod_cols)


_FLIPS = [(0, 0, 1), (1, 0, 0), (0, 1, 0), (1, 1, 0), (1, 0, 1), (0, 1, 1), (1, 1, 1)]


def _exchange_ops(srcs, outs, sems, gather):
    n = len(srcs)
    send_sems, recv_sems, loc_sems = sems
    x, y, c = lax.axis_index("x"), lax.axis_index("y"), lax.axis_index("c")
    me = 4 * x + 2 * y + c
    peers = []
    for fx, fy, fc in _FLIPS:
        px, py, pc = (1 - x if fx else x), (1 - y if fy else y), (1 - c if fc else c)
        peers.append(((px, py, pc), 4 * px + 2 * py + pc))

    def copy(k, j, slot_src, slot_dst):
        src = srcs[k] if gather[k] else srcs[k].at[slot_src]
        return pltpu.make_async_remote_copy(src_ref=src, dst_ref=outs[k].at[slot_dst],
                                            send_sem=send_sems.at[k, j], recv_sem=recv_sems.at[k, j],
                                            device_id=peers[j][0], device_id_type=MESH)

    def local(k):
        own = srcs[k] if gather[k] else srcs[k].at[me]
        return pltpu.make_async_copy(own, outs[k].at[me], loc_sems.at[k])

    def start():
        for k in range(n):
            for j in range(N_DEV - 1):
                copy(k, j, peers[j][1], me).start()
            local(k).start()

    def wait():
        for k in range(n):
            for j in range(N_DEV - 1):
                copy(k, j, me, peers[j][1]).wait_recv()
        for k in range(n):
            for j in range(N_DEV - 1):
                copy(k, j, peers[j][1], me).wait_send()
            local(k).wait()

    return start, wait


def _gather_two_level(srcs, outs, sems):
    n = len(srcs)
    send_sems, recv_sems, loc_sems = sems
    x, y, c = lax.axis_index("x"), lax.axis_index("y"), lax.axis_index("c")
    slot = lambda px, py, pc: 4 * px + 2 * py + pc
    me, sibling = (x, y, c), (x, y, 1 - c)
    chips = [(1 - x, y), (x, 1 - y), (1 - x, 1 - y)]

    def copy(k, j, block, to, own=False):
        return pltpu.make_async_remote_copy(src_ref=srcs[k] if own else outs[k].at[slot(*block)],
                                            dst_ref=outs[k].at[slot(*block)],
                                            send_sem=send_sems.at[k, j], recv_sem=recv_sems.at[k, j],
                                            device_id=to, device_id_type=MESH)

    locs = [pltpu.make_async_copy(srcs[k], outs[k].at[slot(*me)], loc_sems.at[k]) for k in range(n)]
    for k in range(n):
        locs[k].start()
        copy(k, 0, me, sibling, own=True).start()
        for j, chip in enumerate(chips):
            copy(k, 1 + j, me, (*chip, c), own=True).start()
    for j, chip in enumerate(chips):
        for k in range(n):
            copy(k, 1 + j, (*chip, c), me).wait_recv()
            copy(k, 4 + j, (*chip, c), sibling).start()
    for k in range(n):
        copy(k, 0, sibling, me).wait_recv()
        for j, chip in enumerate(chips):
            copy(k, 4 + j, (*chip, 1 - c), me).wait_recv()
    for k in range(n):
        copy(k, 0, me, sibling, own=True).wait_send()
        for j, chip in enumerate(chips):
            copy(k, 1 + j, me, (*chip, c), own=True).wait_send()
            copy(k, 4 + j, (*chip, c), sibling).wait_send()
        locs[k].wait()


def gather_two_level(name, arrs):
    n = len(arrs)
    specs, shapes, sems = _exchange_parts(arrs, [True] * n)

    def body(*refs):
        _gather_two_level(refs[:n], refs[n:2 * n], refs[2 * n:])

    return pl.pallas_call(
        body, name=name, in_specs=specs, out_specs=specs, out_shape=shapes, scratch_shapes=sems,
        compiler_params=pltpu.CompilerParams(has_side_effects=True),
    )(*arrs)


def _exchange_parts(arrs, gather):
    n = len(arrs)
    any_spec = pl.BlockSpec(memory_space=pl.ANY)
    shapes = [jax.ShapeDtypeStruct(((N_DEV,) + a.shape) if g else a.shape, a.dtype) for a, g in zip(arrs, gather)]
    sems = [pltpu.SemaphoreType.DMA((n, N_DEV - 1)), pltpu.SemaphoreType.DMA((n, N_DEV - 1)),
            pltpu.SemaphoreType.DMA((n,))]
    return [any_spec] * n, shapes, sems


def exchange(name, arrs, gather):
    n = len(arrs)
    specs, shapes, sems = _exchange_parts(arrs, gather)

    def body(*refs):
        start, wait = _exchange_ops(refs[:n], refs[n:2 * n], refs[2 * n:], gather)
        start()
        wait()

    return pl.pallas_call(
        body, name=name, in_specs=specs, out_specs=specs, out_shape=shapes, scratch_shapes=sems,
        compiler_params=pltpu.CompilerParams(has_side_effects=True),
    )(*arrs)


def hosted_call(body, *, name, grid, in_specs, out_specs, out_shape, args, scratch=(), ride=None):
    sem = ("arbitrary",) * len(grid)
    if ride is None:
        res = pl.pallas_call(body, name=name, grid=grid, in_specs=in_specs, out_specs=out_specs, out_shape=out_shape,
                             scratch_shapes=list(scratch), compiler_params=_cp(*sem))(*args)
        return list(res), []
    arrs, gather = ride
    n, n_in, n_out, n_scr = len(arrs), len(in_specs), len(out_specs), len(scratch)
    specs, shapes, sems = _exchange_parts(arrs, gather)

    def both(*refs):
        ins, srcs = refs[:n_in], refs[n_in:n_in + n]
        outs, landed = refs[n_in + n:n_in + n + n_out], refs[n_in + n + n_out:n_in + 2 * n + n_out]
        scr, ex_sems = refs[n_in + 2 * n + n_out:n_in + 2 * n + n_out + n_scr], refs[n_in + 2 * n + n_out + n_scr:]
        start, wait = _exchange_ops(srcs, landed, ex_sems, gather)
        first = functools.reduce(lambda a, b: a & b, [pl.program_id(d) == 0 for d in range(len(grid))])
        last = functools.reduce(lambda a, b: a & b, [pl.program_id(d) == grid[d] - 1 for d in range(len(grid))])
        pl.when(first)(start)
        body(*ins, *outs, *scr)
        pl.when(last)(wait)

    res = pl.pallas_call(
        both, name=name, grid=grid, in_specs=list(in_specs) + specs, out_specs=list(out_specs) + specs,
        out_shape=list(out_shape) + shapes, scratch_shapes=list(scratch) + sems, compiler_params=_cp(*sem),
    )(*args, *arrs)
    return list(res[:n_out]), list(res[n_out:])


def sum_slots(name, slots, tr):
    _, r, c = slots.shape

    def body(s_ref, o_ref):
        acc = s_ref[0].astype(F32)
        for j in range(1, N_DEV):
            acc = acc + s_ref[j].astype(F32)
        o_ref[...] = acc

    return pl.pallas_call(
        body, name=name, grid=(r // tr,), in_specs=[pl.BlockSpec((N_DEV, tr, c), lambda i: (0, i, 0))],
        out_specs=pl.BlockSpec((tr, c), lambda i: (i, 0)), out_shape=jax.ShapeDtypeStruct((r, c), F32),
        compiler_params=_cp("parallel"),
    )(slots)


def adamw(name, g, w, m, v, tr):
    slots = g.ndim == 3
    r, c = w.shape
    c1, c2 = 1.0 - ADAM_B1 ** ADAM_STEP, 1.0 - ADAM_B2 ** ADAM_STEP

    def body(g_ref, w_ref, m_ref, v_ref, go, do, mo, vo):
        if slots:
            gg = g_ref[0].astype(F32)
            for j in range(1, N_DEV):
                gg = gg + g_ref[j].astype(F32)
        else:
            gg = g_ref[...]
        mn = ADAM_B1 * m_ref[...] + (1.0 - ADAM_B1) * gg
        vn = ADAM_B2 * v_ref[...] + (1.0 - ADAM_B2) * (gg * gg)
        go[...], mo[...], vo[...] = gg, mn, vn
        do[...] = -ADAM_LR * ((mn / c1) / (jnp.sqrt(vn / c2) + ADAM_EPS) + ADAM_WD * w_ref[...])

    blk = pl.BlockSpec((tr, c), lambda i: (i, 0))
    gspec = pl.BlockSpec((N_DEV, tr, c), lambda i: (0, i, 0)) if slots else blk
    return pl.pallas_call(
        body, name=name, grid=(r // tr,), in_specs=[gspec, blk, blk, blk], out_specs=[blk] * 4,
        out_shape=[jax.ShapeDtypeStruct((r, c), F32)] * 4, compiler_params=_cp("parallel"),
    )(g, w, m, v)


def _lane_rows(n):
    return -(-n // (8 * LANE)) * 8


def _pack(arrs):
    pieces = []
    for a in arrs:
        n = math.prod(a.shape)
        flat = a.reshape(-1).astype(F32)
        pieces.append(jnp.pad(flat, (0, _lane_rows(n) * LANE - n)).reshape(_lane_rows(n), LANE))
    return jnp.concatenate(pieces, axis=0)


def _unpack(buf, shapes):
    out, off = [], 0
    for s in shapes:
        n = math.prod(s)
        out.append(buf[off:off + _lane_rows(n)].reshape(-1)[:n].reshape(s))
        off += _lane_rows(n)
    return out


FF_CHUNK = D_FF
DW_TOKENS = 2048
FFN_TM = 256


def _resident(shape):
    return pl.BlockSpec(shape, lambda i: (0,) * len(shape), pipeline_mode=pl.Buffered(1))


def _ffn_fwd(tag, x, sc, sh, g, w1, w3, w2, lg, lb, seq, tm, ride=None, target=None):
    t = x.shape[0]
    tm = min(FFN_TM, tm)
    tps = seq // tm
    ln = _res_ln(0.5)
    head = target is not None

    def body(x_ref, sc_ref, sh_ref, g_ref, lg_ref, lb_ref, w1_ref, w3_ref, w2_ref, *rest):
        if head:
            t_ref, y_ref, h_ref, a_ref, b_ref, f_ref, l_ref = rest
        else:
            y_ref, h_ref, a_ref, b_ref, f_ref = rest
        xv = x_ref[...]
        h = (xv * (1.0 + sc_ref[0]) + sh_ref[0]).astype(BF16)
        h_ref[...] = h
        acc = jnp.zeros((tm, D_MODEL), F32)
        for j in range(D_FF // FF_CHUNK):
            sl = slice(j * FF_CHUNK, (j + 1) * FF_CHUNK)
            a = _dg(h, w1_ref[sl, :], 1, 1)
            b = _dg(h, w3_ref[sl, :], 1, 1)
            a_ref[:, sl] = a
            b_ref[:, sl] = b
            acc = acc + _dg(_silu(a) * b, w2_ref[sl, :], 1, 0)
        f_ref[...] = acc
        y = ln(xv, acc, g_ref[0], lg_ref[...], lb_ref[...])[0]
        if head:
            @pl.when(pl.program_id(0) == 0)
            def _():
                l_ref[...] = jnp.zeros_like(l_ref)

            e = y - t_ref[...]
            y_ref[...] = e * (1.0 / D_MODEL)
            l_ref[...] += 0.5 * jnp.sum(jnp.mean(e * e, axis=-1, keepdims=True), axis=0, keepdims=True)
        else:
            y_ref[...] = y

    row = lambda c: pl.BlockSpec((tm, c), lambda i: (i, 0))
    per_seq = pl.BlockSpec((1, 1, D_MODEL), lambda i: (i // tps, 0, 0))
    vec = pl.BlockSpec((1, D_MODEL), lambda i: (0, 0))
    res, landed = hosted_call(
        body, name=tag + "_fwd", grid=(t // tm,),
        in_specs=[row(D_MODEL), per_seq, per_seq, per_seq, vec, vec,
                  _resident((D_FF, D_MODEL)), _resident((D_FF, D_MODEL)), _resident((D_FF, D_MODEL))]
        + ([row(D_MODEL)] if head else []),
        out_specs=[row(D_MODEL), row(D_MODEL), row(D_FF), row(D_FF), row(D_MODEL)]
        + ([pl.BlockSpec((1, 1), lambda i: (0, 0))] if head else []),
        out_shape=[jax.ShapeDtypeStruct((t, D_MODEL), F32), jax.ShapeDtypeStruct((t, D_MODEL), BF16),
                   jax.ShapeDtypeStruct((t, D_FF), F32), jax.ShapeDtypeStruct((t, D_FF), F32),
                   jax.ShapeDtypeStruct((t, D_MODEL), F32)] + ([jax.ShapeDtypeStruct((1, 1), F32)] if head else []),
        args=(x, sc, sh, g, lg, lb, w1, w3, w2) + ((target,) if head else ()), ride=ride)
    first = (res[0], res[5][0, 0]) if head else res[0]
    return first, tuple(res[1:5]), landed


def _ffn_bwd(tag, dy, x, sc, sh, g, w1, w3, w2, lg, lb, res, seq, tm, ride=None, chain=None):
    h, a, b, f = res
    t = x.shape[0]
    tmk = min(FFN_TM, tm)
    tps = seq // tmk
    ln = _res_ln(0.5)

    def body(dy_ref, x_ref, f_ref, a_ref, b_ref, sc_ref, sh_ref, g_ref, lg_ref, lb_ref, w1_ref, w3_ref, w2_ref,
             dx_ref, da_ref, db_ref, s_ref, df_ref, dsc_ref, dsh_ref, dg_ref, dlg_ref, dlb_ref):
        i = pl.program_id(0)

        @pl.when(i % tps == 0)
        def _():
            for r in (dsc_ref, dsh_ref, dg_ref):
                r[...] = jnp.zeros_like(r)

        @pl.when(i == 0)
        def _():
            dlg_ref[...] = jnp.zeros_like(dlg_ref)
            dlb_ref[...] = jnp.zeros_like(dlb_ref)

        xv = x_ref[...]
        _, pull = jax.vjp(ln, xv, f_ref[...], g_ref[0], lg_ref[...], lb_ref[...])
        dx_res, df, dg, dlg, dlb = pull((dy_ref[...],))
        dfb = df.astype(BF16)
        df_ref[...] = dfb
        dh = jnp.zeros((tmk, D_MODEL), F32)
        for j in range(D_FF // FF_CHUNK):
            sl = slice(j * FF_CHUNK, (j + 1) * FF_CHUNK)
            ds = _dg(dfb, w2_ref[sl, :], 1, 1)
            av, bv = a_ref[:, sl], b_ref[:, sl]
            sg = jax.nn.sigmoid(av)
            si = av * sg
            s_ref[:, sl] = (si * bv).astype(BF16)
            da = (ds * bv * (sg * (1.0 + av * (1.0 - sg)))).astype(BF16)
            db = (ds * si).astype(BF16)
            da_ref[:, sl] = da
            db_ref[:, sl] = db
            dh = dh + _dg(da, w1_ref[sl, :], 1, 0) + _dg(db, w3_ref[sl, :], 1, 0)
        dx_ref[...] = dx_res + dh * (1.0 + sc_ref[0])
        dsc_ref[0] += jnp.sum(dh * xv, axis=0, keepdims=True)
        dsh_ref[0] += jnp.sum(dh, axis=0, keepdims=True)
        dg_ref[0] += dg
        dlg_ref[...] += dlg
        dlb_ref[...] += dlb

    row = lambda c: pl.BlockSpec((tmk, c), lambda i: (i, 0))
    per_seq = pl.BlockSpec((1, 1, D_MODEL), lambda i: (i // tps, 0, 0))
    vec = pl.BlockSpec((1, D_MODEL), lambda i: (0, 0))
    seq_shape = jax.ShapeDtypeStruct(sc.shape, F32)
    vec_shape = jax.ShapeDtypeStruct((1, D_MODEL), F32)
    (dx, da, db, s, df, dsc, dsh, dg, dlg, dlb), landed = hosted_call(
        body, name=tag + "_bwd", grid=(t // tmk,),
        in_specs=[row(D_MODEL), row(D_MODEL), row(D_MODEL), row(D_FF), row(D_FF), per_seq, per_seq, per_seq, vec, vec,
                  _resident((D_FF, D_MODEL)), _resident((D_FF, D_MODEL)), _resident((D_FF, D_MODEL))],
        out_specs=[row(D_MODEL), row(D_FF), row(D_FF), row(D_FF), row(D_MODEL), per_seq, per_seq, per_seq, vec, vec],
        out_shape=[jax.ShapeDtypeStruct((t, D_MODEL), F32), jax.ShapeDtypeStruct((t, D_FF), BF16),
                   jax.ShapeDtypeStruct((t, D_FF), BF16), jax.ShapeDtypeStruct((t, D_FF), BF16),
                   jax.ShapeDtypeStruct((t, D_MODEL), BF16), seq_shape, seq_shape, seq_shape, vec_shape, vec_shape],
        args=(dy, x, f, a, b, sc, sh, g, lg, lb, w1, w3, w2), ride=ride)
    tt = min(DW_TOKENS, seq)
    shards = lambda dw: dw.reshape(N_DEV, D_FF // N_DEV, D_MODEL)
    if chain is None:
        dw2, landed = mm_tn(tag + "_dw2", s, df, D_FF // 2, D_MODEL, tt, BF16), []
    else:
        dw2, landed = mm_tn(tag + "_dw2", s, df, D_FF // 2, D_MODEL, tt, BF16, ride=chain((dsh, dsc, dg), dlg, dlb))
    dw1, (s_w2,) = mm_tn(tag + "_dw1", da, h, D_FF // 2, D_MODEL, tt, BF16, ride=([shards(dw2)], [False]))
    dw3, (s_w1,) = mm_tn(tag + "_dw3", db, h, D_FF // 2, D_MODEL, tt, BF16, ride=([shards(dw1)], [False]))
    return dx, (dsh, dsc, dg), (s_w1, shards(dw3), s_w2, dlg, dlb), landed


def kernel(x, c, w_ada, b_ada, ffn1_w1, ffn1_w3, ffn1_w2, ln1_g, ln1_b, w_in, conv_w, conv_b, dt_bias, a_log, d_ssd, ssd_norm_w, s5_a_re, s5_a_im, s5_log_dt, s5_b_re, s5_b_im, s5_c_re, s5_c_im, s5_d, w_glu, b_glu, w_out, ln2_g, ln2_b, ffn2_w1, ffn2_w3, ffn2_w2, ln3_g, ln3_b, loss_target, m_w_ada, m_b_ada, m_ffn1_w1, m_ffn1_w3, m_ffn1_w2, m_ln1_g, m_ln1_b, m_w_in, m_conv_w, m_conv_b, m_dt_bias, m_a_log, m_d_ssd, m_ssd_norm_w, m_s5_a_re, m_s5_a_im, m_s5_log_dt, m_s5_b_re, m_s5_b_im, m_s5_c_re, m_s5_c_im, m_s5_d, m_w_glu, m_b_glu, m_w_out, m_ln2_g, m_ln2_b, m_ffn2_w1, m_ffn2_w3, m_ffn2_w2, m_ln3_g, m_ln3_b, v_w_ada, v_b_ada, v_ffn1_w1, v_ffn1_w3, v_ffn1_w2, v_ln1_g, v_ln1_b, v_w_in, v_conv_w, v_conv_b, v_dt_bias, v_a_log, v_d_ssd, v_ssd_norm_w, v_s5_a_re, v_s5_a_im, v_s5_log_dt, v_s5_b_re, v_s5_b_im, v_s5_c_re, v_s5_c_im, v_s5_d, v_w_glu, v_b_glu, v_w_out, v_ln2_g, v_ln2_b, v_ffn2_w1, v_ffn2_w3, v_ffn2_w2, v_ln3_g, v_ln3_b):
    given = dict(locals())
    bsz, seq, _ = x.shape
    t = bsz * seq
    tm = min(1024, seq)
    me = 4 * lax.axis_index("x") + 2 * lax.axis_index("y") + lax.axis_index("c")
    x0 = x.reshape(t, D_MODEL)
    target = loss_target.reshape(t, D_MODEL)

    tr16 = lambda w: w[0].T.astype(BF16)
    whole = lambda g: g.reshape(N_DEV * g.shape[1], g.shape[2])
    g_f1w1, g_f1w3, g_f1w2, g_c = gather_two_level(
        "gather_ffn1", [tr16(ffn1_w1), tr16(ffn1_w3), ffn1_w2[0].astype(BF16), c])
    f1w1, f1w3, f1w2 = whole(g_f1w1), whole(g_f1w3), whole(g_f1w2)
    c_all = whole(g_c)

    n_loc = w_ada.shape[2]
    b_loc = lax.dynamic_slice(b_ada, (0, me * n_loc), (1, n_loc))
    mod_cols = ada_fwd(c_all, w_ada[0], b_loc)
    g_mod, = exchange("gather_mod", [mod_cols], [True])
    mine = lax.dynamic_slice(g_mod, (0, me * bsz, 0), (N_DEV, bsz, n_loc))
    mod = jnp.transpose(mine, (1, 0, 2)).reshape(bsz, N_MOD, 1, D_MODEL)
    sh1, sc1, g1, sh2, sc2, g2, sh3, sc3, g3 = [mod[:, k] for k in range(N_MOD)]

    x1, res1, (g_win, g_glu, g_out, g_conv, g_f2w1) = _ffn_fwd(
        "ffn1", x0, sc1, sh1, g1, f1w1, f1w3, f1w2, ln1_g, ln1_b, seq, tm,
        ride=([tr16(w_in), w_glu[0].astype(BF16), w_out[0].astype(BF16), conv_w[0], tr16(ffn2_w1)], [True] * 5))
    win = whole(g_win)
    wglu = whole(g_glu).astype(F32)
    wout = whole(g_out)
    wo_ssd, wo_s5 = wout[:SSD_WIDTH], wout[SSD_WIDTH:]
    convw = jnp.transpose(g_conv, (1, 0, 2)).reshape(CONV_K, CONV_CH)
    w_z, w_xbc = win[:SSD_WIDTH], win[SSD_WIDTH:SSD_WIDTH + CONV_CH]
    w_dt = win[SSD_WIDTH + CONV_CH:SSD_WIDTH + CONV_CH + SSD_HEADS]
    w_u = win[SSD_WIDTH + CONV_CH + SSD_HEADS:]
    dt_pad = [jnp.pad(w_dt[HEADS_PER_GROUP * g:HEADS_PER_GROUP * (g + 1)], ((0, LANE - HEADS_PER_GROUP), (0, 0)))
              for g in range(SSD_GROUPS)]
    w_dtp = jnp.concatenate(dt_pad, axis=0)
    w_proj = jnp.concatenate([w_xbc, w_z, w_u, w_dtp], axis=0)

    h2, = rowwise_fwd("mix_mod", f_modulate, [x1], [sc2, sh2], [], [(D_MODEL, BF16)], seq, tm)
    proj = mm_nt("mix_proj", [h2], [w_proj], tm, P_COLS // 2)
    xc = conv_fwd(proj, convw, conv_b, seq, tm)
    dtb = jnp.pad(dt_bias.reshape(SSD_GROUPS, 1, HEADS_PER_GROUP), ((0, 0), (0, 0), (0, LANE - HEADS_PER_GROUP)))
    alog = jnp.pad(a_log.reshape(SSD_GROUPS, 1, HEADS_PER_GROUP), ((0, 0), (0, 0), (0, LANE - HEADS_PER_GROUP)))
    dcol = jnp.pad(d_ssd.reshape(SSD_GROUPS, HEADS_PER_GROUP, 1), ((0, 0), (0, LANE - HEADS_PER_GROUP), (0, 0)))
    nw = ssd_norm_w.reshape(SSD_GROUPS, 1, GROUP_COLS)
    (y_ssd, hprev), (g_f2w3,) = ssd_fwd(xc, proj, dtb, alog, dcol, nw, bsz, seq,
                                        ride=([tr16(ffn2_w3)], [True]))

    a_re2, a_im2, ldt2 = s5_a_re[0], s5_a_im[0], s5_log_dt.reshape(S5_GROUPS, 1)
    ab_re, ab_im, f_re, f_im = _whole(_disc_a, "s5_disc_a", [a_re2, a_im2, ldt2], [(S5_GROUPS, S5_STATE)] * 4)
    b_re2, b_im2 = s5_b_re.reshape(S5_COLS, S5_GROUP_CH), s5_b_im.reshape(S5_COLS, S5_GROUP_CH)
    fr_col, fi_col = f_re.reshape(S5_COLS, 1), f_im.reshape(S5_COLS, 1)
    bb_re, bb_im = _whole(_disc_b, "s5_disc_b", [fr_col, fi_col, b_re2, b_im2], [(S5_COLS, S5_GROUP_CH)] * 2)
    wb_re, wb_im = _blockdiag_b(bb_re).astype(BF16), _blockdiag_b(bb_im).astype(BF16)
    wc_re, wc_im = _blockdiag_c(s5_c_re[0]).astype(BF16), _blockdiag_c(s5_c_im[0]).astype(BF16)
    dt5 = jnp.exp(ldt2)
    lam_re, lam_im = (dt5 * a_re2).reshape(1, S5_COLS), (dt5 * a_im2).reshape(1, S5_COLS)
    sf_re, sf_im, sb_re, sb_im, cf_re, cf_im, cb_re, cb_im = s5_tables(lam_re, lam_im)
    d5 = s5_d.reshape(S5_Q, 1, LANE)
    (y5, xr_all, xi_all), (g_f2w2,) = s5_fwd(
        proj, wb_re, wb_im, wc_re, wc_im, sf_re, sf_im, cf_re, cf_im, d5, bsz, seq,
        ride=([ffn2_w2[0].astype(BF16)], [True]))
    f2w1, f2w3, f2w2 = whole(g_f2w1), whole(g_f2w3), whole(g_f2w2)
    o5, = rowwise_fwd("s5_glu", f_glu, [y5], [], [wglu, b_glu], [(S5_WIDTH, F32)], seq, tm)

    mix = mm_nn("mix_out", [y_ssd, o5], [wo_ssd, wo_s5], tm, D_MODEL)
    x2, = rowwise_fwd("mix_ln", _res_ln(1.0), [x1, mix], [g2], [ln2_g, ln2_b], [(D_MODEL, F32)], seq, tm)

    (dy, loss_loc), res3, _ = _ffn_fwd("ffn2", x2, sc3, sh3, g3, f2w1, f2w3, f2w2, ln3_g, ln3_b, seq, tm, target=target)

    dx2, dmod3, (s_f2w1, d_f2w3, s_f2w2, d_ln3g, d_ln3b), _ = _ffn_bwd(
        "ffn2", dy, x2, sc3, sh3, g3, f2w1, f2w3, f2w2, ln3_g, ln3_b, res3, seq, tm)

    (dx1_a, dmix), (dg2,), (d_ln2g, d_ln2b) = rowwise_bwd(
        "mix_ln_b", _res_ln(1.0), [x1, mix], [g2], [ln2_g, ln2_b], [dx2], seq, tm, [F32, BF16])
    tw = min(DW_TOKENS, seq)
    d_wo = jnp.concatenate([mm_tn("mix_dwo_ssd", y_ssd, dmix, SSD_WIDTH, D_MODEL, tw, BF16),
                            mm_tn("mix_dwo_s5", o5, dmix, S5_WIDTH, D_MODEL, tw, BF16)], axis=0)
    dy_mixed = mm_nt("mix_dy", [dmix], [wout], tm, D_MODEL)
    dy_ssd, do5 = dy_mixed, (dy_mixed, SSD_WIDTH, S5_WIDTH)

    (dy5,), _, (d_wglu, d_bglu) = rowwise_bwd("s5_glu_b", f_glu, [y5], [], [wglu, b_glu], [do5], seq, tm, [F32])
    (du, dwbr, dwbi, dwcr, dwci, dab_re, dab_im, dd5), (s_f2w3, s_out, s_glu) = s5_bwd(
        proj, wb_re, wb_im, wc_re, wc_im, sb_re, sb_im, cb_re, cb_im, d5, xr_all, xi_all, dy5, bsz, seq,
        ride=([d_f2w3, d_wo.reshape(N_DEV, D_MODEL // N_DEV, D_MODEL),
               d_wglu.reshape(N_DEV, S5_WIDTH // N_DEV, S5_WIDTH).astype(BF16)], [False] * 3))
    dbb_re, dbb_im = _unblock_b(dwbr), _unblock_b(dwbi)
    dfr_col, dfi_col, d_b_re, d_b_im = _whole_vjp(_disc_b, "s5_disc_b_b", [fr_col, fi_col, b_re2, b_im2],
                                                  [dbb_re, dbb_im])
    d_a_re, d_a_im, d_ldt = _whole_vjp(
        _disc_a, "s5_disc_a_b", [a_re2, a_im2, ldt2],
        [dab_re.reshape(S5_GROUPS, S5_STATE), dab_im.reshape(S5_GROUPS, S5_STATE),
         dfr_col.reshape(S5_GROUPS, S5_STATE), dfi_col.reshape(S5_GROUPS, S5_STATE)])
    d_c_re, d_c_im = _unblock_c(dwcr), _unblock_c(dwci)

    dxs, dbm, dcm, ddt, dz, ddtb, dalog, ddcol, dnw = ssd_bwd(xc, proj, dtb, alog, dcol, nw, hprev, dy_ssd, bsz, seq)
    dpre, d_convw, d_convb = conv_bwd_pre(proj, convw, conv_b, dxs, dbm, dcm, seq, tm)
    dxbc = conv_bwd_x(dpre, convw, seq, tm)

    dw_xbc = mm_tn("mix_dw_xbc", dxbc, h2, CONV_CH, D_MODEL, tw, BF16)
    dw_z = mm_tn("mix_dw_z", dz, h2, SSD_WIDTH, D_MODEL, tw, BF16)
    dw_u = mm_tn("mix_dw_u", du, h2, S5_WIDTH, D_MODEL, tw, BF16)
    dw_dt = mm_tn("mix_dw_dt", ddt, h2, 2 * LANE, D_MODEL, tw, BF16)
    dw_dt8 = jnp.concatenate([dw_dt[LANE * g:LANE * g + HEADS_PER_GROUP] for g in range(SSD_GROUPS)], axis=0)
    d_win = jnp.concatenate([dw_z, dw_xbc, dw_dt8, dw_u], axis=0)
    dh2, (s_win,) = mm_nn("mix_dh", [dxbc, dz, du, ddt], [w_xbc, w_z, w_u, w_dtp], tm, D_MODEL,
                          ride=([d_win.reshape(N_DEV, IN_COLS // N_DEV, D_MODEL)], [False]))
    (dx1,), (dsc2, dsh2), _ = rowwise_bwd("mix_mod_b", f_modulate, [x1], [sc2, sh2], [], [dh2], seq, tm, [F32],
                                          add_rows={0: dx1_a})

    packing = {}

    def small_and_dmod(dmod1, d_ln1g, d_ln1b):
        dmod = jnp.concatenate(list(dmod1) + [dsh2, dsc2, dg2] + list(dmod3), axis=1).reshape(bsz, N_MOD * D_MODEL)
        small = _small_grads(d_ln1g, d_ln1b)
        packing["names"] = list(small)
        packing["shapes"] = [small[k].shape for k in small]
        return [_pack(list(small.values())), dmod], [True, True]

    def _small_grads(d_ln1g, d_ln1b):
        return {
            "ln1_g": d_ln1g, "ln1_b": d_ln1b, "conv_w": d_convw, "conv_b": d_convb,
            "dt_bias": ddtb[:, 0, :HEADS_PER_GROUP].reshape(1, SSD_HEADS),
            "a_log": dalog[:, 0, :HEADS_PER_GROUP].reshape(1, SSD_HEADS),
            "d_ssd": ddcol[:, :HEADS_PER_GROUP, 0].reshape(1, SSD_HEADS),
            "ssd_norm_w": dnw.reshape(1, SSD_WIDTH),
            "s5_a_re": d_a_re[None], "s5_a_im": d_a_im[None], "s5_log_dt": d_ldt.reshape(1, S5_GROUPS),
            "s5_b_re": d_b_re.reshape(s5_b_re.shape), "s5_b_im": d_b_im.reshape(s5_b_im.shape),
            "s5_c_re": d_c_re[None], "s5_c_im": d_c_im[None], "s5_d": dd5.reshape(1, S5_WIDTH),
            "b_glu": d_bglu, "ln2_g": d_ln2g, "ln2_b": d_ln2b, "ln3_g": d_ln3g, "ln3_b": d_ln3b,
            "loss": loss_loc.reshape(1, 1),
        }

    dx0, _, (s_f1w1, d_f1w3, s_f1w2, _, _), (s_small, s_dmod) = _ffn_bwd(
        "ffn1", dx1, x0, sc1, sh1, g1, f1w1, f1w3, f1w2, ln1_g, ln1_b, res1, seq, tm, chain=small_and_dmod)
    names, shapes = packing["names"], packing["shapes"]
    s_f1w3, = exchange("sum_grads", [d_f1w3], [False])

    out = {"grad_x": dx0.reshape(x.shape)}

    def put(name, res, shape):
        for key, val in zip(("grad_", "delta_", "new_m_", "new_v_"), res):
            out[key + name] = val.reshape(shape)

    for name, slots, tr in (("ffn1_w1", s_f1w1, 176), ("ffn1_w3", s_f1w3, 176), ("ffn2_w1", s_f2w1, 176),
                            ("ffn2_w3", s_f2w3, 176), ("w_in", s_win, IN_COLS // N_DEV)):
        w = given[name]
        grad = sum_slots("sum_" + name, slots, tr).T
        put(name, adamw("adam_" + name, grad, w[0], given["m_" + name][0], given["v_" + name][0], 256), w.shape)
    for name, slots in (("ffn1_w2", s_f1w2), ("ffn2_w2", s_f2w2)):
        w = given[name]
        put(name, adamw("adam_" + name, slots, w[0], given["m_" + name][0], given["v_" + name][0], 176), w.shape)
    put("w_glu", adamw("adam_w_glu", s_glu, w_glu[0], m_w_glu[0], v_w_glu[0], 64), w_glu.shape)
    put("w_out", adamw("adam_w_out", s_out, w_out[0], m_w_out[0], v_w_out[0], 128), w_out.shape)

    dmod_all = s_dmod.reshape(N_DEV * bsz, N_MOD * D_MODEL)
    g_bada, g_wada = ada_bwd(c_all, dmod_all, lax.dynamic_slice(dmod_all, (0, me * n_loc), (N_DEV * bsz, n_loc)))
    put("w_ada", adamw("adam_w_ada", g_wada, w_ada[0], m_w_ada[0], v_w_ada[0], 256), w_ada.shape)
    put("b_ada", adamw("adam_b_ada", g_bada, b_ada, m_b_ada, v_b_ada, 1), b_ada.shape)

    not_params = {"conv_w": jnp.zeros((CONV_K, CONV_CH), F32), "loss": jnp.zeros((1, 1), F32)}
    pw, pm, pv = [_pack([not_params[k] if k in not_params else given[pre + k] for k in names]) for pre in ("", "m_", "v_")]
    res_small = adamw("adam_small", s_small, pw, pm, pv, pw.shape[0])
    parts = [_unpack(r, shapes) for r in res_small]
    for i, k in enumerate(names):
        if k not in not_params:
            put(k, [p[i] for p in parts], given[k].shape)
    out["loss"] = parts[0][names.index("loss")][0, 0]
    g_cw = lax.dynamic_slice(parts[0][names.index("conv_w")], (0, me * LANE), (CONV_K, LANE))
    put("conv_w", adamw("adam_conv_w", g_cw, conv_w[0], m_conv_w[0], v_conv_w[0], CONV_K), conv_w.shape)

    order = ["w_ada", "b_ada", "ffn1_w1", "ffn1_w3", "ffn1_w2", "ln1_g", "ln1_b", "w_in", "conv_w", "conv_b", "dt_bias",
             "a_log", "d_ssd", "ssd_norm_w", "s5_a_re", "s5_a_im", "s5_log_dt", "s5_b_re", "s5_b_im", "s5_c_re",
             "s5_c_im", "s5_d", "w_glu", "b_glu", "w_out", "ln2_g", "ln2_b", "ffn2_w1", "ffn2_w3", "ffn2_w2", "ln3_g",
             "ln3_b"]
    return (out["loss"], out["grad_x"], *[out[p + n] for p in ("grad_", "delta_", "new_m_", "new_v_") for n in order])
```

```python
import functools
import math

import jax
import jax.numpy as jnp
from jax import lax
from jax.experimental import pallas as pl
from jax.experimental.pallas import tpu as pltpu

F32 = jnp.float32
BF16 = jnp.bfloat16
MESH = pl.DeviceIdType.MESH

N_DEV = 8
D_MODEL = 1024
D_FF = 2816
N_MOD = 9
SSD_WIDTH = 512
SSD_HEADS = 8
SSD_HEAD_DIM = 64
SSD_GROUPS = 2
SSD_STATE = 128
SSD_CHUNK = 128
GROUP_COLS = SSD_WIDTH // SSD_GROUPS
HEADS_PER_GROUP = SSD_HEADS // SSD_GROUPS
CONV_K = 4
CONV_CH = 1024
S5_WIDTH = 512
S5_GROUPS = 32
S5_GROUP_CH = 16
S5_STATE = 64
S5_COLS = S5_GROUPS * S5_STATE
S5_Q = 4
S5_CHUNK = 2048
ALPHA = 2.0 ** 0.25
LN_EPS = 1e-5
LANE = 128
HALO = 8

P_XBC, P_Z, P_U, P_DT = 0, 1024, 1536, 2048
P_COLS = 2048 + SSD_GROUPS * LANE
IN_COLS = SSD_WIDTH + CONV_CH + SSD_HEADS + S5_WIDTH

ADAM_LR, ADAM_B1, ADAM_B2, ADAM_EPS, ADAM_WD, ADAM_STEP = 0.001, 0.9, 0.999, 1e-08, 0.01, 10

VMEM_LIMIT = 56 * 1024 * 1024


def _cp(*sem):
    return pltpu.CompilerParams(dimension_semantics=sem if sem else None, vmem_limit_bytes=VMEM_LIMIT)


def _dg(a, b, ca, cb):
    return lax.dot_general(a.astype(BF16), b.astype(BF16), (((ca,), (cb,)), ((), ())), preferred_element_type=F32)


@jax.custom_vjp
def bdot_nn(a, b):
    return _dg(a, b, 1, 0)


bdot_nn.defvjp(lambda a, b: (_dg(a, b, 1, 0), (a, b)),
               lambda r, g: (_dg(g, r[1], 1, 1), _dg(r[0], g, 0, 0)))


@jax.custom_vjp
def bdot_nt(a, b):
    return _dg(a, b, 1, 1)


bdot_nt.defvjp(lambda a, b: (_dg(a, b, 1, 1), (a, b)),
               lambda r, g: (_dg(g, r[1], 1, 0), _dg(g, r[0], 0, 0)))


@jax.custom_vjp
def bdot_tn(a, b):
    return _dg(a, b, 0, 0)


bdot_tn.defvjp(lambda a, b: (_dg(a, b, 0, 0), (a, b)),
               lambda r, g: (_dg(r[1], g, 1, 1), _dg(r[0], g, 1, 0)))


def _split3(x):
    def top(v):
        bits = lax.bitcast_convert_type(v, jnp.int32) & jnp.int32(-65536)
        return lax.bitcast_convert_type(bits, F32)

    hi = top(x)
    r1 = x - hi
    mid = top(r1)
    return hi.astype(BF16), mid.astype(BF16), (r1 - mid).astype(BF16)


def _dot3(a, b, ca, cb, split_a):
    dims = (((ca,), (cb,)), ((), ()))
    if split_a:
        c = b.astype(BF16)
        return sum(lax.dot_general(p, c, dims, preferred_element_type=F32) for p in _split3(a))
    c = a.astype(BF16)
    return sum(lax.dot_general(c, p, dims, preferred_element_type=F32) for p in _split3(b))


@jax.custom_vjp
def mask_dot_left(c, x):
    return _dot3(c, x, 1, 0, False)


mask_dot_left.defvjp(lambda c, x: (_dot3(c, x, 1, 0, False), c),
                     lambda c, g: (jnp.zeros_like(c), _dot3(c, g, 0, 0, False)))


@jax.custom_vjp
def mask_dot_right(x, c):
    return _dot3(x, c, 1, 0, True)


mask_dot_right.defvjp(lambda x, c: (_dot3(x, c, 1, 0, True), c),
                      lambda c, g: (_dot3(g, c, 1, 1, True), jnp.zeros_like(c)))


def _take_col(z):
    @jax.custom_vjp
    def take(x):
        return x[:, z:z + 1]

    def bwd(shape, g):
        hot = (lax.broadcasted_iota(jnp.int32, (1, shape[1]), 1) == z).astype(F32)
        return (g * hot,)

    take.defvjp(lambda x: (x[:, z:z + 1], x.shape), bwd)
    return take


def _take_row(z):
    @jax.custom_vjp
    def take(x):
        return x[z:z + 1, :]

    def bwd(shape, g):
        hot = (lax.broadcasted_iota(jnp.int32, (shape[0], 1), 0) == z).astype(F32)
        return (hot * g,)

    take.defvjp(lambda x: (x[z:z + 1, :], x.shape), bwd)
    return take


def _view(a):
    return a if isinstance(a, tuple) else (a, 0, a.shape[1])


def _col_spec(view, rows, width, index):
    _, off, _ = view
    assert off % width == 0
    return pl.BlockSpec((rows, width), lambda *g: (index(*g)[0], off // width + index(*g)[1]))


def _rw_in_specs(rows, bps, gps, tm, tps):
    specs = [_col_spec(_view(r), tm, _view(r)[2], lambda i: (i, 0)) for r in rows]
    specs += [pl.BlockSpec((1, 1, b.shape[2]), lambda i: (i // tps, 0, 0)) for b in bps]
    specs += [pl.BlockSpec(g.shape, lambda i, nd=g.ndim: (0,) * nd) for g in gps]
    return specs


def _rw_vals(refs, nr, nb, ng):
    vals = [r[...] for r in refs[:nr]]
    vals += [b[0] for b in refs[nr:nr + nb]]
    vals += [g[...] for g in refs[nr + nb:nr + nb + ng]]
    return vals


def rowwise_fwd(name, f, rows, bps, gps, outs, seq, tm):
    t = _view(rows[0])[0].shape[0]
    tps = seq // tm
    nr, nb, ng = len(rows), len(bps), len(gps)

    def body(*refs):
        res = f(*_rw_vals(refs, nr, nb, ng))
        for o, v in zip(refs[nr + nb + ng:], res):
            o[...] = v.astype(o.dtype)

    return pl.pallas_call(
        body, name=name, grid=(t // tm,),
        in_specs=_rw_in_specs(rows, bps, gps, tm, tps),
        out_specs=[pl.BlockSpec((tm, c), lambda i: (i, 0)) for c, _ in outs],
        out_shape=[jax.ShapeDtypeStruct((t, c), d) for c, d in outs],
        compiler_params=_cp("arbitrary"),
    )(*[_view(r)[0] for r in rows], *bps, *gps)


def rowwise_bwd(name, f, rows, bps, gps, douts, seq, tm, row_grads, add_rows=None):
    add_rows = add_rows or {}
    t = _view(rows[0])[0].shape[0]
    tps = seq // tm
    nr, nb, ng, nd = len(rows), len(bps), len(gps), len(douts)
    want = [k for k in range(nr) if row_grads[k] is not None]
    adds = sorted(add_rows)
    n_in = nr + nb + ng + nd + len(adds)

    def body(*refs):
        vals = _rw_vals(refs, nr, nb, ng)
        dvals = tuple(r[...] for r in refs[nr + nb + ng:nr + nb + ng + nd])
        add_refs = dict(zip(adds, refs[nr + nb + ng + nd:n_in]))
        out_refs = refs[n_in:]
        _, pull = jax.vjp(f, *vals)
        grads = pull(dvals)
        i = pl.program_id(0)
        for o, k in zip(out_refs, want):
            g = grads[k]
            if k in add_refs:
                g = g + add_refs[k][...]
            o[...] = g.astype(o.dtype)
        for j in range(nb):
            o = out_refs[len(want) + j]

            @pl.when(i % tps == 0)
            def _(o=o):
                o[...] = jnp.zeros_like(o)

            o[0] = o[0] + grads[nr + j]
        for j in range(ng):
            o = out_refs[len(want) + nb + j]

            @pl.when(i == 0)
            def _(o=o):
                o[...] = jnp.zeros_like(o)

            o[...] = o[...] + grads[nr + nb + j]

    in_specs = _rw_in_specs(rows, bps, gps, tm, tps)
    in_specs += [_col_spec(_view(d), tm, _view(d)[2], lambda i: (i, 0)) for d in douts]
    in_specs += [pl.BlockSpec((tm, add_rows[k].shape[1]), lambda i: (i, 0)) for k in adds]
    out_specs = [pl.BlockSpec((tm, _view(rows[k])[2]), lambda i: (i, 0)) for k in want]
    out_shape = [jax.ShapeDtypeStruct((t, _view(rows[k])[2]), row_grads[k]) for k in want]
    out_specs += [pl.BlockSpec((1, 1, b.shape[2]), lambda i: (i // tps, 0, 0)) for b in bps]
    out_shape += [jax.ShapeDtypeStruct(b.shape, F32) for b in bps]
    out_specs += [pl.BlockSpec(g.shape, lambda i, n=g.ndim: (0,) * n) for g in gps]
    out_shape += [jax.ShapeDtypeStruct(g.shape, F32) for g in gps]
    res = pl.pallas_call(
        body, name=name, grid=(t // tm,), in_specs=in_specs, out_specs=out_specs, out_shape=out_shape,
        compiler_params=_cp("arbitrary"),
    )(*[_view(r)[0] for r in rows], *bps, *gps, *[_view(d)[0] for d in douts], *[add_rows[k] for k in adds])
    nw = len(want)
    return res[:nw], res[nw:nw + nb], res[nw + nb:]


def mm_nn(name, xs, ws, tm, tn, out_dtype=F32, ride=None):
    views = [_view(x) for x in xs]
    t, n, k = views[0][0].shape[0], ws[0].shape[1], len(xs)

    def body(*refs):
        acc = _dg(refs[0][...], refs[k][...], 1, 0)
        for i in range(1, k):
            acc = acc + _dg(refs[i][...], refs[k + i][...], 1, 0)
        refs[2 * k][...] = acc.astype(out_dtype)

    in_specs = [_col_spec(v, tm, v[2], lambda i, j: (i, 0)) for v in views]
    in_specs += [pl.BlockSpec((w.shape[0], tn), lambda i, j: (0, j)) for w in ws]
    out_spec = pl.BlockSpec((tm, tn), lambda i, j: (i, j))
    out_shape = jax.ShapeDtypeStruct((t, n), out_dtype)
    if ride is not None:
        (res,), landed = hosted_call(body, name=name, grid=(t // tm, n // tn), in_specs=in_specs, out_specs=[out_spec],
                                     out_shape=[out_shape], args=(*[v[0] for v in views], *ws), ride=ride)
        return res, landed
    return pl.pallas_call(
        body, name=name, grid=(t // tm, n // tn), in_specs=in_specs, out_specs=out_spec, out_shape=out_shape,
        compiler_params=_cp("parallel", "parallel"),
    )(*[v[0] for v in views], *ws)


def mm_nt(name, dys, ws, tm, tk, out_dtype=F32, ride=None):
    views = [_view(d) for d in dys]
    t, kk, k = views[0][0].shape[0], ws[0].shape[0], len(dys)

    def body(*refs):
        acc = _dg(refs[0][...], refs[k][...], 1, 1)
        for i in range(1, k):
            acc = acc + _dg(refs[i][...], refs[k + i][...], 1, 1)
        refs[2 * k][...] = acc.astype(out_dtype)

    in_specs = [_col_spec(v, tm, v[2], lambda i, j: (i, 0)) for v in views]
    in_specs += [pl.BlockSpec((tk, w.shape[1]), lambda i, j: (j, 0)) for w in ws]
    out_spec = pl.BlockSpec((tm, tk), lambda i, j: (i, j))
    out_shape = jax.ShapeDtypeStruct((t, kk), out_dtype)
    if ride is not None:
        (res,), landed = hosted_call(body, name=name, grid=(t // tm, kk // tk), in_specs=in_specs, out_specs=[out_spec],
                                     out_shape=[out_shape], args=(*[v[0] for v in views], *ws), ride=ride)
        return res, landed
    return pl.pallas_call(
        body, name=name, grid=(t // tm, kk // tk), in_specs=in_specs, out_specs=out_spec, out_shape=out_shape,
        compiler_params=_cp("parallel", "parallel"),
    )(*[v[0] for v in views], *ws)


def modulate_proj(name, x, sc, sh, w, seq, tm, tk):
    t, kdim = x.shape
    n = w.shape[0]
    tps = seq // tm

    def body(x_ref, sc_ref, sh_ref, w_ref, o_ref, h_ref):
        h = (x_ref[...] * (1.0 + sc_ref[0]) + sh_ref[0]).astype(BF16)
        h_ref[...] = h
        o_ref[...] = _dg(h, w_ref[...], 1, 1)

    per_seq = pl.BlockSpec((1, 1, kdim), lambda i, j: (i // tps, 0, 0))
    return pl.pallas_call(
        body, name=name, grid=(t // tm, n // tk),
        in_specs=[pl.BlockSpec((tm, kdim), lambda i, j: (i, 0)), per_seq, per_seq,
                  pl.BlockSpec((tk, kdim), lambda i, j: (j, 0))],
        out_specs=[pl.BlockSpec((tm, tk), lambda i, j: (i, j)), pl.BlockSpec((tm, kdim), lambda i, j: (i, 0))],
        out_shape=[jax.ShapeDtypeStruct((t, n), F32), jax.ShapeDtypeStruct((t, kdim), BF16)],
        compiler_params=_cp("arbitrary", "arbitrary"),
    )(x, sc, sh, w)


def mm_tn(name, x, dy, tk, tn, tt, out_dtype=F32, ride=None):
    xv, dv = _view(x), _view(dy)
    t, kk, n = xv[0].shape[0], xv[2], dv[2]
    steps = t // tt

    def body(x_ref, d_ref, o_ref, acc_ref):
        @pl.when(pl.program_id(2) == 0)
        def _():
            acc_ref[...] = jnp.zeros_like(acc_ref)

        acc_ref[...] += _dg(x_ref[...], d_ref[...], 0, 0)

        @pl.when(pl.program_id(2) == steps - 1)
        def _():
            o_ref[...] = acc_ref[...].astype(out_dtype)

    in_specs = [_col_spec(xv, tt, tk, lambda a, b, c: (c, a)), _col_spec(dv, tt, tn, lambda a, b, c: (c, b))]
    out_spec = pl.BlockSpec((tk, tn), lambda a, b, c: (a, b))
    out_shape = jax.ShapeDtypeStruct((kk, n), out_dtype)
    if ride is not None:
        (res,), landed = hosted_call(body, name=name, grid=(kk // tk, n // tn, steps), in_specs=in_specs,
                                     out_specs=[out_spec], out_shape=[out_shape], scratch=[pltpu.VMEM((tk, tn), F32)],
                                     args=(xv[0], dv[0]), ride=ride)
        return res, landed
    return pl.pallas_call(
        body, name=name, grid=(kk // tk, n // tn, steps), in_specs=in_specs, out_specs=out_spec, out_shape=out_shape,
        scratch_shapes=[pltpu.VMEM((tk, tn), F32)],
        compiler_params=_cp("parallel", "parallel", "arbitrary"),
    )(xv[0], dv[0])


def _silu(x):
    return x * jax.nn.sigmoid(x)


def f_modulate(x, sc, sh):
    return (x * (1.0 + sc) + sh,)


def _res_ln(coef):
    def f(x, y, g, lg, lb):
        r = ALPHA * x + (coef * g) * y
        mu = jnp.mean(r, axis=-1, keepdims=True)
        d = r - mu
        var = jnp.mean(d * d, axis=-1, keepdims=True)
        return (d * lax.rsqrt(var + LN_EPS) * lg + lb,)
    return f


def f_glu(y, w, b):
    g = jax.nn.gelu(y)
    return (g * jax.nn.sigmoid(bdot_nn(g, w) + b),)


def _shift_down(x, halo, k):
    if k == 0:
        return x
    r = pltpu.roll(x, k, 0)
    hr = pltpu.roll(halo, k, 0)
    row = lax.broadcasted_iota(jnp.int32, (HALO, 1), 0)
    top = jnp.where(row < k, hr, r[:HALO])
    return jnp.concatenate([top, r[HALO:]], axis=0)


def _shift_up(x, halo, k):
    if k == 0:
        return x
    n = x.shape[0]
    r = pltpu.roll(x, n - k, 0)
    hr = pltpu.roll(halo, HALO - k, 0)
    row = lax.broadcasted_iota(jnp.int32, (HALO, 1), 0)
    bot = jnp.where(row >= HALO - k, hr, r[n - HALO:])
    return jnp.concatenate([r[:n - HALO], bot], axis=0)


def _conv_pre(x, halo, w, b):
    acc = x * w[CONV_K - 1:CONV_K, :] + b
    for k in range(1, CONV_K):
        acc = acc + _shift_down(x, halo, k) * w[CONV_K - 1 - k:CONV_K - k, :]
    return acc


def _rows_before(width, tm):
    return pl.BlockSpec((HALO, width), lambda i: (jnp.maximum(i * (tm // HALO) - 1, 0), 0))


def conv_fwd(proj, w, b, seq, tm):
    t = proj.shape[0]
    tps = seq // tm

    def body(x_ref, h_ref, w_ref, b_ref, o_ref):
        first = (pl.program_id(0) % tps == 0)
        halo = jnp.where(first, 0.0, h_ref[...])
        o_ref[...] = _silu(_conv_pre(x_ref[...], halo, w_ref[...], b_ref[...]))

    return pl.pallas_call(
        body, name="conv_fwd", grid=(t // tm,),
        in_specs=[pl.BlockSpec((tm, CONV_CH), lambda i: (i, 0)), _rows_before(CONV_CH, tm),
                  pl.BlockSpec((CONV_K, CONV_CH), lambda i: (0, 0)), pl.BlockSpec((1, CONV_CH), lambda i: (0, 0))],
        out_specs=pl.BlockSpec((tm, CONV_CH), lambda i: (i, 0)),
        out_shape=jax.ShapeDtypeStruct((t, CONV_CH), F32),
        compiler_params=_cp("arbitrary"),
    )(proj, proj, w, b)


def conv_bwd_pre(proj, w, b, dxs, dbm, dcm, seq, tm):
    t = proj.shape[0]
    tps = seq // tm

    def body(x_ref, h_ref, w_ref, b_ref, d1, d2, d3, dp_ref, dw_ref, db_ref):
        i = pl.program_id(0)
        halo = jnp.where(i % tps == 0, 0.0, h_ref[...])
        x = x_ref[...]
        pre = _conv_pre(x, halo, w_ref[...], b_ref[...])
        sg = jax.nn.sigmoid(pre)
        dout = jnp.concatenate([d1[...], d2[...], d3[...]], axis=1)
        dp = dout * (sg * (1.0 + pre * (1.0 - sg)))
        dp_ref[...] = dp

        @pl.when(i == 0)
        def _():
            dw_ref[...] = jnp.zeros_like(dw_ref)
            db_ref[...] = jnp.zeros_like(db_ref)

        db_ref[...] += jnp.sum(dp, axis=0, keepdims=True)
        for k in range(CONV_K):
            j = CONV_K - 1 - k
            dw_ref[j:j + 1, :] += jnp.sum(dp * _shift_down(x, halo, k), axis=0, keepdims=True)

    return pl.pallas_call(
        body, name="conv_bwd_pre", grid=(t // tm,),
        in_specs=[pl.BlockSpec((tm, CONV_CH), lambda i: (i, 0)), _rows_before(CONV_CH, tm),
                  pl.BlockSpec((CONV_K, CONV_CH), lambda i: (0, 0)), pl.BlockSpec((1, CONV_CH), lambda i: (0, 0)),
                  pl.BlockSpec((tm, 512), lambda i: (i, 0)), pl.BlockSpec((tm, 256), lambda i: (i, 0)),
                  pl.BlockSpec((tm, 256), lambda i: (i, 0))],
        out_specs=[pl.BlockSpec((tm, CONV_CH), lambda i: (i, 0)), pl.BlockSpec((CONV_K, CONV_CH), lambda i: (0, 0)),
                   pl.BlockSpec((1, CONV_CH), lambda i: (0, 0))],
        out_shape=[jax.ShapeDtypeStruct((t, CONV_CH), F32), jax.ShapeDtypeStruct((CONV_K, CONV_CH), F32),
                   jax.ShapeDtypeStruct((1, CONV_CH), F32)],
        compiler_params=_cp("arbitrary"),
    )(proj, proj, w, b, dxs, dbm, dcm)


def conv_bwd_x(dpre, w, seq, tm):
    t = dpre.shape[0]
    tps = seq // tm
    blocks = tm // HALO
    last = t // HALO - 1

    def body(d_ref, h_ref, w_ref, o_ref):
        halo = jnp.where(pl.program_id(0) % tps == tps - 1, 0.0, h_ref[...])
        d = d_ref[...]
        w = w_ref[...]
        acc = d * w[CONV_K - 1:CONV_K, :]
        for k in range(1, CONV_K):
            acc = acc + _shift_up(d, halo, k) * w[CONV_K - 1 - k:CONV_K - k, :]
        o_ref[...] = acc

    return pl.pallas_call(
        body, name="conv_bwd_x", grid=(t // tm,),
        in_specs=[pl.BlockSpec((tm, CONV_CH), lambda i: (i, 0)),
                  pl.BlockSpec((HALO, CONV_CH), lambda i: (jnp.minimum((i + 1) * blocks, last), 0)),
                  pl.BlockSpec((CONV_K, CONV_CH), lambda i: (0, 0))],
        out_specs=pl.BlockSpec((tm, CONV_CH), lambda i: (i, 0)),
        out_shape=jax.ShapeDtypeStruct((t, CONV_CH), F32),
        compiler_params=_cp("arbitrary"),
    )(dpre, dpre, w)


def _softplus(x):
    return jnp.maximum(x, 0.0) + jnp.log1p(jnp.exp(-jnp.abs(x)))


def _ssd_chunk(xs, bg, cg, dtr, zz, hp, dtb, alog, dcol, nw):
    l = xs.shape[0]
    row = lax.broadcasted_iota(jnp.int32, (l, l), 0)
    col = lax.broadcasted_iota(jnp.int32, (l, l), 1)
    causal = row >= col
    tril = causal.astype(F32)
    expand = (lax.broadcasted_iota(jnp.int32, (LANE, GROUP_COLS), 1) // SSD_HEAD_DIM
              == lax.broadcasted_iota(jnp.int32, (LANE, GROUP_COLS), 0)).astype(F32)
    head_of_col = lax.broadcasted_iota(jnp.int32, (1, GROUP_COLS), 1) // SSD_HEAD_DIM
    last_row = (lax.broadcasted_iota(jnp.int32, (l, 1), 0) == l - 1).astype(F32)

    dtc = _softplus(dtr + dtb)
    a_c = dtc * (-jnp.exp(alog))
    acs_c = mask_dot_left(tril, a_c)
    dt_e = mask_dot_right(dtc, expand)
    acs_e = mask_dot_right(acs_c, expand)
    alast_e = jnp.sum(acs_e * last_row, axis=0, keepdims=True)
    x = xs * dt_e
    states = bdot_tn(bg, x * jnp.exp(alast_e - acs_e))
    h_next = jnp.exp(alast_e) * hp + states
    d_e = jnp.sum(dcol * expand, axis=0, keepdims=True)
    y = bdot_nn(cg, hp) * jnp.exp(acs_e) + d_e * xs
    cb = bdot_nt(cg, bg)
    acs_t = acs_c.T
    for z in range(HEADS_PER_GROUP):
        seg = _take_col(z)(acs_c) - _take_row(z)(acs_t)
        lmat = jnp.exp(jnp.where(causal, seg, -1e30))
        y = y + bdot_nn(cb * lmat, x * (head_of_col == z).astype(F32))
    yz = y * _silu(zz)
    ms = jnp.mean(yz * yz, axis=-1, keepdims=True)
    return yz * lax.rsqrt(ms + LN_EPS) * nw, h_next


SSD_SUB = 4
SSD_ROWS = SSD_SUB * SSD_CHUNK


def _ssd_in_specs(steps, rev):
    def tok(b, c):
        return b * steps + (steps - 1 - c if rev else c)

    whole = lambda *shape: pl.BlockSpec(shape, lambda b, c: (0,) * len(shape))
    both = SSD_GROUPS * SSD_STATE
    return [
        pl.BlockSpec((SSD_ROWS, SSD_WIDTH), lambda b, c: (tok(b, c), 0)),
        pl.BlockSpec((SSD_ROWS, both), lambda b, c: (tok(b, c), SSD_WIDTH // both)),
        pl.BlockSpec((SSD_ROWS, both), lambda b, c: (tok(b, c), SSD_WIDTH // both + 1)),
        pl.BlockSpec((SSD_ROWS, SSD_GROUPS * LANE), lambda b, c: (tok(b, c), P_DT // (SSD_GROUPS * LANE))),
        pl.BlockSpec((SSD_ROWS, SSD_WIDTH), lambda b, c: (tok(b, c), P_Z // SSD_WIDTH)),
        whole(SSD_GROUPS, 1, LANE), whole(SSD_GROUPS, 1, LANE), whole(SSD_GROUPS, LANE, 1),
        whole(SSD_GROUPS, 1, GROUP_COLS),
    ], tok


def _piece(ref, s, g, width):
    return ref[s * SSD_CHUNK:(s + 1) * SSD_CHUNK, g * width:(g + 1) * width]


def ssd_fwd(xc, proj, dtb, alog, dcol, nw, bsz, seq, ride=None):
    t = xc.shape[0]
    nc = seq // SSD_CHUNK
    steps = nc // SSD_SUB
    in_specs, tok = _ssd_in_specs(steps, False)

    def body(xs, bm, cm, dtr, zz, dtb_r, alog_r, dcol_r, nw_r, y_ref, hp_ref, h_scr):
        @pl.when(pl.program_id(1) == 0)
        def _():
            h_scr[...] = jnp.zeros_like(h_scr)

        for g in range(SSD_GROUPS):
            h = h_scr[g]
            for s in range(SSD_SUB):
                hp_ref[g, 0, s] = h
                y, h = _ssd_chunk(_piece(xs, s, g, GROUP_COLS), _piece(bm, s, g, SSD_STATE),
                                  _piece(cm, s, g, SSD_STATE), _piece(dtr, s, g, LANE), _piece(zz, s, g, GROUP_COLS), h,
                                  dtb_r[g], alog_r[g], dcol_r[g], nw_r[g])
                y_ref[s * SSD_CHUNK:(s + 1) * SSD_CHUNK, g * GROUP_COLS:(g + 1) * GROUP_COLS] = y
            h_scr[g] = h

    return hosted_call(
        body, name="ssd_fwd", grid=(bsz, steps), in_specs=in_specs,
        out_specs=[pl.BlockSpec((SSD_ROWS, SSD_WIDTH), lambda b, c: (tok(b, c), 0)),
                   pl.BlockSpec((SSD_GROUPS, 1, SSD_SUB, SSD_STATE, GROUP_COLS), lambda b, c: (0, b, c, 0, 0))],
        out_shape=[jax.ShapeDtypeStruct((t, SSD_WIDTH), F32),
                   jax.ShapeDtypeStruct((SSD_GROUPS, bsz, nc, SSD_STATE, GROUP_COLS), F32)],
        scratch=[pltpu.VMEM((SSD_GROUPS, SSD_STATE, GROUP_COLS), F32)],
        args=(xc, xc, xc, proj, proj, dtb, alog, dcol, nw), ride=ride)


def ssd_bwd(xc, proj, dtb, alog, dcol, nw, hprev, dy, bsz, seq):
    t = xc.shape[0]
    nc = seq // SSD_CHUNK
    steps = nc // SSD_SUB
    in_specs, tok = _ssd_in_specs(steps, True)
    in_specs += [pl.BlockSpec((SSD_GROUPS, 1, SSD_SUB, SSD_STATE, GROUP_COLS), lambda b, c: (0, b, steps - 1 - c, 0, 0)),
                 pl.BlockSpec((SSD_ROWS, SSD_WIDTH), lambda b, c: (tok(b, c), 0))]

    def body(xs, bm, cm, dtr, zz, dtb_r, alog_r, dcol_r, nw_r, hp_ref, dy_ref,
             dxs, dbm, dcm, ddt, dzz, ddtb, dalog, ddcol, dnw, dh_scr):
        b, c = pl.program_id(0), pl.program_id(1)

        @pl.when(c == 0)
        def _():
            dh_scr[...] = jnp.zeros_like(dh_scr)

        @pl.when((b == 0) & (c == 0))
        def _():
            for r in (ddtb, dalog, ddcol, dnw):
                r[...] = jnp.zeros_like(r)

        for g in range(SSD_GROUPS):
            wide = slice(g * GROUP_COLS, (g + 1) * GROUP_COLS)
            state = slice(g * SSD_STATE, (g + 1) * SSD_STATE)
            dh = dh_scr[g]
            for s in reversed(range(SSD_SUB)):
                rows = slice(s * SSD_CHUNK, (s + 1) * SSD_CHUNK)
                _, pull = jax.vjp(_ssd_chunk, xs[rows, wide], bm[rows, state], cm[rows, state],
                                  _piece(dtr, s, g, LANE), zz[rows, wide], hp_ref[g, 0, s],
                                  dtb_r[g], alog_r[g], dcol_r[g], nw_r[g])
                d = pull((dy_ref[rows, wide], dh))
                dxs[rows, wide], dbm[rows, state], dcm[rows, state], dzz[rows, wide] = d[0], d[1], d[2], d[4]
                ddt[rows, g * LANE:(g + 1) * LANE] = d[3]
                dh = d[5]
                ddtb[g] += d[6]
                dalog[g] += d[7]
                ddcol[g] += d[8]
                dnw[g] += d[9]
            dh_scr[g] = dh

    def tile(w):
        return pl.BlockSpec((SSD_ROWS, w), lambda b, c: (tok(b, c), 0))

    whole = lambda *shape: pl.BlockSpec(shape, lambda b, c: (0,) * len(shape))
    return pl.pallas_call(
        body, name="ssd_bwd", grid=(bsz, steps), in_specs=in_specs,
        out_specs=[tile(SSD_WIDTH), tile(2 * SSD_STATE), tile(2 * SSD_STATE), tile(2 * LANE), tile(SSD_WIDTH),
                   whole(SSD_GROUPS, 1, LANE), whole(SSD_GROUPS, 1, LANE), whole(SSD_GROUPS, LANE, 1),
                   whole(SSD_GROUPS, 1, GROUP_COLS)],
        out_shape=[jax.ShapeDtypeStruct((t, SSD_WIDTH), F32), jax.ShapeDtypeStruct((t, 2 * SSD_STATE), F32),
                   jax.ShapeDtypeStruct((t, 2 * SSD_STATE), F32), jax.ShapeDtypeStruct((t, 2 * LANE), F32),
                   jax.ShapeDtypeStruct((t, SSD_WIDTH), F32),
                   jax.ShapeDtypeStruct((SSD_GROUPS, 1, LANE), F32), jax.ShapeDtypeStruct((SSD_GROUPS, 1, LANE), F32),
                   jax.ShapeDtypeStruct((SSD_GROUPS, LANE, 1), F32),
                   jax.ShapeDtypeStruct((SSD_GROUPS, 1, GROUP_COLS), F32)],
        scratch_shapes=[pltpu.VMEM((SSD_GROUPS, SSD_STATE, GROUP_COLS), F32)],
        compiler_params=_cp("arbitrary", "arbitrary"),
    )(xc, xc, xc, proj, proj, dtb, alog, dcol, nw, hprev, dy)


def _disc_a(a_re, a_im, log_dt):
    dt = jnp.exp(log_dt)
    mag = jnp.exp(dt * a_re)
    ab_re, ab_im = mag * jnp.cos(dt * a_im), mag * jnp.sin(dt * a_im)
    den = a_re * a_re + a_im * a_im
    nr, ni = ab_re - 1.0, ab_im
    f_re, f_im = (nr * a_re + ni * a_im) / den, (ni * a_re - nr * a_im) / den
    return ab_re, ab_im, f_re, f_im


def _disc_b(f_re, f_im, b_re, b_im):
    return f_re * b_re - f_im * b_im, f_re * b_im + f_im * b_re


def _whole(f, name, args, outs):
    def body(*refs):
        res = f(*[r[...] for r in refs[:len(args)]])
        for o, v in zip(refs[len(args):], res):
            o[...] = v

    return pl.pallas_call(body, name=name, out_shape=[jax.ShapeDtypeStruct(s, F32) for s in outs])(*args)


def _whole_vjp(f, name, args, cts):
    def body(*refs):
        vals = [r[...] for r in refs[:len(args)]]
        _, pull = jax.vjp(f, *vals)
        res = pull(tuple(r[...] for r in refs[len(args):len(args) + len(cts)]))
        for o, v in zip(refs[len(args) + len(cts):], res):
            o[...] = v

    return pl.pallas_call(body, name=name, out_shape=[jax.ShapeDtypeStruct(a.shape, F32) for a in args])(*args, *cts)


S5_SUB = 8
S5_STEPS = 3


def s5_tables(lam_re, lam_im):
    rows = S5_STEPS * S5_SUB

    def body(lr_ref, li_ref, sf_re, sf_im, sb_re, sb_im, cf_re, cf_im, cb_re, cb_im):
        lr, li = lr_ref[...], li_ref[...]

        def power(k):
            m = jnp.exp(k * lr)
            return m * jnp.cos(k * li), m * jnp.sin(k * li)

        srow = lax.broadcasted_iota(jnp.int32, (rows, 1), 0)
        k = jnp.left_shift(1, srow // S5_SUB)
        tt = srow % S5_SUB
        pr, pi = power(k.astype(F32))
        fwd, bwd = tt >= k, tt < S5_SUB - k
        sf_re[...], sf_im[...] = jnp.where(fwd, pr, 0.0), jnp.where(fwd, pi, 0.0)
        sb_re[...], sb_im[...] = jnp.where(bwd, pr, 0.0), jnp.where(bwd, pi, 0.0)
        trow = lax.broadcasted_iota(jnp.int32, (S5_SUB, 1), 0)
        cf_re[...], cf_im[...] = power((trow + 1).astype(F32))
        cb_re[...], cb_im[...] = power((S5_SUB - trow).astype(F32))

    shp = [jax.ShapeDtypeStruct((rows, S5_COLS), F32)] * 4 + [jax.ShapeDtypeStruct((S5_SUB, S5_COLS), F32)] * 4
    return pl.pallas_call(body, name="s5_tables", out_shape=shp)(lam_re, lam_im)


def _s5_coefs(steps_re, steps_im, carry_re, carry_im, reverse):
    sign = -1.0 if reverse else 1.0
    steps = [(steps_re[s * S5_SUB:(s + 1) * S5_SUB, :], sign * steps_im[s * S5_SUB:(s + 1) * S5_SUB, :])
             for s in range(S5_STEPS)]
    return steps, (carry_re[...], sign * carry_im[...])


def _s5_block_scan(ar, ai, coefs, cr, ci, reverse):
    steps, (qr, qi) = coefs
    for s, (pr, pi) in enumerate(steps):
        shift = S5_SUB - (1 << s) if reverse else (1 << s)
        sr, si = pltpu.roll(ar, shift, 0), pltpu.roll(ai, shift, 0)
        ar, ai = ar + pr * sr - pi * si, ai + pr * si + pi * sr
    br, bi = jnp.broadcast_to(cr, ar.shape), jnp.broadcast_to(ci, ai.shape)
    return ar + qr * br - qi * bi, ai + qr * bi + qi * br


def _s5_specs(n5, rev):
    def tok(q, b, c):
        return b * n5 + (n5 - 1 - c if rev else c)

    qcols = S5_COLS // S5_Q
    specs = [
        pl.BlockSpec((S5_CHUNK, LANE), lambda q, b, c: (tok(q, b, c), P_U // LANE + q)),
        pl.BlockSpec((1, LANE, qcols), lambda q, b, c: (q, 0, 0)),
        pl.BlockSpec((1, LANE, qcols), lambda q, b, c: (q, 0, 0)),
        pl.BlockSpec((1, qcols, LANE), lambda q, b, c: (q, 0, 0)),
        pl.BlockSpec((1, qcols, LANE), lambda q, b, c: (q, 0, 0)),
        pl.BlockSpec((S5_STEPS * S5_SUB, qcols), lambda q, b, c: (0, q)),
        pl.BlockSpec((S5_STEPS * S5_SUB, qcols), lambda q, b, c: (0, q)),
        pl.BlockSpec((S5_SUB, qcols), lambda q, b, c: (0, q)),
        pl.BlockSpec((S5_SUB, qcols), lambda q, b, c: (0, q)),
        pl.BlockSpec((1, 1, LANE), lambda q, b, c: (q, 0, 0)),
    ]
    return specs, tok, qcols


def s5_fwd(proj, wb_re, wb_im, wc_re, wc_im, sf_re, sf_im, cf_re, cf_im, dvec, bsz, seq, ride=None):
    t = proj.shape[0]
    n5 = seq // S5_CHUNK
    in_specs, tok, qcols = _s5_specs(n5, False)

    def body(u_ref, wbr, wbi, wcr, wci, sfr, sfi, cfr, cfi, d_ref, y_ref, xr_ref, xi_ref, cr_scr, ci_scr):
        @pl.when(pl.program_id(2) == 0)
        def _():
            cr_scr[...] = jnp.zeros_like(cr_scr)
            ci_scr[...] = jnp.zeros_like(ci_scr)

        u = u_ref[...]
        bur, bui = _dg(u, wbr[0], 1, 0), _dg(u, wbi[0], 1, 0)
        coefs = _s5_coefs(sfr, sfi, cfr, cfi, False)
        cr, ci = cr_scr[...], ci_scr[...]
        for r in range(S5_CHUNK // S5_SUB):
            rows = slice(r * S5_SUB, (r + 1) * S5_SUB)
            xr, xi = _s5_block_scan(bur[rows], bui[rows], coefs, cr, ci, False)
            xr_ref[rows, :], xi_ref[rows, :] = xr, xi
            cr, ci = xr[S5_SUB - 1:, :], xi[S5_SUB - 1:, :]
        cr_scr[...], ci_scr[...] = cr, ci
        y_ref[...] = _dg(xr_ref[...], wcr[0], 1, 0) - _dg(xi_ref[...], wci[0], 1, 0) + u * d_ref[0]

    def tile(w):
        return pl.BlockSpec((S5_CHUNK, w), lambda q, b, c: (tok(q, b, c), q))

    return hosted_call(
        body, name="s5_fwd", grid=(S5_Q, bsz, n5), in_specs=in_specs,
        out_specs=[tile(LANE), tile(qcols), tile(qcols)],
        out_shape=[jax.ShapeDtypeStruct((t, S5_WIDTH), F32), jax.ShapeDtypeStruct((t, S5_COLS), F32),
                   jax.ShapeDtypeStruct((t, S5_COLS), F32)],
        scratch=[pltpu.VMEM((1, qcols), F32)] * 2,
        args=(proj, wb_re, wb_im, wc_re, wc_im, sf_re, sf_im, cf_re, cf_im, dvec), ride=ride)


def s5_bwd(proj, wb_re, wb_im, wc_re, wc_im, sb_re, sb_im, cb_re, cb_im, dvec, xr_all, xi_all, dy, bsz, seq,
           ride=None):
    t = proj.shape[0]
    n5 = seq // S5_CHUNK
    in_specs, tok, qcols = _s5_specs(n5, True)
    blocks = S5_CHUNK // HALO

    def prev_rows(q, b, c):
        return (jnp.maximum(tok(q, b, c) * blocks - 1, 0), q)

    in_specs += [pl.BlockSpec((S5_CHUNK, qcols), lambda q, b, c: (tok(q, b, c), q)),
                 pl.BlockSpec((S5_CHUNK, qcols), lambda q, b, c: (tok(q, b, c), q)),
                 pl.BlockSpec((HALO, qcols), prev_rows), pl.BlockSpec((HALO, qcols), prev_rows),
                 pl.BlockSpec((S5_CHUNK, LANE), lambda q, b, c: (tok(q, b, c), q))]

    def body(u_ref, wbr, wbi, wcr, wci, sbr, sbi, cbr, cbi, d_ref, xr_ref, xi_ref, pr_ref, pi_ref, dy_ref,
             du_ref, dwbr, dwbi, dwcr, dwci, dar, dai, dd_ref, gr_scr, gi_scr, gr_all, gi_all):
        b, c = pl.program_id(1), pl.program_id(2)

        @pl.when(c == 0)
        def _():
            gr_scr[...] = jnp.zeros_like(gr_scr)
            gi_scr[...] = jnp.zeros_like(gi_scr)

        @pl.when((b == 0) & (c == 0))
        def _():
            for r in (dwbr, dwbi, dwcr, dwci, dar, dai, dd_ref):
                r[...] = jnp.zeros_like(r)

        u, dy_v = u_ref[...], dy_ref[...]
        g0r, g0i = _dg(dy_v, wcr[0], 1, 1), -_dg(dy_v, wci[0], 1, 1)
        coefs = _s5_coefs(sbr, sbi, cbr, cbi, True)
        cr, ci = gr_scr[...], gi_scr[...]
        for r in reversed(range(S5_CHUNK // S5_SUB)):
            rows = slice(r * S5_SUB, (r + 1) * S5_SUB)
            br, bi = _s5_block_scan(g0r[rows], g0i[rows], coefs, cr, ci, True)
            gr_all[rows, :], gi_all[rows, :] = br, bi
            cr, ci = br[:1, :], bi[:1, :]
        gr_scr[...], gi_scr[...] = cr, ci
        gr, gi = gr_all[...], gi_all[...]

        row = lax.broadcasted_iota(jnp.int32, (S5_CHUNK, 1), 0)
        xr, xi = xr_ref[...], xi_ref[...]
        is_first = (c == n5 - 1)
        hr = jnp.where(is_first, 0.0, pr_ref[...][HALO - 1:, :])
        hi = jnp.where(is_first, 0.0, pi_ref[...][HALO - 1:, :])
        xpr = jnp.where(row >= 1, pltpu.roll(xr, 1, 0), hr)
        xpi = jnp.where(row >= 1, pltpu.roll(xi, 1, 0), hi)
        dar[0] += jnp.sum(xpr * gr + xpi * gi, axis=0, keepdims=True)
        dai[0] += jnp.sum(xpr * gi - xpi * gr, axis=0, keepdims=True)
        du_ref[...] = _dg(gr, wbr[0], 1, 1) + _dg(gi, wbi[0], 1, 1) + dy_v * d_ref[0]
        dwbr[0] += _dg(u, gr, 0, 0)
        dwbi[0] += _dg(u, gi, 0, 0)
        dwcr[0] += _dg(xr, dy_v, 0, 0)
        dwci[0] -= _dg(xi, dy_v, 0, 0)
        dd_ref[0] += jnp.sum(dy_v * u, axis=0, keepdims=True)

    def acc(shape):
        return pl.BlockSpec((1,) + shape, lambda q, b, c: (q, 0, 0))

    return hosted_call(
        body, name="s5_bwd", grid=(S5_Q, bsz, n5), in_specs=in_specs,
        out_specs=[pl.BlockSpec((S5_CHUNK, LANE), lambda q, b, c: (tok(q, b, c), q)),
                   acc((LANE, qcols)), acc((LANE, qcols)), acc((qcols, LANE)), acc((qcols, LANE)),
                   acc((1, qcols)), acc((1, qcols)), acc((1, LANE))],
        out_shape=[jax.ShapeDtypeStruct((t, S5_WIDTH), F32),
                   jax.ShapeDtypeStruct((S5_Q, LANE, qcols), F32), jax.ShapeDtypeStruct((S5_Q, LANE, qcols), F32),
                   jax.ShapeDtypeStruct((S5_Q, qcols, LANE), F32), jax.ShapeDtypeStruct((S5_Q, qcols, LANE), F32),
                   jax.ShapeDtypeStruct((S5_Q, 1, qcols), F32), jax.ShapeDtypeStruct((S5_Q, 1, qcols), F32),
                   jax.ShapeDtypeStruct((S5_Q, 1, LANE), F32)],
        scratch=[pltpu.VMEM((1, qcols), F32)] * 2 + [pltpu.VMEM((S5_CHUNK, qcols), F32)] * 2,
        args=(proj, wb_re, wb_im, wc_re, wc_im, sb_re, sb_im, cb_re, cb_im, dvec, xr_all, xi_all, xr_all, xi_all, dy),
        ride=ride)


def _blockdiag_b(bb):
    b4 = bb.reshape(S5_Q, 8, S5_STATE, S5_GROUP_CH)
    eye = jnp.eye(8, dtype=bb.dtype)
    w = jnp.einsum("qgph,gk->qghkp", b4, eye)
    return w.reshape(S5_Q, LANE, S5_COLS // S5_Q)


def _unblock_b(dw):
    d = dw.reshape(S5_Q, 8, S5_GROUP_CH, 8, S5_STATE)
    d = jnp.einsum("qghgp->qgph", d)
    return d.reshape(S5_COLS, S5_GROUP_CH)


def _blockdiag_c(cc):
    c4 = cc.reshape(S5_Q, 8, S5_GROUP_CH, S5_STATE)
    eye = jnp.eye(8, dtype=cc.dtype)
    w = jnp.einsum("qghp,gk->qgpkh", c4, eye)
    return w.reshape(S5_Q, S5_COLS // S5_Q, LANE)


def _unblock_c(dw):
    d = dw.reshape(S5_Q, 8, S5_STATE, 8, S5_GROUP_CH)
    d = jnp.einsum("qgpgh->qghp", d)
    return d.reshape(S5_GROUPS, S5_GROUP_CH, S5_STATE)


def ada_fwd(c_all, w_loc, b_loc):
    def body(c_ref, w_ref, b_ref, o_ref):
        o_ref[...] = _dg(_silu(c_ref[...]), w_ref[...], 1, 0) + b_ref[...]

    return pl.pallas_call(body, name="ada_fwd",
                          out_shape=jax.ShapeDtypeStruct((c_all.shape[0], w_loc.shape[1]), F32),
                          compiler_params=_cp())(c_all, w_loc, b_loc)


def ada_bwd(c_all, dmod_all, dmod_cols):
    def body(c_ref, da_ref, dc_ref, gb_ref, gw_ref):
        gb_ref[...] = jnp.sum(da_ref[...], axis=0, keepdims=True)
        gw_ref[...] = _dg(_silu(c_ref[...]), dc_ref[...], 0, 0)

    return pl.pallas_call(body, name="ada_bwd",
                          out_shape=[jax.ShapeDtypeStruct((1, dmod_all.shape[1]), F32),
                                     jax.ShapeDtypeStruct((c_all.shape[1], dmod_cols.shape[1]), F32)],
                          compiler_params=_cp())(c_all, dmod_all, dmod_cols)


_FLIPS = [(0, 0, 1), (1, 0, 0), (0, 1, 0), (1, 1, 0), (1, 0, 1), (0, 1, 1), (1, 1, 1)]


def _exchange_ops(srcs, outs, sems, gather):
    n = len(srcs)
    send_sems, recv_sems, loc_sems = sems
    x, y, c = lax.axis_index("x"), lax.axis_index("y"), lax.axis_index("c")
    me = 4 * x + 2 * y + c
    peers = []
    for fx, fy, fc in _FLIPS:
        px, py, pc = (1 - x if fx else x), (1 - y if fy else y), (1 - c if fc else c)
        peers.append(((px, py, pc), 4 * px + 2 * py + pc))

    def copy(k, j, slot_src, slot_dst):
        src = srcs[k] if gather[k] else srcs[k].at[slot_src]
        return pltpu.make_async_remote_copy(src_ref=src, dst_ref=outs[k].at[slot_dst],
                                            send_sem=send_sems.at[k, j], recv_sem=recv_sems.at[k, j],
                                            device_id=peers[j][0], device_id_type=MESH)

    def local(k):
        own = srcs[k] if gather[k] else srcs[k].at[me]
        return pltpu.make_async_copy(own, outs[k].at[me], loc_sems.at[k])

    def start():
        for k in range(n):
            for j in range(N_DEV - 1):
                copy(k, j, peers[j][1], me).start()
            local(k).start()

    def wait():
        for k in range(n):
            for j in range(N_DEV - 1):
                copy(k, j, me, peers[j][1]).wait_recv()
        for k in range(n):
            for j in range(N_DEV - 1):
                copy(k, j, peers[j][1], me).wait_send()
            local(k).wait()

    return start, wait


def _gather_two_level(srcs, outs, sems):
    n = len(srcs)
    send_sems, recv_sems, loc_sems = sems
    x, y, c = lax.axis_index("x"), lax.axis_index("y"), lax.axis_index("c")
    slot = lambda px, py, pc: 4 * px + 2 * py + pc
    me, sibling = (x, y, c), (x, y, 1 - c)
    chips = [(1 - x, y), (x, 1 - y), (1 - x, 1 - y)]

    def copy(k, j, block, to, own=False):
        return pltpu.make_async_remote_copy(src_ref=srcs[k] if own else outs[k].at[slot(*block)],
                                            dst_ref=outs[k].at[slot(*block)],
                                            send_sem=send_sems.at[k, j], recv_sem=recv_sems.at[k, j],
                                            device_id=to, device_id_type=MESH)

    locs = [pltpu.make_async_copy(srcs[k], outs[k].at[slot(*me)], loc_sems.at[k]) for k in range(n)]
    for k in range(n):
        locs[k].start()
        copy(k, 0, me, sibling, own=True).start()
        for j, chip in enumerate(chips):
            copy(k, 1 + j, me, (*chip, c), own=True).start()
    for j, chip in enumerate(chips):
        for k in range(n):
            copy(k, 1 + j, (*chip, c), me).wait_recv()
            copy(k, 4 + j, (*chip, c), sibling).start()
    for k in range(n):
        copy(k, 0, sibling, me).wait_recv()
        for j, chip in enumerate(chips):
            copy(k, 4 + j, (*chip, 1 - c), me).wait_recv()
    for k in range(n):
        copy(k, 0, me, sibling, own=True).wait_send()
        for j, chip in enumerate(chips):
            copy(k, 1 + j, me, (*chip, c), own=True).wait_send()
            copy(k, 4 + j, (*chip, c), sibling).wait_send()
        locs[k].wait()


def gather_two_level(name, arrs):
    n = len(arrs)
    specs, shapes, sems = _exchange_parts(arrs, [True] * n)

    def body(*refs):
        _gather_two_level(refs[:n], refs[n:2 * n], refs[2 * n:])

    return pl.pallas_call(
        body, name=name, in_specs=specs, out_specs=specs, out_shape=shapes, scratch_shapes=sems,
        compiler_params=pltpu.CompilerParams(has_side_effects=True),
    )(*arrs)


def _exchange_parts(arrs, gather):
    n = len(arrs)
    any_spec = pl.BlockSpec(memory_space=pl.ANY)
    shapes = [jax.ShapeDtypeStruct(((N_DEV,) + a.shape) if g else a.shape, a.dtype) for a, g in zip(arrs, gather)]
    sems = [pltpu.SemaphoreType.DMA((n, N_DEV - 1)), pltpu.SemaphoreType.DMA((n, N_DEV - 1)),
            pltpu.SemaphoreType.DMA((n,))]
    return [any_spec] * n, shapes, sems


def exchange(name, arrs, gather):
    n = len(arrs)
    specs, shapes, sems = _exchange_parts(arrs, gather)

    def body(*refs):
        start, wait = _exchange_ops(refs[:n], refs[n:2 * n], refs[2 * n:], gather)
        start()
        wait()

    return pl.pallas_call(
        body, name=name, in_specs=specs, out_specs=specs, out_shape=shapes, scratch_shapes=sems,
        compiler_params=pltpu.CompilerParams(has_side_effects=True),
    )(*arrs)


def hosted_call(body, *, name, grid, in_specs, out_specs, out_shape, args, scratch=(), ride=None):
    sem = ("arbitrary",) * len(grid)
    if ride is None:
        res = pl.pallas_call(body, name=name, grid=grid, in_specs=in_specs, out_specs=out_specs, out_shape=out_shape,
                             scratch_shapes=list(scratch), compiler_params=_cp(*sem))(*args)
        return list(res), []
    arrs, gather = ride
    n, n_in, n_out, n_scr = len(arrs), len(in_specs), len(out_specs), len(scratch)
    specs, shapes, sems = _exchange_parts(arrs, gather)

    def both(*refs):
        ins, srcs = refs[:n_in], refs[n_in:n_in + n]
        outs, landed = refs[n_in + n:n_in + n + n_out], refs[n_in + n + n_out:n_in + 2 * n + n_out]
        scr, ex_sems = refs[n_in + 2 * n + n_out:n_in + 2 * n + n_out + n_scr], refs[n_in + 2 * n + n_out + n_scr:]
        start, wait = _exchange_ops(srcs, landed, ex_sems, gather)
        first = functools.reduce(lambda a, b: a & b, [pl.program_id(d) == 0 for d in range(len(grid))])
        last = functools.reduce(lambda a, b: a & b, [pl.program_id(d) == grid[d] - 1 for d in range(len(grid))])
        pl.when(first)(start)
        body(*ins, *outs, *scr)
        pl.when(last)(wait)

    res = pl.pallas_call(
        both, name=name, grid=grid, in_specs=list(in_specs) + specs, out_specs=list(out_specs) + specs,
        out_shape=list(out_shape) + shapes, scratch_shapes=list(scratch) + sems, compiler_params=_cp(*sem),
    )(*args, *arrs)
    return list(res[:n_out]), list(res[n_out:])


def sum_slots(name, slots, tr):
    _, r, c = slots.shape

    def body(s_ref, o_ref):
        acc = s_ref[0].astype(F32)
        for j in range(1, N_DEV):
            acc = acc + s_ref[j].astype(F32)
        o_ref[...] = acc

    return pl.pallas_call(
        body, name=name, grid=(r // tr,), in_specs=[pl.BlockSpec((N_DEV, tr, c), lambda i: (0, i, 0))],
        out_specs=pl.BlockSpec((tr, c), lambda i: (i, 0)), out_shape=jax.ShapeDtypeStruct((r, c), F32),
        compiler_params=_cp("parallel"),
    )(slots)


def adamw(name, g, w, m, v, tr):
    slots = g.ndim == 3
    r, c = w.shape
    c1, c2 = 1.0 - ADAM_B1 ** ADAM_STEP, 1.0 - ADAM_B2 ** ADAM_STEP

    def body(g_ref, w_ref, m_ref, v_ref, go, do, mo, vo):
        if slots:
            gg = g_ref[0].astype(F32)
            for j in range(1, N_DEV):
                gg = gg + g_ref[j].astype(F32)
        else:
            gg = g_ref[...]
        mn = ADAM_B1 * m_ref[...] + (1.0 - ADAM_B1) * gg
        vn = ADAM_B2 * v_ref[...] + (1.0 - ADAM_B2) * (gg * gg)
        go[...], mo[...], vo[...] = gg, mn, vn
        do[...] = -ADAM_LR * ((mn / c1) / (jnp.sqrt(vn / c2) + ADAM_EPS) + ADAM_WD * w_ref[...])

    blk = pl.BlockSpec((tr, c), lambda i: (i, 0))
    gspec = pl.BlockSpec((N_DEV, tr, c), lambda i: (0, i, 0)) if slots else blk
    return pl.pallas_call(
        body, name=name, grid=(r // tr,), in_specs=[gspec, blk, blk, blk], out_specs=[blk] * 4,
        out_shape=[jax.ShapeDtypeStruct((r, c), F32)] * 4, compiler_params=_cp("parallel"),
    )(g, w, m, v)


def _lane_rows(n):
    return -(-n // (8 * LANE)) * 8


def _pack(arrs):
    pieces = []
    for a in arrs:
        n = math.prod(a.shape)
        flat = a.reshape(-1).astype(F32)
        pieces.append(jnp.pad(flat, (0, _lane_rows(n) * LANE - n)).reshape(_lane_rows(n), LANE))
    return jnp.concatenate(pieces, axis=0)


def _unpack(buf, shapes):
    out, off = [], 0
    for s in shapes:
        n = math.prod(s)
        out.append(buf[off:off + _lane_rows(n)].reshape(-1)[:n].reshape(s))
        off += _lane_rows(n)
    return out


FF_CHUNK = D_FF
DW_TOKENS = 2048
FFN_TM = 256


def _resident(shape):
    return pl.BlockSpec(shape, lambda i: (0,) * len(shape), pipeline_mode=pl.Buffered(1))


def _ffn_fwd(tag, x, sc, sh, g, w1, w3, w2, lg, lb, seq, tm, ride=None, target=None):
    t = x.shape[0]
    tm = min(FFN_TM, tm)
    tps = seq // tm
    ln = _res_ln(0.5)
    head = target is not None

    def body(x_ref, sc_ref, sh_ref, g_ref, lg_ref, lb_ref, w1_ref, w3_ref, w2_ref, *rest):
        if head:
            t_ref, y_ref, h_ref, a_ref, b_ref, f_ref, l_ref = rest
        else:
            y_ref, h_ref, a_ref, b_ref, f_ref = rest
        xv = x_ref[...]
        h = (xv * (1.0 + sc_ref[0]) + sh_ref[0]).astype(BF16)
        h_ref[...] = h
        acc = jnp.zeros((tm, D_MODEL), F32)
        for j in range(D_FF // FF_CHUNK):
            sl = slice(j * FF_CHUNK, (j + 1) * FF_CHUNK)
            a = _dg(h, w1_ref[sl, :], 1, 1)
            b = _dg(h, w3_ref[sl, :], 1, 1)
            a_ref[:, sl] = a
            b_ref[:, sl] = b
            acc = acc + _dg(_silu(a) * b, w2_ref[sl, :], 1, 0)
        f_ref[...] = acc
        y = ln(xv, acc, g_ref[0], lg_ref[...], lb_ref[...])[0]
        if head:
            @pl.when(pl.program_id(0) == 0)
            def _():
                l_ref[...] = jnp.zeros_like(l_ref)

            e = y - t_ref[...]
            y_ref[...] = e * (1.0 / D_MODEL)
            l_ref[...] += 0.5 * jnp.sum(jnp.mean(e * e, axis=-1, keepdims=True), axis=0, keepdims=True)
        else:
            y_ref[...] = y

    row = lambda c: pl.BlockSpec((tm, c), lambda i: (i, 0))
    per_seq = pl.BlockSpec((1, 1, D_MODEL), lambda i: (i // tps, 0, 0))
    vec = pl.BlockSpec((1, D_MODEL), lambda i: (0, 0))
    res, landed = hosted_call(
        body, name=tag + "_fwd", grid=(t // tm,),
        in_specs=[row(D_MODEL), per_seq, per_seq, per_seq, vec, vec,
                  _resident((D_FF, D_MODEL)), _resident((D_FF, D_MODEL)), _resident((D_FF, D_MODEL))]
        + ([row(D_MODEL)] if head else []),
        out_specs=[row(D_MODEL), row(D_MODEL), row(D_FF), row(D_FF), row(D_MODEL)]
        + ([pl.BlockSpec((1, 1), lambda i: (0, 0))] if head else []),
        out_shape=[jax.ShapeDtypeStruct((t, D_MODEL), F32), jax.ShapeDtypeStruct((t, D_MODEL), BF16),
                   jax.ShapeDtypeStruct((t, D_FF), F32), jax.ShapeDtypeStruct((t, D_FF), F32),
                   jax.ShapeDtypeStruct((t, D_MODEL), F32)] + ([jax.ShapeDtypeStruct((1, 1), F32)] if head else []),
        args=(x, sc, sh, g, lg, lb, w1, w3, w2) + ((target,) if head else ()), ride=ride)
    first = (res[0], res[5][0, 0]) if head else res[0]
    return first, tuple(res[1:5]), landed


def _ffn_bwd(tag, dy, x, sc, sh, g, w1, w3, w2, lg, lb, res, seq, tm, ride=None, chain=None):
    h, a, b, f = res
    t = x.shape[0]
    tmk = min(FFN_TM, tm)
    tps = seq // tmk
    ln = _res_ln(0.5)

    def body(dy_ref, x_ref, f_ref, a_ref, b_ref, sc_ref, sh_ref, g_ref, lg_ref, lb_ref, w1_ref, w3_ref, w2_ref,
             dx_ref, da_ref, db_ref, s_ref, df_ref, dsc_ref, dsh_ref, dg_ref, dlg_ref, dlb_ref):
        i = pl.program_id(0)

        @pl.when(i % tps == 0)
        def _():
            for r in (dsc_ref, dsh_ref, dg_ref):
                r[...] = jnp.zeros_like(r)

        @pl.when(i == 0)
        def _():
            dlg_ref[...] = jnp.zeros_like(dlg_ref)
            dlb_ref[...] = jnp.zeros_like(dlb_ref)

        xv = x_ref[...]
        _, pull = jax.vjp(ln, xv, f_ref[...], g_ref[0], lg_ref[...], lb_ref[...])
        dx_res, df, dg, dlg, dlb = pull((dy_ref[...],))
        dfb = df.astype(BF16)
        df_ref[...] = dfb
        dh = jnp.zeros((tmk, D_MODEL), F32)
        for j in range(D_FF // FF_CHUNK):
            sl = slice(j * FF_CHUNK, (j + 1) * FF_CHUNK)
            ds = _dg(dfb, w2_ref[sl, :], 1, 1)
            av, bv = a_ref[:, sl], b_ref[:, sl]
            sg = jax.nn.sigmoid(av)
            si = av * sg
            s_ref[:, sl] = (si * bv).astype(BF16)
            da = (ds * bv * (sg * (1.0 + av * (1.0 - sg)))).astype(BF16)
            db = (ds * si).astype(BF16)
            da_ref[:, sl] = da
            db_ref[:, sl] = db
            dh = dh + _dg(da, w1_ref[sl, :], 1, 0) + _dg(db, w3_ref[sl, :], 1, 0)
        dx_ref[...] = dx_res + dh * (1.0 + sc_ref[0])
        dsc_ref[0] += jnp.sum(dh * xv, axis=0, keepdims=True)
        dsh_ref[0] += jnp.sum(dh, axis=0, keepdims=True)
        dg_ref[0] += dg
        dlg_ref[...] += dlg
        dlb_ref[...] += dlb

    row = lambda c: pl.BlockSpec((tmk, c), lambda i: (i, 0))
    per_seq = pl.BlockSpec((1, 1, D_MODEL), lambda i: (i // tps, 0, 0))
    vec = pl.BlockSpec((1, D_MODEL), lambda i: (0, 0))
    seq_shape = jax.ShapeDtypeStruct(sc.shape, F32)
    vec_shape = jax.ShapeDtypeStruct((1, D_MODEL), F32)
    (dx, da, db, s, df, dsc, dsh, dg, dlg, dlb), landed = hosted_call(
        body, name=tag + "_bwd", grid=(t // tmk,),
        in_specs=[row(D_MODEL), row(D_MODEL), row(D_MODEL), row(D_FF), row(D_FF), per_seq, per_seq, per_seq, vec, vec,
                  _resident((D_FF, D_MODEL)), _resident((D_FF, D_MODEL)), _resident((D_FF, D_MODEL))],
        out_specs=[row(D_MODEL), row(D_FF), row(D_FF), row(D_FF), row(D_MODEL), per_seq, per_seq, per_seq, vec, vec],
        out_shape=[jax.ShapeDtypeStruct((t, D_MODEL), F32), jax.ShapeDtypeStruct((t, D_FF), BF16),
                   jax.ShapeDtypeStruct((t, D_FF), BF16), jax.ShapeDtypeStruct((t, D_FF), BF16),
                   jax.ShapeDtypeStruct((t, D_MODEL), BF16), seq_shape, seq_shape, seq_shape, vec_shape, vec_shape],
        args=(dy, x, f, a, b, sc, sh, g, lg, lb, w1, w3, w2), ride=ride)
    tt = min(DW_TOKENS, seq)
    shards = lambda dw: dw.reshape(N_DEV, D_FF // N_DEV, D_MODEL)
    if chain is None:
        dw2, landed = mm_tn(tag + "_dw2", s, df, D_FF // 2, D_MODEL, tt, BF16), []
    else:
        dw2, landed = mm_tn(tag + "_dw2", s, df, D_FF // 2, D_MODEL, tt, BF16, ride=chain((dsh, dsc, dg), dlg, dlb))
    dw1, (s_w2,) = mm_tn(tag + "_dw1", da, h, D_FF // 2, D_MODEL, tt, BF16, ride=([shards(dw2)], [False]))
    dw3, (s_w1,) = mm_tn(tag + "_dw3", db, h, D_FF // 2, D_MODEL, tt, BF16, ride=([shards(dw1)], [False]))
    return dx, (dsh, dsc, dg), (s_w1, shards(dw3), s_w2, dlg, dlb), landed


def kernel(x, c, w_ada, b_ada, ffn1_w1, ffn1_w3, ffn1_w2, ln1_g, ln1_b, w_in, conv_w, conv_b, dt_bias, a_log, d_ssd, ssd_norm_w, s5_a_re, s5_a_im, s5_log_dt, s5_b_re, s5_b_im, s5_c_re, s5_c_im, s5_d, w_glu, b_glu, w_out, ln2_g, ln2_b, ffn2_w1, ffn2_w3, ffn2_w2, ln3_g, ln3_b, loss_target, m_w_ada, m_b_ada, m_ffn1_w1, m_ffn1_w3, m_ffn1_w2, m_ln1_g, m_ln1_b, m_w_in, m_conv_w, m_conv_b, m_dt_bias, m_a_log, m_d_ssd, m_ssd_norm_w, m_s5_a_re, m_s5_a_im, m_s5_log_dt, m_s5_b_re, m_s5_b_im, m_s5_c_re, m_s5_c_im, m_s5_d, m_w_glu, m_b_glu, m_w_out, m_ln2_g, m_ln2_b, m_ffn2_w1, m_ffn2_w3, m_ffn2_w2, m_ln3_g, m_ln3_b, v_w_ada, v_b_ada, v_ffn1_w1, v_ffn1_w3, v_ffn1_w2, v_ln1_g, v_ln1_b, v_w_in, v_conv_w, v_conv_b, v_dt_bias, v_a_log, v_d_ssd, v_ssd_norm_w, v_s5_a_re, v_s5_a_im, v_s5_log_dt, v_s5_b_re, v_s5_b_im, v_s5_c_re, v_s5_c_im, v_s5_d, v_w_glu, v_b_glu, v_w_out, v_ln2_g, v_ln2_b, v_ffn2_w1, v_ffn2_w3, v_ffn2_w2, v_ln3_g, v_ln3_b):
    given = dict(locals())
    bsz, seq, _ = x.shape
    t = bsz * seq
    tm = min(1024, seq)
    me = 4 * lax.axis_index("x") + 2 * lax.axis_index("y") + lax.axis_index("c")
    x0 = x.reshape(t, D_MODEL)
    target = loss_target.reshape(t, D_MODEL)

    tr16 = lambda w: w[0].T.astype(BF16)
    whole = lambda g: g.reshape(N_DEV * g.shape[1], g.shape[2])
    g_f1w1, g_f1w3, g_f1w2, g_c = gather_two_level(
        "gather_ffn1", [tr16(ffn1_w1), tr16(ffn1_w3), ffn1_w2[0].astype(BF16), c])
    f1w1, f1w3, f1w2 = whole(g_f1w1), whole(g_f1w3), whole(g_f1w2)
    c_all = whole(g_c)

    n_loc = w_ada.shape[2]
    b_loc = lax.dynamic_slice(b_ada, (0, me * n_loc), (1, n_loc))
    mod_cols = ada_fwd(c_all, w_ada[0], b_loc)
    g_mod, = exchange("gather_mod", [mod_cols], [True])
    mine = lax.dynamic_slice(g_mod, (0, me * bsz, 0), (N_DEV, bsz, n_loc))
    mod = jnp.transpose(mine, (1, 0, 2)).reshape(bsz, N_MOD, 1, D_MODEL)
    sh1, sc1, g1, sh2, sc2, g2, sh3, sc3, g3 = [mod[:, k] for k in range(N_MOD)]

    x1, res1, (g_win, g_glu, g_out, g_conv, g_f2w1) = _ffn_fwd(
        "ffn1", x0, sc1, sh1, g1, f1w1, f1w3, f1w2, ln1_g, ln1_b, seq, tm,
        ride=([tr16(w_in), w_glu[0].astype(BF16), w_out[0].astype(BF16), conv_w[0], tr16(ffn2_w1)], [True] * 5))
    win = whole(g_win)
    wglu = whole(g_glu).astype(F32)
    wout = whole(g_out)
    wo_ssd, wo_s5 = wout[:SSD_WIDTH], wout[SSD_WIDTH:]
    convw = jnp.transpose(g_conv, (1, 0, 2)).reshape(CONV_K, CONV_CH)
    w_z, w_xbc = win[:SSD_WIDTH], win[SSD_WIDTH:SSD_WIDTH + CONV_CH]
    w_dt = win[SSD_WIDTH + CONV_CH:SSD_WIDTH + CONV_CH + SSD_HEADS]
    w_u = win[SSD_WIDTH + CONV_CH + SSD_HEADS:]
    dt_pad = [jnp.pad(w_dt[HEADS_PER_GROUP * g:HEADS_PER_GROUP * (g + 1)], ((0, LANE - HEADS_PER_GROUP), (0, 0)))
              for g in range(SSD_GROUPS)]
    w_dtp = jnp.concatenate(dt_pad, axis=0)
    w_proj = jnp.concatenate([w_xbc, w_z, w_u, w_dtp], axis=0)

    proj, h2 = modulate_proj("mix_proj", x1, sc2, sh2, w_proj, seq, tm, P_COLS // 2)
    xc = conv_fwd(proj, convw, conv_b, seq, tm)
    dtb = jnp.pad(dt_bias.reshape(SSD_GROUPS, 1, HEADS_PER_GROUP), ((0, 0), (0, 0), (0, LANE - HEADS_PER_GROUP)))
    alog = jnp.pad(a_log.reshape(SSD_GROUPS, 1, HEADS_PER_GROUP), ((0, 0), (0, 0), (0, LANE - HEADS_PER_GROUP)))
    dcol = jnp.pad(d_ssd.reshape(SSD_GROUPS, HEADS_PER_GROUP, 1), ((0, 0), (0, LANE - HEADS_PER_GROUP), (0, 0)))
    nw = ssd_norm_w.reshape(SSD_GROUPS, 1, GROUP_COLS)
    (y_ssd, hprev), (g_f2w3,) = ssd_fwd(xc, proj, dtb, alog, dcol, nw, bsz, seq,
                                        ride=([tr16(ffn2_w3)], [True]))

    a_re2, a_im2, ldt2 = s5_a_re[0], s5_a_im[0], s5_log_dt.reshape(S5_GROUPS, 1)
    ab_re, ab_im, f_re, f_im = _whole(_disc_a, "s5_disc_a", [a_re2, a_im2, ldt2], [(S5_GROUPS, S5_STATE)] * 4)
    b_re2, b_im2 = s5_b_re.reshape(S5_COLS, S5_GROUP_CH), s5_b_im.reshape(S5_COLS, S5_GROUP_CH)
    fr_col, fi_col = f_re.reshape(S5_COLS, 1), f_im.reshape(S5_COLS, 1)
    bb_re, bb_im = _whole(_disc_b, "s5_disc_b", [fr_col, fi_col, b_re2, b_im2], [(S5_COLS, S5_GROUP_CH)] * 2)
    wb_re, wb_im = _blockdiag_b(bb_re).astype(BF16), _blockdiag_b(bb_im).astype(BF16)
    wc_re, wc_im = _blockdiag_c(s5_c_re[0]).astype(BF16), _blockdiag_c(s5_c_im[0]).astype(BF16)
    dt5 = jnp.exp(ldt2)
    lam_re, lam_im = (dt5 * a_re2).reshape(1, S5_COLS), (dt5 * a_im2).reshape(1, S5_COLS)
    sf_re, sf_im, sb_re, sb_im, cf_re, cf_im, cb_re, cb_im = s5_tables(lam_re, lam_im)
    d5 = s5_d.reshape(S5_Q, 1, LANE)
    (y5, xr_all, xi_all), (g_f2w2,) = s5_fwd(
        proj, wb_re, wb_im, wc_re, wc_im, sf_re, sf_im, cf_re, cf_im, d5, bsz, seq,
        ride=([ffn2_w2[0].astype(BF16)], [True]))
    f2w1, f2w3, f2w2 = whole(g_f2w1), whole(g_f2w3), whole(g_f2w2)
    o5, = rowwise_fwd("s5_glu", f_glu, [y5], [], [wglu, b_glu], [(S5_WIDTH, F32)], seq, tm)

    mix = mm_nn("mix_out", [y_ssd, o5], [wo_ssd, wo_s5], tm, D_MODEL)
    x2, = rowwise_fwd("mix_ln", _res_ln(1.0), [x1, mix], [g2], [ln2_g, ln2_b], [(D_MODEL, F32)], seq, tm)

    (dy, loss_loc), res3, _ = _ffn_fwd("ffn2", x2, sc3, sh3, g3, f2w1, f2w3, f2w2, ln3_g, ln3_b, seq, tm, target=target)

    dx2, dmod3, (s_f2w1, d_f2w3, s_f2w2, d_ln3g, d_ln3b), _ = _ffn_bwd(
        "ffn2", dy, x2, sc3, sh3, g3, f2w1, f2w3, f2w2, ln3_g, ln3_b, res3, seq, tm)

    (dx1_a, dmix), (dg2,), (d_ln2g, d_ln2b) = rowwise_bwd(
        "mix_ln_b", _res_ln(1.0), [x1, mix], [g2], [ln2_g, ln2_b], [dx2], seq, tm, [F32, BF16])
    tw = min(DW_TOKENS, seq)
    d_wo = jnp.concatenate([mm_tn("mix_dwo_ssd", y_ssd, dmix, SSD_WIDTH, D_MODEL, tw, BF16),
                            mm_tn("mix_dwo_s5", o5, dmix, S5_WIDTH, D_MODEL, tw, BF16)], axis=0)
    dy_mixed = mm_nt("mix_dy", [dmix], [wout], tm, D_MODEL)
    dy_ssd, do5 = dy_mixed, (dy_mixed, SSD_WIDTH, S5_WIDTH)

    (dy5,), _, (d_wglu, d_bglu) = rowwise_bwd("s5_glu_b", f_glu, [y5], [], [wglu, b_glu], [do5], seq, tm, [F32])
    (du, dwbr, dwbi, dwcr, dwci, dab_re, dab_im, dd5), (s_f2w3, s_out, s_glu) = s5_bwd(
        proj, wb_re, wb_im, wc_re, wc_im, sb_re, sb_im, cb_re, cb_im, d5, xr_all, xi_all, dy5, bsz, seq,
        ride=([d_f2w3, d_wo.reshape(N_DEV, D_MODEL // N_DEV, D_MODEL),
               d_wglu.reshape(N_DEV, S5_WIDTH // N_DEV, S5_WIDTH).astype(BF16)], [False] * 3))
    dbb_re, dbb_im = _unblock_b(dwbr), _unblock_b(dwbi)
    dfr_col, dfi_col, d_b_re, d_b_im = _whole_vjp(_disc_b, "s5_disc_b_b", [fr_col, fi_col, b_re2, b_im2],
                                                  [dbb_re, dbb_im])
    d_a_re, d_a_im, d_ldt = _whole_vjp(
        _disc_a, "s5_disc_a_b", [a_re2, a_im2, ldt2],
        [dab_re.reshape(S5_GROUPS, S5_STATE), dab_im.reshape(S5_GROUPS, S5_STATE),
         dfr_col.reshape(S5_GROUPS, S5_STATE), dfi_col.reshape(S5_GROUPS, S5_STATE)])
    d_c_re, d_c_im = _unblock_c(dwcr), _unblock_c(dwci)

    dxs, dbm, dcm, ddt, dz, ddtb, dalog, ddcol, dnw = ssd_bwd(xc, proj, dtb, alog, dcol, nw, hprev, dy_ssd, bsz, seq)
    dpre, d_convw, d_convb = conv_bwd_pre(proj, convw, conv_b, dxs, dbm, dcm, seq, tm)
    dxbc = conv_bwd_x(dpre, convw, seq, tm)

    dw_xbc = mm_tn("mix_dw_xbc", dxbc, h2, CONV_CH, D_MODEL, tw, BF16)
    dw_z = mm_tn("mix_dw_z", dz, h2, SSD_WIDTH, D_MODEL, tw, BF16)
    dw_u = mm_tn("mix_dw_u", du, h2, S5_WIDTH, D_MODEL, tw, BF16)
    dw_dt = mm_tn("mix_dw_dt", ddt, h2, 2 * LANE, D_MODEL, tw, BF16)
    dw_dt8 = jnp.concatenate([dw_dt[LANE * g:LANE * g + HEADS_PER_GROUP] for g in range(SSD_GROUPS)], axis=0)
    d_win = jnp.concatenate([dw_z, dw_xbc, dw_dt8, dw_u], axis=0)
    dh2, (s_win,) = mm_nn("mix_dh", [dxbc, dz, du, ddt], [w_xbc, w_z, w_u, w_dtp], tm, D_MODEL,
                          ride=([d_win.reshape(N_DEV, IN_COLS // N_DEV, D_MODEL)], [False]))
    (dx1,), (dsc2, dsh2), _ = rowwise_bwd("mix_mod_b", f_modulate, [x1], [sc2, sh2], [], [dh2], seq, tm, [F32],
                                          add_rows={0: dx1_a})

    packing = {}

    def small_and_dmod(dmod1, d_ln1g, d_ln1b):
        dmod = jnp.concatenate(list(dmod1) + [dsh2, dsc2, dg2] + list(dmod3), axis=1).reshape(bsz, N_MOD * D_MODEL)
        small = _small_grads(d_ln1g, d_ln1b)
        packing["names"] = list(small)
        packing["shapes"] = [small[k].shape for k in small]
        return [_pack(list(small.values())), dmod], [True, True]

    def _small_grads(d_ln1g, d_ln1b):
        return {
            "ln1_g": d_ln1g, "ln1_b": d_ln1b, "conv_w": d_convw, "conv_b": d_convb,
            "dt_bias": ddtb[:, 0, :HEADS_PER_GROUP].reshape(1, SSD_HEADS),
            "a_log": dalog[:, 0, :HEADS_PER_GROUP].reshape(1, SSD_HEADS),
            "d_ssd": ddcol[:, :HEADS_PER_GROUP, 0].reshape(1, SSD_HEADS),
            "ssd_norm_w": dnw.reshape(1, SSD_WIDTH),
            "s5_a_re": d_a_re[None], "s5_a_im": d_a_im[None], "s5_log_dt": d_ldt.reshape(1, S5_GROUPS),
            "s5_b_re": d_b_re.reshape(s5_b_re.shape), "s5_b_im": d_b_im.reshape(s5_b_im.shape),
            "s5_c_re": d_c_re[None], "s5_c_im": d_c_im[None], "s5_d": dd5.reshape(1, S5_WIDTH),
            "b_glu": d_bglu, "ln2_g": d_ln2g, "ln2_b": d_ln2b, "ln3_g": d_ln3g, "ln3_b": d_ln3b,
            "loss": loss_loc.reshape(1, 1),
        }

    dx0, _, (s_f1w1, d_f1w3, s_f1w2, _, _), (s_small, s_dmod) = _ffn_bwd(
        "ffn1", dx1, x0, sc1, sh1, g1, f1w1, f1w3, f1w2, ln1_g, ln1_b, res1, seq, tm, chain=small_and_dmod)
    names, shapes = packing["names"], packing["shapes"]
    s_f1w3, = exchange("sum_grads", [d_f1w3], [False])

    out = {"grad_x": dx0.reshape(x.shape)}

    def put(name, res, shape):
        for key, val in zip(("grad_", "delta_", "new_m_", "new_v_"), res):
            out[key + name] = val.reshape(shape)

    for name, slots, tr in (("ffn1_w1", s_f1w1, 176), ("ffn1_w3", s_f1w3, 176), ("ffn2_w1", s_f2w1, 176),
                            ("ffn2_w3", s_f2w3, 176), ("w_in", s_win, IN_COLS // N_DEV)):
        w = given[name]
        grad = sum_slots("sum_" + name, slots, tr).T
        put(name, adamw("adam_" + name, grad, w[0], given["m_" + name][0], given["v_" + name][0], 256), w.shape)
    for name, slots in (("ffn1_w2", s_f1w2), ("ffn2_w2", s_f2w2)):
        w = given[name]
        put(name, adamw("adam_" + name, slots, w[0], given["m_" + name][0], given["v_" + name][0], 176), w.shape)
    put("w_glu", adamw("adam_w_glu", s_glu, w_glu[0], m_w_glu[0], v_w_glu[0], 64), w_glu.shape)
    put("w_out", adamw("adam_w_out", s_out, w_out[0], m_w_out[0], v_w_out[0], 128), w_out.shape)

    dmod_all = s_dmod.reshape(N_DEV * bsz, N_MOD * D_MODEL)
    g_bada, g_wada = ada_bwd(c_all, dmod_all, lax.dynamic_slice(dmod_all, (0, me * n_loc), (N_DEV * bsz, n_loc)))
    put("w_ada", adamw("adam_w_ada", g_wada, w_ada[0], m_w_ada[0], v_w_ada[0], 256), w_ada.shape)
    put("b_ada", adamw("adam_b_ada", g_bada, b_ada, m_b_ada, v_b_ada, 1), b_ada.shape)

    not_params = {"conv_w": jnp.zeros((CONV_K, CONV_CH), F32), "loss": jnp.zeros((1, 1), F32)}
    pw, pm, pv = [_pack([not_params[k] if k in not_params else given[pre + k] for k in names]) for pre in ("", "m_", "v_")]
    res_small = adamw("adam_small", s_small, pw, pm, pv, pw.shape[0])
    parts = [_unpack(r, shapes) for r in res_small]
    for i, k in enumerate(names):
        if k not in not_params:
            put(k, [p[i] for p in parts], given[k].shape)
    out["loss"] = parts[0][names.index("loss")][0, 0]
    g_cw = lax.dynamic_slice(parts[0][names.index("conv_w")], (0, me * LANE), (CONV_K, LANE))
    put("conv_w", adamw("adam_conv_w", g_cw, conv_w[0], m_conv_w[0], v_conv_w[0], CONV_K), conv_w.shape)

    order = ["w_ada", "b_ada", "ffn1_w1", "ffn1_w3", "ffn1_w2", "ln1_g", "ln1_b", "w_in", "conv_w", "conv_b", "dt_bias",
             "a_log", "d_ssd", "ssd_norm_w", "s5_a_re", "s5_a_im", "s5_log_dt", "s5_b_re", "s5_b_im", "s5_c_re",
             "s5_c_im", "s5_d", "w_glu", "b_glu", "w_out", "ln2_g", "ln2_b", "ffn2_w1", "ffn2_w3", "ffn2_w2", "ln3_g",
             "ln3_b"]
    return (out["loss"], out["grad_x"], *[out[p + n] for p in ("grad_", "delta_", "new_m_", "new_v_") for n in order])
```

```python
import functools
import math

import jax
import jax.numpy as jnp
from jax import lax
from jax.experimental import pallas as pl
from jax.experimental.pallas import tpu as pltpu

F32 = jnp.float32
BF16 = jnp.bfloat16
MESH = pl.DeviceIdType.MESH

N_DEV = 8
D_MODEL = 1024
D_FF = 2816
N_MOD = 9
SSD_WIDTH = 512
SSD_HEADS = 8
SSD_HEAD_DIM = 64
SSD_GROUPS = 2
SSD_STATE = 128
SSD_CHUNK = 128
GROUP_COLS = SSD_WIDTH // SSD_GROUPS
HEADS_PER_GROUP = SSD_HEADS // SSD_GROUPS
CONV_K = 4
CONV_CH = 1024
S5_WIDTH = 512
S5_GROUPS = 32
S5_GROUP_CH = 16
S5_STATE = 64
S5_COLS = S5_GROUPS * S5_STATE
S5_Q = 4
S5_CHUNK = 2048
ALPHA = 2.0 ** 0.25
LN_EPS = 1e-5
LANE = 128
HALO = 8

P_XBC, P_Z, P_U, P_DT = 0, 1024, 1536, 2048
P_COLS = 2048 + SSD_GROUPS * LANE
IN_COLS = SSD_WIDTH + CONV_CH + SSD_HEADS + S5_WIDTH

ADAM_LR, ADAM_B1, ADAM_B2, ADAM_EPS, ADAM_WD, ADAM_STEP = 0.001, 0.9, 0.999, 1e-08, 0.01, 10

VMEM_LIMIT = 56 * 1024 * 1024


def _cp(*sem):
    return pltpu.CompilerParams(dimension_semantics=sem if sem else None, vmem_limit_bytes=VMEM_LIMIT)


def _dg(a, b, ca, cb):
    return lax.dot_general(a.astype(BF16), b.astype(BF16), (((ca,), (cb,)), ((), ())), preferred_element_type=F32)


@jax.custom_vjp
def bdot_nn(a, b):
    return _dg(a, b, 1, 0)


bdot_nn.defvjp(lambda a, b: (_dg(a, b, 1, 0), (a, b)),
               lambda r, g: (_dg(g, r[1], 1, 1), _dg(r[0], g, 0, 0)))


@jax.custom_vjp
def bdot_nt(a, b):
    return _dg(a, b, 1, 1)


bdot_nt.defvjp(lambda a, b: (_dg(a, b, 1, 1), (a, b)),
               lambda r, g: (_dg(g, r[1], 1, 0), _dg(g, r[0], 0, 0)))


@jax.custom_vjp
def bdot_tn(a, b):
    return _dg(a, b, 0, 0)


bdot_tn.defvjp(lambda a, b: (_dg(a, b, 0, 0), (a, b)),
               lambda r, g: (_dg(r[1], g, 1, 1), _dg(r[0], g, 1, 0)))


def _split3(x):
    def top(v):
        bits = lax.bitcast_convert_type(v, jnp.int32) & jnp.int32(-65536)
        return lax.bitcast_convert_type(bits, F32)

    hi = top(x)
    r1 = x - hi
    mid = top(r1)
    return hi.astype(BF16), mid.astype(BF16), (r1 - mid).astype(BF16)


def _dot3(a, b, ca, cb, split_a):
    dims = (((ca,), (cb,)), ((), ()))
    if split_a:
        c = b.astype(BF16)
        return sum(lax.dot_general(p, c, dims, preferred_element_type=F32) for p in _split3(a))
    c = a.astype(BF16)
    return sum(lax.dot_general(c, p, dims, preferred_element_type=F32) for p in _split3(b))


@jax.custom_vjp
def mask_dot_left(c, x):
    return _dot3(c, x, 1, 0, False)


mask_dot_left.defvjp(lambda c, x: (_dot3(c, x, 1, 0, False), c),
                     lambda c, g: (jnp.zeros_like(c), _dot3(c, g, 0, 0, False)))


@jax.custom_vjp
def mask_dot_right(x, c):
    return _dot3(x, c, 1, 0, True)


mask_dot_right.defvjp(lambda x, c: (_dot3(x, c, 1, 0, True), c),
                      lambda c, g: (_dot3(g, c, 1, 1, True), jnp.zeros_like(c)))


def _take_col(z):
    @jax.custom_vjp
    def take(x):
        return x[:, z:z + 1]

    def bwd(shape, g):
        hot = (lax.broadcasted_iota(jnp.int32, (1, shape[1]), 1) == z).astype(F32)
        return (g * hot,)

    take.defvjp(lambda x: (x[:, z:z + 1], x.shape), bwd)
    return take


def _take_row(z):
    @jax.custom_vjp
    def take(x):
        return x[z:z + 1, :]

    def bwd(shape, g):
        hot = (lax.broadcasted_iota(jnp.int32, (shape[0], 1), 0) == z).astype(F32)
        return (hot * g,)

    take.defvjp(lambda x: (x[z:z + 1, :], x.shape), bwd)
    return take


def _view(a):
    return a if isinstance(a, tuple) else (a, 0, a.shape[1])


def _col_spec(view, rows, width, index):
    _, off, _ = view
    assert off % width == 0
    return pl.BlockSpec((rows, width), lambda *g: (index(*g)[0], off // width + index(*g)[1]))


def _rw_in_specs(rows, bps, gps, tm, tps):
    specs = [_col_spec(_view(r), tm, _view(r)[2], lambda i: (i, 0)) for r in rows]
    specs += [pl.BlockSpec((1, 1, b.shape[2]), lambda i: (i // tps, 0, 0)) for b in bps]
    specs += [pl.BlockSpec(g.shape, lambda i, nd=g.ndim: (0,) * nd) for g in gps]
    return specs


def _rw_vals(refs, nr, nb, ng):
    vals = [r[...] for r in refs[:nr]]
    vals += [b[0] for b in refs[nr:nr + nb]]
    vals += [g[...] for g in refs[nr + nb:nr + nb + ng]]
    return vals


def rowwise_fwd(name, f, rows, bps, gps, outs, seq, tm):
    t = _view(rows[0])[0].shape[0]
    tps = seq // tm
    nr, nb, ng = len(rows), len(bps), len(gps)

    def body(*refs):
        res = f(*_rw_vals(refs, nr, nb, ng))
        for o, v in zip(refs[nr + nb + ng:], res):
            o[...] = v.astype(o.dtype)

    return pl.pallas_call(
        body, name=name, grid=(t // tm,),
        in_specs=_rw_in_specs(rows, bps, gps, tm, tps),
        out_specs=[pl.BlockSpec((tm, c), lambda i: (i, 0)) for c, _ in outs],
        out_shape=[jax.ShapeDtypeStruct((t, c), d) for c, d in outs],
        compiler_params=_cp("arbitrary"),
    )(*[_view(r)[0] for r in rows], *bps, *gps)


def rowwise_bwd(name, f, rows, bps, gps, douts, seq, tm, row_grads, add_rows=None):
    add_rows = add_rows or {}
    t = _view(rows[0])[0].shape[0]
    tps = seq // tm
    nr, nb, ng, nd = len(rows), len(bps), len(gps), len(douts)
    want = [k for k in range(nr) if row_grads[k] is not None]
    adds = sorted(add_rows)
    n_in = nr + nb + ng + nd + len(adds)

    def body(*refs):
        vals = _rw_vals(refs, nr, nb, ng)
        dvals = tuple(r[...] for r in refs[nr + nb + ng:nr + nb + ng + nd])
        add_refs = dict(zip(adds, refs[nr + nb + ng + nd:n_in]))
        out_refs = refs[n_in:]
        _, pull = jax.vjp(f, *vals)
        grads = pull(dvals)
        i = pl.program_id(0)
        for o, k in zip(out_refs, want):
            g = grads[k]
            if k in add_refs:
                g = g + add_refs[k][...]
            o[...] = g.astype(o.dtype)
        for j in range(nb):
            o = out_refs[len(want) + j]

            @pl.when(i % tps == 0)
            def _(o=o):
                o[...] = jnp.zeros_like(o)

            o[0] = o[0] + grads[nr + j]
        for j in range(ng):
            o = out_refs[len(want) + nb + j]

            @pl.when(i == 0)
            def _(o=o):
                o[...] = jnp.zeros_like(o)

            o[...] = o[...] + grads[nr + nb + j]

    in_specs = _rw_in_specs(rows, bps, gps, tm, tps)
    in_specs += [_col_spec(_view(d), tm, _view(d)[2], lambda i: (i, 0)) for d in douts]
    in_specs += [pl.BlockSpec((tm, add_rows[k].shape[1]), lambda i: (i, 0)) for k in adds]
    out_specs = [pl.BlockSpec((tm, _view(rows[k])[2]), lambda i: (i, 0)) for k in want]
    out_shape = [jax.ShapeDtypeStruct((t, _view(rows[k])[2]), row_grads[k]) for k in want]
    out_specs += [pl.BlockSpec((1, 1, b.shape[2]), lambda i: (i // tps, 0, 0)) for b in bps]
    out_shape += [jax.ShapeDtypeStruct(b.shape, F32) for b in bps]
    out_specs += [pl.BlockSpec(g.shape, lambda i, n=g.ndim: (0,) * n) for g in gps]
    out_shape += [jax.ShapeDtypeStruct(g.shape, F32) for g in gps]
    res = pl.pallas_call(
        body, name=name, grid=(t // tm,), in_specs=in_specs, out_specs=out_specs, out_shape=out_shape,
        compiler_params=_cp("arbitrary"),
    )(*[_view(r)[0] for r in rows], *bps, *gps, *[_view(d)[0] for d in douts], *[add_rows[k] for k in adds])
    nw = len(want)
    return res[:nw], res[nw:nw + nb], res[nw + nb:]


def mm_nn(name, xs, ws, tm, tn, out_dtype=F32, ride=None):
    views = [_view(x) for x in xs]
    t, n, k = views[0][0].shape[0], ws[0].shape[1], len(xs)

    def body(*refs):
        acc = _dg(refs[0][...], refs[k][...], 1, 0)
        for i in range(1, k):
            acc = acc + _dg(refs[i][...], refs[k + i][...], 1, 0)
        refs[2 * k][...] = acc.astype(out_dtype)

    in_specs = [_col_spec(v, tm, v[2], lambda i, j: (i, 0)) for v in views]
    in_specs += [pl.BlockSpec((w.shape[0], tn), lambda i, j: (0, j)) for w in ws]
    out_spec = pl.BlockSpec((tm, tn), lambda i, j: (i, j))
    out_shape = jax.ShapeDtypeStruct((t, n), out_dtype)
    if ride is not None:
        (res,), landed = hosted_call(body, name=name, grid=(t // tm, n // tn), in_specs=in_specs, out_specs=[out_spec],
                                     out_shape=[out_shape], args=(*[v[0] for v in views], *ws), ride=ride)
        return res, landed
    return pl.pallas_call(
        body, name=name, grid=(t // tm, n // tn), in_specs=in_specs, out_specs=out_spec, out_shape=out_shape,
        compiler_params=_cp("parallel", "parallel"),
    )(*[v[0] for v in views], *ws)


def mm_nt(name, dys, ws, tm, tk, out_dtype=F32, ride=None):
    views = [_view(d) for d in dys]
    t, kk, k = views[0][0].shape[0], ws[0].shape[0], len(dys)

    def body(*refs):
        acc = _dg(refs[0][...], refs[k][...], 1, 1)
        for i in range(1, k):
            acc = acc + _dg(refs[i][...], refs[k + i][...], 1, 1)
        refs[2 * k][...] = acc.astype(out_dtype)

    in_specs = [_col_spec(v, tm, v[2], lambda i, j: (i, 0)) for v in views]
    in_specs += [pl.BlockSpec((tk, w.shape[1]), lambda i, j: (j, 0)) for w in ws]
    out_spec = pl.BlockSpec((tm, tk), lambda i, j: (i, j))
    out_shape = jax.ShapeDtypeStruct((t, kk), out_dtype)
    if ride is not None:
        (res,), landed = hosted_call(body, name=name, grid=(t // tm, kk // tk), in_specs=in_specs, out_specs=[out_spec],
                                     out_shape=[out_shape], args=(*[v[0] for v in views], *ws), ride=ride)
        return res, landed
    return pl.pallas_call(
        body, name=name, grid=(t // tm, kk // tk), in_specs=in_specs, out_specs=out_spec, out_shape=out_shape,
        compiler_params=_cp("parallel", "parallel"),
    )(*[v[0] for v in views], *ws)


def modulate_proj(name, x, sc, sh, w, seq, tm, tk):
    t, kdim = x.shape
    n = w.shape[0]
    tps = seq // tm

    def body(x_ref, sc_ref, sh_ref, w_ref, o_ref, h_ref):
        h = (x_ref[...] * (1.0 + sc_ref[0]) + sh_ref[0]).astype(BF16)
        h_ref[...] = h
        o_ref[...] = _dg(h, w_ref[...], 1, 1)

    per_seq = pl.BlockSpec((1, 1, kdim), lambda i, j: (i // tps, 0, 0))
    return pl.pallas_call(
        body, name=name, grid=(t // tm, n // tk),
        in_specs=[pl.BlockSpec((tm, kdim), lambda i, j: (i, 0)), per_seq, per_seq,
                  pl.BlockSpec((tk, kdim), lambda i, j: (j, 0))],
        out_specs=[pl.BlockSpec((tm, tk), lambda i, j: (i, j)), pl.BlockSpec((tm, kdim), lambda i, j: (i, 0))],
        out_shape=[jax.ShapeDtypeStruct((t, n), F32), jax.ShapeDtypeStruct((t, kdim), BF16)],
        compiler_params=_cp("arbitrary", "arbitrary"),
    )(x, sc, sh, w)


def mm_tn(name, x, dy, tk, tn, tt, out_dtype=F32, ride=None):
    xv, dv = _view(x), _view(dy)
    t, kk, n = xv[0].shape[0], xv[2], dv[2]
    steps = t // tt

    def body(x_ref, d_ref, o_ref, acc_ref):
        @pl.when(pl.program_id(2) == 0)
        def _():
            acc_ref[...] = jnp.zeros_like(acc_ref)

        acc_ref[...] += _dg(x_ref[...], d_ref[...], 0, 0)

        @pl.when(pl.program_id(2) == steps - 1)
        def _():
            o_ref[...] = acc_ref[...].astype(out_dtype)

    in_specs = [_col_spec(xv, tt, tk, lambda a, b, c: (c, a)), _col_spec(dv, tt, tn, lambda a, b, c: (c, b))]
    out_spec = pl.BlockSpec((tk, tn), lambda a, b, c: (a, b))
    out_shape = jax.ShapeDtypeStruct((kk, n), out_dtype)
    if ride is not None:
        (res,), landed = hosted_call(body, name=name, grid=(kk // tk, n // tn, steps), in_specs=in_specs,
                                     out_specs=[out_spec], out_shape=[out_shape], scratch=[pltpu.VMEM((tk, tn), F32)],
                                     args=(xv[0], dv[0]), ride=ride)
        return res, landed
    return pl.pallas_call(
        body, name=name, grid=(kk // tk, n // tn, steps), in_specs=in_specs, out_specs=out_spec, out_shape=out_shape,
        scratch_shapes=[pltpu.VMEM((tk, tn), F32)],
        compiler_params=_cp("parallel", "parallel", "arbitrary"),
    )(xv[0], dv[0])


def _silu(x):
    return x * jax.nn.sigmoid(x)


def f_modulate(x, sc, sh):
    return (x * (1.0 + sc) + sh,)


def _res_ln(coef):
    def f(x, y, g, lg, lb):
        r = ALPHA * x + (coef * g) * y
        mu = jnp.mean(r, axis=-1, keepdims=True)
        d = r - mu
        var = jnp.mean(d * d, axis=-1, keepdims=True)
        return (d * lax.rsqrt(var + LN_EPS) * lg + lb,)
    return f


def f_glu(y, w, b):
    g = jax.nn.gelu(y)
    return (g * jax.nn.sigmoid(bdot_nn(g, w) + b),)


def _shift_down(x, halo, k):
    if k == 0:
        return x
    r = pltpu.roll(x, k, 0)
    hr = pltpu.roll(halo, k, 0)
    row = lax.broadcasted_iota(jnp.int32, (HALO, 1), 0)
    top = jnp.where(row < k, hr, r[:HALO])
    return jnp.concatenate([top, r[HALO:]], axis=0)


def _shift_up(x, halo, k):
    if k == 0:
        return x
    n = x.shape[0]
    r = pltpu.roll(x, n - k, 0)
    hr = pltpu.roll(halo, HALO - k, 0)
    row = lax.broadcasted_iota(jnp.int32, (HALO, 1), 0)
    bot = jnp.where(row >= HALO - k, hr, r[n - HALO:])
    return jnp.concatenate([r[:n - HALO], bot], axis=0)


def _conv_pre(x, halo, w, b):
    acc = x * w[CONV_K - 1:CONV_K, :] + b
    for k in range(1, CONV_K):
        acc = acc + _shift_down(x, halo, k) * w[CONV_K - 1 - k:CONV_K - k, :]
    return acc


def _rows_before(width, tm):
    return pl.BlockSpec((HALO, width), lambda i: (jnp.maximum(i * (tm // HALO) - 1, 0), 0))


def conv_fwd(proj, w, b, seq, tm):
    t = proj.shape[0]
    tps = seq // tm

    def body(x_ref, h_ref, w_ref, b_ref, o_ref):
        first = (pl.program_id(0) % tps == 0)
        halo = jnp.where(first, 0.0, h_ref[...])
        o_ref[...] = _silu(_conv_pre(x_ref[...], halo, w_ref[...], b_ref[...]))

    return pl.pallas_call(
        body, name="conv_fwd", grid=(t // tm,),
        in_specs=[pl.BlockSpec((tm, CONV_CH), lambda i: (i, 0)), _rows_before(CONV_CH, tm),
                  pl.BlockSpec((CONV_K, CONV_CH), lambda i: (0, 0)), pl.BlockSpec((1, CONV_CH), lambda i: (0, 0))],
        out_specs=pl.BlockSpec((tm, CONV_CH), lambda i: (i, 0)),
        out_shape=jax.ShapeDtypeStruct((t, CONV_CH), F32),
        compiler_params=_cp("arbitrary"),
    )(proj, proj, w, b)


def conv_bwd_pre(proj, w, b, dxs, dbm, dcm, seq, tm):
    t = proj.shape[0]
    tps = seq // tm

    def body(x_ref, h_ref, w_ref, b_ref, d1, d2, d3, dp_ref, dw_ref, db_ref):
        i = pl.program_id(0)
        halo = jnp.where(i % tps == 0, 0.0, h_ref[...])
        x = x_ref[...]
        pre = _conv_pre(x, halo, w_ref[...], b_ref[...])
        sg = jax.nn.sigmoid(pre)
        dout = jnp.concatenate([d1[...], d2[...], d3[...]], axis=1)
        dp = dout * (sg * (1.0 + pre * (1.0 - sg)))
        dp_ref[...] = dp

        @pl.when(i == 0)
        def _():
            dw_ref[...] = jnp.zeros_like(dw_ref)
            db_ref[...] = jnp.zeros_like(db_ref)

        db_ref[...] += jnp.sum(dp, axis=0, keepdims=True)
        for k in range(CONV_K):
            j = CONV_K - 1 - k
            dw_ref[j:j + 1, :] += jnp.sum(dp * _shift_down(x, halo, k), axis=0, keepdims=True)

    return pl.pallas_call(
        body, name="conv_bwd_pre", grid=(t // tm,),
        in_specs=[pl.BlockSpec((tm, CONV_CH), lambda i: (i, 0)), _rows_before(CONV_CH, tm),
                  pl.BlockSpec((CONV_K, CONV_CH), lambda i: (0, 0)), pl.BlockSpec((1, CONV_CH), lambda i: (0, 0)),
                  pl.BlockSpec((tm, 512), lambda i: (i, 0)), pl.BlockSpec((tm, 256), lambda i: (i, 0)),
                  pl.BlockSpec((tm, 256), lambda i: (i, 0))],
        out_specs=[pl.BlockSpec((tm, CONV_CH), lambda i: (i, 0)), pl.BlockSpec((CONV_K, CONV_CH), lambda i: (0, 0)),
                   pl.BlockSpec((1, CONV_CH), lambda i: (0, 0))],
        out_shape=[jax.ShapeDtypeStruct((t, CONV_CH), F32), jax.ShapeDtypeStruct((CONV_K, CONV_CH), F32),
                   jax.ShapeDtypeStruct((1, CONV_CH), F32)],
        compiler_params=_cp("arbitrary"),
    )(proj, proj, w, b, dxs, dbm, dcm)


def conv_bwd_x(dpre, w, seq, tm):
    t = dpre.shape[0]
    tps = seq // tm
    blocks = tm // HALO
    last = t // HALO - 1

    def body(d_ref, h_ref, w_ref, o_ref):
        halo = jnp.where(pl.program_id(0) % tps == tps - 1, 0.0, h_ref[...])
        d = d_ref[...]
        w = w_ref[...]
        acc = d * w[CONV_K - 1:CONV_K, :]
        for k in range(1, CONV_K):
            acc = acc + _shift_up(d, halo, k) * w[CONV_K - 1 - k:CONV_K - k, :]
        o_ref[...] = acc

    return pl.pallas_call(
        body, name="conv_bwd_x", grid=(t // tm,),
        in_specs=[pl.BlockSpec((tm, CONV_CH), lambda i: (i, 0)),
                  pl.BlockSpec((HALO, CONV_CH), lambda i: (jnp.minimum((i + 1) * blocks, last), 0)),
                  pl.BlockSpec((CONV_K, CONV_CH), lambda i: (0, 0))],
        out_specs=pl.BlockSpec((tm, CONV_CH), lambda i: (i, 0)),
        out_shape=jax.ShapeDtypeStruct((t, CONV_CH), F32),
        compiler_params=_cp("arbitrary"),
    )(dpre, dpre, w)


def _softplus(x):
    return jnp.maximum(x, 0.0) + jnp.log1p(jnp.exp(-jnp.abs(x)))


def _ssd_chunk(xs, bg, cg, dtr, zz, hp, dtb, alog, dcol, nw):
    l = xs.shape[0]
    row = lax.broadcasted_iota(jnp.int32, (l, l), 0)
    col = lax.broadcasted_iota(jnp.int32, (l, l), 1)
    causal = row >= col
    tril = causal.astype(F32)
    expand = (lax.broadcasted_iota(jnp.int32, (LANE, GROUP_COLS), 1) // SSD_HEAD_DIM
              == lax.broadcasted_iota(jnp.int32, (LANE, GROUP_COLS), 0)).astype(F32)
    head_of_col = lax.broadcasted_iota(jnp.int32, (1, GROUP_COLS), 1) // SSD_HEAD_DIM
    last_row = (lax.broadcasted_iota(jnp.int32, (l, 1), 0) == l - 1).astype(F32)

    dtc = _softplus(dtr + dtb)
    a_c = dtc * (-jnp.exp(alog))
    acs_c = mask_dot_left(tril, a_c)
    dt_e = mask_dot_right(dtc, expand)
    acs_e = mask_dot_right(acs_c, expand)
    alast_e = jnp.sum(acs_e * last_row, axis=0, keepdims=True)
    x = xs * dt_e
    states = bdot_tn(bg, x * jnp.exp(alast_e - acs_e))
    h_next = jnp.exp(alast_e) * hp + states
    d_e = jnp.sum(dcol * expand, axis=0, keepdims=True)
    y = bdot_nn(cg, hp) * jnp.exp(acs_e) + d_e * xs
    cb = bdot_nt(cg, bg)
    acs_t = acs_c.T
    for z in range(HEADS_PER_GROUP):
        seg = _take_col(z)(acs_c) - _take_row(z)(acs_t)
        lmat = jnp.exp(jnp.where(causal, seg, -1e30))
        y = y + bdot_nn(cb * lmat, x * (head_of_col == z).astype(F32))
    yz = y * _silu(zz)
    ms = jnp.mean(yz * yz, axis=-1, keepdims=True)
    return yz * lax.rsqrt(ms + LN_EPS) * nw, h_next


SSD_SUB = 8
SSD_ROWS = SSD_SUB * SSD_CHUNK


def _ssd_in_specs(steps, rev):
    def tok(b, c):
        return b * steps + (steps - 1 - c if rev else c)

    whole = lambda *shape: pl.BlockSpec(shape, lambda b, c: (0,) * len(shape))
    both = SSD_GROUPS * SSD_STATE
    return [
        pl.BlockSpec((SSD_ROWS, SSD_WIDTH), lambda b, c: (tok(b, c), 0)),
        pl.BlockSpec((SSD_ROWS, both), lambda b, c: (tok(b, c), SSD_WIDTH // both)),
        pl.BlockSpec((SSD_ROWS, both), lambda b, c: (tok(b, c), SSD_WIDTH // both + 1)),
        pl.BlockSpec((SSD_ROWS, SSD_GROUPS * LANE), lambda b, c: (tok(b, c), P_DT // (SSD_GROUPS * LANE))),
        pl.BlockSpec((SSD_ROWS, SSD_WIDTH), lambda b, c: (tok(b, c), P_Z // SSD_WIDTH)),
        whole(SSD_GROUPS, 1, LANE), whole(SSD_GROUPS, 1, LANE), whole(SSD_GROUPS, LANE, 1),
        whole(SSD_GROUPS, 1, GROUP_COLS),
    ], tok


def _piece(ref, s, g, width):
    return ref[s * SSD_CHUNK:(s + 1) * SSD_CHUNK, g * width:(g + 1) * width]


def ssd_fwd(xc, proj, dtb, alog, dcol, nw, bsz, seq, ride=None):
    t = xc.shape[0]
    nc = seq // SSD_CHUNK
    steps = nc // SSD_SUB
    in_specs, tok = _ssd_in_specs(steps, False)

    def body(xs, bm, cm, dtr, zz, dtb_r, alog_r, dcol_r, nw_r, y_ref, hp_ref, h_scr):
        @pl.when(pl.program_id(1) == 0)
        def _():
            h_scr[...] = jnp.zeros_like(h_scr)

        for g in range(SSD_GROUPS):
            h = h_scr[g]
            for s in range(SSD_SUB):
                hp_ref[g, 0, s] = h
                y, h = _ssd_chunk(_piece(xs, s, g, GROUP_COLS), _piece(bm, s, g, SSD_STATE),
                                  _piece(cm, s, g, SSD_STATE), _piece(dtr, s, g, LANE), _piece(zz, s, g, GROUP_COLS), h,
                                  dtb_r[g], alog_r[g], dcol_r[g], nw_r[g])
                y_ref[s * SSD_CHUNK:(s + 1) * SSD_CHUNK, g * GROUP_COLS:(g + 1) * GROUP_COLS] = y
            h_scr[g] = h

    return hosted_call(
        body, name="ssd_fwd", grid=(bsz, steps), in_specs=in_specs,
        out_specs=[pl.BlockSpec((SSD_ROWS, SSD_WIDTH), lambda b, c: (tok(b, c), 0)),
                   pl.BlockSpec((SSD_GROUPS, 1, SSD_SUB, SSD_STATE, GROUP_COLS), lambda b, c: (0, b, c, 0, 0))],
        out_shape=[jax.ShapeDtypeStruct((t, SSD_WIDTH), F32),
                   jax.ShapeDtypeStruct((SSD_GROUPS, bsz, nc, SSD_STATE, GROUP_COLS), F32)],
        scratch=[pltpu.VMEM((SSD_GROUPS, SSD_STATE, GROUP_COLS), F32)],
        args=(xc, xc, xc, proj, proj, dtb, alog, dcol, nw), ride=ride)


def ssd_bwd(xc, proj, dtb, alog, dcol, nw, hprev, dy, bsz, seq):
    t = xc.shape[0]
    nc = seq // SSD_CHUNK
    steps = nc // SSD_SUB
    in_specs, tok = _ssd_in_specs(steps, True)
    in_specs += [pl.BlockSpec((SSD_GROUPS, 1, SSD_SUB, SSD_STATE, GROUP_COLS), lambda b, c: (0, b, steps - 1 - c, 0, 0)),
                 pl.BlockSpec((SSD_ROWS, SSD_WIDTH), lambda b, c: (tok(b, c), 0))]

    def body(xs, bm, cm, dtr, zz, dtb_r, alog_r, dcol_r, nw_r, hp_ref, dy_ref,
             dxs, dbm, dcm, ddt, dzz, ddtb, dalog, ddcol, dnw, dh_scr):
        b, c = pl.program_id(0), pl.program_id(1)

        @pl.when(c == 0)
        def _():
            dh_scr[...] = jnp.zeros_like(dh_scr)

        @pl.when((b == 0) & (c == 0))
        def _():
            for r in (ddtb, dalog, ddcol, dnw):
                r[...] = jnp.zeros_like(r)

        for g in range(SSD_GROUPS):
            wide = slice(g * GROUP_COLS, (g + 1) * GROUP_COLS)
            state = slice(g * SSD_STATE, (g + 1) * SSD_STATE)
            dh = dh_scr[g]
            for s in reversed(range(SSD_SUB)):
                rows = slice(s * SSD_CHUNK, (s + 1) * SSD_CHUNK)
                _, pull = jax.vjp(_ssd_chunk, xs[rows, wide], bm[rows, state], cm[rows, state],
                                  _piece(dtr, s, g, LANE), zz[rows, wide], hp_ref[g, 0, s],
                                  dtb_r[g], alog_r[g], dcol_r[g], nw_r[g])
                d = pull((dy_ref[rows, wide], dh))
                dxs[rows, wide], dbm[rows, state], dcm[rows, state], dzz[rows, wide] = d[0], d[1], d[2], d[4]
                ddt[rows, g * LANE:(g + 1) * LANE] = d[3]
                dh = d[5]
                ddtb[g] += d[6]
                dalog[g] += d[7]
                ddcol[g] += d[8]
                dnw[g] += d[9]
            dh_scr[g] = dh

    def tile(w):
        return pl.BlockSpec((SSD_ROWS, w), lambda b, c: (tok(b, c), 0))

    whole = lambda *shape: pl.BlockSpec(shape, lambda b, c: (0,) * len(shape))
    return pl.pallas_call(
        body, name="ssd_bwd", grid=(bsz, steps), in_specs=in_specs,
        out_specs=[tile(SSD_WIDTH), tile(2 * SSD_STATE), tile(2 * SSD_STATE), tile(2 * LANE), tile(SSD_WIDTH),
                   whole(SSD_GROUPS, 1, LANE), whole(SSD_GROUPS, 1, LANE), whole(SSD_GROUPS, LANE, 1),
                   whole(SSD_GROUPS, 1, GROUP_COLS)],
        out_shape=[jax.ShapeDtypeStruct((t, SSD_WIDTH), F32), jax.ShapeDtypeStruct((t, 2 * SSD_STATE), F32),
                   jax.ShapeDtypeStruct((t, 2 * SSD_STATE), F32), jax.ShapeDtypeStruct((t, 2 * LANE), F32),
                   jax.ShapeDtypeStruct((t, SSD_WIDTH), F32),
                   jax.ShapeDtypeStruct((SSD_GROUPS, 1, LANE), F32), jax.ShapeDtypeStruct((SSD_GROUPS, 1, LANE), F32),
                   jax.ShapeDtypeStruct((SSD_GROUPS, LANE, 1), F32),
                   jax.ShapeDtypeStruct((SSD_GROUPS, 1, GROUP_COLS), F32)],
        scratch_shapes=[pltpu.VMEM((SSD_GROUPS, SSD_STATE, GROUP_COLS), F32)],
        compiler_params=_cp("arbitrary", "arbitrary"),
    )(xc, xc, xc, proj, proj, dtb, alog, dcol, nw, hprev, dy)


def _disc_a(a_re, a_im, log_dt):
    dt = jnp.exp(log_dt)
    mag = jnp.exp(dt * a_re)
    ab_re, ab_im = mag * jnp.cos(dt * a_im), mag * jnp.sin(dt * a_im)
    den = a_re * a_re + a_im * a_im
    nr, ni = ab_re - 1.0, ab_im
    f_re, f_im = (nr * a_re + ni * a_im) / den, (ni * a_re - nr * a_im) / den
    return ab_re, ab_im, f_re, f_im


def _disc_b(f_re, f_im, b_re, b_im):
    return f_re * b_re - f_im * b_im, f_re * b_im + f_im * b_re


def _whole(f, name, args, outs):
    def body(*refs):
        res = f(*[r[...] for r in refs[:len(args)]])
        for o, v in zip(refs[len(args):], res):
            o[...] = v

    return pl.pallas_call(body, name=name, out_shape=[jax.ShapeDtypeStruct(s, F32) for s in outs])(*args)


def _whole_vjp(f, name, args, cts):
    def body(*refs):
        vals = [r[...] for r in refs[:len(args)]]
        _, pull = jax.vjp(f, *vals)
        res = pull(tuple(r[...] for r in refs[len(args):len(args) + len(cts)]))
        for o, v in zip(refs[len(args) + len(cts):], res):
            o[...] = v

    return pl.pallas_call(body, name=name, out_shape=[jax.ShapeDtypeStruct(a.shape, F32) for a in args])(*args, *cts)


S5_SUB = 8
S5_STEPS = 3


def s5_tables(lam_re, lam_im):
    rows = S5_STEPS * S5_SUB

    def body(lr_ref, li_ref, sf_re, sf_im, sb_re, sb_im, cf_re, cf_im, cb_re, cb_im):
        lr, li = lr_ref[...], li_ref[...]

        def power(k):
            m = jnp.exp(k * lr)
            return m * jnp.cos(k * li), m * jnp.sin(k * li)

        srow = lax.broadcasted_iota(jnp.int32, (rows, 1), 0)
        k = jnp.left_shift(1, srow // S5_SUB)
        tt = srow % S5_SUB
        pr, pi = power(k.astype(F32))
        fwd, bwd = tt >= k, tt < S5_SUB - k
        sf_re[...], sf_im[...] = jnp.where(fwd, pr, 0.0), jnp.where(fwd, pi, 0.0)
        sb_re[...], sb_im[...] = jnp.where(bwd, pr, 0.0), jnp.where(bwd, pi, 0.0)
        trow = lax.broadcasted_iota(jnp.int32, (S5_SUB, 1), 0)
        cf_re[...], cf_im[...] = power((trow + 1).astype(F32))
        cb_re[...], cb_im[...] = power((S5_SUB - trow).astype(F32))

    shp = [jax.ShapeDtypeStruct((rows, S5_COLS), F32)] * 4 + [jax.ShapeDtypeStruct((S5_SUB, S5_COLS), F32)] * 4
    return pl.pallas_call(body, name="s5_tables", out_shape=shp)(lam_re, lam_im)


def _s5_coefs(steps_re, steps_im, carry_re, carry_im, reverse):
    sign = -1.0 if reverse else 1.0
    steps = [(steps_re[s * S5_SUB:(s + 1) * S5_SUB, :], sign * steps_im[s * S5_SUB:(s + 1) * S5_SUB, :])
             for s in range(S5_STEPS)]
    return steps, (carry_re[...], sign * carry_im[...])


def _s5_block_scan(ar, ai, coefs, cr, ci, reverse):
    steps, (qr, qi) = coefs
    for s, (pr, pi) in enumerate(steps):
        shift = S5_SUB - (1 << s) if reverse else (1 << s)
        sr, si = pltpu.roll(ar, shift, 0), pltpu.roll(ai, shift, 0)
        ar, ai = ar + pr * sr - pi * si, ai + pr * si + pi * sr
    br, bi = jnp.broadcast_to(cr, ar.shape), jnp.broadcast_to(ci, ai.shape)
    return ar + qr * br - qi * bi, ai + qr * bi + qi * br


def _s5_specs(n5, rev):
    def tok(q, b, c):
        return b * n5 + (n5 - 1 - c if rev else c)

    qcols = S5_COLS // S5_Q
    specs = [
        pl.BlockSpec((S5_CHUNK, LANE), lambda q, b, c: (tok(q, b, c), P_U // LANE + q)),
        pl.BlockSpec((1, LANE, qcols), lambda q, b, c: (q, 0, 0)),
        pl.BlockSpec((1, LANE, qcols), lambda q, b, c: (q, 0, 0)),
        pl.BlockSpec((1, qcols, LANE), lambda q, b, c: (q, 0, 0)),
        pl.BlockSpec((1, qcols, LANE), lambda q, b, c: (q, 0, 0)),
        pl.BlockSpec((S5_STEPS * S5_SUB, qcols), lambda q, b, c: (0, q)),
        pl.BlockSpec((S5_STEPS * S5_SUB, qcols), lambda q, b, c: (0, q)),
        pl.BlockSpec((S5_SUB, qcols), lambda q, b, c: (0, q)),
        pl.BlockSpec((S5_SUB, qcols), lambda q, b, c: (0, q)),
        pl.BlockSpec((1, 1, LANE), lambda q, b, c: (q, 0, 0)),
    ]
    return specs, tok, qcols


def s5_fwd(proj, wb_re, wb_im, wc_re, wc_im, sf_re, sf_im, cf_re, cf_im, dvec, bsz, seq, ride=None):
    t = proj.shape[0]
    n5 = seq // S5_CHUNK
    in_specs, tok, qcols = _s5_specs(n5, False)

    def body(u_ref, wbr, wbi, wcr, wci, sfr, sfi, cfr, cfi, d_ref, y_ref, xr_ref, xi_ref, cr_scr, ci_scr):
        @pl.when(pl.program_id(2) == 0)
        def _():
            cr_scr[...] = jnp.zeros_like(cr_scr)
            ci_scr[...] = jnp.zeros_like(ci_scr)

        u = u_ref[...]
        bur, bui = _dg(u, wbr[0], 1, 0), _dg(u, wbi[0], 1, 0)
        coefs = _s5_coefs(sfr, sfi, cfr, cfi, False)
        cr, ci = cr_scr[...], ci_scr[...]
        for r in range(S5_CHUNK // S5_SUB):
            rows = slice(r * S5_SUB, (r + 1) * S5_SUB)
            xr, xi = _s5_block_scan(bur[rows], bui[rows], coefs, cr, ci, False)
            xr_ref[rows, :], xi_ref[rows, :] = xr, xi
            cr, ci = xr[S5_SUB - 1:, :], xi[S5_SUB - 1:, :]
        cr_scr[...], ci_scr[...] = cr, ci
        y_ref[...] = _dg(xr_ref[...], wcr[0], 1, 0) - _dg(xi_ref[...], wci[0], 1, 0) + u * d_ref[0]

    def tile(w):
        return pl.BlockSpec((S5_CHUNK, w), lambda q, b, c: (tok(q, b, c), q))

    return hosted_call(
        body, name="s5_fwd", grid=(S5_Q, bsz, n5), in_specs=in_specs,
        out_specs=[tile(LANE), tile(qcols), tile(qcols)],
        out_shape=[jax.ShapeDtypeStruct((t, S5_WIDTH), F32), jax.ShapeDtypeStruct((t, S5_COLS), F32),
                   jax.ShapeDtypeStruct((t, S5_COLS), F32)],
        scratch=[pltpu.VMEM((1, qcols), F32)] * 2,
        args=(proj, wb_re, wb_im, wc_re, wc_im, sf_re, sf_im, cf_re, cf_im, dvec), ride=ride)


def s5_bwd(proj, wb_re, wb_im, wc_re, wc_im, sb_re, sb_im, cb_re, cb_im, dvec, xr_all, xi_all, dy, bsz, seq,
           ride=None):
    t = proj.shape[0]
    n5 = seq // S5_CHUNK
    in_specs, tok, qcols = _s5_specs(n5, True)
    blocks = S5_CHUNK // HALO

    def prev_rows(q, b, c):
        return (jnp.maximum(tok(q, b, c) * blocks - 1, 0), q)

    in_specs += [pl.BlockSpec((S5_CHUNK, qcols), lambda q, b, c: (tok(q, b, c), q)),
                 pl.BlockSpec((S5_CHUNK, qcols), lambda q, b, c: (tok(q, b, c), q)),
                 pl.BlockSpec((HALO, qcols), prev_rows), pl.BlockSpec((HALO, qcols), prev_rows),
                 pl.BlockSpec((S5_CHUNK, LANE), lambda q, b, c: (tok(q, b, c), q))]

    def body(u_ref, wbr, wbi, wcr, wci, sbr, sbi, cbr, cbi, d_ref, xr_ref, xi_ref, pr_ref, pi_ref, dy_ref,
             du_ref, dwbr, dwbi, dwcr, dwci, dar, dai, dd_ref, gr_scr, gi_scr, gr_all, gi_all):
        b, c = pl.program_id(1), pl.program_id(2)

        @pl.when(c == 0)
        def _():
            gr_scr[...] = jnp.zeros_like(gr_scr)
            gi_scr[...] = jnp.zeros_like(gi_scr)

        @pl.when((b == 0) & (c == 0))
        def _():
            for r in (dwbr, dwbi, dwcr, dwci, dar, dai, dd_ref):
                r[...] = jnp.zeros_like(r)

        u, dy_v = u_ref[...], dy_ref[...]
        g0r, g0i = _dg(dy_v, wcr[0], 1, 1), -_dg(dy_v, wci[0], 1, 1)
        coefs = _s5_coefs(sbr, sbi, cbr, cbi, True)
        cr, ci = gr_scr[...], gi_scr[...]
        for r in reversed(range(S5_CHUNK // S5_SUB)):
            rows = slice(r * S5_SUB, (r + 1) * S5_SUB)
            br, bi = _s5_block_scan(g0r[rows], g0i[rows], coefs, cr, ci, True)
            gr_all[rows, :], gi_all[rows, :] = br, bi
            cr, ci = br[:1, :], bi[:1, :]
        gr_scr[...], gi_scr[...] = cr, ci
        gr, gi = gr_all[...], gi_all[...]

        row = lax.broadcasted_iota(jnp.int32, (S5_CHUNK, 1), 0)
        xr, xi = xr_ref[...], xi_ref[...]
        is_first = (c == n5 - 1)
        hr = jnp.where(is_first, 0.0, pr_ref[...][HALO - 1:, :])
        hi = jnp.where(is_first, 0.0, pi_ref[...][HALO - 1:, :])
        xpr = jnp.where(row >= 1, pltpu.roll(xr, 1, 0), hr)
        xpi = jnp.where(row >= 1, pltpu.roll(xi, 1, 0), hi)
        dar[0] += jnp.sum(xpr * gr + xpi * gi, axis=0, keepdims=True)
        dai[0] += jnp.sum(xpr * gi - xpi * gr, axis=0, keepdims=True)
        du_ref[...] = _dg(gr, wbr[0], 1, 1) + _dg(gi, wbi[0], 1, 1) + dy_v * d_ref[0]
        dwbr[0] += _dg(u, gr, 0, 0)
        dwbi[0] += _dg(u, gi, 0, 0)
        dwcr[0] += _dg(xr, dy_v, 0, 0)
        dwci[0] -= _dg(xi, dy_v, 0, 0)
        dd_ref[0] += jnp.sum(dy_v * u, axis=0, keepdims=True)

    def acc(shape):
        return pl.BlockSpec((1,) + shape, lambda q, b, c: (q, 0, 0))

    return hosted_call(
        body, name="s5_bwd", grid=(S5_Q, bsz, n5), in_specs=in_specs,
        out_specs=[pl.BlockSpec((S5_CHUNK, LANE), lambda q, b, c: (tok(q, b, c), q)),
                   acc((LANE, qcols)), acc((LANE, qcols)), acc((qcols, LANE)), acc((qcols, LANE)),
                   acc((1, qcols)), acc((1, qcols)), acc((1, LANE))],
        out_shape=[jax.ShapeDtypeStruct((t, S5_WIDTH), F32),
                   jax.ShapeDtypeStruct((S5_Q, LANE, qcols), F32), jax.ShapeDtypeStruct((S5_Q, LANE, qcols), F32),
                   jax.ShapeDtypeStruct((S5_Q, qcols, LANE), F32), jax.ShapeDtypeStruct((S5_Q, qcols, LANE), F32),
                   jax.ShapeDtypeStruct((S5_Q, 1, qcols), F32), jax.ShapeDtypeStruct((S5_Q, 1, qcols), F32),
                   jax.ShapeDtypeStruct((S5_Q, 1, LANE), F32)],
        scratch=[pltpu.VMEM((1, qcols), F32)] * 2 + [pltpu.VMEM((S5_CHUNK, qcols), F32)] * 2,
        args=(proj, wb_re, wb_im, wc_re, wc_im, sb_re, sb_im, cb_re, cb_im, dvec, xr_all, xi_all, xr_all, xi_all, dy),
        ride=ride)


def _blockdiag_b(bb):
    b4 = bb.reshape(S5_Q, 8, S5_STATE, S5_GROUP_CH)
    eye = jnp.eye(8, dtype=bb.dtype)
    w = jnp.einsum("qgph,gk->qghkp", b4, eye)
    return w.reshape(S5_Q, LANE, S5_COLS // S5_Q)


def _unblock_b(dw):
    d = dw.reshape(S5_Q, 8, S5_GROUP_CH, 8, S5_STATE)
    d = jnp.einsum("qghgp->qgph", d)
    return d.reshape(S5_COLS, S5_GROUP_CH)


def _blockdiag_c(cc):
    c4 = cc.reshape(S5_Q, 8, S5_GROUP_CH, S5_STATE)
    eye = jnp.eye(8, dtype=cc.dtype)
    w = jnp.einsum("qghp,gk->qgpkh", c4, eye)
    return w.reshape(S5_Q, S5_COLS // S5_Q, LANE)


def _unblock_c(dw):
    d = dw.reshape(S5_Q, 8, S5_STATE, 8, S5_GROUP_CH)
    d = jnp.einsum("qgpgh->qghp", d)
    return d.reshape(S5_GROUPS, S5_GROUP_CH, S5_STATE)


def ada_fwd(c_all, w_loc, b_loc):
    def body(c_ref, w_ref, b_ref, o_ref):
        o_ref[...] = _dg(_silu(c_ref[...]), w_ref[...], 1, 0) + b_ref[...]

    return pl.pallas_call(body, name="ada_fwd",
                          out_shape=jax.ShapeDtypeStruct((c_all.shape[0], w_loc.shape[1]), F32),
                          compiler_params=_cp())(c_all, w_loc, b_loc)


def ada_bwd(c_all, dmod_all, dmod_cols):
    def body(c_ref, da_ref, dc_ref, gb_ref, gw_ref):
        gb_ref[...] = jnp.sum(da_ref[...], axis=0, keepdims=True)
        gw_ref[...] = _dg(_silu(c_ref[...]), dc_ref[...], 0, 0)

    return pl.pallas_call(body, name="ada_bwd",
                          out_shape=[jax.ShapeDtypeStruct((1, dmod_all.shape[1]), F32),
                                     jax.ShapeDtypeStruct((c_all.shape[1], dmod_cols.shape[1]), F32)],
                          compiler_params=_cp())(c_all, dmod_all, dmod_cols)


_FLIPS = [(0, 0, 1), (1, 0, 0), (0, 1, 0), (1, 1, 0), (1, 0, 1), (0, 1, 1), (1, 1, 1)]


def _exchange_ops(srcs, outs, sems, gather):
    n = len(srcs)
    send_sems, recv_sems, loc_sems = sems
    x, y, c = lax.axis_index("x"), lax.axis_index("y"), lax.axis_index("c")
    me = 4 * x + 2 * y + c
    peers = []
    for fx, fy, fc in _FLIPS:
        px, py, pc = (1 - x if fx else x), (1 - y if fy else y), (1 - c if fc else c)
        peers.append(((px, py, pc), 4 * px + 2 * py + pc))

    def copy(k, j, slot_src, slot_dst):
        src = srcs[k] if gather[k] else srcs[k].at[slot_src]
        return pltpu.make_async_remote_copy(src_ref=src, dst_ref=outs[k].at[slot_dst],
                                            send_sem=send_sems.at[k, j], recv_sem=recv_sems.at[k, j],
                                            device_id=peers[j][0], device_id_type=MESH)

    def local(k):
        own = srcs[k] if gather[k] else srcs[k].at[me]
        return pltpu.make_async_copy(own, outs[k].at[me], loc_sems.at[k])

    def start():
        for k in range(n):
            for j in range(N_DEV - 1):
                copy(k, j, peers[j][1], me).start()
            local(k).start()

    def wait():
        for k in range(n):
            for j in range(N_DEV - 1):
                copy(k, j, me, peers[j][1]).wait_recv()
        for k in range(n):
            for j in range(N_DEV - 1):
                copy(k, j, peers[j][1], me).wait_send()
            local(k).wait()

    return start, wait


def _gather_two_level(srcs, outs, sems):
    n = len(srcs)
    send_sems, recv_sems, loc_sems = sems
    x, y, c = lax.axis_index("x"), lax.axis_index("y"), lax.axis_index("c")
    slot = lambda px, py, pc: 4 * px + 2 * py + pc
    me, sibling = (x, y, c), (x, y, 1 - c)
    chips = [(1 - x, y), (x, 1 - y), (1 - x, 1 - y)]

    def copy(k, j, block, to, own=False):
        return pltpu.make_async_remote_copy(src_ref=srcs[k] if own else outs[k].at[slot(*block)],
                                            dst_ref=outs[k].at[slot(*block)],
                                            send_sem=send_sems.at[k, j], recv_sem=recv_sems.at[k, j],
                                            device_id=to, device_id_type=MESH)

    locs = [pltpu.make_async_copy(srcs[k], outs[k].at[slot(*me)], loc_sems.at[k]) for k in range(n)]
    for k in range(n):
        locs[k].start()
        copy(k, 0, me, sibling, own=True).start()
        for j, chip in enumerate(chips):
            copy(k, 1 + j, me, (*chip, c), own=True).start()
    for j, chip in enumerate(chips):
        for k in range(n):
            copy(k, 1 + j, (*chip, c), me).wait_recv()
            copy(k, 4 + j, (*chip, c), sibling).start()
    for k in range(n):
        copy(k, 0, sibling, me).wait_recv()
        for j, chip in enumerate(chips):
            copy(k, 4 + j, (*chip, 1 - c), me).wait_recv()
    for k in range(n):
        copy(k, 0, me, sibling, own=True).wait_send()
        for j, chip in enumerate(chips):
            copy(k, 1 + j, me, (*chip, c), own=True).wait_send()
            copy(k, 4 + j, (*chip, c), sibling).wait_send()
        locs[k].wait()


def gather_two_level(name, arrs):
    n = len(arrs)
    specs, shapes, sems = _exchange_parts(arrs, [True] * n)

    def body(*refs):
        _gather_two_level(refs[:n], refs[n:2 * n], refs[2 * n:])

    return pl.pallas_call(
        body, name=name, in_specs=specs, out_specs=specs, out_shape=shapes, scratch_shapes=sems,
        compiler_params=pltpu.CompilerParams(has_side_effects=True),
    )(*arrs)


def _exchange_parts(arrs, gather):
    n = len(arrs)
    any_spec = pl.BlockSpec(memory_space=pl.ANY)
    shapes = [jax.ShapeDtypeStruct(((N_DEV,) + a.shape) if g else a.shape, a.dtype) for a, g in zip(arrs, gather)]
    sems = [pltpu.SemaphoreType.DMA((n, N_DEV - 1)), pltpu.SemaphoreType.DMA((n, N_DEV - 1)),
            pltpu.SemaphoreType.DMA((n,))]
    return [any_spec] * n, shapes, sems


def exchange(name, arrs, gather):
    n = len(arrs)
    specs, shapes, sems = _exchange_parts(arrs, gather)

    def body(*refs):
        start, wait = _exchange_ops(refs[:n], refs[n:2 * n], refs[2 * n:], gather)
        start()
        wait()

    return pl.pallas_call(
        body, name=name, in_specs=specs, out_specs=specs, out_shape=shapes, scratch_shapes=sems,
        compiler_params=pltpu.CompilerParams(has_side_effects=True),
    )(*arrs)


def hosted_call(body, *, name, grid, in_specs, out_specs, out_shape, args, scratch=(), ride=None):
    sem = ("arbitrary",) * len(grid)
    if ride is None:
        res = pl.pallas_call(body, name=name, grid=grid, in_specs=in_specs, out_specs=out_specs, out_shape=out_shape,
                             scratch_shapes=list(scratch), compiler_params=_cp(*sem))(*args)
        return list(res), []
    arrs, gather = ride
    n, n_in, n_out, n_scr = len(arrs), len(in_specs), len(out_specs), len(scratch)
    specs, shapes, sems = _exchange_parts(arrs, gather)

    def both(*refs):
        ins, srcs = refs[:n_in], refs[n_in:n_in + n]
        outs, landed = refs[n_in + n:n_in + n + n_out], refs[n_in + n + n_out:n_in + 2 * n + n_out]
        scr, ex_sems = refs[n_in + 2 * n + n_out:n_in + 2 * n + n_out + n_scr], refs[n_in + 2 * n + n_out + n_scr:]
        start, wait = _exchange_ops(srcs, landed, ex_sems, gather)
        first = functools.reduce(lambda a, b: a & b, [pl.program_id(d) == 0 for d in range(len(grid))])
        last = functools.reduce(lambda a, b: a & b, [pl.program_id(d) == grid[d] - 1 for d in range(len(grid))])
        pl.when(first)(start)
        body(*ins, *outs, *scr)
        pl.when(last)(wait)

    res = pl.pallas_call(
        both, name=name, grid=grid, in_specs=list(in_specs) + specs, out_specs=list(out_specs) + specs,
        out_shape=list(out_shape) + shapes, scratch_shapes=list(scratch) + sems, compiler_params=_cp(*sem),
    )(*args, *arrs)
    return list(res[:n_out]), list(res[n_out:])


def sum_slots(name, slots, tr):
    _, r, c = slots.shape

    def body(s_ref, o_ref):
        acc = s_ref[0].astype(F32)
        for j in range(1, N_DEV):
            acc = acc + s_ref[j].astype(F32)
        o_ref[...] = acc

    return pl.pallas_call(
        body, name=name, grid=(r // tr,), in_specs=[pl.BlockSpec((N_DEV, tr, c), lambda i: (0, i, 0))],
        out_specs=pl.BlockSpec((tr, c), lambda i: (i, 0)), out_shape=jax.ShapeDtypeStruct((r, c), F32),
        compiler_params=_cp("parallel"),
    )(slots)


def adamw(name, g, w, m, v, tr):
    slots = g.ndim == 3
    r, c = w.shape
    c1, c2 = 1.0 - ADAM_B1 ** ADAM_STEP, 1.0 - ADAM_B2 ** ADAM_STEP

    def body(g_ref, w_ref, m_ref, v_ref, go, do, mo, vo):
        if slots:
            gg = g_ref[0].astype(F32)
            for j in range(1, N_DEV):
                gg = gg + g_ref[j].astype(F32)
        else:
            gg = g_ref[...]
        mn = ADAM_B1 * m_ref[...] + (1.0 - ADAM_B1) * gg
        vn = ADAM_B2 * v_ref[...] + (1.0 - ADAM_B2) * (gg * gg)
        go[...], mo[...], vo[...] = gg, mn, vn
        do[...] = -ADAM_LR * ((mn / c1) / (jnp.sqrt(vn / c2) + ADAM_EPS) + ADAM_WD * w_ref[...])

    blk = pl.BlockSpec((tr, c), lambda i: (i, 0))
    gspec = pl.BlockSpec((N_DEV, tr, c), lambda i: (0, i, 0)) if slots else blk
    return pl.pallas_call(
        body, name=name, grid=(r // tr,), in_specs=[gspec, blk, blk, blk], out_specs=[blk] * 4,
        out_shape=[jax.ShapeDtypeStruct((r, c), F32)] * 4, compiler_params=_cp("parallel"),
    )(g, w, m, v)


def _lane_rows(n):
    return -(-n // (8 * LANE)) * 8


def _pack(arrs):
    pieces = []
    for a in arrs:
        n = math.prod(a.shape)
        flat = a.reshape(-1).astype(F32)
        pieces.append(jnp.pad(flat, (0, _lane_rows(n) * LANE - n)).reshape(_lane_rows(n), LANE))
    return jnp.concatenate(pieces, axis=0)


def _unpack(buf, shapes):
    out, off = [], 0
    for s in shapes:
        n = math.prod(s)
        out.append(buf[off:off + _lane_rows(n)].reshape(-1)[:n].reshape(s))
        off += _lane_rows(n)
    return out


FF_CHUNK = D_FF
DW_TOKENS = 2048
FFN_TM = 256


def _resident(shape):
    return pl.BlockSpec(shape, lambda i: (0,) * len(shape), pipeline_mode=pl.Buffered(1))


def _ffn_fwd(tag, x, sc, sh, g, w1, w3, w2, lg, lb, seq, tm, ride=None, target=None):
    t = x.shape[0]
    tm = min(FFN_TM, tm)
    tps = seq // tm
    ln = _res_ln(0.5)
    head = target is not None

    def body(x_ref, sc_ref, sh_ref, g_ref, lg_ref, lb_ref, w1_ref, w3_ref, w2_ref, *rest):
        if head:
            t_ref, y_ref, h_ref, a_ref, b_ref, f_ref, l_ref = rest
        else:
            y_ref, h_ref, a_ref, b_ref, f_ref = rest
        xv = x_ref[...]
        h = (xv * (1.0 + sc_ref[0]) + sh_ref[0]).astype(BF16)
        h_ref[...] = h
        acc = jnp.zeros((tm, D_MODEL), F32)
        for j in range(D_FF // FF_CHUNK):
            sl = slice(j * FF_CHUNK, (j + 1) * FF_CHUNK)
            a = _dg(h, w1_ref[sl, :], 1, 1)
            b = _dg(h, w3_ref[sl, :], 1, 1)
            a_ref[:, sl] = a
            b_ref[:, sl] = b
            acc = acc + _dg(_silu(a) * b, w2_ref[sl, :], 1, 0)
        f_ref[...] = acc
        y = ln(xv, acc, g_ref[0], lg_ref[...], lb_ref[...])[0]
        if head:
            @pl.when(pl.program_id(0) == 0)
            def _():
                l_ref[...] = jnp.zeros_like(l_ref)

            e = y - t_ref[...]
            y_ref[...] = e * (1.0 / D_MODEL)
            l_ref[...] += 0.5 * jnp.sum(jnp.mean(e * e, axis=-1, keepdims=True), axis=0, keepdims=True)
        else:
            y_ref[...] = y

    row = lambda c: pl.BlockSpec((tm, c), lambda i: (i, 0))
    per_seq = pl.BlockSpec((1, 1, D_MODEL), lambda i: (i // tps, 0, 0))
    vec = pl.BlockSpec((1, D_MODEL), lambda i: (0, 0))
    res, landed = hosted_call(
        body, name=tag + "_fwd", grid=(t // tm,),
        in_specs=[row(D_MODEL), per_seq, per_seq, per_seq, vec, vec,
                  _resident((D_FF, D_MODEL)), _resident((D_FF, D_MODEL)), _resident((D_FF, D_MODEL))]
        + ([row(D_MODEL)] if head else []),
        out_specs=[row(D_MODEL), row(D_MODEL), row(D_FF), row(D_FF), row(D_MODEL)]
        + ([pl.BlockSpec((1, 1), lambda i: (0, 0))] if head else []),
        out_shape=[jax.ShapeDtypeStruct((t, D_MODEL), F32), jax.ShapeDtypeStruct((t, D_MODEL), BF16),
                   jax.ShapeDtypeStruct((t, D_FF), F32), jax.ShapeDtypeStruct((t, D_FF), F32),
                   jax.ShapeDtypeStruct((t, D_MODEL), F32)] + ([jax.ShapeDtypeStruct((1, 1), F32)] if head else []),
        args=(x, sc, sh, g, lg, lb, w1, w3, w2) + ((target,) if head else ()), ride=ride)
    first = (res[0], res[5][0, 0]) if head else res[0]
    return first, tuple(res[1:5]), landed


def _ffn_bwd(tag, dy, x, sc, sh, g, w1, w3, w2, lg, lb, res, seq, tm, ride=None, chain=None):
    h, a, b, f = res
    t = x.shape[0]
    tmk = min(FFN_TM, tm)
    tps = seq // tmk
    ln = _res_ln(0.5)

    def body(dy_ref, x_ref, f_ref, a_ref, b_ref, sc_ref, sh_ref, g_ref, lg_ref, lb_ref, w1_ref, w3_ref, w2_ref,
             dx_ref, da_ref, db_ref, s_ref, df_ref, dsc_ref, dsh_ref, dg_ref, dlg_ref, dlb_ref):
        i = pl.program_id(0)

        @pl.when(i % tps == 0)
        def _():
            for r in (dsc_ref, dsh_ref, dg_ref):
                r[...] = jnp.zeros_like(r)

        @pl.when(i == 0)
        def _():
            dlg_ref[...] = jnp.zeros_like(dlg_ref)
            dlb_ref[...] = jnp.zeros_like(dlb_ref)

        xv = x_ref[...]
        _, pull = jax.vjp(ln, xv, f_ref[...], g_ref[0], lg_ref[...], lb_ref[...])
        dx_res, df, dg, dlg, dlb = pull((dy_ref[...],))
        dfb = df.astype(BF16)
        df_ref[...] = dfb
        dh = jnp.zeros((tmk, D_MODEL), F32)
        for j in range(D_FF // FF_CHUNK):
            sl = slice(j * FF_CHUNK, (j + 1) * FF_CHUNK)
            ds = _dg(dfb, w2_ref[sl, :], 1, 1)
            av, bv = a_ref[:, sl], b_ref[:, sl]
            sg = jax.nn.sigmoid(av)
            si = av * sg
            s_ref[:, sl] = (si * bv).astype(BF16)
            da = (ds * bv * (sg * (1.0 + av * (1.0 - sg)))).astype(BF16)
            db = (ds * si).astype(BF16)
            da_ref[:, sl] = da
            db_ref[:, sl] = db
            dh = dh + _dg(da, w1_ref[sl, :], 1, 0) + _dg(db, w3_ref[sl, :], 1, 0)
        dx_ref[...] = dx_res + dh * (1.0 + sc_ref[0])
        dsc_ref[0] += jnp.sum(dh * xv, axis=0, keepdims=True)
        dsh_ref[0] += jnp.sum(dh, axis=0, keepdims=True)
        dg_ref[0] += dg
        dlg_ref[...] += dlg
        dlb_ref[...] += dlb

    row = lambda c: pl.BlockSpec((tmk, c), lambda i: (i, 0))
    per_seq = pl.BlockSpec((1, 1, D_MODEL), lambda i: (i // tps, 0, 0))
    vec = pl.BlockSpec((1, D_MODEL), lambda i: (0, 0))
    seq_shape = jax.ShapeDtypeStruct(sc.shape, F32)
    vec_shape = jax.ShapeDtypeStruct((1, D_MODEL), F32)
    (dx, da, db, s, df, dsc, dsh, dg, dlg, dlb), landed = hosted_call(
        body, name=tag + "_bwd", grid=(t // tmk,),
        in_specs=[row(D_MODEL), row(D_MODEL), row(D_MODEL), row(D_FF), row(D_FF), per_seq, per_seq, per_seq, vec, vec,
                  _resident((D_FF, D_MODEL)), _resident((D_FF, D_MODEL)), _resident((D_FF, D_MODEL))],
        out_specs=[row(D_MODEL), row(D_FF), row(D_FF), row(D_FF), row(D_MODEL), per_seq, per_seq, per_seq, vec, vec],
        out_shape=[jax.ShapeDtypeStruct((t, D_MODEL), F32), jax.ShapeDtypeStruct((t, D_FF), BF16),
                   jax.ShapeDtypeStruct((t, D_FF), BF16), jax.ShapeDtypeStruct((t, D_FF), BF16),
                   jax.ShapeDtypeStruct((t, D_MODEL), BF16), seq_shape, seq_shape, seq_shape, vec_shape, vec_shape],
        args=(dy, x, f, a, b, sc, sh, g, lg, lb, w1, w3, w2), ride=ride)
    tt = min(DW_TOKENS, seq)
    shards = lambda dw: dw.reshape(N_DEV, D_FF // N_DEV, D_MODEL)
    if chain is None:
        dw2, landed = mm_tn(tag + "_dw2", s, df, D_FF // 2, D_MODEL, tt, BF16), []
    else:
        dw2, landed = mm_tn(tag + "_dw2", s, df, D_FF // 2, D_MODEL, tt, BF16, ride=chain((dsh, dsc, dg), dlg, dlb))
    dw1, (s_w2,) = mm_tn(tag + "_dw1", da, h, D_FF // 2, D_MODEL, tt, BF16, ride=([shards(dw2)], [False]))
    dw3, (s_w1,) = mm_tn(tag + "_dw3", db, h, D_FF // 2, D_MODEL, tt, BF16, ride=([shards(dw1)], [False]))
    return dx, (dsh, dsc, dg), (s_w1, shards(dw3), s_w2, dlg, dlb), landed


def kernel(x, c, w_ada, b_ada, ffn1_w1, ffn1_w3, ffn1_w2, ln1_g, ln1_b, w_in, conv_w, conv_b, dt_bias, a_log, d_ssd, ssd_norm_w, s5_a_re, s5_a_im, s5_log_dt, s5_b_re, s5_b_im, s5_c_re, s5_c_im, s5_d, w_glu, b_glu, w_out, ln2_g, ln2_b, ffn2_w1, ffn2_w3, ffn2_w2, ln3_g, ln3_b, loss_target, m_w_ada, m_b_ada, m_ffn1_w1, m_ffn1_w3, m_ffn1_w2, m_ln1_g, m_ln1_b, m_w_in, m_conv_w, m_conv_b, m_dt_bias, m_a_log, m_d_ssd, m_ssd_norm_w, m_s5_a_re, m_s5_a_im, m_s5_log_dt, m_s5_b_re, m_s5_b_im, m_s5_c_re, m_s5_c_im, m_s5_d, m_w_glu, m_b_glu, m_w_out, m_ln2_g, m_ln2_b, m_ffn2_w1, m_ffn2_w3, m_ffn2_w2, m_ln3_g, m_ln3_b, v_w_ada, v_b_ada, v_ffn1_w1, v_ffn1_w3, v_ffn1_w2, v_ln1_g, v_ln1_b, v_w_in, v_conv_w, v_conv_b, v_dt_bias, v_a_log, v_d_ssd, v_ssd_norm_w, v_s5_a_re, v_s5_a_im, v_s5_log_dt, v_s5_b_re, v_s5_b_im, v_s5_c_re, v_s5_c_im, v_s5_d, v_w_glu, v_b_glu, v_w_out, v_ln2_g, v_ln2_b, v_ffn2_w1, v_ffn2_w3, v_ffn2_w2, v_ln3_g, v_ln3_b):
    given = dict(locals())
    bsz, seq, _ = x.shape
    t = bsz * seq
    tm = min(1024, seq)
    me = 4 * lax.axis_index("x") + 2 * lax.axis_index("y") + lax.axis_index("c")
    x0 = x.reshape(t, D_MODEL)
    target = loss_target.reshape(t, D_MODEL)

    tr16 = lambda w: w[0].T.astype(BF16)
    whole = lambda g: g.reshape(N_DEV * g.shape[1], g.shape[2])
    g_f1w1, g_f1w3, g_f1w2, g_c = gather_two_level(
        "gather_ffn1", [tr16(ffn1_w1), tr16(ffn1_w3), ffn1_w2[0].astype(BF16), c])
    f1w1, f1w3, f1w2 = whole(g_f1w1), whole(g_f1w3), whole(g_f1w2)
    c_all = whole(g_c)

    n_loc = w_ada.shape[2]
    b_loc = lax.dynamic_slice(b_ada, (0, me * n_loc), (1, n_loc))
    mod_cols = ada_fwd(c_all, w_ada[0], b_loc)
    g_mod, = exchange("gather_mod", [mod_cols], [True])
    mine = lax.dynamic_slice(g_mod, (0, me * bsz, 0), (N_DEV, bsz, n_loc))
    mod = jnp.transpose(mine, (1, 0, 2)).reshape(bsz, N_MOD, 1, D_MODEL)
    sh1, sc1, g1, sh2, sc2, g2, sh3, sc3, g3 = [mod[:, k] for k in range(N_MOD)]

    x1, res1, (g_win, g_glu, g_out, g_conv, g_f2w1) = _ffn_fwd(
        "ffn1", x0, sc1, sh1, g1, f1w1, f1w3, f1w2, ln1_g, ln1_b, seq, tm,
        ride=([tr16(w_in), w_glu[0].astype(BF16), w_out[0].astype(BF16), conv_w[0], tr16(ffn2_w1)], [True] * 5))
    win = whole(g_win)
    wglu = whole(g_glu).astype(F32)
    wout = whole(g_out)
    wo_ssd, wo_s5 = wout[:SSD_WIDTH], wout[SSD_WIDTH:]
    convw = jnp.transpose(g_conv, (1, 0, 2)).reshape(CONV_K, CONV_CH)
    w_z, w_xbc = win[:SSD_WIDTH], win[SSD_WIDTH:SSD_WIDTH + CONV_CH]
    w_dt = win[SSD_WIDTH + CONV_CH:SSD_WIDTH + CONV_CH + SSD_HEADS]
    w_u = win[SSD_WIDTH + CONV_CH + SSD_HEADS:]
    dt_pad = [jnp.pad(w_dt[HEADS_PER_GROUP * g:HEADS_PER_GROUP * (g + 1)], ((0, LANE - HEADS_PER_GROUP), (0, 0)))
              for g in range(SSD_GROUPS)]
    w_dtp = jnp.concatenate(dt_pad, axis=0)
    w_proj = jnp.concatenate([w_xbc, w_z, w_u, w_dtp], axis=0)

    proj, h2 = modulate_proj("mix_proj", x1, sc2, sh2, w_proj, seq, tm, P_COLS // 2)
    xc = conv_fwd(proj, convw, conv_b, seq, tm)
    dtb = jnp.pad(dt_bias.reshape(SSD_GROUPS, 1, HEADS_PER_GROUP), ((0, 0), (0, 0), (0, LANE - HEADS_PER_GROUP)))
    alog = jnp.pad(a_log.reshape(SSD_GROUPS, 1, HEADS_PER_GROUP), ((0, 0), (0, 0), (0, LANE - HEADS_PER_GROUP)))
    dcol = jnp.pad(d_ssd.reshape(SSD_GROUPS, HEADS_PER_GROUP, 1), ((0, 0), (0, LANE - HEADS_PER_GROUP), (0, 0)))
    nw = ssd_norm_w.reshape(SSD_GROUPS, 1, GROUP_COLS)
    (y_ssd, hprev), (g_f2w3,) = ssd_fwd(xc, proj, dtb, alog, dcol, nw, bsz, seq,
                                        ride=([tr16(ffn2_w3)], [True]))

    a_re2, a_im2, ldt2 = s5_a_re[0], s5_a_im[0], s5_log_dt.reshape(S5_GROUPS, 1)
    ab_re, ab_im, f_re, f_im = _whole(_disc_a, "s5_disc_a", [a_re2, a_im2, ldt2], [(S5_GROUPS, S5_STATE)] * 4)
    b_re2, b_im2 = s5_b_re.reshape(S5_COLS, S5_GROUP_CH), s5_b_im.reshape(S5_COLS, S5_GROUP_CH)
    fr_col, fi_col = f_re.reshape(S5_COLS, 1), f_im.reshape(S5_COLS, 1)
    bb_re, bb_im = _whole(_disc_b, "s5_disc_b", [fr_col, fi_col, b_re2, b_im2], [(S5_COLS, S5_GROUP_CH)] * 2)
    wb_re, wb_im = _blockdiag_b(bb_re).astype(BF16), _blockdiag_b(bb_im).astype(BF16)
    wc_re, wc_im = _blockdiag_c(s5_c_re[0]).astype(BF16), _blockdiag_c(s5_c_im[0]).astype(BF16)
    dt5 = jnp.exp(ldt2)
    lam_re, lam_im = (dt5 * a_re2).reshape(1, S5_COLS), (dt5 * a_im2).reshape(1, S5_COLS)
    sf_re, sf_im, sb_re, sb_im, cf_re, cf_im, cb_re, cb_im = s5_tables(lam_re, lam_im)
    d5 = s5_d.reshape(S5_Q, 1, LANE)
    (y5, xr_all, xi_all), (g_f2w2,) = s5_fwd(
        proj, wb_re, wb_im, wc_re, wc_im, sf_re, sf_im, cf_re, cf_im, d5, bsz, seq,
        ride=([ffn2_w2[0].astype(BF16)], [True]))
    f2w1, f2w3, f2w2 = whole(g_f2w1), whole(g_f2w3), whole(g_f2w2)
    o5, = rowwise_fwd("s5_glu", f_glu, [y5], [], [wglu, b_glu], [(S5_WIDTH, F32)], seq, tm)

    mix = mm_nn("mix_out", [y_ssd, o5], [wo_ssd, wo_s5], tm, D_MODEL)
    x2, = rowwise_fwd("mix_ln", _res_ln(1.0), [x1, mix], [g2], [ln2_g, ln2_b], [(D_MODEL, F32)], seq, tm)

    (dy, loss_loc), res3, _ = _ffn_fwd("ffn2", x2, sc3, sh3, g3, f2w1, f2w3, f2w2, ln3_g, ln3_b, seq, tm, target=target)

    dx2, dmod3, (s_f2w1, d_f2w3, s_f2w2, d_ln3g, d_ln3b), _ = _ffn_bwd(
        "ffn2", dy, x2, sc3, sh3, g3, f2w1, f2w3, f2w2, ln3_g, ln3_b, res3, seq, tm)

    (dx1_a, dmix), (dg2,), (d_ln2g, d_ln2b) = rowwise_bwd(
        "mix_ln_b", _res_ln(1.0), [x1, mix], [g2], [ln2_g, ln2_b], [dx2], seq, tm, [F32, BF16])
    tw = min(DW_TOKENS, seq)
    d_wo = jnp.concatenate([mm_tn("mix_dwo_ssd", y_ssd, dmix, SSD_WIDTH, D_MODEL, tw, BF16),
                            mm_tn("mix_dwo_s5", o5, dmix, S5_WIDTH, D_MODEL, tw, BF16)], axis=0)
    dy_mixed = mm_nt("mix_dy", [dmix], [wout], tm, D_MODEL)
    dy_ssd, do5 = dy_mixed, (dy_mixed, SSD_WIDTH, S5_WIDTH)

    (dy5,), _, (d_wglu, d_bglu) = rowwise_bwd("s5_glu_b", f_glu, [y5], [], [wglu, b_glu], [do5], seq, tm, [F32])
    (du, dwbr, dwbi, dwcr, dwci, dab_re, dab_im, dd5), (s_f2w3, s_out, s_glu) = s5_bwd(
        proj, wb_re, wb_im, wc_re, wc_im, sb_re, sb_im, cb_re, cb_im, d5, xr_all, xi_all, dy5, bsz, seq,
        ride=([d_f2w3, d_wo.reshape(N_DEV, D_MODEL // N_DEV, D_MODEL),
               d_wglu.reshape(N_DEV, S5_WIDTH // N_DEV, S5_WIDTH).astype(BF16)], [False] * 3))
    dbb_re, dbb_im = _unblock_b(dwbr), _unblock_b(dwbi)
    dfr_col, dfi_col, d_b_re, d_b_im = _whole_vjp(_disc_b, "s5_disc_b_b", [fr_col, fi_col, b_re2, b_im2],
                                                  [dbb_re, dbb_im])
    d_a_re, d_a_im, d_ldt = _whole_vjp(
        _disc_a, "s5_disc_a_b", [a_re2, a_im2, ldt2],
        [dab_re.reshape(S5_GROUPS, S5_STATE), dab_im.reshape(S5_GROUPS, S5_STATE),
         dfr_col.reshape(S5_GROUPS, S5_STATE), dfi_col.reshape(S5_GROUPS, S5_STATE)])
    d_c_re, d_c_im = _unblock_c(dwcr), _unblock_c(dwci)

    dxs, dbm, dcm, ddt, dz, ddtb, dalog, ddcol, dnw = ssd_bwd(xc, proj, dtb, alog, dcol, nw, hprev, dy_ssd, bsz, seq)
    dpre, d_convw, d_convb = conv_bwd_pre(proj, convw, conv_b, dxs, dbm, dcm, seq, tm)
    dxbc = conv_bwd_x(dpre, convw, seq, tm)

    dw_xbc = mm_tn("mix_dw_xbc", dxbc, h2, CONV_CH, D_MODEL, tw, BF16)
    dw_z = mm_tn("mix_dw_z", dz, h2, SSD_WIDTH, D_MODEL, tw, BF16)
    dw_u = mm_tn("mix_dw_u", du, h2, S5_WIDTH, D_MODEL, tw, BF16)
    dw_dt = mm_tn("mix_dw_dt", ddt, h2, 2 * LANE, D_MODEL, tw, BF16)
    dw_dt8 = jnp.concatenate([dw_dt[LANE * g:LANE * g + HEADS_PER_GROUP] for g in range(SSD_GROUPS)], axis=0)
    d_win = jnp.concatenate([dw_z, dw_xbc, dw_dt8, dw_u], axis=0)
    dh2, (s_win,) = mm_nn("mix_dh", [dxbc, dz, du, ddt], [w_xbc, w_z, w_u, w_dtp], tm, D_MODEL,
                          ride=([d_win.reshape(N_DEV, IN_COLS // N_DEV, D_MODEL)], [False]))
    (dx1,), (dsc2, dsh2), _ = rowwise_bwd("mix_mod_b", f_modulate, [x1], [sc2, sh2], [], [dh2], seq, tm, [F32],
                                          add_rows={0: dx1_a})

    packing = {}

    def small_and_dmod(dmod1, d_ln1g, d_ln1b):
        dmod = jnp.concatenate(list(dmod1) + [dsh2, dsc2, dg2] + list(dmod3), axis=1).reshape(bsz, N_MOD * D_MODEL)
        small = _small_grads(d_ln1g, d_ln1b)
        packing["names"] = list(small)
        packing["shapes"] = [small[k].shape for k in small]
        return [_pack(list(small.values())), dmod], [True, True]

    def _small_grads(d_ln1g, d_ln1b):
        return {
            "ln1_g": d_ln1g, "ln1_b": d_ln1b, "conv_w": d_convw, "conv_b": d_convb,
            "dt_bias": ddtb[:, 0, :HEADS_PER_GROUP].reshape(1, SSD_HEADS),
            "a_log": dalog[:, 0, :HEADS_PER_GROUP].reshape(1, SSD_HEADS),
            "d_ssd": ddcol[:, :HEADS_PER_GROUP, 0].reshape(1, SSD_HEADS),
            "ssd_norm_w": dnw.reshape(1, SSD_WIDTH),
            "s5_a_re": d_a_re[None], "s5_a_im": d_a_im[None], "s5_log_dt": d_ldt.reshape(1, S5_GROUPS),
            "s5_b_re": d_b_re.reshape(s5_b_re.shape), "s5_b_im": d_b_im.reshape(s5_b_im.shape),
            "s5_c_re": d_c_re[None], "s5_c_im": d_c_im[None], "s5_d": dd5.reshape(1, S5_WIDTH),
            "b_glu": d_bglu, "ln2_g": d_ln2g, "ln2_b": d_ln2b, "ln3_g": d_ln3g, "ln3_b": d_ln3b,
            "loss": loss_loc.reshape(1, 1),
        }

    dx0, _, (s_f1w1, d_f1w3, s_f1w2, _, _), (s_small, s_dmod) = _ffn_bwd(
        "ffn1", dx1, x0, sc1, sh1, g1, f1w1, f1w3, f1w2, ln1_g, ln1_b, res1, seq, tm, chain=small_and_dmod)
    names, shapes = packing["names"], packing["shapes"]
    s_f1w3, = exchange("sum_grads", [d_f1w3], [False])

    out = {"grad_x": dx0.reshape(x.shape)}

    def put(name, res, shape):
        for key, val in zip(("grad_", "delta_", "new_m_", "new_v_"), res):
            out[key + name] = val.reshape(shape)

    for name, slots, tr in (("ffn1_w1", s_f1w1, 176), ("ffn1_w3", s_f1w3, 176), ("ffn2_w1", s_f2w1, 176),
                            ("ffn2_w3", s_f2w3, 176), ("w_in", s_win, IN_COLS // N_DEV)):
        w = given[name]
        grad = sum_slots("sum_" + name, slots, tr).T
        put(name, adamw("adam_" + name, grad, w[0], given["m_" + name][0], given["v_" + name][0], 256), w.shape)
    for name, slots in (("ffn1_w2", s_f1w2), ("ffn2_w2", s_f2w2)):
        w = given[name]
        put(name, adamw("adam_" + name, slots, w[0], given["m_" + name][0], given["v_" + name][0], 176), w.shape)
    put("w_glu", adamw("adam_w_glu", s_glu, w_glu[0], m_w_glu[0], v_w_glu[0], 64), w_glu.shape)
    put("w_out", adamw("adam_w_out", s_out, w_out[0], m_w_out[0], v_w_out[0], 128), w_out.shape)

    dmod_all = s_dmod.reshape(N_DEV * bsz, N_MOD * D_MODEL)
    g_bada, g_wada = ada_bwd(c_all, dmod_all, lax.dynamic_slice(dmod_all, (0, me * n_loc), (N_DEV * bsz, n_loc)))
    put("w_ada", adamw("adam_w_ada", g_wada, w_ada[0], m_w_ada[0], v_w_ada[0], 256), w_ada.shape)
    put("b_ada", adamw("adam_b_ada", g_bada, b_ada, m_b_ada, v_b_ada, 1), b_ada.shape)

    not_params = {"conv_w": jnp.zeros((CONV_K, CONV_CH), F32), "loss": jnp.zeros((1, 1), F32)}
    pw, pm, pv = [_pack([not_params[k] if k in not_params else given[pre + k] for k in names]) for pre in ("", "m_", "v_")]
    res_small = adamw("adam_small", s_small, pw, pm, pv, pw.shape[0])
    parts = [_unpack(r, shapes) for r in res_small]
    for i, k in enumerate(names):
        if k not in not_params:
            put(k, [p[i] for p in parts], given[k].shape)
    out["loss"] = parts[0][names.index("loss")][0, 0]
    g_cw = lax.dynamic_slice(parts[0][names.index("conv_w")], (0, me * LANE), (CONV_K, LANE))
    put("conv_w", adamw("adam_conv_w", g_cw, conv_w[0], m_conv_w[0], v_conv_w[0], CONV_K), conv_w.shape)

    order = ["w_ada", "b_ada", "ffn1_w1", "ffn1_w3", "ffn1_w2", "ln1_g", "ln1_b", "w_in", "conv_w", "conv_b", "dt_bias",
             "a_log", "d_ssd", "ssd_norm_w", "s5_a_re", "s5_a_im", "s5_log_dt", "s5_b_re", "s5_b_im", "s5_c_re",
             "s5_c_im", "s5_d", "w_glu", "b_glu", "w_out", "ln2_g", "ln2_b", "ffn2_w1", "ffn2_w3", "ffn2_w2", "ln3_g",
             "ln3_b"]
    return (out["loss"], out["grad_x"], *[out[p + n] for p in ("grad_", "delta_", "new_m_", "new_v_") for n in order])
```

```python
import functools
import math

import jax
import jax.numpy as jnp
from jax import lax
from jax.experimental import pallas as pl
from jax.experimental.pallas import tpu as pltpu

F32 = jnp.float32
BF16 = jnp.bfloat16
MESH = pl.DeviceIdType.MESH

N_DEV = 8
D_MODEL = 1024
D_FF = 2816
N_MOD = 9
SSD_WIDTH = 512
SSD_HEADS = 8
SSD_HEAD_DIM = 64
SSD_GROUPS = 2
SSD_STATE = 128
SSD_CHUNK = 128
GROUP_COLS = SSD_WIDTH // SSD_GROUPS
HEADS_PER_GROUP = SSD_HEADS // SSD_GROUPS
CONV_K = 4
CONV_CH = 1024
S5_WIDTH = 512
S5_GROUPS = 32
S5_GROUP_CH = 16
S5_STATE = 64
S5_COLS = S5_GROUPS * S5_STATE
S5_Q = 4
S5_CHUNK = 2048
ALPHA = 2.0 ** 0.25
LN_EPS = 1e-5
LANE = 128
HALO = 8

P_XBC, P_Z, P_U, P_DT = 0, 1024, 1536, 2048
P_COLS = 2048 + SSD_GROUPS * LANE
IN_COLS = SSD_WIDTH + CONV_CH + SSD_HEADS + S5_WIDTH

ADAM_LR, ADAM_B1, ADAM_B2, ADAM_EPS, ADAM_WD, ADAM_STEP = 0.001, 0.9, 0.999, 1e-08, 0.01, 10

VMEM_LIMIT = 56 * 1024 * 1024


def _cp(*sem):
    return pltpu.CompilerParams(dimension_semantics=sem if sem else None, vmem_limit_bytes=VMEM_LIMIT)


def _dg(a, b, ca, cb):
    return lax.dot_general(a.astype(BF16), b.astype(BF16), (((ca,), (cb,)), ((), ())), preferred_element_type=F32)


@jax.custom_vjp
def bdot_nn(a, b):
    return _dg(a, b, 1, 0)


bdot_nn.defvjp(lambda a, b: (_dg(a, b, 1, 0), (a, b)),
               lambda r, g: (_dg(g, r[1], 1, 1), _dg(r[0], g, 0, 0)))


@jax.custom_vjp
def bdot_nt(a, b):
    return _dg(a, b, 1, 1)


bdot_nt.defvjp(lambda a, b: (_dg(a, b, 1, 1), (a, b)),
               lambda r, g: (_dg(g, r[1], 1, 0), _dg(g, r[0], 0, 0)))


@jax.custom_vjp
def bdot_tn(a, b):
    return _dg(a, b, 0, 0)


bdot_tn.defvjp(lambda a, b: (_dg(a, b, 0, 0), (a, b)),
               lambda r, g: (_dg(r[1], g, 1, 1), _dg(r[0], g, 1, 0)))


def _split3(x):
    def top(v):
        bits = lax.bitcast_convert_type(v, jnp.int32) & jnp.int32(-65536)
        return lax.bitcast_convert_type(bits, F32)

    hi = top(x)
    r1 = x - hi
    mid = top(r1)
    return hi.astype(BF16), mid.astype(BF16), (r1 - mid).astype(BF16)


def _dot3(a, b, ca, cb, split_a):
    dims = (((ca,), (cb,)), ((), ()))
    if split_a:
        c = b.astype(BF16)
        return sum(lax.dot_general(p, c, dims, preferred_element_type=F32) for p in _split3(a))
    c = a.astype(BF16)
    return sum(lax.dot_general(c, p, dims, preferred_element_type=F32) for p in _split3(b))


@jax.custom_vjp
def mask_dot_left(c, x):
    return _dot3(c, x, 1, 0, False)


mask_dot_left.defvjp(lambda c, x: (_dot3(c, x, 1, 0, False), c),
                     lambda c, g: (jnp.zeros_like(c), _dot3(c, g, 0, 0, False)))


@jax.custom_vjp
def mask_dot_right(x, c):
    return _dot3(x, c, 1, 0, True)


mask_dot_right.defvjp(lambda x, c: (_dot3(x, c, 1, 0, True), c),
                      lambda c, g: (_dot3(g, c, 1, 1, True), jnp.zeros_like(c)))


def _take_col(z):
    @jax.custom_vjp
    def take(x):
        return x[:, z:z + 1]

    def bwd(shape, g):
        hot = (lax.broadcasted_iota(jnp.int32, (1, shape[1]), 1) == z).astype(F32)
        return (g * hot,)

    take.defvjp(lambda x: (x[:, z:z + 1], x.shape), bwd)
    return take


def _take_row(z):
    @jax.custom_vjp
    def take(x):
        return x[z:z + 1, :]

    def bwd(shape, g):
        hot = (lax.broadcasted_iota(jnp.int32, (shape[0], 1), 0) == z).astype(F32)
        return (hot * g,)

    take.defvjp(lambda x: (x[z:z + 1, :], x.shape), bwd)
    return take


def _view(a):
    return a if isinstance(a, tuple) else (a, 0, a.shape[1])


def _col_spec(view, rows, width, index):
    _, off, _ = view
    assert off % width == 0
    return pl.BlockSpec((rows, width), lambda *g: (index(*g)[0], off // width + index(*g)[1]))


def _rw_in_specs(rows, bps, gps, tm, tps):
    specs = [_col_spec(_view(r), tm, _view(r)[2], lambda i: (i, 0)) for r in rows]
    specs += [pl.BlockSpec((1, 1, b.shape[2]), lambda i: (i // tps, 0, 0)) for b in bps]
    specs += [pl.BlockSpec(g.shape, lambda i, nd=g.ndim: (0,) * nd) for g in gps]
    return specs


def _rw_vals(refs, nr, nb, ng):
    vals = [r[...] for r in refs[:nr]]
    vals += [b[0] for b in refs[nr:nr + nb]]
    vals += [g[...] for g in refs[nr + nb:nr + nb + ng]]
    return vals


def rowwise_fwd(name, f, rows, bps, gps, outs, seq, tm):
    t = _view(rows[0])[0].shape[0]
    tps = seq // tm
    nr, nb, ng = len(rows), len(bps), len(gps)

    def body(*refs):
        res = f(*_rw_vals(refs, nr, nb, ng))
        for o, v in zip(refs[nr + nb + ng:], res):
            o[...] = v.astype(o.dtype)

    return pl.pallas_call(
        body, name=name, grid=(t // tm,),
        in_specs=_rw_in_specs(rows, bps, gps, tm, tps),
        out_specs=[pl.BlockSpec((tm, c), lambda i: (i, 0)) for c, _ in outs],
        out_shape=[jax.ShapeDtypeStruct((t, c), d) for c, d in outs],
        compiler_params=_cp("arbitrary"),
    )(*[_view(r)[0] for r in rows], *bps, *gps)


def rowwise_bwd(name, f, rows, bps, gps, douts, seq, tm, row_grads, add_rows=None):
    add_rows = add_rows or {}
    t = _view(rows[0])[0].shape[0]
    tps = seq // tm
    nr, nb, ng, nd = len(rows), len(bps), len(gps), len(douts)
    want = [k for k in range(nr) if row_grads[k] is not None]
    adds = sorted(add_rows)
    n_in = nr + nb + ng + nd + len(adds)

    def body(*refs):
        vals = _rw_vals(refs, nr, nb, ng)
        dvals = tuple(r[...] for r in refs[nr + nb + ng:nr + nb + ng + nd])
        add_refs = dict(zip(adds, refs[nr + nb + ng + nd:n_in]))
        out_refs = refs[n_in:]
        _, pull = jax.vjp(f, *vals)
        grads = pull(dvals)
        i = pl.program_id(0)
        for o, k in zip(out_refs, want):
            g = grads[k]
            if k in add_refs:
                g = g + add_refs[k][...]
            o[...] = g.astype(o.dtype)
        for j in range(nb):
            o = out_refs[len(want) + j]

            @pl.when(i % tps == 0)
            def _(o=o):
                o[...] = jnp.zeros_like(o)

            o[0] = o[0] + grads[nr + j]
        for j in range(ng):
            o = out_refs[len(want) + nb + j]

            @pl.when(i == 0)
            def _(o=o):
                o[...] = jnp.zeros_like(o)

            o[...] = o[...] + grads[nr + nb + j]

    in_specs = _rw_in_specs(rows, bps, gps, tm, tps)
    in_specs += [_col_spec(_view(d), tm, _view(d)[2], lambda i: (i, 0)) for d in douts]
    in_specs += [pl.BlockSpec((tm, add_rows[k].shape[1]), lambda i: (i, 0)) for k in adds]
    out_specs = [pl.BlockSpec((tm, _view(rows[k])[2]), lambda i: (i, 0)) for k in want]
    out_shape = [jax.ShapeDtypeStruct((t, _view(rows[k])[2]), row_grads[k]) for k in want]
    out_specs += [pl.BlockSpec((1, 1, b.shape[2]), lambda i: (i // tps, 0, 0)) for b in bps]
    out_shape += [jax.ShapeDtypeStruct(b.shape, F32) for b in bps]
    out_specs += [pl.BlockSpec(g.shape, lambda i, n=g.ndim: (0,) * n) for g in gps]
    out_shape += [jax.ShapeDtypeStruct(g.shape, F32) for g in gps]
    res = pl.pallas_call(
        body, name=name, grid=(t // tm,), in_specs=in_specs, out_specs=out_specs, out_shape=out_shape,
        compiler_params=_cp("arbitrary"),
    )(*[_view(r)[0] for r in rows], *bps, *gps, *[_view(d)[0] for d in douts], *[add_rows[k] for k in adds])
    nw = len(want)
    return res[:nw], res[nw:nw + nb], res[nw + nb:]


def mm_nn(name, xs, ws, tm, tn, out_dtype=F32, ride=None):
    views = [_view(x) for x in xs]
    t, n, k = views[0][0].shape[0], ws[0].shape[1], len(xs)

    def body(*refs):
        acc = _dg(refs[0][...], refs[k][...], 1, 0)
        for i in range(1, k):
            acc = acc + _dg(refs[i][...], refs[k + i][...], 1, 0)
        refs[2 * k][...] = acc.astype(out_dtype)

    in_specs = [_col_spec(v, tm, v[2], lambda i, j: (i, 0)) for v in views]
    in_specs += [pl.BlockSpec((w.shape[0], tn), lambda i, j: (0, j)) for w in ws]
    out_spec = pl.BlockSpec((tm, tn), lambda i, j: (i, j))
    out_shape = jax.ShapeDtypeStruct((t, n), out_dtype)
    if ride is not None:
        (res,), landed = hosted_call(body, name=name, grid=(t // tm, n // tn), in_specs=in_specs, out_specs=[out_spec],
                                     out_shape=[out_shape], args=(*[v[0] for v in views], *ws), ride=ride)
        return res, landed
    return pl.pallas_call(
        body, name=name, grid=(t // tm, n // tn), in_specs=in_specs, out_specs=out_spec, out_shape=out_shape,
        compiler_params=_cp("parallel", "parallel"),
    )(*[v[0] for v in views], *ws)


def mm_nt(name, dys, ws, tm, tk, out_dtype=F32, ride=None):
    views = [_view(d) for d in dys]
    t, kk, k = views[0][0].shape[0], ws[0].shape[0], len(dys)

    def body(*refs):
        acc = _dg(refs[0][...], refs[k][...], 1, 1)
        for i in range(1, k):
            acc = acc + _dg(refs[i][...], refs[k + i][...], 1, 1)
        refs[2 * k][...] = acc.astype(out_dtype)

    in_specs = [_col_spec(v, tm, v[2], lambda i, j: (i, 0)) for v in views]
    in_specs += [pl.BlockSpec((tk, w.shape[1]), lambda i, j: (j, 0)) for w in ws]
    out_spec = pl.BlockSpec((tm, tk), lambda i, j: (i, j))
    out_shape = jax.ShapeDtypeStruct((t, kk), out_dtype)
    if ride is not None:
        (res,), landed = hosted_call(body, name=name, grid=(t // tm, kk // tk), in_specs=in_specs, out_specs=[out_spec],
                                     out_shape=[out_shape], args=(*[v[0] for v in views], *ws), ride=ride)
        return res, landed
    return pl.pallas_call(
        body, name=name, grid=(t // tm, kk // tk), in_specs=in_specs, out_specs=out_spec, out_shape=out_shape,
        compiler_params=_cp("parallel", "parallel"),
    )(*[v[0] for v in views], *ws)


def modulate_proj(name, x, sc, sh, w, seq, tm, tk):
    t, kdim = x.shape
    n = w.shape[0]
    tps = seq // tm

    def body(x_ref, sc_ref, sh_ref, w_ref, o_ref, h_ref):
        h = (x_ref[...] * (1.0 + sc_ref[0]) + sh_ref[0]).astype(BF16)
        h_ref[...] = h
        o_ref[...] = _dg(h, w_ref[...], 1, 1)

    per_seq = pl.BlockSpec((1, 1, kdim), lambda i, j: (i // tps, 0, 0))
    return pl.pallas_call(
        body, name=name, grid=(t // tm, n // tk),
        in_specs=[pl.BlockSpec((tm, kdim), lambda i, j: (i, 0)), per_seq, per_seq,
                  pl.BlockSpec((tk, kdim), lambda i, j: (j, 0))],
        out_specs=[pl.BlockSpec((tm, tk), lambda i, j: (i, j)), pl.BlockSpec((tm, kdim), lambda i, j: (i, 0))],
        out_shape=[jax.ShapeDtypeStruct((t, n), F32), jax.ShapeDtypeStruct((t, kdim), BF16)],
        compiler_params=_cp("arbitrary", "arbitrary"),
    )(x, sc, sh, w)


def mm_tn(name, x, dy, tk, tn, tt, out_dtype=F32, ride=None):
    xv, dv = _view(x), _view(dy)
    t, kk, n = xv[0].shape[0], xv[2], dv[2]
    steps = t // tt

    def body(x_ref, d_ref, o_ref, acc_ref):
        @pl.when(pl.program_id(2) == 0)
        def _():
            acc_ref[...] = jnp.zeros_like(acc_ref)

        acc_ref[...] += _dg(x_ref[...], d_ref[...], 0, 0)

        @pl.when(pl.program_id(2) == steps - 1)
        def _():
            o_ref[...] = acc_ref[...].astype(out_dtype)

    in_specs = [_col_spec(xv, tt, tk, lambda a, b, c: (c, a)), _col_spec(dv, tt, tn, lambda a, b, c: (c, b))]
    out_spec = pl.BlockSpec((tk, tn), lambda a, b, c: (a, b))
    out_shape = jax.ShapeDtypeStruct((kk, n), out_dtype)
    if ride is not None:
        (res,), landed = hosted_call(body, name=name, grid=(kk // tk, n // tn, steps), in_specs=in_specs,
                                     out_specs=[out_spec], out_shape=[out_shape], scratch=[pltpu.VMEM((tk, tn), F32)],
                                     args=(xv[0], dv[0]), ride=ride)
        return res, landed
    return pl.pallas_call(
        body, name=name, grid=(kk // tk, n // tn, steps), in_specs=in_specs, out_specs=out_spec, out_shape=out_shape,
        scratch_shapes=[pltpu.VMEM((tk, tn), F32)],
        compiler_params=_cp("parallel", "parallel", "arbitrary"),
    )(xv[0], dv[0])


def _silu(x):
    return x * jax.nn.sigmoid(x)


def f_modulate(x, sc, sh):
    return (x * (1.0 + sc) + sh,)


def _res_ln(coef):
    def f(x, y, g, lg, lb):
        r = ALPHA * x + (coef * g) * y
        mu = jnp.mean(r, axis=-1, keepdims=True)
        d = r - mu
        var = jnp.mean(d * d, axis=-1, keepdims=True)
        return (d * lax.rsqrt(var + LN_EPS) * lg + lb,)
    return f


def f_glu(y, w, b):
    g = jax.nn.gelu(y)
    return (g * jax.nn.sigmoid(bdot_nn(g, w) + b),)


def _shift_down(x, halo, k):
    if k == 0:
        return x
    r = pltpu.roll(x, k, 0)
    hr = pltpu.roll(halo, k, 0)
    row = lax.broadcasted_iota(jnp.int32, (HALO, 1), 0)
    top = jnp.where(row < k, hr, r[:HALO])
    return jnp.concatenate([top, r[HALO:]], axis=0)


def _shift_up(x, halo, k):
    if k == 0:
        return x
    n = x.shape[0]
    r = pltpu.roll(x, n - k, 0)
    hr = pltpu.roll(halo, HALO - k, 0)
    row = lax.broadcasted_iota(jnp.int32, (HALO, 1), 0)
    bot = jnp.where(row >= HALO - k, hr, r[n - HALO:])
    return jnp.concatenate([r[:n - HALO], bot], axis=0)


def _conv_pre(x, halo, w, b):
    acc = x * w[CONV_K - 1:CONV_K, :] + b
    for k in range(1, CONV_K):
        acc = acc + _shift_down(x, halo, k) * w[CONV_K - 1 - k:CONV_K - k, :]
    return acc


def _rows_before(width, tm):
    return pl.BlockSpec((HALO, width), lambda i: (jnp.maximum(i * (tm // HALO) - 1, 0), 0))


def conv_fwd(proj, w, b, seq, tm):
    t = proj.shape[0]
    tps = seq // tm

    def body(x_ref, h_ref, w_ref, b_ref, o_ref):
        first = (pl.program_id(0) % tps == 0)
        halo = jnp.where(first, 0.0, h_ref[...])
        o_ref[...] = _silu(_conv_pre(x_ref[...], halo, w_ref[...], b_ref[...]))

    return pl.pallas_call(
        body, name="conv_fwd", grid=(t // tm,),
        in_specs=[pl.BlockSpec((tm, CONV_CH), lambda i: (i, 0)), _rows_before(CONV_CH, tm),
                  pl.BlockSpec((CONV_K, CONV_CH), lambda i: (0, 0)), pl.BlockSpec((1, CONV_CH), lambda i: (0, 0))],
        out_specs=pl.BlockSpec((tm, CONV_CH), lambda i: (i, 0)),
        out_shape=jax.ShapeDtypeStruct((t, CONV_CH), F32),
        compiler_params=_cp("arbitrary"),
    )(proj, proj, w, b)


def conv_bwd_pre(proj, w, b, dxs, dbm, dcm, seq, tm):
    t = proj.shape[0]
    tps = seq // tm

    def body(x_ref, h_ref, w_ref, b_ref, d1, d2, d3, dp_ref, dw_ref, db_ref):
        i = pl.program_id(0)
        halo = jnp.where(i % tps == 0, 0.0, h_ref[...])
        x = x_ref[...]
        pre = _conv_pre(x, halo, w_ref[...], b_ref[...])
        sg = jax.nn.sigmoid(pre)
        dout = jnp.concatenate([d1[...], d2[...], d3[...]], axis=1)
        dp = dout * (sg * (1.0 + pre * (1.0 - sg)))
        dp_ref[...] = dp

        @pl.when(i == 0)
        def _():
            dw_ref[...] = jnp.zeros_like(dw_ref)
            db_ref[...] = jnp.zeros_like(db_ref)

        db_ref[...] += jnp.sum(dp, axis=0, keepdims=True)
        for k in range(CONV_K):
            j = CONV_K - 1 - k
            dw_ref[j:j + 1, :] += jnp.sum(dp * _shift_down(x, halo, k), axis=0, keepdims=True)

    return pl.pallas_call(
        body, name="conv_bwd_pre", grid=(t // tm,),
        in_specs=[pl.BlockSpec((tm, CONV_CH), lambda i: (i, 0)), _rows_before(CONV_CH, tm),
                  pl.BlockSpec((CONV_K, CONV_CH), lambda i: (0, 0)), pl.BlockSpec((1, CONV_CH), lambda i: (0, 0)),
                  pl.BlockSpec((tm, 512), lambda i: (i, 0)), pl.BlockSpec((tm, 256), lambda i: (i, 0)),
                  pl.BlockSpec((tm, 256), lambda i: (i, 0))],
        out_specs=[pl.BlockSpec((tm, CONV_CH), lambda i: (i, 0)), pl.BlockSpec((CONV_K, CONV_CH), lambda i: (0, 0)),
                   pl.BlockSpec((1, CONV_CH), lambda i: (0, 0))],
        out_shape=[jax.ShapeDtypeStruct((t, CONV_CH), F32), jax.ShapeDtypeStruct((CONV_K, CONV_CH), F32),
                   jax.ShapeDtypeStruct((1, CONV_CH), F32)],
        compiler_params=_cp("arbitrary"),
    )(proj, proj, w, b, dxs, dbm, dcm)


def conv_bwd_x(dpre, w, seq, tm):
    t = dpre.shape[0]
    tps = seq // tm
    blocks = tm // HALO
    last = t // HALO - 1

    def body(d_ref, h_ref, w_ref, o_ref):
        halo = jnp.where(pl.program_id(0) % tps == tps - 1, 0.0, h_ref[...])
        d = d_ref[...]
        w = w_ref[...]
        acc = d * w[CONV_K - 1:CONV_K, :]
        for k in range(1, CONV_K):
            acc = acc + _shift_up(d, halo, k) * w[CONV_K - 1 - k:CONV_K - k, :]
        o_ref[...] = acc

    return pl.pallas_call(
        body, name="conv_bwd_x", grid=(t // tm,),
        in_specs=[pl.BlockSpec((tm, CONV_CH), lambda i: (i, 0)),
                  pl.BlockSpec((HALO, CONV_CH), lambda i: (jnp.minimum((i + 1) * blocks, last), 0)),
                  pl.BlockSpec((CONV_K, CONV_CH), lambda i: (0, 0))],
        out_specs=pl.BlockSpec((tm, CONV_CH), lambda i: (i, 0)),
        out_shape=jax.ShapeDtypeStruct((t, CONV_CH), F32),
        compiler_params=_cp("arbitrary"),
    )(dpre, dpre, w)


def _softplus(x):
    return jnp.maximum(x, 0.0) + jnp.log1p(jnp.exp(-jnp.abs(x)))


def _ssd_chunk(xs, bg, cg, dtr, zz, hp, dtb, alog, dcol, nw):
    l = xs.shape[0]
    row = lax.broadcasted_iota(jnp.int32, (l, l), 0)
    col = lax.broadcasted_iota(jnp.int32, (l, l), 1)
    causal = row >= col
    tril = causal.astype(F32)
    expand = (lax.broadcasted_iota(jnp.int32, (LANE, GROUP_COLS), 1) // SSD_HEAD_DIM
              == lax.broadcasted_iota(jnp.int32, (LANE, GROUP_COLS), 0)).astype(F32)
    head_of_col = lax.broadcasted_iota(jnp.int32, (1, GROUP_COLS), 1) // SSD_HEAD_DIM
    last_row = (lax.broadcasted_iota(jnp.int32, (l, 1), 0) == l - 1).astype(F32)

    dtc = _softplus(dtr + dtb)
    a_c = dtc * (-jnp.exp(alog))
    acs_c = mask_dot_left(tril, a_c)
    dt_e = mask_dot_right(dtc, expand)
    acs_e = mask_dot_right(acs_c, expand)
    alast_e = jnp.sum(acs_e * last_row, axis=0, keepdims=True)
    x = xs * dt_e
    states = bdot_tn(bg, x * jnp.exp(alast_e - acs_e))
    h_next = jnp.exp(alast_e) * hp + states
    d_e = jnp.sum(dcol * expand, axis=0, keepdims=True)
    y = bdot_nn(cg, hp) * jnp.exp(acs_e) + d_e * xs
    cb = bdot_nt(cg, bg)
    acs_t = acs_c.T
    for z in range(HEADS_PER_GROUP):
        seg = _take_col(z)(acs_c) - _take_row(z)(acs_t)
        lmat = jnp.exp(jnp.where(causal, seg, -1e30))
        y = y + bdot_nn(cb * lmat, x * (head_of_col == z).astype(F32))
    yz = y * _silu(zz)
    ms = jnp.mean(yz * yz, axis=-1, keepdims=True)
    return yz * lax.rsqrt(ms + LN_EPS) * nw, h_next


SSD_SUB = 4
SSD_ROWS = SSD_SUB * SSD_CHUNK


def _ssd_in_specs(steps, rev):
    def tok(b, c):
        return b * steps + (steps - 1 - c if rev else c)

    whole = lambda *shape: pl.BlockSpec(shape, lambda b, c: (0,) * len(shape))
    both = SSD_GROUPS * SSD_STATE
    return [
        pl.BlockSpec((SSD_ROWS, SSD_WIDTH), lambda b, c: (tok(b, c), 0)),
        pl.BlockSpec((SSD_ROWS, both), lambda b, c: (tok(b, c), SSD_WIDTH // both)),
        pl.BlockSpec((SSD_ROWS, both), lambda b, c: (tok(b, c), SSD_WIDTH // both + 1)),
        pl.BlockSpec((SSD_ROWS, SSD_GROUPS * LANE), lambda b, c: (tok(b, c), P_DT // (SSD_GROUPS * LANE))),
        pl.BlockSpec((SSD_ROWS, SSD_WIDTH), lambda b, c: (tok(b, c), P_Z // SSD_WIDTH)),
        whole(SSD_GROUPS, 1, LANE), whole(SSD_GROUPS, 1, LANE), whole(SSD_GROUPS, LANE, 1),
        whole(SSD_GROUPS, 1, GROUP_COLS),
    ], tok


def _piece(ref, s, g, width):
    return ref[s * SSD_CHUNK:(s + 1) * SSD_CHUNK, g * width:(g + 1) * width]


def ssd_fwd(xc, proj, dtb, alog, dcol, nw, bsz, seq, ride=None):
    t = xc.shape[0]
    nc = seq // SSD_CHUNK
    steps = nc // SSD_SUB
    in_specs, tok = _ssd_in_specs(steps, False)

    def body(xs, bm, cm, dtr, zz, dtb_r, alog_r, dcol_r, nw_r, y_ref, hp_ref, h_scr):
        @pl.when(pl.program_id(1) == 0)
        def _():
            h_scr[...] = jnp.zeros_like(h_scr)

        for g in range(SSD_GROUPS):
            h = h_scr[g]
            for s in range(SSD_SUB):
                hp_ref[g, 0, s] = h
                y, h = _ssd_chunk(_piece(xs, s, g, GROUP_COLS), _piece(bm, s, g, SSD_STATE),
                                  _piece(cm, s, g, SSD_STATE), _piece(dtr, s, g, LANE), _piece(zz, s, g, GROUP_COLS), h,
                                  dtb_r[g], alog_r[g], dcol_r[g], nw_r[g])
                y_ref[s * SSD_CHUNK:(s + 1) * SSD_CHUNK, g * GROUP_COLS:(g + 1) * GROUP_COLS] = y
            h_scr[g] = h

    return hosted_call(
        body, name="ssd_fwd", grid=(bsz, steps), in_specs=in_specs,
        out_specs=[pl.BlockSpec((SSD_ROWS, SSD_WIDTH), lambda b, c: (tok(b, c), 0)),
                   pl.BlockSpec((SSD_GROUPS, 1, SSD_SUB, SSD_STATE, GROUP_COLS), lambda b, c: (0, b, c, 0, 0))],
        out_shape=[jax.ShapeDtypeStruct((t, SSD_WIDTH), F32),
                   jax.ShapeDtypeStruct((SSD_GROUPS, bsz, nc, SSD_STATE, GROUP_COLS), F32)],
        scratch=[pltpu.VMEM((SSD_GROUPS, SSD_STATE, GROUP_COLS), F32)],
        args=(xc, xc, xc, proj, proj, dtb, alog, dcol, nw), ride=ride)


def ssd_bwd(xc, proj, dtb, alog, dcol, nw, hprev, dy, bsz, seq):
    t = xc.shape[0]
    nc = seq // SSD_CHUNK
    steps = nc // SSD_SUB
    in_specs, tok = _ssd_in_specs(steps, True)
    in_specs += [pl.BlockSpec((SSD_GROUPS, 1, SSD_SUB, SSD_STATE, GROUP_COLS), lambda b, c: (0, b, steps - 1 - c, 0, 0)),
                 pl.BlockSpec((SSD_ROWS, SSD_WIDTH), lambda b, c: (tok(b, c), 0))]

    def body(xs, bm, cm, dtr, zz, dtb_r, alog_r, dcol_r, nw_r, hp_ref, dy_ref,
             dxs, dbm, dcm, ddt, dzz, ddtb, dalog, ddcol, dnw, dh_scr):
        b, c = pl.program_id(0), pl.program_id(1)

        @pl.when(c == 0)
        def _():
            dh_scr[...] = jnp.zeros_like(dh_scr)

        @pl.when((b == 0) & (c == 0))
        def _():
            for r in (ddtb, dalog, ddcol, dnw):
                r[...] = jnp.zeros_like(r)

        for g in range(SSD_GROUPS):
            wide = slice(g * GROUP_COLS, (g + 1) * GROUP_COLS)
            state = slice(g * SSD_STATE, (g + 1) * SSD_STATE)
            dh = dh_scr[g]
            for s in reversed(range(SSD_SUB)):
                rows = slice(s * SSD_CHUNK, (s + 1) * SSD_CHUNK)
                _, pull = jax.vjp(_ssd_chunk, xs[rows, wide], bm[rows, state], cm[rows, state],
                                  _piece(dtr, s, g, LANE), zz[rows, wide], hp_ref[g, 0, s],
                                  dtb_r[g], alog_r[g], dcol_r[g], nw_r[g])
                d = pull((dy_ref[rows, wide], dh))
                dxs[rows, wide], dbm[rows, state], dcm[rows, state], dzz[rows, wide] = d[0], d[1], d[2], d[4]
                ddt[rows, g * LANE:(g + 1) * LANE] = d[3]
                dh = d[5]
                ddtb[g] += d[6]
                dalog[g] += d[7]
                ddcol[g] += d[8]
                dnw[g] += d[9]
            dh_scr[g] = dh

    def tile(w):
        return pl.BlockSpec((SSD_ROWS, w), lambda b, c: (tok(b, c), 0))

    whole = lambda *shape: pl.BlockSpec(shape, lambda b, c: (0,) * len(shape))
    return pl.pallas_call(
        body, name="ssd_bwd", grid=(bsz, steps), in_specs=in_specs,
        out_specs=[tile(SSD_WIDTH), tile(2 * SSD_STATE), tile(2 * SSD_STATE), tile(2 * LANE), tile(SSD_WIDTH),
                   whole(SSD_GROUPS, 1, LANE), whole(SSD_GROUPS, 1, LANE), whole(SSD_GROUPS, LANE, 1),
                   whole(SSD_GROUPS, 1, GROUP_COLS)],
        out_shape=[jax.ShapeDtypeStruct((t, SSD_WIDTH), F32), jax.ShapeDtypeStruct((t, 2 * SSD_STATE), F32),
                   jax.ShapeDtypeStruct((t, 2 * SSD_STATE), F32), jax.ShapeDtypeStruct((t, 2 * LANE), F32),
                   jax.ShapeDtypeStruct((t, SSD_WIDTH), F32),
                   jax.ShapeDtypeStruct((SSD_GROUPS, 1, LANE), F32), jax.ShapeDtypeStruct((SSD_GROUPS, 1, LANE), F32),
                   jax.ShapeDtypeStruct((SSD_GROUPS, LANE, 1), F32),
                   jax.ShapeDtypeStruct((SSD_GROUPS, 1, GROUP_COLS), F32)],
        scratch_shapes=[pltpu.VMEM((SSD_GROUPS, SSD_STATE, GROUP_COLS), F32)],
        compiler_params=_cp("arbitrary", "arbitrary"),
    )(xc, xc, xc, proj, proj, dtb, alog, dcol, nw, hprev, dy)


def _disc_a(a_re, a_im, log_dt):
    dt = jnp.exp(log_dt)
    mag = jnp.exp(dt * a_re)
    ab_re, ab_im = mag * jnp.cos(dt * a_im), mag * jnp.sin(dt * a_im)
    den = a_re * a_re + a_im * a_im
    nr, ni = ab_re - 1.0, ab_im
    f_re, f_im = (nr * a_re + ni * a_im) / den, (ni * a_re - nr * a_im) / den
    return ab_re, ab_im, f_re, f_im


def _disc_b(f_re, f_im, b_re, b_im):
    return f_re * b_re - f_im * b_im, f_re * b_im + f_im * b_re


def _whole(f, name, args, outs):
    def body(*refs):
        res = f(*[r[...] for r in refs[:len(args)]])
        for o, v in zip(refs[len(args):], res):
            o[...] = v

    return pl.pallas_call(body, name=name, out_shape=[jax.ShapeDtypeStruct(s, F32) for s in outs])(*args)


def _whole_vjp(f, name, args, cts):
    def body(*refs):
        vals = [r[...] for r in refs[:len(args)]]
        _, pull = jax.vjp(f, *vals)
        res = pull(tuple(r[...] for r in refs[len(args):len(args) + len(cts)]))
        for o, v in zip(refs[len(args) + len(cts):], res):
            o[...] = v

    return pl.pallas_call(body, name=name, out_shape=[jax.ShapeDtypeStruct(a.shape, F32) for a in args])(*args, *cts)


S5_SUB = 8
S5_STEPS = 3


def s5_tables(lam_re, lam_im):
    rows = S5_STEPS * S5_SUB

    def body(lr_ref, li_ref, sf_re, sf_im, sb_re, sb_im, cf_re, cf_im, cb_re, cb_im):
        lr, li = lr_ref[...], li_ref[...]

        def power(k):
            m = jnp.exp(k * lr)
            return m * jnp.cos(k * li), m * jnp.sin(k * li)

        srow = lax.broadcasted_iota(jnp.int32, (rows, 1), 0)
        k = jnp.left_shift(1, srow // S5_SUB)
        tt = srow % S5_SUB
        pr, pi = power(k.astype(F32))
        fwd, bwd = tt >= k, tt < S5_SUB - k
        sf_re[...], sf_im[...] = jnp.where(fwd, pr, 0.0), jnp.where(fwd, pi, 0.0)
        sb_re[...], sb_im[...] = jnp.where(bwd, pr, 0.0), jnp.where(bwd, pi, 0.0)
        trow = lax.broadcasted_iota(jnp.int32, (S5_SUB, 1), 0)
        cf_re[...], cf_im[...] = power((trow + 1).astype(F32))
        cb_re[...], cb_im[...] = power((S5_SUB - trow).astype(F32))

    shp = [jax.ShapeDtypeStruct((rows, S5_COLS), F32)] * 4 + [jax.ShapeDtypeStruct((S5_SUB, S5_COLS), F32)] * 4
    return pl.pallas_call(body, name="s5_tables", out_shape=shp)(lam_re, lam_im)


def _s5_coefs(steps_re, steps_im, carry_re, carry_im, reverse):
    sign = -1.0 if reverse else 1.0
    steps = [(steps_re[s * S5_SUB:(s + 1) * S5_SUB, :], sign * steps_im[s * S5_SUB:(s + 1) * S5_SUB, :])
             for s in range(S5_STEPS)]
    return steps, (carry_re[...], sign * carry_im[...])


def _s5_block_scan(ar, ai, coefs, cr, ci, reverse):
    steps, (qr, qi) = coefs
    for s, (pr, pi) in enumerate(steps):
        shift = S5_SUB - (1 << s) if reverse else (1 << s)
        sr, si = pltpu.roll(ar, shift, 0), pltpu.roll(ai, shift, 0)
        ar, ai = ar + pr * sr - pi * si, ai + pr * si + pi * sr
    br, bi = jnp.broadcast_to(cr, ar.shape), jnp.broadcast_to(ci, ai.shape)
    return ar + qr * br - qi * bi, ai + qr * bi + qi * br


def _s5_specs(n5, rev):
    def tok(q, b, c):
        return b * n5 + (n5 - 1 - c if rev else c)

    qcols = S5_COLS // S5_Q
    specs = [
        pl.BlockSpec((S5_CHUNK, LANE), lambda q, b, c: (tok(q, b, c), P_U // LANE + q)),
        pl.BlockSpec((1, LANE, qcols), lambda q, b, c: (q, 0, 0)),
        pl.BlockSpec((1, LANE, qcols), lambda q, b, c: (q, 0, 0)),
        pl.BlockSpec((1, qcols, LANE), lambda q, b, c: (q, 0, 0)),
        pl.BlockSpec((1, qcols, LANE), lambda q, b, c: (q, 0, 0)),
        pl.BlockSpec((S5_STEPS * S5_SUB, qcols), lambda q, b, c: (0, q)),
        pl.BlockSpec((S5_STEPS * S5_SUB, qcols), lambda q, b, c: (0, q)),
        pl.BlockSpec((S5_SUB, qcols), lambda q, b, c: (0, q)),
        pl.BlockSpec((S5_SUB, qcols), lambda q, b, c: (0, q)),
        pl.BlockSpec((1, 1, LANE), lambda q, b, c: (q, 0, 0)),
    ]
    return specs, tok, qcols


def s5_fwd(proj, wb_re, wb_im, wc_re, wc_im, sf_re, sf_im, cf_re, cf_im, dvec, bsz, seq, ride=None):
    t = proj.shape[0]
    n5 = seq // S5_CHUNK
    in_specs, tok, qcols = _s5_specs(n5, False)

    def body(u_ref, wbr, wbi, wcr, wci, sfr, sfi, cfr, cfi, d_ref, y_ref, xr_ref, xi_ref, cr_scr, ci_scr):
        @pl.when(pl.program_id(2) == 0)
        def _():
            cr_scr[...] = jnp.zeros_like(cr_scr)
            ci_scr[...] = jnp.zeros_like(ci_scr)

        u = u_ref[...]
        bur, bui = _dg(u, wbr[0], 1, 0), _dg(u, wbi[0], 1, 0)
        coefs = _s5_coefs(sfr, sfi, cfr, cfi, False)
        cr, ci = cr_scr[...], ci_scr[...]
        for r in range(S5_CHUNK // S5_SUB):
            rows = slice(r * S5_SUB, (r + 1) * S5_SUB)
            xr, xi = _s5_block_scan(bur[rows], bui[rows], coefs, cr, ci, False)
            xr_ref[rows, :], xi_ref[rows, :] = xr, xi
            cr, ci = xr[S5_SUB - 1:, :], xi[S5_SUB - 1:, :]
        cr_scr[...], ci_scr[...] = cr, ci
        y_ref[...] = _dg(xr_ref[...], wcr[0], 1, 0) - _dg(xi_ref[...], wci[0], 1, 0) + u * d_ref[0]

    def tile(w):
        return pl.BlockSpec((S5_CHUNK, w), lambda q, b, c: (tok(q, b, c), q))

    return hosted_call(
        body, name="s5_fwd", grid=(S5_Q, bsz, n5), in_specs=in_specs,
        out_specs=[tile(LANE), tile(qcols), tile(qcols)],
        out_shape=[jax.ShapeDtypeStruct((t, S5_WIDTH), F32), jax.ShapeDtypeStruct((t, S5_COLS), F32),
                   jax.ShapeDtypeStruct((t, S5_COLS), F32)],
        scratch=[pltpu.VMEM((1, qcols), F32)] * 2,
        args=(proj, wb_re, wb_im, wc_re, wc_im, sf_re, sf_im, cf_re, cf_im, dvec), ride=ride)


def s5_bwd(proj, wb_re, wb_im, wc_re, wc_im, sb_re, sb_im, cb_re, cb_im, dvec, xr_all, xi_all, dy, bsz, seq,
           ride=None):
    t = proj.shape[0]
    n5 = seq // S5_CHUNK
    in_specs, tok, qcols = _s5_specs(n5, True)
    blocks = S5_CHUNK // HALO

    def prev_rows(q, b, c):
        return (jnp.maximum(tok(q, b, c) * blocks - 1, 0), q)

    in_specs += [pl.BlockSpec((S5_CHUNK, qcols), lambda q, b, c: (tok(q, b, c), q)),
                 pl.BlockSpec((S5_CHUNK, qcols), lambda q, b, c: (tok(q, b, c), q)),
                 pl.BlockSpec((HALO, qcols), prev_rows), pl.BlockSpec((HALO, qcols), prev_rows),
                 pl.BlockSpec((S5_CHUNK, LANE), lambda q, b, c: (tok(q, b, c), q))]

    def body(u_ref, wbr, wbi, wcr, wci, sbr, sbi, cbr, cbi, d_ref, xr_ref, xi_ref, pr_ref, pi_ref, dy_ref,
             du_ref, dwbr, dwbi, dwcr, dwci, dar, dai, dd_ref, gr_scr, gi_scr, gr_all, gi_all):
        b, c = pl.program_id(1), pl.program_id(2)

        @pl.when(c == 0)
        def _():
            gr_scr[...] = jnp.zeros_like(gr_scr)
            gi_scr[...] = jnp.zeros_like(gi_scr)

        @pl.when((b == 0) & (c == 0))
        def _():
            for r in (dwbr, dwbi, dwcr, dwci, dar, dai, dd_ref):
                r[...] = jnp.zeros_like(r)

        u, dy_v = u_ref[...], dy_ref[...]
        g0r, g0i = _dg(dy_v, wcr[0], 1, 1), -_dg(dy_v, wci[0], 1, 1)
        coefs = _s5_coefs(sbr, sbi, cbr, cbi, True)
        cr, ci = gr_scr[...], gi_scr[...]
        for r in reversed(range(S5_CHUNK // S5_SUB)):
            rows = slice(r * S5_SUB, (r + 1) * S5_SUB)
            br, bi = _s5_block_scan(g0r[rows], g0i[rows], coefs, cr, ci, True)
            gr_all[rows, :], gi_all[rows, :] = br, bi
            cr, ci = br[:1, :], bi[:1, :]
        gr_scr[...], gi_scr[...] = cr, ci
        gr, gi = gr_all[...], gi_all[...]

        row = lax.broadcasted_iota(jnp.int32, (S5_CHUNK, 1), 0)
        xr, xi = xr_ref[...], xi_ref[...]
        is_first = (c == n5 - 1)
        hr = jnp.where(is_first, 0.0, pr_ref[...][HALO - 1:, :])
        hi = jnp.where(is_first, 0.0, pi_ref[...][HALO - 1:, :])
        xpr = jnp.where(row >= 1, pltpu.roll(xr, 1, 0), hr)
        xpi = jnp.where(row >= 1, pltpu.roll(xi, 1, 0), hi)
        dar[0] += jnp.sum(xpr * gr + xpi * gi, axis=0, keepdims=True)
        dai[0] += jnp.sum(xpr * gi - xpi * gr, axis=0, keepdims=True)
        du_ref[...] = _dg(gr, wbr[0], 1, 1) + _dg(gi, wbi[0], 1, 1) + dy_v * d_ref[0]
        dwbr[0] += _dg(u, gr, 0, 0)
        dwbi[0] += _dg(u, gi, 0, 0)
        dwcr[0] += _dg(xr, dy_v, 0, 0)
        dwci[0] -= _dg(xi, dy_v, 0, 0)
        dd_ref[0] += jnp.sum(dy_v * u, axis=0, keepdims=True)

    def acc(shape):
        return pl.BlockSpec((1,) + shape, lambda q, b, c: (q, 0, 0))

    return hosted_call(
        body, name="s5_bwd", grid=(S5_Q, bsz, n5), in_specs=in_specs,
        out_specs=[pl.BlockSpec((S5_CHUNK, LANE), lambda q, b, c: (tok(q, b, c), q)),
                   acc((LANE, qcols)), acc((LANE, qcols)), acc((qcols, LANE)), acc((qcols, LANE)),
                   acc((1, qcols)), acc((1, qcols)), acc((1, LANE))],
        out_shape=[jax.ShapeDtypeStruct((t, S5_WIDTH), F32),
                   jax.ShapeDtypeStruct((S5_Q, LANE, qcols), F32), jax.ShapeDtypeStruct((S5_Q, LANE, qcols), F32),
                   jax.ShapeDtypeStruct((S5_Q, qcols, LANE), F32), jax.ShapeDtypeStruct((S5_Q, qcols, LANE), F32),
                   jax.ShapeDtypeStruct((S5_Q, 1, qcols), F32), jax.ShapeDtypeStruct((S5_Q, 1, qcols), F32),
                   jax.ShapeDtypeStruct((S5_Q, 1, LANE), F32)],
        scratch=[pltpu.VMEM((1, qcols), F32)] * 2 + [pltpu.VMEM((S5_CHUNK, qcols), F32)] * 2,
        args=(proj, wb_re, wb_im, wc_re, wc_im, sb_re, sb_im, cb_re, cb_im, dvec, xr_all, xi_all, xr_all, xi_all, dy),
        ride=ride)


def _blockdiag_b(bb):
    b4 = bb.reshape(S5_Q, 8, S5_STATE, S5_GROUP_CH)
    eye = jnp.eye(8, dtype=bb.dtype)
    w = jnp.einsum("qgph,gk->qghkp", b4, eye)
    return w.reshape(S5_Q, LANE, S5_COLS // S5_Q)


def _unblock_b(dw):
    d = dw.reshape(S5_Q, 8, S5_GROUP_CH, 8, S5_STATE)
    d = jnp.einsum("qghgp->qgph", d)
    return d.reshape(S5_COLS, S5_GROUP_CH)


def _blockdiag_c(cc):
    c4 = cc.reshape(S5_Q, 8, S5_GROUP_CH, S5_STATE)
    eye = jnp.eye(8, dtype=cc.dtype)
    w = jnp.einsum("qghp,gk->qgpkh", c4, eye)
    return w.reshape(S5_Q, S5_COLS // S5_Q, LANE)


def _unblock_c(dw):
    d = dw.reshape(S5_Q, 8, S5_STATE, 8, S5_GROUP_CH)
    d = jnp.einsum("qgpgh->qghp", d)
    return d.reshape(S5_GROUPS, S5_GROUP_CH, S5_STATE)


def ada_fwd(c_all, w_loc, b_loc):
    def body(c_ref, w_ref, b_ref, o_ref):
        o_ref[...] = _dg(_silu(c_ref[...]), w_ref[...], 1, 0) + b_ref[...]

    return pl.pallas_call(body, name="ada_fwd",
                          out_shape=jax.ShapeDtypeStruct((c_all.shape[0], w_loc.shape[1]), F32),
                          compiler_params=_cp())(c_all, w_loc, b_loc)


def ada_bwd(c_all, dmod_all, dmod_cols):
    def body(c_ref, da_ref, dc_ref, gb_ref, gw_ref):
        gb_ref[...] = jnp.sum(da_ref[...], axis=0, keepdims=True)
        gw_ref[...] = _dg(_silu(c_ref[...]), dc_ref[...], 0, 0)

    return pl.pallas_call(body, name="ada_bwd",
                          out_shape=[jax.ShapeDtypeStruct((1, dmod_all.shape[1]), F32),
                                     jax.ShapeDtypeStruct((c_all.shape[1], dmod_cols.shape[1]), F32)],
                          compiler_params=_cp())(c_all, dmod_all, dmod_cols)


_FLIPS = [(0, 0, 1), (1, 0, 0), (0, 1, 0), (1, 1, 0), (1, 0, 1), (0, 1, 1), (1, 1, 1)]


def _exchange_ops(srcs, outs, sems, gather):
    n = len(srcs)
    send_sems, recv_sems, loc_sems = sems
    x, y, c = lax.axis_index("x"), lax.axis_index("y"), lax.axis_index("c")
    me = 4 * x + 2 * y + c
    peers = []
    for fx, fy, fc in _FLIPS:
        px, py, pc = (1 - x if fx else x), (1 - y if fy else y), (1 - c if fc else c)
        peers.append(((px, py, pc), 4 * px + 2 * py + pc))

    def copy(k, j, slot_src, slot_dst):
        src = srcs[k] if gather[k] else srcs[k].at[slot_src]
        return pltpu.make_async_remote_copy(src_ref=src, dst_ref=outs[k].at[slot_dst],
                                            send_sem=send_sems.at[k, j], recv_sem=recv_sems.at[k, j],
                                            device_id=peers[j][0], device_id_type=MESH)

    def local(k):
        own = srcs[k] if gather[k] else srcs[k].at[me]
        return pltpu.make_async_copy(own, outs[k].at[me], loc_sems.at[k])

    def start():
        for k in range(n):
            for j in range(N_DEV - 1):
                copy(k, j, peers[j][1], me).start()
            local(k).start()

    def wait():
        for k in range(n):
            for j in range(N_DEV - 1):
                copy(k, j, me, peers[j][1]).wait_recv()
        for k in range(n):
            for j in range(N_DEV - 1):
                copy(k, j, peers[j][1], me).wait_send()
            local(k).wait()

    return start, wait


def _gather_two_level(srcs, outs, sems):
    n = len(srcs)
    send_sems, recv_sems, loc_sems = sems
    x, y, c = lax.axis_index("x"), lax.axis_index("y"), lax.axis_index("c")
    slot = lambda px, py, pc: 4 * px + 2 * py + pc
    me, sibling = (x, y, c), (x, y, 1 - c)
    chips = [(1 - x, y), (x, 1 - y), (1 - x, 1 - y)]

    def copy(k, j, block, to, own=False):
        return pltpu.make_async_remote_copy(src_ref=srcs[k] if own else outs[k].at[slot(*block)],
                                            dst_ref=outs[k].at[slot(*block)],
                                            send_sem=send_sems.at[k, j], recv_sem=recv_sems.at[k, j],
                                            device_id=to, device_id_type=MESH)

    locs = [pltpu.make_async_copy(srcs[k], outs[k].at[slot(*me)], loc_sems.at[k]) for k in range(n)]
    for k in range(n):
        locs[k].start()
        copy(k, 0, me, sibling, own=True).start()
        for j, chip in enumerate(chips):
            copy(k, 1 + j, me, (*chip, c), own=True).start()
    for j, chip in enumerate(chips):
        for k in range(n):
            copy(k, 1 + j, (*chip, c), me).wait_recv()
            copy(k, 4 + j, (*chip, c), sibling).start()
    for k in range(n):
        copy(k, 0, sibling, me).wait_recv()
        for j, chip in enumerate(chips):
            copy(k, 4 + j, (*chip, 1 - c), me).wait_recv()
    for k in range(n):
        copy(k, 0, me, sibling, own=True).wait_send()
        for j, chip in enumerate(chips):
            copy(k, 1 + j, me, (*chip, c), own=True).wait_send()
            copy(k, 4 + j, (*chip, c), sibling).wait_send()
        locs[k].wait()


def gather_two_level(name, arrs):
    n = len(arrs)
    specs, shapes, sems = _exchange_parts(arrs, [True] * n)

    def body(*refs):
        _gather_two_level(refs[:n], refs[n:2 * n], refs[2 * n:])

    return pl.pallas_call(
        body, name=name, in_specs=specs, out_specs=specs, out_shape=shapes, scratch_shapes=sems,
        compiler_params=pltpu.CompilerParams(has_side_effects=True),
    )(*arrs)


def _exchange_parts(arrs, gather):
    n = len(arrs)
    any_spec = pl.BlockSpec(memory_space=pl.ANY)
    shapes = [jax.ShapeDtypeStruct(((N_DEV,) + a.shape) if g else a.shape, a.dtype) for a, g in zip(arrs, gather)]
    sems = [pltpu.SemaphoreType.DMA((n, N_DEV - 1)), pltpu.SemaphoreType.DMA((n, N_DEV - 1)),
            pltpu.SemaphoreType.DMA((n,))]
    return [any_spec] * n, shapes, sems


def exchange(name, arrs, gather):
    n = len(arrs)
    specs, shapes, sems = _exchange_parts(arrs, gather)

    def body(*refs):
        start, wait = _exchange_ops(refs[:n], refs[n:2 * n], refs[2 * n:], gather)
        start()
        wait()

    return pl.pallas_call(
        body, name=name, in_specs=specs, out_specs=specs, out_shape=shapes, scratch_shapes=sems,
        compiler_params=pltpu.CompilerParams(has_side_effects=True),
    )(*arrs)


def hosted_call(body, *, name, grid, in_specs, out_specs, out_shape, args, scratch=(), ride=None):
    sem = ("arbitrary",) * len(grid)
    if ride is None:
        res = pl.pallas_call(body, name=name, grid=grid, in_specs=in_specs, out_specs=out_specs, out_shape=out_shape,
                             scratch_shapes=list(scratch), compiler_params=_cp(*sem))(*args)
        return list(res), []
    arrs, gather = ride
    n, n_in, n_out, n_scr = len(arrs), len(in_specs), len(out_specs), len(scratch)
    specs, shapes, sems = _exchange_parts(arrs, gather)

    def both(*refs):
        ins, srcs = refs[:n_in], refs[n_in:n_in + n]
        outs, landed = refs[n_in + n:n_in + n + n_out], refs[n_in + n + n_out:n_in + 2 * n + n_out]
        scr, ex_sems = refs[n_in + 2 * n + n_out:n_in + 2 * n + n_out + n_scr], refs[n_in + 2 * n + n_out + n_scr:]
        start, wait = _exchange_ops(srcs, landed, ex_sems, gather)
        first = functools.reduce(lambda a, b: a & b, [pl.program_id(d) == 0 for d in range(len(grid))])
        last = functools.reduce(lambda a, b: a & b, [pl.program_id(d) == grid[d] - 1 for d in range(len(grid))])
        pl.when(first)(start)
        body(*ins, *outs, *scr)
        pl.when(last)(wait)

    res = pl.pallas_call(
        both, name=name, grid=grid, in_specs=list(in_specs) + specs, out_specs=list(out_specs) + specs,
        out_shape=list(out_shape) + shapes, scratch_shapes=list(scratch) + sems, compiler_params=_cp(*sem),
    )(*args, *arrs)
    return list(res[:n_out]), list(res[n_out:])


def sum_slots(name, slots, tr):
    _, r, c = slots.shape

    def body(s_ref, o_ref):
        acc = s_ref[0].astype(F32)
        for j in range(1, N_DEV):
            acc = acc + s_ref[j].astype(F32)
        o_ref[...] = acc

    return pl.pallas_call(
        body, name=name, grid=(r // tr,), in_specs=[pl.BlockSpec((N_DEV, tr, c), lambda i: (0, i, 0))],
        out_specs=pl.BlockSpec((tr, c), lambda i: (i, 0)), out_shape=jax.ShapeDtypeStruct((r, c), F32),
        compiler_params=_cp("parallel"),
    )(slots)


def adamw(name, g, w, m, v, tr):
    slots = g.ndim == 3
    r, c = w.shape
    c1, c2 = 1.0 - ADAM_B1 ** ADAM_STEP, 1.0 - ADAM_B2 ** ADAM_STEP

    def body(g_ref, w_ref, m_ref, v_ref, go, do, mo, vo):
        if slots:
            gg = g_ref[0].astype(F32)
            for j in range(1, N_DEV):
                gg = gg + g_ref[j].astype(F32)
        else:
            gg = g_ref[...]
        mn = ADAM_B1 * m_ref[...] + (1.0 - ADAM_B1) * gg
        vn = ADAM_B2 * v_ref[...] + (1.0 - ADAM_B2) * (gg * gg)
        go[...], mo[...], vo[...] = gg, mn, vn
        do[...] = -ADAM_LR * ((mn / c1) / (jnp.sqrt(vn / c2) + ADAM_EPS) + ADAM_WD * w_ref[...])

    blk = pl.BlockSpec((tr, c), lambda i: (i, 0))
    gspec = pl.BlockSpec((N_DEV, tr, c), lambda i: (0, i, 0)) if slots else blk
    return pl.pallas_call(
        body, name=name, grid=(r // tr,), in_specs=[gspec, blk, blk, blk], out_specs=[blk] * 4,
        out_shape=[jax.ShapeDtypeStruct((r, c), F32)] * 4, compiler_params=_cp("parallel"),
    )(g, w, m, v)


def _lane_rows(n):
    return -(-n // (8 * LANE)) * 8


def _pack(arrs):
    pieces = []
    for a in arrs:
        n = math.prod(a.shape)
        flat = a.reshape(-1).astype(F32)
        pieces.append(jnp.pad(flat, (0, _lane_rows(n) * LANE - n)).reshape(_lane_rows(n), LANE))
    return jnp.concatenate(pieces, axis=0)


def _unpack(buf, shapes):
    out, off = [], 0
    for s in shapes:
        n = math.prod(s)
        out.append(buf[off:off + _lane_rows(n)].reshape(-1)[:n].reshape(s))
        off += _lane_rows(n)
    return out


FF_CHUNK = D_FF
DW_TOKENS = 2048
FFN_TM = 256


def _resident(shape):
    return pl.BlockSpec(shape, lambda i: (0,) * len(shape), pipeline_mode=pl.Buffered(1))


def _ffn_fwd(tag, x, sc, sh, g, w1, w3, w2, lg, lb, seq, tm, ride=None, target=None):
    t = x.shape[0]
    tm = min(FFN_TM, tm)
    tps = seq // tm
    ln = _res_ln(0.5)
    head = target is not None

    def body(x_ref, sc_ref, sh_ref, g_ref, lg_ref, lb_ref, w1_ref, w3_ref, w2_ref, *rest):
        if head:
            t_ref, y_ref, h_ref, a_ref, b_ref, f_ref, l_ref = rest
        else:
            y_ref, h_ref, a_ref, b_ref, f_ref = rest
        xv = x_ref[...]
        h = (xv * (1.0 + sc_ref[0]) + sh_ref[0]).astype(BF16)
        h_ref[...] = h
        acc = jnp.zeros((tm, D_MODEL), F32)
        for j in range(D_FF // FF_CHUNK):
            sl = slice(j * FF_CHUNK, (j + 1) * FF_CHUNK)
            a = _dg(h, w1_ref[sl, :], 1, 1)
            b = _dg(h, w3_ref[sl, :], 1, 1)
            a_ref[:, sl] = a
            b_ref[:, sl] = b
            acc = acc + _dg(_silu(a) * b, w2_ref[sl, :], 1, 0)
        f_ref[...] = acc
        y = ln(xv, acc, g_ref[0], lg_ref[...], lb_ref[...])[0]
        if head:
            @pl.when(pl.program_id(0) == 0)
            def _():
                l_ref[...] = jnp.zeros_like(l_ref)

            e = y - t_ref[...]
            y_ref[...] = e * (1.0 / D_MODEL)
            l_ref[...] += 0.5 * jnp.sum(jnp.mean(e * e, axis=-1, keepdims=True), axis=0, keepdims=True)
        else:
            y_ref[...] = y

    row = lambda c: pl.BlockSpec((tm, c), lambda i: (i, 0))
    per_seq = pl.BlockSpec((1, 1, D_MODEL), lambda i: (i // tps, 0, 0))
    vec = pl.BlockSpec((1, D_MODEL), lambda i: (0, 0))
    res, landed = hosted_call(
        body, name=tag + "_fwd", grid=(t // tm,),
        in_specs=[row(D_MODEL), per_seq, per_seq, per_seq, vec, vec,
                  _resident((D_FF, D_MODEL)), _resident((D_FF, D_MODEL)), _resident((D_FF, D_MODEL))]
        + ([row(D_MODEL)] if head else []),
        out_specs=[row(D_MODEL), row(D_MODEL), row(D_FF), row(D_FF), row(D_MODEL)]
        + ([pl.BlockSpec((1, 1), lambda i: (0, 0))] if head else []),
        out_shape=[jax.ShapeDtypeStruct((t, D_MODEL), F32), jax.ShapeDtypeStruct((t, D_MODEL), BF16),
                   jax.ShapeDtypeStruct((t, D_FF), F32), jax.ShapeDtypeStruct((t, D_FF), F32),
                   jax.ShapeDtypeStruct((t, D_MODEL), F32)] + ([jax.ShapeDtypeStruct((1, 1), F32)] if head else []),
        args=(x, sc, sh, g, lg, lb, w1, w3, w2) + ((target,) if head else ()), ride=ride)
    first = (res[0], res[5][0, 0]) if head else res[0]
    return first, tuple(res[1:5]), landed


def _ffn_bwd(tag, dy, x, sc, sh, g, w1, w3, w2, lg, lb, res, seq, tm, ride=None, chain=None):
    h, a, b, f = res
    t = x.shape[0]
    tmk = min(FFN_TM, tm)
    tps = seq // tmk
    ln = _res_ln(0.5)

    def body(dy_ref, x_ref, f_ref, a_ref, b_ref, sc_ref, sh_ref, g_ref, lg_ref, lb_ref, w1_ref, w3_ref, w2_ref,
             dx_ref, da_ref, db_ref, s_ref, df_ref, dsc_ref, dsh_ref, dg_ref, dlg_ref, dlb_ref):
        i = pl.program_id(0)

        @pl.when(i % tps == 0)
        def _():
            for r in (dsc_ref, dsh_ref, dg_ref):
                r[...] = jnp.zeros_like(r)

        @pl.when(i == 0)
        def _():
            dlg_ref[...] = jnp.zeros_like(dlg_ref)
            dlb_ref[...] = jnp.zeros_like(dlb_ref)

        xv = x_ref[...]
        _, pull = jax.vjp(ln, xv, f_ref[...], g_ref[0], lg_ref[...], lb_ref[...])
        dx_res, df, dg, dlg, dlb = pull((dy_ref[...],))
        dfb = df.astype(BF16)
        df_ref[...] = dfb
        dh = jnp.zeros((tmk, D_MODEL), F32)
        for j in range(D_FF // FF_CHUNK):
            sl = slice(j * FF_CHUNK, (j + 1) * FF_CHUNK)
            ds = _dg(dfb, w2_ref[sl, :], 1, 1)
            av, bv = a_ref[:, sl], b_ref[:, sl]
            sg = jax.nn.sigmoid(av)
            si = av * sg
            s_ref[:, sl] = (si * bv).astype(BF16)
            da = (ds * bv * (sg * (1.0 + av * (1.0 - sg)))).astype(BF16)
            db = (ds * si).astype(BF16)
            da_ref[:, sl] = da
            db_ref[:, sl] = db
            dh = dh + _dg(da, w1_ref[sl, :], 1, 0) + _dg(db, w3_ref[sl, :], 1, 0)
        dx_ref[...] = dx_res + dh * (1.0 + sc_ref[0])
        dsc_ref[0] += jnp.sum(dh * xv, axis=0, keepdims=True)
        dsh_ref[0] += jnp.sum(dh, axis=0, keepdims=True)
        dg_ref[0] += dg
        dlg_ref[...] += dlg
        dlb_ref[...] += dlb

    row = lambda c: pl.BlockSpec((tmk, c), lambda i: (i, 0))
    per_seq = pl.BlockSpec((1, 1, D_MODEL), lambda i: (i // tps, 0, 0))
    vec = pl.BlockSpec((1, D_MODEL), lambda i: (0, 0))
    seq_shape = jax.ShapeDtypeStruct(sc.shape, F32)
    vec_shape = jax.ShapeDtypeStruct((1, D_MODEL), F32)
    (dx, da, db, s, df, dsc, dsh, dg, dlg, dlb), landed = hosted_call(
        body, name=tag + "_bwd", grid=(t // tmk,),
        in_specs=[row(D_MODEL), row(D_MODEL), row(D_MODEL), row(D_FF), row(D_FF), per_seq, per_seq, per_seq, vec, vec,
                  _resident((D_FF, D_MODEL)), _resident((D_FF, D_MODEL)), _resident((D_FF, D_MODEL))],
        out_specs=[row(D_MODEL), row(D_FF), row(D_FF), row(D_FF), row(D_MODEL), per_seq, per_seq, per_seq, vec, vec],
        out_shape=[jax.ShapeDtypeStruct((t, D_MODEL), F32), jax.ShapeDtypeStruct((t, D_FF), BF16),
                   jax.ShapeDtypeStruct((t, D_FF), BF16), jax.ShapeDtypeStruct((t, D_FF), BF16),
                   jax.ShapeDtypeStruct((t, D_MODEL), BF16), seq_shape, seq_shape, seq_shape, vec_shape, vec_shape],
        args=(dy, x, f, a, b, sc, sh, g, lg, lb, w1, w3, w2), ride=ride)
    tt = min(DW_TOKENS, seq)
    shards = lambda dw: dw.reshape(N_DEV, D_FF // N_DEV, D_MODEL)
    if chain is None:
        dw2, landed = mm_tn(tag + "_dw2", s, df, D_FF // 2, D_MODEL, tt, BF16), []
    else:
        dw2, landed = mm_tn(tag + "_dw2", s, df, D_FF // 2, D_MODEL, tt, BF16, ride=chain((dsh, dsc, dg), dlg, dlb))
    dw1, (s_w2,) = mm_tn(tag + "_dw1", da, h, D_FF // 2, D_MODEL, tt, BF16, ride=([shards(dw2)], [False]))
    dw3, (s_w1,) = mm_tn(tag + "_dw3", db, h, D_FF // 2, D_MODEL, tt, BF16, ride=([shards(dw1)], [False]))
    return dx, (dsh, dsc, dg), (s_w1, shards(dw3), s_w2, dlg, dlb), landed


def kernel(x, c, w_ada, b_ada, ffn1_w1, ffn1_w3, ffn1_w2, ln1_g, ln1_b, w_in, conv_w, conv_b, dt_bias, a_log, d_ssd, ssd_norm_w, s5_a_re, s5_a_im, s5_log_dt, s5_b_re, s5_b_im, s5_c_re, s5_c_im, s5_d, w_glu, b_glu, w_out, ln2_g, ln2_b, ffn2_w1, ffn2_w3, ffn2_w2, ln3_g, ln3_b, loss_target, m_w_ada, m_b_ada, m_ffn1_w1, m_ffn1_w3, m_ffn1_w2, m_ln1_g, m_ln1_b, m_w_in, m_conv_w, m_conv_b, m_dt_bias, m_a_log, m_d_ssd, m_ssd_norm_w, m_s5_a_re, m_s5_a_im, m_s5_log_dt, m_s5_b_re, m_s5_b_im, m_s5_c_re, m_s5_c_im, m_s5_d, m_w_glu, m_b_glu, m_w_out, m_ln2_g, m_ln2_b, m_ffn2_w1, m_ffn2_w3, m_ffn2_w2, m_ln3_g, m_ln3_b, v_w_ada, v_b_ada, v_ffn1_w1, v_ffn1_w3, v_ffn1_w2, v_ln1_g, v_ln1_b, v_w_in, v_conv_w, v_conv_b, v_dt_bias, v_a_log, v_d_ssd, v_ssd_norm_w, v_s5_a_re, v_s5_a_im, v_s5_log_dt, v_s5_b_re, v_s5_b_im, v_s5_c_re, v_s5_c_im, v_s5_d, v_w_glu, v_b_glu, v_w_out, v_ln2_g, v_ln2_b, v_ffn2_w1, v_ffn2_w3, v_ffn2_w2, v_ln3_g, v_ln3_b):
    given = dict(locals())
    bsz, seq, _ = x.shape
    t = bsz * seq
    tm = min(1024, seq)
    me = 4 * lax.axis_index("x") + 2 * lax.axis_index("y") + lax.axis_index("c")
    x0 = x.reshape(t, D_MODEL)
    target = loss_target.reshape(t, D_MODEL)

    tr16 = lambda w: w[0].T.astype(BF16)
    whole = lambda g: g.reshape(N_DEV * g.shape[1], g.shape[2])
    g_f1w1, g_f1w3, g_f1w2, g_c = gather_two_level(
        "gather_ffn1", [tr16(ffn1_w1), tr16(ffn1_w3), ffn1_w2[0].astype(BF16), c])
    f1w1, f1w3, f1w2 = whole(g_f1w1), whole(g_f1w3), whole(g_f1w2)
    c_all = whole(g_c)

    n_loc = w_ada.shape[2]
    b_loc = lax.dynamic_slice(b_ada, (0, me * n_loc), (1, n_loc))
    mod_cols = ada_fwd(c_all, w_ada[0], b_loc)
    g_mod, = exchange("gather_mod", [mod_cols], [True])
    mine = lax.dynamic_slice(g_mod, (0, me * bsz, 0), (N_DEV, bsz, n_loc))
    mod = jnp.transpose(mine, (1, 0, 2)).reshape(bsz, N_MOD, 1, D_MODEL)
    sh1, sc1, g1, sh2, sc2, g2, sh3, sc3, g3 = [mod[:, k] for k in range(N_MOD)]

    x1, res1, (g_win, g_glu, g_out, g_conv, g_f2w1) = _ffn_fwd(
        "ffn1", x0, sc1, sh1, g1, f1w1, f1w3, f1w2, ln1_g, ln1_b, seq, tm,
        ride=([tr16(w_in), w_glu[0].astype(BF16), w_out[0].astype(BF16), conv_w[0], tr16(ffn2_w1)], [True] * 5))
    win = whole(g_win)
    wglu = whole(g_glu).astype(F32)
    wout = whole(g_out)
    wo_ssd, wo_s5 = wout[:SSD_WIDTH], wout[SSD_WIDTH:]
    convw = jnp.transpose(g_conv, (1, 0, 2)).reshape(CONV_K, CONV_CH)
    w_z, w_xbc = win[:SSD_WIDTH], win[SSD_WIDTH:SSD_WIDTH + CONV_CH]
    w_dt = win[SSD_WIDTH + CONV_CH:SSD_WIDTH + CONV_CH + SSD_HEADS]
    w_u = win[SSD_WIDTH + CONV_CH + SSD_HEADS:]
    dt_pad = [jnp.pad(w_dt[HEADS_PER_GROUP * g:HEADS_PER_GROUP * (g + 1)], ((0, LANE - HEADS_PER_GROUP), (0, 0)))
              for g in range(SSD_GROUPS)]
    w_dtp = jnp.concatenate(dt_pad, axis=0)
    w_proj = jnp.concatenate([w_xbc, w_z, w_u, w_dtp], axis=0)

    proj, h2 = modulate_proj("mix_proj", x1, sc2, sh2, w_proj, seq, tm, P_COLS)
    xc = conv_fwd(proj, convw, conv_b, seq, tm)
    dtb = jnp.pad(dt_bias.reshape(SSD_GROUPS, 1, HEADS_PER_GROUP), ((0, 0), (0, 0), (0, LANE - HEADS_PER_GROUP)))
    alog = jnp.pad(a_log.reshape(SSD_GROUPS, 1, HEADS_PER_GROUP), ((0, 0), (0, 0), (0, LANE - HEADS_PER_GROUP)))
    dcol = jnp.pad(d_ssd.reshape(SSD_GROUPS, HEADS_PER_GROUP, 1), ((0, 0), (0, LANE - HEADS_PER_GROUP), (0, 0)))
    nw = ssd_norm_w.reshape(SSD_GROUPS, 1, GROUP_COLS)
    (y_ssd, hprev), (g_f2w3,) = ssd_fwd(xc, proj, dtb, alog, dcol, nw, bsz, seq,
                                        ride=([tr16(ffn2_w3)], [True]))

    a_re2, a_im2, ldt2 = s5_a_re[0], s5_a_im[0], s5_log_dt.reshape(S5_GROUPS, 1)
    ab_re, ab_im, f_re, f_im = _whole(_disc_a, "s5_disc_a", [a_re2, a_im2, ldt2], [(S5_GROUPS, S5_STATE)] * 4)
    b_re2, b_im2 = s5_b_re.reshape(S5_COLS, S5_GROUP_CH), s5_b_im.reshape(S5_COLS, S5_GROUP_CH)
    fr_col, fi_col = f_re.reshape(S5_COLS, 1), f_im.reshape(S5_COLS, 1)
    bb_re, bb_im = _whole(_disc_b, "s5_disc_b", [fr_col, fi_col, b_re2, b_im2], [(S5_COLS, S5_GROUP_CH)] * 2)
    wb_re, wb_im = _blockdiag_b(bb_re).astype(BF16), _blockdiag_b(bb_im).astype(BF16)
    wc_re, wc_im = _blockdiag_c(s5_c_re[0]).astype(BF16), _blockdiag_c(s5_c_im[0]).astype(BF16)
    dt5 = jnp.exp(ldt2)
    lam_re, lam_im = (dt5 * a_re2).reshape(1, S5_COLS), (dt5 * a_im2).reshape(1, S5_COLS)
    sf_re, sf_im, sb_re, sb_im, cf_re, cf_im, cb_re, cb_im = s5_tables(lam_re, lam_im)
    d5 = s5_d.reshape(S5_Q, 1, LANE)
    (y5, xr_all, xi_all), (g_f2w2,) = s5_fwd(
        proj, wb_re, wb_im, wc_re, wc_im, sf_re, sf_im, cf_re, cf_im, d5, bsz, seq,
        ride=([ffn2_w2[0].astype(BF16)], [True]))
    f2w1, f2w3, f2w2 = whole(g_f2w1), whole(g_f2w3), whole(g_f2w2)
    o5, = rowwise_fwd("s5_glu", f_glu, [y5], [], [wglu, b_glu], [(S5_WIDTH, F32)], seq, tm)

    mix = mm_nn("mix_out", [y_ssd, o5], [wo_ssd, wo_s5], tm, D_MODEL)
    x2, = rowwise_fwd("mix_ln", _res_ln(1.0), [x1, mix], [g2], [ln2_g, ln2_b], [(D_MODEL, F32)], seq, tm)

    (dy, loss_loc), res3, _ = _ffn_fwd("ffn2", x2, sc3, sh3, g3, f2w1, f2w3, f2w2, ln3_g, ln3_b, seq, tm, target=target)

    dx2, dmod3, (s_f2w1, d_f2w3, s_f2w2, d_ln3g, d_ln3b), _ = _ffn_bwd(
        "ffn2", dy, x2, sc3, sh3, g3, f2w1, f2w3, f2w2, ln3_g, ln3_b, res3, seq, tm)

    (dx1_a, dmix), (dg2,), (d_ln2g, d_ln2b) = rowwise_bwd(
        "mix_ln_b", _res_ln(1.0), [x1, mix], [g2], [ln2_g, ln2_b], [dx2], seq, tm, [F32, BF16])
    tw = min(DW_TOKENS, seq)
    d_wo = jnp.concatenate([mm_tn("mix_dwo_ssd", y_ssd, dmix, SSD_WIDTH, D_MODEL, tw, BF16),
                            mm_tn("mix_dwo_s5", o5, dmix, S5_WIDTH, D_MODEL, tw, BF16)], axis=0)
    dy_mixed = mm_nt("mix_dy", [dmix], [wout], tm, D_MODEL)
    dy_ssd, do5 = dy_mixed, (dy_mixed, SSD_WIDTH, S5_WIDTH)

    (dy5,), _, (d_wglu, d_bglu) = rowwise_bwd("s5_glu_b", f_glu, [y5], [], [wglu, b_glu], [do5], seq, tm, [F32])
    (du, dwbr, dwbi, dwcr, dwci, dab_re, dab_im, dd5), (s_f2w3, s_out, s_glu) = s5_bwd(
        proj, wb_re, wb_im, wc_re, wc_im, sb_re, sb_im, cb_re, cb_im, d5, xr_all, xi_all, dy5, bsz, seq,
        ride=([d_f2w3, d_wo.reshape(N_DEV, D_MODEL // N_DEV, D_MODEL),
               d_wglu.reshape(N_DEV, S5_WIDTH // N_DEV, S5_WIDTH).astype(BF16)], [False] * 3))
    dbb_re, dbb_im = _unblock_b(dwbr), _unblock_b(dwbi)
    dfr_col, dfi_col, d_b_re, d_b_im = _whole_vjp(_disc_b, "s5_disc_b_b", [fr_col, fi_col, b_re2, b_im2],
                                                  [dbb_re, dbb_im])
    d_a_re, d_a_im, d_ldt = _whole_vjp(
        _disc_a, "s5_disc_a_b", [a_re2, a_im2, ldt2],
        [dab_re.reshape(S5_GROUPS, S5_STATE), dab_im.reshape(S5_GROUPS, S5_STATE),
         dfr_col.reshape(S5_GROUPS, S5_STATE), dfi_col.reshape(S5_GROUPS, S5_STATE)])
    d_c_re, d_c_im = _unblock_c(dwcr), _unblock_c(dwci)

    dxs, dbm, dcm, ddt, dz, ddtb, dalog, ddcol, dnw = ssd_bwd(xc, proj, dtb, alog, dcol, nw, hprev, dy_ssd, bsz, seq)
    dpre, d_convw, d_convb = conv_bwd_pre(proj, convw, conv_b, dxs, dbm, dcm, seq, tm)
    dxbc = conv_bwd_x(dpre, convw, seq, tm)

    dw_xbc = mm_tn("mix_dw_xbc", dxbc, h2, CONV_CH, D_MODEL, tw, BF16)
    dw_z = mm_tn("mix_dw_z", dz, h2, SSD_WIDTH, D_MODEL, tw, BF16)
    dw_u = mm_tn("mix_dw_u", du, h2, S5_WIDTH, D_MODEL, tw, BF16)
    dw_dt = mm_tn("mix_dw_dt", ddt, h2, 2 * LANE, D_MODEL, tw, BF16)
    dw_dt8 = jnp.concatenate([dw_dt[LANE * g:LANE * g + HEADS_PER_GROUP] for g in range(SSD_GROUPS)], axis=0)
    d_win = jnp.concatenate([dw_z, dw_xbc, dw_dt8, dw_u], axis=0)
    dh2, (s_win,) = mm_nn("mix_dh", [dxbc, dz, du, ddt], [w_xbc, w_z, w_u, w_dtp], tm, D_MODEL,
                          ride=([d_win.reshape(N_DEV, IN_COLS // N_DEV, D_MODEL)], [False]))
    (dx1,), (dsc2, dsh2), _ = rowwise_bwd("mix_mod_b", f_modulate, [x1], [sc2, sh2], [], [dh2], seq, tm, [F32],
                                          add_rows={0: dx1_a})

    packing = {}

    def small_and_dmod(dmod1, d_ln1g, d_ln1b):
        dmod = jnp.concatenate(list(dmod1) + [dsh2, dsc2, dg2] + list(dmod3), axis=1).reshape(bsz, N_MOD * D_MODEL)
        small = _small_grads(d_ln1g, d_ln1b)
        packing["names"] = list(small)
        packing["shapes"] = [small[k].shape for k in small]
        return [_pack(list(small.values())), dmod], [True, True]

    def _small_grads(d_ln1g, d_ln1b):
        return {
            "ln1_g": d_ln1g, "ln1_b": d_ln1b, "conv_w": d_convw, "conv_b": d_convb,
            "dt_bias": ddtb[:, 0, :HEADS_PER_GROUP].reshape(1, SSD_HEADS),
            "a_log": dalog[:, 0, :HEADS_PER_GROUP].reshape(1, SSD_HEADS),
            "d_ssd": ddcol[:, :HEADS_PER_GROUP, 0].reshape(1, SSD_HEADS),
            "ssd_norm_w": dnw.reshape(1, SSD_WIDTH),
            "s5_a_re": d_a_re[None], "s5_a_im": d_a_im[None], "s5_log_dt": d_ldt.reshape(1, S5_GROUPS),
            "s5_b_re": d_b_re.reshape(s5_b_re.shape), "s5_b_im": d_b_im.reshape(s5_b_im.shape),
            "s5_c_re": d_c_re[None], "s5_c_im": d_c_im[None], "s5_d": dd5.reshape(1, S5_WIDTH),
            "b_glu": d_bglu, "ln2_g": d_ln2g, "ln2_b": d_ln2b, "ln3_g": d_ln3g, "ln3_b": d_ln3b,
            "loss": loss_loc.reshape(1, 1),
        }

    dx0, _, (s_f1w1, d_f1w3, s_f1w2, _, _), (s_small, s_dmod) = _ffn_bwd(
        "ffn1", dx1, x0, sc1, sh1, g1, f1w1, f1w3, f1w2, ln1_g, ln1_b, res1, seq, tm, chain=small_and_dmod)
    names, shapes = packing["names"], packing["shapes"]
    s_f1w3, = exchange("sum_grads", [d_f1w3], [False])

    out = {"grad_x": dx0.reshape(x.shape)}

    def put(name, res, shape):
        for key, val in zip(("grad_", "delta_", "new_m_", "new_v_"), res):
            out[key + name] = val.reshape(shape)

    for name, slots, tr in (("ffn1_w1", s_f1w1, 176), ("ffn1_w3", s_f1w3, 176), ("ffn2_w1", s_f2w1, 176),
                            ("ffn2_w3", s_f2w3, 176), ("w_in", s_win, IN_COLS // N_DEV)):
        w = given[name]
        grad = sum_slots("sum_" + name, slots, tr).T
        put(name, adamw("adam_" + name, grad, w[0], given["m_" + name][0], given["v_" + name][0], 256), w.shape)
    for name, slots in (("ffn1_w2", s_f1w2), ("ffn2_w2", s_f2w2)):
        w = given[name]
        put(name, adamw("adam_" + name, slots, w[0], given["m_" + name][0], given["v_" + name][0], 176), w.shape)
    put("w_glu", adamw("adam_w_glu", s_glu, w_glu[0], m_w_glu[0], v_w_glu[0], 64), w_glu.shape)
    put("w_out", adamw("adam_w_out", s_out, w_out[0], m_w_out[0], v_w_out[0], 128), w_out.shape)

    dmod_all = s_dmod.reshape(N_DEV * bsz, N_MOD * D_MODEL)
    g_bada, g_wada = ada_bwd(c_all, dmod_all, lax.dynamic_slice(dmod_all, (0, me * n_loc), (N_DEV * bsz, n_loc)))
    put("w_ada", adamw("adam_w_ada", g_wada, w_ada[0], m_w_ada[0], v_w_ada[0], 256), w_ada.shape)
    put("b_ada", adamw("adam_b_ada", g_bada, b_ada, m_b_ada, v_b_ada, 1), b_ada.shape)

    not_params = {"conv_w": jnp.zeros((CONV_K, CONV_CH), F32), "loss": jnp.zeros((1, 1), F32)}
    pw, pm, pv = [_pack([not_params[k] if k in not_params else given[pre + k] for k in names]) for pre in ("", "m_", "v_")]
    res_small = adamw("adam_small", s_small, pw, pm, pv, pw.shape[0])
    parts = [_unpack(r, shapes) for r in res_small]
    for i, k in enumerate(names):
        if k not in not_params:
            put(k, [p[i] for p in parts], given[k].shape)
    out["loss"] = parts[0][names.index("loss")][0, 0]
    g_cw = lax.dynamic_slice(parts[0][names.index("conv_w")], (0, me * LANE), (CONV_K, LANE))
    put("conv_w", adamw("adam_conv_w", g_cw, conv_w[0], m_conv_w[0], v_conv_w[0], CONV_K), conv_w.shape)

    order = ["w_ada", "b_ada", "ffn1_w1", "ffn1_w3", "ffn1_w2", "ln1_g", "ln1_b", "w_in", "conv_w", "conv_b", "dt_bias",
             "a_log", "d_ssd", "ssd_norm_w", "s5_a_re", "s5_a_im", "s5_log_dt", "s5_b_re", "s5_b_im", "s5_c_re",
             "s5_c_im", "s5_d", "w_glu", "b_glu", "w_out", "ln2_g", "ln2_b", "ffn2_w1", "ffn2_w3", "ffn2_w2", "ln3_g",
             "ln3_b"]
    return (out["loss"], out["grad_x"], *[out[p + n] for p in ("grad_", "delta_", "new_m_", "new_v_") for n in order])
```

```python
import functools
import math

import jax
import jax.numpy as jnp
from jax import lax
from jax.experimental import pallas as pl
from jax.experimental.pallas import tpu as pltpu

F32 = jnp.float32
BF16 = jnp.bfloat16
MESH = pl.DeviceIdType.MESH

N_DEV = 8
D_MODEL = 1024
D_FF = 2816
N_MOD = 9
SSD_WIDTH = 512
SSD_HEADS = 8
SSD_HEAD_DIM = 64
SSD_GROUPS = 2
SSD_STATE = 128
SSD_CHUNK = 128
GROUP_COLS = SSD_WIDTH // SSD_GROUPS
HEADS_PER_GROUP = SSD_HEADS // SSD_GROUPS
CONV_K = 4
CONV_CH = 1024
S5_WIDTH = 512
S5_GROUPS = 32
S5_GROUP_CH = 16
S5_STATE = 64
S5_COLS = S5_GROUPS * S5_STATE
S5_Q = 4
S5_CHUNK = 2048
ALPHA = 2.0 ** 0.25
LN_EPS = 1e-5
LANE = 128
HALO = 8

P_XBC, P_Z, P_U, P_DT = 0, 1024, 1536, 2048
P_COLS = 2048 + SSD_GROUPS * LANE
IN_COLS = SSD_WIDTH + CONV_CH + SSD_HEADS + S5_WIDTH

ADAM_LR, ADAM_B1, ADAM_B2, ADAM_EPS, ADAM_WD, ADAM_STEP = 0.001, 0.9, 0.999, 1e-08, 0.01, 10

VMEM_LIMIT = 56 * 1024 * 1024


def _cp(*sem):
    return pltpu.CompilerParams(dimension_semantics=sem if sem else None, vmem_limit_bytes=VMEM_LIMIT)


def _dg(a, b, ca, cb):
    return lax.dot_general(a.astype(BF16), b.astype(BF16), (((ca,), (cb,)), ((), ())), preferred_element_type=F32)


@jax.custom_vjp
def bdot_nn(a, b):
    return _dg(a, b, 1, 0)


bdot_nn.defvjp(lambda a, b: (_dg(a, b, 1, 0), (a, b)),
               lambda r, g: (_dg(g, r[1], 1, 1), _dg(r[0], g, 0, 0)))


@jax.custom_vjp
def bdot_nt(a, b):
    return _dg(a, b, 1, 1)


bdot_nt.defvjp(lambda a, b: (_dg(a, b, 1, 1), (a, b)),
               lambda r, g: (_dg(g, r[1], 1, 0), _dg(g, r[0], 0, 0)))


@jax.custom_vjp
def bdot_tn(a, b):
    return _dg(a, b, 0, 0)


bdot_tn.defvjp(lambda a, b: (_dg(a, b, 0, 0), (a, b)),
               lambda r, g: (_dg(r[1], g, 1, 1), _dg(r[0], g, 1, 0)))


def _split3(x):
    def top(v):
        bits = lax.bitcast_convert_type(v, jnp.int32) & jnp.int32(-65536)
        return lax.bitcast_convert_type(bits, F32)

    hi = top(x)
    r1 = x - hi
    mid = top(r1)
    return hi.astype(BF16), mid.astype(BF16), (r1 - mid).astype(BF16)


def _dot3(a, b, ca, cb, split_a):
    dims = (((ca,), (cb,)), ((), ()))
    if split_a:
        c = b.astype(BF16)
        return sum(lax.dot_general(p, c, dims, preferred_element_type=F32) for p in _split3(a))
    c = a.astype(BF16)
    return sum(lax.dot_general(c, p, dims, preferred_element_type=F32) for p in _split3(b))


@jax.custom_vjp
def mask_dot_left(c, x):
    return _dot3(c, x, 1, 0, False)


mask_dot_left.defvjp(lambda c, x: (_dot3(c, x, 1, 0, False), c),
                     lambda c, g: (jnp.zeros_like(c), _dot3(c, g, 0, 0, False)))


@jax.custom_vjp
def mask_dot_right(x, c):
    return _dot3(x, c, 1, 0, True)


mask_dot_right.defvjp(lambda x, c: (_dot3(x, c, 1, 0, True), c),
                      lambda c, g: (_dot3(g, c, 1, 1, True), jnp.zeros_like(c)))


def _take_col(z):
    @jax.custom_vjp
    def take(x):
        return x[:, z:z + 1]

    def bwd(shape, g):
        hot = (lax.broadcasted_iota(jnp.int32, (1, shape[1]), 1) == z).astype(F32)
        return (g * hot,)

    take.defvjp(lambda x: (x[:, z:z + 1], x.shape), bwd)
    return take


def _take_row(z):
    @jax.custom_vjp
    def take(x):
        return x[z:z + 1, :]

    def bwd(shape, g):
        hot = (lax.broadcasted_iota(jnp.int32, (shape[0], 1), 0) == z).astype(F32)
        return (hot * g,)

    take.defvjp(lambda x: (x[z:z + 1, :], x.shape), bwd)
    return take


def _view(a):
    return a if isinstance(a, tuple) else (a, 0, a.shape[1])


def _col_spec(view, rows, width, index):
    _, off, _ = view
    assert off % width == 0
    return pl.BlockSpec((rows, width), lambda *g: (index(*g)[0], off // width + index(*g)[1]))


def _rw_in_specs(rows, bps, gps, tm, tps):
    specs = [_col_spec(_view(r), tm, _view(r)[2], lambda i: (i, 0)) for r in rows]
    specs += [pl.BlockSpec((1, 1, b.shape[2]), lambda i: (i // tps, 0, 0)) for b in bps]
    specs += [pl.BlockSpec(g.shape, lambda i, nd=g.ndim: (0,) * nd) for g in gps]
    return specs


def _rw_vals(refs, nr, nb, ng):
    vals = [r[...] for r in refs[:nr]]
    vals += [b[0] for b in refs[nr:nr + nb]]
    vals += [g[...] for g in refs[nr + nb:nr + nb + ng]]
    return vals


def rowwise_fwd(name, f, rows, bps, gps, outs, seq, tm):
    t = _view(rows[0])[0].shape[0]
    tps = seq // tm
    nr, nb, ng = len(rows), len(bps), len(gps)

    def body(*refs):
        res = f(*_rw_vals(refs, nr, nb, ng))
        for o, v in zip(refs[nr + nb + ng:], res):
            o[...] = v.astype(o.dtype)

    return pl.pallas_call(
        body, name=name, grid=(t // tm,),
        in_specs=_rw_in_specs(rows, bps, gps, tm, tps),
        out_specs=[pl.BlockSpec((tm, c), lambda i: (i, 0)) for c, _ in outs],
        out_shape=[jax.ShapeDtypeStruct((t, c), d) for c, d in outs],
        compiler_params=_cp("arbitrary"),
    )(*[_view(r)[0] for r in rows], *bps, *gps)


def rowwise_bwd(name, f, rows, bps, gps, douts, seq, tm, row_grads, add_rows=None):
    add_rows = add_rows or {}
    t = _view(rows[0])[0].shape[0]
    tps = seq // tm
    nr, nb, ng, nd = len(rows), len(bps), len(gps), len(douts)
    want = [k for k in range(nr) if row_grads[k] is not None]
    adds = sorted(add_rows)
    n_in = nr + nb + ng + nd + len(adds)

    def body(*refs):
        vals = _rw_vals(refs, nr, nb, ng)
        dvals = tuple(r[...] for r in refs[nr + nb + ng:nr + nb + ng + nd])
        add_refs = dict(zip(adds, refs[nr + nb + ng + nd:n_in]))
        out_refs = refs[n_in:]
        _, pull = jax.vjp(f, *vals)
        grads = pull(dvals)
        i = pl.program_id(0)
        for o, k in zip(out_refs, want):
            g = grads[k]
            if k in add_refs:
                g = g + add_refs[k][...]
            o[...] = g.astype(o.dtype)
        for j in range(nb):
            o = out_refs[len(want) + j]

            @pl.when(i % tps == 0)
            def _(o=o):
                o[...] = jnp.zeros_like(o)

            o[0] = o[0] + grads[nr + j]
        for j in range(ng):
            o = out_refs[len(want) + nb + j]

            @pl.when(i == 0)
            def _(o=o):
                o[...] = jnp.zeros_like(o)

            o[...] = o[...] + grads[nr + nb + j]

    in_specs = _rw_in_specs(rows, bps, gps, tm, tps)
    in_specs += [_col_spec(_view(d), tm, _view(d)[2], lambda i: (i, 0)) for d in douts]
    in_specs += [pl.BlockSpec((tm, add_rows[k].shape[1]), lambda i: (i, 0)) for k in adds]
    out_specs = [pl.BlockSpec((tm, _view(rows[k])[2]), lambda i: (i, 0)) for k in want]
    out_shape = [jax.ShapeDtypeStruct((t, _view(rows[k])[2]), row_grads[k]) for k in want]
    out_specs += [pl.BlockSpec((1, 1, b.shape[2]), lambda i: (i // tps, 0, 0)) for b in bps]
    out_shape += [jax.ShapeDtypeStruct(b.shape, F32) for b in bps]
    out_specs += [pl.BlockSpec(g.shape, lambda i, n=g.ndim: (0,) * n) for g in gps]
    out_shape += [jax.ShapeDtypeStruct(g.shape, F32) for g in gps]
    res = pl.pallas_call(
        body, name=name, grid=(t // tm,), in_specs=in_specs, out_specs=out_specs, out_shape=out_shape,
        compiler_params=_cp("arbitrary"),
    )(*[_view(r)[0] for r in rows], *bps, *gps, *[_view(d)[0] for d in douts], *[add_rows[k] for k in adds])
    nw = len(want)
    return res[:nw], res[nw:nw + nb], res[nw + nb:]


def mm_nn(name, xs, ws, tm, tn, out_dtype=F32, ride=None):
    views = [_view(x) for x in xs]
    t, n, k = views[0][0].shape[0], ws[0].shape[1], len(xs)

    def body(*refs):
        acc = _dg(refs[0][...], refs[k][...], 1, 0)
        for i in range(1, k):
            acc = acc + _dg(refs[i][...], refs[k + i][...], 1, 0)
        refs[2 * k][...] = acc.astype(out_dtype)

    in_specs = [_col_spec(v, tm, v[2], lambda i, j: (i, 0)) for v in views]
    in_specs += [pl.BlockSpec((w.shape[0], tn), lambda i, j: (0, j)) for w in ws]
    out_spec = pl.BlockSpec((tm, tn), lambda i, j: (i, j))
    out_shape = jax.ShapeDtypeStruct((t, n), out_dtype)
    if ride is not None:
        (res,), landed = hosted_call(body, name=name, grid=(t // tm, n // tn), in_specs=in_specs, out_specs=[out_spec],
                                     out_shape=[out_shape], args=(*[v[0] for v in views], *ws), ride=ride)
        return res, landed
    return pl.pallas_call(
        body, name=name, grid=(t // tm, n // tn), in_specs=in_specs, out_specs=out_spec, out_shape=out_shape,
        compiler_params=_cp("parallel", "parallel"),
    )(*[v[0] for v in views], *ws)


def mm_nt(name, dys, ws, tm, tk, out_dtype=F32, ride=None):
    views = [_view(d) for d in dys]
    t, kk, k = views[0][0].shape[0], ws[0].shape[0], len(dys)

    def body(*refs):
        acc = _dg(refs[0][...], refs[k][...], 1, 1)
        for i in range(1, k):
            acc = acc + _dg(refs[i][...], refs[k + i][...], 1, 1)
        refs[2 * k][...] = acc.astype(out_dtype)

    in_specs = [_col_spec(v, tm, v[2], lambda i, j: (i, 0)) for v in views]
    in_specs += [pl.BlockSpec((tk, w.shape[1]), lambda i, j: (j, 0)) for w in ws]
    out_spec = pl.BlockSpec((tm, tk), lambda i, j: (i, j))
    out_shape = jax.ShapeDtypeStruct((t, kk), out_dtype)
    if ride is not None:
        (res,), landed = hosted_call(body, name=name, grid=(t // tm, kk // tk), in_specs=in_specs, out_specs=[out_spec],
                                     out_shape=[out_shape], args=(*[v[0] for v in views], *ws), ride=ride)
        return res, landed
    return pl.pallas_call(
        body, name=name, grid=(t // tm, kk // tk), in_specs=in_specs, out_specs=out_spec, out_shape=out_shape,
        compiler_params=_cp("parallel", "parallel"),
    )(*[v[0] for v in views], *ws)


def modulate_proj(name, x, sc, sh, w, seq, tm, tk):
    t, kdim = x.shape
    n = w.shape[0]
    tps = seq // tm

    def body(x_ref, sc_ref, sh_ref, w_ref, o_ref, h_ref):
        h = (x_ref[...] * (1.0 + sc_ref[0]) + sh_ref[0]).astype(BF16)
        h_ref[...] = h
        o_ref[...] = _dg(h, w_ref[...], 1, 1)

    per_seq = pl.BlockSpec((1, 1, kdim), lambda i, j: (i // tps, 0, 0))
    return pl.pallas_call(
        body, name=name, grid=(t // tm, n // tk),
        in_specs=[pl.BlockSpec((tm, kdim), lambda i, j: (i, 0)), per_seq, per_seq,
                  pl.BlockSpec((tk, kdim), lambda i, j: (j, 0))],
        out_specs=[pl.BlockSpec((tm, tk), lambda i, j: (i, j)), pl.BlockSpec((tm, kdim), lambda i, j: (i, 0))],
        out_shape=[jax.ShapeDtypeStruct((t, n), F32), jax.ShapeDtypeStruct((t, kdim), BF16)],
        compiler_params=_cp("arbitrary", "arbitrary"),
    )(x, sc, sh, w)


def proj_res_ln(name, xs, ws, x, g, lg, lb, coef, seq, tm):
    t, d = x.shape
    tps = seq // tm
    k = len(xs)
    ln = _res_ln(coef)

    def body(*refs):
        x_ref, g_ref, lg_ref, lb_ref, m_ref, y_ref = refs[2 * k:]
        m = _dg(refs[0][...], refs[k][...], 1, 0)
        for i in range(1, k):
            m = m + _dg(refs[i][...], refs[k + i][...], 1, 0)
        m_ref[...] = m
        y_ref[...] = ln(x_ref[...], m, g_ref[0], lg_ref[...], lb_ref[...])[0]

    row = lambda c: pl.BlockSpec((tm, c), lambda i: (i, 0))
    vec = pl.BlockSpec((1, d), lambda i: (0, 0))
    return pl.pallas_call(
        body, name=name, grid=(t // tm,),
        in_specs=[row(a.shape[1]) for a in xs] + [pl.BlockSpec(w.shape, lambda i: (0, 0)) for w in ws]
        + [row(d), pl.BlockSpec((1, 1, d), lambda i: (i // tps, 0, 0)), vec, vec],
        out_specs=[row(d), row(d)], out_shape=[jax.ShapeDtypeStruct((t, d), F32)] * 2,
        compiler_params=_cp("arbitrary"),
    )(*xs, *ws, x, g, lg, lb)


def mm_tn(name, x, dy, tk, tn, tt, out_dtype=F32, ride=None):
    xv, dv = _view(x), _view(dy)
    t, kk, n = xv[0].shape[0], xv[2], dv[2]
    steps = t // tt

    def body(x_ref, d_ref, o_ref, acc_ref):
        @pl.when(pl.program_id(2) == 0)
        def _():
            acc_ref[...] = jnp.zeros_like(acc_ref)

        acc_ref[...] += _dg(x_ref[...], d_ref[...], 0, 0)

        @pl.when(pl.program_id(2) == steps - 1)
        def _():
            o_ref[...] = acc_ref[...].astype(out_dtype)

    in_specs = [_col_spec(xv, tt, tk, lambda a, b, c: (c, a)), _col_spec(dv, tt, tn, lambda a, b, c: (c, b))]
    out_spec = pl.BlockSpec((tk, tn), lambda a, b, c: (a, b))
    out_shape = jax.ShapeDtypeStruct((kk, n), out_dtype)
    if ride is not None:
        (res,), landed = hosted_call(body, name=name, grid=(kk // tk, n // tn, steps), in_specs=in_specs,
                                     out_specs=[out_spec], out_shape=[out_shape], scratch=[pltpu.VMEM((tk, tn), F32)],
                                     args=(xv[0], dv[0]), ride=ride)
        return res, landed
    return pl.pallas_call(
        body, name=name, grid=(kk // tk, n // tn, steps), in_specs=in_specs, out_specs=out_spec, out_shape=out_shape,
        scratch_shapes=[pltpu.VMEM((tk, tn), F32)],
        compiler_params=_cp("parallel", "parallel", "arbitrary"),
    )(xv[0], dv[0])


def _silu(x):
    return x * jax.nn.sigmoid(x)


def f_modulate(x, sc, sh):
    return (x * (1.0 + sc) + sh,)


def _res_ln(coef):
    def f(x, y, g, lg, lb):
        r = ALPHA * x + (coef * g) * y
        mu = jnp.mean(r, axis=-1, keepdims=True)
        d = r - mu
        var = jnp.mean(d * d, axis=-1, keepdims=True)
        return (d * lax.rsqrt(var + LN_EPS) * lg + lb,)
    return f


def f_glu(y, w, b):
    g = jax.nn.gelu(y)
    return (g * jax.nn.sigmoid(bdot_nn(g, w) + b),)


def _shift_down(x, halo, k):
    if k == 0:
        return x
    r = pltpu.roll(x, k, 0)
    hr = pltpu.roll(halo, k, 0)
    row = lax.broadcasted_iota(jnp.int32, (HALO, 1), 0)
    top = jnp.where(row < k, hr, r[:HALO])
    return jnp.concatenate([top, r[HALO:]], axis=0)


def _shift_up(x, halo, k):
    if k == 0:
        return x
    n = x.shape[0]
    r = pltpu.roll(x, n - k, 0)
    hr = pltpu.roll(halo, HALO - k, 0)
    row = lax.broadcasted_iota(jnp.int32, (HALO, 1), 0)
    bot = jnp.where(row >= HALO - k, hr, r[n - HALO:])
    return jnp.concatenate([r[:n - HALO], bot], axis=0)


def _conv_pre(x, halo, w, b):
    acc = x * w[CONV_K - 1:CONV_K, :] + b
    for k in range(1, CONV_K):
        acc = acc + _shift_down(x, halo, k) * w[CONV_K - 1 - k:CONV_K - k, :]
    return acc


def _rows_before(width, tm):
    return pl.BlockSpec((HALO, width), lambda i: (jnp.maximum(i * (tm // HALO) - 1, 0), 0))


def conv_fwd(proj, w, b, seq, tm):
    t = proj.shape[0]
    tps = seq // tm

    def body(x_ref, h_ref, w_ref, b_ref, o_ref):
        first = (pl.program_id(0) % tps == 0)
        halo = jnp.where(first, 0.0, h_ref[...])
        o_ref[...] = _silu(_conv_pre(x_ref[...], halo, w_ref[...], b_ref[...]))

    return pl.pallas_call(
        body, name="conv_fwd", grid=(t // tm,),
        in_specs=[pl.BlockSpec((tm, CONV_CH), lambda i: (i, 0)), _rows_before(CONV_CH, tm),
                  pl.BlockSpec((CONV_K, CONV_CH), lambda i: (0, 0)), pl.BlockSpec((1, CONV_CH), lambda i: (0, 0))],
        out_specs=pl.BlockSpec((tm, CONV_CH), lambda i: (i, 0)),
        out_shape=jax.ShapeDtypeStruct((t, CONV_CH), F32),
        compiler_params=_cp("arbitrary"),
    )(proj, proj, w, b)


def conv_bwd_pre(proj, w, b, dxs, dbm, dcm, seq, tm):
    t = proj.shape[0]
    tps = seq // tm

    def body(x_ref, h_ref, w_ref, b_ref, d1, d2, d3, dp_ref, dw_ref, db_ref):
        i = pl.program_id(0)
        halo = jnp.where(i % tps == 0, 0.0, h_ref[...])
        x = x_ref[...]
        pre = _conv_pre(x, halo, w_ref[...], b_ref[...])
        sg = jax.nn.sigmoid(pre)
        dout = jnp.concatenate([d1[...], d2[...], d3[...]], axis=1)
        dp = dout * (sg * (1.0 + pre * (1.0 - sg)))
        dp_ref[...] = dp

        @pl.when(i == 0)
        def _():
            dw_ref[...] = jnp.zeros_like(dw_ref)
            db_ref[...] = jnp.zeros_like(db_ref)

        db_ref[...] += jnp.sum(dp, axis=0, keepdims=True)
        for k in range(CONV_K):
            j = CONV_K - 1 - k
            dw_ref[j:j + 1, :] += jnp.sum(dp * _shift_down(x, halo, k), axis=0, keepdims=True)

    return pl.pallas_call(
        body, name="conv_bwd_pre", grid=(t // tm,),
        in_specs=[pl.BlockSpec((tm, CONV_CH), lambda i: (i, 0)), _rows_before(CONV_CH, tm),
                  pl.BlockSpec((CONV_K, CONV_CH), lambda i: (0, 0)), pl.BlockSpec((1, CONV_CH), lambda i: (0, 0)),
                  pl.BlockSpec((tm, 512), lambda i: (i, 0)), pl.BlockSpec((tm, 256), lambda i: (i, 0)),
                  pl.BlockSpec((tm, 256), lambda i: (i, 0))],
        out_specs=[pl.BlockSpec((tm, CONV_CH), lambda i: (i, 0)), pl.BlockSpec((CONV_K, CONV_CH), lambda i: (0, 0)),
                   pl.BlockSpec((1, CONV_CH), lambda i: (0, 0))],
        out_shape=[jax.ShapeDtypeStruct((t, CONV_CH), F32), jax.ShapeDtypeStruct((CONV_K, CONV_CH), F32),
                   jax.ShapeDtypeStruct((1, CONV_CH), F32)],
        compiler_params=_cp("arbitrary"),
    )(proj, proj, w, b, dxs, dbm, dcm)


def conv_bwd_x(dpre, w, seq, tm):
    t = dpre.shape[0]
    tps = seq // tm
    blocks = tm // HALO
    last = t // HALO - 1

    def body(d_ref, h_ref, w_ref, o_ref):
        halo = jnp.where(pl.program_id(0) % tps == tps - 1, 0.0, h_ref[...])
        d = d_ref[...]
        w = w_ref[...]
        acc = d * w[CONV_K - 1:CONV_K, :]
        for k in range(1, CONV_K):
            acc = acc + _shift_up(d, halo, k) * w[CONV_K - 1 - k:CONV_K - k, :]
        o_ref[...] = acc

    return pl.pallas_call(
        body, name="conv_bwd_x", grid=(t // tm,),
        in_specs=[pl.BlockSpec((tm, CONV_CH), lambda i: (i, 0)),
                  pl.BlockSpec((HALO, CONV_CH), lambda i: (jnp.minimum((i + 1) * blocks, last), 0)),
                  pl.BlockSpec((CONV_K, CONV_CH), lambda i: (0, 0))],
        out_specs=pl.BlockSpec((tm, CONV_CH), lambda i: (i, 0)),
        out_shape=jax.ShapeDtypeStruct((t, CONV_CH), F32),
        compiler_params=_cp("arbitrary"),
    )(dpre, dpre, w)


def _softplus(x):
    return jnp.maximum(x, 0.0) + jnp.log1p(jnp.exp(-jnp.abs(x)))


def _ssd_chunk(xs, bg, cg, dtr, zz, hp, dtb, alog, dcol, nw):
    l = xs.shape[0]
    row = lax.broadcasted_iota(jnp.int32, (l, l), 0)
    col = lax.broadcasted_iota(jnp.int32, (l, l), 1)
    causal = row >= col
    tril = causal.astype(F32)
    expand = (lax.broadcasted_iota(jnp.int32, (LANE, GROUP_COLS), 1) // SSD_HEAD_DIM
              == lax.broadcasted_iota(jnp.int32, (LANE, GROUP_COLS), 0)).astype(F32)
    head_of_col = lax.broadcasted_iota(jnp.int32, (1, GROUP_COLS), 1) // SSD_HEAD_DIM
    last_row = (lax.broadcasted_iota(jnp.int32, (l, 1), 0) == l - 1).astype(F32)

    dtc = _softplus(dtr + dtb)
    a_c = dtc * (-jnp.exp(alog))
    acs_c = mask_dot_left(tril, a_c)
    dt_e = mask_dot_right(dtc, expand)
    acs_e = mask_dot_right(acs_c, expand)
    alast_e = jnp.sum(acs_e * last_row, axis=0, keepdims=True)
    x = xs * dt_e
    states = bdot_tn(bg, x * jnp.exp(alast_e - acs_e))
    h_next = jnp.exp(alast_e) * hp + states
    d_e = jnp.sum(dcol * expand, axis=0, keepdims=True)
    y = bdot_nn(cg, hp) * jnp.exp(acs_e) + d_e * xs
    cb = bdot_nt(cg, bg)
    acs_t = acs_c.T
    for z in range(HEADS_PER_GROUP):
        seg = _take_col(z)(acs_c) - _take_row(z)(acs_t)
        lmat = jnp.exp(jnp.where(causal, seg, -1e30))
        y = y + bdot_nn(cb * lmat, x * (head_of_col == z).astype(F32))
    yz = y * _silu(zz)
    ms = jnp.mean(yz * yz, axis=-1, keepdims=True)
    return yz * lax.rsqrt(ms + LN_EPS) * nw, h_next


SSD_SUB = 4
SSD_ROWS = SSD_SUB * SSD_CHUNK


def _ssd_in_specs(steps, rev):
    def tok(b, c):
        return b * steps + (steps - 1 - c if rev else c)

    whole = lambda *shape: pl.BlockSpec(shape, lambda b, c: (0,) * len(shape))
    both = SSD_GROUPS * SSD_STATE
    return [
        pl.BlockSpec((SSD_ROWS, SSD_WIDTH), lambda b, c: (tok(b, c), 0)),
        pl.BlockSpec((SSD_ROWS, both), lambda b, c: (tok(b, c), SSD_WIDTH // both)),
        pl.BlockSpec((SSD_ROWS, both), lambda b, c: (tok(b, c), SSD_WIDTH // both + 1)),
        pl.BlockSpec((SSD_ROWS, SSD_GROUPS * LANE), lambda b, c: (tok(b, c), P_DT // (SSD_GROUPS * LANE))),
        pl.BlockSpec((SSD_ROWS, SSD_WIDTH), lambda b, c: (tok(b, c), P_Z // SSD_WIDTH)),
        whole(SSD_GROUPS, 1, LANE), whole(SSD_GROUPS, 1, LANE), whole(SSD_GROUPS, LANE, 1),
        whole(SSD_GROUPS, 1, GROUP_COLS),
    ], tok


def _piece(ref, s, g, width):
    return ref[s * SSD_CHUNK:(s + 1) * SSD_CHUNK, g * width:(g + 1) * width]


def ssd_fwd(xc, proj, dtb, alog, dcol, nw, bsz, seq, ride=None):
    t = xc.shape[0]
    nc = seq // SSD_CHUNK
    steps = nc // SSD_SUB
    in_specs, tok = _ssd_in_specs(steps, False)

    def body(xs, bm, cm, dtr, zz, dtb_r, alog_r, dcol_r, nw_r, y_ref, hp_ref, h_scr):
        @pl.when(pl.program_id(1) == 0)
        def _():
            h_scr[...] = jnp.zeros_like(h_scr)

        for g in range(SSD_GROUPS):
            h = h_scr[g]
            for s in range(SSD_SUB):
                hp_ref[g, 0, s] = h
                y, h = _ssd_chunk(_piece(xs, s, g, GROUP_COLS), _piece(bm, s, g, SSD_STATE),
                                  _piece(cm, s, g, SSD_STATE), _piece(dtr, s, g, LANE), _piece(zz, s, g, GROUP_COLS), h,
                                  dtb_r[g], alog_r[g], dcol_r[g], nw_r[g])
                y_ref[s * SSD_CHUNK:(s + 1) * SSD_CHUNK, g * GROUP_COLS:(g + 1) * GROUP_COLS] = y
            h_scr[g] = h

    return hosted_call(
        body, name="ssd_fwd", grid=(bsz, steps), in_specs=in_specs,
        out_specs=[pl.BlockSpec((SSD_ROWS, SSD_WIDTH), lambda b, c: (tok(b, c), 0)),
                   pl.BlockSpec((SSD_GROUPS, 1, SSD_SUB, SSD_STATE, GROUP_COLS), lambda b, c: (0, b, c, 0, 0))],
        out_shape=[jax.ShapeDtypeStruct((t, SSD_WIDTH), F32),
                   jax.ShapeDtypeStruct((SSD_GROUPS, bsz, nc, SSD_STATE, GROUP_COLS), F32)],
        scratch=[pltpu.VMEM((SSD_GROUPS, SSD_STATE, GROUP_COLS), F32)],
        args=(xc, xc, xc, proj, proj, dtb, alog, dcol, nw), ride=ride)


def ssd_bwd(xc, proj, dtb, alog, dcol, nw, hprev, dy, bsz, seq):
    t = xc.shape[0]
    nc = seq // SSD_CHUNK
    steps = nc // SSD_SUB
    in_specs, tok = _ssd_in_specs(steps, True)
    in_specs += [pl.BlockSpec((SSD_GROUPS, 1, SSD_SUB, SSD_STATE, GROUP_COLS), lambda b, c: (0, b, steps - 1 - c, 0, 0)),
                 pl.BlockSpec((SSD_ROWS, SSD_WIDTH), lambda b, c: (tok(b, c), 0))]

    def body(xs, bm, cm, dtr, zz, dtb_r, alog_r, dcol_r, nw_r, hp_ref, dy_ref,
             dxs, dbm, dcm, ddt, dzz, ddtb, dalog, ddcol, dnw, dh_scr):
        b, c = pl.program_id(0), pl.program_id(1)

        @pl.when(c == 0)
        def _():
            dh_scr[...] = jnp.zeros_like(dh_scr)

        @pl.when((b == 0) & (c == 0))
        def _():
            for r in (ddtb, dalog, ddcol, dnw):
                r[...] = jnp.zeros_like(r)

        for g in range(SSD_GROUPS):
            wide = slice(g * GROUP_COLS, (g + 1) * GROUP_COLS)
            state = slice(g * SSD_STATE, (g + 1) * SSD_STATE)
            dh = dh_scr[g]
            for s in reversed(range(SSD_SUB)):
                rows = slice(s * SSD_CHUNK, (s + 1) * SSD_CHUNK)
                _, pull = jax.vjp(_ssd_chunk, xs[rows, wide], bm[rows, state], cm[rows, state],
                                  _piece(dtr, s, g, LANE), zz[rows, wide], hp_ref[g, 0, s],
                                  dtb_r[g], alog_r[g], dcol_r[g], nw_r[g])
                d = pull((dy_ref[rows, wide], dh))
                dxs[rows, wide], dbm[rows, state], dcm[rows, state], dzz[rows, wide] = d[0], d[1], d[2], d[4]
                ddt[rows, g * LANE:(g + 1) * LANE] = d[3]
                dh = d[5]
                ddtb[g] += d[6]
                dalog[g] += d[7]
                ddcol[g] += d[8]
                dnw[g] += d[9]
            dh_scr[g] = dh

    def tile(w):
        return pl.BlockSpec((SSD_ROWS, w), lambda b, c: (tok(b, c), 0))

    whole = lambda *shape: pl.BlockSpec(shape, lambda b, c: (0,) * len(shape))
    return pl.pallas_call(
        body, name="ssd_bwd", grid=(bsz, steps), in_specs=in_specs,
        out_specs=[tile(SSD_WIDTH), tile(2 * SSD_STATE), tile(2 * SSD_STATE), tile(2 * LANE), tile(SSD_WIDTH),
                   whole(SSD_GROUPS, 1, LANE), whole(SSD_GROUPS, 1, LANE), whole(SSD_GROUPS, LANE, 1),
                   whole(SSD_GROUPS, 1, GROUP_COLS)],
        out_shape=[jax.ShapeDtypeStruct((t, SSD_WIDTH), F32), jax.ShapeDtypeStruct((t, 2 * SSD_STATE), F32),
                   jax.ShapeDtypeStruct((t, 2 * SSD_STATE), F32), jax.ShapeDtypeStruct((t, 2 * LANE), F32),
                   jax.ShapeDtypeStruct((t, SSD_WIDTH), F32),
                   jax.ShapeDtypeStruct((SSD_GROUPS, 1, LANE), F32), jax.ShapeDtypeStruct((SSD_GROUPS, 1, LANE), F32),
                   jax.ShapeDtypeStruct((SSD_GROUPS, LANE, 1), F32),
                   jax.ShapeDtypeStruct((SSD_GROUPS, 1, GROUP_COLS), F32)],
        scratch_shapes=[pltpu.VMEM((SSD_GROUPS, SSD_STATE, GROUP_COLS), F32)],
        compiler_params=_cp("arbitrary", "arbitrary"),
    )(xc, xc, xc, proj, proj, dtb, alog, dcol, nw, hprev, dy)


def _disc_a(a_re, a_im, log_dt):
    dt = jnp.exp(log_dt)
    mag = jnp.exp(dt * a_re)
    ab_re, ab_im = mag * jnp.cos(dt * a_im), mag * jnp.sin(dt * a_im)
    den = a_re * a_re + a_im * a_im
    nr, ni = ab_re - 1.0, ab_im
    f_re, f_im = (nr * a_re + ni * a_im) / den, (ni * a_re - nr * a_im) / den
    return ab_re, ab_im, f_re, f_im


def _disc_b(f_re, f_im, b_re, b_im):
    return f_re * b_re - f_im * b_im, f_re * b_im + f_im * b_re


def _whole(f, name, args, outs):
    def body(*refs):
        res = f(*[r[...] for r in refs[:len(args)]])
        for o, v in zip(refs[len(args):], res):
            o[...] = v

    return pl.pallas_call(body, name=name, out_shape=[jax.ShapeDtypeStruct(s, F32) for s in outs])(*args)


def _whole_vjp(f, name, args, cts):
    def body(*refs):
        vals = [r[...] for r in refs[:len(args)]]
        _, pull = jax.vjp(f, *vals)
        res = pull(tuple(r[...] for r in refs[len(args):len(args) + len(cts)]))
        for o, v in zip(refs[len(args) + len(cts):], res):
            o[...] = v

    return pl.pallas_call(body, name=name, out_shape=[jax.ShapeDtypeStruct(a.shape, F32) for a in args])(*args, *cts)


S5_SUB = 8
S5_STEPS = 3


def s5_tables(lam_re, lam_im):
    rows = S5_STEPS * S5_SUB

    def body(lr_ref, li_ref, sf_re, sf_im, sb_re, sb_im, cf_re, cf_im, cb_re, cb_im):
        lr, li = lr_ref[...], li_ref[...]

        def power(k):
            m = jnp.exp(k * lr)
            return m * jnp.cos(k * li), m * jnp.sin(k * li)

        srow = lax.broadcasted_iota(jnp.int32, (rows, 1), 0)
        k = jnp.left_shift(1, srow // S5_SUB)
        tt = srow % S5_SUB
        pr, pi = power(k.astype(F32))
        fwd, bwd = tt >= k, tt < S5_SUB - k
        sf_re[...], sf_im[...] = jnp.where(fwd, pr, 0.0), jnp.where(fwd, pi, 0.0)
        sb_re[...], sb_im[...] = jnp.where(bwd, pr, 0.0), jnp.where(bwd, pi, 0.0)
        trow = lax.broadcasted_iota(jnp.int32, (S5_SUB, 1), 0)
        cf_re[...], cf_im[...] = power((trow + 1).astype(F32))
        cb_re[...], cb_im[...] = power((S5_SUB - trow).astype(F32))

    shp = [jax.ShapeDtypeStruct((rows, S5_COLS), F32)] * 4 + [jax.ShapeDtypeStruct((S5_SUB, S5_COLS), F32)] * 4
    return pl.pallas_call(body, name="s5_tables", out_shape=shp)(lam_re, lam_im)


def _s5_coefs(steps_re, steps_im, carry_re, carry_im, reverse):
    sign = -1.0 if reverse else 1.0
    steps = [(steps_re[s * S5_SUB:(s + 1) * S5_SUB, :], sign * steps_im[s * S5_SUB:(s + 1) * S5_SUB, :])
             for s in range(S5_STEPS)]
    return steps, (carry_re[...], sign * carry_im[...])


def _s5_block_scan(ar, ai, coefs, cr, ci, reverse):
    steps, (qr, qi) = coefs
    for s, (pr, pi) in enumerate(steps):
        shift = S5_SUB - (1 << s) if reverse else (1 << s)
        sr, si = pltpu.roll(ar, shift, 0), pltpu.roll(ai, shift, 0)
        ar, ai = ar + pr * sr - pi * si, ai + pr * si + pi * sr
    br, bi = jnp.broadcast_to(cr, ar.shape), jnp.broadcast_to(ci, ai.shape)
    return ar + qr * br - qi * bi, ai + qr * bi + qi * br


def _s5_specs(n5, rev):
    def tok(q, b, c):
        return b * n5 + (n5 - 1 - c if rev else c)

    qcols = S5_COLS // S5_Q
    specs = [
        pl.BlockSpec((S5_CHUNK, LANE), lambda q, b, c: (tok(q, b, c), P_U // LANE + q)),
        pl.BlockSpec((1, LANE, qcols), lambda q, b, c: (q, 0, 0)),
        pl.BlockSpec((1, LANE, qcols), lambda q, b, c: (q, 0, 0)),
        pl.BlockSpec((1, qcols, LANE), lambda q, b, c: (q, 0, 0)),
        pl.BlockSpec((1, qcols, LANE), lambda q, b, c: (q, 0, 0)),
        pl.BlockSpec((S5_STEPS * S5_SUB, qcols), lambda q, b, c: (0, q)),
        pl.BlockSpec((S5_STEPS * S5_SUB, qcols), lambda q, b, c: (0, q)),
        pl.BlockSpec((S5_SUB, qcols), lambda q, b, c: (0, q)),
        pl.BlockSpec((S5_SUB, qcols), lambda q, b, c: (0, q)),
        pl.BlockSpec((1, 1, LANE), lambda q, b, c: (q, 0, 0)),
    ]
    return specs, tok, qcols


def s5_fwd(proj, wb_re, wb_im, wc_re, wc_im, sf_re, sf_im, cf_re, cf_im, dvec, bsz, seq, ride=None):
    t = proj.shape[0]
    n5 = seq // S5_CHUNK
    in_specs, tok, qcols = _s5_specs(n5, False)

    def body(u_ref, wbr, wbi, wcr, wci, sfr, sfi, cfr, cfi, d_ref, y_ref, xr_ref, xi_ref, cr_scr, ci_scr):
        @pl.when(pl.program_id(2) == 0)
        def _():
            cr_scr[...] = jnp.zeros_like(cr_scr)
            ci_scr[...] = jnp.zeros_like(ci_scr)

        u = u_ref[...]
        bur, bui = _dg(u, wbr[0], 1, 0), _dg(u, wbi[0], 1, 0)
        coefs = _s5_coefs(sfr, sfi, cfr, cfi, False)
        cr, ci = cr_scr[...], ci_scr[...]
        for r in range(S5_CHUNK // S5_SUB):
            rows = slice(r * S5_SUB, (r + 1) * S5_SUB)
            xr, xi = _s5_block_scan(bur[rows], bui[rows], coefs, cr, ci, False)
            xr_ref[rows, :], xi_ref[rows, :] = xr, xi
            cr, ci = xr[S5_SUB - 1:, :], xi[S5_SUB - 1:, :]
        cr_scr[...], ci_scr[...] = cr, ci
        y_ref[...] = _dg(xr_ref[...], wcr[0], 1, 0) - _dg(xi_ref[...], wci[0], 1, 0) + u * d_ref[0]

    def tile(w):
        return pl.BlockSpec((S5_CHUNK, w), lambda q, b, c: (tok(q, b, c), q))

    return hosted_call(
        body, name="s5_fwd", grid=(S5_Q, bsz, n5), in_specs=in_specs,
        out_specs=[tile(LANE), tile(qcols), tile(qcols)],
        out_shape=[jax.ShapeDtypeStruct((t, S5_WIDTH), F32), jax.ShapeDtypeStruct((t, S5_COLS), F32),
                   jax.ShapeDtypeStruct((t, S5_COLS), F32)],
        scratch=[pltpu.VMEM((1, qcols), F32)] * 2,
        args=(proj, wb_re, wb_im, wc_re, wc_im, sf_re, sf_im, cf_re, cf_im, dvec), ride=ride)


def s5_bwd(proj, wb_re, wb_im, wc_re, wc_im, sb_re, sb_im, cb_re, cb_im, dvec, xr_all, xi_all, dy, bsz, seq,
           ride=None):
    t = proj.shape[0]
    n5 = seq // S5_CHUNK
    in_specs, tok, qcols = _s5_specs(n5, True)
    blocks = S5_CHUNK // HALO

    def prev_rows(q, b, c):
        return (jnp.maximum(tok(q, b, c) * blocks - 1, 0), q)

    in_specs += [pl.BlockSpec((S5_CHUNK, qcols), lambda q, b, c: (tok(q, b, c), q)),
                 pl.BlockSpec((S5_CHUNK, qcols), lambda q, b, c: (tok(q, b, c), q)),
                 pl.BlockSpec((HALO, qcols), prev_rows), pl.BlockSpec((HALO, qcols), prev_rows),
                 pl.BlockSpec((S5_CHUNK, LANE), lambda q, b, c: (tok(q, b, c), q))]

    def body(u_ref, wbr, wbi, wcr, wci, sbr, sbi, cbr, cbi, d_ref, xr_ref, xi_ref, pr_ref, pi_ref, dy_ref,
             du_ref, dwbr, dwbi, dwcr, dwci, dar, dai, dd_ref, gr_scr, gi_scr, gr_all, gi_all):
        b, c = pl.program_id(1), pl.program_id(2)

        @pl.when(c == 0)
        def _():
            gr_scr[...] = jnp.zeros_like(gr_scr)
            gi_scr[...] = jnp.zeros_like(gi_scr)

        @pl.when((b == 0) & (c == 0))
        def _():
            for r in (dwbr, dwbi, dwcr, dwci, dar, dai, dd_ref):
                r[...] = jnp.zeros_like(r)

        u, dy_v = u_ref[...], dy_ref[...]
        g0r, g0i = _dg(dy_v, wcr[0], 1, 1), -_dg(dy_v, wci[0], 1, 1)
        coefs = _s5_coefs(sbr, sbi, cbr, cbi, True)
        cr, ci = gr_scr[...], gi_scr[...]
        for r in reversed(range(S5_CHUNK // S5_SUB)):
            rows = slice(r * S5_SUB, (r + 1) * S5_SUB)
            br, bi = _s5_block_scan(g0r[rows], g0i[rows], coefs, cr, ci, True)
            gr_all[rows, :], gi_all[rows, :] = br, bi
            cr, ci = br[:1, :], bi[:1, :]
        gr_scr[...], gi_scr[...] = cr, ci
        gr, gi = gr_all[...], gi_all[...]

        row = lax.broadcasted_iota(jnp.int32, (S5_CHUNK, 1), 0)
        xr, xi = xr_ref[...], xi_ref[...]
        is_first = (c == n5 - 1)
        hr = jnp.where(is_first, 0.0, pr_ref[...][HALO - 1:, :])
        hi = jnp.where(is_first, 0.0, pi_ref[...][HALO - 1:, :])
        xpr = jnp.where(row >= 1, pltpu.roll(xr, 1, 0), hr)
        xpi = jnp.where(row >= 1, pltpu.roll(xi, 1, 0), hi)
        dar[0] += jnp.sum(xpr * gr + xpi * gi, axis=0, keepdims=True)
        dai[0] += jnp.sum(xpr * gi - xpi * gr, axis=0, keepdims=True)
        du_ref[...] = _dg(gr, wbr[0], 1, 1) + _dg(gi, wbi[0], 1, 1) + dy_v * d_ref[0]
        dwbr[0] += _dg(u, gr, 0, 0)
        dwbi[0] += _dg(u, gi, 0, 0)
        dwcr[0] += _dg(xr, dy_v, 0, 0)
        dwci[0] -= _dg(xi, dy_v, 0, 0)
        dd_ref[0] += jnp.sum(dy_v * u, axis=0, keepdims=True)

    def acc(shape):
        return pl.BlockSpec((1,) + shape, lambda q, b, c: (q, 0, 0))

    return hosted_call(
        body, name="s5_bwd", grid=(S5_Q, bsz, n5), in_specs=in_specs,
        out_specs=[pl.BlockSpec((S5_CHUNK, LANE), lambda q, b, c: (tok(q, b, c), q)),
                   acc((LANE, qcols)), acc((LANE, qcols)), acc((qcols, LANE)), acc((qcols, LANE)),
                   acc((1, qcols)), acc((1, qcols)), acc((1, LANE))],
        out_shape=[jax.ShapeDtypeStruct((t, S5_WIDTH), F32),
                   jax.ShapeDtypeStruct((S5_Q, LANE, qcols), F32), jax.ShapeDtypeStruct((S5_Q, LANE, qcols), F32),
                   jax.ShapeDtypeStruct((S5_Q, qcols, LANE), F32), jax.ShapeDtypeStruct((S5_Q, qcols, LANE), F32),
                   jax.ShapeDtypeStruct((S5_Q, 1, qcols), F32), jax.ShapeDtypeStruct((S5_Q, 1, qcols), F32),
                   jax.ShapeDtypeStruct((S5_Q, 1, LANE), F32)],
        scratch=[pltpu.VMEM((1, qcols), F32)] * 2 + [pltpu.VMEM((S5_CHUNK, qcols), F32)] * 2,
        args=(proj, wb_re, wb_im, wc_re, wc_im, sb_re, sb_im, cb_re, cb_im, dvec, xr_all, xi_all, xr_all, xi_all, dy),
        ride=ride)


def _blockdiag_b(bb):
    b4 = bb.reshape(S5_Q, 8, S5_STATE, S5_GROUP_CH)
    eye = jnp.eye(8, dtype=bb.dtype)
    w = jnp.einsum("qgph,gk->qghkp", b4, eye)
    return w.reshape(S5_Q, LANE, S5_COLS // S5_Q)


def _unblock_b(dw):
    d = dw.reshape(S5_Q, 8, S5_GROUP_CH, 8, S5_STATE)
    d = jnp.einsum("qghgp->qgph", d)
    return d.reshape(S5_COLS, S5_GROUP_CH)


def _blockdiag_c(cc):
    c4 = cc.reshape(S5_Q, 8, S5_GROUP_CH, S5_STATE)
    eye = jnp.eye(8, dtype=cc.dtype)
    w = jnp.einsum("qghp,gk->qgpkh", c4, eye)
    return w.reshape(S5_Q, S5_COLS // S5_Q, LANE)


def _unblock_c(dw):
    d = dw.reshape(S5_Q, 8, S5_STATE, 8, S5_GROUP_CH)
    d = jnp.einsum("qgpgh->qghp", d)
    return d.reshape(S5_GROUPS, S5_GROUP_CH, S5_STATE)


def ada_fwd(c_all, w_loc, b_loc):
    def body(c_ref, w_ref, b_ref, o_ref):
        o_ref[...] = _dg(_silu(c_ref[...]), w_ref[...], 1, 0) + b_ref[...]

    return pl.pallas_call(body, name="ada_fwd",
                          out_shape=jax.ShapeDtypeStruct((c_all.shape[0], w_loc.shape[1]), F32),
                          compiler_params=_cp())(c_all, w_loc, b_loc)


def ada_bwd(c_all, dmod_all, dmod_cols):
    def body(c_ref, da_ref, dc_ref, gb_ref, gw_ref):
        gb_ref[...] = jnp.sum(da_ref[...], axis=0, keepdims=True)
        gw_ref[...] = _dg(_silu(c_ref[...]), dc_ref[...], 0, 0)

    return pl.pallas_call(body, name="ada_bwd",
                          out_shape=[jax.ShapeDtypeStruct((1, dmod_all.shape[1]), F32),
                                     jax.ShapeDtypeStruct((c_all.shape[1], dmod_cols.shape[1]), F32)],
                          compiler_params=_cp())(c_all, dmod_all, dmod_cols)


_FLIPS = [(0, 0, 1), (1, 0, 0), (0, 1, 0), (1, 1, 0), (1, 0, 1), (0, 1, 1), (1, 1, 1)]


def _exchange_ops(srcs, outs, sems, gather):
    n = len(srcs)
    send_sems, recv_sems, loc_sems = sems
    x, y, c = lax.axis_index("x"), lax.axis_index("y"), lax.axis_index("c")
    me = 4 * x + 2 * y + c
    peers = []
    for fx, fy, fc in _FLIPS:
        px, py, pc = (1 - x if fx else x), (1 - y if fy else y), (1 - c if fc else c)
        peers.append(((px, py, pc), 4 * px + 2 * py + pc))

    def copy(k, j, slot_src, slot_dst):
        src = srcs[k] if gather[k] else srcs[k].at[slot_src]
        return pltpu.make_async_remote_copy(src_ref=src, dst_ref=outs[k].at[slot_dst],
                                            send_sem=send_sems.at[k, j], recv_sem=recv_sems.at[k, j],
                                            device_id=peers[j][0], device_id_type=MESH)

    def local(k):
        own = srcs[k] if gather[k] else srcs[k].at[me]
        return pltpu.make_async_copy(own, outs[k].at[me], loc_sems.at[k])

    def start():
        for k in range(n):
            for j in range(N_DEV - 1):
                copy(k, j, peers[j][1], me).start()
            local(k).start()

    def wait():
        for k in range(n):
            for j in range(N_DEV - 1):
                copy(k, j, me, peers[j][1]).wait_recv()
        for k in range(n):
            for j in range(N_DEV - 1):
                copy(k, j, peers[j][1], me).wait_send()
            local(k).wait()

    return start, wait


def _gather_two_level(srcs, outs, sems):
    n = len(srcs)
    send_sems, recv_sems, loc_sems = sems
    x, y, c = lax.axis_index("x"), lax.axis_index("y"), lax.axis_index("c")
    slot = lambda px, py, pc: 4 * px + 2 * py + pc
    me, sibling = (x, y, c), (x, y, 1 - c)
    chips = [(1 - x, y), (x, 1 - y), (1 - x, 1 - y)]

    def copy(k, j, block, to, own=False):
        return pltpu.make_async_remote_copy(src_ref=srcs[k] if own else outs[k].at[slot(*block)],
                                            dst_ref=outs[k].at[slot(*block)],
                                            send_sem=send_sems.at[k, j], recv_sem=recv_sems.at[k, j],
                                            device_id=to, device_id_type=MESH)

    locs = [pltpu.make_async_copy(srcs[k], outs[k].at[slot(*me)], loc_sems.at[k]) for k in range(n)]
    for k in range(n):
        locs[k].start()
        copy(k, 0, me, sibling, own=True).start()
        for j, chip in enumerate(chips):
            copy(k, 1 + j, me, (*chip, c), own=True).start()
    for j, chip in enumerate(chips):
        for k in range(n):
            copy(k, 1 + j, (*chip, c), me).wait_recv()
            copy(k, 4 + j, (*chip, c), sibling).start()
    for k in range(n):
        copy(k, 0, sibling, me).wait_recv()
        for j, chip in enumerate(chips):
            copy(k, 4 + j, (*chip, 1 - c), me).wait_recv()
    for k in range(n):
        copy(k, 0, me, sibling, own=True).wait_send()
        for j, chip in enumerate(chips):
            copy(k, 1 + j, me, (*chip, c), own=True).wait_send()
            copy(k, 4 + j, (*chip, c), sibling).wait_send()
        locs[k].wait()


def gather_two_level(name, arrs):
    n = len(arrs)
    specs, shapes, sems = _exchange_parts(arrs, [True] * n)

    def body(*refs):
        _gather_two_level(refs[:n], refs[n:2 * n], refs[2 * n:])

    return pl.pallas_call(
        body, name=name, in_specs=specs, out_specs=specs, out_shape=shapes, scratch_shapes=sems,
        compiler_params=pltpu.CompilerParams(has_side_effects=True),
    )(*arrs)


def _exchange_parts(arrs, gather):
    n = len(arrs)
    any_spec = pl.BlockSpec(memory_space=pl.ANY)
    shapes = [jax.ShapeDtypeStruct(((N_DEV,) + a.shape) if g else a.shape, a.dtype) for a, g in zip(arrs, gather)]
    sems = [pltpu.SemaphoreType.DMA((n, N_DEV - 1)), pltpu.SemaphoreType.DMA((n, N_DEV - 1)),
            pltpu.SemaphoreType.DMA((n,))]
    return [any_spec] * n, shapes, sems


def exchange(name, arrs, gather):
    n = len(arrs)
    specs, shapes, sems = _exchange_parts(arrs, gather)

    def body(*refs):
        start, wait = _exchange_ops(refs[:n], refs[n:2 * n], refs[2 * n:], gather)
        start()
        wait()

    return pl.pallas_call(
        body, name=name, in_specs=specs, out_specs=specs, out_shape=shapes, scratch_shapes=sems,
        compiler_params=pltpu.CompilerParams(has_side_effects=True),
    )(*arrs)


def hosted_call(body, *, name, grid, in_specs, out_specs, out_shape, args, scratch=(), ride=None):
    sem = ("arbitrary",) * len(grid)
    if ride is None:
        res = pl.pallas_call(body, name=name, grid=grid, in_specs=in_specs, out_specs=out_specs, out_shape=out_shape,
                             scratch_shapes=list(scratch), compiler_params=_cp(*sem))(*args)
        return list(res), []
    arrs, gather = ride
    n, n_in, n_out, n_scr = len(arrs), len(in_specs), len(out_specs), len(scratch)
    specs, shapes, sems = _exchange_parts(arrs, gather)

    def both(*refs):
        ins, srcs = refs[:n_in], refs[n_in:n_in + n]
        outs, landed = refs[n_in + n:n_in + n + n_out], refs[n_in + n + n_out:n_in + 2 * n + n_out]
        scr, ex_sems = refs[n_in + 2 * n + n_out:n_in + 2 * n + n_out + n_scr], refs[n_in + 2 * n + n_out + n_scr:]
        start, wait = _exchange_ops(srcs, landed, ex_sems, gather)
        first = functools.reduce(lambda a, b: a & b, [pl.program_id(d) == 0 for d in range(len(grid))])
        last = functools.reduce(lambda a, b: a & b, [pl.program_id(d) == grid[d] - 1 for d in range(len(grid))])
        pl.when(first)(start)
        body(*ins, *outs, *scr)
        pl.when(last)(wait)

    res = pl.pallas_call(
        both, name=name, grid=grid, in_specs=list(in_specs) + specs, out_specs=list(out_specs) + specs,
        out_shape=list(out_shape) + shapes, scratch_shapes=list(scratch) + sems, compiler_params=_cp(*sem),
    )(*args, *arrs)
    return list(res[:n_out]), list(res[n_out:])


def sum_slots(name, slots, tr):
    _, r, c = slots.shape

    def body(s_ref, o_ref):
        acc = s_ref[0].astype(F32)
        for j in range(1, N_DEV):
            acc = acc + s_ref[j].astype(F32)
        o_ref[...] = acc

    return pl.pallas_call(
        body, name=name, grid=(r // tr,), in_specs=[pl.BlockSpec((N_DEV, tr, c), lambda i: (0, i, 0))],
        out_specs=pl.BlockSpec((tr, c), lambda i: (i, 0)), out_shape=jax.ShapeDtypeStruct((r, c), F32),
        compiler_params=_cp("parallel"),
    )(slots)


def adamw(name, g, w, m, v, tr):
    slots = g.ndim == 3
    r, c = w.shape
    c1, c2 = 1.0 - ADAM_B1 ** ADAM_STEP, 1.0 - ADAM_B2 ** ADAM_STEP

    def body(g_ref, w_ref, m_ref, v_ref, go, do, mo, vo):
        if slots:
            gg = g_ref[0].astype(F32)
            for j in range(1, N_DEV):
                gg = gg + g_ref[j].astype(F32)
        else:
            gg = g_ref[...]
        mn = ADAM_B1 * m_ref[...] + (1.0 - ADAM_B1) * gg
        vn = ADAM_B2 * v_ref[...] + (1.0 - ADAM_B2) * (gg * gg)
        go[...], mo[...], vo[...] = gg, mn, vn
        do[...] = -ADAM_LR * ((mn / c1) / (jnp.sqrt(vn / c2) + ADAM_EPS) + ADAM_WD * w_ref[...])

    blk = pl.BlockSpec((tr, c), lambda i: (i, 0))
    gspec = pl.BlockSpec((N_DEV, tr, c), lambda i: (0, i, 0)) if slots else blk
    return pl.pallas_call(
        body, name=name, grid=(r // tr,), in_specs=[gspec, blk, blk, blk], out_specs=[blk] * 4,
        out_shape=[jax.ShapeDtypeStruct((r, c), F32)] * 4, compiler_params=_cp("parallel"),
    )(g, w, m, v)


def _lane_rows(n):
    return -(-n // (8 * LANE)) * 8


def _pack(arrs):
    pieces = []
    for a in arrs:
        n = math.prod(a.shape)
        flat = a.reshape(-1).astype(F32)
        pieces.append(jnp.pad(flat, (0, _lane_rows(n) * LANE - n)).reshape(_lane_rows(n), LANE))
    return jnp.concatenate(pieces, axis=0)


def _unpack(buf, shapes):
    out, off = [], 0
    for s in shapes:
        n = math.prod(s)
        out.append(buf[off:off + _lane_rows(n)].reshape(-1)[:n].reshape(s))
        off += _lane_rows(n)
    return out


FF_CHUNK = D_FF
DW_TOKENS = 2048
FFN_TM = 256


def _resident(shape):
    return pl.BlockSpec(shape, lambda i: (0,) * len(shape), pipeline_mode=pl.Buffered(1))


def _ffn_fwd(tag, x, sc, sh, g, w1, w3, w2, lg, lb, seq, tm, ride=None, target=None):
    t = x.shape[0]
    tm = min(FFN_TM, tm)
    tps = seq // tm
    ln = _res_ln(0.5)
    head = target is not None

    def body(x_ref, sc_ref, sh_ref, g_ref, lg_ref, lb_ref, w1_ref, w3_ref, w2_ref, *rest):
        if head:
            t_ref, y_ref, h_ref, a_ref, b_ref, f_ref, l_ref = rest
        else:
            y_ref, h_ref, a_ref, b_ref, f_ref = rest
        xv = x_ref[...]
        h = (xv * (1.0 + sc_ref[0]) + sh_ref[0]).astype(BF16)
        h_ref[...] = h
        acc = jnp.zeros((tm, D_MODEL), F32)
        for j in range(D_FF // FF_CHUNK):
            sl = slice(j * FF_CHUNK, (j + 1) * FF_CHUNK)
            a = _dg(h, w1_ref[sl, :], 1, 1)
            b = _dg(h, w3_ref[sl, :], 1, 1)
            a_ref[:, sl] = a
            b_ref[:, sl] = b
            acc = acc + _dg(_silu(a) * b, w2_ref[sl, :], 1, 0)
        f_ref[...] = acc
        y = ln(xv, acc, g_ref[0], lg_ref[...], lb_ref[...])[0]
        if head:
            @pl.when(pl.program_id(0) == 0)
            def _():
                l_ref[...] = jnp.zeros_like(l_ref)

            e = y - t_ref[...]
            y_ref[...] = e * (1.0 / D_MODEL)
            l_ref[...] += 0.5 * jnp.sum(jnp.mean(e * e, axis=-1, keepdims=True), axis=0, keepdims=True)
        else:
            y_ref[...] = y

    row = lambda c: pl.BlockSpec((tm, c), lambda i: (i, 0))
    per_seq = pl.BlockSpec((1, 1, D_MODEL), lambda i: (i // tps, 0, 0))
    vec = pl.BlockSpec((1, D_MODEL), lambda i: (0, 0))
    res, landed = hosted_call(
        body, name=tag + "_fwd", grid=(t // tm,),
        in_specs=[row(D_MODEL), per_seq, per_seq, per_seq, vec, vec,
                  _resident((D_FF, D_MODEL)), _resident((D_FF, D_MODEL)), _resident((D_FF, D_MODEL))]
        + ([row(D_MODEL)] if head else []),
        out_specs=[row(D_MODEL), row(D_MODEL), row(D_FF), row(D_FF), row(D_MODEL)]
        + ([pl.BlockSpec((1, 1), lambda i: (0, 0))] if head else []),
        out_shape=[jax.ShapeDtypeStruct((t, D_MODEL), F32), jax.ShapeDtypeStruct((t, D_MODEL), BF16),
                   jax.ShapeDtypeStruct((t, D_FF), F32), jax.ShapeDtypeStruct((t, D_FF), F32),
                   jax.ShapeDtypeStruct((t, D_MODEL), F32)] + ([jax.ShapeDtypeStruct((1, 1), F32)] if head else []),
        args=(x, sc, sh, g, lg, lb, w1, w3, w2) + ((target,) if head else ()), ride=ride)
    first = (res[0], res[5][0, 0]) if head else res[0]
    return first, tuple(res[1:5]), landed


def _ffn_bwd(tag, dy, x, sc, sh, g, w1, w3, w2, lg, lb, res, seq, tm, ride=None, chain=None):
    h, a, b, f = res
    t = x.shape[0]
    tmk = min(FFN_TM, tm)
    tps = seq // tmk
    ln = _res_ln(0.5)

    def body(dy_ref, x_ref, f_ref, a_ref, b_ref, sc_ref, sh_ref, g_ref, lg_ref, lb_ref, w1_ref, w3_ref, w2_ref,
             dx_ref, da_ref, db_ref, s_ref, df_ref, dsc_ref, dsh_ref, dg_ref, dlg_ref, dlb_ref):
        i = pl.program_id(0)

        @pl.when(i % tps == 0)
        def _():
            for r in (dsc_ref, dsh_ref, dg_ref):
                r[...] = jnp.zeros_like(r)

        @pl.when(i == 0)
        def _():
            dlg_ref[...] = jnp.zeros_like(dlg_ref)
            dlb_ref[...] = jnp.zeros_like(dlb_ref)

        xv = x_ref[...]
        _, pull = jax.vjp(ln, xv, f_ref[...], g_ref[0], lg_ref[...], lb_ref[...])
        dx_res, df, dg, dlg, dlb = pull((dy_ref[...],))
        dfb = df.astype(BF16)
        df_ref[...] = dfb
        dh = jnp.zeros((tmk, D_MODEL), F32)
        for j in range(D_FF // FF_CHUNK):
            sl = slice(j * FF_CHUNK, (j + 1) * FF_CHUNK)
            ds = _dg(dfb, w2_ref[sl, :], 1, 1)
            av, bv = a_ref[:, sl], b_ref[:, sl]
            sg = jax.nn.sigmoid(av)
            si = av * sg
            s_ref[:, sl] = (si * bv).astype(BF16)
            da = (ds * bv * (sg * (1.0 + av * (1.0 - sg)))).astype(BF16)
            db = (ds * si).astype(BF16)
            da_ref[:, sl] = da
            db_ref[:, sl] = db
            dh = dh + _dg(da, w1_ref[sl, :], 1, 0) + _dg(db, w3_ref[sl, :], 1, 0)
        dx_ref[...] = dx_res + dh * (1.0 + sc_ref[0])
        dsc_ref[0] += jnp.sum(dh * xv, axis=0, keepdims=True)
        dsh_ref[0] += jnp.sum(dh, axis=0, keepdims=True)
        dg_ref[0] += dg
        dlg_ref[...] += dlg
        dlb_ref[...] += dlb

    row = lambda c: pl.BlockSpec((tmk, c), lambda i: (i, 0))
    per_seq = pl.BlockSpec((1, 1, D_MODEL), lambda i: (i // tps, 0, 0))
    vec = pl.BlockSpec((1, D_MODEL), lambda i: (0, 0))
    seq_shape = jax.ShapeDtypeStruct(sc.shape, F32)
    vec_shape = jax.ShapeDtypeStruct((1, D_MODEL), F32)
    (dx, da, db, s, df, dsc, dsh, dg, dlg, dlb), landed = hosted_call(
        body, name=tag + "_bwd", grid=(t // tmk,),
        in_specs=[row(D_MODEL), row(D_MODEL), row(D_MODEL), row(D_FF), row(D_FF), per_seq, per_seq, per_seq, vec, vec,
                  _resident((D_FF, D_MODEL)), _resident((D_FF, D_MODEL)), _resident((D_FF, D_MODEL))],
        out_specs=[row(D_MODEL), row(D_FF), row(D_FF), row(D_FF), row(D_MODEL), per_seq, per_seq, per_seq, vec, vec],
        out_shape=[jax.ShapeDtypeStruct((t, D_MODEL), F32), jax.ShapeDtypeStruct((t, D_FF), BF16),
                   jax.ShapeDtypeStruct((t, D_FF), BF16), jax.ShapeDtypeStruct((t, D_FF), BF16),
                   jax.ShapeDtypeStruct((t, D_MODEL), BF16), seq_shape, seq_shape, seq_shape, vec_shape, vec_shape],
        args=(dy, x, f, a, b, sc, sh, g, lg, lb, w1, w3, w2), ride=ride)
    tt = min(DW_TOKENS, seq)
    shards = lambda dw: dw.reshape(N_DEV, D_FF // N_DEV, D_MODEL)
    if chain is None:
        dw2, landed = mm_tn(tag + "_dw2", s, df, D_FF // 2, D_MODEL, tt, BF16), []
    else:
        dw2, landed = mm_tn(tag + "_dw2", s, df, D_FF // 2, D_MODEL, tt, BF16, ride=chain((dsh, dsc, dg), dlg, dlb))
    dw1, (s_w2,) = mm_tn(tag + "_dw1", da, h, D_FF // 2, D_MODEL, tt, BF16, ride=([shards(dw2)], [False]))
    dw3, (s_w1,) = mm_tn(tag + "_dw3", db, h, D_FF // 2, D_MODEL, tt, BF16, ride=([shards(dw1)], [False]))
    return dx, (dsh, dsc, dg), (s_w1, shards(dw3), s_w2, dlg, dlb), landed


def kernel(x, c, w_ada, b_ada, ffn1_w1, ffn1_w3, ffn1_w2, ln1_g, ln1_b, w_in, conv_w, conv_b, dt_bias, a_log, d_ssd, ssd_norm_w, s5_a_re, s5_a_im, s5_log_dt, s5_b_re, s5_b_im, s5_c_re, s5_c_im, s5_d, w_glu, b_glu, w_out, ln2_g, ln2_b, ffn2_w1, ffn2_w3, ffn2_w2, ln3_g, ln3_b, loss_target, m_w_ada, m_b_ada, m_ffn1_w1, m_ffn1_w3, m_ffn1_w2, m_ln1_g, m_ln1_b, m_w_in, m_conv_w, m_conv_b, m_dt_bias, m_a_log, m_d_ssd, m_ssd_norm_w, m_s5_a_re, m_s5_a_im, m_s5_log_dt, m_s5_b_re, m_s5_b_im, m_s5_c_re, m_s5_c_im, m_s5_d, m_w_glu, m_b_glu, m_w_out, m_ln2_g, m_ln2_b, m_ffn2_w1, m_ffn2_w3, m_ffn2_w2, m_ln3_g, m_ln3_b, v_w_ada, v_b_ada, v_ffn1_w1, v_ffn1_w3, v_ffn1_w2, v_ln1_g, v_ln1_b, v_w_in, v_conv_w, v_conv_b, v_dt_bias, v_a_log, v_d_ssd, v_ssd_norm_w, v_s5_a_re, v_s5_a_im, v_s5_log_dt, v_s5_b_re, v_s5_b_im, v_s5_c_re, v_s5_c_im, v_s5_d, v_w_glu, v_b_glu, v_w_out, v_ln2_g, v_ln2_b, v_ffn2_w1, v_ffn2_w3, v_ffn2_w2, v_ln3_g, v_ln3_b):
    given = dict(locals())
    bsz, seq, _ = x.shape
    t = bsz * seq
    tm = min(1024, seq)
    me = 4 * lax.axis_index("x") + 2 * lax.axis_index("y") + lax.axis_index("c")
    x0 = x.reshape(t, D_MODEL)
    target = loss_target.reshape(t, D_MODEL)

    tr16 = lambda w: w[0].T.astype(BF16)
    whole = lambda g: g.reshape(N_DEV * g.shape[1], g.shape[2])
    g_f1w1, g_f1w3, g_f1w2, g_c = gather_two_level(
        "gather_ffn1", [tr16(ffn1_w1), tr16(ffn1_w3), ffn1_w2[0].astype(BF16), c])
    f1w1, f1w3, f1w2 = whole(g_f1w1), whole(g_f1w3), whole(g_f1w2)
    c_all = whole(g_c)

    n_loc = w_ada.shape[2]
    b_loc = lax.dynamic_slice(b_ada, (0, me * n_loc), (1, n_loc))
    mod_cols = ada_fwd(c_all, w_ada[0], b_loc)
    g_mod, = exchange("gather_mod", [mod_cols], [True])
    mine = lax.dynamic_slice(g_mod, (0, me * bsz, 0), (N_DEV, bsz, n_loc))
    mod = jnp.transpose(mine, (1, 0, 2)).reshape(bsz, N_MOD, 1, D_MODEL)
    sh1, sc1, g1, sh2, sc2, g2, sh3, sc3, g3 = [mod[:, k] for k in range(N_MOD)]

    x1, res1, (g_win, g_glu, g_out, g_conv, g_f2w1) = _ffn_fwd(
        "ffn1", x0, sc1, sh1, g1, f1w1, f1w3, f1w2, ln1_g, ln1_b, seq, tm,
        ride=([tr16(w_in), w_glu[0].astype(BF16), w_out[0].astype(BF16), conv_w[0], tr16(ffn2_w1)], [True] * 5))
    win = whole(g_win)
    wglu = whole(g_glu).astype(F32)
    wout = whole(g_out)
    wo_ssd, wo_s5 = wout[:SSD_WIDTH], wout[SSD_WIDTH:]
    convw = jnp.transpose(g_conv, (1, 0, 2)).reshape(CONV_K, CONV_CH)
    w_z, w_xbc = win[:SSD_WIDTH], win[SSD_WIDTH:SSD_WIDTH + CONV_CH]
    w_dt = win[SSD_WIDTH + CONV_CH:SSD_WIDTH + CONV_CH + SSD_HEADS]
    w_u = win[SSD_WIDTH + CONV_CH + SSD_HEADS:]
    dt_pad = [jnp.pad(w_dt[HEADS_PER_GROUP * g:HEADS_PER_GROUP * (g + 1)], ((0, LANE - HEADS_PER_GROUP), (0, 0)))
              for g in range(SSD_GROUPS)]
    w_dtp = jnp.concatenate(dt_pad, axis=0)
    w_proj = jnp.concatenate([w_xbc, w_z, w_u, w_dtp], axis=0)

    proj, h2 = modulate_proj("mix_proj", x1, sc2, sh2, w_proj, seq, tm, P_COLS)
    xc = conv_fwd(proj, convw, conv_b, seq, tm)
    dtb = jnp.pad(dt_bias.reshape(SSD_GROUPS, 1, HEADS_PER_GROUP), ((0, 0), (0, 0), (0, LANE - HEADS_PER_GROUP)))
    alog = jnp.pad(a_log.reshape(SSD_GROUPS, 1, HEADS_PER_GROUP), ((0, 0), (0, 0), (0, LANE - HEADS_PER_GROUP)))
    dcol = jnp.pad(d_ssd.reshape(SSD_GROUPS, HEADS_PER_GROUP, 1), ((0, 0), (0, LANE - HEADS_PER_GROUP), (0, 0)))
    nw = ssd_norm_w.reshape(SSD_GROUPS, 1, GROUP_COLS)
    (y_ssd, hprev), (g_f2w3,) = ssd_fwd(xc, proj, dtb, alog, dcol, nw, bsz, seq,
                                        ride=([tr16(ffn2_w3)], [True]))

    a_re2, a_im2, ldt2 = s5_a_re[0], s5_a_im[0], s5_log_dt.reshape(S5_GROUPS, 1)
    ab_re, ab_im, f_re, f_im = _whole(_disc_a, "s5_disc_a", [a_re2, a_im2, ldt2], [(S5_GROUPS, S5_STATE)] * 4)
    b_re2, b_im2 = s5_b_re.reshape(S5_COLS, S5_GROUP_CH), s5_b_im.reshape(S5_COLS, S5_GROUP_CH)
    fr_col, fi_col = f_re.reshape(S5_COLS, 1), f_im.reshape(S5_COLS, 1)
    bb_re, bb_im = _whole(_disc_b, "s5_disc_b", [fr_col, fi_col, b_re2, b_im2], [(S5_COLS, S5_GROUP_CH)] * 2)
    wb_re, wb_im = _blockdiag_b(bb_re).astype(BF16), _blockdiag_b(bb_im).astype(BF16)
    wc_re, wc_im = _blockdiag_c(s5_c_re[0]).astype(BF16), _blockdiag_c(s5_c_im[0]).astype(BF16)
    dt5 = jnp.exp(ldt2)
    lam_re, lam_im = (dt5 * a_re2).reshape(1, S5_COLS), (dt5 * a_im2).reshape(1, S5_COLS)
    sf_re, sf_im, sb_re, sb_im, cf_re, cf_im, cb_re, cb_im = s5_tables(lam_re, lam_im)
    d5 = s5_d.reshape(S5_Q, 1, LANE)
    (y5, xr_all, xi_all), (g_f2w2,) = s5_fwd(
        proj, wb_re, wb_im, wc_re, wc_im, sf_re, sf_im, cf_re, cf_im, d5, bsz, seq,
        ride=([ffn2_w2[0].astype(BF16)], [True]))
    f2w1, f2w3, f2w2 = whole(g_f2w1), whole(g_f2w3), whole(g_f2w2)
    o5, = rowwise_fwd("s5_glu", f_glu, [y5], [], [wglu, b_glu], [(S5_WIDTH, F32)], seq, tm)

    mix, x2 = proj_res_ln("mix_out_ln", [y_ssd, o5], [wo_ssd, wo_s5], x1, g2, ln2_g, ln2_b, 1.0, seq, tm)

    (dy, loss_loc), res3, _ = _ffn_fwd("ffn2", x2, sc3, sh3, g3, f2w1, f2w3, f2w2, ln3_g, ln3_b, seq, tm, target=target)

    dx2, dmod3, (s_f2w1, d_f2w3, s_f2w2, d_ln3g, d_ln3b), _ = _ffn_bwd(
        "ffn2", dy, x2, sc3, sh3, g3, f2w1, f2w3, f2w2, ln3_g, ln3_b, res3, seq, tm)

    (dx1_a, dmix), (dg2,), (d_ln2g, d_ln2b) = rowwise_bwd(
        "mix_ln_b", _res_ln(1.0), [x1, mix], [g2], [ln2_g, ln2_b], [dx2], seq, tm, [F32, BF16])
    tw = min(DW_TOKENS, seq)
    d_wo = jnp.concatenate([mm_tn("mix_dwo_ssd", y_ssd, dmix, SSD_WIDTH, D_MODEL, tw, BF16),
                            mm_tn("mix_dwo_s5", o5, dmix, S5_WIDTH, D_MODEL, tw, BF16)], axis=0)
    dy_mixed = mm_nt("mix_dy", [dmix], [wout], tm, D_MODEL)
    dy_ssd, do5 = dy_mixed, (dy_mixed, SSD_WIDTH, S5_WIDTH)

    (dy5,), _, (d_wglu, d_bglu) = rowwise_bwd("s5_glu_b", f_glu, [y5], [], [wglu, b_glu], [do5], seq, tm, [F32])
    (du, dwbr, dwbi, dwcr, dwci, dab_re, dab_im, dd5), (s_f2w3, s_out, s_glu) = s5_bwd(
        proj, wb_re, wb_im, wc_re, wc_im, sb_re, sb_im, cb_re, cb_im, d5, xr_all, xi_all, dy5, bsz, seq,
        ride=([d_f2w3, d_wo.reshape(N_DEV, D_MODEL // N_DEV, D_MODEL),
               d_wglu.reshape(N_DEV, S5_WIDTH // N_DEV, S5_WIDTH).astype(BF16)], [False] * 3))
    dbb_re, dbb_im = _unblock_b(dwbr), _unblock_b(dwbi)
    dfr_col, dfi_col, d_b_re, d_b_im = _whole_vjp(_disc_b, "s5_disc_b_b", [fr_col, fi_col, b_re2, b_im2],
                                                  [dbb_re, dbb_im])
    d_a_re, d_a_im, d_ldt = _whole_vjp(
        _disc_a, "s5_disc_a_b", [a_re2, a_im2, ldt2],
        [dab_re.reshape(S5_GROUPS, S5_STATE), dab_im.reshape(S5_GROUPS, S5_STATE),
         dfr_col.reshape(S5_GROUPS, S5_STATE), dfi_col.reshape(S5_GROUPS, S5_STATE)])
    d_c_re, d_c_im = _unblock_c(dwcr), _unblock_c(dwci)

    dxs, dbm, dcm, ddt, dz, ddtb, dalog, ddcol, dnw = ssd_bwd(xc, proj, dtb, alog, dcol, nw, hprev, dy_ssd, bsz, seq)
    dpre, d_convw, d_convb = conv_bwd_pre(proj, convw, conv_b, dxs, dbm, dcm, seq, tm)
    dxbc = conv_bwd_x(dpre, convw, seq, tm)

    dw_xbc = mm_tn("mix_dw_xbc", dxbc, h2, CONV_CH, D_MODEL, tw, BF16)
    dw_z = mm_tn("mix_dw_z", dz, h2, SSD_WIDTH, D_MODEL, tw, BF16)
    dw_u = mm_tn("mix_dw_u", du, h2, S5_WIDTH, D_MODEL, tw, BF16)
    dw_dt = mm_tn("mix_dw_dt", ddt, h2, 2 * LANE, D_MODEL, tw, BF16)
    dw_dt8 = jnp.concatenate([dw_dt[LANE * g:LANE * g + HEADS_PER_GROUP] for g in range(SSD_GROUPS)], axis=0)
    d_win = jnp.concatenate([dw_z, dw_xbc, dw_dt8, dw_u], axis=0)
    dh2, (s_win,) = mm_nn("mix_dh", [dxbc, dz, du, ddt], [w_xbc, w_z, w_u, w_dtp], tm, D_MODEL,
                          ride=([d_win.reshape(N_DEV, IN_COLS // N_DEV, D_MODEL)], [False]))
    (dx1,), (dsc2, dsh2), _ = rowwise_bwd("mix_mod_b", f_modulate, [x1], [sc2, sh2], [], [dh2], seq, tm, [F32],
                                          add_rows={0: dx1_a})

    packing = {}

    def small_and_dmod(dmod1, d_ln1g, d_ln1b):
        dmod = jnp.concatenate(list(dmod1) + [dsh2, dsc2, dg2] + list(dmod3), axis=1).reshape(bsz, N_MOD * D_MODEL)
        small = _small_grads(d_ln1g, d_ln1b)
        packing["names"] = list(small)
        packing["shapes"] = [small[k].shape for k in small]
        return [_pack(list(small.values())), dmod], [True, True]

    def _small_grads(d_ln1g, d_ln1b):
        return {
            "ln1_g": d_ln1g, "ln1_b": d_ln1b, "conv_w": d_convw, "conv_b": d_convb,
            "dt_bias": ddtb[:, 0, :HEADS_PER_GROUP].reshape(1, SSD_HEADS),
            "a_log": dalog[:, 0, :HEADS_PER_GROUP].reshape(1, SSD_HEADS),
            "d_ssd": ddcol[:, :HEADS_PER_GROUP, 0].reshape(1, SSD_HEADS),
            "ssd_norm_w": dnw.reshape(1, SSD_WIDTH),
            "s5_a_re": d_a_re[None], "s5_a_im": d_a_im[None], "s5_log_dt": d_ldt.reshape(1, S5_GROUPS),
            "s5_b_re": d_b_re.reshape(s5_b_re.shape), "s5_b_im": d_b_im.reshape(s5_b_im.shape),
            "s5_c_re": d_c_re[None], "s5_c_im": d_c_im[None], "s5_d": dd5.reshape(1, S5_WIDTH),
            "b_glu": d_bglu, "ln2_g": d_ln2g, "ln2_b": d_ln2b, "ln3_g": d_ln3g, "ln3_b": d_ln3b,
            "loss": loss_loc.reshape(1, 1),
        }

    dx0, _, (s_f1w1, d_f1w3, s_f1w2, _, _), (s_small, s_dmod) = _ffn_bwd(
        "ffn1", dx1, x0, sc1, sh1, g1, f1w1, f1w3, f1w2, ln1_g, ln1_b, res1, seq, tm, chain=small_and_dmod)
    names, shapes = packing["names"], packing["shapes"]
    s_f1w3, = exchange("sum_grads", [d_f1w3], [False])

    out = {"grad_x": dx0.reshape(x.shape)}

    def put(name, res, shape):
        for key, val in zip(("grad_", "delta_", "new_m_", "new_v_"), res):
            out[key + name] = val.reshape(shape)

    for name, slots, tr in (("ffn1_w1", s_f1w1, 176), ("ffn1_w3", s_f1w3, 176), ("ffn2_w1", s_f2w1, 176),
                            ("ffn2_w3", s_f2w3, 176), ("w_in", s_win, IN_COLS // N_DEV)):
        w = given[name]
        grad = sum_slots("sum_" + name, slots, tr).T
        put(name, adamw("adam_" + name, grad, w[0], given["m_" + name][0], given["v_" + name][0], 256), w.shape)
    for name, slots in (("ffn1_w2", s_f1w2), ("ffn2_w2", s_f2w2)):
        w = given[name]
        put(name, adamw("adam_" + name, slots, w[0], given["m_" + name][0], given["v_" + name][0], 176), w.shape)
    put("w_glu", adamw("adam_w_glu", s_glu, w_glu[0], m_w_glu[0], v_w_glu[0], 64), w_glu.shape)
    put("w_out", adamw("adam_w_out", s_out, w_out[0], m_w_out[0], v_w_out[0], 128), w_out.shape)

    dmod_all = s_dmod.reshape(N_DEV * bsz, N_MOD * D_MODEL)
    g_bada, g_wada = ada_bwd(c_all, dmod_all, lax.dynamic_slice(dmod_all, (0, me * n_loc), (N_DEV * bsz, n_loc)))
    put("w_ada", adamw("adam_w_ada", g_wada, w_ada[0], m_w_ada[0], v_w_ada[0], 256), w_ada.shape)
    put("b_ada", adamw("adam_b_ada", g_bada, b_ada, m_b_ada, v_b_ada, 1), b_ada.shape)

    not_params = {"conv_w": jnp.zeros((CONV_K, CONV_CH), F32), "loss": jnp.zeros((1, 1), F32)}
    pw, pm, pv = [_pack([not_params[k] if k in not_params else given[pre + k] for k in names]) for pre in ("", "m_", "v_")]
    res_small = adamw("adam_small", s_small, pw, pm, pv, pw.shape[0])
    parts = [_unpack(r, shapes) for r in res_small]
    for i, k in enumerate(names):
        if k not in not_params:
            put(k, [p[i] for p in parts], given[k].shape)
    out["loss"] = parts[0][names.index("loss")][0, 0]
    g_cw = lax.dynamic_slice(parts[0][names.index("conv_w")], (0, me * LANE), (CONV_K, LANE))
    put("conv_w", adamw("adam_conv_w", g_cw, conv_w[0], m_conv_w[0], v_conv_w[0], CONV_K), conv_w.shape)

    order = ["w_ada", "b_ada", "ffn1_w1", "ffn1_w3", "ffn1_w2", "ln1_g", "ln1_b", "w_in", "conv_w", "conv_b", "dt_bias",
             "a_log", "d_ssd", "ssd_norm_w", "s5_a_re", "s5_a_im", "s5_log_dt", "s5_b_re", "s5_b_im", "s5_c_re",
             "s5_c_im", "s5_d", "w_glu", "b_glu", "w_out", "ln2_g", "ln2_b", "ffn2_w1", "ffn2_w3", "ffn2_w2", "ln3_g",
             "ln3_b"]
    return (out["loss"], out["grad_x"], *[out[p + n] for p in ("grad_", "delta_", "new_m_", "new_v_") for n in order])
```
